```python
import jax
import jax.numpy as jnp
from jax import lax
import numpy as np

D_MODEL = 1024
BATCH = 8
SEQ = 4096
DEPTH = 4

CTX_LEN = 256
GRID_W = 64
N_MIXERS = 3
N_ADA = 6
RMS_EPS = 1e-6

CONV_WIDTH = 3

ML_HEADS = 8
ML_DQK = 64
ML_DV = 128
ML_QK = ML_HEADS * ML_DQK
ML_V = ML_HEADS * ML_DV
ML_CHUNK = 64
GATE_CAP = 15.0
ML_IN = 2 * ML_QK + 2 * ML_V + 4 * ML_HEADS

MLA_HEADS = 8
MLA_NOPE = 128
MLA_ROPE = 64
MLA_V = 128
MLA_Q_LORA = 384
MLA_KV_LORA = 256
MLA_SCALE = (MLA_NOPE + MLA_ROPE) ** -0.5
ROPE_THETA = 10000.0
Q_BLOCK = 128

N_EXPERTS = 32
TOP_K = 4
MOE_FF = D_MODEL
SWIGLU_ALPHA = 1.702
SWIGLU_LIMIT = 7.0
MOE_BLOCK = 256

kernel_name = 'hybrid_conv_mlstm_mla_moe_dit'


def rms_norm(x, gain):
    xf = x.astype(jnp.float32)
    y = xf * lax.rsqrt(jnp.mean(xf * xf, axis=-1, keepdims=True) + RMS_EPS)
    return y.astype(x.dtype) * gain


def ada_params(cond, w, b):
    return jnp.split(jax.nn.silu(cond) @ w + b, N_ADA, axis=-1)


def axial_rope_tables(rows, dim):
    n_freq = dim // 4
    inv = ROPE_THETA ** (-jnp.arange(n_freq, dtype=jnp.float32) / n_freq)
    row = jnp.repeat(jnp.arange(rows, dtype=jnp.float32), GRID_W)
    col = jnp.tile(jnp.arange(GRID_W, dtype=jnp.float32), rows)
    a_r = row[:, None] * inv
    a_c = col[:, None] * inv
    ang = jnp.concatenate([a_r, a_r, a_c, a_c], axis=-1)
    return jnp.cos(ang), jnp.sin(ang)


def _rot_half(a):
    a1, a2 = jnp.split(a, 2, axis=-1)
    return jnp.concatenate([-a2, a1], axis=-1)


def apply_axial_rope(x, cos, sin):
    xr, xc = jnp.split(x, 2, axis=-1)
    rotated = jnp.concatenate([_rot_half(xr), _rot_half(xc)], axis=-1)
    return x * cos.astype(x.dtype) + rotated * sin.astype(x.dtype)


def short_conv_mixer(h, w_in, w_conv, w_out):
    L = h.shape[1]
    b_gate, c_gate, u = jnp.split(h @ w_in, 3, axis=-1)
    pad = (CONV_WIDTH - 1) // 2
    vp = jnp.pad(c_gate * u, ((0, 0), (pad, CONV_WIDTH - 1 - pad), (0, 0)))
    y = sum(vp[:, j:j + L] * w_conv[j] for j in range(CONV_WIDTH))
    return (b_gate * y) @ w_out


def mlstm_scan(q, k, v, log_i, log_f, state):
    Bn, Hn, L = q.shape[:3]
    nc = L // ML_CHUNK

    def chunks(a):
        a = a.astype(jnp.float32)
        a = a.reshape(a.shape[:2] + (nc, ML_CHUNK) + a.shape[3:])
        return jnp.moveaxis(a, 2, 0)

    tri = jnp.tril(jnp.ones((ML_CHUNK, ML_CHUNK), dtype=bool))

    def step(carry, inp):
        C, n, m = carry
        qc, kc, vc, ic, fc = inp
        b = jnp.cumsum(fc, axis=-1)
        d = jnp.where(tri, b[..., :, None] - b[..., None, :] + ic[..., None, :], -jnp.inf)
        inter = b + m[..., None]
        m_t = jnp.maximum(inter, jnp.max(d, axis=-1))
        s = jnp.einsum('bhtd,bhsd->bhts', qc, kc) * jnp.exp(d - m_t[..., None])
        a = jnp.exp(inter - m_t)
        num = jnp.einsum('bhts,bhsv->bhtv', s, vc) + a[..., None] * jnp.einsum('bhtd,bhdv->bhtv', qc, C)
        den = jnp.sum(s, axis=-1) + a * jnp.einsum('bhtd,bhd->bht', qc, n)
        h = num / jnp.maximum(jnp.abs(den), jnp.exp(-m_t))[..., None]
        b_last = b[..., -1]
        g = b_last[..., None] - b + ic
        m_new = jnp.maximum(b_last + m, jnp.max(g, axis=-1))
        decay = jnp.exp(b_last + m - m_new)
        wk = jnp.exp(g - m_new[..., None])
        C_new = decay[..., None, None] * C + jnp.einsum('bhs,bhsd,bhsv->bhdv', wk, kc, vc)
        n_new = decay[..., None] * n + jnp.einsum('bhs,bhsd->bhd', wk, kc)
        return (C_new, n_new, m_new), h

    state, h = lax.scan(step, state, tuple(chunks(a) for a in (q, k, v, log_i, log_f)))
    h = jnp.moveaxis(h, 0, 2).reshape(Bn, Hn, L, ML_DV)
    return h, state


def mlstm_mixer(hl, hc, w_in, b_gate, norm_g, w_out, with_ctx):
    def project(h):
        Bn, L, _ = h.shape
        q, k, v, o, g = jnp.split(h @ w_in, [ML_QK, 2 * ML_QK, 2 * ML_QK + ML_V, 2 * ML_QK + 2 * ML_V], axis=-1)
        heads = lambda a: a.reshape(Bn, L, ML_HEADS, -1).transpose(0, 2, 1, 3)
        g = GATE_CAP * jnp.tanh((g + b_gate).astype(jnp.float32) / GATE_CAP)
        g = g.reshape(Bn, L, 4, ML_HEADS).transpose(2, 0, 3, 1)
        fwd = (g[0], jax.nn.log_sigmoid(g[1]))
        bwd = (g[2], jax.nn.log_sigmoid(g[3]))
        return (heads(q) * ML_DQK ** -0.5, heads(k), heads(v)), o, fwd, bwd

    def zero_state(Bn):
        return (jnp.zeros((Bn, ML_HEADS, ML_DQK, ML_DV), jnp.float32),
                jnp.zeros((Bn, ML_HEADS, ML_DQK), jnp.float32),
                jnp.zeros((Bn, ML_HEADS), jnp.float32))

    flip = lambda a: jnp.flip(a, axis=2)

    def bidir(qkv, fwd, bwd, st_f, st_b):
        h_f, st_f = mlstm_scan(*qkv, *fwd, st_f)
        h_b, st_b = mlstm_scan(*[flip(a) for a in qkv], *[flip(a) for a in bwd], st_b)
        return h_f + flip(h_b), st_f, st_b

    def finish(h, o):
        Bn, _, L, _ = h.shape
        h = rms_norm(h, norm_g.reshape(ML_HEADS, 1, ML_DV))
        h = h.transpose(0, 2, 1, 3).reshape(Bn, L, ML_V).astype(o.dtype)
        return (jax.nn.sigmoid(o) * h) @ w_out

    Bn = hl.shape[0]
    qkv_c, o_c, fwd_c, bwd_c = project(hc)
    h_c, st_f, st_b = bidir(qkv_c, fwd_c, bwd_c, zero_state(Bn), zero_state(Bn))
    qkv_l, o_l, fwd_l, bwd_l = project(hl)
    h_l, _, _ = bidir(qkv_l, fwd_l, bwd_l, st_f, st_b)
    y_c = finish(h_c, o_c) if with_ctx else None
    return finish(h_l, o_l), y_c


def mla_attend(qn, qr, kn, kr, v):
    s = jnp.einsum('bhqd,bhkd->bhqk', qn, kn) + jnp.einsum('bhqr,bkr->bhqk', qr, kr)
    p = jax.nn.softmax(s.astype(jnp.float32) * MLA_SCALE, axis=-1)
    return jnp.einsum('bhqk,bhkv->bhqv', p.astype(v.dtype), v)


def mla_mixer(hl, hc, w_in, q_norm, kv_norm, w_uq, w_ukv, qn_nope, qn_rope, kn_nope, kn_rope,
              w_out, rope, with_ctx):
    def down(h):
        return jnp.split(h @ w_in, [MLA_Q_LORA, MLA_Q_LORA + MLA_KV_LORA], axis=-1)

    def heads(a):
        Bn, L, _ = a.shape
        return a.reshape(Bn, L, MLA_HEADS, -1).transpose(0, 2, 1, 3)

    def queries(cq, rope):
        q = heads(rms_norm(cq, q_norm) @ w_uq)
        qn = rms_norm(q[..., :MLA_NOPE], qn_nope)
        qr = rms_norm(q[..., MLA_NOPE:], qn_rope)
        if rope is not None:
            qr = apply_axial_rope(qr, *rope)
        return qn, qr

    def keys_values(ckv, kr, rope):
        kv = heads(rms_norm(ckv, kv_norm) @ w_ukv)
        kn = rms_norm(kv[..., :MLA_NOPE], kn_nope)
        kr = rms_norm(kr, kn_rope)
        if rope is not None:
            kr = apply_axial_rope(kr, *rope)
        return kn, kr, kv[..., MLA_NOPE:]

    def merge(o):
        Bn, _, L, _ = o.shape
        return o.transpose(0, 2, 1, 3).reshape(Bn, L, MLA_HEADS * MLA_V) @ w_out

    cq_l, ckv_l, kr_l = down(hl)
    cq_c, ckv_c, kr_c = down(hc)
    qn_l, qr_l = queries(cq_l, rope)
    kn_l, kr_l, v_l = keys_values(ckv_l, kr_l, rope)
    kn_c, kr_c, v_c = keys_values(ckv_c, kr_c, None)
    kn = jnp.concatenate([kn_l, kn_c], axis=2)
    kr = jnp.concatenate([kr_l, kr_c], axis=1)
    v = jnp.concatenate([v_l, v_c], axis=2)
    Bn, Hn, L, _ = qn_l.shape
    nb = L // Q_BLOCK
    blocks = lambda a: jnp.moveaxis(a.reshape(Bn, Hn, nb, Q_BLOCK, a.shape[-1]), 2, 0)
    o = lax.map(lambda qb: mla_attend(qb[0], qb[1], kn, kr, v), (blocks(qn_l), blocks(qr_l)))
    o_l = jnp.moveaxis(o, 0, 2).reshape(Bn, Hn, L, MLA_V)
    y_c = None
    if with_ctx:
        qn_c, qr_c = queries(cq_c, None)
        y_c = merge(mla_attend(qn_c, qr_c, kn_c, kr_c, v_c))
    return merge(o_l), y_c


def moe_ffn(h, w_r, b_r, w1, b1, w2, b2):
    T, D = h.shape
    logits = (h @ w_r + b_r).astype(jnp.float32)
    top_v, top_e = lax.top_k(logits, TOP_K)
    gates = jax.nn.softmax(top_v, axis=-1)
    flat_e = top_e.reshape(-1)
    order = jnp.argsort(flat_e)
    sorted_e = flat_e[order]
    sorted_tok = order // TOP_K
    sorted_gate = gates.reshape(-1)[order].astype(h.dtype)
    counts = jnp.bincount(flat_e, length=N_EXPERTS)
    group_start = jnp.cumsum(counts) - counts
    padded = (counts + MOE_BLOCK - 1) // MOE_BLOCK * MOE_BLOCK
    padded_end = jnp.cumsum(padded)
    padded_start = padded_end - padded
    dest = padded_start[sorted_e] + (jnp.arange(T * TOP_K) - group_start[sorted_e])
    n_rows = -(-(T * TOP_K) // MOE_BLOCK) * MOE_BLOCK + N_EXPERTS * MOE_BLOCK
    n_blk = n_rows // MOE_BLOCK
    xp = jnp.zeros((n_rows, D), h.dtype).at[dest].set(h[sorted_tok])
    blk_start = jnp.arange(n_blk) * MOE_BLOCK
    blk_e = jnp.minimum(jnp.sum(padded_end[None, :] <= blk_start[:, None], axis=1), N_EXPERTS - 1)

    def expert_block(args):
        xb, e = args
        glu, lin = jnp.split(xb @ w1[e] + b1[e], 2, axis=-1)
        glu = jnp.minimum(glu, SWIGLU_LIMIT)
        lin = jnp.clip(lin, -SWIGLU_LIMIT, SWIGLU_LIMIT)
        act = glu * jax.nn.sigmoid(SWIGLU_ALPHA * glu) * (lin + 1.0)
        return act @ w2[e] + b2[e]

    yp = lax.map(expert_block, (xp.reshape(n_blk, MOE_BLOCK, D), blk_e)).reshape(n_rows, D)
    y = yp[dest] * sorted_gate[:, None]
    return jnp.zeros_like(h).at[sorted_tok].add(y)


def setup_inputs(seed: int = 0) -> dict:
    key = jax.random.key(seed)
    ks = iter(jax.random.split(key, 48))
    nrm = lambda shape, scale: jax.random.normal(next(ks), shape, jnp.float32) * scale
    gain = lambda shape: 1.0 + nrm(shape, 0.05)
    n_conv = len(range(0, DEPTH, N_MIXERS))
    n_ml = len(range(1, DEPTH, N_MIXERS))
    n_mla = len(range(2, DEPTH, N_MIXERS))
    D = D_MODEL
    gate_offset = jnp.array([0.0, 3.0, 0.0, 3.0], jnp.float32)[None, :, None]
    return {
        'x': nrm((BATCH, SEQ, D), 1.0),
        'c': nrm((BATCH, D), 1.0),
        'ctx': nrm((BATCH, CTX_LEN, D), 1.0),
        'c_ctx': nrm((D,), 1.0),
        'norm_mix': gain((DEPTH, D)),
        'norm_ffn': gain((DEPTH, D)),
        'w_mod': nrm((DEPTH, D, N_ADA * D), 0.5 * D ** -0.5),
        'b_mod': nrm((DEPTH, N_ADA * D), 0.02),
        'conv_w_in': nrm((n_conv, D, 3 * D), D ** -0.5),
        'conv_w': nrm((n_conv, CONV_WIDTH, D), CONV_WIDTH ** -0.5),
        'conv_w_out': nrm((n_conv, D, D), D ** -0.5),
        'ml_w_in': nrm((n_ml, D, ML_IN), D ** -0.5),
        'ml_b_gate': (nrm((n_ml, 4, ML_HEADS), 0.3) + gate_offset).reshape(n_ml, 4 * ML_HEADS),
        'ml_norm': gain((n_ml, ML_V)),
        'ml_w_out': nrm((n_ml, ML_V, D), ML_V ** -0.5),
        'mla_w_in': nrm((n_mla, D, MLA_Q_LORA + MLA_KV_LORA + MLA_ROPE), D ** -0.5),
        'mla_q_norm': gain((n_mla, MLA_Q_LORA)),
        'mla_kv_norm': gain((n_mla, MLA_KV_LORA)),
        'mla_w_uq': nrm((n_mla, MLA_Q_LORA, MLA_HEADS * (MLA_NOPE + MLA_ROPE)), MLA_Q_LORA ** -0.5),
        'mla_w_ukv': nrm((n_mla, MLA_KV_LORA, MLA_HEADS * (MLA_NOPE + MLA_V)), MLA_KV_LORA ** -0.5),
        'mla_qn_nope': gain((n_mla, MLA_NOPE)),
        'mla_qn_rope': gain((n_mla, MLA_ROPE)),
        'mla_kn_nope': gain((n_mla, MLA_NOPE)),
        'mla_kn_rope': gain((n_mla, MLA_ROPE)),
        'mla_w_out': nrm((n_mla, MLA_HEADS * MLA_V, D), (MLA_HEADS * MLA_V) ** -0.5),
        'moe_w_router': nrm((DEPTH, D, N_EXPERTS), D ** -0.5),
        'moe_b_router': nrm((DEPTH, N_EXPERTS), 0.01),
        'moe_w1': nrm((DEPTH, N_EXPERTS, D, 2 * MOE_FF), D ** -0.5),
        'moe_b1': nrm((DEPTH, N_EXPERTS, 2 * MOE_FF), 0.01),
        'moe_w2': nrm((DEPTH, N_EXPERTS, MOE_FF, D), MOE_FF ** -0.5),
        'moe_b2': nrm((DEPTH, N_EXPERTS, D), 0.01),
    }


def reference(x, c, ctx, c_ctx, norm_mix, norm_ffn, w_mod, b_mod,
              conv_w_in, conv_w, conv_w_out,
              ml_w_in, ml_b_gate, ml_norm, ml_w_out,
              mla_w_in, mla_q_norm, mla_kv_norm, mla_w_uq, mla_w_ukv,
              mla_qn_nope, mla_qn_rope, mla_kn_nope, mla_kn_rope, mla_w_out,
              moe_w_router, moe_b_router, moe_w1, moe_b1, moe_w2, moe_b2):
    Bn, n_lat, D = x.shape
    rows = n_lat // GRID_W
    rope = axial_rope_tables(rows, MLA_ROPE)
    xc = ctx
    for layer in range(DEPTH):
        kind, j = layer % N_MIXERS, layer // N_MIXERS
        with_ctx = layer != DEPTH - 1
        sh_a, sc_a, g_a, sh_f, sc_f, g_f = [p[:, None, :] for p in ada_params(c, w_mod[layer], b_mod[layer])]
        csh_a, csc_a, cg_a, csh_f, csc_f, cg_f = ada_params(c_ctx, w_mod[layer], b_mod[layer])
        hl = rms_norm(x, norm_mix[layer]) * (1.0 + sc_a) + sh_a
        if with_ctx or kind != 0:
            hc = rms_norm(xc, norm_mix[layer]) * (1.0 + csc_a) + csh_a
        if kind == 0:
            yl = short_conv_mixer(hl, conv_w_in[j], conv_w[j], conv_w_out[j])
            yc = short_conv_mixer(hc, conv_w_in[j], conv_w[j], conv_w_out[j]) if with_ctx else None
        elif kind == 1:
            yl, yc = mlstm_mixer(hl, hc, ml_w_in[j], ml_b_gate[j], ml_norm[j], ml_w_out[j], with_ctx)
        else:
            yl, yc = mla_mixer(hl, hc, mla_w_in[j], mla_q_norm[j], mla_kv_norm[j], mla_w_uq[j],
                               mla_w_ukv[j], mla_qn_nope[j], mla_qn_rope[j], mla_kn_nope[j],
                               mla_kn_rope[j], mla_w_out[j], rope, with_ctx)
        x = x + g_a * yl
        hl = rms_norm(x, norm_ffn[layer]) * (1.0 + sc_f) + sh_f
        moe_w = (moe_w_router[layer], moe_b_router[layer], moe_w1[layer], moe_b1[layer],
                 moe_w2[layer], moe_b2[layer])
        if with_ctx:
            xc = xc + cg_a * yc
            hc = rms_norm(xc, norm_ffn[layer]) * (1.0 + csc_f) + csh_f
            n_ctx = Bn * xc.shape[1]
            y = moe_ffn(jnp.concatenate([hc.reshape(-1, D), hl.reshape(-1, D)], axis=0), *moe_w)
            xc = xc + cg_f * y[:n_ctx].reshape(xc.shape)
            x = x + g_f * y[n_ctx:].reshape(x.shape)
        else:
            x = x + g_f * moe_ffn(hl.reshape(-1, D), *moe_w).reshape(x.shape)
    return x
```

```python
import functools

import jax
import jax.numpy as jnp
from jax import lax
from jax.experimental import pallas as pl
from jax.experimental.pallas import tpu as pltpu

D_MODEL = 1024
DEPTH = 4
GRID_W = 64
N_MIXERS = 3
N_ADA = 6
RMS_EPS = 1e-6
CONV_WIDTH = 3
ML_HEADS = 8
ML_DQK = 64
ML_DV = 128
ML_QK = ML_HEADS * ML_DQK
ML_V = ML_HEADS * ML_DV
ML_CHUNK = 64
GATE_CAP = 15.0
MLA_HEADS = 8
MLA_NOPE = 128
MLA_ROPE = 64
MLA_V = 128
MLA_Q_LORA = 384
MLA_KV_LORA = 256
MLA_SCALE = (MLA_NOPE + MLA_ROPE) ** -0.5
ROPE_THETA = 10000.0
Q_BLOCK = 128
N_EXPERTS = 32
TOP_K = 4
MOE_FF = D_MODEL
SWIGLU_ALPHA = 1.702
SWIGLU_LIMIT = 7.0
MOE_BLOCK = 256

VMEM_LIMIT = 48 * 1024 * 1024


def rms_norm(x, gain):
    xf = x.astype(jnp.float32)
    y = xf * lax.rsqrt(jnp.mean(xf * xf, axis=-1, keepdims=True) + RMS_EPS)
    return y.astype(x.dtype) * gain


def ada_params(cond, w, b):
    return jnp.split(jax.nn.silu(cond) @ w + b, N_ADA, axis=-1)


def axial_rope_tables(rows, dim):
    n_freq = dim // 4
    inv = ROPE_THETA ** (-jnp.arange(n_freq, dtype=jnp.float32) / n_freq)
    row = jnp.repeat(jnp.arange(rows, dtype=jnp.float32), GRID_W)
    col = jnp.tile(jnp.arange(GRID_W, dtype=jnp.float32), rows)
    a_r = row[:, None] * inv
    a_c = col[:, None] * inv
    ang = jnp.concatenate([a_r, a_r, a_c, a_c], axis=-1)
    return jnp.cos(ang), jnp.sin(ang)


def _rot_half(a):
    a1, a2 = jnp.split(a, 2, axis=-1)
    return jnp.concatenate([-a2, a1], axis=-1)


def apply_axial_rope(x, cos, sin):
    xr, xc = jnp.split(x, 2, axis=-1)
    rotated = jnp.concatenate([_rot_half(xr), _rot_half(xc)], axis=-1)
    return x * cos.astype(x.dtype) + rotated * sin.astype(x.dtype)


def short_conv_mixer(h, w_in, w_conv, w_out):
    L = h.shape[1]
    b_gate, c_gate, u = jnp.split(h @ w_in, 3, axis=-1)
    pad = (CONV_WIDTH - 1) // 2
    vp = jnp.pad(c_gate * u, ((0, 0), (pad, CONV_WIDTH - 1 - pad), (0, 0)))
    y = sum(vp[:, j:j + L] * w_conv[j] for j in range(CONV_WIDTH))
    return (b_gate * y) @ w_out


def mlstm_scan(q, k, v, log_i, log_f, state):
    Bn, Hn, L = q.shape[:3]
    nc = L // ML_CHUNK

    def chunks(a):
        a = a.astype(jnp.float32)
        a = a.reshape(a.shape[:2] + (nc, ML_CHUNK) + a.shape[3:])
        return jnp.moveaxis(a, 2, 0)

    tri = jnp.tril(jnp.ones((ML_CHUNK, ML_CHUNK), dtype=bool))

    def step(carry, inp):
        C, n, m = carry
        qc, kc, vc, ic, fc = inp
        b = jnp.cumsum(fc, axis=-1)
        d = jnp.where(tri, b[..., :, None] - b[..., None, :] + ic[..., None, :], -jnp.inf)
        inter = b + m[..., None]
        m_t = jnp.maximum(inter, jnp.max(d, axis=-1))
        s = jnp.einsum('bhtd,bhsd->bhts', qc, kc) * jnp.exp(d - m_t[..., None])
        a = jnp.exp(inter - m_t)
        num = jnp.einsum('bhts,bhsv->bhtv', s, vc) + a[..., None] * jnp.einsum('bhtd,bhdv->bhtv', qc, C)
        den = jnp.sum(s, axis=-1) + a * jnp.einsum('bhtd,bhd->bht', qc, n)
        h = num / jnp.maximum(jnp.abs(den), jnp.exp(-m_t))[..., None]
        b_last = b[..., -1]
        g = b_last[..., None] - b + ic
        m_new = jnp.maximum(b_last + m, jnp.max(g, axis=-1))
        decay = jnp.exp(b_last + m - m_new)
        wk = jnp.exp(g - m_new[..., None])
        C_new = decay[..., None, None] * C + jnp.einsum('bhs,bhsd,bhsv->bhdv', wk, kc, vc)
        n_new = decay[..., None] * n + jnp.einsum('bhs,bhsd->bhd', wk, kc)
        return (C_new, n_new, m_new), h

    state, h = lax.scan(step, state, tuple(chunks(a) for a in (q, k, v, log_i, log_f)))
    h = jnp.moveaxis(h, 0, 2).reshape(Bn, Hn, L, ML_DV)
    return h, state


def mlstm_mixer(hl, hc, w_in, b_gate, norm_g, w_out, with_ctx):
    def project(h):
        Bn, L, _ = h.shape
        q, k, v, o, g = jnp.split(h @ w_in, [ML_QK, 2 * ML_QK, 2 * ML_QK + ML_V, 2 * ML_QK + 2 * ML_V], axis=-1)
        heads = lambda a: a.reshape(Bn, L, ML_HEADS, -1).transpose(0, 2, 1, 3)
        g = GATE_CAP * jnp.tanh((g + b_gate).astype(jnp.float32) / GATE_CAP)
        g = g.reshape(Bn, L, 4, ML_HEADS).transpose(2, 0, 3, 1)
        fwd = (g[0], jax.nn.log_sigmoid(g[1]))
        bwd = (g[2], jax.nn.log_sigmoid(g[3]))
        return (heads(q) * ML_DQK ** -0.5, heads(k), heads(v)), o, fwd, bwd

    def zero_state(Bn):
        return (jnp.zeros((Bn, ML_HEADS, ML_DQK, ML_DV), jnp.float32),
                jnp.zeros((Bn, ML_HEADS, ML_DQK), jnp.float32),
                jnp.zeros((Bn, ML_HEADS), jnp.float32))

    flip = lambda a: jnp.flip(a, axis=2)

    def bidir(qkv, fwd, bwd, st_f, st_b):
        h_f, st_f = mlstm_scan(*qkv, *fwd, st_f)
        h_b, st_b = mlstm_scan(*[flip(a) for a in qkv], *[flip(a) for a in bwd], st_b)
        return h_f + flip(h_b), st_f, st_b

    def finish(h, o):
        Bn, _, L, _ = h.shape
        h = rms_norm(h, norm_g.reshape(ML_HEADS, 1, ML_DV))
        h = h.transpose(0, 2, 1, 3).reshape(Bn, L, ML_V).astype(o.dtype)
        return (jax.nn.sigmoid(o) * h) @ w_out

    Bn = hl.shape[0]
    qkv_c, o_c, fwd_c, bwd_c = project(hc)
    h_c, st_f, st_b = bidir(qkv_c, fwd_c, bwd_c, zero_state(Bn), zero_state(Bn))
    qkv_l, o_l, fwd_l, bwd_l = project(hl)
    h_l, _, _ = bidir(qkv_l, fwd_l, bwd_l, st_f, st_b)
    y_c = finish(h_c, o_c) if with_ctx else None
    return finish(h_l, o_l), y_c


def mla_attend(qn, qr, kn, kr, v):
    s = jnp.einsum('bhqd,bhkd->bhqk', qn, kn) + jnp.einsum('bhqr,bkr->bhqk', qr, kr)
    p = jax.nn.softmax(s.astype(jnp.float32) * MLA_SCALE, axis=-1)
    return jnp.einsum('bhqk,bhkv->bhqv', p.astype(v.dtype), v)


def mla_mixer(hl, hc, w_in, q_norm, kv_norm, w_uq, w_ukv, qn_nope, qn_rope, kn_nope, kn_rope,
              w_out, rope, with_ctx):
    def down(h):
        return jnp.split(h @ w_in, [MLA_Q_LORA, MLA_Q_LORA + MLA_KV_LORA], axis=-1)

    def heads(a):
        Bn, L, _ = a.shape
        return a.reshape(Bn, L, MLA_HEADS, -1).transpose(0, 2, 1, 3)

    def queries(cq, rope):
        q = heads(rms_norm(cq, q_norm) @ w_uq)
        qn = rms_norm(q[..., :MLA_NOPE], qn_nope)
        qr = rms_norm(q[..., MLA_NOPE:], qn_rope)
        if rope is not None:
            qr = apply_axial_rope(qr, *rope)
        return qn, qr

    def keys_values(ckv, kr, rope):
        kv = heads(rms_norm(ckv, kv_norm) @ w_ukv)
        kn = rms_norm(kv[..., :MLA_NOPE], kn_nope)
        kr = rms_norm(kr, kn_rope)
        if rope is not None:
            kr = apply_axial_rope(kr, *rope)
        return kn, kr, kv[..., MLA_NOPE:]

    def merge(o):
        Bn, _, L, _ = o.shape
        return o.transpose(0, 2, 1, 3).reshape(Bn, L, MLA_HEADS * MLA_V) @ w_out

    cq_l, ckv_l, kr_l = down(hl)
    cq_c, ckv_c, kr_c = down(hc)
    qn_l, qr_l = queries(cq_l, rope)
    kn_l, kr_l, v_l = keys_values(ckv_l, kr_l, rope)
    kn_c, kr_c, v_c = keys_values(ckv_c, kr_c, None)
    kn = jnp.concatenate([kn_l, kn_c], axis=2)
    kr = jnp.concatenate([kr_l, kr_c], axis=1)
    v = jnp.concatenate([v_l, v_c], axis=2)
    Bn, Hn, L, _ = qn_l.shape
    nb = L // Q_BLOCK
    blocks = lambda a: jnp.moveaxis(a.reshape(Bn, Hn, nb, Q_BLOCK, a.shape[-1]), 2, 0)
    o = lax.map(lambda qb: mla_attend(qb[0], qb[1], kn, kr, v), (blocks(qn_l), blocks(qr_l)))
    o_l = jnp.moveaxis(o, 0, 2).reshape(Bn, Hn, L, MLA_V)
    y_c = None
    if with_ctx:
        qn_c, qr_c = queries(cq_c, None)
        y_c = merge(mla_attend(qn_c, qr_c, kn_c, kr_c, v_c))
    return merge(o_l), y_c


def _expert_ffn_kernel(blk_e_ref, x_ref, w1_ref, b1_ref, w2_ref, b2_ref, o_ref):
    del blk_e_ref
    x = x_ref[...]
    h = jnp.dot(x, w1_ref[0], preferred_element_type=jnp.float32) + b1_ref[0]
    glu = jnp.minimum(h[:, :MOE_FF], SWIGLU_LIMIT)
    lin = jnp.clip(h[:, MOE_FF:], -SWIGLU_LIMIT, SWIGLU_LIMIT)
    act = glu * jax.nn.sigmoid(SWIGLU_ALPHA * glu) * (lin + 1.0)
    y = jnp.dot(act.astype(jnp.bfloat16), w2_ref[0], preferred_element_type=jnp.float32)
    o_ref[...] = y + b2_ref[0]


def expert_ffn(xp, blk_e, w1, b1, w2, b2):
    n_rows, D = xp.shape
    n_blk = n_rows // MOE_BLOCK
    grid_spec = pltpu.PrefetchScalarGridSpec(
        num_scalar_prefetch=1,
        grid=(n_blk,),
        in_specs=[
            pl.BlockSpec((MOE_BLOCK, D), lambda i, be: (i, 0)),
            pl.BlockSpec((1, D, 2 * MOE_FF), lambda i, be: (be[i], 0, 0)),
            pl.BlockSpec((1, 1, 2 * MOE_FF), lambda i, be: (be[i], 0, 0)),
            pl.BlockSpec((1, MOE_FF, D), lambda i, be: (be[i], 0, 0)),
            pl.BlockSpec((1, 1, D), lambda i, be: (be[i], 0, 0)),
        ],
        out_specs=pl.BlockSpec((MOE_BLOCK, D), lambda i, be: (i, 0)),
    )
    return pl.pallas_call(
        _expert_ffn_kernel,
        grid_spec=grid_spec,
        out_shape=jax.ShapeDtypeStruct((n_rows, D), jnp.float32),
        compiler_params=pltpu.CompilerParams(
            dimension_semantics=("arbitrary",), vmem_limit_bytes=VMEM_LIMIT),
        name="expert_ffn",
    )(blk_e, xp, w1, b1, w2, b2)


def moe_ffn(h, w_r, b_r, w1, b1, w2, b2):
    T, D = h.shape
    logits = (h @ w_r + b_r).astype(jnp.float32)
    top_v, top_e = lax.top_k(logits, TOP_K)
    gates = jax.nn.softmax(top_v, axis=-1)
    flat_e = top_e.reshape(-1)
    order = jnp.argsort(flat_e)
    sorted_e = flat_e[order]
    sorted_tok = order // TOP_K
    sorted_gate = gates.reshape(-1)[order].astype(h.dtype)
    counts = jnp.bincount(flat_e, length=N_EXPERTS)
    group_start = jnp.cumsum(counts) - counts
    padded = (counts + MOE_BLOCK - 1) // MOE_BLOCK * MOE_BLOCK
    padded_end = jnp.cumsum(padded)
    padded_start = padded_end - padded
    dest = padded_start[sorted_e] + (jnp.arange(T * TOP_K) - group_start[sorted_e])
    n_rows = -(-(T * TOP_K) // MOE_BLOCK) * MOE_BLOCK + N_EXPERTS * MOE_BLOCK
    n_blk = n_rows // MOE_BLOCK
    hb = h.astype(jnp.bfloat16)
    xp = jnp.zeros((n_rows, D), jnp.bfloat16).at[dest].set(hb[sorted_tok])
    blk_start = jnp.arange(n_blk) * MOE_BLOCK
    blk_e = jnp.minimum(jnp.sum(padded_end[None, :] <= blk_start[:, None], axis=1), N_EXPERTS - 1)
    yp = expert_ffn(xp, blk_e.astype(jnp.int32), w1.astype(jnp.bfloat16),
                    b1.reshape(N_EXPERTS, 1, 2 * MOE_FF), w2.astype(jnp.bfloat16),
                    b2.reshape(N_EXPERTS, 1, D))
    y = yp[dest] * sorted_gate[:, None]
    return jnp.zeros_like(h).at[sorted_tok].add(y)


def kernel(x, c, ctx, c_ctx, norm_mix, norm_ffn, w_mod, b_mod, conv_w_in, conv_w, conv_w_out, ml_w_in, ml_b_gate, ml_norm, ml_w_out, mla_w_in, mla_q_norm, mla_kv_norm, mla_w_uq, mla_w_ukv, mla_qn_nope, mla_qn_rope, mla_kn_nope, mla_kn_rope, mla_w_out, moe_w_router, moe_b_router, moe_w1, moe_b1, moe_w2, moe_b2):
    Bn, n_lat, D = x.shape
    rows = n_lat // GRID_W
    rope = axial_rope_tables(rows, MLA_ROPE)
    xc = ctx
    for layer in range(DEPTH):
        kind, j = layer % N_MIXERS, layer // N_MIXERS
        with_ctx = layer != DEPTH - 1
        sh_a, sc_a, g_a, sh_f, sc_f, g_f = [p[:, None, :] for p in ada_params(c, w_mod[layer], b_mod[layer])]
        csh_a, csc_a, cg_a, csh_f, csc_f, cg_f = ada_params(c_ctx, w_mod[layer], b_mod[layer])
        hl = rms_norm(x, norm_mix[layer]) * (1.0 + sc_a) + sh_a
        if with_ctx or kind != 0:
            hc = rms_norm(xc, norm_mix[layer]) * (1.0 + csc_a) + csh_a
        if kind == 0:
            yl = short_conv_mixer(hl, conv_w_in[j], conv_w[j], conv_w_out[j])
            yc = short_conv_mixer(hc, conv_w_in[j], conv_w[j], conv_w_out[j]) if with_ctx else None
        elif kind == 1:
            yl, yc = mlstm_mixer(hl, hc, ml_w_in[j], ml_b_gate[j], ml_norm[j], ml_w_out[j], with_ctx)
        else:
            yl, yc = mla_mixer(hl, hc, mla_w_in[j], mla_q_norm[j], mla_kv_norm[j], mla_w_uq[j],
                               mla_w_ukv[j], mla_qn_nope[j], mla_qn_rope[j], mla_kn_nope[j],
                               mla_kn_rope[j], mla_w_out[j], rope, with_ctx)
        x = x + g_a * yl
        hl = rms_norm(x, norm_ffn[layer]) * (1.0 + sc_f) + sh_f
        moe_w = (moe_w_router[layer], moe_b_router[layer], moe_w1[layer], moe_b1[layer],
                 moe_w2[layer], moe_b2[layer])
        if with_ctx:
            xc = xc + cg_a * yc
            hc = rms_norm(xc, norm_ffn[layer]) * (1.0 + csc_f) + csh_f
            n_ctx = Bn * xc.shape[1]
            y = moe_ffn(jnp.concatenate([hc.reshape(-1, D), hl.reshape(-1, D)], axis=0), *moe_w)
            xc = xc + cg_f * y[:n_ctx].reshape(xc.shape)
            x = x + g_f * y[n_ctx:].reshape(x.shape)
        else:
            x = x + g_f * moe_ffn(hl.reshape(-1, D), *moe_w).reshape(x.shape)
    return x
```

```python
import functools

import jax
import jax.numpy as jnp
from jax import lax
from jax.experimental import pallas as pl
from jax.experimental.pallas import tpu as pltpu

D_MODEL = 1024
DEPTH = 4
GRID_W = 64
N_MIXERS = 3
N_ADA = 6
RMS_EPS = 1e-6
CONV_WIDTH = 3
ML_HEADS = 8
ML_DQK = 64
ML_DV = 128
ML_QK = ML_HEADS * ML_DQK
ML_V = ML_HEADS * ML_DV
GATE_CAP = 15.0
MLA_HEADS = 8
MLA_NOPE = 128
MLA_ROPE = 64
MLA_V = 128
MLA_Q_LORA = 384
MLA_KV_LORA = 256
MLA_SCALE = (MLA_NOPE + MLA_ROPE) ** -0.5
ROPE_THETA = 10000.0
N_EXPERTS = 32
TOP_K = 4
MOE_FF = D_MODEL
SWIGLU_ALPHA = 1.702
SWIGLU_LIMIT = 7.0
MOE_BLOCK = 256

VMEM_LIMIT = 48 * 1024 * 1024


def rms_norm(x, gain):
    xf = x.astype(jnp.float32)
    y = xf * lax.rsqrt(jnp.mean(xf * xf, axis=-1, keepdims=True) + RMS_EPS)
    return y.astype(x.dtype) * gain


def ada_params(cond, w, b):
    return jnp.split(jax.nn.silu(cond) @ w + b, N_ADA, axis=-1)


def axial_rope_tables(rows, dim):
    n_freq = dim // 4
    inv = ROPE_THETA ** (-jnp.arange(n_freq, dtype=jnp.float32) / n_freq)
    row = jnp.repeat(jnp.arange(rows, dtype=jnp.float32), GRID_W)
    col = jnp.tile(jnp.arange(GRID_W, dtype=jnp.float32), rows)
    a_r = row[:, None] * inv
    a_c = col[:, None] * inv
    ang = jnp.concatenate([a_r, a_r, a_c, a_c], axis=-1)
    return jnp.cos(ang), jnp.sin(ang)


def _rot_half(a):
    a1, a2 = jnp.split(a, 2, axis=-1)
    return jnp.concatenate([-a2, a1], axis=-1)


def apply_axial_rope(x, cos, sin):
    xr, xc = jnp.split(x, 2, axis=-1)
    rotated = jnp.concatenate([_rot_half(xr), _rot_half(xc)], axis=-1)
    return x * cos.astype(x.dtype) + rotated * sin.astype(x.dtype)


def short_conv_mixer(h, w_in, w_conv, w_out):
    L = h.shape[1]
    b_gate, c_gate, u = jnp.split(h @ w_in, 3, axis=-1)
    pad = (CONV_WIDTH - 1) // 2
    vp = jnp.pad(c_gate * u, ((0, 0), (pad, CONV_WIDTH - 1 - pad), (0, 0)))
    y = sum(vp[:, j:j + L] * w_conv[j] for j in range(CONV_WIDTH))
    return (b_gate * y) @ w_out


ML_T = 256
ML_SW = 2 * ML_DV
ML_NG = 4 * ML_HEADS


def _mlstm_dir(reverse, q_ref, k_ref, v_ref, g_ref, gt_ref, o_ref, s_ref, m_ref):
    T = ML_T
    f32 = jnp.float32
    hi = lax.Precision.HIGHEST
    row = lax.broadcasted_iota(jnp.int32, (T, T), 0)
    col = lax.broadcasted_iota(jnp.int32, (T, T), 1)
    mask = (col >= row) if reverse else (col <= row)
    g = g_ref[0]
    gt = gt_ref[0]
    tri_c = mask.astype(f32)
    tri_r = ((row >= col) if reverse else (row <= col)).astype(f32)
    bc = jnp.dot(tri_c, g, precision=hi, preferred_element_type=f32)
    br = jnp.dot(gt, tri_r, precision=hi, preferred_element_type=f32)
    gi, gf = (2 * ML_HEADS, 3 * ML_HEADS) if reverse else (0, ML_HEADS)
    lane = lax.broadcasted_iota(jnp.int32, (T, 2 * ML_DQK), 1)
    ones_col = (lax.broadcasted_iota(jnp.int32, (T, ML_DV), 1) == 0).astype(jnp.bfloat16)
    for h in range(ML_HEADS):
        pair = (h // 2) * 2 * ML_DQK
        own = (lane >= ML_DQK) if (h % 2) else (lane < ML_DQK)
        qp = q_ref[0, :, pair:pair + 2 * ML_DQK]
        kp = k_ref[0, :, pair:pair + 2 * ML_DQK]
        qm = jnp.where(own, qp, jnp.zeros_like(qp))
        km = jnp.where(own, kp, jnp.zeros_like(kp))
        v_ext = jnp.concatenate([v_ref[0, :, h * ML_DV:(h + 1) * ML_DV], ones_col], axis=1)
        b_col = bc[:, gf + h:gf + h + 1]
        i_col = g[:, gi + h:gi + h + 1]
        b_row = br[gf + h:gf + h + 1, :]
        i_row = gt[gi + h:gi + h + 1, :]
        tot = b_row[:, 0:1] if reverse else b_row[:, T - 1:T]
        m_prev = m_ref[h][0:1, 0:1]
        s_prev = s_ref[h]
        d = jnp.where(mask, b_col - b_row + i_row, -jnp.inf)
        inter = b_col + m_prev
        m_t = jnp.maximum(inter, jnp.max(d, axis=1, keepdims=True))
        s_raw = lax.dot_general(qm, km, (((1,), (1,)), ((), ())), preferred_element_type=f32)
        p = (s_raw * jnp.exp(d - m_t)).astype(jnp.bfloat16)
        a = jnp.exp(inter - m_t)
        r = (jnp.dot(p, v_ext, preferred_element_type=f32)
             + a * jnp.dot(qm, s_prev.astype(jnp.bfloat16), preferred_element_type=f32))
        den = r[:, ML_DV:ML_DV + 1]
        o_ref[0, :, h * ML_DV:(h + 1) * ML_DV] = (
            r[:, :ML_DV] / jnp.maximum(jnp.abs(den), jnp.exp(-m_t))).astype(o_ref.dtype)
        g_col = tot - b_col + i_col
        m_new = jnp.maximum(tot + m_prev, jnp.max(g_col, axis=0, keepdims=True))
        decay = jnp.exp(tot + m_prev - m_new)
        wv = (jnp.exp(g_col - m_new) * v_ext.astype(f32)).astype(jnp.bfloat16)
        s_ref[h] = decay * s_prev + lax.dot_general(
            km, wv, (((0,), (0,)), ((), ())), preferred_element_type=f32)
        m_ref[h] = jnp.broadcast_to(m_new, m_ref.shape[1:])


def _mlstm_scan_kernel(qf_ref, kf_ref, vf_ref, gf_ref, gtf_ref, qb_ref, kb_ref, vb_ref, gb_ref, gtb_ref,
                       of_ref, ob_ref, sf_ref, mf_ref, sb_ref, mb_ref):
    @pl.when(pl.program_id(1) == 0)
    def _():
        sf_ref[...] = jnp.zeros_like(sf_ref)
        mf_ref[...] = jnp.zeros_like(mf_ref)
        sb_ref[...] = jnp.zeros_like(sb_ref)
        mb_ref[...] = jnp.zeros_like(mb_ref)

    _mlstm_dir(False, qf_ref, kf_ref, vf_ref, gf_ref, gtf_ref, of_ref, sf_ref, mf_ref)
    _mlstm_dir(True, qb_ref, kb_ref, vb_ref, gb_ref, gtb_ref, ob_ref, sb_ref, mb_ref)


def mlstm_scan_pallas(q, k, v, g, gt):
    Bn, L, _ = q.shape
    nc = L // ML_T
    fwd = lambda b, j: (b, j, 0)
    bwd = lambda b, j: (b, jnp.where(j == 0, 0, nc - j), 0)
    fwd_t = lambda b, j: (b, 0, j)
    bwd_t = lambda b, j: (b, 0, jnp.where(j == 0, 0, nc - j))

    def specs(im, imt):
        return [pl.BlockSpec((1, ML_T, ML_QK), im), pl.BlockSpec((1, ML_T, ML_QK), im),
                pl.BlockSpec((1, ML_T, ML_V), im), pl.BlockSpec((1, ML_T, ML_NG), im),
                pl.BlockSpec((1, ML_NG, ML_T), imt)]

    out_sds = jax.ShapeDtypeStruct((Bn, L, ML_V), jnp.float32)
    state = [pltpu.VMEM((ML_HEADS, 2 * ML_DQK, ML_SW), jnp.float32),
             pltpu.VMEM((ML_HEADS, 8, 128), jnp.float32)]
    return pl.pallas_call(
        _mlstm_scan_kernel,
        grid=(Bn, nc),
        in_specs=specs(fwd, fwd_t) + specs(bwd, bwd_t),
        out_specs=[pl.BlockSpec((1, ML_T, ML_V), fwd), pl.BlockSpec((1, ML_T, ML_V), bwd)],
        out_shape=[out_sds, out_sds],
        scratch_shapes=state + state,
        compiler_params=pltpu.CompilerParams(
            dimension_semantics=("arbitrary", "arbitrary"), vmem_limit_bytes=VMEM_LIMIT),
        name="mlstm_scan",
    )(q, k, v, g, gt, q, k, v, g, gt)


def mlstm_mixer(hl, hc, w_in, b_gate, norm_g, w_out, with_ctx):
    Bn, n_lat, _ = hl.shape
    n_ctx = hc.shape[1]
    assert n_ctx == ML_T and n_lat % ML_T == 0
    h = jnp.concatenate([hc, hl], axis=1)
    L = h.shape[1]
    q, k, v, o, g = jnp.split(h @ w_in, [ML_QK, 2 * ML_QK, 2 * ML_QK + ML_V, 2 * ML_QK + 2 * ML_V], axis=-1)
    g = GATE_CAP * jnp.tanh((g + b_gate).astype(jnp.float32) / GATE_CAP)
    is_f = (jnp.arange(ML_NG) // ML_HEADS) % 2 == 1
    g = jnp.where(is_f, jax.nn.log_sigmoid(g), g)
    bf = jnp.bfloat16
    h_f, h_b = mlstm_scan_pallas((q * ML_DQK ** -0.5).astype(bf), k.astype(bf), v.astype(bf),
                                 g, g.transpose(0, 2, 1))
    hh = (h_f + h_b).reshape(Bn, L, ML_HEADS, ML_DV)
    hh = rms_norm(hh, norm_g.reshape(ML_HEADS, ML_DV)).reshape(Bn, L, ML_V)
    y = (jax.nn.sigmoid(o) * hh) @ w_out
    return y[:, n_ctx:], (y[:, :n_ctx] if with_ctx else None)


MLA_QK_PAD = 256
MLA_TQ = 256


def _mla_attn_kernel(*refs, n_seg):
    q_ref, kv_refs, o_ref = refs[0], refs[1:1 + 2 * n_seg], refs[1 + 2 * n_seg]
    q = q_ref[0]
    scores = [lax.dot_general(q, kv_refs[2 * i][0], (((1,), (1,)), ((), ())),
                              preferred_element_type=jnp.float32) for i in range(n_seg)]
    m = scores[0].max(axis=1, keepdims=True)
    for s in scores[1:]:
        m = jnp.maximum(m, s.max(axis=1, keepdims=True))
    acc = None
    denom = None
    for i, s in enumerate(scores):
        p = jnp.exp(s - m)
        l = p.sum(axis=1, keepdims=True)
        pv = jnp.dot(p.astype(jnp.bfloat16), kv_refs[2 * i + 1][0], preferred_element_type=jnp.float32)
        acc = pv if acc is None else acc + pv
        denom = l if denom is None else denom + l
    o_ref[0] = (acc / denom).astype(o_ref.dtype)


def mla_attention(q, segments):
    Bn, Lq, _ = q.shape
    tq = min(MLA_TQ, Lq)
    in_specs = [pl.BlockSpec((1, tq, MLA_QK_PAD), lambda b, h, i: (b, i, h))]
    args = [q]
    for k, v in segments:
        Lk = k.shape[1]
        in_specs.append(pl.BlockSpec((1, Lk, MLA_QK_PAD), lambda b, h, i: (b, 0, h)))
        in_specs.append(pl.BlockSpec((1, Lk, MLA_V), lambda b, h, i: (b, 0, h)))
        args += [k, v]
    return pl.pallas_call(
        functools.partial(_mla_attn_kernel, n_seg=len(segments)),
        grid=(Bn, MLA_HEADS, Lq // tq),
        in_specs=in_specs,
        out_specs=pl.BlockSpec((1, tq, MLA_V), lambda b, h, i: (b, i, h)),
        out_shape=jax.ShapeDtypeStruct((Bn, Lq, MLA_HEADS * MLA_V), jnp.bfloat16),
        compiler_params=pltpu.CompilerParams(
            dimension_semantics=("arbitrary", "arbitrary", "arbitrary"), vmem_limit_bytes=VMEM_LIMIT),
        name="mla_attention",
    )(*args)


def mla_mixer(hl, hc, w_in, q_norm, kv_norm, w_uq, w_ukv, qn_nope, qn_rope, kn_nope, kn_rope,
              w_out, rope, with_ctx):
    bf = jnp.bfloat16

    def down(h):
        return jnp.split(h @ w_in, [MLA_Q_LORA, MLA_Q_LORA + MLA_KV_LORA], axis=-1)

    def heads(a):
        Bn, L, _ = a.shape
        return a.reshape(Bn, L, MLA_HEADS, -1)

    def pad_heads(nope, r):
        Bn, L = nope.shape[:2]
        z = jnp.zeros(r.shape[:-1] + (MLA_QK_PAD - MLA_NOPE - MLA_ROPE,), r.dtype)
        return jnp.concatenate([nope, r, z], axis=-1).astype(bf).reshape(Bn, L, MLA_HEADS * MLA_QK_PAD)

    def queries(cq, rope):
        q = heads(rms_norm(cq, q_norm) @ w_uq)
        qn = rms_norm(q[..., :MLA_NOPE], qn_nope)
        qr = rms_norm(q[..., MLA_NOPE:], qn_rope)
        if rope is not None:
            qr = apply_axial_rope(qr.transpose(0, 2, 1, 3), *rope).transpose(0, 2, 1, 3)
        return pad_heads(qn * MLA_SCALE, qr * MLA_SCALE)

    def keys_values(ckv, kr, rope):
        kv = heads(rms_norm(ckv, kv_norm) @ w_ukv)
        kn = rms_norm(kv[..., :MLA_NOPE], kn_nope)
        kr = rms_norm(kr, kn_rope)
        if rope is not None:
            kr = apply_axial_rope(kr, *rope)
        kr = jnp.broadcast_to(kr[:, :, None, :], kn.shape[:-1] + (MLA_ROPE,))
        Bn, L = kn.shape[:2]
        return pad_heads(kn, kr), kv[..., MLA_NOPE:].astype(bf).reshape(Bn, L, MLA_HEADS * MLA_V)

    cq_l, ckv_l, kr_l = down(hl)
    cq_c, ckv_c, kr_c = down(hc)
    q_l = queries(cq_l, rope)
    k_l, v_l = keys_values(ckv_l, kr_l, rope)
    k_c, v_c = keys_values(ckv_c, kr_c, None)
    o_l = mla_attention(q_l, [(k_l, v_l), (k_c, v_c)])
    y_c = None
    if with_ctx:
        y_c = mla_attention(queries(cq_c, None), [(k_c, v_c)]).astype(jnp.float32) @ w_out
    return o_l.astype(jnp.float32) @ w_out, y_c


def _expert_ffn_kernel(blk_e_ref, x_ref, w1_ref, b1_ref, w2_ref, b2_ref, o_ref):
    del blk_e_ref
    x = x_ref[...]
    h = jnp.dot(x, w1_ref[0], preferred_element_type=jnp.float32) + b1_ref[0]
    glu = jnp.minimum(h[:, :MOE_FF], SWIGLU_LIMIT)
    lin = jnp.clip(h[:, MOE_FF:], -SWIGLU_LIMIT, SWIGLU_LIMIT)
    act = glu * jax.nn.sigmoid(SWIGLU_ALPHA * glu) * (lin + 1.0)
    y = jnp.dot(act.astype(jnp.bfloat16), w2_ref[0], preferred_element_type=jnp.float32)
    o_ref[...] = (y + b2_ref[0]).astype(o_ref.dtype)


def expert_ffn(xp, blk_e, w1, b1, w2, b2):
    n_rows, D = xp.shape
    n_blk = n_rows // MOE_BLOCK
    grid_spec = pltpu.PrefetchScalarGridSpec(
        num_scalar_prefetch=1,
        grid=(n_blk,),
        in_specs=[
            pl.BlockSpec((MOE_BLOCK, D), lambda i, be: (i, 0)),
            pl.BlockSpec((1, D, 2 * MOE_FF), lambda i, be: (be[i], 0, 0)),
            pl.BlockSpec((1, 1, 2 * MOE_FF), lambda i, be: (be[i], 0, 0)),
            pl.BlockSpec((1, MOE_FF, D), lambda i, be: (be[i], 0, 0)),
            pl.BlockSpec((1, 1, D), lambda i, be: (be[i], 0, 0)),
        ],
        out_specs=pl.BlockSpec((MOE_BLOCK, D), lambda i, be: (i, 0)),
    )
    return pl.pallas_call(
        _expert_ffn_kernel,
        grid_spec=grid_spec,
        out_shape=jax.ShapeDtypeStruct((n_rows, D), jnp.bfloat16),
        compiler_params=pltpu.CompilerParams(
            dimension_semantics=("arbitrary",), vmem_limit_bytes=VMEM_LIMIT),
        name="expert_ffn",
    )(blk_e, xp, w1, b1, w2, b2)


def moe_ffn(h, w_r, b_r, w1, b1, w2, b2):
    T, D = h.shape
    logits = (h @ w_r + b_r).astype(jnp.float32)
    top_v, top_e = lax.top_k(logits, TOP_K)
    gates = jax.nn.softmax(top_v, axis=-1)
    flat_e = top_e.reshape(-1)
    order = jnp.argsort(flat_e)
    sel = jnp.sum(top_e[:, :, None] == jnp.arange(N_EXPERTS), axis=1).astype(jnp.int32)
    csum = jnp.cumsum(sel, axis=0)
    counts = csum[-1]
    pos = jnp.take_along_axis(csum - sel, top_e, axis=1)
    group_start = jnp.cumsum(counts) - counts
    padded = (counts + MOE_BLOCK - 1) // MOE_BLOCK * MOE_BLOCK
    padded_end = jnp.cumsum(padded)
    padded_start = padded_end - padded
    dest = padded_start[top_e] + pos
    n_rows = -(-(T * TOP_K) // MOE_BLOCK) * MOE_BLOCK + N_EXPERTS * MOE_BLOCK
    n_blk = n_rows // MOE_BLOCK
    blk_start = jnp.arange(n_blk) * MOE_BLOCK
    blk_e = jnp.minimum(jnp.sum(padded_end[None, :] <= blk_start[:, None], axis=1), N_EXPERTS - 1)
    row_e = jnp.repeat(blk_e, MOE_BLOCK)
    local = jnp.arange(n_rows) - padded_start[row_e]
    src_flat = order[jnp.clip(group_start[row_e] + local, 0, T * TOP_K - 1)]
    xp = h.astype(jnp.bfloat16)[src_flat // TOP_K]
    yp = expert_ffn(xp, blk_e.astype(jnp.int32), w1.astype(jnp.bfloat16),
                    b1.reshape(N_EXPERTS, 1, 2 * MOE_FF), w2.astype(jnp.bfloat16),
                    b2.reshape(N_EXPERTS, 1, D))
    yg = yp[dest.reshape(-1)].reshape(T, TOP_K, D).astype(jnp.float32)
    return jnp.sum(yg * gates[:, :, None], axis=1)


def kernel(x, c, ctx, c_ctx, norm_mix, norm_ffn, w_mod, b_mod, conv_w_in, conv_w, conv_w_out, ml_w_in, ml_b_gate, ml_norm, ml_w_out, mla_w_in, mla_q_norm, mla_kv_norm, mla_w_uq, mla_w_ukv, mla_qn_nope, mla_qn_rope, mla_kn_nope, mla_kn_rope, mla_w_out, moe_w_router, moe_b_router, moe_w1, moe_b1, moe_w2, moe_b2):
    Bn, n_lat, D = x.shape
    rows = n_lat // GRID_W
    rope = axial_rope_tables(rows, MLA_ROPE)
    xc = ctx
    for layer in range(DEPTH):
        kind, j = layer % N_MIXERS, layer // N_MIXERS
        with_ctx = layer != DEPTH - 1
        sh_a, sc_a, g_a, sh_f, sc_f, g_f = [p[:, None, :] for p in ada_params(c, w_mod[layer], b_mod[layer])]
        csh_a, csc_a, cg_a, csh_f, csc_f, cg_f = ada_params(c_ctx, w_mod[layer], b_mod[layer])
        hl = rms_norm(x, norm_mix[layer]) * (1.0 + sc_a) + sh_a
        if with_ctx or kind != 0:
            hc = rms_norm(xc, norm_mix[layer]) * (1.0 + csc_a) + csh_a
        if kind == 0:
            yl = short_conv_mixer(hl, conv_w_in[j], conv_w[j], conv_w_out[j])
            yc = short_conv_mixer(hc, conv_w_in[j], conv_w[j], conv_w_out[j]) if with_ctx else None
        elif kind == 1:
            yl, yc = mlstm_mixer(hl, hc, ml_w_in[j], ml_b_gate[j], ml_norm[j], ml_w_out[j], with_ctx)
        else:
            yl, yc = mla_mixer(hl, hc, mla_w_in[j], mla_q_norm[j], mla_kv_norm[j], mla_w_uq[j],
                               mla_w_ukv[j], mla_qn_nope[j], mla_qn_rope[j], mla_kn_nope[j],
                               mla_kn_rope[j], mla_w_out[j], rope, with_ctx)
        x = x + g_a * yl
        hl = rms_norm(x, norm_ffn[layer]) * (1.0 + sc_f) + sh_f
        moe_w = (moe_w_router[layer], moe_b_router[layer], moe_w1[layer], moe_b1[layer],
                 moe_w2[layer], moe_b2[layer])
        if with_ctx:
            xc = xc + cg_a * yc
            hc = rms_norm(xc, norm_ffn[layer]) * (1.0 + csc_f) + csh_f
            n_ctx = Bn * xc.shape[1]
            y = moe_ffn(jnp.concatenate([hc.reshape(-1, D), hl.reshape(-1, D)], axis=0), *moe_w)
            xc = xc + cg_f * y[:n_ctx].reshape(xc.shape)
            x = x + g_f * y[n_ctx:].reshape(x.shape)
        else:
            x = x + g_f * moe_ffn(hl.reshape(-1, D), *moe_w).reshape(x.shape)
    return x
```

```python
import functools

import jax
import jax.numpy as jnp
from jax import lax
from jax.experimental import pallas as pl
from jax.experimental.pallas import tpu as pltpu

D_MODEL = 1024
DEPTH = 4
GRID_W = 64
N_MIXERS = 3
N_ADA = 6
RMS_EPS = 1e-6
CONV_WIDTH = 3
ML_HEADS = 8
ML_DQK = 64
ML_DV = 128
ML_QK = ML_HEADS * ML_DQK
ML_V = ML_HEADS * ML_DV
GATE_CAP = 15.0
MLA_HEADS = 8
MLA_NOPE = 128
MLA_ROPE = 64
MLA_V = 128
MLA_Q_LORA = 384
MLA_KV_LORA = 256
MLA_SCALE = (MLA_NOPE + MLA_ROPE) ** -0.5
ROPE_THETA = 10000.0
N_EXPERTS = 32
TOP_K = 4
MOE_FF = D_MODEL
SWIGLU_ALPHA = 1.702
SWIGLU_LIMIT = 7.0
MOE_BLOCK = 256

TL = 256
LANES = 128
BF16_ROWS = 16
VMEM_LIMIT = 48 * 1024 * 1024
HI = lax.Precision.HIGHEST
F32 = jnp.float32
BF16 = jnp.bfloat16


def _params(n_axes):
    return pltpu.CompilerParams(dimension_semantics=("arbitrary",) * n_axes,
                                vmem_limit_bytes=VMEM_LIMIT)


def _rms(x, width=None):
    width = x.shape[-1] if width is None else width
    return x * lax.rsqrt(jnp.sum(x * x, axis=-1, keepdims=True) * (1.0 / width) + RMS_EPS)


def _norm_mod(x, gain, scale, shift):
    return _rms(x) * gain * (1.0 + scale) + shift


class Layout:
    def __init__(self, n_batch, nb, ctx_first):
        self.n_batch, self.nb, self.ctx_first = n_batch, nb, ctx_first
        self.n_blocks = n_batch * nb
        self.rows = self.n_blocks * TL

    def mod_row(self, r):
        b = r // self.nb
        return jnp.where(r % self.nb == 0, self.n_batch, b) if self.ctx_first else b

    def row_spec(self, width):
        return pl.BlockSpec((TL, width), lambda r: (r, 0))

    def mod_spec(self, piece):
        return pl.BlockSpec((1, 1, D_MODEL), lambda r: (self.mod_row(r), 0, piece))


def _const_spec(shape):
    return pl.BlockSpec(shape, lambda *_: (0,) * len(shape))


ADA_ROWS = 16
ADA_TN = 1536


def _ada_kernel(c_ref, w_ref, b_ref, o_ref):
    c = c_ref[...]
    s = c * jax.nn.sigmoid(c)
    o_ref[0] = jnp.dot(s, w_ref[0], precision=HI, preferred_element_type=F32) + b_ref[0]


def ada_all(c, c_ctx, w_mod, b_mod):
    Bn, D = c.shape
    cond = jnp.zeros((ADA_ROWS, D), F32).at[:Bn].set(c).at[Bn].set(c_ctx)
    out = pl.pallas_call(
        _ada_kernel,
        grid=(DEPTH, N_ADA * D // ADA_TN),
        in_specs=[pl.BlockSpec((ADA_ROWS, D), lambda l, n: (0, 0)),
                  pl.BlockSpec((1, D, ADA_TN), lambda l, n: (l, 0, n)),
                  pl.BlockSpec((1, 1, ADA_TN), lambda l, n: (l, 0, n))],
        out_specs=pl.BlockSpec((1, ADA_ROWS, ADA_TN), lambda l, n: (l, 0, n)),
        out_shape=jax.ShapeDtypeStruct((DEPTH, ADA_ROWS, N_ADA * D), F32),
        compiler_params=_params(2),
        name="ada_mod",
    )(cond, w_mod, b_mod.reshape(DEPTH, 1, N_ADA * D))
    return out[:, :Bn + 1, None, :]


def _conv_in_kernel(x_ref, gain_ref, sc_ref, sh_ref, w_ref, bg_ref, v_ref):
    D = D_MODEL
    h = _norm_mod(x_ref[...], gain_ref[...], sc_ref[0], sh_ref[0]).astype(BF16)
    p = jnp.dot(h, w_ref[...], preferred_element_type=F32)
    bg_ref[...] = p[:, :D].astype(BF16)
    v_ref[...] = (p[:, D:2 * D] * p[:, 2 * D:]).astype(BF16)


def conv_in(lay, x, gain, mod, w_in):
    D = D_MODEL
    sds = jax.ShapeDtypeStruct((lay.rows, D), BF16)
    return pl.pallas_call(
        _conv_in_kernel,
        grid=(lay.n_blocks,),
        in_specs=[lay.row_spec(D), _const_spec((1, D)), lay.mod_spec(1), lay.mod_spec(0),
                  _const_spec((D, 3 * D))],
        out_specs=[lay.row_spec(D), lay.row_spec(D)],
        out_shape=[sds, sds],
        compiler_params=_params(1),
        name="conv_in",
    )(x, gain, mod, mod, w_in)


ML_T = TL
ML_SW = 2 * ML_DV
ML_NG = 4 * ML_HEADS


def _gate_act(g, is_forget):
    g = GATE_CAP * jnp.tanh(g * (1.0 / GATE_CAP))
    log_sig = jnp.minimum(g, 0.0) - jnp.log(1.0 + jnp.exp(-jnp.abs(g)))
    return jnp.where(is_forget, log_sig, g)


def _mlstm_in_kernel(x_ref, gain_ref, sc_ref, sh_ref, w_ref, wg_ref, wgt_ref, bg_ref, bgt_ref,
                     q_ref, k_ref, v_ref, og_ref, g_ref, gt_ref):
    h = _norm_mod(x_ref[...], gain_ref[...], sc_ref[0], sh_ref[0])
    p = jnp.dot(h.astype(BF16), w_ref[...], preferred_element_type=F32)
    q_ref[...] = p[:, :ML_QK].astype(BF16)
    k_ref[...] = p[:, ML_QK:2 * ML_QK].astype(BF16)
    v_ref[...] = p[:, 2 * ML_QK:2 * ML_QK + ML_V].astype(BF16)
    og_ref[...] = jax.nn.sigmoid(p[:, 2 * ML_QK + ML_V:]).astype(BF16)
    g = jnp.dot(h, wg_ref[...], precision=HI, preferred_element_type=F32) + bg_ref[...]
    col = lax.broadcasted_iota(jnp.int32, g.shape, 1)
    g_ref[...] = _gate_act(g, (col // ML_HEADS) % 2 == 1)
    gt = lax.dot_general(wgt_ref[...], h, (((1,), (1,)), ((), ())), precision=HI,
                         preferred_element_type=F32) + bgt_ref[...]
    row = lax.broadcasted_iota(jnp.int32, gt.shape, 0)
    gt_ref[...] = _gate_act(gt, (row // ML_HEADS) % 2 == 1)


def mlstm_in(lay, x, gain, mod, w_main, w_g, b_g):
    D = D_MODEL
    n_main = 2 * ML_QK + 2 * ML_V
    bf = lambda w: jax.ShapeDtypeStruct((lay.rows, w), BF16)
    return pl.pallas_call(
        _mlstm_in_kernel,
        grid=(lay.n_blocks,),
        in_specs=[lay.row_spec(D), _const_spec((1, D)), lay.mod_spec(1), lay.mod_spec(0),
                  _const_spec((D, n_main)), _const_spec((D, ML_NG)), _const_spec((ML_NG, D)),
                  _const_spec((1, ML_NG)), _const_spec((ML_NG, 1))],
        out_specs=[lay.row_spec(ML_QK), lay.row_spec(ML_QK), lay.row_spec(ML_V), lay.row_spec(ML_V),
                   lay.row_spec(ML_NG), pl.BlockSpec((ML_NG, TL), lambda r: (0, r))],
        out_shape=[bf(ML_QK), bf(ML_QK), bf(ML_V), bf(ML_V),
                   jax.ShapeDtypeStruct((lay.rows, ML_NG), F32),
                   jax.ShapeDtypeStruct((ML_NG, lay.rows), F32)],
        compiler_params=_params(1),
        name="mlstm_in",
    )(x, gain, mod, mod, w_main, w_g, w_g.T, b_g.reshape(1, ML_NG), b_g.reshape(ML_NG, 1))


def _mlstm_dir(reverse, q_ref, k_ref, v_ref, g_ref, gt_ref, o_ref, s_ref, m_ref):
    T = ML_T
    row = lax.broadcasted_iota(jnp.int32, (T, T), 0)
    col = lax.broadcasted_iota(jnp.int32, (T, T), 1)
    mask = (col >= row) if reverse else (col <= row)
    g = g_ref[...]
    gt = gt_ref[...]
    tri_c = mask.astype(F32)
    tri_r = ((row >= col) if reverse else (row <= col)).astype(F32)
    bc = jnp.dot(tri_c, g, precision=HI, preferred_element_type=F32)
    br = jnp.dot(gt, tri_r, precision=HI, preferred_element_type=F32)
    gi, gf = (2 * ML_HEADS, 3 * ML_HEADS) if reverse else (0, ML_HEADS)
    lane = lax.broadcasted_iota(jnp.int32, (T, 2 * ML_DQK), 1)
    ones_col = (lax.broadcasted_iota(jnp.int32, (T, ML_DV), 1) == 0).astype(BF16)
    for h in range(ML_HEADS):
        pair = (h // 2) * 2 * ML_DQK
        own = (lane >= ML_DQK) if (h % 2) else (lane < ML_DQK)
        qp = q_ref[:, pair:pair + 2 * ML_DQK]
        kp = k_ref[:, pair:pair + 2 * ML_DQK]
        qm = jnp.where(own, qp, jnp.zeros_like(qp))
        km = jnp.where(own, kp, jnp.zeros_like(kp))
        v_ext = jnp.concatenate([v_ref[:, h * ML_DV:(h + 1) * ML_DV], ones_col], axis=1)
        b_col = bc[:, gf + h:gf + h + 1]
        i_col = g[:, gi + h:gi + h + 1]
        b_row = br[gf + h:gf + h + 1, :]
        i_row = gt[gi + h:gi + h + 1, :]
        tot = b_row[:, 0:1] if reverse else b_row[:, T - 1:T]
        m_prev = m_ref[h][0:1, 0:1]
        s_prev = s_ref[h]
        d = jnp.where(mask, b_col - b_row + i_row, -jnp.inf)
        inter = b_col + m_prev
        m_t = jnp.maximum(inter, jnp.max(d, axis=1, keepdims=True))
        s_raw = lax.dot_general(qm, km, (((1,), (1,)), ((), ())), preferred_element_type=F32)
        p = (s_raw * jnp.exp(d - m_t)).astype(BF16)
        a = jnp.exp(inter - m_t)
        r = (jnp.dot(p, v_ext, preferred_element_type=F32)
             + a * jnp.dot(qm, s_prev.astype(BF16), preferred_element_type=F32))
        den = r[:, ML_DV:ML_DV + 1]
        o_ref[:, h * ML_DV:(h + 1) * ML_DV] = (
            r[:, :ML_DV] / jnp.maximum(jnp.abs(den), jnp.exp(-m_t))).astype(o_ref.dtype)
        g_col = tot - b_col + i_col
        m_new = jnp.maximum(tot + m_prev, jnp.max(g_col, axis=0, keepdims=True))
        decay = jnp.exp(tot + m_prev - m_new)
        wv = (jnp.exp(g_col - m_new) * v_ext.astype(F32)).astype(BF16)
        s_ref[h] = decay * s_prev + lax.dot_general(
            km, wv, (((0,), (0,)), ((), ())), preferred_element_type=F32)
        m_ref[h] = jnp.broadcast_to(m_new, m_ref.shape[1:])


def _mlstm_scan_kernel(qf_ref, kf_ref, vf_ref, gf_ref, gtf_ref, qb_ref, kb_ref, vb_ref, gb_ref, gtb_ref,
                       of_ref, ob_ref, sf_ref, mf_ref, sb_ref, mb_ref):
    @pl.when(pl.program_id(1) == 0)
    def _():
        sf_ref[...] = jnp.zeros_like(sf_ref)
        mf_ref[...] = jnp.zeros_like(mf_ref)
        sb_ref[...] = jnp.zeros_like(sb_ref)
        mb_ref[...] = jnp.zeros_like(mb_ref)

    _mlstm_dir(False, qf_ref, kf_ref, vf_ref, gf_ref, gtf_ref, of_ref, sf_ref, mf_ref)
    _mlstm_dir(True, qb_ref, kb_ref, vb_ref, gb_ref, gtb_ref, ob_ref, sb_ref, mb_ref)


def mlstm_scan(lay, q, k, v, g, gt):
    assert lay.ctx_first
    nb = lay.nb
    fwd = lambda b, j: (b * nb + j, 0)
    bwd = lambda b, j: (b * nb + jnp.where(j == 0, 0, nb - j), 0)
    fwd_t = lambda b, j: (0, b * nb + j)
    bwd_t = lambda b, j: (0, b * nb + jnp.where(j == 0, 0, nb - j))

    def specs(im, imt):
        return [pl.BlockSpec((ML_T, ML_QK), im), pl.BlockSpec((ML_T, ML_QK), im),
                pl.BlockSpec((ML_T, ML_V), im), pl.BlockSpec((ML_T, ML_NG), im),
                pl.BlockSpec((ML_NG, ML_T), imt)]

    out_sds = jax.ShapeDtypeStruct((lay.rows, ML_V), F32)
    state = [pltpu.VMEM((ML_HEADS, 2 * ML_DQK, ML_SW), F32),
             pltpu.VMEM((ML_HEADS, 8, LANES), F32)]
    return pl.pallas_call(
        _mlstm_scan_kernel,
        grid=(lay.n_batch, nb),
        in_specs=specs(fwd, fwd_t) + specs(bwd, bwd_t),
        out_specs=[pl.BlockSpec((ML_T, ML_V), fwd), pl.BlockSpec((ML_T, ML_V), bwd)],
        out_shape=[out_sds, out_sds],
        scratch_shapes=state + state,
        compiler_params=_params(2),
        name="mlstm_scan",
    )(q, k, v, g, gt, q, k, v, g, gt)


MLA_QK_PAD = 256
MLA_IN_PAD = MLA_Q_LORA + MLA_KV_LORA + LANES
ROPE_HALF = MLA_ROPE // 4


def _mla_in_kernel(x_ref, gain_ref, sc_ref, sh_ref, win_ref, wuq_ref, wukv_ref, qn_ref, kvn_ref,
                   qnn_ref, qnr_ref, knn_ref, knr_ref, cos_ref, sa_ref, sb_ref,
                   q_out, k_out, v_out):
    h = _norm_mod(x_ref[...], gain_ref[...], sc_ref[0], sh_ref[0]).astype(BF16)
    p = jnp.dot(h, win_ref[...], preferred_element_type=F32)
    cq = _rms(p[:, :MLA_Q_LORA]) * qn_ref[...]
    ckv = _rms(p[:, MLA_Q_LORA:MLA_Q_LORA + MLA_KV_LORA]) * kvn_ref[...]
    kr = p[:, MLA_Q_LORA + MLA_KV_LORA:]
    q = jnp.dot(cq.astype(BF16), wuq_ref[...], preferred_element_type=F32)
    kv = jnp.dot(ckv.astype(BF16), wukv_ref[...], preferred_element_type=F32)
    nv = MLA_HEADS * MLA_NOPE
    v_out[...] = kv[:, nv:].astype(BF16)
    cos, sa, sb = cos_ref[...], sa_ref[...], sb_ref[...]

    def rope(xp):
        return (xp * cos + pltpu.roll(xp, LANES - ROPE_HALF, 1) * sa + pltpu.roll(xp, ROPE_HALF, 1) * sb)

    kr = rope(_rms(kr, MLA_ROPE) * knr_ref[...]).astype(BF16)
    for hd in range(MLA_HEADS):
        c0 = hd * MLA_QK_PAD
        qn = _rms(q[:, c0:c0 + MLA_NOPE]) * qnn_ref[...]
        qr = rope(_rms(q[:, c0 + MLA_NOPE:c0 + MLA_QK_PAD], MLA_ROPE) * qnr_ref[...])
        q_out[:, c0:c0 + MLA_NOPE] = (qn * MLA_SCALE).astype(BF16)
        q_out[:, c0 + MLA_NOPE:c0 + MLA_QK_PAD] = (qr * MLA_SCALE).astype(BF16)
        kn = _rms(kv[:, hd * MLA_NOPE:(hd + 1) * MLA_NOPE]) * knn_ref[...]
        k_out[:, c0:c0 + MLA_NOPE] = kn.astype(BF16)
        k_out[:, c0 + MLA_NOPE:c0 + MLA_QK_PAD] = kr


def _pad_lanes(g):
    return jnp.pad(g, (0, LANES - g.shape[0])).reshape(1, LANES)


def mla_in(lay, x, gain, mod, w_in, q_norm, kv_norm, w_uq, w_ukv, qn_nope, qn_rope, kn_nope, kn_rope,
           tables):
    D = D_MODEL
    Hn = MLA_HEADS
    win = jnp.pad(w_in, ((0, 0), (0, MLA_IN_PAD - w_in.shape[1]))).astype(BF16)
    wuq = jnp.pad(w_uq.reshape(MLA_Q_LORA, Hn, MLA_NOPE + MLA_ROPE),
                  ((0, 0), (0, 0), (0, MLA_QK_PAD - MLA_NOPE - MLA_ROPE)))
    wuq = wuq.reshape(MLA_Q_LORA, Hn * MLA_QK_PAD).astype(BF16)
    wkv = w_ukv.reshape(MLA_KV_LORA, Hn, MLA_NOPE + MLA_V)
    wukv = jnp.concatenate([wkv[:, :, :MLA_NOPE].reshape(MLA_KV_LORA, Hn * MLA_NOPE),
                            wkv[:, :, MLA_NOPE:].reshape(MLA_KV_LORA, Hn * MLA_V)], axis=1).astype(BF16)
    nb = lay.nb
    tab_spec = pl.BlockSpec((TL, LANES), lambda r: (r % nb, 0))
    bf = lambda w: jax.ShapeDtypeStruct((lay.rows, w), BF16)
    return pl.pallas_call(
        _mla_in_kernel,
        grid=(lay.n_blocks,),
        in_specs=[lay.row_spec(D), _const_spec((1, D)), lay.mod_spec(1), lay.mod_spec(0),
                  _const_spec(win.shape), _const_spec(wuq.shape), _const_spec(wukv.shape),
                  _const_spec((1, MLA_Q_LORA)), _const_spec((1, MLA_KV_LORA)),
                  _const_spec((1, LANES)), _const_spec((1, LANES)), _const_spec((1, LANES)),
                  _const_spec((1, LANES)), tab_spec, tab_spec, tab_spec],
        out_specs=[lay.row_spec(Hn * MLA_QK_PAD), lay.row_spec(Hn * MLA_QK_PAD), lay.row_spec(Hn * MLA_V)],
        out_shape=[bf(Hn * MLA_QK_PAD), bf(Hn * MLA_QK_PAD), bf(Hn * MLA_V)],
        compiler_params=_params(1),
        name="mla_in",
    )(x, gain, mod, mod, win, wuq, wukv, q_norm.reshape(1, -1), kv_norm.reshape(1, -1),
      qn_nope.reshape(1, -1), _pad_lanes(qn_rope), kn_nope.reshape(1, -1), _pad_lanes(kn_rope), *tables)


def rope_tables(n_ctx, n_lat):
    n_freq = MLA_ROPE // 4
    inv = ROPE_THETA ** (-jnp.arange(n_freq, dtype=F32) / n_freq)
    t = jnp.arange(n_lat)
    a_r = (t // GRID_W).astype(F32)[:, None] * inv
    a_c = (t % GRID_W).astype(F32)[:, None] * inv
    ang = jnp.concatenate([a_r, a_r, a_c, a_c], axis=-1)
    ang = jnp.concatenate([jnp.zeros((n_ctx, MLA_ROPE), F32), ang], axis=0)
    cos, sin = jnp.cos(ang), jnp.sin(ang)
    low = (jnp.arange(MLA_ROPE) % (2 * ROPE_HALF)) < ROPE_HALF
    pad = lambda a: jnp.pad(a, ((0, 0), (0, LANES - MLA_ROPE)))
    return pad(cos), pad(jnp.where(low, -sin, 0.0)), pad(jnp.where(low, 0.0, sin))


def _mla_attn_kernel(q_ref, k_ref, v_ref, o_ref, *, n_ctx):
    q = q_ref[0]

    def attend(k, v):
        s = lax.dot_general(q, k, (((1,), (1,)), ((), ())), preferred_element_type=F32)
        p = jnp.exp(s - s.max(axis=1, keepdims=True))
        l = p.sum(axis=1, keepdims=True)
        o_ref[0] = (jnp.dot(p.astype(BF16), v, preferred_element_type=F32) / l).astype(o_ref.dtype)

    @pl.when(pl.program_id(2) == 0)
    def _():
        attend(k_ref[0, :n_ctx], v_ref[0, :n_ctx])

    @pl.when(pl.program_id(2) > 0)
    def _():
        attend(k_ref[0], v_ref[0])


def mla_attention(lay, q, k, v):
    assert lay.ctx_first
    Bn, S = lay.n_batch, lay.nb * TL
    q3, k3, v3 = (a.reshape(Bn, S, a.shape[-1]) for a in (q, k, v))
    out = pl.pallas_call(
        functools.partial(_mla_attn_kernel, n_ctx=TL),
        grid=(Bn, MLA_HEADS, lay.nb),
        in_specs=[pl.BlockSpec((1, TL, MLA_QK_PAD), lambda b, h, i: (b, i, h)),
                  pl.BlockSpec((1, S, MLA_QK_PAD), lambda b, h, i: (b, 0, h)),
                  pl.BlockSpec((1, S, MLA_V), lambda b, h, i: (b, 0, h))],
        out_specs=pl.BlockSpec((1, TL, MLA_V), lambda b, h, i: (b, i, h)),
        out_shape=jax.ShapeDtypeStruct((Bn, S, MLA_HEADS * MLA_V), BF16),
        compiler_params=_params(3),
        name="mla_attention",
    )(q3, k3, v3)
    return out.reshape(lay.rows, MLA_HEADS * MLA_V)


N_PROLOGUE = {"conv": 5, "mlstm": 4, "mla": 1}


def _mixer_out_kernel(*refs, kind, nb, ctx_first):
    n_pro = N_PROLOGUE[kind]
    pro = refs[:n_pro]
    (wout_ref, x_ref, ga_ref, gain_ref, sc_ref, sh_ref, wr_ref, br_ref,
     xo_ref, h2_ref, te_ref, gate_ref, rank_ref, cnt_ref, carry_ref) = refs[n_pro:]
    r = pl.program_id(0)

    if kind == "conv":
        vprev_ref, v_ref, vnext_ref, bg_ref, cw_ref = pro
        j = r % nb
        first = (j == 0) | (j == 1) if ctx_first else (j == 0)
        last = (j == nb - 1) | (j == 0) if ctx_first else (j == nb - 1)
        v = v_ref[...].astype(F32)
        rows = lax.broadcasted_iota(jnp.int32, (TL, 1), 0)
        prev_row = jnp.where(first, 0.0, vprev_ref[BF16_ROWS - 1:BF16_ROWS, :].astype(F32))
        next_row = jnp.where(last, 0.0, vnext_ref[0:1, :].astype(F32))
        up = jnp.where(rows == 0, prev_row, pltpu.roll(v, 1, 0))
        dn = jnp.where(rows == TL - 1, next_row, pltpu.roll(v, TL - 1, 0))
        cw = cw_ref[...]
        a = bg_ref[...].astype(F32) * (up * cw[0:1] + v * cw[1:2] + dn * cw[2:3])
    elif kind == "mlstm":
        hf_ref, hb_ref, og_ref, ng_ref = pro
        hh = hf_ref[...] + hb_ref[...]
        a = jnp.concatenate([_rms(hh[:, h * ML_DV:(h + 1) * ML_DV]) for h in range(ML_HEADS)], axis=1)
        a = a * ng_ref[...] * og_ref[...].astype(F32)
    else:
        a = pro[0][...]

    y = jnp.dot(a.astype(BF16), wout_ref[...], preferred_element_type=F32)
    xn = x_ref[...] + ga_ref[0] * y
    xo_ref[...] = xn
    h2 = _norm_mod(xn, gain_ref[...], sc_ref[0], sh_ref[0])
    h2_ref[...] = h2.astype(BF16)
    logits = jnp.dot(h2, wr_ref[...], precision=HI, preferred_element_type=F32) + br_ref[...]

    lane = lax.broadcasted_iota(jnp.int32, (TL, N_EXPERTS), 1)
    lane_k = lax.broadcasted_iota(jnp.int32, (TL, TOP_K), 1)
    work = logits
    sel = jnp.zeros((TL, N_EXPERTS), F32)
    top_e = jnp.zeros((TL, TOP_K), jnp.int32)
    top_v = jnp.zeros((TL, TOP_K), F32)
    picks = []
    for kk in range(TOP_K):
        m = work.max(axis=1, keepdims=True)
        idx = jnp.min(jnp.where(work == m, lane, N_EXPERTS), axis=1, keepdims=True)
        hit = lane == idx
        picks.append(hit)
        sel = jnp.where(hit, 1.0, sel)
        work = jnp.where(hit, -jnp.inf, work)
        top_e = jnp.where(lane_k == kk, idx, top_e)
        top_v = jnp.where(lane_k == kk, m, top_v)
    ex = jnp.exp(top_v - top_v.max(axis=1, keepdims=True))
    gate_ref[...] = ex / ex.sum(axis=1, keepdims=True)
    te_ref[...] = top_e

    @pl.when(r == 0)
    def _():
        carry_ref[...] = jnp.zeros_like(carry_ref)

    tr = lax.broadcasted_iota(jnp.int32, (TL, TL), 0)
    tc = lax.broadcasted_iota(jnp.int32, (TL, TL), 1)
    before = jnp.dot((tc < tr).astype(BF16), sel.astype(BF16), preferred_element_type=F32)
    pos = before + carry_ref[...]
    rank = jnp.zeros((TL, TOP_K), F32)
    for kk in range(TOP_K):
        rk = jnp.sum(jnp.where(picks[kk], pos, 0.0), axis=1, keepdims=True)
        rank = jnp.where(lane_k == kk, rk, rank)
    rank_ref[...] = rank.astype(jnp.int32)
    total = carry_ref[...] + jnp.sum(sel, axis=0, keepdims=True)
    carry_ref[...] = total
    cnt_ref[...] = total


def mixer_out(lay, kind, pro_args, w_out, x, mod, gain_f, w_r, b_r):
    D = D_MODEL
    nb = lay.nb
    if kind == "conv":
        v, bg, cw = pro_args
        per = TL // BF16_ROWS
        last_tile = lay.rows // BF16_ROWS - 1
        pro_specs = [pl.BlockSpec((BF16_ROWS, D), lambda r: (jnp.maximum(r * per - 1, 0), 0)),
                     lay.row_spec(D),
                     pl.BlockSpec((BF16_ROWS, D), lambda r: (jnp.minimum((r + 1) * per, last_tile), 0)),
                     lay.row_spec(D), _const_spec((CONV_WIDTH, D))]
        pro_in = [v, v, v, bg, cw]
    elif kind == "mlstm":
        h_f, h_b, og, ng = pro_args
        pro_specs = [lay.row_spec(ML_V), lay.row_spec(ML_V), lay.row_spec(ML_V), _const_spec((1, ML_V))]
        pro_in = [h_f, h_b, og, ng.reshape(1, ML_V)]
    else:
        pro_specs = [lay.row_spec(D)]
        pro_in = list(pro_args)
    k_in = w_out.shape[0]
    small = lambda dt: jax.ShapeDtypeStruct((lay.rows, TOP_K), dt)
    return pl.pallas_call(
        functools.partial(_mixer_out_kernel, kind=kind, nb=nb, ctx_first=lay.ctx_first),
        grid=(lay.n_blocks,),
        in_specs=pro_specs + [_const_spec((k_in, D)), lay.row_spec(D), lay.mod_spec(2),
                              _const_spec((1, D)), lay.mod_spec(4), lay.mod_spec(3),
                              _const_spec((D, N_EXPERTS)), _const_spec((1, N_EXPERTS))],
        out_specs=[lay.row_spec(D), lay.row_spec(D), lay.row_spec(TOP_K), lay.row_spec(TOP_K),
                   lay.row_spec(TOP_K), _const_spec((1, N_EXPERTS))],
        out_shape=[jax.ShapeDtypeStruct((lay.rows, D), F32), jax.ShapeDtypeStruct((lay.rows, D), BF16),
                   small(jnp.int32), small(F32), small(jnp.int32),
                   jax.ShapeDtypeStruct((1, N_EXPERTS), F32)],
        scratch_shapes=[pltpu.VMEM((1, N_EXPERTS), F32)],
        compiler_params=_params(1),
        name="mixer_out_" + kind,
    )(*pro_in, w_out.astype(BF16), x, mod, gain_f, mod, mod, w_r, b_r.reshape(1, N_EXPERTS))


def _expert_ffn_kernel(blk_e_ref, x_ref, w1_ref, b1_ref, w2_ref, b2_ref, o_ref):
    del blk_e_ref
    x = x_ref[...]
    h = jnp.dot(x, w1_ref[0], preferred_element_type=F32) + b1_ref[0]
    glu = jnp.minimum(h[:, :MOE_FF], SWIGLU_LIMIT)
    lin = jnp.clip(h[:, MOE_FF:], -SWIGLU_LIMIT, SWIGLU_LIMIT)
    act = glu * jax.nn.sigmoid(SWIGLU_ALPHA * glu) * (lin + 1.0)
    y = jnp.dot(act.astype(BF16), w2_ref[0], preferred_element_type=F32)
    o_ref[...] = (y + b2_ref[0]).astype(o_ref.dtype)


def expert_ffn(xp, blk_e, w1, b1, w2, b2):
    n_rows, D = xp.shape
    n_blk = n_rows // MOE_BLOCK
    grid_spec = pltpu.PrefetchScalarGridSpec(
        num_scalar_prefetch=1,
        grid=(n_blk,),
        in_specs=[
            pl.BlockSpec((MOE_BLOCK, D), lambda i, be: (i, 0)),
            pl.BlockSpec((1, D, 2 * MOE_FF), lambda i, be: (be[i], 0, 0)),
            pl.BlockSpec((1, 1, 2 * MOE_FF), lambda i, be: (be[i], 0, 0)),
            pl.BlockSpec((1, MOE_FF, D), lambda i, be: (be[i], 0, 0)),
            pl.BlockSpec((1, 1, D), lambda i, be: (be[i], 0, 0)),
        ],
        out_specs=pl.BlockSpec((MOE_BLOCK, D), lambda i, be: (i, 0)),
    )
    return pl.pallas_call(
        _expert_ffn_kernel,
        grid_spec=grid_spec,
        out_shape=jax.ShapeDtypeStruct((n_rows, D), BF16),
        compiler_params=_params(1),
        name="expert_ffn",
    )(blk_e, xp, w1, b1, w2, b2)


def _combine_kernel(x_ref, yg_ref, gate_ref, gf_ref, o_ref):
    D = D_MODEL
    gates = gate_ref[...]
    acc = gates[:, 0:1] * yg_ref[:, :D].astype(F32)
    for kk in range(1, TOP_K):
        acc = acc + gates[:, kk:kk + 1] * yg_ref[:, kk * D:(kk + 1) * D].astype(F32)
    o_ref[...] = x_ref[...] + gf_ref[0] * acc


def moe_combine(lay, x, yg, gates, mod, drop_ctx):
    D = D_MODEL
    if drop_ctx:
        nbo = lay.nb - 1
        src = lambda r: (r // nbo) * lay.nb + 1 + r % nbo
        n_out = lay.n_batch * nbo
    else:
        src = lambda r: r
        n_out = lay.n_blocks
    return pl.pallas_call(
        _combine_kernel,
        grid=(n_out,),
        in_specs=[pl.BlockSpec((TL, D), lambda r: (src(r), 0)),
                  pl.BlockSpec((TL, TOP_K * D), lambda r: (src(r), 0)),
                  pl.BlockSpec((TL, TOP_K), lambda r: (src(r), 0)),
                  pl.BlockSpec((1, 1, D), lambda r: (lay.mod_row(src(r)), 0, 5))],
        out_specs=pl.BlockSpec((TL, D), lambda r: (r, 0)),
        out_shape=jax.ShapeDtypeStruct((n_out * TL, D), F32),
        compiler_params=_params(1),
        name="moe_combine",
    )(x, yg, gates, mod)


def moe_route(top_e, rank, counts):
    T = top_e.shape[0]
    counts = counts.reshape(N_EXPERTS).astype(jnp.int32)
    group_start = jnp.cumsum(counts) - counts
    padded = (counts + MOE_BLOCK - 1) // MOE_BLOCK * MOE_BLOCK
    padded_end = jnp.cumsum(padded)
    padded_start = padded_end - padded
    dest = padded_start[top_e] + rank
    n_rows = -(-(T * TOP_K) // MOE_BLOCK) * MOE_BLOCK + N_EXPERTS * MOE_BLOCK
    n_blk = n_rows // MOE_BLOCK
    blk_start = jnp.arange(n_blk) * MOE_BLOCK
    blk_e = jnp.minimum(jnp.sum(padded_end[None, :] <= blk_start[:, None], axis=1), N_EXPERTS - 1)
    order = jnp.argsort(top_e.reshape(-1))
    row_e = jnp.repeat(blk_e, MOE_BLOCK)
    local = jnp.arange(n_rows) - padded_start[row_e]
    src_flat = order[jnp.clip(group_start[row_e] + local, 0, T * TOP_K - 1)]
    return src_flat // TOP_K, dest.reshape(-1), blk_e.astype(jnp.int32)


def kernel(x, c, ctx, c_ctx, norm_mix, norm_ffn, w_mod, b_mod, conv_w_in, conv_w, conv_w_out, ml_w_in, ml_b_gate, ml_norm, ml_w_out, mla_w_in, mla_q_norm, mla_kv_norm, mla_w_uq, mla_w_ukv, mla_qn_nope, mla_qn_rope, mla_kn_nope, mla_kn_rope, mla_w_out, moe_w_router, moe_b_router, moe_w1, moe_b1, moe_w2, moe_b2):
    Bn, n_lat, D = x.shape
    n_ctx = ctx.shape[1]
    assert D == D_MODEL and n_ctx == TL and n_lat % TL == 0
    assert (DEPTH - 1) % N_MIXERS == 0
    full = Layout(Bn, (n_ctx + n_lat) // TL, True)
    lat_only = Layout(Bn, n_lat // TL, False)
    mods = ada_all(c, c_ctx, w_mod, b_mod)
    tables = rope_tables(n_ctx, n_lat)
    X = jnp.concatenate([ctx, x], axis=1).reshape(full.rows, D)
    for layer in range(DEPTH):
        kind, j = layer % N_MIXERS, layer // N_MIXERS
        last = layer == DEPTH - 1
        lay = lat_only if last else full
        mod = mods[layer]
        gain_a = norm_mix[layer].reshape(1, D)
        gain_f = norm_ffn[layer].reshape(1, D)
        if kind == 0:
            bg, v = conv_in(lay, X, gain_a, mod, conv_w_in[j].astype(BF16))
            pro, w_out, name = (v, bg, conv_w[j]), conv_w_out[j], "conv"
        elif kind == 1:
            w = ml_w_in[j]
            n_main = 2 * ML_QK + 2 * ML_V
            w_main = jnp.concatenate([w[:, :ML_QK] * ML_DQK ** -0.5, w[:, ML_QK:n_main]], axis=1)
            q, k, v, og, g, gt = mlstm_in(lay, X, gain_a, mod, w_main.astype(BF16), w[:, n_main:],
                                          ml_b_gate[j])
            h_f, h_b = mlstm_scan(lay, q, k, v, g, gt)
            pro, w_out, name = (h_f, h_b, og, ml_norm[j]), ml_w_out[j], "mlstm"
        else:
            q, k, v = mla_in(lay, X, gain_a, mod, mla_w_in[j], mla_q_norm[j], mla_kv_norm[j],
                             mla_w_uq[j], mla_w_ukv[j], mla_qn_nope[j], mla_qn_rope[j],
                             mla_kn_nope[j], mla_kn_rope[j], tables)
            pro, w_out, name = (mla_attention(lay, q, k, v),), mla_w_out[j], "mla"
        X, h2, top_e, gates, rank, counts = mixer_out(
            lay, name, pro, w_out, X, mod, gain_f, moe_w_router[layer], moe_b_router[layer])
        src_tok, dest, blk_e = moe_route(top_e, rank, counts)
        xp = h2[src_tok]
        yp = expert_ffn(xp, blk_e, moe_w1[layer].astype(BF16),
                        moe_b1[layer].reshape(N_EXPERTS, 1, 2 * MOE_FF), moe_w2[layer].astype(BF16),
                        moe_b2[layer].reshape(N_EXPERTS, 1, D))
        yg = yp[dest].reshape(lay.rows, TOP_K * D)
        X = moe_combine(lay, X, yg, gates, mod, drop_ctx=(layer == DEPTH - 2))
    return X.reshape(Bn, n_lat, D)
```

```python
import functools

import jax
import jax.numpy as jnp
from jax import lax
from jax.experimental import pallas as pl
from jax.experimental.pallas import tpu as pltpu
from jax.experimental.pallas import tpu_sc as plsc

D_MODEL = 1024
DEPTH = 4
GRID_W = 64
N_MIXERS = 3
N_ADA = 6
RMS_EPS = 1e-6
CONV_WIDTH = 3
ML_HEADS = 8
ML_DQK = 64
ML_DV = 128
ML_QK = ML_HEADS * ML_DQK
ML_V = ML_HEADS * ML_DV
GATE_CAP = 15.0
MLA_HEADS = 8
MLA_NOPE = 128
MLA_ROPE = 64
MLA_V = 128
MLA_Q_LORA = 384
MLA_KV_LORA = 256
MLA_SCALE = (MLA_NOPE + MLA_ROPE) ** -0.5
ROPE_THETA = 10000.0
N_EXPERTS = 32
TOP_K = 4
MOE_FF = D_MODEL
SWIGLU_ALPHA = 1.702
SWIGLU_LIMIT = 7.0
MOE_BLOCK = 256

TL = 256
LANES = 128
BF16_ROWS = 16
VMEM_LIMIT = 48 * 1024 * 1024
HI = lax.Precision.HIGHEST
F32 = jnp.float32
BF16 = jnp.bfloat16


def _params(n_axes):
    return pltpu.CompilerParams(dimension_semantics=("arbitrary",) * n_axes,
                                vmem_limit_bytes=VMEM_LIMIT)


def _rms(x, width=None):
    width = x.shape[-1] if width is None else width
    return x * lax.rsqrt(jnp.sum(x * x, axis=-1, keepdims=True) * (1.0 / width) + RMS_EPS)


def _norm_mod(x, gain, scale, shift):
    return _rms(x) * gain * (1.0 + scale) + shift


PACK_W = D_MODEL // 2
HIGH_HALF = -65536


def _pack_rows(x):
    xb = x.astype(BF16).astype(F32)
    lo = lax.bitcast_convert_type(xb[:, :PACK_W], jnp.int32)
    hi = lax.bitcast_convert_type(xb[:, PACK_W:], jnp.int32)
    return hi | lax.shift_right_logical(lo, 16)


def _unpack_rows(w, dtype):
    lo = lax.bitcast_convert_type(lax.shift_left(w, 16), F32)
    hi = lax.bitcast_convert_type(w & HIGH_HALF, F32)
    return jnp.concatenate([lo.astype(dtype), hi.astype(dtype)], axis=1)


class Layout:
    def __init__(self, n_batch, nb, ctx_first):
        self.n_batch, self.nb, self.ctx_first = n_batch, nb, ctx_first
        self.n_blocks = n_batch * nb
        self.rows = self.n_blocks * TL

    def mod_row(self, r):
        b = r // self.nb
        return jnp.where(r % self.nb == 0, self.n_batch, b) if self.ctx_first else b

    def row_spec(self, width):
        return pl.BlockSpec((TL, width), lambda r: (r, 0))

    def mod_spec(self, piece):
        return pl.BlockSpec((1, 1, D_MODEL), lambda r: (self.mod_row(r), 0, piece))


def _const_spec(shape):
    return pl.BlockSpec(shape, lambda *_: (0,) * len(shape))


ADA_ROWS = 16
ADA_TN = 1536


def _ada_kernel(c_ref, w_ref, b_ref, o_ref):
    c = c_ref[...]
    s = c * jax.nn.sigmoid(c)
    o_ref[0] = jnp.dot(s, w_ref[0], precision=HI, preferred_element_type=F32) + b_ref[0]


def ada_all(c, c_ctx, w_mod, b_mod):
    Bn, D = c.shape
    cond = jnp.zeros((ADA_ROWS, D), F32).at[:Bn].set(c).at[Bn].set(c_ctx)
    out = pl.pallas_call(
        _ada_kernel,
        grid=(DEPTH, N_ADA * D // ADA_TN),
        in_specs=[pl.BlockSpec((ADA_ROWS, D), lambda l, n: (0, 0)),
                  pl.BlockSpec((1, D, ADA_TN), lambda l, n: (l, 0, n)),
                  pl.BlockSpec((1, 1, ADA_TN), lambda l, n: (l, 0, n))],
        out_specs=pl.BlockSpec((1, ADA_ROWS, ADA_TN), lambda l, n: (l, 0, n)),
        out_shape=jax.ShapeDtypeStruct((DEPTH, ADA_ROWS, N_ADA * D), F32),
        compiler_params=_params(2),
        name="ada_mod",
    )(cond, w_mod, b_mod.reshape(DEPTH, 1, N_ADA * D))
    return out[:, :Bn + 1, None, :]


def _conv_in_kernel(x_ref, gain_ref, sc_ref, sh_ref, w_ref, bg_ref, v_ref):
    D = D_MODEL
    h = _norm_mod(x_ref[...], gain_ref[...], sc_ref[0], sh_ref[0]).astype(BF16)
    p = jnp.dot(h, w_ref[...], preferred_element_type=F32)
    bg_ref[...] = p[:, :D].astype(BF16)
    v_ref[...] = (p[:, D:2 * D] * p[:, 2 * D:]).astype(BF16)


def conv_in(lay, x, gain, mod, w_in):
    D = D_MODEL
    sds = jax.ShapeDtypeStruct((lay.rows, D), BF16)
    return pl.pallas_call(
        _conv_in_kernel,
        grid=(lay.n_blocks,),
        in_specs=[lay.row_spec(D), _const_spec((1, D)), lay.mod_spec(1), lay.mod_spec(0),
                  _const_spec((D, 3 * D))],
        out_specs=[lay.row_spec(D), lay.row_spec(D)],
        out_shape=[sds, sds],
        compiler_params=_params(1),
        name="conv_in",
    )(x, gain, mod, mod, w_in)


ML_T = TL
ML_SW = 2 * ML_DV
ML_NG = 4 * ML_HEADS


def _gate_act(g, is_forget):
    g = GATE_CAP * jnp.tanh(g * (1.0 / GATE_CAP))
    log_sig = jnp.minimum(g, 0.0) - jnp.log(1.0 + jnp.exp(-jnp.abs(g)))
    return jnp.where(is_forget, log_sig, g)


def _mlstm_in_kernel(x_ref, gain_ref, sc_ref, sh_ref, w_ref, wg_ref, wgt_ref, bg_ref, bgt_ref,
                     q_ref, k_ref, v_ref, og_ref, g_ref, gt_ref):
    h = _norm_mod(x_ref[...], gain_ref[...], sc_ref[0], sh_ref[0])
    p = jnp.dot(h.astype(BF16), w_ref[...], preferred_element_type=F32)
    q_ref[...] = p[:, :ML_QK].astype(BF16)
    k_ref[...] = p[:, ML_QK:2 * ML_QK].astype(BF16)
    v_ref[...] = p[:, 2 * ML_QK:2 * ML_QK + ML_V].astype(BF16)
    og_ref[...] = jax.nn.sigmoid(p[:, 2 * ML_QK + ML_V:]).astype(BF16)
    g = jnp.dot(h, wg_ref[...], precision=HI, preferred_element_type=F32) + bg_ref[...]
    col = lax.broadcasted_iota(jnp.int32, g.shape, 1)
    g_ref[...] = _gate_act(g, (col // ML_HEADS) % 2 == 1)
    gt = lax.dot_general(wgt_ref[...], h, (((1,), (1,)), ((), ())), precision=HI,
                         preferred_element_type=F32) + bgt_ref[...]
    row = lax.broadcasted_iota(jnp.int32, gt.shape, 0)
    gt_ref[...] = _gate_act(gt, (row // ML_HEADS) % 2 == 1)


def mlstm_in(lay, x, gain, mod, w_main, w_g, b_g):
    D = D_MODEL
    n_main = 2 * ML_QK + 2 * ML_V
    bf = lambda w: jax.ShapeDtypeStruct((lay.rows, w), BF16)
    return pl.pallas_call(
        _mlstm_in_kernel,
        grid=(lay.n_blocks,),
        in_specs=[lay.row_spec(D), _const_spec((1, D)), lay.mod_spec(1), lay.mod_spec(0),
                  _const_spec((D, n_main)), _const_spec((D, ML_NG)), _const_spec((ML_NG, D)),
                  _const_spec((1, ML_NG)), _const_spec((ML_NG, 1))],
        out_specs=[lay.row_spec(ML_QK), lay.row_spec(ML_QK), lay.row_spec(ML_V), lay.row_spec(ML_V),
                   lay.row_spec(ML_NG), pl.BlockSpec((ML_NG, TL), lambda r: (0, r))],
        out_shape=[bf(ML_QK), bf(ML_QK), bf(ML_V), bf(ML_V),
                   jax.ShapeDtypeStruct((lay.rows, ML_NG), F32),
                   jax.ShapeDtypeStruct((ML_NG, lay.rows), F32)],
        compiler_params=_params(1),
        name="mlstm_in",
    )(x, gain, mod, mod, w_main, w_g, w_g.T, b_g.reshape(1, ML_NG), b_g.reshape(ML_NG, 1))


def _mlstm_dir(reverse, q_ref, k_ref, v_ref, g_ref, gt_ref, o_ref, s_ref, m_ref):
    T = ML_T
    row = lax.broadcasted_iota(jnp.int32, (T, T), 0)
    col = lax.broadcasted_iota(jnp.int32, (T, T), 1)
    mask = (col >= row) if reverse else (col <= row)
    g = g_ref[...]
    gt = gt_ref[...]
    tri_c = mask.astype(F32)
    tri_r = ((row >= col) if reverse else (row <= col)).astype(F32)
    bc = jnp.dot(tri_c, g, precision=HI, preferred_element_type=F32)
    br = jnp.dot(gt, tri_r, precision=HI, preferred_element_type=F32)
    gi, gf = (2 * ML_HEADS, 3 * ML_HEADS) if reverse else (0, ML_HEADS)
    lane = lax.broadcasted_iota(jnp.int32, (T, 2 * ML_DQK), 1)
    ones_col = (lax.broadcasted_iota(jnp.int32, (T, ML_DV), 1) == 0).astype(BF16)
    for h in range(ML_HEADS):
        pair = (h // 2) * 2 * ML_DQK
        own = (lane >= ML_DQK) if (h % 2) else (lane < ML_DQK)
        qp = q_ref[:, pair:pair + 2 * ML_DQK]
        kp = k_ref[:, pair:pair + 2 * ML_DQK]
        qm = jnp.where(own, qp, jnp.zeros_like(qp))
        km = jnp.where(own, kp, jnp.zeros_like(kp))
        v_ext = jnp.concatenate([v_ref[:, h * ML_DV:(h + 1) * ML_DV], ones_col], axis=1)
        b_col = bc[:, gf + h:gf + h + 1]
        i_col = g[:, gi + h:gi + h + 1]
        b_row = br[gf + h:gf + h + 1, :]
        i_row = gt[gi + h:gi + h + 1, :]
        tot = b_row[:, 0:1] if reverse else b_row[:, T - 1:T]
        m_prev = m_ref[h][0:1, 0:1]
        s_prev = s_ref[h]
        d = jnp.where(mask, b_col - b_row + i_row, -jnp.inf)
        inter = b_col + m_prev
        m_t = jnp.maximum(inter, jnp.max(d, axis=1, keepdims=True))
        s_raw = lax.dot_general(qm, km, (((1,), (1,)), ((), ())), preferred_element_type=F32)
        p = (s_raw * jnp.exp(d - m_t)).astype(BF16)
        a = jnp.exp(inter - m_t)
        r = (jnp.dot(p, v_ext, preferred_element_type=F32)
             + a * jnp.dot(qm, s_prev.astype(BF16), preferred_element_type=F32))
        den = r[:, ML_DV:ML_DV + 1]
        o_ref[:, h * ML_DV:(h + 1) * ML_DV] = (
            r[:, :ML_DV] / jnp.maximum(jnp.abs(den), jnp.exp(-m_t))).astype(o_ref.dtype)
        g_col = tot - b_col + i_col
        m_new = jnp.maximum(tot + m_prev, jnp.max(g_col, axis=0, keepdims=True))
        decay = jnp.exp(tot + m_prev - m_new)
        wv = (jnp.exp(g_col - m_new) * v_ext.astype(F32)).astype(BF16)
        s_ref[h] = decay * s_prev + lax.dot_general(
            km, wv, (((0,), (0,)), ((), ())), preferred_element_type=F32)
        m_ref[h] = jnp.broadcast_to(m_new, m_ref.shape[1:])


def _mlstm_scan_kernel(qf_ref, kf_ref, vf_ref, gf_ref, gtf_ref, qb_ref, kb_ref, vb_ref, gb_ref, gtb_ref,
                       of_ref, ob_ref, sf_ref, mf_ref, sb_ref, mb_ref):
    @pl.when(pl.program_id(1) == 0)
    def _():
        sf_ref[...] = jnp.zeros_like(sf_ref)
        mf_ref[...] = jnp.zeros_like(mf_ref)
        sb_ref[...] = jnp.zeros_like(sb_ref)
        mb_ref[...] = jnp.zeros_like(mb_ref)

    _mlstm_dir(False, qf_ref, kf_ref, vf_ref, gf_ref, gtf_ref, of_ref, sf_ref, mf_ref)
    _mlstm_dir(True, qb_ref, kb_ref, vb_ref, gb_ref, gtb_ref, ob_ref, sb_ref, mb_ref)


def mlstm_scan(lay, q, k, v, g, gt):
    assert lay.ctx_first
    nb = lay.nb
    fwd = lambda b, j: (b * nb + j, 0)
    bwd = lambda b, j: (b * nb + jnp.where(j == 0, 0, nb - j), 0)
    fwd_t = lambda b, j: (0, b * nb + j)
    bwd_t = lambda b, j: (0, b * nb + jnp.where(j == 0, 0, nb - j))

    def specs(im, imt):
        return [pl.BlockSpec((ML_T, ML_QK), im), pl.BlockSpec((ML_T, ML_QK), im),
                pl.BlockSpec((ML_T, ML_V), im), pl.BlockSpec((ML_T, ML_NG), im),
                pl.BlockSpec((ML_NG, ML_T), imt)]

    out_sds = jax.ShapeDtypeStruct((lay.rows, ML_V), F32)
    state = [pltpu.VMEM((ML_HEADS, 2 * ML_DQK, ML_SW), F32),
             pltpu.VMEM((ML_HEADS, 8, LANES), F32)]
    return pl.pallas_call(
        _mlstm_scan_kernel,
        grid=(lay.n_batch, nb),
        in_specs=specs(fwd, fwd_t) + specs(bwd, bwd_t),
        out_specs=[pl.BlockSpec((ML_T, ML_V), fwd), pl.BlockSpec((ML_T, ML_V), bwd)],
        out_shape=[out_sds, out_sds],
        scratch_shapes=state + state,
        compiler_params=_params(2),
        name="mlstm_scan",
    )(q, k, v, g, gt, q, k, v, g, gt)


MLA_QK_PAD = 256
MLA_IN_PAD = MLA_Q_LORA + MLA_KV_LORA + LANES
ROPE_HALF = MLA_ROPE // 4


def _mla_in_kernel(x_ref, gain_ref, sc_ref, sh_ref, win_ref, wuq_ref, wukv_ref, qn_ref, kvn_ref,
                   qnn_ref, qnr_ref, knn_ref, knr_ref, cos_ref, sa_ref, sb_ref,
                   q_out, k_out, v_out):
    h = _norm_mod(x_ref[...], gain_ref[...], sc_ref[0], sh_ref[0]).astype(BF16)
    p = jnp.dot(h, win_ref[...], preferred_element_type=F32)
    cq = _rms(p[:, :MLA_Q_LORA]) * qn_ref[...]
    ckv = _rms(p[:, MLA_Q_LORA:MLA_Q_LORA + MLA_KV_LORA]) * kvn_ref[...]
    kr = p[:, MLA_Q_LORA + MLA_KV_LORA:]
    q = jnp.dot(cq.astype(BF16), wuq_ref[...], preferred_element_type=F32)
    kv = jnp.dot(ckv.astype(BF16), wukv_ref[...], preferred_element_type=F32)
    nv = MLA_HEADS * MLA_NOPE
    v_out[...] = kv[:, nv:].astype(BF16)
    cos, sa, sb = cos_ref[...], sa_ref[...], sb_ref[...]

    def rope(xp):
        return (xp * cos + pltpu.roll(xp, LANES - ROPE_HALF, 1) * sa + pltpu.roll(xp, ROPE_HALF, 1) * sb)

    kr = rope(_rms(kr, MLA_ROPE) * knr_ref[...]).astype(BF16)
    for hd in range(MLA_HEADS):
        c0 = hd * MLA_QK_PAD
        qn = _rms(q[:, c0:c0 + MLA_NOPE]) * qnn_ref[...]
        qr = rope(_rms(q[:, c0 + MLA_NOPE:c0 + MLA_QK_PAD], MLA_ROPE) * qnr_ref[...])
        q_out[:, c0:c0 + MLA_NOPE] = (qn * MLA_SCALE).astype(BF16)
        q_out[:, c0 + MLA_NOPE:c0 + MLA_QK_PAD] = (qr * MLA_SCALE).astype(BF16)
        kn = _rms(kv[:, hd * MLA_NOPE:(hd + 1) * MLA_NOPE]) * knn_ref[...]
        k_out[:, c0:c0 + MLA_NOPE] = kn.astype(BF16)
        k_out[:, c0 + MLA_NOPE:c0 + MLA_QK_PAD] = kr


def _pad_lanes(g):
    return jnp.pad(g, (0, LANES - g.shape[0])).reshape(1, LANES)


def mla_in(lay, x, gain, mod, w_in, q_norm, kv_norm, w_uq, w_ukv, qn_nope, qn_rope, kn_nope, kn_rope,
           tables):
    D = D_MODEL
    Hn = MLA_HEADS
    win = jnp.pad(w_in, ((0, 0), (0, MLA_IN_PAD - w_in.shape[1]))).astype(BF16)
    wuq = jnp.pad(w_uq.reshape(MLA_Q_LORA, Hn, MLA_NOPE + MLA_ROPE),
                  ((0, 0), (0, 0), (0, MLA_QK_PAD - MLA_NOPE - MLA_ROPE)))
    wuq = wuq.reshape(MLA_Q_LORA, Hn * MLA_QK_PAD).astype(BF16)
    wkv = w_ukv.reshape(MLA_KV_LORA, Hn, MLA_NOPE + MLA_V)
    wukv = jnp.concatenate([wkv[:, :, :MLA_NOPE].reshape(MLA_KV_LORA, Hn * MLA_NOPE),
                            wkv[:, :, MLA_NOPE:].reshape(MLA_KV_LORA, Hn * MLA_V)], axis=1).astype(BF16)
    nb = lay.nb
    tab_spec = pl.BlockSpec((TL, LANES), lambda r: (r % nb, 0))
    bf = lambda w: jax.ShapeDtypeStruct((lay.rows, w), BF16)
    return pl.pallas_call(
        _mla_in_kernel,
        grid=(lay.n_blocks,),
        in_specs=[lay.row_spec(D), _const_spec((1, D)), lay.mod_spec(1), lay.mod_spec(0),
                  _const_spec(win.shape), _const_spec(wuq.shape), _const_spec(wukv.shape),
                  _const_spec((1, MLA_Q_LORA)), _const_spec((1, MLA_KV_LORA)),
                  _const_spec((1, LANES)), _const_spec((1, LANES)), _const_spec((1, LANES)),
                  _const_spec((1, LANES)), tab_spec, tab_spec, tab_spec],
        out_specs=[lay.row_spec(Hn * MLA_QK_PAD), lay.row_spec(Hn * MLA_QK_PAD), lay.row_spec(Hn * MLA_V)],
        out_shape=[bf(Hn * MLA_QK_PAD), bf(Hn * MLA_QK_PAD), bf(Hn * MLA_V)],
        compiler_params=_params(1),
        name="mla_in",
    )(x, gain, mod, mod, win, wuq, wukv, q_norm.reshape(1, -1), kv_norm.reshape(1, -1),
      qn_nope.reshape(1, -1), _pad_lanes(qn_rope), kn_nope.reshape(1, -1), _pad_lanes(kn_rope), *tables)


def rope_tables(n_ctx, n_lat):
    n_freq = MLA_ROPE // 4
    inv = ROPE_THETA ** (-jnp.arange(n_freq, dtype=F32) / n_freq)
    t = jnp.arange(n_lat)
    a_r = (t // GRID_W).astype(F32)[:, None] * inv
    a_c = (t % GRID_W).astype(F32)[:, None] * inv
    ang = jnp.concatenate([a_r, a_r, a_c, a_c], axis=-1)
    ang = jnp.concatenate([jnp.zeros((n_ctx, MLA_ROPE), F32), ang], axis=0)
    cos, sin = jnp.cos(ang), jnp.sin(ang)
    low = (jnp.arange(MLA_ROPE) % (2 * ROPE_HALF)) < ROPE_HALF
    pad = lambda a: jnp.pad(a, ((0, 0), (0, LANES - MLA_ROPE)))
    return pad(cos), pad(jnp.where(low, -sin, 0.0)), pad(jnp.where(low, 0.0, sin))


def _mla_attn_kernel(q_ref, k_ref, v_ref, o_ref, *, n_ctx):
    q = q_ref[0]

    def attend(k, v):
        s = lax.dot_general(q, k, (((1,), (1,)), ((), ())), preferred_element_type=F32)
        p = jnp.exp(s - s.max(axis=1, keepdims=True))
        l = p.sum(axis=1, keepdims=True)
        o_ref[0] = (jnp.dot(p.astype(BF16), v, preferred_element_type=F32) / l).astype(o_ref.dtype)

    @pl.when(pl.program_id(2) == 0)
    def _():
        attend(k_ref[0, :n_ctx], v_ref[0, :n_ctx])

    @pl.when(pl.program_id(2) > 0)
    def _():
        attend(k_ref[0], v_ref[0])


def mla_attention(lay, q, k, v):
    assert lay.ctx_first
    Bn, S = lay.n_batch, lay.nb * TL
    q3, k3, v3 = (a.reshape(Bn, S, a.shape[-1]) for a in (q, k, v))
    out = pl.pallas_call(
        functools.partial(_mla_attn_kernel, n_ctx=TL),
        grid=(Bn, MLA_HEADS, lay.nb),
        in_specs=[pl.BlockSpec((1, TL, MLA_QK_PAD), lambda b, h, i: (b, i, h)),
                  pl.BlockSpec((1, S, MLA_QK_PAD), lambda b, h, i: (b, 0, h)),
                  pl.BlockSpec((1, S, MLA_V), lambda b, h, i: (b, 0, h))],
        out_specs=pl.BlockSpec((1, TL, MLA_V), lambda b, h, i: (b, i, h)),
        out_shape=jax.ShapeDtypeStruct((Bn, S, MLA_HEADS * MLA_V), BF16),
        compiler_params=_params(3),
        name="mla_attention",
    )(q3, k3, v3)
    return out.reshape(lay.rows, MLA_HEADS * MLA_V)


N_PROLOGUE = {"conv": 5, "mlstm": 4, "mla": 1}


def _mixer_out_kernel(*refs, kind, nb, ctx_first):
    n_pro = N_PROLOGUE[kind]
    pro = refs[:n_pro]
    (wout_ref, x_ref, ga_ref, gain_ref, sc_ref, sh_ref, wr_ref, br_ref,
     xo_ref, h2_ref, te_ref, gate_ref, rank_ref, cnt_ref, carry_ref) = refs[n_pro:]
    r = pl.program_id(0)

    if kind == "conv":
        vprev_ref, v_ref, vnext_ref, bg_ref, cw_ref = pro
        j = r % nb
        first = (j == 0) | (j == 1) if ctx_first else (j == 0)
        last = (j == nb - 1) | (j == 0) if ctx_first else (j == nb - 1)
        v = v_ref[...].astype(F32)
        rows = lax.broadcasted_iota(jnp.int32, (TL, 1), 0)
        prev_row = jnp.where(first, 0.0, vprev_ref[BF16_ROWS - 1:BF16_ROWS, :].astype(F32))
        next_row = jnp.where(last, 0.0, vnext_ref[0:1, :].astype(F32))
        up = jnp.where(rows == 0, prev_row, pltpu.roll(v, 1, 0))
        dn = jnp.where(rows == TL - 1, next_row, pltpu.roll(v, TL - 1, 0))
        cw = cw_ref[...]
        a = bg_ref[...].astype(F32) * (up * cw[0:1] + v * cw[1:2] + dn * cw[2:3])
    elif kind == "mlstm":
        hf_ref, hb_ref, og_ref, ng_ref = pro
        hh = hf_ref[...] + hb_ref[...]
        a = jnp.concatenate([_rms(hh[:, h * ML_DV:(h + 1) * ML_DV]) for h in range(ML_HEADS)], axis=1)
        a = a * ng_ref[...] * og_ref[...].astype(F32)
    else:
        a = pro[0][...]

    y = jnp.dot(a.astype(BF16), wout_ref[...], preferred_element_type=F32)
    xn = x_ref[...] + ga_ref[0] * y
    xo_ref[...] = xn
    h2 = _norm_mod(xn, gain_ref[...], sc_ref[0], sh_ref[0])
    h2_ref[...] = _pack_rows(h2)
    logits = jnp.dot(h2, wr_ref[...], precision=HI, preferred_element_type=F32) + br_ref[...]

    lane = lax.broadcasted_iota(jnp.int32, (TL, N_EXPERTS), 1)
    lane_k = lax.broadcasted_iota(jnp.int32, (TL, TOP_K), 1)
    work = logits
    sel = jnp.zeros((TL, N_EXPERTS), F32)
    top_e = jnp.zeros((TL, TOP_K), jnp.int32)
    top_v = jnp.zeros((TL, TOP_K), F32)
    picks = []
    for kk in range(TOP_K):
        m = work.max(axis=1, keepdims=True)
        idx = jnp.min(jnp.where(work == m, lane, N_EXPERTS), axis=1, keepdims=True)
        hit = lane == idx
        picks.append(hit)
        sel = jnp.where(hit, 1.0, sel)
        work = jnp.where(hit, -jnp.inf, work)
        top_e = jnp.where(lane_k == kk, idx, top_e)
        top_v = jnp.where(lane_k == kk, m, top_v)
    ex = jnp.exp(top_v - top_v.max(axis=1, keepdims=True))
    gate_ref[...] = ex / ex.sum(axis=1, keepdims=True)
    te_ref[...] = top_e

    @pl.when(r == 0)
    def _():
        carry_ref[...] = jnp.zeros_like(carry_ref)

    tr = lax.broadcasted_iota(jnp.int32, (TL, TL), 0)
    tc = lax.broadcasted_iota(jnp.int32, (TL, TL), 1)
    before = jnp.dot((tc < tr).astype(BF16), sel.astype(BF16), preferred_element_type=F32)
    pos = before + carry_ref[...]
    rank = jnp.zeros((TL, TOP_K), F32)
    for kk in range(TOP_K):
        rk = jnp.sum(jnp.where(picks[kk], pos, 0.0), axis=1, keepdims=True)
        rank = jnp.where(lane_k == kk, rk, rank)
    rank_ref[...] = rank.astype(jnp.int32)
    total = carry_ref[...] + jnp.sum(sel, axis=0, keepdims=True)
    carry_ref[...] = total
    cnt_ref[...] = total


def mixer_out(lay, kind, pro_args, w_out, x, mod, gain_f, w_r, b_r):
    D = D_MODEL
    nb = lay.nb
    if kind == "conv":
        v, bg, cw = pro_args
        per = TL // BF16_ROWS
        last_tile = lay.rows // BF16_ROWS - 1
        pro_specs = [pl.BlockSpec((BF16_ROWS, D), lambda r: (jnp.maximum(r * per - 1, 0), 0)),
                     lay.row_spec(D),
                     pl.BlockSpec((BF16_ROWS, D), lambda r: (jnp.minimum((r + 1) * per, last_tile), 0)),
                     lay.row_spec(D), _const_spec((CONV_WIDTH, D))]
        pro_in = [v, v, v, bg, cw]
    elif kind == "mlstm":
        h_f, h_b, og, ng = pro_args
        pro_specs = [lay.row_spec(ML_V), lay.row_spec(ML_V), lay.row_spec(ML_V), _const_spec((1, ML_V))]
        pro_in = [h_f, h_b, og, ng.reshape(1, ML_V)]
    else:
        pro_specs = [lay.row_spec(D)]
        pro_in = list(pro_args)
    k_in = w_out.shape[0]
    small = lambda dt: jax.ShapeDtypeStruct((lay.rows, TOP_K), dt)
    return pl.pallas_call(
        functools.partial(_mixer_out_kernel, kind=kind, nb=nb, ctx_first=lay.ctx_first),
        grid=(lay.n_blocks,),
        in_specs=pro_specs + [_const_spec((k_in, D)), lay.row_spec(D), lay.mod_spec(2),
                              _const_spec((1, D)), lay.mod_spec(4), lay.mod_spec(3),
                              _const_spec((D, N_EXPERTS)), _const_spec((1, N_EXPERTS))],
        out_specs=[lay.row_spec(D), lay.row_spec(PACK_W), lay.row_spec(TOP_K), lay.row_spec(TOP_K),
                   lay.row_spec(TOP_K), _const_spec((1, N_EXPERTS))],
        out_shape=[jax.ShapeDtypeStruct((lay.rows, D), F32),
                   jax.ShapeDtypeStruct((lay.rows, PACK_W), jnp.int32),
                   small(jnp.int32), small(F32), small(jnp.int32),
                   jax.ShapeDtypeStruct((1, N_EXPERTS), F32)],
        scratch_shapes=[pltpu.VMEM((1, N_EXPERTS), F32)],
        compiler_params=_params(1),
        name="mixer_out_" + kind,
    )(*pro_in, w_out.astype(BF16), x, mod, gain_f, mod, mod, w_r, b_r.reshape(1, N_EXPERTS))


def _expert_ffn_kernel(blk_e_ref, first_ref, x_ref, w1_ref, b1_ref, w2_ref, b2_ref, o_ref,
                       w1b_ref, w2b_ref):
    del blk_e_ref

    @pl.when(first_ref[pl.program_id(0)] == 1)
    def _():
        w1b_ref[...] = w1_ref[0, 0].astype(BF16)
        w2b_ref[...] = w2_ref[0, 0].astype(BF16)

    x = _unpack_rows(x_ref[...], BF16)
    h = jnp.dot(x, w1b_ref[...], preferred_element_type=F32) + b1_ref[0, 0]
    glu = jnp.minimum(h[:, :MOE_FF], SWIGLU_LIMIT)
    lin = jnp.clip(h[:, MOE_FF:], -SWIGLU_LIMIT, SWIGLU_LIMIT)
    act = glu * jax.nn.sigmoid(SWIGLU_ALPHA * glu) * (lin + 1.0)
    y = jnp.dot(act.astype(BF16), w2b_ref[...], preferred_element_type=F32)
    o_ref[...] = _pack_rows(y + b2_ref[0, 0])


def expert_ffn(layer, xp, blk_e, blk_first, w1, b1, w2, b2):
    n_rows = xp.shape[0]
    D, F2 = D_MODEL, 2 * MOE_FF
    n_blk = n_rows // MOE_BLOCK
    grid_spec = pltpu.PrefetchScalarGridSpec(
        num_scalar_prefetch=2,
        grid=(n_blk,),
        in_specs=[
            pl.BlockSpec((MOE_BLOCK, PACK_W), lambda i, be, fi: (i, 0)),
            pl.BlockSpec((1, 1, D, F2), lambda i, be, fi: (layer, be[i], 0, 0)),
            pl.BlockSpec((1, 1, 1, F2), lambda i, be, fi: (layer, be[i], 0, 0)),
            pl.BlockSpec((1, 1, MOE_FF, D), lambda i, be, fi: (layer, be[i], 0, 0)),
            pl.BlockSpec((1, 1, 1, D), lambda i, be, fi: (layer, be[i], 0, 0)),
        ],
        out_specs=pl.BlockSpec((MOE_BLOCK, PACK_W), lambda i, be, fi: (i, 0)),
        scratch_shapes=[pltpu.VMEM((D, F2), BF16), pltpu.VMEM((MOE_FF, D), BF16)],
    )
    return pl.pallas_call(
        _expert_ffn_kernel,
        grid_spec=grid_spec,
        out_shape=jax.ShapeDtypeStruct((n_rows, PACK_W), jnp.int32),
        compiler_params=_params(1),
        name="expert_ffn",
    )(blk_e, blk_first, xp, w1, b1, w2, b2)


SC_CORES = 2
SC_SUBCORES = 16
SC_CHUNK = 64


def sc_gather(table, idx):
    n_idx = idx.shape[0]
    width = table.shape[1]
    n_workers = SC_CORES * SC_SUBCORES
    per_worker = n_idx // n_workers
    n_chunks = per_worker // SC_CHUNK
    assert n_chunks * SC_CHUNK * n_workers == n_idx
    mesh = plsc.VectorSubcoreMesh(core_axis_name="c", subcore_axis_name="s",
                                  num_cores=SC_CORES, num_subcores=SC_SUBCORES)

    def body(table_hbm, idx_hbm, out_hbm, idx_v, rows_v, sem):
        wid = lax.axis_index("s") * SC_CORES + lax.axis_index("c")
        pltpu.sync_copy(idx_hbm.at[wid], idx_v)

        @pl.loop(0, n_chunks)
        def _(ci):
            pltpu.async_copy(table_hbm.at[idx_v.at[ci]], rows_v, sem).wait()
            pltpu.sync_copy(rows_v, out_hbm.at[wid, ci])

    out = pl.kernel(
        body,
        out_type=jax.ShapeDtypeStruct((n_workers, n_chunks, SC_CHUNK, width), table.dtype),
        mesh=mesh,
        scratch_types=[pltpu.VMEM((n_chunks, SC_CHUNK), jnp.int32),
                       pltpu.VMEM((SC_CHUNK, width), table.dtype),
                       pltpu.SemaphoreType.DMA],
        name="sc_gather",
    )(table, idx.reshape(n_workers, n_chunks, SC_CHUNK))
    return out.reshape(n_idx, width)


def _combine_kernel(x_ref, *refs):
    y_refs, (gate_ref, gf_ref, o_ref) = refs[:TOP_K], refs[TOP_K:]
    gates = gate_ref[...]
    acc = gates[:, 0:1] * _unpack_rows(y_refs[0][...], F32)
    for kk in range(1, TOP_K):
        acc = acc + gates[:, kk:kk + 1] * _unpack_rows(y_refs[kk][...], F32)
    o_ref[...] = x_ref[...] + gf_ref[0] * acc


def moe_combine(lay, x, yg, gates, mod, drop_ctx):
    D = D_MODEL
    if drop_ctx:
        nbo = lay.nb - 1
        src = lambda r: (r // nbo) * lay.nb + 1 + r % nbo
        n_out = lay.n_batch * nbo
    else:
        src = lambda r: r
        n_out = lay.n_blocks
    y_specs = [pl.BlockSpec((TL, PACK_W), functools.partial(lambda kk, r: (kk * lay.n_blocks + src(r), 0), kk))
               for kk in range(TOP_K)]
    return pl.pallas_call(
        _combine_kernel,
        grid=(n_out,),
        in_specs=[pl.BlockSpec((TL, D), lambda r: (src(r), 0))] + y_specs + [
            pl.BlockSpec((TL, TOP_K), lambda r: (src(r), 0)),
            pl.BlockSpec((1, 1, D), lambda r: (lay.mod_row(src(r)), 0, 5))],
        out_specs=pl.BlockSpec((TL, D), lambda r: (r, 0)),
        out_shape=jax.ShapeDtypeStruct((n_out * TL, D), F32),
        compiler_params=_params(1),
        name="moe_combine",
    )(x, yg, yg, yg, yg, gates, mod)


def moe_route(top_e, rank, counts):
    T = top_e.shape[0]
    counts = counts.reshape(N_EXPERTS).astype(jnp.int32)
    group_start = jnp.cumsum(counts) - counts
    padded = (counts + MOE_BLOCK - 1) // MOE_BLOCK * MOE_BLOCK
    padded_end = jnp.cumsum(padded)
    padded_start = padded_end - padded
    dest = padded_start[top_e] + rank
    n_rows = -(-(T * TOP_K) // MOE_BLOCK) * MOE_BLOCK + N_EXPERTS * MOE_BLOCK
    n_blk = n_rows // MOE_BLOCK
    blk_start = jnp.arange(n_blk) * MOE_BLOCK
    blk_e = jnp.minimum(jnp.sum(padded_end[None, :] <= blk_start[:, None], axis=1), N_EXPERTS - 1)
    order = jnp.argsort(top_e.reshape(-1))
    row_e = jnp.repeat(blk_e, MOE_BLOCK)
    local = jnp.arange(n_rows) - padded_start[row_e]
    src_flat = order[jnp.clip(group_start[row_e] + local, 0, T * TOP_K - 1)]
    blk_e = blk_e.astype(jnp.int32)
    blk_first = jnp.concatenate([jnp.ones((1,), jnp.int32), (blk_e[1:] != blk_e[:-1]).astype(jnp.int32)])
    return (src_flat // TOP_K).astype(jnp.int32), dest.T.reshape(-1).astype(jnp.int32), blk_e, blk_first


def kernel(x, c, ctx, c_ctx, norm_mix, norm_ffn, w_mod, b_mod, conv_w_in, conv_w, conv_w_out, ml_w_in, ml_b_gate, ml_norm, ml_w_out, mla_w_in, mla_q_norm, mla_kv_norm, mla_w_uq, mla_w_ukv, mla_qn_nope, mla_qn_rope, mla_kn_nope, mla_kn_rope, mla_w_out, moe_w_router, moe_b_router, moe_w1, moe_b1, moe_w2, moe_b2):
    Bn, n_lat, D = x.shape
    n_ctx = ctx.shape[1]
    assert D == D_MODEL and n_ctx == TL and n_lat % TL == 0
    assert (DEPTH - 1) % N_MIXERS == 0
    full = Layout(Bn, (n_ctx + n_lat) // TL, True)
    lat_only = Layout(Bn, n_lat // TL, False)
    mods = ada_all(c, c_ctx, w_mod, b_mod)
    tables = rope_tables(n_ctx, n_lat)
    b1_all = moe_b1.reshape(DEPTH, N_EXPERTS, 1, 2 * MOE_FF)
    b2_all = moe_b2.reshape(DEPTH, N_EXPERTS, 1, D)
    X = jnp.concatenate([ctx, x], axis=1).reshape(full.rows, D)
    for layer in range(DEPTH):
        kind, j = layer % N_MIXERS, layer // N_MIXERS
        last = layer == DEPTH - 1
        lay = lat_only if last else full
        mod = mods[layer]
        gain_a = norm_mix[layer].reshape(1, D)
        gain_f = norm_ffn[layer].reshape(1, D)
        if kind == 0:
            bg, v = conv_in(lay, X, gain_a, mod, conv_w_in[j].astype(BF16))
            pro, w_out, name = (v, bg, conv_w[j]), conv_w_out[j], "conv"
        elif kind == 1:
            w = ml_w_in[j]
            n_main = 2 * ML_QK + 2 * ML_V
            w_main = jnp.concatenate([w[:, :ML_QK] * ML_DQK ** -0.5, w[:, ML_QK:n_main]], axis=1)
            q, k, v, og, g, gt = mlstm_in(lay, X, gain_a, mod, w_main.astype(BF16), w[:, n_main:],
                                          ml_b_gate[j])
            h_f, h_b = mlstm_scan(lay, q, k, v, g, gt)
            pro, w_out, name = (h_f, h_b, og, ml_norm[j]), ml_w_out[j], "mlstm"
        else:
            q, k, v = mla_in(lay, X, gain_a, mod, mla_w_in[j], mla_q_norm[j], mla_kv_norm[j],
                             mla_w_uq[j], mla_w_ukv[j], mla_qn_nope[j], mla_qn_rope[j],
                             mla_kn_nope[j], mla_kn_rope[j], tables)
            pro, w_out, name = (mla_attention(lay, q, k, v),), mla_w_out[j], "mla"
        X, h2, top_e, gates, rank, counts = mixer_out(
            lay, name, pro, w_out, X, mod, gain_f, moe_w_router[layer], moe_b_router[layer])
        src_tok, dest, blk_e, blk_first = moe_route(top_e, rank, counts)
        xp = sc_gather(h2, src_tok)
        yp = expert_ffn(layer, xp, blk_e, blk_first, moe_w1, b1_all, moe_w2, b2_all)
        yg = sc_gather(yp, dest)
        X = moe_combine(lay, X, yg, gates, mod, drop_ctx=(layer == DEPTH - 2))
    return X.reshape(Bn, n_lat, D)
```

```python
import functools

import jax
import jax.numpy as jnp
from jax import lax
from jax.experimental import pallas as pl
from jax.experimental.pallas import tpu as pltpu
from jax.experimental.pallas import tpu_sc as plsc

D_MODEL = 1024
DEPTH = 4
GRID_W = 64
N_MIXERS = 3
N_ADA = 6
RMS_EPS = 1e-6
CONV_WIDTH = 3
ML_HEADS = 8
ML_DQK = 64
ML_DV = 128
ML_QK = ML_HEADS * ML_DQK
ML_V = ML_HEADS * ML_DV
GATE_CAP = 15.0
MLA_HEADS = 8
MLA_NOPE = 128
MLA_ROPE = 64
MLA_V = 128
MLA_Q_LORA = 384
MLA_KV_LORA = 256
MLA_SCALE = (MLA_NOPE + MLA_ROPE) ** -0.5
ROPE_THETA = 10000.0
N_EXPERTS = 32
TOP_K = 4
MOE_FF = D_MODEL
SWIGLU_ALPHA = 1.702
SWIGLU_LIMIT = 7.0
MOE_BLOCK = 512

TL = 256
LANES = 128
BF16_ROWS = 16
VMEM_LIMIT = 48 * 1024 * 1024
HI = lax.Precision.HIGHEST
F32 = jnp.float32
BF16 = jnp.bfloat16


def _params(n_axes):
    return pltpu.CompilerParams(dimension_semantics=("arbitrary",) * n_axes,
                                vmem_limit_bytes=VMEM_LIMIT)


def _rms(x, width=None):
    width = x.shape[-1] if width is None else width
    return x * lax.rsqrt(jnp.sum(x * x, axis=-1, keepdims=True) * (1.0 / width) + RMS_EPS)


def _norm_mod(x, gain, scale, shift):
    return _rms(x) * gain * (1.0 + scale) + shift


def _split_weight_t(w):
    hi = w.astype(BF16)
    lo = (w - hi.astype(F32)).astype(BF16)
    return jnp.concatenate([hi.T, lo.T], axis=0)


def _split_dot_t(w2, h):
    n = w2.shape[0] // 2
    dn = (((1,), (1,)), ((), ()))
    h_hi = h.astype(BF16)
    h_lo = (h - h_hi.astype(F32)).astype(BF16)
    both = lax.dot_general(w2, h_hi, dn, preferred_element_type=F32)
    cross = lax.dot_general(w2[:n], h_lo, dn, preferred_element_type=F32)
    return both[:n] + both[n:] + cross


PACK_W = D_MODEL // 2
HIGH_HALF = -65536


def _pack_rows(x):
    xb = x.astype(BF16).astype(F32)
    lo = lax.bitcast_convert_type(xb[:, :PACK_W], jnp.int32)
    hi = lax.bitcast_convert_type(xb[:, PACK_W:], jnp.int32)
    return hi | lax.shift_right_logical(lo, 16)


def _unpack_rows(w, dtype):
    lo = lax.bitcast_convert_type(lax.shift_left(w, 16), F32)
    hi = lax.bitcast_convert_type(w & HIGH_HALF, F32)
    return jnp.concatenate([lo.astype(dtype), hi.astype(dtype)], axis=1)


class Layout:
    def __init__(self, n_batch, nb, ctx_first):
        self.n_batch, self.nb, self.ctx_first = n_batch, nb, ctx_first
        self.n_blocks = n_batch * nb
        self.rows = self.n_blocks * TL

    def mod_row(self, r):
        b = r // self.nb
        return jnp.where(r % self.nb == 0, self.n_batch, b) if self.ctx_first else b

    def row_spec(self, width):
        return pl.BlockSpec((TL, width), lambda r: (r, 0))

    def mod_spec(self, piece):
        return pl.BlockSpec((1, 1, D_MODEL), lambda r: (self.mod_row(r), 0, piece))


def _const_spec(shape):
    return pl.BlockSpec(shape, lambda *_: (0,) * len(shape))


ADA_ROWS = 16
ADA_TN = 1536


def _ada_kernel(c_ref, w_ref, b_ref, o_ref):
    c = c_ref[...]
    s = c * jax.nn.sigmoid(c)
    o_ref[0] = jnp.dot(s, w_ref[0], precision=HI, preferred_element_type=F32) + b_ref[0]


def ada_all(c, c_ctx, w_mod, b_mod):
    Bn, D = c.shape
    cond = jnp.zeros((ADA_ROWS, D), F32).at[:Bn].set(c).at[Bn].set(c_ctx)
    out = pl.pallas_call(
        _ada_kernel,
        grid=(DEPTH, N_ADA * D // ADA_TN),
        in_specs=[pl.BlockSpec((ADA_ROWS, D), lambda l, n: (0, 0)),
                  pl.BlockSpec((1, D, ADA_TN), lambda l, n: (l, 0, n)),
                  pl.BlockSpec((1, 1, ADA_TN), lambda l, n: (l, 0, n))],
        out_specs=pl.BlockSpec((1, ADA_ROWS, ADA_TN), lambda l, n: (l, 0, n)),
        out_shape=jax.ShapeDtypeStruct((DEPTH, ADA_ROWS, N_ADA * D), F32),
        compiler_params=_params(2),
        name="ada_mod",
    )(cond, w_mod, b_mod.reshape(DEPTH, 1, N_ADA * D))
    return out[:, :Bn + 1, None, :]


def _conv_in_kernel(x_ref, gain_ref, sc_ref, sh_ref, w_ref, bg_ref, v_ref):
    D = D_MODEL
    h = _norm_mod(x_ref[...], gain_ref[...], sc_ref[0], sh_ref[0]).astype(BF16)
    p = jnp.dot(h, w_ref[...], preferred_element_type=F32)
    bg_ref[...] = p[:, :D].astype(BF16)
    v_ref[...] = (p[:, D:2 * D] * p[:, 2 * D:]).astype(BF16)


def conv_in(lay, x, gain, mod, w_in):
    D = D_MODEL
    sds = jax.ShapeDtypeStruct((lay.rows, D), BF16)
    return pl.pallas_call(
        _conv_in_kernel,
        grid=(lay.n_blocks,),
        in_specs=[lay.row_spec(D), _const_spec((1, D)), lay.mod_spec(1), lay.mod_spec(0),
                  _const_spec((D, 3 * D))],
        out_specs=[lay.row_spec(D), lay.row_spec(D)],
        out_shape=[sds, sds],
        compiler_params=_params(1),
        name="conv_in",
    )(x, gain, mod, mod, w_in)


ML_T = TL
ML_SW = 2 * ML_DV
ML_NG = 4 * ML_HEADS


LOG2E = 1.4426950408889634


def _gate_act(g, is_forget):
    g = GATE_CAP * jnp.tanh(g * (1.0 / GATE_CAP))
    log_sig = jnp.minimum(g, 0.0) - jnp.log(1.0 + jnp.exp(-jnp.abs(g)))
    return jnp.where(is_forget, log_sig, g) * LOG2E


def _mlstm_in_kernel(x_ref, gain_ref, sc_ref, sh_ref, w_ref, wg_ref, bgt_ref,
                     q_ref, k_ref, v_ref, og_ref, g_ref, gt_ref):
    h = _norm_mod(x_ref[...], gain_ref[...], sc_ref[0], sh_ref[0])
    p = jnp.dot(h.astype(BF16), w_ref[...], preferred_element_type=F32)
    q_ref[...] = p[:, :ML_QK].astype(BF16)
    k_ref[...] = p[:, ML_QK:2 * ML_QK].astype(BF16)
    v_ref[...] = p[:, 2 * ML_QK:2 * ML_QK + ML_V].astype(BF16)
    og_ref[...] = jax.nn.sigmoid(p[:, 2 * ML_QK + ML_V:]).astype(BF16)
    gt = _split_dot_t(wg_ref[...], h) + bgt_ref[...]
    row = lax.broadcasted_iota(jnp.int32, gt.shape, 0)
    gt = _gate_act(gt, (row // ML_HEADS) % 2 == 1)
    gt_ref[...] = gt
    eye = (lax.broadcasted_iota(jnp.int32, (TL, TL), 0)
           == lax.broadcasted_iota(jnp.int32, (TL, TL), 1)).astype(F32)
    g_ref[...] = lax.dot_general(eye, gt, (((1,), (1,)), ((), ())), precision=HI,
                                 preferred_element_type=F32)


def mlstm_in(lay, x, gain, mod, w_main, w_g, b_g):
    D = D_MODEL
    n_main = 2 * ML_QK + 2 * ML_V
    bf = lambda w: jax.ShapeDtypeStruct((lay.rows, w), BF16)
    return pl.pallas_call(
        _mlstm_in_kernel,
        grid=(lay.n_blocks,),
        in_specs=[lay.row_spec(D), _const_spec((1, D)), lay.mod_spec(1), lay.mod_spec(0),
                  _const_spec((D, n_main)), _const_spec((2 * ML_NG, D)), _const_spec((ML_NG, 1))],
        out_specs=[lay.row_spec(ML_QK), lay.row_spec(ML_QK), lay.row_spec(ML_V), lay.row_spec(ML_V),
                   lay.row_spec(ML_NG), pl.BlockSpec((ML_NG, TL), lambda r: (0, r))],
        out_shape=[bf(ML_QK), bf(ML_QK), bf(ML_V), bf(ML_V),
                   jax.ShapeDtypeStruct((lay.rows, ML_NG), F32),
                   jax.ShapeDtypeStruct((ML_NG, lay.rows), F32)],
        compiler_params=_params(1),
        name="mlstm_in",
    )(x, gain, mod, mod, w_main, _split_weight_t(w_g), b_g.reshape(ML_NG, 1))


def _mlstm_dir(reverse, q_ref, k_ref, v_ref, g_ref, gt_ref, o_ref, s_ref, m_ref):
    T = ML_T
    row = lax.broadcasted_iota(jnp.int32, (T, T), 0)
    col = lax.broadcasted_iota(jnp.int32, (T, T), 1)
    mask = (col >= row) if reverse else (col <= row)
    g = g_ref[...]
    gt = gt_ref[...]
    tri_c = mask.astype(F32)
    tri_r = ((row >= col) if reverse else (row <= col)).astype(F32)
    bc = jnp.dot(tri_c, g, precision=HI, preferred_element_type=F32)
    br = jnp.dot(gt, tri_r, precision=HI, preferred_element_type=F32)
    gi, gf = (2 * ML_HEADS, 3 * ML_HEADS) if reverse else (0, ML_HEADS)
    lane = lax.broadcasted_iota(jnp.int32, (T, 2 * ML_DQK), 1)
    ones_col = (lax.broadcasted_iota(jnp.int32, (T, ML_DV), 1) == 0).astype(BF16)
    for h in range(ML_HEADS):
        pair = (h // 2) * 2 * ML_DQK
        own = (lane >= ML_DQK) if (h % 2) else (lane < ML_DQK)
        qp = q_ref[:, pair:pair + 2 * ML_DQK]
        kp = k_ref[:, pair:pair + 2 * ML_DQK]
        qm = jnp.where(own, qp, jnp.zeros_like(qp))
        km = jnp.where(own, kp, jnp.zeros_like(kp))
        v_ext = jnp.concatenate([v_ref[:, h * ML_DV:(h + 1) * ML_DV], ones_col], axis=1)
        b_col = bc[:, gf + h:gf + h + 1]
        i_col = g[:, gi + h:gi + h + 1]
        b_row = br[gf + h:gf + h + 1, :]
        i_row = gt[gi + h:gi + h + 1, :]
        tot = b_row[:, 0:1] if reverse else b_row[:, T - 1:T]
        m_prev = m_ref[h][0:1, 0:1]
        s_prev = s_ref[h]
        d = jnp.where(mask, b_col - b_row + i_row, -jnp.inf)
        inter = b_col + m_prev
        m_t = jnp.maximum(inter, jnp.max(d, axis=1, keepdims=True))
        s_raw = lax.dot_general(qm, km, (((1,), (1,)), ((), ())), preferred_element_type=F32)
        p = (s_raw * jnp.exp2(d - m_t)).astype(BF16)
        a = jnp.exp2(inter - m_t)
        r = (jnp.dot(p, v_ext, preferred_element_type=F32)
             + a * jnp.dot(qm, s_prev.astype(BF16), preferred_element_type=F32))
        den = r[:, ML_DV:ML_DV + 1]
        o_ref[:, h * ML_DV:(h + 1) * ML_DV] = (
            r[:, :ML_DV] / jnp.maximum(jnp.abs(den), jnp.exp2(-m_t))).astype(o_ref.dtype)
        g_col = tot - b_col + i_col
        m_new = jnp.maximum(tot + m_prev, jnp.max(g_col, axis=0, keepdims=True))
        decay = jnp.exp2(tot + m_prev - m_new)
        wv = (jnp.exp2(g_col - m_new) * v_ext.astype(F32)).astype(BF16)
        s_ref[h] = decay * s_prev + lax.dot_general(
            km, wv, (((0,), (0,)), ((), ())), preferred_element_type=F32)
        m_ref[h] = jnp.broadcast_to(m_new, m_ref.shape[1:])


def _mlstm_scan_kernel(qf_ref, kf_ref, vf_ref, gf_ref, gtf_ref, qb_ref, kb_ref, vb_ref, gb_ref, gtb_ref,
                       of_ref, ob_ref, sf_ref, mf_ref, sb_ref, mb_ref):
    @pl.when(pl.program_id(1) == 0)
    def _():
        sf_ref[...] = jnp.zeros_like(sf_ref)
        mf_ref[...] = jnp.zeros_like(mf_ref)
        sb_ref[...] = jnp.zeros_like(sb_ref)
        mb_ref[...] = jnp.zeros_like(mb_ref)

    _mlstm_dir(False, qf_ref, kf_ref, vf_ref, gf_ref, gtf_ref, of_ref, sf_ref, mf_ref)
    _mlstm_dir(True, qb_ref, kb_ref, vb_ref, gb_ref, gtb_ref, ob_ref, sb_ref, mb_ref)


def mlstm_scan(lay, q, k, v, g, gt):
    assert lay.ctx_first
    nb = lay.nb
    fwd = lambda b, j: (b * nb + j, 0)
    bwd = lambda b, j: (b * nb + jnp.where(j == 0, 0, nb - j), 0)
    fwd_t = lambda b, j: (0, b * nb + j)
    bwd_t = lambda b, j: (0, b * nb + jnp.where(j == 0, 0, nb - j))

    def specs(im, imt):
        return [pl.BlockSpec((ML_T, ML_QK), im), pl.BlockSpec((ML_T, ML_QK), im),
                pl.BlockSpec((ML_T, ML_V), im), pl.BlockSpec((ML_T, ML_NG), im),
                pl.BlockSpec((ML_NG, ML_T), imt)]

    out_sds = jax.ShapeDtypeStruct((lay.rows, ML_V), F32)
    state = [pltpu.VMEM((ML_HEADS, 2 * ML_DQK, ML_SW), F32),
             pltpu.VMEM((ML_HEADS, 8, LANES), F32)]
    return pl.pallas_call(
        _mlstm_scan_kernel,
        grid=(lay.n_batch, nb),
        in_specs=specs(fwd, fwd_t) + specs(bwd, bwd_t),
        out_specs=[pl.BlockSpec((ML_T, ML_V), fwd), pl.BlockSpec((ML_T, ML_V), bwd)],
        out_shape=[out_sds, out_sds],
        scratch_shapes=state + state,
        compiler_params=_params(2),
        name="mlstm_scan",
    )(q, k, v, g, gt, q, k, v, g, gt)


MLA_QK_PAD = 256
MLA_VW = 2 * MLA_V
MLA_IN_PAD = MLA_Q_LORA + MLA_KV_LORA + LANES
ROPE_HALF = MLA_ROPE // 4


def _mla_in_kernel(x_ref, gain_ref, sc_ref, sh_ref, win_ref, wuq_ref, wukv_ref, qn_ref, kvn_ref,
                   qnn_ref, qnr_ref, knn_ref, knr_ref, cos_ref, sa_ref, sb_ref,
                   q_out, k_out, v_out):
    h = _norm_mod(x_ref[...], gain_ref[...], sc_ref[0], sh_ref[0]).astype(BF16)
    p = jnp.dot(h, win_ref[...], preferred_element_type=F32)
    cq = _rms(p[:, :MLA_Q_LORA]) * qn_ref[...]
    ckv = _rms(p[:, MLA_Q_LORA:MLA_Q_LORA + MLA_KV_LORA]) * kvn_ref[...]
    kr = p[:, MLA_Q_LORA + MLA_KV_LORA:]
    q = jnp.dot(cq.astype(BF16), wuq_ref[...], preferred_element_type=F32)
    kv = jnp.dot(ckv.astype(BF16), wukv_ref[...], preferred_element_type=F32)
    nv = MLA_HEADS * MLA_NOPE
    ones_col = (lax.broadcasted_iota(jnp.int32, (TL, MLA_VW - MLA_V), 1) == 0).astype(BF16)
    cos, sa, sb = cos_ref[...], sa_ref[...], sb_ref[...]

    def rope(xp):
        return (xp * cos + pltpu.roll(xp, LANES - ROPE_HALF, 1) * sa + pltpu.roll(xp, ROPE_HALF, 1) * sb)

    kr = rope(_rms(kr, MLA_ROPE) * knr_ref[...]).astype(BF16)
    for hd in range(MLA_HEADS):
        c0 = hd * MLA_QK_PAD
        qn = _rms(q[:, c0:c0 + MLA_NOPE]) * qnn_ref[...]
        qr = rope(_rms(q[:, c0 + MLA_NOPE:c0 + MLA_QK_PAD], MLA_ROPE) * qnr_ref[...])
        q_out[:, c0:c0 + MLA_NOPE] = (qn * (MLA_SCALE * LOG2E)).astype(BF16)
        q_out[:, c0 + MLA_NOPE:c0 + MLA_QK_PAD] = (qr * (MLA_SCALE * LOG2E)).astype(BF16)
        v0 = nv + hd * MLA_V
        v_out[:, hd * MLA_VW:hd * MLA_VW + MLA_V] = kv[:, v0:v0 + MLA_V].astype(BF16)
        v_out[:, hd * MLA_VW + MLA_V:(hd + 1) * MLA_VW] = ones_col
        kn = _rms(kv[:, hd * MLA_NOPE:(hd + 1) * MLA_NOPE]) * knn_ref[...]
        k_out[:, c0:c0 + MLA_NOPE] = kn.astype(BF16)
        k_out[:, c0 + MLA_NOPE:c0 + MLA_QK_PAD] = kr


def _pad_lanes(g):
    return jnp.pad(g, (0, LANES - g.shape[0])).reshape(1, LANES)


def mla_in(lay, x, gain, mod, w_in, q_norm, kv_norm, w_uq, w_ukv, qn_nope, qn_rope, kn_nope, kn_rope,
           tables):
    D = D_MODEL
    Hn = MLA_HEADS
    win = jnp.pad(w_in, ((0, 0), (0, MLA_IN_PAD - w_in.shape[1]))).astype(BF16)
    wuq = jnp.pad(w_uq.reshape(MLA_Q_LORA, Hn, MLA_NOPE + MLA_ROPE),
                  ((0, 0), (0, 0), (0, MLA_QK_PAD - MLA_NOPE - MLA_ROPE)))
    wuq = wuq.reshape(MLA_Q_LORA, Hn * MLA_QK_PAD).astype(BF16)
    wkv = w_ukv.reshape(MLA_KV_LORA, Hn, MLA_NOPE + MLA_V)
    wukv = jnp.concatenate([wkv[:, :, :MLA_NOPE].reshape(MLA_KV_LORA, Hn * MLA_NOPE),
                            wkv[:, :, MLA_NOPE:].reshape(MLA_KV_LORA, Hn * MLA_V)], axis=1).astype(BF16)
    nb = lay.nb
    tab_spec = pl.BlockSpec((TL, LANES), lambda r: (r % nb, 0))
    bf = lambda w: jax.ShapeDtypeStruct((lay.rows, w), BF16)
    return pl.pallas_call(
        _mla_in_kernel,
        grid=(lay.n_blocks,),
        in_specs=[lay.row_spec(D), _const_spec((1, D)), lay.mod_spec(1), lay.mod_spec(0),
                  _const_spec(win.shape), _const_spec(wuq.shape), _const_spec(wukv.shape),
                  _const_spec((1, MLA_Q_LORA)), _const_spec((1, MLA_KV_LORA)),
                  _const_spec((1, LANES)), _const_spec((1, LANES)), _const_spec((1, LANES)),
                  _const_spec((1, LANES)), tab_spec, tab_spec, tab_spec],
        out_specs=[lay.row_spec(Hn * MLA_QK_PAD), lay.row_spec(Hn * MLA_QK_PAD), lay.row_spec(Hn * MLA_VW)],
        out_shape=[bf(Hn * MLA_QK_PAD), bf(Hn * MLA_QK_PAD), bf(Hn * MLA_VW)],
        compiler_params=_params(1),
        name="mla_in",
    )(x, gain, mod, mod, win, wuq, wukv, q_norm.reshape(1, -1), kv_norm.reshape(1, -1),
      qn_nope.reshape(1, -1), _pad_lanes(qn_rope), kn_nope.reshape(1, -1), _pad_lanes(kn_rope), *tables)


def rope_tables(n_ctx, n_lat):
    n_freq = MLA_ROPE // 4
    inv = ROPE_THETA ** (-jnp.arange(n_freq, dtype=F32) / n_freq)
    t = jnp.arange(n_lat)
    a_r = (t // GRID_W).astype(F32)[:, None] * inv
    a_c = (t % GRID_W).astype(F32)[:, None] * inv
    ang = jnp.concatenate([a_r, a_r, a_c, a_c], axis=-1)
    ang = jnp.concatenate([jnp.zeros((n_ctx, MLA_ROPE), F32), ang], axis=0)
    cos, sin = jnp.cos(ang), jnp.sin(ang)
    low = (jnp.arange(MLA_ROPE) % (2 * ROPE_HALF)) < ROPE_HALF
    pad = lambda a: jnp.pad(a, ((0, 0), (0, LANES - MLA_ROPE)))
    return pad(cos), pad(jnp.where(low, -sin, 0.0)), pad(jnp.where(low, 0.0, sin))


MLA_HPS = 2


def _mla_attn_kernel(q_ref, k_ref, v_ref, o_ref, *, n_ctx):
    def attend(n_keys):
        for hd in range(MLA_HPS):
            q = q_ref[0, :, hd * MLA_QK_PAD:(hd + 1) * MLA_QK_PAD]
            k = k_ref[0, :n_keys, hd * MLA_QK_PAD:(hd + 1) * MLA_QK_PAD]
            v = v_ref[0, :n_keys, hd * MLA_VW:(hd + 1) * MLA_VW]
            s = lax.dot_general(q, k, (((1,), (1,)), ((), ())), preferred_element_type=F32)
            p = jnp.exp2(s - s.max(axis=1, keepdims=True)).astype(BF16)
            r = jnp.dot(p, v, preferred_element_type=F32)
            o_ref[0, :, hd * MLA_V:(hd + 1) * MLA_V] = (
                r[:, :MLA_V] / r[:, MLA_V:MLA_V + 1]).astype(o_ref.dtype)

    @pl.when(pl.program_id(2) == 0)
    def _():
        attend(n_ctx)

    @pl.when(pl.program_id(2) > 0)
    def _():
        attend(k_ref.shape[1])


def mla_attention(lay, q, k, v):
    assert lay.ctx_first
    Bn, S = lay.n_batch, lay.nb * TL
    q3, k3, v3 = (a.reshape(Bn, S, a.shape[-1]) for a in (q, k, v))
    out = pl.pallas_call(
        functools.partial(_mla_attn_kernel, n_ctx=TL),
        grid=(Bn, MLA_HEADS // MLA_HPS, lay.nb),
        in_specs=[pl.BlockSpec((1, TL, MLA_HPS * MLA_QK_PAD), lambda b, h, i: (b, i, h)),
                  pl.BlockSpec((1, S, MLA_HPS * MLA_QK_PAD), lambda b, h, i: (b, 0, h)),
                  pl.BlockSpec((1, S, MLA_HPS * MLA_VW), lambda b, h, i: (b, 0, h))],
        out_specs=pl.BlockSpec((1, TL, MLA_HPS * MLA_V), lambda b, h, i: (b, i, h)),
        out_shape=jax.ShapeDtypeStruct((Bn, S, MLA_HEADS * MLA_V), BF16),
        compiler_params=_params(3),
        name="mla_attention",
    )(q3, k3, v3)
    return out.reshape(lay.rows, MLA_HEADS * MLA_V)


N_PROLOGUE = {"conv": 5, "mlstm": 4, "mla": 1}


def _mixer_out_kernel(*refs, kind, nb, ctx_first):
    n_pro = N_PROLOGUE[kind]
    pro = refs[:n_pro]
    (wout_ref, x_ref, ga_ref, gain_ref, sc_ref, sh_ref, wr_ref, br_ref,
     xo_ref, h2_ref, te_ref, gate_ref, rank_ref, cnt_ref, carry_ref) = refs[n_pro:]
    r = pl.program_id(0)

    if kind == "conv":
        vprev_ref, v_ref, vnext_ref, bg_ref, cw_ref = pro
        j = r % nb
        first = (j == 0) | (j == 1) if ctx_first else (j == 0)
        last = (j == nb - 1) | (j == 0) if ctx_first else (j == nb - 1)
        v = v_ref[...].astype(F32)
        rows = lax.broadcasted_iota(jnp.int32, (TL, 1), 0)
        prev_row = jnp.where(first, 0.0, vprev_ref[BF16_ROWS - 1:BF16_ROWS, :].astype(F32))
        next_row = jnp.where(last, 0.0, vnext_ref[0:1, :].astype(F32))
        up = jnp.where(rows == 0, prev_row, pltpu.roll(v, 1, 0))
        dn = jnp.where(rows == TL - 1, next_row, pltpu.roll(v, TL - 1, 0))
        cw = cw_ref[...]
        a = bg_ref[...].astype(F32) * (up * cw[0:1] + v * cw[1:2] + dn * cw[2:3])
    elif kind == "mlstm":
        hf_ref, hb_ref, og_ref, ng_ref = pro
        hh = hf_ref[...] + hb_ref[...]
        a = jnp.concatenate([_rms(hh[:, h * ML_DV:(h + 1) * ML_DV]) for h in range(ML_HEADS)], axis=1)
        a = a * ng_ref[...] * og_ref[...].astype(F32)
    else:
        a = pro[0][...]

    y = jnp.dot(a.astype(BF16), wout_ref[...], preferred_element_type=F32)
    xn = x_ref[...] + ga_ref[0] * y
    xo_ref[...] = xn
    h2 = _norm_mod(xn, gain_ref[...], sc_ref[0], sh_ref[0])
    h2_ref[...] = _pack_rows(h2)
    logits = _split_dot_t(wr_ref[...], h2) + br_ref[...]

    sub = lax.broadcasted_iota(jnp.int32, (N_EXPERTS, TL), 0)
    sub_k = lax.broadcasted_iota(jnp.int32, (TOP_K, TL), 0)
    work = logits
    sel = jnp.zeros((N_EXPERTS, TL), F32)
    top_e = jnp.zeros((TOP_K, TL), jnp.int32)
    top_v = jnp.zeros((TOP_K, TL), F32)
    picks = []
    for kk in range(TOP_K):
        m = work.max(axis=0, keepdims=True)
        idx = jnp.min(jnp.where(work == m, sub, N_EXPERTS), axis=0, keepdims=True)
        hit = sub == idx
        picks.append(hit)
        sel = jnp.where(hit, 1.0, sel)
        work = jnp.where(hit, -jnp.inf, work)
        top_e = jnp.where(sub_k == kk, idx, top_e)
        top_v = jnp.where(sub_k == kk, m, top_v)
    ex = jnp.exp(top_v - top_v[0:1])
    gate_ref[...] = ex / ex.sum(axis=0, keepdims=True)
    te_ref[...] = top_e

    @pl.when(r == 0)
    def _():
        carry_ref[...] = jnp.zeros_like(carry_ref)

    tr = lax.broadcasted_iota(jnp.int32, (TL, TL), 0)
    tc = lax.broadcasted_iota(jnp.int32, (TL, TL), 1)
    before = jnp.dot(sel.astype(BF16), (tr < tc).astype(BF16), preferred_element_type=F32)
    pos = before + carry_ref[...]
    rank = jnp.zeros((TOP_K, TL), F32)
    for kk in range(TOP_K):
        rk = jnp.sum(jnp.where(picks[kk], pos, 0.0), axis=0, keepdims=True)
        rank = jnp.where(sub_k == kk, rk, rank)
    rank_ref[...] = rank.astype(jnp.int32)
    total = carry_ref[...] + jnp.sum(sel, axis=1, keepdims=True)
    carry_ref[...] = total
    cnt_ref[...] = total


def mixer_out(lay, kind, pro_args, w_out, x, mod, gain_f, w_r, b_r):
    D = D_MODEL
    nb = lay.nb
    if kind == "conv":
        v, bg, cw = pro_args
        per = TL // BF16_ROWS
        last_tile = lay.rows // BF16_ROWS - 1
        pro_specs = [pl.BlockSpec((BF16_ROWS, D), lambda r: (jnp.maximum(r * per - 1, 0), 0)),
                     lay.row_spec(D),
                     pl.BlockSpec((BF16_ROWS, D), lambda r: (jnp.minimum((r + 1) * per, last_tile), 0)),
                     lay.row_spec(D), _const_spec((CONV_WIDTH, D))]
        pro_in = [v, v, v, bg, cw]
    elif kind == "mlstm":
        h_f, h_b, og, ng = pro_args
        pro_specs = [lay.row_spec(ML_V), lay.row_spec(ML_V), lay.row_spec(ML_V), _const_spec((1, ML_V))]
        pro_in = [h_f, h_b, og, ng.reshape(1, ML_V)]
    else:
        pro_specs = [lay.row_spec(D)]
        pro_in = list(pro_args)
    k_in = w_out.shape[0]
    small = lambda dt: jax.ShapeDtypeStruct((TOP_K, lay.rows), dt)
    small_spec = pl.BlockSpec((TOP_K, TL), lambda r: (0, r))
    return pl.pallas_call(
        functools.partial(_mixer_out_kernel, kind=kind, nb=nb, ctx_first=lay.ctx_first),
        grid=(lay.n_blocks,),
        in_specs=pro_specs + [_const_spec((k_in, D)), lay.row_spec(D), lay.mod_spec(2),
                              _const_spec((1, D)), lay.mod_spec(4), lay.mod_spec(3),
                              _const_spec((2 * N_EXPERTS, D)), _const_spec((N_EXPERTS, 1))],
        out_specs=[lay.row_spec(D), lay.row_spec(PACK_W), small_spec, small_spec, small_spec,
                   _const_spec((N_EXPERTS, 1))],
        out_shape=[jax.ShapeDtypeStruct((lay.rows, D), F32),
                   jax.ShapeDtypeStruct((lay.rows, PACK_W), jnp.int32),
                   small(jnp.int32), small(F32), small(jnp.int32),
                   jax.ShapeDtypeStruct((N_EXPERTS, 1), F32)],
        scratch_shapes=[pltpu.VMEM((N_EXPERTS, 1), F32)],
        compiler_params=_params(1),
        name="mixer_out_" + kind,
    )(*pro_in, w_out.astype(BF16), x, mod, gain_f, mod, mod, _split_weight_t(w_r),
      b_r.reshape(N_EXPERTS, 1))


def _expert_ffn_kernel(blk_e_ref, first_ref, x_ref, w1_ref, b1_ref, w2_ref, b2_ref, o_ref,
                       w1b_ref, w2b_ref):
    del blk_e_ref

    @pl.when(first_ref[pl.program_id(0)] == 1)
    def _():
        w1b_ref[...] = w1_ref[0, 0].astype(BF16)
        w2b_ref[...] = w2_ref[0, 0].astype(BF16)

    x = _unpack_rows(x_ref[...], BF16)
    h = jnp.dot(x, w1b_ref[...], preferred_element_type=F32) + b1_ref[0, 0]
    glu = jnp.minimum(h[:, :MOE_FF], SWIGLU_LIMIT)
    lin = jnp.clip(h[:, MOE_FF:], -SWIGLU_LIMIT, SWIGLU_LIMIT)
    act = glu * jax.nn.sigmoid(SWIGLU_ALPHA * glu) * (lin + 1.0)
    y = jnp.dot(act.astype(BF16), w2b_ref[...], preferred_element_type=F32)
    o_ref[...] = _pack_rows(y + b2_ref[0, 0])


def expert_ffn(layer, xp, blk_e, blk_first, w1, b1, w2, b2):
    n_rows = xp.shape[0]
    D, F2 = D_MODEL, 2 * MOE_FF
    n_blk = n_rows // MOE_BLOCK
    grid_spec = pltpu.PrefetchScalarGridSpec(
        num_scalar_prefetch=2,
        grid=(n_blk,),
        in_specs=[
            pl.BlockSpec((MOE_BLOCK, PACK_W), lambda i, be, fi: (i, 0)),
            pl.BlockSpec((1, 1, D, F2), lambda i, be, fi: (layer, be[i], 0, 0)),
            pl.BlockSpec((1, 1, 1, F2), lambda i, be, fi: (layer, be[i], 0, 0)),
            pl.BlockSpec((1, 1, MOE_FF, D), lambda i, be, fi: (layer, be[i], 0, 0)),
            pl.BlockSpec((1, 1, 1, D), lambda i, be, fi: (layer, be[i], 0, 0)),
        ],
        out_specs=pl.BlockSpec((MOE_BLOCK, PACK_W), lambda i, be, fi: (i, 0)),
        scratch_shapes=[pltpu.VMEM((D, F2), BF16), pltpu.VMEM((MOE_FF, D), BF16)],
    )
    return pl.pallas_call(
        _expert_ffn_kernel,
        grid_spec=grid_spec,
        out_shape=jax.ShapeDtypeStruct((n_rows, PACK_W), jnp.int32),
        compiler_params=_params(1),
        name="expert_ffn",
    )(blk_e, blk_first, xp, w1, b1, w2, b2)


SC_CORES = 2
SC_SUBCORES = 16
SC_CHUNK = 64


def sc_gather(table, idx):
    n_idx = idx.shape[0]
    width = table.shape[1]
    n_workers = SC_CORES * SC_SUBCORES
    per_worker = n_idx // n_workers
    n_chunks = per_worker // SC_CHUNK
    assert n_chunks * SC_CHUNK * n_workers == n_idx
    mesh = plsc.VectorSubcoreMesh(core_axis_name="c", subcore_axis_name="s",
                                  num_cores=SC_CORES, num_subcores=SC_SUBCORES)

    def body(table_hbm, idx_hbm, out_hbm, idx_v, rows_v, sem):
        wid = lax.axis_index("s") * SC_CORES + lax.axis_index("c")
        pltpu.sync_copy(idx_hbm.at[wid], idx_v)

        @pl.loop(0, n_chunks)
        def _(ci):
            pltpu.async_copy(table_hbm.at[idx_v.at[ci]], rows_v, sem).wait()
            pltpu.sync_copy(rows_v, out_hbm.at[wid, ci])

    out = pl.kernel(
        body,
        out_type=jax.ShapeDtypeStruct((n_workers, n_chunks, SC_CHUNK, width), table.dtype),
        mesh=mesh,
        scratch_types=[pltpu.VMEM((n_chunks, SC_CHUNK), jnp.int32),
                       pltpu.VMEM((SC_CHUNK, width), table.dtype),
                       pltpu.SemaphoreType.DMA],
        name="sc_gather",
    )(table, idx.reshape(n_workers, n_chunks, SC_CHUNK))
    return out.reshape(n_idx, width)


def _combine_kernel(x_ref, *refs):
    y_refs, (gate_ref, gf_ref, o_ref) = refs[:TOP_K], refs[TOP_K:]
    gates = gate_ref[...]
    acc = gates[:, 0:1] * _unpack_rows(y_refs[0][...], F32)
    for kk in range(1, TOP_K):
        acc = acc + gates[:, kk:kk + 1] * _unpack_rows(y_refs[kk][...], F32)
    o_ref[...] = x_ref[...] + gf_ref[0] * acc


def moe_combine(lay, x, yg, gates, mod, drop_ctx):
    D = D_MODEL
    if drop_ctx:
        nbo = lay.nb - 1
        src = lambda r: (r // nbo) * lay.nb + 1 + r % nbo
        n_out = lay.n_batch * nbo
    else:
        src = lambda r: r
        n_out = lay.n_blocks
    y_specs = [pl.BlockSpec((TL, PACK_W), functools.partial(lambda kk, r: (kk * lay.n_blocks + src(r), 0), kk))
               for kk in range(TOP_K)]
    return pl.pallas_call(
        _combine_kernel,
        grid=(n_out,),
        in_specs=[pl.BlockSpec((TL, D), lambda r: (src(r), 0))] + y_specs + [
            pl.BlockSpec((TL, TOP_K), lambda r: (src(r), 0)),
            pl.BlockSpec((1, 1, D), lambda r: (lay.mod_row(src(r)), 0, 5))],
        out_specs=pl.BlockSpec((TL, D), lambda r: (r, 0)),
        out_shape=jax.ShapeDtypeStruct((n_out * TL, D), F32),
        compiler_params=_params(1),
        name="moe_combine",
    )(x, yg, yg, yg, yg, gates, mod)


def moe_route(top_e, rank, counts):
    T = top_e.shape[1]
    counts = counts.reshape(N_EXPERTS).astype(jnp.int32)
    group_start = jnp.cumsum(counts) - counts
    padded = (counts + MOE_BLOCK - 1) // MOE_BLOCK * MOE_BLOCK
    padded_end = jnp.cumsum(padded)
    padded_start = padded_end - padded
    dest = padded_start[top_e] + rank
    n_rows = -(-(T * TOP_K) // MOE_BLOCK) * MOE_BLOCK + N_EXPERTS * MOE_BLOCK
    n_blk = n_rows // MOE_BLOCK
    blk_start = jnp.arange(n_blk) * MOE_BLOCK
    blk_e = jnp.minimum(jnp.sum(padded_end[None, :] <= blk_start[:, None], axis=1), N_EXPERTS - 1)
    order = jnp.argsort(top_e.T.reshape(-1))
    row_e = jnp.repeat(blk_e, MOE_BLOCK)
    local = jnp.arange(n_rows) - padded_start[row_e]
    src_flat = order[jnp.clip(group_start[row_e] + local, 0, T * TOP_K - 1)]
    blk_e = blk_e.astype(jnp.int32)
    blk_first = jnp.concatenate([jnp.ones((1,), jnp.int32), (blk_e[1:] != blk_e[:-1]).astype(jnp.int32)])
    return (src_flat // TOP_K).astype(jnp.int32), dest.reshape(-1).astype(jnp.int32), blk_e, blk_first


def kernel(x, c, ctx, c_ctx, norm_mix, norm_ffn, w_mod, b_mod, conv_w_in, conv_w, conv_w_out, ml_w_in, ml_b_gate, ml_norm, ml_w_out, mla_w_in, mla_q_norm, mla_kv_norm, mla_w_uq, mla_w_ukv, mla_qn_nope, mla_qn_rope, mla_kn_nope, mla_kn_rope, mla_w_out, moe_w_router, moe_b_router, moe_w1, moe_b1, moe_w2, moe_b2):
    Bn, n_lat, D = x.shape
    n_ctx = ctx.shape[1]
    assert D == D_MODEL and n_ctx == TL and n_lat % TL == 0
    assert (DEPTH - 1) % N_MIXERS == 0
    full = Layout(Bn, (n_ctx + n_lat) // TL, True)
    lat_only = Layout(Bn, n_lat // TL, False)
    mods = ada_all(c, c_ctx, w_mod, b_mod)
    tables = rope_tables(n_ctx, n_lat)
    b1_all = moe_b1.reshape(DEPTH, N_EXPERTS, 1, 2 * MOE_FF)
    b2_all = moe_b2.reshape(DEPTH, N_EXPERTS, 1, D)
    X = jnp.concatenate([ctx, x], axis=1).reshape(full.rows, D)
    for layer in range(DEPTH):
        kind, j = layer % N_MIXERS, layer // N_MIXERS
        last = layer == DEPTH - 1
        lay = lat_only if last else full
        mod = mods[layer]
        gain_a = norm_mix[layer].reshape(1, D)
        gain_f = norm_ffn[layer].reshape(1, D)
        if kind == 0:
            bg, v = conv_in(lay, X, gain_a, mod, conv_w_in[j].astype(BF16))
            pro, w_out, name = (v, bg, conv_w[j]), conv_w_out[j], "conv"
        elif kind == 1:
            w = ml_w_in[j]
            n_main = 2 * ML_QK + 2 * ML_V
            w_main = jnp.concatenate([w[:, :ML_QK] * ML_DQK ** -0.5, w[:, ML_QK:n_main]], axis=1)
            q, k, v, og, g, gt = mlstm_in(lay, X, gain_a, mod, w_main.astype(BF16), w[:, n_main:],
                                          ml_b_gate[j])
            h_f, h_b = mlstm_scan(lay, q, k, v, g, gt)
            pro, w_out, name = (h_f, h_b, og, ml_norm[j]), ml_w_out[j], "mlstm"
        else:
            q, k, v = mla_in(lay, X, gain_a, mod, mla_w_in[j], mla_q_norm[j], mla_kv_norm[j],
                             mla_w_uq[j], mla_w_ukv[j], mla_qn_nope[j], mla_qn_rope[j],
                             mla_kn_nope[j], mla_kn_rope[j], tables)
            pro, w_out, name = (mla_attention(lay, q, k, v),), mla_w_out[j], "mla"
        X, h2, top_e, gates, rank, counts = mixer_out(
            lay, name, pro, w_out, X, mod, gain_f, moe_w_router[layer], moe_b_router[layer])
        src_tok, dest, blk_e, blk_first = moe_route(top_e, rank, counts)
        xp = sc_gather(h2, src_tok)
        yp = expert_ffn(layer, xp, blk_e, blk_first, moe_w1, b1_all, moe_w2, b2_all)
        yg = sc_gather(yp, dest)
        X = moe_combine(lay, X, yg, gates.T, mod, drop_ctx=(layer == DEPTH - 2))
    return X.reshape(Bn, n_lat, D)
```

```python
import functools

import jax
import jax.numpy as jnp
from jax import lax
from jax.experimental import pallas as pl
from jax.experimental.pallas import tpu as pltpu
from jax.experimental.pallas import tpu_sc as plsc

D_MODEL = 1024
DEPTH = 4
GRID_W = 64
N_MIXERS = 3
N_ADA = 6
RMS_EPS = 1e-6
CONV_WIDTH = 3
ML_HEADS = 8
ML_DQK = 64
ML_DV = 128
ML_QK = ML_HEADS * ML_DQK
ML_V = ML_HEADS * ML_DV
GATE_CAP = 15.0
MLA_HEADS = 8
MLA_NOPE = 128
MLA_ROPE = 64
MLA_V = 128
MLA_Q_LORA = 384
MLA_KV_LORA = 256
MLA_SCALE = (MLA_NOPE + MLA_ROPE) ** -0.5
ROPE_THETA = 10000.0
N_EXPERTS = 32
TOP_K = 4
MOE_FF = D_MODEL
SWIGLU_ALPHA = 1.702
SWIGLU_LIMIT = 7.0
MOE_BLOCK = 512

TL = 256
LANES = 128
BF16_ROWS = 16
VMEM_LIMIT = 48 * 1024 * 1024
HI = lax.Precision.HIGHEST
F32 = jnp.float32
BF16 = jnp.bfloat16


def _params(n_axes):
    return pltpu.CompilerParams(dimension_semantics=("arbitrary",) * n_axes,
                                vmem_limit_bytes=VMEM_LIMIT)


def _rms(x, width=None):
    width = x.shape[-1] if width is None else width
    return x * lax.rsqrt(jnp.sum(x * x, axis=-1, keepdims=True) * (1.0 / width) + RMS_EPS)


def _norm_mod(x, gain, scale, shift):
    return _rms(x) * gain * (1.0 + scale) + shift


def _split_weight_t(w):
    hi = w.astype(BF16)
    lo = (w - hi.astype(F32)).astype(BF16)
    return jnp.concatenate([hi.T, lo.T], axis=0)


def _split_dot_t(w2, h):
    n = w2.shape[0] // 2
    dn = (((1,), (1,)), ((), ()))
    h_hi = h.astype(BF16)
    h_lo = (h - h_hi.astype(F32)).astype(BF16)
    both = lax.dot_general(w2, h_hi, dn, preferred_element_type=F32)
    cross = lax.dot_general(w2[:n], h_lo, dn, preferred_element_type=F32)
    return both[:n] + both[n:] + cross


PACK_W = D_MODEL // 2
HIGH_HALF = -65536


def _pack_rows(x):
    xb = x.astype(BF16).astype(F32)
    lo = lax.bitcast_convert_type(xb[:, :PACK_W], jnp.int32)
    hi = lax.bitcast_convert_type(xb[:, PACK_W:], jnp.int32)
    return hi | lax.shift_right_logical(lo, 16)


def _unpack_rows(w, dtype):
    lo = lax.bitcast_convert_type(lax.shift_left(w, 16), F32)
    hi = lax.bitcast_convert_type(w & HIGH_HALF, F32)
    return jnp.concatenate([lo.astype(dtype), hi.astype(dtype)], axis=1)


class Layout:
    def __init__(self, n_batch, nb, ctx_first):
        self.n_batch, self.nb, self.ctx_first = n_batch, nb, ctx_first
        self.n_blocks = n_batch * nb
        self.rows = self.n_blocks * TL

    def mod_row(self, r):
        b = r // self.nb
        return jnp.where(r % self.nb == 0, self.n_batch, b) if self.ctx_first else b

    def row_spec(self, width):
        return pl.BlockSpec((TL, width), lambda r: (r, 0))

    def mod_spec(self, piece):
        return pl.BlockSpec((1, 1, D_MODEL), lambda r: (self.mod_row(r), 0, piece))


def _const_spec(shape):
    return pl.BlockSpec(shape, lambda *_: (0,) * len(shape))


ADA_ROWS = 16
ADA_TN = 1536


def _ada_kernel(c_ref, w_ref, b_ref, o_ref):
    c = c_ref[...]
    s = c * jax.nn.sigmoid(c)
    o_ref[0] = jnp.dot(s, w_ref[0], precision=HI, preferred_element_type=F32) + b_ref[0]


def ada_all(c, c_ctx, w_mod, b_mod):
    Bn, D = c.shape
    cond = jnp.zeros((ADA_ROWS, D), F32).at[:Bn].set(c).at[Bn].set(c_ctx)
    out = pl.pallas_call(
        _ada_kernel,
        grid=(DEPTH, N_ADA * D // ADA_TN),
        in_specs=[pl.BlockSpec((ADA_ROWS, D), lambda l, n: (0, 0)),
                  pl.BlockSpec((1, D, ADA_TN), lambda l, n: (l, 0, n)),
                  pl.BlockSpec((1, 1, ADA_TN), lambda l, n: (l, 0, n))],
        out_specs=pl.BlockSpec((1, ADA_ROWS, ADA_TN), lambda l, n: (l, 0, n)),
        out_shape=jax.ShapeDtypeStruct((DEPTH, ADA_ROWS, N_ADA * D), F32),
        compiler_params=_params(2),
        name="ada_mod",
    )(cond, w_mod, b_mod.reshape(DEPTH, 1, N_ADA * D))
    return out[:, :Bn + 1, None, :]


def _conv_in_kernel(x_ref, gain_ref, sc_ref, sh_ref, w_ref, bg_ref, v_ref):
    D = D_MODEL
    h = _norm_mod(x_ref[...], gain_ref[...], sc_ref[0], sh_ref[0]).astype(BF16)
    p = jnp.dot(h, w_ref[...], preferred_element_type=F32)
    bg_ref[...] = p[:, :D].astype(BF16)
    v_ref[...] = (p[:, D:2 * D] * p[:, 2 * D:]).astype(BF16)


def conv_in(lay, x, gain, mod, w_in):
    D = D_MODEL
    sds = jax.ShapeDtypeStruct((lay.rows, D), BF16)
    return pl.pallas_call(
        _conv_in_kernel,
        grid=(lay.n_blocks,),
        in_specs=[lay.row_spec(D), _const_spec((1, D)), lay.mod_spec(1), lay.mod_spec(0),
                  _const_spec((D, 3 * D))],
        out_specs=[lay.row_spec(D), lay.row_spec(D)],
        out_shape=[sds, sds],
        compiler_params=_params(1),
        name="conv_in",
    )(x, gain, mod, mod, w_in)


ML_T = TL
ML_SW = 2 * ML_DV
ML_NG = 4 * ML_HEADS


LOG2E = 1.4426950408889634


def _gate_act(g, is_forget):
    g = GATE_CAP * jnp.tanh(g * (1.0 / GATE_CAP))
    log_sig = jnp.minimum(g, 0.0) - jnp.log(1.0 + jnp.exp(-jnp.abs(g)))
    return jnp.where(is_forget, log_sig, g) * LOG2E


def _mlstm_in_kernel(x_ref, gain_ref, sc_ref, sh_ref, w_ref, wg_ref, bgt_ref,
                     q_ref, k_ref, v_ref, og_ref, g_ref, gt_ref):
    h = _norm_mod(x_ref[...], gain_ref[...], sc_ref[0], sh_ref[0])
    p = jnp.dot(h.astype(BF16), w_ref[...], preferred_element_type=F32)
    q_ref[...] = p[:, :ML_QK].astype(BF16)
    k_ref[...] = p[:, ML_QK:2 * ML_QK].astype(BF16)
    v_ref[...] = p[:, 2 * ML_QK:2 * ML_QK + ML_V].astype(BF16)
    og_ref[...] = jax.nn.sigmoid(p[:, 2 * ML_QK + ML_V:]).astype(BF16)
    gt = _split_dot_t(wg_ref[...], h) + bgt_ref[...]
    row = lax.broadcasted_iota(jnp.int32, gt.shape, 0)
    gt = _gate_act(gt, (row // ML_HEADS) % 2 == 1)
    gt_ref[...] = gt
    eye = (lax.broadcasted_iota(jnp.int32, (TL, TL), 0)
           == lax.broadcasted_iota(jnp.int32, (TL, TL), 1)).astype(F32)
    g_ref[...] = lax.dot_general(eye, gt, (((1,), (1,)), ((), ())), precision=HI,
                                 preferred_element_type=F32)


def mlstm_in(lay, x, gain, mod, w_main, w_g, b_g):
    D = D_MODEL
    n_main = 2 * ML_QK + 2 * ML_V
    bf = lambda w: jax.ShapeDtypeStruct((lay.rows, w), BF16)
    return pl.pallas_call(
        _mlstm_in_kernel,
        grid=(lay.n_blocks,),
        in_specs=[lay.row_spec(D), _const_spec((1, D)), lay.mod_spec(1), lay.mod_spec(0),
                  _const_spec((D, n_main)), _const_spec((2 * ML_NG, D)), _const_spec((ML_NG, 1))],
        out_specs=[lay.row_spec(ML_QK), lay.row_spec(ML_QK), lay.row_spec(ML_V), lay.row_spec(ML_V),
                   lay.row_spec(ML_NG), pl.BlockSpec((ML_NG, TL), lambda r: (0, r))],
        out_shape=[bf(ML_QK), bf(ML_QK), bf(ML_V), bf(ML_V),
                   jax.ShapeDtypeStruct((lay.rows, ML_NG), F32),
                   jax.ShapeDtypeStruct((ML_NG, lay.rows), F32)],
        compiler_params=_params(1),
        name="mlstm_in",
    )(x, gain, mod, mod, w_main, _split_weight_t(w_g), b_g.reshape(ML_NG, 1))


def _mlstm_dir(reverse, q_ref, k_ref, v_ref, g_ref, gt_ref, o_ref, s_ref, m_ref):
    T = ML_T
    row = lax.broadcasted_iota(jnp.int32, (T, T), 0)
    col = lax.broadcasted_iota(jnp.int32, (T, T), 1)
    mask = (col >= row) if reverse else (col <= row)
    g = g_ref[...]
    gt = gt_ref[...]
    tri_c = mask.astype(F32)
    tri_r = ((row >= col) if reverse else (row <= col)).astype(F32)
    bc = jnp.dot(tri_c, g, precision=HI, preferred_element_type=F32)
    br = jnp.dot(gt, tri_r, precision=HI, preferred_element_type=F32)
    gi, gf = (2 * ML_HEADS, 3 * ML_HEADS) if reverse else (0, ML_HEADS)
    lane = lax.broadcasted_iota(jnp.int32, (T, 2 * ML_DQK), 1)
    ones_col = (lax.broadcasted_iota(jnp.int32, (T, ML_DV), 1) == 0).astype(BF16)
    for h in range(ML_HEADS):
        pair = (h // 2) * 2 * ML_DQK
        own = (lane >= ML_DQK) if (h % 2) else (lane < ML_DQK)
        qp = q_ref[:, pair:pair + 2 * ML_DQK]
        kp = k_ref[:, pair:pair + 2 * ML_DQK]
        qm = jnp.where(own, qp, jnp.zeros_like(qp))
        km = jnp.where(own, kp, jnp.zeros_like(kp))
        v_ext = jnp.concatenate([v_ref[:, h * ML_DV:(h + 1) * ML_DV], ones_col], axis=1)
        b_col = bc[:, gf + h:gf + h + 1]
        i_col = g[:, gi + h:gi + h + 1]
        b_row = br[gf + h:gf + h + 1, :]
        i_row = gt[gi + h:gi + h + 1, :]
        tot = b_row[:, 0:1] if reverse else b_row[:, T - 1:T]
        m_prev = m_ref[h][0:1, 0:1]
        s_prev = s_ref[h]
        d = jnp.where(mask, b_col - b_row + i_row, -jnp.inf)
        inter = b_col + m_prev
        m_t = jnp.maximum(inter, jnp.max(d, axis=1, keepdims=True))
        s_raw = lax.dot_general(qm, km, (((1,), (1,)), ((), ())), preferred_element_type=F32)
        p = (s_raw * jnp.exp2(d - m_t)).astype(BF16)
        a = jnp.exp2(inter - m_t)
        r = (jnp.dot(p, v_ext, preferred_element_type=F32)
             + a * jnp.dot(qm, s_prev.astype(BF16), preferred_element_type=F32))
        den = r[:, ML_DV:ML_DV + 1]
        o_ref[:, h * ML_DV:(h + 1) * ML_DV] = (
            r[:, :ML_DV] / jnp.maximum(jnp.abs(den), jnp.exp2(-m_t))).astype(o_ref.dtype)
        g_col = tot - b_col + i_col
        m_new = jnp.maximum(tot + m_prev, jnp.max(g_col, axis=0, keepdims=True))
        decay = jnp.exp2(tot + m_prev - m_new)
        wv = (jnp.exp2(g_col - m_new) * v_ext.astype(F32)).astype(BF16)
        s_ref[h] = decay * s_prev + lax.dot_general(
            km, wv, (((0,), (0,)), ((), ())), preferred_element_type=F32)
        m_ref[h] = jnp.broadcast_to(m_new, m_ref.shape[1:])


def _mlstm_scan_kernel(qf_ref, kf_ref, vf_ref, gf_ref, gtf_ref, qb_ref, kb_ref, vb_ref, gb_ref, gtb_ref,
                       of_ref, ob_ref, sf_ref, mf_ref, sb_ref, mb_ref):
    @pl.when(pl.program_id(1) == 0)
    def _():
        sf_ref[...] = jnp.zeros_like(sf_ref)
        mf_ref[...] = jnp.zeros_like(mf_ref)
        sb_ref[...] = jnp.zeros_like(sb_ref)
        mb_ref[...] = jnp.zeros_like(mb_ref)

    _mlstm_dir(False, qf_ref, kf_ref, vf_ref, gf_ref, gtf_ref, of_ref, sf_ref, mf_ref)
    _mlstm_dir(True, qb_ref, kb_ref, vb_ref, gb_ref, gtb_ref, ob_ref, sb_ref, mb_ref)


def mlstm_scan(lay, q, k, v, g, gt):
    assert lay.ctx_first
    nb = lay.nb
    fwd = lambda b, j: (b * nb + j, 0)
    bwd = lambda b, j: (b * nb + jnp.where(j == 0, 0, nb - j), 0)
    fwd_t = lambda b, j: (0, b * nb + j)
    bwd_t = lambda b, j: (0, b * nb + jnp.where(j == 0, 0, nb - j))

    def specs(im, imt):
        return [pl.BlockSpec((ML_T, ML_QK), im), pl.BlockSpec((ML_T, ML_QK), im),
                pl.BlockSpec((ML_T, ML_V), im), pl.BlockSpec((ML_T, ML_NG), im),
                pl.BlockSpec((ML_NG, ML_T), imt)]

    out_sds = jax.ShapeDtypeStruct((lay.rows, ML_V), F32)
    state = [pltpu.VMEM((ML_HEADS, 2 * ML_DQK, ML_SW), F32),
             pltpu.VMEM((ML_HEADS, 8, LANES), F32)]
    return pl.pallas_call(
        _mlstm_scan_kernel,
        grid=(lay.n_batch, nb),
        in_specs=specs(fwd, fwd_t) + specs(bwd, bwd_t),
        out_specs=[pl.BlockSpec((ML_T, ML_V), fwd), pl.BlockSpec((ML_T, ML_V), bwd)],
        out_shape=[out_sds, out_sds],
        scratch_shapes=state + state,
        compiler_params=_params(2),
        name="mlstm_scan",
    )(q, k, v, g, gt, q, k, v, g, gt)


MLA_QK_PAD = 256
MLA_VW = 2 * MLA_V
MLA_IN_PAD = MLA_Q_LORA + MLA_KV_LORA + LANES
ROPE_HALF = MLA_ROPE // 4


def _mla_in_kernel(x_ref, gain_ref, sc_ref, sh_ref, win_ref, wuq_ref, wukv_ref, qn_ref, kvn_ref,
                   qnn_ref, qnr_ref, knn_ref, knr_ref, cos_ref, sa_ref, sb_ref,
                   q_out, k_out, v_out):
    h = _norm_mod(x_ref[...], gain_ref[...], sc_ref[0], sh_ref[0]).astype(BF16)
    p = jnp.dot(h, win_ref[...], preferred_element_type=F32)
    cq = _rms(p[:, :MLA_Q_LORA]) * qn_ref[...]
    ckv = _rms(p[:, MLA_Q_LORA:MLA_Q_LORA + MLA_KV_LORA]) * kvn_ref[...]
    kr = p[:, MLA_Q_LORA + MLA_KV_LORA:]
    q = jnp.dot(cq.astype(BF16), wuq_ref[...], preferred_element_type=F32)
    kv = jnp.dot(ckv.astype(BF16), wukv_ref[...], preferred_element_type=F32)
    nv = MLA_HEADS * MLA_NOPE
    ones_col = (lax.broadcasted_iota(jnp.int32, (TL, MLA_VW - MLA_V), 1) == 0).astype(BF16)
    cos, sa, sb = cos_ref[...], sa_ref[...], sb_ref[...]

    def rope(xp):
        return (xp * cos + pltpu.roll(xp, LANES - ROPE_HALF, 1) * sa + pltpu.roll(xp, ROPE_HALF, 1) * sb)

    kr = rope(_rms(kr, MLA_ROPE) * knr_ref[...]).astype(BF16)
    for hd in range(MLA_HEADS):
        c0 = hd * MLA_QK_PAD
        qn = _rms(q[:, c0:c0 + MLA_NOPE]) * qnn_ref[...]
        qr = rope(_rms(q[:, c0 + MLA_NOPE:c0 + MLA_QK_PAD], MLA_ROPE) * qnr_ref[...])
        q_out[:, c0:c0 + MLA_NOPE] = (qn * (MLA_SCALE * LOG2E)).astype(BF16)
        q_out[:, c0 + MLA_NOPE:c0 + MLA_QK_PAD] = (qr * (MLA_SCALE * LOG2E)).astype(BF16)
        v0 = nv + hd * MLA_V
        v_out[:, hd * MLA_VW:hd * MLA_VW + MLA_V] = kv[:, v0:v0 + MLA_V].astype(BF16)
        v_out[:, hd * MLA_VW + MLA_V:(hd + 1) * MLA_VW] = ones_col
        kn = _rms(kv[:, hd * MLA_NOPE:(hd + 1) * MLA_NOPE]) * knn_ref[...]
        k_out[:, c0:c0 + MLA_NOPE] = kn.astype(BF16)
        k_out[:, c0 + MLA_NOPE:c0 + MLA_QK_PAD] = kr


def _pad_lanes(g):
    return jnp.pad(g, (0, LANES - g.shape[0])).reshape(1, LANES)


def mla_in(lay, x, gain, mod, w_in, q_norm, kv_norm, w_uq, w_ukv, qn_nope, qn_rope, kn_nope, kn_rope,
           tables):
    D = D_MODEL
    Hn = MLA_HEADS
    win = jnp.pad(w_in, ((0, 0), (0, MLA_IN_PAD - w_in.shape[1]))).astype(BF16)
    wuq = jnp.pad(w_uq.reshape(MLA_Q_LORA, Hn, MLA_NOPE + MLA_ROPE),
                  ((0, 0), (0, 0), (0, MLA_QK_PAD - MLA_NOPE - MLA_ROPE)))
    wuq = wuq.reshape(MLA_Q_LORA, Hn * MLA_QK_PAD).astype(BF16)
    wkv = w_ukv.reshape(MLA_KV_LORA, Hn, MLA_NOPE + MLA_V)
    wukv = jnp.concatenate([wkv[:, :, :MLA_NOPE].reshape(MLA_KV_LORA, Hn * MLA_NOPE),
                            wkv[:, :, MLA_NOPE:].reshape(MLA_KV_LORA, Hn * MLA_V)], axis=1).astype(BF16)
    nb = lay.nb
    tab_spec = pl.BlockSpec((TL, LANES), lambda r: (r % nb, 0))
    bf = lambda w: jax.ShapeDtypeStruct((lay.rows, w), BF16)
    return pl.pallas_call(
        _mla_in_kernel,
        grid=(lay.n_blocks,),
        in_specs=[lay.row_spec(D), _const_spec((1, D)), lay.mod_spec(1), lay.mod_spec(0),
                  _const_spec(win.shape), _const_spec(wuq.shape), _const_spec(wukv.shape),
                  _const_spec((1, MLA_Q_LORA)), _const_spec((1, MLA_KV_LORA)),
                  _const_spec((1, LANES)), _const_spec((1, LANES)), _const_spec((1, LANES)),
                  _const_spec((1, LANES)), tab_spec, tab_spec, tab_spec],
        out_specs=[lay.row_spec(Hn * MLA_QK_PAD), lay.row_spec(Hn * MLA_QK_PAD), lay.row_spec(Hn * MLA_VW)],
        out_shape=[bf(Hn * MLA_QK_PAD), bf(Hn * MLA_QK_PAD), bf(Hn * MLA_VW)],
        compiler_params=_params(1),
        name="mla_in",
    )(x, gain, mod, mod, win, wuq, wukv, q_norm.reshape(1, -1), kv_norm.reshape(1, -1),
      qn_nope.reshape(1, -1), _pad_lanes(qn_rope), kn_nope.reshape(1, -1), _pad_lanes(kn_rope), *tables)


def rope_tables(n_ctx, n_lat):
    n_freq = MLA_ROPE // 4
    inv = ROPE_THETA ** (-jnp.arange(n_freq, dtype=F32) / n_freq)
    t = jnp.arange(n_lat)
    a_r = (t // GRID_W).astype(F32)[:, None] * inv
    a_c = (t % GRID_W).astype(F32)[:, None] * inv
    ang = jnp.concatenate([a_r, a_r, a_c, a_c], axis=-1)
    ang = jnp.concatenate([jnp.zeros((n_ctx, MLA_ROPE), F32), ang], axis=0)
    cos, sin = jnp.cos(ang), jnp.sin(ang)
    low = (jnp.arange(MLA_ROPE) % (2 * ROPE_HALF)) < ROPE_HALF
    pad = lambda a: jnp.pad(a, ((0, 0), (0, LANES - MLA_ROPE)))
    return pad(cos), pad(jnp.where(low, -sin, 0.0)), pad(jnp.where(low, 0.0, sin))


MLA_HPS = 2


def _mla_attn_kernel(q_ref, k_ref, v_ref, o_ref, *, n_ctx):
    def attend(n_keys):
        for hd in range(MLA_HPS):
            q = q_ref[0, :, hd * MLA_QK_PAD:(hd + 1) * MLA_QK_PAD]
            k = k_ref[0, :n_keys, hd * MLA_QK_PAD:(hd + 1) * MLA_QK_PAD]
            v = v_ref[0, :n_keys, hd * MLA_VW:(hd + 1) * MLA_VW]
            s = lax.dot_general(q, k, (((1,), (1,)), ((), ())), preferred_element_type=F32)
            p = jnp.exp2(s - s.max(axis=1, keepdims=True)).astype(BF16)
            r = jnp.dot(p, v, preferred_element_type=F32)
            o_ref[0, :, hd * MLA_V:(hd + 1) * MLA_V] = (
                r[:, :MLA_V] / r[:, MLA_V:MLA_V + 1]).astype(o_ref.dtype)

    @pl.when(pl.program_id(2) == 0)
    def _():
        attend(n_ctx)

    @pl.when(pl.program_id(2) > 0)
    def _():
        attend(k_ref.shape[1])


def mla_attention(lay, q, k, v):
    assert lay.ctx_first
    Bn, S = lay.n_batch, lay.nb * TL
    q3, k3, v3 = (a.reshape(Bn, S, a.shape[-1]) for a in (q, k, v))
    out = pl.pallas_call(
        functools.partial(_mla_attn_kernel, n_ctx=TL),
        grid=(Bn, MLA_HEADS // MLA_HPS, lay.nb),
        in_specs=[pl.BlockSpec((1, TL, MLA_HPS * MLA_QK_PAD), lambda b, h, i: (b, i, h)),
                  pl.BlockSpec((1, S, MLA_HPS * MLA_QK_PAD), lambda b, h, i: (b, 0, h)),
                  pl.BlockSpec((1, S, MLA_HPS * MLA_VW), lambda b, h, i: (b, 0, h))],
        out_specs=pl.BlockSpec((1, TL, MLA_HPS * MLA_V), lambda b, h, i: (b, i, h)),
        out_shape=jax.ShapeDtypeStruct((Bn, S, MLA_HEADS * MLA_V), BF16),
        compiler_params=_params(3),
        name="mla_attention",
    )(q3, k3, v3)
    return out.reshape(lay.rows, MLA_HEADS * MLA_V)


N_PROLOGUE = {"conv": 5, "mlstm": 4, "mla": 1}


def _mixer_out_kernel(*refs, kind, nb, ctx_first):
    n_pro = N_PROLOGUE[kind]
    pro = refs[:n_pro]
    (wout_ref, x_ref, ga_ref, gain_ref, sc_ref, sh_ref, wr_ref, br_ref,
     xo_ref, h2_ref, te_ref, gate_ref, rank_ref, cnt_ref, carry_ref) = refs[n_pro:]
    r = pl.program_id(0)

    if kind == "conv":
        vprev_ref, v_ref, vnext_ref, bg_ref, cw_ref = pro
        j = r % nb
        first = (j == 0) | (j == 1) if ctx_first else (j == 0)
        last = (j == nb - 1) | (j == 0) if ctx_first else (j == nb - 1)
        v = v_ref[...].astype(F32)
        rows = lax.broadcasted_iota(jnp.int32, (TL, 1), 0)
        prev_row = jnp.where(first, 0.0, vprev_ref[BF16_ROWS - 1:BF16_ROWS, :].astype(F32))
        next_row = jnp.where(last, 0.0, vnext_ref[0:1, :].astype(F32))
        up = jnp.where(rows == 0, prev_row, pltpu.roll(v, 1, 0))
        dn = jnp.where(rows == TL - 1, next_row, pltpu.roll(v, TL - 1, 0))
        cw = cw_ref[...]
        a = bg_ref[...].astype(F32) * (up * cw[0:1] + v * cw[1:2] + dn * cw[2:3])
    elif kind == "mlstm":
        hf_ref, hb_ref, og_ref, ng_ref = pro
        hh = hf_ref[...] + hb_ref[...]
        a = jnp.concatenate([_rms(hh[:, h * ML_DV:(h + 1) * ML_DV]) for h in range(ML_HEADS)], axis=1)
        a = a * ng_ref[...] * og_ref[...].astype(F32)
    else:
        a = pro[0][...]

    y = jnp.dot(a.astype(BF16), wout_ref[...], preferred_element_type=F32)
    xn = x_ref[...] + ga_ref[0] * y
    xo_ref[...] = xn
    h2 = _norm_mod(xn, gain_ref[...], sc_ref[0], sh_ref[0])
    h2_ref[...] = _pack_rows(h2)
    logits = _split_dot_t(wr_ref[...], h2) + br_ref[...]

    sub = lax.broadcasted_iota(jnp.int32, (N_EXPERTS, TL), 0)
    sub_k = lax.broadcasted_iota(jnp.int32, (TOP_K, TL), 0)
    work = logits
    sel = jnp.zeros((N_EXPERTS, TL), F32)
    top_e = jnp.zeros((TOP_K, TL), jnp.int32)
    top_v = jnp.zeros((TOP_K, TL), F32)
    picks = []
    for kk in range(TOP_K):
        m = work.max(axis=0, keepdims=True)
        idx = jnp.min(jnp.where(work == m, sub, N_EXPERTS), axis=0, keepdims=True)
        hit = sub == idx
        picks.append(hit)
        sel = jnp.where(hit, 1.0, sel)
        work = jnp.where(hit, -jnp.inf, work)
        top_e = jnp.where(sub_k == kk, idx, top_e)
        top_v = jnp.where(sub_k == kk, m, top_v)
    ex = jnp.exp(top_v - top_v[0:1])
    gate_ref[...] = ex / ex.sum(axis=0, keepdims=True)
    te_ref[...] = top_e

    @pl.when(r == 0)
    def _():
        carry_ref[...] = jnp.zeros_like(carry_ref)

    tr = lax.broadcasted_iota(jnp.int32, (TL, TL), 0)
    tc = lax.broadcasted_iota(jnp.int32, (TL, TL), 1)
    before = jnp.dot(sel.astype(BF16), (tr < tc).astype(BF16), preferred_element_type=F32)
    pos = before + carry_ref[...]
    rank = jnp.zeros((TOP_K, TL), F32)
    for kk in range(TOP_K):
        rk = jnp.sum(jnp.where(picks[kk], pos, 0.0), axis=0, keepdims=True)
        rank = jnp.where(sub_k == kk, rk, rank)
    rank_ref[...] = rank.astype(jnp.int32)
    total = carry_ref[...] + jnp.sum(sel, axis=1, keepdims=True)
    carry_ref[...] = total
    cnt_ref[...] = total


def mixer_out(lay, kind, pro_args, w_out, x, mod, gain_f, w_r, b_r):
    D = D_MODEL
    nb = lay.nb
    if kind == "conv":
        v, bg, cw = pro_args
        per = TL // BF16_ROWS
        last_tile = lay.rows // BF16_ROWS - 1
        pro_specs = [pl.BlockSpec((BF16_ROWS, D), lambda r: (jnp.maximum(r * per - 1, 0), 0)),
                     lay.row_spec(D),
                     pl.BlockSpec((BF16_ROWS, D), lambda r: (jnp.minimum((r + 1) * per, last_tile), 0)),
                     lay.row_spec(D), _const_spec((CONV_WIDTH, D))]
        pro_in = [v, v, v, bg, cw]
    elif kind == "mlstm":
        h_f, h_b, og, ng = pro_args
        pro_specs = [lay.row_spec(ML_V), lay.row_spec(ML_V), lay.row_spec(ML_V), _const_spec((1, ML_V))]
        pro_in = [h_f, h_b, og, ng.reshape(1, ML_V)]
    else:
        pro_specs = [lay.row_spec(D)]
        pro_in = list(pro_args)
    k_in = w_out.shape[0]
    small = lambda dt: jax.ShapeDtypeStruct((TOP_K, lay.rows), dt)
    small_spec = pl.BlockSpec((TOP_K, TL), lambda r: (0, r))
    return pl.pallas_call(
        functools.partial(_mixer_out_kernel, kind=kind, nb=nb, ctx_first=lay.ctx_first),
        grid=(lay.n_blocks,),
        in_specs=pro_specs + [_const_spec((k_in, D)), lay.row_spec(D), lay.mod_spec(2),
                              _const_spec((1, D)), lay.mod_spec(4), lay.mod_spec(3),
                              _const_spec((2 * N_EXPERTS, D)), _const_spec((N_EXPERTS, 1))],
        out_specs=[lay.row_spec(D), lay.row_spec(PACK_W), small_spec, small_spec, small_spec,
                   _const_spec((N_EXPERTS, 1))],
        out_shape=[jax.ShapeDtypeStruct((lay.rows, D), F32),
                   jax.ShapeDtypeStruct((lay.rows, PACK_W), jnp.int32),
                   small(jnp.int32), small(F32), small(jnp.int32),
                   jax.ShapeDtypeStruct((N_EXPERTS, 1), F32)],
        scratch_shapes=[pltpu.VMEM((N_EXPERTS, 1), F32)],
        compiler_params=_params(1),
        name="mixer_out_" + kind,
    )(*pro_in, w_out.astype(BF16), x, mod, gain_f, mod, mod, _split_weight_t(w_r),
      b_r.reshape(N_EXPERTS, 1))


def _expert_ffn_kernel(blk_e_ref, first_ref, x_ref, w1_ref, b1_ref, w2_ref, b2_ref, o_ref,
                       w1b_ref, w2b_ref):
    del blk_e_ref

    @pl.when(first_ref[pl.program_id(0)] == 1)
    def _():
        w1b_ref[...] = w1_ref[0, 0].astype(BF16)
        w2b_ref[...] = w2_ref[0, 0].astype(BF16)

    x = _unpack_rows(x_ref[...], BF16)
    h = jnp.dot(x, w1b_ref[...], preferred_element_type=F32) + b1_ref[0, 0]
    glu = jnp.minimum(h[:, :MOE_FF], SWIGLU_LIMIT)
    lin = jnp.clip(h[:, MOE_FF:], -SWIGLU_LIMIT, SWIGLU_LIMIT)
    act = glu * jax.nn.sigmoid(SWIGLU_ALPHA * glu) * (lin + 1.0)
    y = jnp.dot(act.astype(BF16), w2b_ref[...], preferred_element_type=F32)
    o_ref[...] = _pack_rows(y + b2_ref[0, 0])


def expert_ffn(layer, xp, blk_e, blk_first, w1, b1, w2, b2):
    n_rows = xp.shape[0]
    D, F2 = D_MODEL, 2 * MOE_FF
    n_blk = n_rows // MOE_BLOCK
    grid_spec = pltpu.PrefetchScalarGridSpec(
        num_scalar_prefetch=2,
        grid=(n_blk,),
        in_specs=[
            pl.BlockSpec((MOE_BLOCK, PACK_W), lambda i, be, fi: (i, 0)),
            pl.BlockSpec((1, 1, D, F2), lambda i, be, fi: (layer, be[i], 0, 0)),
            pl.BlockSpec((1, 1, 1, F2), lambda i, be, fi: (layer, be[i], 0, 0)),
            pl.BlockSpec((1, 1, MOE_FF, D), lambda i, be, fi: (layer, be[i], 0, 0)),
            pl.BlockSpec((1, 1, 1, D), lambda i, be, fi: (layer, be[i], 0, 0)),
        ],
        out_specs=pl.BlockSpec((MOE_BLOCK, PACK_W), lambda i, be, fi: (i, 0)),
        scratch_shapes=[pltpu.VMEM((D, F2), BF16), pltpu.VMEM((MOE_FF, D), BF16)],
    )
    return pl.pallas_call(
        _expert_ffn_kernel,
        grid_spec=grid_spec,
        out_shape=jax.ShapeDtypeStruct((n_rows, PACK_W), jnp.int32),
        compiler_params=_params(1),
        name="expert_ffn",
    )(blk_e, blk_first, xp, w1, b1, w2, b2)


SC_CORES = 2
SC_SUBCORES = 16
SC_CHUNK = 64


def sc_gather(table, idx):
    n_idx = idx.shape[0]
    width = table.shape[1]
    n_workers = SC_CORES * SC_SUBCORES
    per_worker = n_idx // n_workers
    n_chunks = per_worker // SC_CHUNK
    assert n_chunks * SC_CHUNK * n_workers == n_idx and n_chunks % 2 == 0
    mesh = plsc.VectorSubcoreMesh(core_axis_name="c", subcore_axis_name="s",
                                  num_cores=SC_CORES, num_subcores=SC_SUBCORES)

    def body(table_hbm, idx_hbm, out_hbm, idx_v, rows_v, gsem, wsem):
        wid = lax.axis_index("s") * SC_CORES + lax.axis_index("c")
        pltpu.sync_copy(idx_hbm.at[wid], idx_v)

        def gather(ci, slot):
            return pltpu.make_async_copy(table_hbm.at[idx_v.at[ci]], rows_v.at[slot], gsem.at[slot])

        def write(ci, slot):
            return pltpu.make_async_copy(rows_v.at[slot], out_hbm.at[wid, ci], wsem.at[slot])

        gather(0, 0).start()

        @pl.loop(0, n_chunks, step=2)
        def _(c0):
            for slot in range(2):
                ci = c0 + slot
                other = 1 - slot

                @pl.when(ci + 1 < n_chunks)
                def _():
                    @pl.when(ci >= 1)
                    def _():
                        write(ci - 1, other).wait()
                    gather(ci + 1, other).start()

                gather(ci, slot).wait()
                write(ci, slot).start()

        write(n_chunks - 2, 0).wait()
        write(n_chunks - 1, 1).wait()

    out = pl.kernel(
        body,
        out_type=jax.ShapeDtypeStruct((n_workers, n_chunks, SC_CHUNK, width), table.dtype),
        mesh=mesh,
        scratch_types=[pltpu.VMEM((n_chunks, SC_CHUNK), jnp.int32),
                       pltpu.VMEM((2, SC_CHUNK, width), table.dtype),
                       pltpu.SemaphoreType.DMA((2,)),
                       pltpu.SemaphoreType.DMA((2,))],
        name="sc_gather",
    )(table, idx.reshape(n_workers, n_chunks, SC_CHUNK))
    return out.reshape(n_idx, width)


def _combine_kernel(x_ref, *refs):
    y_refs, (gate_ref, gf_ref, o_ref) = refs[:TOP_K], refs[TOP_K:]
    gates = gate_ref[...]
    acc = gates[:, 0:1] * _unpack_rows(y_refs[0][...], F32)
    for kk in range(1, TOP_K):
        acc = acc + gates[:, kk:kk + 1] * _unpack_rows(y_refs[kk][...], F32)
    o_ref[...] = x_ref[...] + gf_ref[0] * acc


def moe_combine(lay, x, yg, gates, mod, drop_ctx):
    D = D_MODEL
    if drop_ctx:
        nbo = lay.nb - 1
        src = lambda r: (r // nbo) * lay.nb + 1 + r % nbo
        n_out = lay.n_batch * nbo
    else:
        src = lambda r: r
        n_out = lay.n_blocks
    y_specs = [pl.BlockSpec((TL, PACK_W), functools.partial(lambda kk, r: (kk * lay.n_blocks + src(r), 0), kk))
               for kk in range(TOP_K)]
    return pl.pallas_call(
        _combine_kernel,
        grid=(n_out,),
        in_specs=[pl.BlockSpec((TL, D), lambda r: (src(r), 0))] + y_specs + [
            pl.BlockSpec((TL, TOP_K), lambda r: (src(r), 0)),
            pl.BlockSpec((1, 1, D), lambda r: (lay.mod_row(src(r)), 0, 5))],
        out_specs=pl.BlockSpec((TL, D), lambda r: (r, 0)),
        out_shape=jax.ShapeDtypeStruct((n_out * TL, D), F32),
        compiler_params=_params(1),
        name="moe_combine",
    )(x, yg, yg, yg, yg, gates, mod)


def moe_route(top_e, rank, counts):
    T = top_e.shape[1]
    counts = counts.reshape(N_EXPERTS).astype(jnp.int32)
    group_start = jnp.cumsum(counts) - counts
    padded = (counts + MOE_BLOCK - 1) // MOE_BLOCK * MOE_BLOCK
    padded_end = jnp.cumsum(padded)
    padded_start = padded_end - padded
    experts = jnp.arange(N_EXPERTS)
    start_of = jnp.sum(jnp.where(top_e[..., None] == experts, padded_start, 0), axis=-1)
    dest = start_of + rank
    n_rows = -(-(T * TOP_K) // MOE_BLOCK) * MOE_BLOCK + N_EXPERTS * MOE_BLOCK
    n_blk = n_rows // MOE_BLOCK
    blk_start = jnp.arange(n_blk) * MOE_BLOCK
    blk_e = jnp.minimum(jnp.sum(padded_end[None, :] <= blk_start[:, None], axis=1), N_EXPERTS - 1)
    order = jnp.argsort(top_e.T.reshape(-1))
    blk_hot = blk_e[:, None] == experts
    blk_src0 = jnp.sum(jnp.where(blk_hot, group_start - padded_start, 0), axis=-1) + blk_start
    pos = (blk_src0[:, None] + jnp.arange(MOE_BLOCK)[None, :]).reshape(-1)
    src_flat = order[jnp.clip(pos, 0, T * TOP_K - 1)]
    blk_e = blk_e.astype(jnp.int32)
    blk_first = jnp.concatenate([jnp.ones((1,), jnp.int32), (blk_e[1:] != blk_e[:-1]).astype(jnp.int32)])
    return (src_flat // TOP_K).astype(jnp.int32), dest.reshape(-1).astype(jnp.int32), blk_e, blk_first


def kernel(x, c, ctx, c_ctx, norm_mix, norm_ffn, w_mod, b_mod, conv_w_in, conv_w, conv_w_out, ml_w_in, ml_b_gate, ml_norm, ml_w_out, mla_w_in, mla_q_norm, mla_kv_norm, mla_w_uq, mla_w_ukv, mla_qn_nope, mla_qn_rope, mla_kn_nope, mla_kn_rope, mla_w_out, moe_w_router, moe_b_router, moe_w1, moe_b1, moe_w2, moe_b2):
    Bn, n_lat, D = x.shape
    n_ctx = ctx.shape[1]
    assert D == D_MODEL and n_ctx == TL and n_lat % TL == 0
    assert (DEPTH - 1) % N_MIXERS == 0
    full = Layout(Bn, (n_ctx + n_lat) // TL, True)
    lat_only = Layout(Bn, n_lat // TL, False)
    mods = ada_all(c, c_ctx, w_mod, b_mod)
    tables = rope_tables(n_ctx, n_lat)
    b1_all = moe_b1.reshape(DEPTH, N_EXPERTS, 1, 2 * MOE_FF)
    b2_all = moe_b2.reshape(DEPTH, N_EXPERTS, 1, D)
    X = jnp.concatenate([ctx, x], axis=1).reshape(full.rows, D)
    for layer in range(DEPTH):
        kind, j = layer % N_MIXERS, layer // N_MIXERS
        last = layer == DEPTH - 1
        lay = lat_only if last else full
        mod = mods[layer]
        gain_a = norm_mix[layer].reshape(1, D)
        gain_f = norm_ffn[layer].reshape(1, D)
        if kind == 0:
            bg, v = conv_in(lay, X, gain_a, mod, conv_w_in[j].astype(BF16))
            pro, w_out, name = (v, bg, conv_w[j]), conv_w_out[j], "conv"
        elif kind == 1:
            w = ml_w_in[j]
            n_main = 2 * ML_QK + 2 * ML_V
            w_main = jnp.concatenate([w[:, :ML_QK] * ML_DQK ** -0.5, w[:, ML_QK:n_main]], axis=1)
            q, k, v, og, g, gt = mlstm_in(lay, X, gain_a, mod, w_main.astype(BF16), w[:, n_main:],
                                          ml_b_gate[j])
            h_f, h_b = mlstm_scan(lay, q, k, v, g, gt)
            pro, w_out, name = (h_f, h_b, og, ml_norm[j]), ml_w_out[j], "mlstm"
        else:
            q, k, v = mla_in(lay, X, gain_a, mod, mla_w_in[j], mla_q_norm[j], mla_kv_norm[j],
                             mla_w_uq[j], mla_w_ukv[j], mla_qn_nope[j], mla_qn_rope[j],
                             mla_kn_nope[j], mla_kn_rope[j], tables)
            pro, w_out, name = (mla_attention(lay, q, k, v),), mla_w_out[j], "mla"
        X, h2, top_e, gates, rank, counts = mixer_out(
            lay, name, pro, w_out, X, mod, gain_f, moe_w_router[layer], moe_b_router[layer])
        src_tok, dest, blk_e, blk_first = moe_route(top_e, rank, counts)
        xp = sc_gather(h2, src_tok)
        yp = expert_ffn(layer, xp, blk_e, blk_first, moe_w1, b1_all, moe_w2, b2_all)
        yg = sc_gather(yp, dest)
        X = moe_combine(lay, X, yg, gates.T, mod, drop_ctx=(layer == DEPTH - 2))
    return X.reshape(Bn, n_lat, D)
```

```python
import functools

import jax
import jax.numpy as jnp
from jax import lax
from jax.experimental import pallas as pl
from jax.experimental.pallas import tpu as pltpu
from jax.experimental.pallas import tpu_sc as plsc

D_MODEL = 1024
DEPTH = 4
GRID_W = 64
N_MIXERS = 3
N_ADA = 6
RMS_EPS = 1e-6
CONV_WIDTH = 3
ML_HEADS = 8
ML_DQK = 64
ML_DV = 128
ML_QK = ML_HEADS * ML_DQK
ML_V = ML_HEADS * ML_DV
GATE_CAP = 15.0
MLA_HEADS = 8
MLA_NOPE = 128
MLA_ROPE = 64
MLA_V = 128
MLA_Q_LORA = 384
MLA_KV_LORA = 256
MLA_SCALE = (MLA_NOPE + MLA_ROPE) ** -0.5
ROPE_THETA = 10000.0
N_EXPERTS = 32
TOP_K = 4
MOE_FF = D_MODEL
SWIGLU_ALPHA = 1.702
SWIGLU_LIMIT = 7.0
MOE_BLOCK = 512

TL = 256
LANES = 128
BF16_ROWS = 16
VMEM_LIMIT = 48 * 1024 * 1024
HI = lax.Precision.HIGHEST
F32 = jnp.float32
BF16 = jnp.bfloat16


def _params(n_axes):
    return pltpu.CompilerParams(dimension_semantics=("arbitrary",) * n_axes,
                                vmem_limit_bytes=VMEM_LIMIT)


def _rms(x, width=None):
    width = x.shape[-1] if width is None else width
    return x * lax.rsqrt(jnp.sum(x * x, axis=-1, keepdims=True) * (1.0 / width) + RMS_EPS)


def _norm_mod(x, gain, scale, shift):
    return _rms(x) * gain * (1.0 + scale) + shift


def _split_weight_t(w):
    hi = w.astype(BF16)
    lo = (w - hi.astype(F32)).astype(BF16)
    return jnp.concatenate([hi.T, lo.T], axis=0)


def _split_dot_t(w2, h):
    n = w2.shape[0] // 2
    dn = (((1,), (1,)), ((), ()))
    h_hi = h.astype(BF16)
    h_lo = (h - h_hi.astype(F32)).astype(BF16)
    both = lax.dot_general(w2, h_hi, dn, preferred_element_type=F32)
    cross = lax.dot_general(w2[:n], h_lo, dn, preferred_element_type=F32)
    return both[:n] + both[n:] + cross


PACK_W = D_MODEL // 2
HIGH_HALF = -65536


def _pack_rows(x):
    xb = x.astype(BF16).astype(F32)
    lo = lax.bitcast_convert_type(xb[:, :PACK_W], jnp.int32)
    hi = lax.bitcast_convert_type(xb[:, PACK_W:], jnp.int32)
    return hi | lax.shift_right_logical(lo, 16)


def _unpack_rows(w, dtype):
    lo = lax.bitcast_convert_type(lax.shift_left(w, 16), F32)
    hi = lax.bitcast_convert_type(w & HIGH_HALF, F32)
    return jnp.concatenate([lo.astype(dtype), hi.astype(dtype)], axis=1)


class Layout:
    def __init__(self, n_batch, nb, ctx_first):
        self.n_batch, self.nb, self.ctx_first = n_batch, nb, ctx_first
        self.n_blocks = n_batch * nb
        self.rows = self.n_blocks * TL

    def mod_row(self, r):
        b = r // self.nb
        return jnp.where(r % self.nb == 0, self.n_batch, b) if self.ctx_first else b

    def row_spec(self, width):
        return pl.BlockSpec((TL, width), lambda r: (r, 0))

    def mod_spec(self, piece):
        return pl.BlockSpec((1, 1, D_MODEL), lambda r: (self.mod_row(r), 0, piece))


def _const_spec(shape):
    return pl.BlockSpec(shape, lambda *_: (0,) * len(shape))


ADA_ROWS = 16
ADA_TN = 1536


def _ada_kernel(c_ref, w_ref, b_ref, o_ref):
    c = c_ref[...]
    s = c * jax.nn.sigmoid(c)
    o_ref[0] = jnp.dot(s, w_ref[0], precision=HI, preferred_element_type=F32) + b_ref[0]


def ada_all(c, c_ctx, w_mod, b_mod):
    Bn, D = c.shape
    cond = jnp.zeros((ADA_ROWS, D), F32).at[:Bn].set(c).at[Bn].set(c_ctx)
    out = pl.pallas_call(
        _ada_kernel,
        grid=(DEPTH, N_ADA * D // ADA_TN),
        in_specs=[pl.BlockSpec((ADA_ROWS, D), lambda l, n: (0, 0)),
                  pl.BlockSpec((1, D, ADA_TN), lambda l, n: (l, 0, n)),
                  pl.BlockSpec((1, 1, ADA_TN), lambda l, n: (l, 0, n))],
        out_specs=pl.BlockSpec((1, ADA_ROWS, ADA_TN), lambda l, n: (l, 0, n)),
        out_shape=jax.ShapeDtypeStruct((DEPTH, ADA_ROWS, N_ADA * D), F32),
        compiler_params=_params(2),
        name="ada_mod",
    )(cond, w_mod, b_mod.reshape(DEPTH, 1, N_ADA * D))
    return out[:, :Bn + 1, None, :]


def _conv_in_kernel(x_ref, gain_ref, sc_ref, sh_ref, w_ref, bg_ref, v_ref):
    D = D_MODEL
    h = _norm_mod(x_ref[...], gain_ref[...], sc_ref[0], sh_ref[0]).astype(BF16)
    p = jnp.dot(h, w_ref[...], preferred_element_type=F32)
    bg_ref[...] = p[:, :D].astype(BF16)
    v_ref[...] = (p[:, D:2 * D] * p[:, 2 * D:]).astype(BF16)


def conv_in(lay, x, gain, mod, w_in):
    D = D_MODEL
    sds = jax.ShapeDtypeStruct((lay.rows, D), BF16)
    return pl.pallas_call(
        _conv_in_kernel,
        grid=(lay.n_blocks,),
        in_specs=[lay.row_spec(D), _const_spec((1, D)), lay.mod_spec(1), lay.mod_spec(0),
                  _const_spec((D, 3 * D))],
        out_specs=[lay.row_spec(D), lay.row_spec(D)],
        out_shape=[sds, sds],
        compiler_params=_params(1),
        name="conv_in",
    )(x, gain, mod, mod, w_in)


ML_T = 256
ML_SW = 2 * ML_DV
ML_NG = 4 * ML_HEADS


LOG2E = 1.4426950408889634


def _gate_act(g, is_forget):
    g = GATE_CAP * jnp.tanh(g * (1.0 / GATE_CAP))
    log_sig = jnp.minimum(g, 0.0) - jnp.log(1.0 + jnp.exp(-jnp.abs(g)))
    return jnp.where(is_forget, log_sig, g) * LOG2E


def _mlstm_in_kernel(x_ref, gain_ref, sc_ref, sh_ref, w_ref, wg_ref, bgt_ref,
                     q_ref, k_ref, v_ref, og_ref, g_ref, gt_ref):
    h = _norm_mod(x_ref[...], gain_ref[...], sc_ref[0], sh_ref[0])
    p = jnp.dot(h.astype(BF16), w_ref[...], preferred_element_type=F32)
    q_ref[...] = p[:, :ML_QK].astype(BF16)
    k_ref[...] = p[:, ML_QK:2 * ML_QK].astype(BF16)
    v_ref[...] = p[:, 2 * ML_QK:2 * ML_QK + ML_V].astype(BF16)
    og_ref[...] = jax.nn.sigmoid(p[:, 2 * ML_QK + ML_V:]).astype(BF16)
    gt = _split_dot_t(wg_ref[...], h) + bgt_ref[...]
    row = lax.broadcasted_iota(jnp.int32, gt.shape, 0)
    gt = _gate_act(gt, (row // ML_HEADS) % 2 == 1)
    gt_ref[...] = gt
    eye = (lax.broadcasted_iota(jnp.int32, (TL, TL), 0)
           == lax.broadcasted_iota(jnp.int32, (TL, TL), 1)).astype(F32)
    g_ref[...] = lax.dot_general(eye, gt, (((1,), (1,)), ((), ())), precision=HI,
                                 preferred_element_type=F32)


def mlstm_in(lay, x, gain, mod, w_main, w_g, b_g):
    D = D_MODEL
    n_main = 2 * ML_QK + 2 * ML_V
    bf = lambda w: jax.ShapeDtypeStruct((lay.rows, w), BF16)
    return pl.pallas_call(
        _mlstm_in_kernel,
        grid=(lay.n_blocks,),
        in_specs=[lay.row_spec(D), _const_spec((1, D)), lay.mod_spec(1), lay.mod_spec(0),
                  _const_spec((D, n_main)), _const_spec((2 * ML_NG, D)), _const_spec((ML_NG, 1))],
        out_specs=[lay.row_spec(ML_QK), lay.row_spec(ML_QK), lay.row_spec(ML_V), lay.row_spec(ML_V),
                   lay.row_spec(ML_NG), pl.BlockSpec((ML_NG, TL), lambda r: (0, r))],
        out_shape=[bf(ML_QK), bf(ML_QK), bf(ML_V), bf(ML_V),
                   jax.ShapeDtypeStruct((lay.rows, ML_NG), F32),
                   jax.ShapeDtypeStruct((ML_NG, lay.rows), F32)],
        compiler_params=_params(1),
        name="mlstm_in",
    )(x, gain, mod, mod, w_main, _split_weight_t(w_g), b_g.reshape(ML_NG, 1))


def _mlstm_dir(reverse, q_ref, k_ref, v_ref, g_ref, gt_ref, o_ref, s_ref, m_ref):
    T = ML_T
    row = lax.broadcasted_iota(jnp.int32, (T, T), 0)
    col = lax.broadcasted_iota(jnp.int32, (T, T), 1)
    mask = (col >= row) if reverse else (col <= row)
    g = g_ref[...]
    gt = gt_ref[...]
    tri_c = mask.astype(F32)
    tri_r = ((row >= col) if reverse else (row <= col)).astype(F32)
    bc = jnp.dot(tri_c, g, precision=HI, preferred_element_type=F32)
    br = jnp.dot(gt, tri_r, precision=HI, preferred_element_type=F32)
    gi, gf = (2 * ML_HEADS, 3 * ML_HEADS) if reverse else (0, ML_HEADS)
    lane = lax.broadcasted_iota(jnp.int32, (T, 2 * ML_DQK), 1)
    ones_col = (lax.broadcasted_iota(jnp.int32, (T, ML_DV), 1) == 0).astype(BF16)
    for h in range(ML_HEADS):
        pair = (h // 2) * 2 * ML_DQK
        own = (lane >= ML_DQK) if (h % 2) else (lane < ML_DQK)
        qp = q_ref[:, pair:pair + 2 * ML_DQK]
        kp = k_ref[:, pair:pair + 2 * ML_DQK]
        qm = jnp.where(own, qp, jnp.zeros_like(qp))
        km = jnp.where(own, kp, jnp.zeros_like(kp))
        v_ext = jnp.concatenate([v_ref[:, h * ML_DV:(h + 1) * ML_DV], ones_col], axis=1)
        b_col = bc[:, gf + h:gf + h + 1]
        i_col = g[:, gi + h:gi + h + 1]
        b_row = br[gf + h:gf + h + 1, :]
        i_row = gt[gi + h:gi + h + 1, :]
        tot = b_row[:, 0:1] if reverse else b_row[:, T - 1:T]
        m_prev = m_ref[h][0:1, 0:1]
        s_prev = s_ref[h]
        d = jnp.where(mask, b_col - b_row + i_row, -jnp.inf)
        inter = b_col + m_prev
        m_t = jnp.maximum(inter, jnp.max(d, axis=1, keepdims=True))
        s_raw = lax.dot_general(qm, km, (((1,), (1,)), ((), ())), preferred_element_type=F32)
        p = (s_raw * jnp.exp2(d - m_t)).astype(BF16)
        a = jnp.exp2(inter - m_t)
        r = (jnp.dot(p, v_ext, preferred_element_type=F32)
             + a * jnp.dot(qm, s_prev.astype(BF16), preferred_element_type=F32))
        den = r[:, ML_DV:ML_DV + 1]
        o_ref[:, h * ML_DV:(h + 1) * ML_DV] = (
            r[:, :ML_DV] / jnp.maximum(jnp.abs(den), jnp.exp2(-m_t))).astype(o_ref.dtype)
        g_col = tot - b_col + i_col
        m_new = jnp.maximum(tot + m_prev, jnp.max(g_col, axis=0, keepdims=True))
        decay = jnp.exp2(tot + m_prev - m_new)
        wv = (jnp.exp2(g_col - m_new) * v_ext.astype(F32)).astype(BF16)
        s_ref[h] = decay * s_prev + lax.dot_general(
            km, wv, (((0,), (0,)), ((), ())), preferred_element_type=F32)
        m_ref[h] = jnp.broadcast_to(m_new, m_ref.shape[1:])


def _mlstm_scan_kernel(qf_ref, kf_ref, vf_ref, gf_ref, gtf_ref, qb_ref, kb_ref, vb_ref, gb_ref, gtb_ref,
                       of_ref, ob_ref, sf_ref, mf_ref, sb_ref, mb_ref):
    @pl.when(pl.program_id(1) == 0)
    def _():
        sf_ref[...] = jnp.zeros_like(sf_ref)
        mf_ref[...] = jnp.zeros_like(mf_ref)
        sb_ref[...] = jnp.zeros_like(sb_ref)
        mb_ref[...] = jnp.zeros_like(mb_ref)

    _mlstm_dir(False, qf_ref, kf_ref, vf_ref, gf_ref, gtf_ref, of_ref, sf_ref, mf_ref)
    _mlstm_dir(True, qb_ref, kb_ref, vb_ref, gb_ref, gtb_ref, ob_ref, sb_ref, mb_ref)


def mlstm_scan(lay, q, k, v, g, gt):
    assert lay.ctx_first
    per = TL // ML_T
    nb = lay.nb * per
    rev = lambda j: jnp.where(j < per, per - 1 - j, nb + per - 1 - j)
    fwd = lambda b, j: (b * nb + j, 0)
    bwd = lambda b, j: (b * nb + rev(j), 0)
    fwd_t = lambda b, j: (0, b * nb + j)
    bwd_t = lambda b, j: (0, b * nb + rev(j))

    def specs(im, imt):
        return [pl.BlockSpec((ML_T, ML_QK), im), pl.BlockSpec((ML_T, ML_QK), im),
                pl.BlockSpec((ML_T, ML_V), im), pl.BlockSpec((ML_T, ML_NG), im),
                pl.BlockSpec((ML_NG, ML_T), imt)]

    out_sds = jax.ShapeDtypeStruct((lay.rows, ML_V), F32)
    state = [pltpu.VMEM((ML_HEADS, 2 * ML_DQK, ML_SW), F32),
             pltpu.VMEM((ML_HEADS, 8, LANES), F32)]
    return pl.pallas_call(
        _mlstm_scan_kernel,
        grid=(lay.n_batch, nb),
        in_specs=specs(fwd, fwd_t) + specs(bwd, bwd_t),
        out_specs=[pl.BlockSpec((ML_T, ML_V), fwd), pl.BlockSpec((ML_T, ML_V), bwd)],
        out_shape=[out_sds, out_sds],
        scratch_shapes=state + state,
        compiler_params=_params(2),
        name="mlstm_scan",
    )(q, k, v, g, gt, q, k, v, g, gt)


MLA_QK_PAD = 256
MLA_VW = 2 * MLA_V
MLA_IN_PAD = MLA_Q_LORA + MLA_KV_LORA + LANES
ROPE_HALF = MLA_ROPE // 4


def _mla_in_kernel(x_ref, gain_ref, sc_ref, sh_ref, win_ref, wuq_ref, wukv_ref, qn_ref, kvn_ref,
                   qnn_ref, qnr_ref, knn_ref, knr_ref, cos_ref, sa_ref, sb_ref,
                   q_out, k_out, v_out):
    h = _norm_mod(x_ref[...], gain_ref[...], sc_ref[0], sh_ref[0]).astype(BF16)
    p = jnp.dot(h, win_ref[...], preferred_element_type=F32)
    cq = _rms(p[:, :MLA_Q_LORA]) * qn_ref[...]
    ckv = _rms(p[:, MLA_Q_LORA:MLA_Q_LORA + MLA_KV_LORA]) * kvn_ref[...]
    kr = p[:, MLA_Q_LORA + MLA_KV_LORA:]
    q = jnp.dot(cq.astype(BF16), wuq_ref[...], preferred_element_type=F32)
    kv = jnp.dot(ckv.astype(BF16), wukv_ref[...], preferred_element_type=F32)
    nv = MLA_HEADS * MLA_NOPE
    ones_col = (lax.broadcasted_iota(jnp.int32, (TL, MLA_VW - MLA_V), 1) == 0).astype(BF16)
    cos, sa, sb = cos_ref[...], sa_ref[...], sb_ref[...]

    def rope(xp):
        return (xp * cos + pltpu.roll(xp, LANES - ROPE_HALF, 1) * sa + pltpu.roll(xp, ROPE_HALF, 1) * sb)

    kr = rope(_rms(kr, MLA_ROPE) * knr_ref[...]).astype(BF16)
    for hd in range(MLA_HEADS):
        c0 = hd * MLA_QK_PAD
        qn = _rms(q[:, c0:c0 + MLA_NOPE]) * qnn_ref[...]
        qr = rope(_rms(q[:, c0 + MLA_NOPE:c0 + MLA_QK_PAD], MLA_ROPE) * qnr_ref[...])
        q_out[:, c0:c0 + MLA_NOPE] = (qn * (MLA_SCALE * LOG2E)).astype(BF16)
        q_out[:, c0 + MLA_NOPE:c0 + MLA_QK_PAD] = (qr * (MLA_SCALE * LOG2E)).astype(BF16)
        v0 = nv + hd * MLA_V
        v_out[:, hd * MLA_VW:hd * MLA_VW + MLA_V] = kv[:, v0:v0 + MLA_V].astype(BF16)
        v_out[:, hd * MLA_VW + MLA_V:(hd + 1) * MLA_VW] = ones_col
        kn = _rms(kv[:, hd * MLA_NOPE:(hd + 1) * MLA_NOPE]) * knn_ref[...]
        k_out[:, c0:c0 + MLA_NOPE] = kn.astype(BF16)
        k_out[:, c0 + MLA_NOPE:c0 + MLA_QK_PAD] = kr


def _pad_lanes(g):
    return jnp.pad(g, (0, LANES - g.shape[0])).reshape(1, LANES)


def mla_in(lay, x, gain, mod, w_in, q_norm, kv_norm, w_uq, w_ukv, qn_nope, qn_rope, kn_nope, kn_rope,
           tables):
    D = D_MODEL
    Hn = MLA_HEADS
    win = jnp.pad(w_in, ((0, 0), (0, MLA_IN_PAD - w_in.shape[1]))).astype(BF16)
    wuq = jnp.pad(w_uq.reshape(MLA_Q_LORA, Hn, MLA_NOPE + MLA_ROPE),
                  ((0, 0), (0, 0), (0, MLA_QK_PAD - MLA_NOPE - MLA_ROPE)))
    wuq = wuq.reshape(MLA_Q_LORA, Hn * MLA_QK_PAD).astype(BF16)
    wkv = w_ukv.reshape(MLA_KV_LORA, Hn, MLA_NOPE + MLA_V)
    wukv = jnp.concatenate([wkv[:, :, :MLA_NOPE].reshape(MLA_KV_LORA, Hn * MLA_NOPE),
                            wkv[:, :, MLA_NOPE:].reshape(MLA_KV_LORA, Hn * MLA_V)], axis=1).astype(BF16)
    nb = lay.nb
    tab_spec = pl.BlockSpec((TL, LANES), lambda r: (r % nb, 0))
    bf = lambda w: jax.ShapeDtypeStruct((lay.rows, w), BF16)
    return pl.pallas_call(
        _mla_in_kernel,
        grid=(lay.n_blocks,),
        in_specs=[lay.row_spec(D), _const_spec((1, D)), lay.mod_spec(1), lay.mod_spec(0),
                  _const_spec(win.shape), _const_spec(wuq.shape), _const_spec(wukv.shape),
                  _const_spec((1, MLA_Q_LORA)), _const_spec((1, MLA_KV_LORA)),
                  _const_spec((1, LANES)), _const_spec((1, LANES)), _const_spec((1, LANES)),
                  _const_spec((1, LANES)), tab_spec, tab_spec, tab_spec],
        out_specs=[lay.row_spec(Hn * MLA_QK_PAD), lay.row_spec(Hn * MLA_QK_PAD), lay.row_spec(Hn * MLA_VW)],
        out_shape=[bf(Hn * MLA_QK_PAD), bf(Hn * MLA_QK_PAD), bf(Hn * MLA_VW)],
        compiler_params=_params(1),
        name="mla_in",
    )(x, gain, mod, mod, win, wuq, wukv, q_norm.reshape(1, -1), kv_norm.reshape(1, -1),
      qn_nope.reshape(1, -1), _pad_lanes(qn_rope), kn_nope.reshape(1, -1), _pad_lanes(kn_rope), *tables)


def rope_tables(n_ctx, n_lat):
    n_freq = MLA_ROPE // 4
    inv = ROPE_THETA ** (-jnp.arange(n_freq, dtype=F32) / n_freq)
    t = jnp.arange(n_lat)
    a_r = (t // GRID_W).astype(F32)[:, None] * inv
    a_c = (t % GRID_W).astype(F32)[:, None] * inv
    ang = jnp.concatenate([a_r, a_r, a_c, a_c], axis=-1)
    ang = jnp.concatenate([jnp.zeros((n_ctx, MLA_ROPE), F32), ang], axis=0)
    cos, sin = jnp.cos(ang), jnp.sin(ang)
    low = (jnp.arange(MLA_ROPE) % (2 * ROPE_HALF)) < ROPE_HALF
    pad = lambda a: jnp.pad(a, ((0, 0), (0, LANES - MLA_ROPE)))
    return pad(cos), pad(jnp.where(low, -sin, 0.0)), pad(jnp.where(low, 0.0, sin))


MLA_HPS = 2


def _mla_attn_kernel(q_ref, k_ref, v_ref, o_ref, *, n_ctx):
    def attend(n_keys):
        for hd in range(MLA_HPS):
            q = q_ref[0, :, hd * MLA_QK_PAD:(hd + 1) * MLA_QK_PAD]
            k = k_ref[0, :n_keys, hd * MLA_QK_PAD:(hd + 1) * MLA_QK_PAD]
            v = v_ref[0, :n_keys, hd * MLA_VW:(hd + 1) * MLA_VW]
            s = lax.dot_general(q, k, (((1,), (1,)), ((), ())), preferred_element_type=F32)
            p = jnp.exp2(s - s.max(axis=1, keepdims=True)).astype(BF16)
            r = jnp.dot(p, v, preferred_element_type=F32)
            o_ref[0, :, hd * MLA_V:(hd + 1) * MLA_V] = (
                r[:, :MLA_V] / r[:, MLA_V:MLA_V + 1]).astype(o_ref.dtype)

    @pl.when(pl.program_id(2) == 0)
    def _():
        attend(n_ctx)

    @pl.when(pl.program_id(2) > 0)
    def _():
        attend(k_ref.shape[1])


def mla_attention(lay, q, k, v):
    assert lay.ctx_first
    Bn, S = lay.n_batch, lay.nb * TL
    q3, k3, v3 = (a.reshape(Bn, S, a.shape[-1]) for a in (q, k, v))
    out = pl.pallas_call(
        functools.partial(_mla_attn_kernel, n_ctx=TL),
        grid=(Bn, MLA_HEADS // MLA_HPS, lay.nb),
        in_specs=[pl.BlockSpec((1, TL, MLA_HPS * MLA_QK_PAD), lambda b, h, i: (b, i, h)),
                  pl.BlockSpec((1, S, MLA_HPS * MLA_QK_PAD), lambda b, h, i: (b, 0, h)),
                  pl.BlockSpec((1, S, MLA_HPS * MLA_VW), lambda b, h, i: (b, 0, h))],
        out_specs=pl.BlockSpec((1, TL, MLA_HPS * MLA_V), lambda b, h, i: (b, i, h)),
        out_shape=jax.ShapeDtypeStruct((Bn, S, MLA_HEADS * MLA_V), BF16),
        compiler_params=_params(3),
        name="mla_attention",
    )(q3, k3, v3)
    return out.reshape(lay.rows, MLA_HEADS * MLA_V)


N_PROLOGUE = {"conv": 5, "mlstm": 4, "mla": 1}


def _mixer_out_kernel(*refs, kind, nb, ctx_first):
    n_pro = N_PROLOGUE[kind]
    pro = refs[:n_pro]
    (wout_ref, x_ref, ga_ref, gain_ref, sc_ref, sh_ref, wr_ref, br_ref,
     xo_ref, h2_ref, te_ref, gate_ref, rank_ref, cnt_ref, carry_ref) = refs[n_pro:]
    r = pl.program_id(0)

    if kind == "conv":
        vprev_ref, v_ref, vnext_ref, bg_ref, cw_ref = pro
        j = r % nb
        first = (j == 0) | (j == 1) if ctx_first else (j == 0)
        last = (j == nb - 1) | (j == 0) if ctx_first else (j == nb - 1)
        v = v_ref[...].astype(F32)
        rows = lax.broadcasted_iota(jnp.int32, (TL, 1), 0)
        prev_row = jnp.where(first, 0.0, vprev_ref[BF16_ROWS - 1:BF16_ROWS, :].astype(F32))
        next_row = jnp.where(last, 0.0, vnext_ref[0:1, :].astype(F32))
        up = jnp.where(rows == 0, prev_row, pltpu.roll(v, 1, 0))
        dn = jnp.where(rows == TL - 1, next_row, pltpu.roll(v, TL - 1, 0))
        cw = cw_ref[...]
        a = bg_ref[...].astype(F32) * (up * cw[0:1] + v * cw[1:2] + dn * cw[2:3])
    elif kind == "mlstm":
        hf_ref, hb_ref, og_ref, ng_ref = pro
        hh = hf_ref[...] + hb_ref[...]
        a = jnp.concatenate([_rms(hh[:, h * ML_DV:(h + 1) * ML_DV]) for h in range(ML_HEADS)], axis=1)
        a = a * ng_ref[...] * og_ref[...].astype(F32)
    else:
        a = pro[0][...]

    y = jnp.dot(a.astype(BF16), wout_ref[...], preferred_element_type=F32)
    xn = x_ref[...] + ga_ref[0] * y
    xo_ref[...] = xn
    h2 = _norm_mod(xn, gain_ref[...], sc_ref[0], sh_ref[0])
    h2_ref[...] = _pack_rows(h2)
    logits = _split_dot_t(wr_ref[...], h2) + br_ref[...]

    sub = lax.broadcasted_iota(jnp.int32, (N_EXPERTS, TL), 0)
    sub_k = lax.broadcasted_iota(jnp.int32, (TOP_K, TL), 0)
    work = logits
    sel = jnp.zeros((N_EXPERTS, TL), F32)
    top_e = jnp.zeros((TOP_K, TL), jnp.int32)
    top_v = jnp.zeros((TOP_K, TL), F32)
    picks = []
    for kk in range(TOP_K):
        m = work.max(axis=0, keepdims=True)
        idx = jnp.min(jnp.where(work == m, sub, N_EXPERTS), axis=0, keepdims=True)
        hit = sub == idx
        picks.append(hit)
        sel = jnp.where(hit, 1.0, sel)
        work = jnp.where(hit, -jnp.inf, work)
        top_e = jnp.where(sub_k == kk, idx, top_e)
        top_v = jnp.where(sub_k == kk, m, top_v)
    ex = jnp.exp(top_v - top_v[0:1])
    gate_ref[...] = ex / ex.sum(axis=0, keepdims=True)
    te_ref[...] = top_e

    @pl.when(r == 0)
    def _():
        carry_ref[...] = jnp.zeros_like(carry_ref)

    tr = lax.broadcasted_iota(jnp.int32, (TL, TL), 0)
    tc = lax.broadcasted_iota(jnp.int32, (TL, TL), 1)
    before = jnp.dot(sel.astype(BF16), (tr < tc).astype(BF16), preferred_element_type=F32)
    pos = before + carry_ref[...]
    rank = jnp.zeros((TOP_K, TL), F32)
    for kk in range(TOP_K):
        rk = jnp.sum(jnp.where(picks[kk], pos, 0.0), axis=0, keepdims=True)
        rank = jnp.where(sub_k == kk, rk, rank)
    rank_ref[...] = rank.astype(jnp.int32)
    total = carry_ref[...] + jnp.sum(sel, axis=1, keepdims=True)
    carry_ref[...] = total
    cnt_ref[...] = total


def mixer_out(lay, kind, pro_args, w_out, x, mod, gain_f, w_r, b_r):
    D = D_MODEL
    nb = lay.nb
    if kind == "conv":
        v, bg, cw = pro_args
        per = TL // BF16_ROWS
        last_tile = lay.rows // BF16_ROWS - 1
        pro_specs = [pl.BlockSpec((BF16_ROWS, D), lambda r: (jnp.maximum(r * per - 1, 0), 0)),
                     lay.row_spec(D),
                     pl.BlockSpec((BF16_ROWS, D), lambda r: (jnp.minimum((r + 1) * per, last_tile), 0)),
                     lay.row_spec(D), _const_spec((CONV_WIDTH, D))]
        pro_in = [v, v, v, bg, cw]
    elif kind == "mlstm":
        h_f, h_b, og, ng = pro_args
        pro_specs = [lay.row_spec(ML_V), lay.row_spec(ML_V), lay.row_spec(ML_V), _const_spec((1, ML_V))]
        pro_in = [h_f, h_b, og, ng.reshape(1, ML_V)]
    else:
        pro_specs = [lay.row_spec(D)]
        pro_in = list(pro_args)
    k_in = w_out.shape[0]
    small = lambda dt: jax.ShapeDtypeStruct((TOP_K, lay.rows), dt)
    small_spec = pl.BlockSpec((TOP_K, TL), lambda r: (0, r))
    return pl.pallas_call(
        functools.partial(_mixer_out_kernel, kind=kind, nb=nb, ctx_first=lay.ctx_first),
        grid=(lay.n_blocks,),
        in_specs=pro_specs + [_const_spec((k_in, D)), lay.row_spec(D), lay.mod_spec(2),
                              _const_spec((1, D)), lay.mod_spec(4), lay.mod_spec(3),
                              _const_spec((2 * N_EXPERTS, D)), _const_spec((N_EXPERTS, 1))],
        out_specs=[lay.row_spec(D), lay.row_spec(PACK_W), small_spec, small_spec, small_spec,
                   _const_spec((N_EXPERTS, 1))],
        out_shape=[jax.ShapeDtypeStruct((lay.rows, D), F32),
                   jax.ShapeDtypeStruct((lay.rows, PACK_W), jnp.int32),
                   small(jnp.int32), small(F32), small(jnp.int32),
                   jax.ShapeDtypeStruct((N_EXPERTS, 1), F32)],
        scratch_shapes=[pltpu.VMEM((N_EXPERTS, 1), F32)],
        compiler_params=_params(1),
        name="mixer_out_" + kind,
    )(*pro_in, w_out.astype(BF16), x, mod, gain_f, mod, mod, _split_weight_t(w_r),
      b_r.reshape(N_EXPERTS, 1))


def _expert_ffn_kernel(blk_e_ref, first_ref, used_ref, x_ref, w1_ref, b1_ref, w2_ref, b2_ref, o_ref,
                       w1b_ref, w2b_ref):
    del blk_e_ref
    i = pl.program_id(0)

    @pl.when(first_ref[i] == 1)
    def _():
        w1b_ref[...] = w1_ref[0, 0].astype(BF16)
        w2b_ref[...] = w2_ref[0, 0].astype(BF16)

    @pl.when(i < used_ref[0])
    def _():
        x = _unpack_rows(x_ref[...], BF16)
        h = jnp.dot(x, w1b_ref[...], preferred_element_type=F32) + b1_ref[0, 0]
        glu = jnp.minimum(h[:, :MOE_FF], SWIGLU_LIMIT)
        lin = jnp.clip(h[:, MOE_FF:], -SWIGLU_LIMIT, SWIGLU_LIMIT)
        act = glu * jax.nn.sigmoid(SWIGLU_ALPHA * glu) * (lin + 1.0)
        y = jnp.dot(act.astype(BF16), w2b_ref[...], preferred_element_type=F32)
        o_ref[...] = _pack_rows(y + b2_ref[0, 0])

    @pl.when(i >= used_ref[0])
    def _():
        o_ref[...] = jnp.zeros_like(o_ref)


def expert_ffn(layer, xp, blk_e, blk_first, n_used, w1, b1, w2, b2):
    n_rows = xp.shape[0]
    D, F2 = D_MODEL, 2 * MOE_FF
    n_blk = n_rows // MOE_BLOCK
    grid_spec = pltpu.PrefetchScalarGridSpec(
        num_scalar_prefetch=3,
        grid=(n_blk,),
        in_specs=[
            pl.BlockSpec((MOE_BLOCK, PACK_W), lambda i, be, fi, nu: (i, 0)),
            pl.BlockSpec((1, 1, D, F2), lambda i, be, fi, nu: (layer, be[i], 0, 0)),
            pl.BlockSpec((1, 1, 1, F2), lambda i, be, fi, nu: (layer, be[i], 0, 0)),
            pl.BlockSpec((1, 1, MOE_FF, D), lambda i, be, fi, nu: (layer, be[i], 0, 0)),
            pl.BlockSpec((1, 1, 1, D), lambda i, be, fi, nu: (layer, be[i], 0, 0)),
        ],
        out_specs=pl.BlockSpec((MOE_BLOCK, PACK_W), lambda i, be, fi, nu: (i, 0)),
        scratch_shapes=[pltpu.VMEM((D, F2), BF16), pltpu.VMEM((MOE_FF, D), BF16)],
    )
    return pl.pallas_call(
        _expert_ffn_kernel,
        grid_spec=grid_spec,
        out_shape=jax.ShapeDtypeStruct((n_rows, PACK_W), jnp.int32),
        compiler_params=_params(1),
        name="expert_ffn",
    )(blk_e, blk_first, n_used, xp, w1, b1, w2, b2)


SC_CORES = 2
SC_SUBCORES = 16
SC_CHUNK = 64


def sc_gather(table, idx):
    n_idx = idx.shape[0]
    width = table.shape[1]
    n_workers = SC_CORES * SC_SUBCORES
    per_worker = n_idx // n_workers
    n_chunks = per_worker // SC_CHUNK
    assert n_chunks * SC_CHUNK * n_workers == n_idx and n_chunks % 2 == 0
    mesh = plsc.VectorSubcoreMesh(core_axis_name="c", subcore_axis_name="s",
                                  num_cores=SC_CORES, num_subcores=SC_SUBCORES)

    def body(table_hbm, idx_hbm, out_hbm, idx_v, rows_v, gsem, wsem):
        wid = lax.axis_index("s") * SC_CORES + lax.axis_index("c")
        pltpu.sync_copy(idx_hbm.at[wid], idx_v)

        def gather(ci, slot):
            return pltpu.make_async_copy(table_hbm.at[idx_v.at[ci]], rows_v.at[slot], gsem.at[slot])

        def write(ci, slot):
            return pltpu.make_async_copy(rows_v.at[slot], out_hbm.at[ci, wid], wsem.at[slot])

        gather(0, 0).start()

        @pl.loop(0, n_chunks, step=2)
        def _(c0):
            for slot in range(2):
                ci = c0 + slot
                other = 1 - slot

                @pl.when(ci + 1 < n_chunks)
                def _():
                    @pl.when(ci >= 1)
                    def _():
                        write(ci - 1, other).wait()
                    gather(ci + 1, other).start()

                gather(ci, slot).wait()
                write(ci, slot).start()

        write(n_chunks - 2, 0).wait()
        write(n_chunks - 1, 1).wait()

    out = pl.kernel(
        body,
        out_type=jax.ShapeDtypeStruct((n_chunks, n_workers, SC_CHUNK, width), table.dtype),
        mesh=mesh,
        scratch_types=[pltpu.VMEM((n_chunks, SC_CHUNK), jnp.int32),
                       pltpu.VMEM((2, SC_CHUNK, width), table.dtype),
                       pltpu.SemaphoreType.DMA((2,)),
                       pltpu.SemaphoreType.DMA((2,))],
        name="sc_gather",
    )(table, idx.reshape(n_chunks, n_workers, SC_CHUNK).transpose(1, 0, 2))
    return out.reshape(n_idx, width)


def _combine_kernel(x_ref, *refs):
    y_refs, (gate_ref, gf_ref, o_ref) = refs[:TOP_K], refs[TOP_K:]
    gates = gate_ref[...]
    acc = gates[:, 0:1] * _unpack_rows(y_refs[0][...], F32)
    for kk in range(1, TOP_K):
        acc = acc + gates[:, kk:kk + 1] * _unpack_rows(y_refs[kk][...], F32)
    o_ref[...] = x_ref[...] + gf_ref[0] * acc


def moe_combine(lay, x, yg, gates, mod, drop_ctx):
    D = D_MODEL
    if drop_ctx:
        nbo = lay.nb - 1
        src = lambda r: (r // nbo) * lay.nb + 1 + r % nbo
        n_out = lay.n_batch * nbo
    else:
        src = lambda r: r
        n_out = lay.n_blocks
    y_specs = [pl.BlockSpec((TL, PACK_W), functools.partial(lambda kk, r: (kk * lay.n_blocks + src(r), 0), kk))
               for kk in range(TOP_K)]
    return pl.pallas_call(
        _combine_kernel,
        grid=(n_out,),
        in_specs=[pl.BlockSpec((TL, D), lambda r: (src(r), 0))] + y_specs + [
            pl.BlockSpec((TL, TOP_K), lambda r: (src(r), 0)),
            pl.BlockSpec((1, 1, D), lambda r: (lay.mod_row(src(r)), 0, 5))],
        out_specs=pl.BlockSpec((TL, D), lambda r: (r, 0)),
        out_shape=jax.ShapeDtypeStruct((n_out * TL, D), F32),
        compiler_params=_params(1),
        name="moe_combine",
    )(x, yg, yg, yg, yg, gates, mod)


def moe_route(top_e, rank, counts):
    T = top_e.shape[1]
    counts = counts.reshape(N_EXPERTS).astype(jnp.int32)
    group_start = jnp.cumsum(counts) - counts
    padded = (counts + MOE_BLOCK - 1) // MOE_BLOCK * MOE_BLOCK
    padded_end = jnp.cumsum(padded)
    padded_start = padded_end - padded
    experts = jnp.arange(N_EXPERTS)
    start_of = jnp.sum(jnp.where(top_e[..., None] == experts, padded_start, 0), axis=-1)
    dest = start_of + rank
    n_rows = -(-(T * TOP_K) // MOE_BLOCK) * MOE_BLOCK + N_EXPERTS * MOE_BLOCK
    n_blk = n_rows // MOE_BLOCK
    blk_start = jnp.arange(n_blk) * MOE_BLOCK
    blk_e = jnp.minimum(jnp.sum(padded_end[None, :] <= blk_start[:, None], axis=1), N_EXPERTS - 1)
    order = jnp.argsort(top_e.T.reshape(-1))
    blk_hot = blk_e[:, None] == experts
    blk_src0 = jnp.sum(jnp.where(blk_hot, group_start - padded_start, 0), axis=-1) + blk_start
    pos = (blk_src0[:, None] + jnp.arange(MOE_BLOCK)[None, :]).reshape(-1)
    src_flat = order[jnp.clip(pos, 0, T * TOP_K - 1)]
    blk_e = blk_e.astype(jnp.int32)
    blk_first = jnp.concatenate([jnp.ones((1,), jnp.int32), (blk_e[1:] != blk_e[:-1]).astype(jnp.int32)])
    n_used = (padded_end[-1:] // MOE_BLOCK).astype(jnp.int32)
    return ((src_flat // TOP_K).astype(jnp.int32), dest.reshape(-1).astype(jnp.int32), blk_e, blk_first,
            n_used)


def kernel(x, c, ctx, c_ctx, norm_mix, norm_ffn, w_mod, b_mod, conv_w_in, conv_w, conv_w_out, ml_w_in, ml_b_gate, ml_norm, ml_w_out, mla_w_in, mla_q_norm, mla_kv_norm, mla_w_uq, mla_w_ukv, mla_qn_nope, mla_qn_rope, mla_kn_nope, mla_kn_rope, mla_w_out, moe_w_router, moe_b_router, moe_w1, moe_b1, moe_w2, moe_b2):
    Bn, n_lat, D = x.shape
    n_ctx = ctx.shape[1]
    assert D == D_MODEL and n_ctx == TL and n_lat % TL == 0
    assert (DEPTH - 1) % N_MIXERS == 0
    full = Layout(Bn, (n_ctx + n_lat) // TL, True)
    lat_only = Layout(Bn, n_lat // TL, False)
    mods = ada_all(c, c_ctx, w_mod, b_mod)
    tables = rope_tables(n_ctx, n_lat)
    b1_all = moe_b1.reshape(DEPTH, N_EXPERTS, 1, 2 * MOE_FF)
    b2_all = moe_b2.reshape(DEPTH, N_EXPERTS, 1, D)
    X = jnp.concatenate([ctx, x], axis=1).reshape(full.rows, D)
    for layer in range(DEPTH):
        kind, j = layer % N_MIXERS, layer // N_MIXERS
        last = layer == DEPTH - 1
        lay = lat_only if last else full
        mod = mods[layer]
        gain_a = norm_mix[layer].reshape(1, D)
        gain_f = norm_ffn[layer].reshape(1, D)
        if kind == 0:
            bg, v = conv_in(lay, X, gain_a, mod, conv_w_in[j].astype(BF16))
            pro, w_out, name = (v, bg, conv_w[j]), conv_w_out[j], "conv"
        elif kind == 1:
            w = ml_w_in[j]
            n_main = 2 * ML_QK + 2 * ML_V
            w_main = jnp.concatenate([w[:, :ML_QK] * ML_DQK ** -0.5, w[:, ML_QK:n_main]], axis=1)
            q, k, v, og, g, gt = mlstm_in(lay, X, gain_a, mod, w_main.astype(BF16), w[:, n_main:],
                                          ml_b_gate[j])
            h_f, h_b = mlstm_scan(lay, q, k, v, g, gt)
            pro, w_out, name = (h_f, h_b, og, ml_norm[j]), ml_w_out[j], "mlstm"
        else:
            q, k, v = mla_in(lay, X, gain_a, mod, mla_w_in[j], mla_q_norm[j], mla_kv_norm[j],
                             mla_w_uq[j], mla_w_ukv[j], mla_qn_nope[j], mla_qn_rope[j],
                             mla_kn_nope[j], mla_kn_rope[j], tables)
            pro, w_out, name = (mla_attention(lay, q, k, v),), mla_w_out[j], "mla"
        X, h2, top_e, gates, rank, counts = mixer_out(
            lay, name, pro, w_out, X, mod, gain_f, moe_w_router[layer], moe_b_router[layer])
        src_tok, dest, blk_e, blk_first, n_used = moe_route(top_e, rank, counts)
        xp = sc_gather(h2, src_tok)
        yp = expert_ffn(layer, xp, blk_e, blk_first, n_used, moe_w1, b1_all, moe_w2, b2_all)
        yg = sc_gather(yp, dest)
        X = moe_combine(lay, X, yg, gates.T, mod, drop_ctx=(layer == DEPTH - 2))
    return X.reshape(Bn, n_lat, D)
```

```python
import functools

import jax
import jax.numpy as jnp
from jax import lax
from jax.experimental import pallas as pl
from jax.experimental.pallas import tpu as pltpu
from jax.experimental.pallas import tpu_sc as plsc

D_MODEL = 1024
DEPTH = 4
GRID_W = 64
N_MIXERS = 3
N_ADA = 6
RMS_EPS = 1e-6
CONV_WIDTH = 3
ML_HEADS = 8
ML_DQK = 64
ML_DV = 128
ML_QK = ML_HEADS * ML_DQK
ML_V = ML_HEADS * ML_DV
GATE_CAP = 15.0
MLA_HEADS = 8
MLA_NOPE = 128
MLA_ROPE = 64
MLA_V = 128
MLA_Q_LORA = 384
MLA_KV_LORA = 256
MLA_SCALE = (MLA_NOPE + MLA_ROPE) ** -0.5
ROPE_THETA = 10000.0
N_EXPERTS = 32
TOP_K = 4
MOE_FF = D_MODEL
SWIGLU_ALPHA = 1.702
SWIGLU_LIMIT = 7.0
MOE_BLOCK = 512

TL = 256
LANES = 128
BF16_ROWS = 16
VMEM_LIMIT = 48 * 1024 * 1024
HI = lax.Precision.HIGHEST
F32 = jnp.float32
BF16 = jnp.bfloat16


def _params(n_axes):
    return pltpu.CompilerParams(dimension_semantics=("arbitrary",) * n_axes,
                                vmem_limit_bytes=VMEM_LIMIT)


def _rms(x, width=None):
    width = x.shape[-1] if width is None else width
    return x * lax.rsqrt(jnp.sum(x * x, axis=-1, keepdims=True) * (1.0 / width) + RMS_EPS)


def _norm_mod(x, gain, scale, shift):
    return _rms(x) * gain * (1.0 + scale) + shift


def _split_weight_t(w):
    hi = w.astype(BF16)
    lo = (w - hi.astype(F32)).astype(BF16)
    return jnp.concatenate([hi.T, lo.T], axis=0)


def _split_dot_t(w2, h):
    n = w2.shape[0] // 2
    dn = (((1,), (1,)), ((), ()))
    h_hi = h.astype(BF16)
    h_lo = (h - h_hi.astype(F32)).astype(BF16)
    both = lax.dot_general(w2, h_hi, dn, preferred_element_type=F32)
    cross = lax.dot_general(w2[:n], h_lo, dn, preferred_element_type=F32)
    return both[:n] + both[n:] + cross


PACK_W = D_MODEL // 2
HIGH_HALF = -65536


def _pack_rows(x):
    xb = x.astype(BF16).astype(F32)
    lo = lax.bitcast_convert_type(xb[:, :PACK_W], jnp.int32)
    hi = lax.bitcast_convert_type(xb[:, PACK_W:], jnp.int32)
    return hi | lax.shift_right_logical(lo, 16)


def _unpack_rows(w, dtype):
    lo = lax.bitcast_convert_type(lax.shift_left(w, 16), F32)
    hi = lax.bitcast_convert_type(w & HIGH_HALF, F32)
    return jnp.concatenate([lo.astype(dtype), hi.astype(dtype)], axis=1)


class Layout:
    def __init__(self, n_batch, nb, ctx_first):
        self.n_batch, self.nb, self.ctx_first = n_batch, nb, ctx_first
        self.n_blocks = n_batch * nb
        self.rows = self.n_blocks * TL

    def mod_row(self, r):
        b = r // self.nb
        return jnp.where(r % self.nb == 0, self.n_batch, b) if self.ctx_first else b

    def row_spec(self, width):
        return pl.BlockSpec((TL, width), lambda r: (r, 0))

    def mod_spec(self, piece):
        return pl.BlockSpec((1, 1, D_MODEL), lambda r: (self.mod_row(r), 0, piece))


def _const_spec(shape):
    return pl.BlockSpec(shape, lambda *_: (0,) * len(shape))


ADA_ROWS = 16
ADA_TN = 1536


def _ada_kernel(c_ref, w_ref, b_ref, o_ref):
    c = c_ref[...]
    s = c * jax.nn.sigmoid(c)
    o_ref[0] = jnp.dot(s, w_ref[0], precision=HI, preferred_element_type=F32) + b_ref[0]


def ada_all(c, c_ctx, w_mod, b_mod):
    Bn, D = c.shape
    cond = jnp.zeros((ADA_ROWS, D), F32).at[:Bn].set(c).at[Bn].set(c_ctx)
    out = pl.pallas_call(
        _ada_kernel,
        grid=(DEPTH, N_ADA * D // ADA_TN),
        in_specs=[pl.BlockSpec((ADA_ROWS, D), lambda l, n: (0, 0)),
                  pl.BlockSpec((1, D, ADA_TN), lambda l, n: (l, 0, n)),
                  pl.BlockSpec((1, 1, ADA_TN), lambda l, n: (l, 0, n))],
        out_specs=pl.BlockSpec((1, ADA_ROWS, ADA_TN), lambda l, n: (l, 0, n)),
        out_shape=jax.ShapeDtypeStruct((DEPTH, ADA_ROWS, N_ADA * D), F32),
        compiler_params=_params(2),
        name="ada_mod",
    )(cond, w_mod, b_mod.reshape(DEPTH, 1, N_ADA * D))
    return out[:, :Bn + 1, None, :]


def _conv_in_kernel(x_ref, gain_ref, sc_ref, sh_ref, w_ref, bg_ref, v_ref):
    D = D_MODEL
    h = _norm_mod(x_ref[...], gain_ref[...], sc_ref[0], sh_ref[0]).astype(BF16)
    p = jnp.dot(h, w_ref[...], preferred_element_type=F32)
    bg_ref[...] = p[:, :D].astype(BF16)
    v_ref[...] = (p[:, D:2 * D] * p[:, 2 * D:]).astype(BF16)


def conv_in(lay, x, gain, mod, w_in):
    D = D_MODEL
    sds = jax.ShapeDtypeStruct((lay.rows, D), BF16)
    return pl.pallas_call(
        _conv_in_kernel,
        grid=(lay.n_blocks,),
        in_specs=[lay.row_spec(D), _const_spec((1, D)), lay.mod_spec(1), lay.mod_spec(0),
                  _const_spec((D, 3 * D))],
        out_specs=[lay.row_spec(D), lay.row_spec(D)],
        out_shape=[sds, sds],
        compiler_params=_params(1),
        name="conv_in",
    )(x, gain, mod, mod, w_in)


ML_T = 256
ML_SW = 2 * ML_DV
ML_NG = 4 * ML_HEADS


LOG2E = 1.4426950408889634


def _gate_act(g, is_forget):
    g = GATE_CAP * jnp.tanh(g * (1.0 / GATE_CAP))
    log_sig = jnp.minimum(g, 0.0) - jnp.log(1.0 + jnp.exp(-jnp.abs(g)))
    return jnp.where(is_forget, log_sig, g) * LOG2E


def _mlstm_in_kernel(x_ref, gain_ref, sc_ref, sh_ref, w_ref, wg_ref, bgt_ref,
                     q_ref, k_ref, v_ref, og_ref, g_ref, gt_ref):
    h = _norm_mod(x_ref[...], gain_ref[...], sc_ref[0], sh_ref[0])
    p = jnp.dot(h.astype(BF16), w_ref[...], preferred_element_type=F32)
    q_ref[...] = p[:, :ML_QK].astype(BF16)
    k_ref[...] = p[:, ML_QK:2 * ML_QK].astype(BF16)
    v_ref[...] = p[:, 2 * ML_QK:2 * ML_QK + ML_V].astype(BF16)
    og_ref[...] = jax.nn.sigmoid(p[:, 2 * ML_QK + ML_V:]).astype(BF16)
    gt = _split_dot_t(wg_ref[...], h) + bgt_ref[...]
    row = lax.broadcasted_iota(jnp.int32, gt.shape, 0)
    gt = _gate_act(gt, (row // ML_HEADS) % 2 == 1)
    gt_ref[...] = gt
    eye = (lax.broadcasted_iota(jnp.int32, (TL, TL), 0)
           == lax.broadcasted_iota(jnp.int32, (TL, TL), 1)).astype(F32)
    g_ref[...] = lax.dot_general(eye, gt, (((1,), (1,)), ((), ())), precision=HI,
                                 preferred_element_type=F32)


def mlstm_in(lay, x, gain, mod, w_main, w_g, b_g):
    D = D_MODEL
    n_main = 2 * ML_QK + 2 * ML_V
    bf = lambda w: jax.ShapeDtypeStruct((lay.rows, w), BF16)
    return pl.pallas_call(
        _mlstm_in_kernel,
        grid=(lay.n_blocks,),
        in_specs=[lay.row_spec(D), _const_spec((1, D)), lay.mod_spec(1), lay.mod_spec(0),
                  _const_spec((D, n_main)), _const_spec((2 * ML_NG, D)), _const_spec((ML_NG, 1))],
        out_specs=[lay.row_spec(ML_QK), lay.row_spec(ML_QK), lay.row_spec(ML_V), lay.row_spec(ML_V),
                   lay.row_spec(ML_NG), pl.BlockSpec((ML_NG, TL), lambda r: (0, r))],
        out_shape=[bf(ML_QK), bf(ML_QK), bf(ML_V), bf(ML_V),
                   jax.ShapeDtypeStruct((lay.rows, ML_NG), F32),
                   jax.ShapeDtypeStruct((ML_NG, lay.rows), F32)],
        compiler_params=_params(1),
        name="mlstm_in",
    )(x, gain, mod, mod, w_main, _split_weight_t(w_g), b_g.reshape(ML_NG, 1))


def _mlstm_dir(reverse, q_ref, k_ref, v_ref, g_ref, gt_ref, o_ref, s_ref, m_ref):
    T = ML_T
    row = lax.broadcasted_iota(jnp.int32, (T, T), 0)
    col = lax.broadcasted_iota(jnp.int32, (T, T), 1)
    mask = (col >= row) if reverse else (col <= row)
    g = g_ref[...]
    gt = gt_ref[...]
    tri_c = mask.astype(F32)
    tri_r = ((row >= col) if reverse else (row <= col)).astype(F32)
    bc = jnp.dot(tri_c, g, precision=HI, preferred_element_type=F32)
    br = jnp.dot(gt, tri_r, precision=HI, preferred_element_type=F32)
    gi, gf = (2 * ML_HEADS, 3 * ML_HEADS) if reverse else (0, ML_HEADS)
    lane = lax.broadcasted_iota(jnp.int32, (T, 2 * ML_DQK), 1)
    ones_col = (lax.broadcasted_iota(jnp.int32, (T, ML_DV), 1) == 0).astype(BF16)
    for h in range(ML_HEADS):
        pair = (h // 2) * 2 * ML_DQK
        own = (lane >= ML_DQK) if (h % 2) else (lane < ML_DQK)
        qp = q_ref[:, pair:pair + 2 * ML_DQK]
        kp = k_ref[:, pair:pair + 2 * ML_DQK]
        qm = jnp.where(own, qp, jnp.zeros_like(qp))
        km = jnp.where(own, kp, jnp.zeros_like(kp))
        v_ext = jnp.concatenate([v_ref[:, h * ML_DV:(h + 1) * ML_DV], ones_col], axis=1)
        b_col = bc[:, gf + h:gf + h + 1]
        i_col = g[:, gi + h:gi + h + 1]
        b_row = br[gf + h:gf + h + 1, :]
        i_row = gt[gi + h:gi + h + 1, :]
        tot = b_row[:, 0:1] if reverse else b_row[:, T - 1:T]
        m_prev = m_ref[h][0:1, 0:1]
        s_prev = s_ref[h]
        d = jnp.where(mask, b_col - b_row + i_row, -jnp.inf)
        inter = b_col + m_prev
        m_t = jnp.maximum(inter, jnp.max(d, axis=1, keepdims=True))
        s_raw = lax.dot_general(qm, km, (((1,), (1,)), ((), ())), preferred_element_type=F32)
        p = (s_raw * jnp.exp2(d - m_t)).astype(BF16)
        a = jnp.exp2(inter - m_t)
        r = (jnp.dot(p, v_ext, preferred_element_type=F32)
             + a * jnp.dot(qm, s_prev.astype(BF16), preferred_element_type=F32))
        den = r[:, ML_DV:ML_DV + 1]
        o_ref[:, h * ML_DV:(h + 1) * ML_DV] = (
            r[:, :ML_DV] / jnp.maximum(jnp.abs(den), jnp.exp2(-m_t))).astype(o_ref.dtype)
        g_col = tot - b_col + i_col
        m_new = jnp.maximum(tot + m_prev, jnp.max(g_col, axis=0, keepdims=True))
        decay = jnp.exp2(tot + m_prev - m_new)
        wv = (jnp.exp2(g_col - m_new) * v_ext.astype(F32)).astype(BF16)
        s_ref[h] = decay * s_prev + lax.dot_general(
            km, wv, (((0,), (0,)), ((), ())), preferred_element_type=F32)
        m_ref[h] = jnp.broadcast_to(m_new, m_ref.shape[1:])


def _mlstm_scan_kernel(qf_ref, kf_ref, vf_ref, gf_ref, gtf_ref, qb_ref, kb_ref, vb_ref, gb_ref, gtb_ref,
                       of_ref, ob_ref, sf_ref, mf_ref, sb_ref, mb_ref):
    @pl.when(pl.program_id(1) == 0)
    def _():
        sf_ref[...] = jnp.zeros_like(sf_ref)
        mf_ref[...] = jnp.zeros_like(mf_ref)
        sb_ref[...] = jnp.zeros_like(sb_ref)
        mb_ref[...] = jnp.zeros_like(mb_ref)

    _mlstm_dir(False, qf_ref, kf_ref, vf_ref, gf_ref, gtf_ref, of_ref, sf_ref, mf_ref)
    _mlstm_dir(True, qb_ref, kb_ref, vb_ref, gb_ref, gtb_ref, ob_ref, sb_ref, mb_ref)


def mlstm_scan(lay, q, k, v, g, gt):
    assert lay.ctx_first
    per = TL // ML_T
    nb = lay.nb * per
    rev = lambda j: jnp.where(j < per, per - 1 - j, nb + per - 1 - j)
    fwd = lambda b, j: (b * nb + j, 0)
    bwd = lambda b, j: (b * nb + rev(j), 0)
    fwd_t = lambda b, j: (0, b * nb + j)
    bwd_t = lambda b, j: (0, b * nb + rev(j))

    def specs(im, imt):
        return [pl.BlockSpec((ML_T, ML_QK), im), pl.BlockSpec((ML_T, ML_QK), im),
                pl.BlockSpec((ML_T, ML_V), im), pl.BlockSpec((ML_T, ML_NG), im),
                pl.BlockSpec((ML_NG, ML_T), imt)]

    out_sds = jax.ShapeDtypeStruct((lay.rows, ML_V), F32)
    state = [pltpu.VMEM((ML_HEADS, 2 * ML_DQK, ML_SW), F32),
             pltpu.VMEM((ML_HEADS, 8, LANES), F32)]
    return pl.pallas_call(
        _mlstm_scan_kernel,
        grid=(lay.n_batch, nb),
        in_specs=specs(fwd, fwd_t) + specs(bwd, bwd_t),
        out_specs=[pl.BlockSpec((ML_T, ML_V), fwd), pl.BlockSpec((ML_T, ML_V), bwd)],
        out_shape=[out_sds, out_sds],
        scratch_shapes=state + state,
        compiler_params=_params(2),
        name="mlstm_scan",
    )(q, k, v, g, gt, q, k, v, g, gt)


MLA_QK_PAD = 256
MLA_VW = 2 * MLA_V
MLA_IN_PAD = MLA_Q_LORA + MLA_KV_LORA + LANES
ROPE_HALF = MLA_ROPE // 4


def _mla_in_kernel(x_ref, gain_ref, sc_ref, sh_ref, win_ref, wuq_ref, wukv_ref, qn_ref, kvn_ref,
                   qnn_ref, qnr_ref, knn_ref, knr_ref, cos_ref, sa_ref, sb_ref,
                   q_out, k_out, v_out):
    h = _norm_mod(x_ref[...], gain_ref[...], sc_ref[0], sh_ref[0]).astype(BF16)
    p = jnp.dot(h, win_ref[...], preferred_element_type=F32)
    cq = _rms(p[:, :MLA_Q_LORA]) * qn_ref[...]
    ckv = _rms(p[:, MLA_Q_LORA:MLA_Q_LORA + MLA_KV_LORA]) * kvn_ref[...]
    kr = p[:, MLA_Q_LORA + MLA_KV_LORA:]
    q = jnp.dot(cq.astype(BF16), wuq_ref[...], preferred_element_type=F32)
    kv = jnp.dot(ckv.astype(BF16), wukv_ref[...], preferred_element_type=F32)
    nv = MLA_HEADS * MLA_NOPE
    ones_col = (lax.broadcasted_iota(jnp.int32, (TL, MLA_VW - MLA_V), 1) == 0).astype(BF16)
    cos, sa, sb = cos_ref[...], sa_ref[...], sb_ref[...]

    def rope(xp):
        return (xp * cos + pltpu.roll(xp, LANES - ROPE_HALF, 1) * sa + pltpu.roll(xp, ROPE_HALF, 1) * sb)

    kr = rope(_rms(kr, MLA_ROPE) * knr_ref[...]).astype(BF16)
    for hd in range(MLA_HEADS):
        c0 = hd * MLA_QK_PAD
        qn = _rms(q[:, c0:c0 + MLA_NOPE]) * qnn_ref[...]
        qr = rope(_rms(q[:, c0 + MLA_NOPE:c0 + MLA_QK_PAD], MLA_ROPE) * qnr_ref[...])
        q_out[:, c0:c0 + MLA_NOPE] = (qn * (MLA_SCALE * LOG2E)).astype(BF16)
        q_out[:, c0 + MLA_NOPE:c0 + MLA_QK_PAD] = (qr * (MLA_SCALE * LOG2E)).astype(BF16)
        v0 = nv + hd * MLA_V
        v_out[:, hd * MLA_VW:hd * MLA_VW + MLA_V] = kv[:, v0:v0 + MLA_V].astype(BF16)
        v_out[:, hd * MLA_VW + MLA_V:(hd + 1) * MLA_VW] = ones_col
        kn = _rms(kv[:, hd * MLA_NOPE:(hd + 1) * MLA_NOPE]) * knn_ref[...]
        k_out[:, c0:c0 + MLA_NOPE] = kn.astype(BF16)
        k_out[:, c0 + MLA_NOPE:c0 + MLA_QK_PAD] = kr


def _pad_lanes(g):
    return jnp.pad(g, (0, LANES - g.shape[0])).reshape(1, LANES)


def mla_in(lay, x, gain, mod, w_in, q_norm, kv_norm, w_uq, w_ukv, qn_nope, qn_rope, kn_nope, kn_rope,
           tables):
    D = D_MODEL
    Hn = MLA_HEADS
    win = jnp.pad(w_in, ((0, 0), (0, MLA_IN_PAD - w_in.shape[1]))).astype(BF16)
    wuq = jnp.pad(w_uq.reshape(MLA_Q_LORA, Hn, MLA_NOPE + MLA_ROPE),
                  ((0, 0), (0, 0), (0, MLA_QK_PAD - MLA_NOPE - MLA_ROPE)))
    wuq = wuq.reshape(MLA_Q_LORA, Hn * MLA_QK_PAD).astype(BF16)
    wkv = w_ukv.reshape(MLA_KV_LORA, Hn, MLA_NOPE + MLA_V)
    wukv = jnp.concatenate([wkv[:, :, :MLA_NOPE].reshape(MLA_KV_LORA, Hn * MLA_NOPE),
                            wkv[:, :, MLA_NOPE:].reshape(MLA_KV_LORA, Hn * MLA_V)], axis=1).astype(BF16)
    nb = lay.nb
    tab_spec = pl.BlockSpec((TL, LANES), lambda r: (r % nb, 0))
    bf = lambda w: jax.ShapeDtypeStruct((lay.rows, w), BF16)
    return pl.pallas_call(
        _mla_in_kernel,
        grid=(lay.n_blocks,),
        in_specs=[lay.row_spec(D), _const_spec((1, D)), lay.mod_spec(1), lay.mod_spec(0),
                  _const_spec(win.shape), _const_spec(wuq.shape), _const_spec(wukv.shape),
                  _const_spec((1, MLA_Q_LORA)), _const_spec((1, MLA_KV_LORA)),
                  _const_spec((1, LANES)), _const_spec((1, LANES)), _const_spec((1, LANES)),
                  _const_spec((1, LANES)), tab_spec, tab_spec, tab_spec],
        out_specs=[lay.row_spec(Hn * MLA_QK_PAD), lay.row_spec(Hn * MLA_QK_PAD), lay.row_spec(Hn * MLA_VW)],
        out_shape=[bf(Hn * MLA_QK_PAD), bf(Hn * MLA_QK_PAD), bf(Hn * MLA_VW)],
        compiler_params=_params(1),
        name="mla_in",
    )(x, gain, mod, mod, win, wuq, wukv, q_norm.reshape(1, -1), kv_norm.reshape(1, -1),
      qn_nope.reshape(1, -1), _pad_lanes(qn_rope), kn_nope.reshape(1, -1), _pad_lanes(kn_rope), *tables)


def rope_tables(n_ctx, n_lat):
    n_freq = MLA_ROPE // 4
    inv = ROPE_THETA ** (-jnp.arange(n_freq, dtype=F32) / n_freq)
    t = jnp.arange(n_lat)
    a_r = (t // GRID_W).astype(F32)[:, None] * inv
    a_c = (t % GRID_W).astype(F32)[:, None] * inv
    ang = jnp.concatenate([a_r, a_r, a_c, a_c], axis=-1)
    ang = jnp.concatenate([jnp.zeros((n_ctx, MLA_ROPE), F32), ang], axis=0)
    cos, sin = jnp.cos(ang), jnp.sin(ang)
    low = (jnp.arange(MLA_ROPE) % (2 * ROPE_HALF)) < ROPE_HALF
    pad = lambda a: jnp.pad(a, ((0, 0), (0, LANES - MLA_ROPE)))
    return pad(cos), pad(jnp.where(low, -sin, 0.0)), pad(jnp.where(low, 0.0, sin))


MLA_HPS = 2


def _mla_attn_kernel(q_ref, k_ref, v_ref, o_ref, *, n_ctx):
    def attend(n_keys):
        for hd in range(MLA_HPS):
            q = q_ref[0, :, hd * MLA_QK_PAD:(hd + 1) * MLA_QK_PAD]
            k = k_ref[0, :n_keys, hd * MLA_QK_PAD:(hd + 1) * MLA_QK_PAD]
            v = v_ref[0, :n_keys, hd * MLA_VW:(hd + 1) * MLA_VW]
            s = lax.dot_general(q, k, (((1,), (1,)), ((), ())), preferred_element_type=F32)
            p = jnp.exp2(s - s.max(axis=1, keepdims=True)).astype(BF16)
            r = jnp.dot(p, v, preferred_element_type=F32)
            o_ref[0, :, hd * MLA_V:(hd + 1) * MLA_V] = (
                r[:, :MLA_V] / r[:, MLA_V:MLA_V + 1]).astype(o_ref.dtype)

    @pl.when(pl.program_id(2) == 0)
    def _():
        attend(n_ctx)

    @pl.when(pl.program_id(2) > 0)
    def _():
        attend(k_ref.shape[1])


def mla_attention(lay, q, k, v):
    assert lay.ctx_first
    Bn, S = lay.n_batch, lay.nb * TL
    q3, k3, v3 = (a.reshape(Bn, S, a.shape[-1]) for a in (q, k, v))
    out = pl.pallas_call(
        functools.partial(_mla_attn_kernel, n_ctx=TL),
        grid=(Bn, MLA_HEADS // MLA_HPS, lay.nb),
        in_specs=[pl.BlockSpec((1, TL, MLA_HPS * MLA_QK_PAD), lambda b, h, i: (b, i, h)),
                  pl.BlockSpec((1, S, MLA_HPS * MLA_QK_PAD), lambda b, h, i: (b, 0, h)),
                  pl.BlockSpec((1, S, MLA_HPS * MLA_VW), lambda b, h, i: (b, 0, h))],
        out_specs=pl.BlockSpec((1, TL, MLA_HPS * MLA_V), lambda b, h, i: (b, i, h)),
        out_shape=jax.ShapeDtypeStruct((Bn, S, MLA_HEADS * MLA_V), BF16),
        compiler_params=_params(3),
        name="mla_attention",
    )(q3, k3, v3)
    return out.reshape(lay.rows, MLA_HEADS * MLA_V)


N_PROLOGUE = {"conv": 5, "mlstm": 4, "mla": 1}


def _mixer_out_kernel(*refs, kind, nb, ctx_first):
    n_pro = N_PROLOGUE[kind]
    pro = refs[:n_pro]
    (wout_ref, x_ref, ga_ref, gain_ref, sc_ref, sh_ref, wr_ref, br_ref,
     xo_ref, h2_ref, te_ref, gate_ref, rank_ref, cnt_ref, carry_ref) = refs[n_pro:]
    r = pl.program_id(0)

    if kind == "conv":
        vprev_ref, v_ref, vnext_ref, bg_ref, cw_ref = pro
        j = r % nb
        first = (j == 0) | (j == 1) if ctx_first else (j == 0)
        last = (j == nb - 1) | (j == 0) if ctx_first else (j == nb - 1)
        v = v_ref[...].astype(F32)
        rows = lax.broadcasted_iota(jnp.int32, (TL, 1), 0)
        prev_row = jnp.where(first, 0.0, vprev_ref[BF16_ROWS - 1:BF16_ROWS, :].astype(F32))
        next_row = jnp.where(last, 0.0, vnext_ref[0:1, :].astype(F32))
        up = jnp.where(rows == 0, prev_row, pltpu.roll(v, 1, 0))
        dn = jnp.where(rows == TL - 1, next_row, pltpu.roll(v, TL - 1, 0))
        cw = cw_ref[...]
        a = bg_ref[...].astype(F32) * (up * cw[0:1] + v * cw[1:2] + dn * cw[2:3])
    elif kind == "mlstm":
        hf_ref, hb_ref, og_ref, ng_ref = pro
        hh = hf_ref[...] + hb_ref[...]
        a = jnp.concatenate([_rms(hh[:, h * ML_DV:(h + 1) * ML_DV]) for h in range(ML_HEADS)], axis=1)
        a = a * ng_ref[...] * og_ref[...].astype(F32)
    else:
        a = pro[0][...]

    y = jnp.dot(a.astype(BF16), wout_ref[...], preferred_element_type=F32)
    xn = x_ref[...] + ga_ref[0] * y
    xo_ref[...] = xn
    h2 = _norm_mod(xn, gain_ref[...], sc_ref[0], sh_ref[0])
    h2_ref[...] = _pack_rows(h2)
    logits = _split_dot_t(wr_ref[...], h2) + br_ref[...]

    sub = lax.broadcasted_iota(jnp.int32, (N_EXPERTS, TL), 0)
    sub_k = lax.broadcasted_iota(jnp.int32, (TOP_K, TL), 0)
    work = logits
    sel = jnp.zeros((N_EXPERTS, TL), F32)
    top_e = jnp.zeros((TOP_K, TL), jnp.int32)
    top_v = jnp.zeros((TOP_K, TL), F32)
    picks = []
    for kk in range(TOP_K):
        m = work.max(axis=0, keepdims=True)
        idx = jnp.min(jnp.where(work == m, sub, N_EXPERTS), axis=0, keepdims=True)
        hit = sub == idx
        picks.append(hit)
        sel = jnp.where(hit, 1.0, sel)
        work = jnp.where(hit, -jnp.inf, work)
        top_e = jnp.where(sub_k == kk, idx, top_e)
        top_v = jnp.where(sub_k == kk, m, top_v)
    ex = jnp.exp(top_v - top_v[0:1])
    gate_ref[...] = ex / ex.sum(axis=0, keepdims=True)
    te_ref[...] = top_e

    @pl.when(r == 0)
    def _():
        carry_ref[...] = jnp.zeros_like(carry_ref)

    tr = lax.broadcasted_iota(jnp.int32, (TL, TL), 0)
    tc = lax.broadcasted_iota(jnp.int32, (TL, TL), 1)
    before = jnp.dot(sel.astype(BF16), (tr < tc).astype(BF16), preferred_element_type=F32)
    pos = before + carry_ref[...]
    rank = jnp.zeros((TOP_K, TL), F32)
    for kk in range(TOP_K):
        rk = jnp.sum(jnp.where(picks[kk], pos, 0.0), axis=0, keepdims=True)
        rank = jnp.where(sub_k == kk, rk, rank)
    rank_ref[...] = rank.astype(jnp.int32)
    total = carry_ref[...] + jnp.sum(sel, axis=1, keepdims=True)
    carry_ref[...] = total
    cnt_ref[...] = total


def mixer_out(lay, kind, pro_args, w_out, x, mod, gain_f, w_r, b_r):
    D = D_MODEL
    nb = lay.nb
    if kind == "conv":
        v, bg, cw = pro_args
        per = TL // BF16_ROWS
        last_tile = lay.rows // BF16_ROWS - 1
        pro_specs = [pl.BlockSpec((BF16_ROWS, D), lambda r: (jnp.maximum(r * per - 1, 0), 0)),
                     lay.row_spec(D),
                     pl.BlockSpec((BF16_ROWS, D), lambda r: (jnp.minimum((r + 1) * per, last_tile), 0)),
                     lay.row_spec(D), _const_spec((CONV_WIDTH, D))]
        pro_in = [v, v, v, bg, cw]
    elif kind == "mlstm":
        h_f, h_b, og, ng = pro_args
        pro_specs = [lay.row_spec(ML_V), lay.row_spec(ML_V), lay.row_spec(ML_V), _const_spec((1, ML_V))]
        pro_in = [h_f, h_b, og, ng.reshape(1, ML_V)]
    else:
        pro_specs = [lay.row_spec(D)]
        pro_in = list(pro_args)
    k_in = w_out.shape[0]
    small = lambda dt: jax.ShapeDtypeStruct((TOP_K, lay.rows), dt)
    small_spec = pl.BlockSpec((TOP_K, TL), lambda r: (0, r))
    return pl.pallas_call(
        functools.partial(_mixer_out_kernel, kind=kind, nb=nb, ctx_first=lay.ctx_first),
        grid=(lay.n_blocks,),
        in_specs=pro_specs + [_const_spec((k_in, D)), lay.row_spec(D), lay.mod_spec(2),
                              _const_spec((1, D)), lay.mod_spec(4), lay.mod_spec(3),
                              _const_spec((2 * N_EXPERTS, D)), _const_spec((N_EXPERTS, 1))],
        out_specs=[lay.row_spec(D), lay.row_spec(PACK_W), small_spec, small_spec, small_spec,
                   _const_spec((N_EXPERTS, 1))],
        out_shape=[jax.ShapeDtypeStruct((lay.rows, D), F32),
                   jax.ShapeDtypeStruct((lay.rows, PACK_W), jnp.int32),
                   small(jnp.int32), small(F32), small(jnp.int32),
                   jax.ShapeDtypeStruct((N_EXPERTS, 1), F32)],
        scratch_shapes=[pltpu.VMEM((N_EXPERTS, 1), F32)],
        compiler_params=_params(1),
        name="mixer_out_" + kind,
    )(*pro_in, w_out.astype(BF16), x, mod, gain_f, mod, mod, _split_weight_t(w_r),
      b_r.reshape(N_EXPERTS, 1))


def _expert_ffn_kernel(blk_e_ref, first_ref, used_ref, x_ref, w1_ref, b1_ref, w2_ref, b2_ref, o_ref,
                       w1b_ref, w2b_ref):
    del blk_e_ref
    i = pl.program_id(0)

    @pl.when(first_ref[i] == 1)
    def _():
        w1b_ref[...] = w1_ref[0, 0].astype(BF16)
        w2b_ref[...] = w2_ref[0, 0].astype(BF16)

    @pl.when(i < used_ref[0])
    def _():
        x = _unpack_rows(x_ref[...], BF16)
        h = jnp.dot(x, w1b_ref[...], preferred_element_type=F32) + b1_ref[0, 0]
        glu = jnp.minimum(h[:, :MOE_FF], SWIGLU_LIMIT)
        lin = jnp.clip(h[:, MOE_FF:], -SWIGLU_LIMIT, SWIGLU_LIMIT)
        act = glu * jax.nn.sigmoid(SWIGLU_ALPHA * glu) * (lin + 1.0)
        y = jnp.dot(act.astype(BF16), w2b_ref[...], preferred_element_type=F32)
        o_ref[...] = _pack_rows(y + b2_ref[0, 0])

    @pl.when(i >= used_ref[0])
    def _():
        o_ref[...] = jnp.zeros_like(o_ref)


def expert_ffn(layer, xp, blk_e, blk_first, n_used, w1, b1, w2, b2):
    n_rows = xp.shape[0]
    D, F2 = D_MODEL, 2 * MOE_FF
    n_blk = n_rows // MOE_BLOCK
    grid_spec = pltpu.PrefetchScalarGridSpec(
        num_scalar_prefetch=3,
        grid=(n_blk,),
        in_specs=[
            pl.BlockSpec((MOE_BLOCK, PACK_W), lambda i, be, fi, nu: (i, 0)),
            pl.BlockSpec((1, 1, D, F2), lambda i, be, fi, nu: (layer, be[i], 0, 0)),
            pl.BlockSpec((1, 1, 1, F2), lambda i, be, fi, nu: (layer, be[i], 0, 0)),
            pl.BlockSpec((1, 1, MOE_FF, D), lambda i, be, fi, nu: (layer, be[i], 0, 0)),
            pl.BlockSpec((1, 1, 1, D), lambda i, be, fi, nu: (layer, be[i], 0, 0)),
        ],
        out_specs=pl.BlockSpec((MOE_BLOCK, PACK_W), lambda i, be, fi, nu: (i, 0)),
        scratch_shapes=[pltpu.VMEM((D, F2), BF16), pltpu.VMEM((MOE_FF, D), BF16)],
    )
    return pl.pallas_call(
        _expert_ffn_kernel,
        grid_spec=grid_spec,
        out_shape=jax.ShapeDtypeStruct((n_rows, PACK_W), jnp.int32),
        compiler_params=_params(1),
        name="expert_ffn",
    )(blk_e, blk_first, n_used, xp, w1, b1, w2, b2)


SC_CORES = 2
SC_SUBCORES = 16
SC_CHUNK = 64


def sc_gather(table, idx):
    n_idx = idx.shape[0]
    width = table.shape[1]
    n_workers = SC_CORES * SC_SUBCORES
    per_worker = n_idx // n_workers
    n_chunks = per_worker // SC_CHUNK
    assert n_chunks * SC_CHUNK * n_workers == n_idx and n_chunks % 2 == 0
    mesh = plsc.VectorSubcoreMesh(core_axis_name="c", subcore_axis_name="s",
                                  num_cores=SC_CORES, num_subcores=SC_SUBCORES)

    def body(table_hbm, idx_hbm, out_hbm, idx_v, rows_v, gsem, wsem):
        wid = lax.axis_index("s") * SC_CORES + lax.axis_index("c")
        pltpu.sync_copy(idx_hbm.at[wid], idx_v)

        def gather(ci, slot):
            return pltpu.make_async_copy(table_hbm.at[idx_v.at[ci]], rows_v.at[slot], gsem.at[slot])

        def write(ci, slot):
            return pltpu.make_async_copy(rows_v.at[slot], out_hbm.at[ci, wid], wsem.at[slot])

        gather(0, 0).start()

        @pl.loop(0, n_chunks, step=2)
        def _(c0):
            for slot in range(2):
                ci = c0 + slot
                other = 1 - slot

                @pl.when(ci + 1 < n_chunks)
                def _():
                    @pl.when(ci >= 1)
                    def _():
                        write(ci - 1, other).wait()
                    gather(ci + 1, other).start()

                gather(ci, slot).wait()
                write(ci, slot).start()

        write(n_chunks - 2, 0).wait()
        write(n_chunks - 1, 1).wait()

    out = pl.kernel(
        body,
        out_type=jax.ShapeDtypeStruct((n_chunks, n_workers, SC_CHUNK, width), table.dtype),
        mesh=mesh,
        scratch_types=[pltpu.VMEM((n_chunks, SC_CHUNK), jnp.int32),
                       pltpu.VMEM((2, SC_CHUNK, width), table.dtype),
                       pltpu.SemaphoreType.DMA((2,)),
                       pltpu.SemaphoreType.DMA((2,))],
        name="sc_gather",
    )(table, idx.reshape(n_chunks, n_workers, SC_CHUNK).transpose(1, 0, 2))
    return out.reshape(n_idx, width)


def sc_dispatch(table, dest, pad_rows):
    n_tok, width = table.shape
    n_picks = dest.shape[0]
    n_pad = pad_rows.shape[0]
    n_workers = SC_CORES * SC_SUBCORES
    per_w = n_tok // SC_CHUNK // n_workers
    pad_w = n_pad // SC_CHUNK // n_workers
    assert per_w * SC_CHUNK * n_workers == n_tok and pad_w * SC_CHUNK * n_workers == n_pad
    mesh = plsc.VectorSubcoreMesh(core_axis_name="c", subcore_axis_name="s",
                                  num_cores=SC_CORES, num_subcores=SC_SUBCORES)

    def body(table_hbm, idx_hbm, pad_hbm, zero_hbm, out_hbm, idx_v, pad_v, rows_v, zero_v,
             rsem, ssem, zsem):
        wid = lax.axis_index("s") * SC_CORES + lax.axis_index("c")
        pltpu.sync_copy(idx_hbm.at[wid], idx_v)
        pltpu.sync_copy(pad_hbm.at[wid], pad_v)
        pltpu.sync_copy(zero_hbm, zero_v)

        def zero_fill(pc):
            return pltpu.make_async_copy(zero_v, out_hbm.at[pad_v.at[pc]], zsem)

        for pc in range(pad_w):
            zero_fill(pc).start()

        def scatter(ci, kk):
            return pltpu.make_async_copy(rows_v, out_hbm.at[idx_v.at[ci * n_picks + kk]], ssem)

        @pl.loop(0, per_w)
        def _(ci):
            pltpu.async_copy(table_hbm.at[ci, wid], rows_v, rsem).wait()
            for kk in range(n_picks):
                scatter(ci, kk).start()
            for kk in range(n_picks):
                scatter(ci, kk).wait()

        for pc in range(pad_w):
            zero_fill(pc).wait()

    idx = dest.reshape(n_picks, per_w, n_workers, SC_CHUNK).transpose(2, 1, 0, 3)
    idx = idx.reshape(n_workers, per_w * n_picks, SC_CHUNK)
    return pl.kernel(
        body,
        out_type=jax.ShapeDtypeStruct((n_tok * n_picks + n_pad, width), table.dtype),
        mesh=mesh,
        scratch_types=[pltpu.VMEM((per_w * n_picks, SC_CHUNK), jnp.int32),
                       pltpu.VMEM((pad_w, SC_CHUNK), jnp.int32),
                       pltpu.VMEM((SC_CHUNK, width), table.dtype),
                       pltpu.VMEM((SC_CHUNK, width), table.dtype),
                       pltpu.SemaphoreType.DMA, pltpu.SemaphoreType.DMA, pltpu.SemaphoreType.DMA],
        name="sc_dispatch",
    )(table.reshape(per_w, n_workers, SC_CHUNK, width), idx,
      pad_rows.reshape(n_workers, pad_w, SC_CHUNK), jnp.zeros((SC_CHUNK, width), table.dtype))


def _combine_kernel(x_ref, *refs):
    y_refs, (gate_ref, gf_ref, o_ref) = refs[:TOP_K], refs[TOP_K:]
    gates = gate_ref[...]
    acc = gates[:, 0:1] * _unpack_rows(y_refs[0][...], F32)
    for kk in range(1, TOP_K):
        acc = acc + gates[:, kk:kk + 1] * _unpack_rows(y_refs[kk][...], F32)
    o_ref[...] = x_ref[...] + gf_ref[0] * acc


def moe_combine(lay, x, yg, gates, mod, drop_ctx):
    D = D_MODEL
    if drop_ctx:
        nbo = lay.nb - 1
        src = lambda r: (r // nbo) * lay.nb + 1 + r % nbo
        n_out = lay.n_batch * nbo
    else:
        src = lambda r: r
        n_out = lay.n_blocks
    y_specs = [pl.BlockSpec((TL, PACK_W), functools.partial(lambda kk, r: (kk * lay.n_blocks + src(r), 0), kk))
               for kk in range(TOP_K)]
    return pl.pallas_call(
        _combine_kernel,
        grid=(n_out,),
        in_specs=[pl.BlockSpec((TL, D), lambda r: (src(r), 0))] + y_specs + [
            pl.BlockSpec((TL, TOP_K), lambda r: (src(r), 0)),
            pl.BlockSpec((1, 1, D), lambda r: (lay.mod_row(src(r)), 0, 5))],
        out_specs=pl.BlockSpec((TL, D), lambda r: (r, 0)),
        out_shape=jax.ShapeDtypeStruct((n_out * TL, D), F32),
        compiler_params=_params(1),
        name="moe_combine",
    )(x, yg, yg, yg, yg, gates, mod)


def moe_route(top_e, rank, counts):
    T = top_e.shape[1]
    assert (T * TOP_K) % MOE_BLOCK == 0
    counts = counts.reshape(N_EXPERTS).astype(jnp.int32)
    padded = (counts + MOE_BLOCK - 1) // MOE_BLOCK * MOE_BLOCK
    padded_end = jnp.cumsum(padded)
    padded_start = padded_end - padded
    experts = jnp.arange(N_EXPERTS)
    start_of = jnp.sum(jnp.where(top_e[..., None] == experts, padded_start, 0), axis=-1)
    dest = (start_of + rank).astype(jnp.int32)
    n_pad = N_EXPERTS * MOE_BLOCK
    n_rows = T * TOP_K + n_pad
    n_blk = n_rows // MOE_BLOCK
    blk_start = jnp.arange(n_blk) * MOE_BLOCK
    blk_e = jnp.minimum(jnp.sum(padded_end[None, :] <= blk_start[:, None], axis=1), N_EXPERTS - 1)
    blk_e = blk_e.astype(jnp.int32)
    blk_first = jnp.concatenate([jnp.ones((1,), jnp.int32), (blk_e[1:] != blk_e[:-1]).astype(jnp.int32)])
    n_used = (padded_end[-1:] // MOE_BLOCK).astype(jnp.int32)
    tail = padded - counts
    tail_end = jnp.cumsum(tail)
    j = jnp.arange(n_pad)
    owner = jnp.sum(tail_end[None, :] <= j[:, None], axis=1)
    base = padded_start + counts - (tail_end - tail)
    in_group = jnp.sum(jnp.where(owner[:, None] == experts, base, 0), axis=-1) + j
    pad_rows = jnp.where(j < tail_end[-1], in_group, padded_end[-1] + j - tail_end[-1])
    return dest, pad_rows.astype(jnp.int32), blk_e, blk_first, n_used


def kernel(x, c, ctx, c_ctx, norm_mix, norm_ffn, w_mod, b_mod, conv_w_in, conv_w, conv_w_out, ml_w_in, ml_b_gate, ml_norm, ml_w_out, mla_w_in, mla_q_norm, mla_kv_norm, mla_w_uq, mla_w_ukv, mla_qn_nope, mla_qn_rope, mla_kn_nope, mla_kn_rope, mla_w_out, moe_w_router, moe_b_router, moe_w1, moe_b1, moe_w2, moe_b2):
    Bn, n_lat, D = x.shape
    n_ctx = ctx.shape[1]
    assert D == D_MODEL and n_ctx == TL and n_lat % TL == 0
    assert (DEPTH - 1) % N_MIXERS == 0
    full = Layout(Bn, (n_ctx + n_lat) // TL, True)
    lat_only = Layout(Bn, n_lat // TL, False)
    mods = ada_all(c, c_ctx, w_mod, b_mod)
    tables = rope_tables(n_ctx, n_lat)
    b1_all = moe_b1.reshape(DEPTH, N_EXPERTS, 1, 2 * MOE_FF)
    b2_all = moe_b2.reshape(DEPTH, N_EXPERTS, 1, D)
    X = jnp.concatenate([ctx, x], axis=1).reshape(full.rows, D)
    for layer in range(DEPTH):
        kind, j = layer % N_MIXERS, layer // N_MIXERS
        last = layer == DEPTH - 1
        lay = lat_only if last else full
        mod = mods[layer]
        gain_a = norm_mix[layer].reshape(1, D)
        gain_f = norm_ffn[layer].reshape(1, D)
        if kind == 0:
            bg, v = conv_in(lay, X, gain_a, mod, conv_w_in[j].astype(BF16))
            pro, w_out, name = (v, bg, conv_w[j]), conv_w_out[j], "conv"
        elif kind == 1:
            w = ml_w_in[j]
            n_main = 2 * ML_QK + 2 * ML_V
            w_main = jnp.concatenate([w[:, :ML_QK] * ML_DQK ** -0.5, w[:, ML_QK:n_main]], axis=1)
            q, k, v, og, g, gt = mlstm_in(lay, X, gain_a, mod, w_main.astype(BF16), w[:, n_main:],
                                          ml_b_gate[j])
            h_f, h_b = mlstm_scan(lay, q, k, v, g, gt)
            pro, w_out, name = (h_f, h_b, og, ml_norm[j]), ml_w_out[j], "mlstm"
        else:
            q, k, v = mla_in(lay, X, gain_a, mod, mla_w_in[j], mla_q_norm[j], mla_kv_norm[j],
                             mla_w_uq[j], mla_w_ukv[j], mla_qn_nope[j], mla_qn_rope[j],
                             mla_kn_nope[j], mla_kn_rope[j], tables)
            pro, w_out, name = (mla_attention(lay, q, k, v),), mla_w_out[j], "mla"
        X, h2, top_e, gates, rank, counts = mixer_out(
            lay, name, pro, w_out, X, mod, gain_f, moe_w_router[layer], moe_b_router[layer])
        dest, pad_rows, blk_e, blk_first, n_used = moe_route(top_e, rank, counts)
        xp = sc_dispatch(h2, dest, pad_rows)
        yp = expert_ffn(layer, xp, blk_e, blk_first, n_used, moe_w1, b1_all, moe_w2, b2_all)
        yg = sc_gather(yp, dest.reshape(-1))
        X = moe_combine(lay, X, yg, gates.T, mod, drop_ctx=(layer == DEPTH - 2))
    return X.reshape(Bn, n_lat, D)
```

```python
import functools

import jax
import jax.numpy as jnp
from jax import lax
from jax.experimental import pallas as pl
from jax.experimental.pallas import tpu as pltpu
from jax.experimental.pallas import tpu_sc as plsc

D_MODEL = 1024
DEPTH = 4
GRID_W = 64
N_MIXERS = 3
N_ADA = 6
RMS_EPS = 1e-6
CONV_WIDTH = 3
ML_HEADS = 8
ML_DQK = 64
ML_DV = 128
ML_QK = ML_HEADS * ML_DQK
ML_V = ML_HEADS * ML_DV
GATE_CAP = 15.0
MLA_HEADS = 8
MLA_NOPE = 128
MLA_ROPE = 64
MLA_V = 128
MLA_Q_LORA = 384
MLA_KV_LORA = 256
MLA_SCALE = (MLA_NOPE + MLA_ROPE) ** -0.5
ROPE_THETA = 10000.0
N_EXPERTS = 32
TOP_K = 4
MOE_FF = D_MODEL
SWIGLU_ALPHA = 1.702
SWIGLU_LIMIT = 7.0
MOE_BLOCK = 512

TL = 256
LANES = 128
BF16_ROWS = 16
VMEM_LIMIT = 48 * 1024 * 1024
HI = lax.Precision.HIGHEST
F32 = jnp.float32
BF16 = jnp.bfloat16


def _params(n_axes):
    return pltpu.CompilerParams(dimension_semantics=("arbitrary",) * n_axes,
                                vmem_limit_bytes=VMEM_LIMIT)


def _rms(x, width=None):
    width = x.shape[-1] if width is None else width
    return x * lax.rsqrt(jnp.sum(x * x, axis=-1, keepdims=True) * (1.0 / width) + RMS_EPS)


def _norm_mod(x, gain, scale, shift):
    return _rms(x) * gain * (1.0 + scale) + shift


def _split_weight_t(w):
    hi = w.astype(BF16)
    lo = (w - hi.astype(F32)).astype(BF16)
    return jnp.concatenate([hi.T, lo.T], axis=0)


def _split_dot_t(w2, h):
    n = w2.shape[0] // 2
    dn = (((1,), (1,)), ((), ()))
    h_hi = h.astype(BF16)
    h_lo = (h - h_hi.astype(F32)).astype(BF16)
    both = lax.dot_general(w2, h_hi, dn, preferred_element_type=F32)
    cross = lax.dot_general(w2[:n], h_lo, dn, preferred_element_type=F32)
    return both[:n] + both[n:] + cross


PACK_W = D_MODEL // 2
HIGH_HALF = -65536


def _pack_rows(x):
    xb = x.astype(BF16).astype(F32)
    lo = lax.bitcast_convert_type(xb[:, :PACK_W], jnp.int32)
    hi = lax.bitcast_convert_type(xb[:, PACK_W:], jnp.int32)
    return hi | lax.shift_right_logical(lo, 16)


def _unpack_rows(w, dtype):
    lo = lax.bitcast_convert_type(lax.shift_left(w, 16), F32)
    hi = lax.bitcast_convert_type(w & HIGH_HALF, F32)
    return jnp.concatenate([lo.astype(dtype), hi.astype(dtype)], axis=1)


class Layout:
    def __init__(self, n_batch, nb, ctx_first):
        self.n_batch, self.nb, self.ctx_first = n_batch, nb, ctx_first
        self.n_blocks = n_batch * nb
        self.rows = self.n_blocks * TL

    def mod_row(self, r):
        b = r // self.nb
        return jnp.where(r % self.nb == 0, self.n_batch, b) if self.ctx_first else b

    def row_spec(self, width):
        return pl.BlockSpec((TL, width), lambda r: (r, 0))

    def mod_spec(self, piece):
        return pl.BlockSpec((1, 1, D_MODEL), lambda r: (self.mod_row(r), 0, piece))


def _const_spec(shape):
    return pl.BlockSpec(shape, lambda *_: (0,) * len(shape))


ADA_ROWS = 16
ADA_TN = 1536


def _ada_kernel(c_ref, w_ref, b_ref, o_ref):
    c = c_ref[...]
    s = c * jax.nn.sigmoid(c)
    o_ref[0] = jnp.dot(s, w_ref[0], precision=HI, preferred_element_type=F32) + b_ref[0]


def ada_all(c, c_ctx, w_mod, b_mod):
    Bn, D = c.shape
    cond = jnp.zeros((ADA_ROWS, D), F32).at[:Bn].set(c).at[Bn].set(c_ctx)
    out = pl.pallas_call(
        _ada_kernel,
        grid=(DEPTH, N_ADA * D // ADA_TN),
        in_specs=[pl.BlockSpec((ADA_ROWS, D), lambda l, n: (0, 0)),
                  pl.BlockSpec((1, D, ADA_TN), lambda l, n: (l, 0, n)),
                  pl.BlockSpec((1, 1, ADA_TN), lambda l, n: (l, 0, n))],
        out_specs=pl.BlockSpec((1, ADA_ROWS, ADA_TN), lambda l, n: (l, 0, n)),
        out_shape=jax.ShapeDtypeStruct((DEPTH, ADA_ROWS, N_ADA * D), F32),
        compiler_params=_params(2),
        name="ada_mod",
    )(cond, w_mod, b_mod.reshape(DEPTH, 1, N_ADA * D))
    return out[:, :Bn + 1, None, :]


def _conv_in_kernel(x_ref, gain_ref, sc_ref, sh_ref, w_ref, bg_ref, v_ref):
    D = D_MODEL
    h = _norm_mod(x_ref[...], gain_ref[...], sc_ref[0], sh_ref[0]).astype(BF16)
    p = jnp.dot(h, w_ref[...], preferred_element_type=F32)
    bg_ref[...] = p[:, :D].astype(BF16)
    v_ref[...] = (p[:, D:2 * D] * p[:, 2 * D:]).astype(BF16)


def conv_in(lay, x, gain, mod, w_in):
    D = D_MODEL
    sds = jax.ShapeDtypeStruct((lay.rows, D), BF16)
    return pl.pallas_call(
        _conv_in_kernel,
        grid=(lay.n_blocks,),
        in_specs=[lay.row_spec(D), _const_spec((1, D)), lay.mod_spec(1), lay.mod_spec(0),
                  _const_spec((D, 3 * D))],
        out_specs=[lay.row_spec(D), lay.row_spec(D)],
        out_shape=[sds, sds],
        compiler_params=_params(1),
        name="conv_in",
    )(x, gain, mod, mod, w_in)


ML_T = 256
ML_SW = 2 * ML_DV
ML_NG = 4 * ML_HEADS


LOG2E = 1.4426950408889634


def _gate_act(g, is_forget):
    g = GATE_CAP * jnp.tanh(g * (1.0 / GATE_CAP))
    log_sig = jnp.minimum(g, 0.0) - jnp.log(1.0 + jnp.exp(-jnp.abs(g)))
    return jnp.where(is_forget, log_sig, g) * LOG2E


def _mlstm_in_kernel(x_ref, gain_ref, sc_ref, sh_ref, w_ref, wkt_ref, wg_ref, bgt_ref,
                     q_ref, k_ref, kt_ref, v_ref, og_ref, g_ref, gt_ref):
    h = _norm_mod(x_ref[...], gain_ref[...], sc_ref[0], sh_ref[0])
    hb = h.astype(BF16)
    p = jnp.dot(hb, w_ref[...], preferred_element_type=F32)
    q_ref[...] = p[:, :ML_QK].astype(BF16)
    k_ref[...] = p[:, ML_QK:2 * ML_QK].astype(BF16)
    kt_ref[...] = lax.dot_general(wkt_ref[...], hb, (((1,), (1,)), ((), ())),
                                  preferred_element_type=F32).astype(BF16)
    v_ref[...] = p[:, 2 * ML_QK:2 * ML_QK + ML_V].astype(BF16)
    og_ref[...] = jax.nn.sigmoid(p[:, 2 * ML_QK + ML_V:]).astype(BF16)
    gt = _split_dot_t(wg_ref[...], h) + bgt_ref[...]
    row = lax.broadcasted_iota(jnp.int32, gt.shape, 0)
    gt = _gate_act(gt, (row // ML_HEADS) % 2 == 1)
    gt_ref[...] = gt
    eye = (lax.broadcasted_iota(jnp.int32, (TL, TL), 0)
           == lax.broadcasted_iota(jnp.int32, (TL, TL), 1)).astype(F32)
    g_ref[...] = lax.dot_general(eye, gt, (((1,), (1,)), ((), ())), precision=HI,
                                 preferred_element_type=F32)


def mlstm_in(lay, x, gain, mod, w_main, w_g, b_g):
    D = D_MODEL
    n_main = 2 * ML_QK + 2 * ML_V
    bf = lambda w: jax.ShapeDtypeStruct((lay.rows, w), BF16)
    return pl.pallas_call(
        _mlstm_in_kernel,
        grid=(lay.n_blocks,),
        in_specs=[lay.row_spec(D), _const_spec((1, D)), lay.mod_spec(1), lay.mod_spec(0),
                  _const_spec((D, n_main)), _const_spec((ML_QK, D)), _const_spec((2 * ML_NG, D)),
                  _const_spec((ML_NG, 1))],
        out_specs=[lay.row_spec(ML_QK), lay.row_spec(ML_QK), pl.BlockSpec((ML_QK, TL), lambda r: (0, r)),
                   lay.row_spec(ML_V), lay.row_spec(ML_V),
                   lay.row_spec(ML_NG), pl.BlockSpec((ML_NG, TL), lambda r: (0, r))],
        out_shape=[bf(ML_QK), bf(ML_QK), jax.ShapeDtypeStruct((ML_QK, lay.rows), BF16),
                   bf(ML_V), bf(ML_V),
                   jax.ShapeDtypeStruct((lay.rows, ML_NG), F32),
                   jax.ShapeDtypeStruct((ML_NG, lay.rows), F32)],
        compiler_params=_params(1),
        name="mlstm_in",
    )(x, gain, mod, mod, w_main, w_main[:, ML_QK:2 * ML_QK].T, _split_weight_t(w_g),
      b_g.reshape(ML_NG, 1))


def _split3(x):
    hi = x.astype(BF16)
    r1 = x - hi.astype(F32)
    mid = r1.astype(BF16)
    lo = (r1 - mid.astype(F32)).astype(BF16)
    return hi, mid, lo


def _dot_exact01(x, sel01, x_on_left):
    sel = sel01.astype(BF16)
    parts = [jnp.dot(p, sel, preferred_element_type=F32) if x_on_left
             else jnp.dot(sel, p, preferred_element_type=F32) for p in _split3(x)]
    return parts[0] + parts[1] + parts[2]


def _mlstm_dir(reverse, q_ref, k_ref, kt_ref, v_ref, g_ref, gt_ref, o_ref, s_ref, m_ref):
    T = ML_T
    row = lax.broadcasted_iota(jnp.int32, (T, T), 0)
    col = lax.broadcasted_iota(jnp.int32, (T, T), 1)
    mask = (col >= row) if reverse else (col <= row)
    gt = gt_ref[...]
    bc = _dot_exact01(g_ref[...], mask, x_on_left=False)
    br = _dot_exact01(gt, (row >= col) if reverse else (row <= col), x_on_left=True)
    gi, gf = (2 * ML_HEADS, 3 * ML_HEADS) if reverse else (0, ML_HEADS)
    lane = lax.broadcasted_iota(jnp.int32, (T, 2 * ML_DQK), 1)
    sub = lax.broadcasted_iota(jnp.int32, (2 * ML_DQK, T), 0)
    gate_row = lax.broadcasted_iota(jnp.int32, (ML_NG, ML_DV), 0)
    ones = jnp.ones((T, ML_DV), BF16)
    heads = range(ML_HEADS)
    qm, vx, s_raw = {}, {}, {}
    for h in heads:
        pair = (h // 2) * 2 * ML_DQK
        own = (lane >= ML_DQK) if (h % 2) else (lane < ML_DQK)
        qp = q_ref[:, pair:pair + 2 * ML_DQK]
        qm[h] = jnp.where(own, qp, jnp.zeros_like(qp))
        vx[h] = jnp.concatenate([v_ref[:, h * ML_DV:(h + 1) * ML_DV], ones], axis=1)
        s_raw[h] = lax.dot_general(qm[h], k_ref[:, pair:pair + 2 * ML_DQK], (((1,), (1,)), ((), ())),
                                   preferred_element_type=F32)
    p, a, u_rep, m_prev, s_prev = {}, {}, {}, {}, {}
    for h in heads:
        c_row = gt[gi + h:gi + h + 1, :] - br[gf + h:gf + h + 1, :]
        e = jnp.where(mask, c_row, -jnp.inf)
        m_prev[h] = m_ref[h][0:1, 0:1]
        u = jnp.maximum(m_prev[h], jnp.max(e, axis=1, keepdims=True))
        p[h] = (s_raw[h] * jnp.exp2(e - u)).astype(BF16)
        u_rep[h] = jnp.broadcast_to(u, (T, ML_DV))
        a[h] = jnp.exp2(m_prev[h] - u_rep[h])
        s_prev[h] = s_ref[h]
    for h in heads:
        r = jnp.dot(p[h], vx[h], preferred_element_type=F32)
        qs = jnp.dot(qm[h], s_prev[h].astype(BF16), preferred_element_type=F32)
        num = r[:, :ML_DV] + a[h] * qs[:, :ML_DV]
        den = r[:, ML_DV:] + a[h] * qs[:, ML_DV:]
        b_rep = _dot_exact01(bc, gate_row == gf + h, x_on_left=True)
        floor = jnp.exp2(-(b_rep + u_rep[h]))
        o_ref[:, h * ML_DV:(h + 1) * ML_DV] = (num / jnp.maximum(jnp.abs(den), floor)).astype(o_ref.dtype)
    for h in heads:
        pair = (h // 2) * 2 * ML_DQK
        b_row = br[gf + h:gf + h + 1, :]
        tot = b_row[:, 0:1] if reverse else b_row[:, T - 1:T]
        g_row = tot - b_row + gt[gi + h:gi + h + 1, :]
        m_new = jnp.maximum(tot + m_prev[h], jnp.max(g_row, axis=1, keepdims=True))
        decay = jnp.exp2(tot + m_prev[h] - m_new)
        wk = jnp.exp2(g_row - m_new)
        own_t = (sub >= ML_DQK) if (h % 2) else (sub < ML_DQK)
        kt = kt_ref[pair:pair + 2 * ML_DQK, :].astype(F32)
        kw = jnp.where(own_t, kt * wk, 0.0).astype(BF16)
        s_ref[h] = decay * s_prev[h] + jnp.dot(kw, vx[h], preferred_element_type=F32)
        m_ref[h] = jnp.broadcast_to(m_new, m_ref.shape[1:])


def _mlstm_scan_kernel(qf_ref, kf_ref, ktf_ref, vf_ref, gf_ref, gtf_ref,
                       qb_ref, kb_ref, ktb_ref, vb_ref, gb_ref, gtb_ref,
                       of_ref, ob_ref, sf_ref, mf_ref, sb_ref, mb_ref):
    @pl.when(pl.program_id(1) == 0)
    def _():
        sf_ref[...] = jnp.zeros_like(sf_ref)
        mf_ref[...] = jnp.zeros_like(mf_ref)
        sb_ref[...] = jnp.zeros_like(sb_ref)
        mb_ref[...] = jnp.zeros_like(mb_ref)

    _mlstm_dir(False, qf_ref, kf_ref, ktf_ref, vf_ref, gf_ref, gtf_ref, of_ref, sf_ref, mf_ref)
    _mlstm_dir(True, qb_ref, kb_ref, ktb_ref, vb_ref, gb_ref, gtb_ref, ob_ref, sb_ref, mb_ref)


def mlstm_scan(lay, q, k, kt, v, g, gt):
    assert lay.ctx_first
    per = TL // ML_T
    nb = lay.nb * per
    rev = lambda j: jnp.where(j < per, per - 1 - j, nb + per - 1 - j)
    fwd = lambda b, j: (b * nb + j, 0)
    bwd = lambda b, j: (b * nb + rev(j), 0)
    fwd_t = lambda b, j: (0, b * nb + j)
    bwd_t = lambda b, j: (0, b * nb + rev(j))

    def specs(im, imt):
        return [pl.BlockSpec((ML_T, ML_QK), im), pl.BlockSpec((ML_T, ML_QK), im),
                pl.BlockSpec((ML_QK, ML_T), imt), pl.BlockSpec((ML_T, ML_V), im),
                pl.BlockSpec((ML_T, ML_NG), im), pl.BlockSpec((ML_NG, ML_T), imt)]

    out_sds = jax.ShapeDtypeStruct((lay.rows, ML_V), F32)
    state = [pltpu.VMEM((ML_HEADS, 2 * ML_DQK, ML_SW), F32),
             pltpu.VMEM((ML_HEADS, 8, LANES), F32)]
    return pl.pallas_call(
        _mlstm_scan_kernel,
        grid=(lay.n_batch, nb),
        in_specs=specs(fwd, fwd_t) + specs(bwd, bwd_t),
        out_specs=[pl.BlockSpec((ML_T, ML_V), fwd), pl.BlockSpec((ML_T, ML_V), bwd)],
        out_shape=[out_sds, out_sds],
        scratch_shapes=state + state,
        compiler_params=_params(2),
        name="mlstm_scan",
    )(q, k, kt, v, g, gt, q, k, kt, v, g, gt)


MLA_QK_PAD = 256
MLA_VW = 2 * MLA_V
MLA_IN_PAD = MLA_Q_LORA + MLA_KV_LORA + LANES
ROPE_HALF = MLA_ROPE // 4


def _mla_in_kernel(x_ref, gain_ref, sc_ref, sh_ref, win_ref, wuq_ref, wukv_ref, qn_ref, kvn_ref,
                   qnn_ref, qnr_ref, knn_ref, knr_ref, cos_ref, sa_ref, sb_ref,
                   q_out, k_out, v_out):
    h = _norm_mod(x_ref[...], gain_ref[...], sc_ref[0], sh_ref[0]).astype(BF16)
    p = jnp.dot(h, win_ref[...], preferred_element_type=F32)
    cq = _rms(p[:, :MLA_Q_LORA]) * qn_ref[...]
    ckv = _rms(p[:, MLA_Q_LORA:MLA_Q_LORA + MLA_KV_LORA]) * kvn_ref[...]
    kr = p[:, MLA_Q_LORA + MLA_KV_LORA:]
    q = jnp.dot(cq.astype(BF16), wuq_ref[...], preferred_element_type=F32)
    kv = jnp.dot(ckv.astype(BF16), wukv_ref[...], preferred_element_type=F32)
    nv = MLA_HEADS * MLA_NOPE
    ones_col = (lax.broadcasted_iota(jnp.int32, (TL, MLA_VW - MLA_V), 1) == 0).astype(BF16)
    cos, sa, sb = cos_ref[...], sa_ref[...], sb_ref[...]

    def rope(xp):
        return (xp * cos + pltpu.roll(xp, LANES - ROPE_HALF, 1) * sa + pltpu.roll(xp, ROPE_HALF, 1) * sb)

    kr = rope(_rms(kr, MLA_ROPE) * knr_ref[...]).astype(BF16)
    for hd in range(MLA_HEADS):
        c0 = hd * MLA_QK_PAD
        qn = _rms(q[:, c0:c0 + MLA_NOPE]) * qnn_ref[...]
        qr = rope(_rms(q[:, c0 + MLA_NOPE:c0 + MLA_QK_PAD], MLA_ROPE) * qnr_ref[...])
        q_out[:, c0:c0 + MLA_NOPE] = (qn * (MLA_SCALE * LOG2E)).astype(BF16)
        q_out[:, c0 + MLA_NOPE:c0 + MLA_QK_PAD] = (qr * (MLA_SCALE * LOG2E)).astype(BF16)
        v0 = nv + hd * MLA_V
        v_out[:, hd * MLA_VW:hd * MLA_VW + MLA_V] = kv[:, v0:v0 + MLA_V].astype(BF16)
        v_out[:, hd * MLA_VW + MLA_V:(hd + 1) * MLA_VW] = ones_col
        kn = _rms(kv[:, hd * MLA_NOPE:(hd + 1) * MLA_NOPE]) * knn_ref[...]
        k_out[:, c0:c0 + MLA_NOPE] = kn.astype(BF16)
        k_out[:, c0 + MLA_NOPE:c0 + MLA_QK_PAD] = kr


def _pad_lanes(g):
    return jnp.pad(g, (0, LANES - g.shape[0])).reshape(1, LANES)


def mla_in(lay, x, gain, mod, w_in, q_norm, kv_norm, w_uq, w_ukv, qn_nope, qn_rope, kn_nope, kn_rope,
           tables):
    D = D_MODEL
    Hn = MLA_HEADS
    win = jnp.pad(w_in, ((0, 0), (0, MLA_IN_PAD - w_in.shape[1]))).astype(BF16)
    wuq = jnp.pad(w_uq.reshape(MLA_Q_LORA, Hn, MLA_NOPE + MLA_ROPE),
                  ((0, 0), (0, 0), (0, MLA_QK_PAD - MLA_NOPE - MLA_ROPE)))
    wuq = wuq.reshape(MLA_Q_LORA, Hn * MLA_QK_PAD).astype(BF16)
    wkv = w_ukv.reshape(MLA_KV_LORA, Hn, MLA_NOPE + MLA_V)
    wukv = jnp.concatenate([wkv[:, :, :MLA_NOPE].reshape(MLA_KV_LORA, Hn * MLA_NOPE),
                            wkv[:, :, MLA_NOPE:].reshape(MLA_KV_LORA, Hn * MLA_V)], axis=1).astype(BF16)
    nb = lay.nb
    tab_spec = pl.BlockSpec((TL, LANES), lambda r: (r % nb, 0))
    bf = lambda w: jax.ShapeDtypeStruct((lay.rows, w), BF16)
    return pl.pallas_call(
        _mla_in_kernel,
        grid=(lay.n_blocks,),
        in_specs=[lay.row_spec(D), _const_spec((1, D)), lay.mod_spec(1), lay.mod_spec(0),
                  _const_spec(win.shape), _const_spec(wuq.shape), _const_spec(wukv.shape),
                  _const_spec((1, MLA_Q_LORA)), _const_spec((1, MLA_KV_LORA)),
                  _const_spec((1, LANES)), _const_spec((1, LANES)), _const_spec((1, LANES)),
                  _const_spec((1, LANES)), tab_spec, tab_spec, tab_spec],
        out_specs=[lay.row_spec(Hn * MLA_QK_PAD), lay.row_spec(Hn * MLA_QK_PAD), lay.row_spec(Hn * MLA_VW)],
        out_shape=[bf(Hn * MLA_QK_PAD), bf(Hn * MLA_QK_PAD), bf(Hn * MLA_VW)],
        compiler_params=_params(1),
        name="mla_in",
    )(x, gain, mod, mod, win, wuq, wukv, q_norm.reshape(1, -1), kv_norm.reshape(1, -1),
      qn_nope.reshape(1, -1), _pad_lanes(qn_rope), kn_nope.reshape(1, -1), _pad_lanes(kn_rope), *tables)


def rope_tables(n_ctx, n_lat):
    n_freq = MLA_ROPE // 4
    inv = ROPE_THETA ** (-jnp.arange(n_freq, dtype=F32) / n_freq)
    t = jnp.arange(n_lat)
    a_r = (t // GRID_W).astype(F32)[:, None] * inv
    a_c = (t % GRID_W).astype(F32)[:, None] * inv
    ang = jnp.concatenate([a_r, a_r, a_c, a_c], axis=-1)
    ang = jnp.concatenate([jnp.zeros((n_ctx, MLA_ROPE), F32), ang], axis=0)
    cos, sin = jnp.cos(ang), jnp.sin(ang)
    low = (jnp.arange(MLA_ROPE) % (2 * ROPE_HALF)) < ROPE_HALF
    pad = lambda a: jnp.pad(a, ((0, 0), (0, LANES - MLA_ROPE)))
    return pad(cos), pad(jnp.where(low, -sin, 0.0)), pad(jnp.where(low, 0.0, sin))


MLA_HPS = 2


def _mla_attn_kernel(q_ref, k_ref, v_ref, o_ref, *, n_ctx):
    def attend(n_keys):
        for hd in range(MLA_HPS):
            q = q_ref[0, :, hd * MLA_QK_PAD:(hd + 1) * MLA_QK_PAD]
            k = k_ref[0, :n_keys, hd * MLA_QK_PAD:(hd + 1) * MLA_QK_PAD]
            v = v_ref[0, :n_keys, hd * MLA_VW:(hd + 1) * MLA_VW]
            s = lax.dot_general(q, k, (((1,), (1,)), ((), ())), preferred_element_type=F32)
            p = jnp.exp2(s - s.max(axis=1, keepdims=True)).astype(BF16)
            r = jnp.dot(p, v, preferred_element_type=F32)
            o_ref[0, :, hd * MLA_V:(hd + 1) * MLA_V] = (
                r[:, :MLA_V] / r[:, MLA_V:MLA_V + 1]).astype(o_ref.dtype)

    @pl.when(pl.program_id(2) == 0)
    def _():
        attend(n_ctx)

    @pl.when(pl.program_id(2) > 0)
    def _():
        attend(k_ref.shape[1])


def mla_attention(lay, q, k, v):
    assert lay.ctx_first
    Bn, S = lay.n_batch, lay.nb * TL
    q3, k3, v3 = (a.reshape(Bn, S, a.shape[-1]) for a in (q, k, v))
    out = pl.pallas_call(
        functools.partial(_mla_attn_kernel, n_ctx=TL),
        grid=(Bn, MLA_HEADS // MLA_HPS, lay.nb),
        in_specs=[pl.BlockSpec((1, TL, MLA_HPS * MLA_QK_PAD), lambda b, h, i: (b, i, h)),
                  pl.BlockSpec((1, S, MLA_HPS * MLA_QK_PAD), lambda b, h, i: (b, 0, h)),
                  pl.BlockSpec((1, S, MLA_HPS * MLA_VW), lambda b, h, i: (b, 0, h))],
        out_specs=pl.BlockSpec((1, TL, MLA_HPS * MLA_V), lambda b, h, i: (b, i, h)),
        out_shape=jax.ShapeDtypeStruct((Bn, S, MLA_HEADS * MLA_V), BF16),
        compiler_params=_params(3),
        name="mla_attention",
    )(q3, k3, v3)
    return out.reshape(lay.rows, MLA_HEADS * MLA_V)


N_PROLOGUE = {"conv": 5, "mlstm": 4, "mla": 1}


def _mixer_out_kernel(*refs, kind, nb, ctx_first):
    n_pro = N_PROLOGUE[kind]
    pro = refs[:n_pro]
    (wout_ref, x_ref, ga_ref, gain_ref, sc_ref, sh_ref, wr_ref, br_ref,
     xo_ref, h2_ref, te_ref, gate_ref, rank_ref, cnt_ref, carry_ref) = refs[n_pro:]
    r = pl.program_id(0)

    if kind == "conv":
        vprev_ref, v_ref, vnext_ref, bg_ref, cw_ref = pro
        j = r % nb
        first = (j == 0) | (j == 1) if ctx_first else (j == 0)
        last = (j == nb - 1) | (j == 0) if ctx_first else (j == nb - 1)
        v = v_ref[...].astype(F32)
        rows = lax.broadcasted_iota(jnp.int32, (TL, 1), 0)
        prev_row = jnp.where(first, 0.0, vprev_ref[BF16_ROWS - 1:BF16_ROWS, :].astype(F32))
        next_row = jnp.where(last, 0.0, vnext_ref[0:1, :].astype(F32))
        up = jnp.where(rows == 0, prev_row, pltpu.roll(v, 1, 0))
        dn = jnp.where(rows == TL - 1, next_row, pltpu.roll(v, TL - 1, 0))
        cw = cw_ref[...]
        a = bg_ref[...].astype(F32) * (up * cw[0:1] + v * cw[1:2] + dn * cw[2:3])
    elif kind == "mlstm":
        hf_ref, hb_ref, og_ref, ng_ref = pro
        hh = hf_ref[...] + hb_ref[...]
        a = jnp.concatenate([_rms(hh[:, h * ML_DV:(h + 1) * ML_DV]) for h in range(ML_HEADS)], axis=1)
        a = a * ng_ref[...] * og_ref[...].astype(F32)
    else:
        a = pro[0][...]

    y = jnp.dot(a.astype(BF16), wout_ref[...], preferred_element_type=F32)
    xn = x_ref[...] + ga_ref[0] * y
    xo_ref[...] = xn
    h2 = _norm_mod(xn, gain_ref[...], sc_ref[0], sh_ref[0])
    h2_ref[...] = _pack_rows(h2)
    logits = _split_dot_t(wr_ref[...], h2) + br_ref[...]

    sub = lax.broadcasted_iota(jnp.int32, (N_EXPERTS, TL), 0)
    sub_k = lax.broadcasted_iota(jnp.int32, (TOP_K, TL), 0)
    work = logits
    sel = jnp.zeros((N_EXPERTS, TL), F32)
    top_e = jnp.zeros((TOP_K, TL), jnp.int32)
    top_v = jnp.zeros((TOP_K, TL), F32)
    picks = []
    for kk in range(TOP_K):
        m = work.max(axis=0, keepdims=True)
        idx = jnp.min(jnp.where(work == m, sub, N_EXPERTS), axis=0, keepdims=True)
        hit = sub == idx
        picks.append(hit)
        sel = jnp.where(hit, 1.0, sel)
        work = jnp.where(hit, -jnp.inf, work)
        top_e = jnp.where(sub_k == kk, idx, top_e)
        top_v = jnp.where(sub_k == kk, m, top_v)
    ex = jnp.exp(top_v - top_v[0:1])
    gate_ref[...] = ex / ex.sum(axis=0, keepdims=True)
    te_ref[...] = top_e

    @pl.when(r == 0)
    def _():
        carry_ref[...] = jnp.zeros_like(carry_ref)

    tr = lax.broadcasted_iota(jnp.int32, (TL, TL), 0)
    tc = lax.broadcasted_iota(jnp.int32, (TL, TL), 1)
    before = jnp.dot(sel.astype(BF16), (tr < tc).astype(BF16), preferred_element_type=F32)
    pos = before + carry_ref[...]
    rank = jnp.zeros((TOP_K, TL), F32)
    for kk in range(TOP_K):
        rk = jnp.sum(jnp.where(picks[kk], pos, 0.0), axis=0, keepdims=True)
        rank = jnp.where(sub_k == kk, rk, rank)
    rank_ref[...] = rank.astype(jnp.int32)
    total = carry_ref[...] + jnp.sum(sel, axis=1, keepdims=True)
    carry_ref[...] = total
    cnt_ref[...] = total


def mixer_out(lay, kind, pro_args, w_out, x, mod, gain_f, w_r, b_r):
    D = D_MODEL
    nb = lay.nb
    if kind == "conv":
        v, bg, cw = pro_args
        per = TL // BF16_ROWS
        last_tile = lay.rows // BF16_ROWS - 1
        pro_specs = [pl.BlockSpec((BF16_ROWS, D), lambda r: (jnp.maximum(r * per - 1, 0), 0)),
                     lay.row_spec(D),
                     pl.BlockSpec((BF16_ROWS, D), lambda r: (jnp.minimum((r + 1) * per, last_tile), 0)),
                     lay.row_spec(D), _const_spec((CONV_WIDTH, D))]
        pro_in = [v, v, v, bg, cw]
    elif kind == "mlstm":
        h_f, h_b, og, ng = pro_args
        pro_specs = [lay.row_spec(ML_V), lay.row_spec(ML_V), lay.row_spec(ML_V), _const_spec((1, ML_V))]
        pro_in = [h_f, h_b, og, ng.reshape(1, ML_V)]
    else:
        pro_specs = [lay.row_spec(D)]
        pro_in = list(pro_args)
    k_in = w_out.shape[0]
    small = lambda dt: jax.ShapeDtypeStruct((TOP_K, lay.rows), dt)
    small_spec = pl.BlockSpec((TOP_K, TL), lambda r: (0, r))
    return pl.pallas_call(
        functools.partial(_mixer_out_kernel, kind=kind, nb=nb, ctx_first=lay.ctx_first),
        grid=(lay.n_blocks,),
        in_specs=pro_specs + [_const_spec((k_in, D)), lay.row_spec(D), lay.mod_spec(2),
                              _const_spec((1, D)), lay.mod_spec(4), lay.mod_spec(3),
                              _const_spec((2 * N_EXPERTS, D)), _const_spec((N_EXPERTS, 1))],
        out_specs=[lay.row_spec(D), lay.row_spec(PACK_W), small_spec, small_spec, small_spec,
                   _const_spec((N_EXPERTS, 1))],
        out_shape=[jax.ShapeDtypeStruct((lay.rows, D), F32),
                   jax.ShapeDtypeStruct((lay.rows, PACK_W), jnp.int32),
                   small(jnp.int32), small(F32), small(jnp.int32),
                   jax.ShapeDtypeStruct((N_EXPERTS, 1), F32)],
        scratch_shapes=[pltpu.VMEM((N_EXPERTS, 1), F32)],
        compiler_params=_params(1),
        name="mixer_out_" + kind,
    )(*pro_in, w_out.astype(BF16), x, mod, gain_f, mod, mod, _split_weight_t(w_r),
      b_r.reshape(N_EXPERTS, 1))


def _expert_ffn_kernel(blk_e_ref, first_ref, used_ref, x_ref, w1_ref, b1_ref, w2_ref, b2_ref, o_ref,
                       w1b_ref, w2b_ref):
    del blk_e_ref
    i = pl.program_id(0)

    @pl.when(first_ref[i] == 1)
    def _():
        w1b_ref[...] = w1_ref[0, 0].astype(BF16)
        w2b_ref[...] = w2_ref[0, 0].astype(BF16)

    @pl.when(i < used_ref[0])
    def _():
        x = _unpack_rows(x_ref[...], BF16)
        h = jnp.dot(x, w1b_ref[...], preferred_element_type=F32) + b1_ref[0, 0]
        glu = jnp.minimum(h[:, :MOE_FF], SWIGLU_LIMIT)
        lin = jnp.clip(h[:, MOE_FF:], -SWIGLU_LIMIT, SWIGLU_LIMIT)
        act = glu * jax.nn.sigmoid(SWIGLU_ALPHA * glu) * (lin + 1.0)
        y = jnp.dot(act.astype(BF16), w2b_ref[...], preferred_element_type=F32)
        o_ref[...] = _pack_rows(y + b2_ref[0, 0])

    @pl.when(i >= used_ref[0])
    def _():
        o_ref[...] = jnp.zeros_like(o_ref)


def expert_ffn(layer, xp, blk_e, blk_first, n_used, w1, b1, w2, b2):
    n_rows = xp.shape[0]
    D, F2 = D_MODEL, 2 * MOE_FF
    n_blk = n_rows // MOE_BLOCK
    grid_spec = pltpu.PrefetchScalarGridSpec(
        num_scalar_prefetch=3,
        grid=(n_blk,),
        in_specs=[
            pl.BlockSpec((MOE_BLOCK, PACK_W), lambda i, be, fi, nu: (i, 0)),
            pl.BlockSpec((1, 1, D, F2), lambda i, be, fi, nu: (layer, be[i], 0, 0)),
            pl.BlockSpec((1, 1, 1, F2), lambda i, be, fi, nu: (layer, be[i], 0, 0)),
            pl.BlockSpec((1, 1, MOE_FF, D), lambda i, be, fi, nu: (layer, be[i], 0, 0)),
            pl.BlockSpec((1, 1, 1, D), lambda i, be, fi, nu: (layer, be[i], 0, 0)),
        ],
        out_specs=pl.BlockSpec((MOE_BLOCK, PACK_W), lambda i, be, fi, nu: (i, 0)),
        scratch_shapes=[pltpu.VMEM((D, F2), BF16), pltpu.VMEM((MOE_FF, D), BF16)],
    )
    return pl.pallas_call(
        _expert_ffn_kernel,
        grid_spec=grid_spec,
        out_shape=jax.ShapeDtypeStruct((n_rows, PACK_W), jnp.int32),
        compiler_params=_params(1),
        name="expert_ffn",
    )(blk_e, blk_first, n_used, xp, w1, b1, w2, b2)


SC_CORES = 2
SC_SUBCORES = 16
SC_CHUNK = 64


def sc_gather(table, idx):
    n_idx = idx.shape[0]
    width = table.shape[1]
    n_workers = SC_CORES * SC_SUBCORES
    per_worker = n_idx // n_workers
    n_chunks = per_worker // SC_CHUNK
    assert n_chunks * SC_CHUNK * n_workers == n_idx and n_chunks % 2 == 0
    mesh = plsc.VectorSubcoreMesh(core_axis_name="c", subcore_axis_name="s",
                                  num_cores=SC_CORES, num_subcores=SC_SUBCORES)

    def body(table_hbm, idx_hbm, out_hbm, idx_v, rows_v, gsem, wsem):
        wid = lax.axis_index("s") * SC_CORES + lax.axis_index("c")
        pltpu.sync_copy(idx_hbm.at[wid], idx_v)

        def gather(ci, slot):
            return pltpu.make_async_copy(table_hbm.at[idx_v.at[ci]], rows_v.at[slot], gsem.at[slot])

        def write(ci, slot):
            return pltpu.make_async_copy(rows_v.at[slot], out_hbm.at[ci, wid], wsem.at[slot])

        gather(0, 0).start()

        @pl.loop(0, n_chunks, step=2)
        def _(c0):
            for slot in range(2):
                ci = c0 + slot
                other = 1 - slot

                @pl.when(ci + 1 < n_chunks)
                def _():
                    @pl.when(ci >= 1)
                    def _():
                        write(ci - 1, other).wait()
                    gather(ci + 1, other).start()

                gather(ci, slot).wait()
                write(ci, slot).start()

        write(n_chunks - 2, 0).wait()
        write(n_chunks - 1, 1).wait()

    out = pl.kernel(
        body,
        out_type=jax.ShapeDtypeStruct((n_chunks, n_workers, SC_CHUNK, width), table.dtype),
        mesh=mesh,
        scratch_types=[pltpu.VMEM((n_chunks, SC_CHUNK), jnp.int32),
                       pltpu.VMEM((2, SC_CHUNK, width), table.dtype),
                       pltpu.SemaphoreType.DMA((2,)),
                       pltpu.SemaphoreType.DMA((2,))],
        name="sc_gather",
    )(table, idx.reshape(n_chunks, n_workers, SC_CHUNK).transpose(1, 0, 2))
    return out.reshape(n_idx, width)


def sc_dispatch(table, dest, pad_rows):
    n_tok, width = table.shape
    n_picks = dest.shape[0]
    n_pad = pad_rows.shape[0]
    n_workers = SC_CORES * SC_SUBCORES
    per_w = n_tok // SC_CHUNK // n_workers
    pad_w = n_pad // SC_CHUNK // n_workers
    assert per_w * SC_CHUNK * n_workers == n_tok and pad_w * SC_CHUNK * n_workers == n_pad
    mesh = plsc.VectorSubcoreMesh(core_axis_name="c", subcore_axis_name="s",
                                  num_cores=SC_CORES, num_subcores=SC_SUBCORES)

    def body(table_hbm, idx_hbm, pad_hbm, zero_hbm, out_hbm, idx_v, pad_v, rows_v, zero_v,
             rsem, ssem, zsem):
        wid = lax.axis_index("s") * SC_CORES + lax.axis_index("c")
        pltpu.sync_copy(idx_hbm.at[wid], idx_v)
        pltpu.sync_copy(pad_hbm.at[wid], pad_v)
        pltpu.sync_copy(zero_hbm, zero_v)

        def zero_fill(pc):
            return pltpu.make_async_copy(zero_v, out_hbm.at[pad_v.at[pc]], zsem)

        for pc in range(pad_w):
            zero_fill(pc).start()

        def scatter(ci, kk):
            return pltpu.make_async_copy(rows_v, out_hbm.at[idx_v.at[ci * n_picks + kk]], ssem)

        @pl.loop(0, per_w)
        def _(ci):
            pltpu.async_copy(table_hbm.at[ci, wid], rows_v, rsem).wait()
            for kk in range(n_picks):
                scatter(ci, kk).start()
            for kk in range(n_picks):
                scatter(ci, kk).wait()

        for pc in range(pad_w):
            zero_fill(pc).wait()

    idx = dest.reshape(n_picks, per_w, n_workers, SC_CHUNK).transpose(2, 1, 0, 3)
    idx = idx.reshape(n_workers, per_w * n_picks, SC_CHUNK)
    return pl.kernel(
        body,
        out_type=jax.ShapeDtypeStruct((n_tok * n_picks + n_pad, width), table.dtype),
        mesh=mesh,
        scratch_types=[pltpu.VMEM((per_w * n_picks, SC_CHUNK), jnp.int32),
                       pltpu.VMEM((pad_w, SC_CHUNK), jnp.int32),
                       pltpu.VMEM((SC_CHUNK, width), table.dtype),
                       pltpu.VMEM((SC_CHUNK, width), table.dtype),
                       pltpu.SemaphoreType.DMA, pltpu.SemaphoreType.DMA, pltpu.SemaphoreType.DMA],
        name="sc_dispatch",
    )(table.reshape(per_w, n_workers, SC_CHUNK, width), idx,
      pad_rows.reshape(n_workers, pad_w, SC_CHUNK), jnp.zeros((SC_CHUNK, width), table.dtype))


def _combine_kernel(x_ref, *refs):
    y_refs, (gate_ref, gf_ref, o_ref) = refs[:TOP_K], refs[TOP_K:]
    gates = gate_ref[...]
    acc = gates[:, 0:1] * _unpack_rows(y_refs[0][...], F32)
    for kk in range(1, TOP_K):
        acc = acc + gates[:, kk:kk + 1] * _unpack_rows(y_refs[kk][...], F32)
    o_ref[...] = x_ref[...] + gf_ref[0] * acc


def moe_combine(lay, x, yg, gates, mod, drop_ctx):
    D = D_MODEL
    if drop_ctx:
        nbo = lay.nb - 1
        src = lambda r: (r // nbo) * lay.nb + 1 + r % nbo
        n_out = lay.n_batch * nbo
    else:
        src = lambda r: r
        n_out = lay.n_blocks
    y_specs = [pl.BlockSpec((TL, PACK_W), functools.partial(lambda kk, r: (kk * lay.n_blocks + src(r), 0), kk))
               for kk in range(TOP_K)]
    return pl.pallas_call(
        _combine_kernel,
        grid=(n_out,),
        in_specs=[pl.BlockSpec((TL, D), lambda r: (src(r), 0))] + y_specs + [
            pl.BlockSpec((TL, TOP_K), lambda r: (src(r), 0)),
            pl.BlockSpec((1, 1, D), lambda r: (lay.mod_row(src(r)), 0, 5))],
        out_specs=pl.BlockSpec((TL, D), lambda r: (r, 0)),
        out_shape=jax.ShapeDtypeStruct((n_out * TL, D), F32),
        compiler_params=_params(1),
        name="moe_combine",
    )(x, yg, yg, yg, yg, gates, mod)


def moe_route(top_e, rank, counts):
    T = top_e.shape[1]
    assert (T * TOP_K) % MOE_BLOCK == 0
    counts = counts.reshape(N_EXPERTS).astype(jnp.int32)
    padded = (counts + MOE_BLOCK - 1) // MOE_BLOCK * MOE_BLOCK
    padded_end = jnp.cumsum(padded)
    padded_start = padded_end - padded
    experts = jnp.arange(N_EXPERTS)
    start_of = jnp.sum(jnp.where(top_e[..., None] == experts, padded_start, 0), axis=-1)
    dest = (start_of + rank).astype(jnp.int32)
    n_pad = N_EXPERTS * MOE_BLOCK
    n_rows = T * TOP_K + n_pad
    n_blk = n_rows // MOE_BLOCK
    blk_start = jnp.arange(n_blk) * MOE_BLOCK
    blk_e = jnp.minimum(jnp.sum(padded_end[None, :] <= blk_start[:, None], axis=1), N_EXPERTS - 1)
    blk_e = blk_e.astype(jnp.int32)
    blk_first = jnp.concatenate([jnp.ones((1,), jnp.int32), (blk_e[1:] != blk_e[:-1]).astype(jnp.int32)])
    n_used = (padded_end[-1:] // MOE_BLOCK).astype(jnp.int32)
    tail = padded - counts
    tail_end = jnp.cumsum(tail)
    j = jnp.arange(n_pad)
    owner = jnp.sum(tail_end[None, :] <= j[:, None], axis=1)
    base = padded_start + counts - (tail_end - tail)
    in_group = jnp.sum(jnp.where(owner[:, None] == experts, base, 0), axis=-1) + j
    pad_rows = jnp.where(j < tail_end[-1], in_group, padded_end[-1] + j - tail_end[-1])
    return dest, pad_rows.astype(jnp.int32), blk_e, blk_first, n_used


def kernel(x, c, ctx, c_ctx, norm_mix, norm_ffn, w_mod, b_mod, conv_w_in, conv_w, conv_w_out, ml_w_in, ml_b_gate, ml_norm, ml_w_out, mla_w_in, mla_q_norm, mla_kv_norm, mla_w_uq, mla_w_ukv, mla_qn_nope, mla_qn_rope, mla_kn_nope, mla_kn_rope, mla_w_out, moe_w_router, moe_b_router, moe_w1, moe_b1, moe_w2, moe_b2):
    Bn, n_lat, D = x.shape
    n_ctx = ctx.shape[1]
    assert D == D_MODEL and n_ctx == TL and n_lat % TL == 0
    assert (DEPTH - 1) % N_MIXERS == 0
    full = Layout(Bn, (n_ctx + n_lat) // TL, True)
    lat_only = Layout(Bn, n_lat // TL, False)
    mods = ada_all(c, c_ctx, w_mod, b_mod)
    tables = rope_tables(n_ctx, n_lat)
    b1_all = moe_b1.reshape(DEPTH, N_EXPERTS, 1, 2 * MOE_FF)
    b2_all = moe_b2.reshape(DEPTH, N_EXPERTS, 1, D)
    X = jnp.concatenate([ctx, x], axis=1).reshape(full.rows, D)
    for layer in range(DEPTH):
        kind, j = layer % N_MIXERS, layer // N_MIXERS
        last = layer == DEPTH - 1
        lay = lat_only if last else full
        mod = mods[layer]
        gain_a = norm_mix[layer].reshape(1, D)
        gain_f = norm_ffn[layer].reshape(1, D)
        if kind == 0:
            bg, v = conv_in(lay, X, gain_a, mod, conv_w_in[j].astype(BF16))
            pro, w_out, name = (v, bg, conv_w[j]), conv_w_out[j], "conv"
        elif kind == 1:
            w = ml_w_in[j]
            n_main = 2 * ML_QK + 2 * ML_V
            w_main = jnp.concatenate([w[:, :ML_QK] * ML_DQK ** -0.5, w[:, ML_QK:n_main]], axis=1)
            q, k, kt, v, og, g, gt = mlstm_in(lay, X, gain_a, mod, w_main.astype(BF16), w[:, n_main:],
                                          ml_b_gate[j])
            h_f, h_b = mlstm_scan(lay, q, k, kt, v, g, gt)
            pro, w_out, name = (h_f, h_b, og, ml_norm[j]), ml_w_out[j], "mlstm"
        else:
            q, k, v = mla_in(lay, X, gain_a, mod, mla_w_in[j], mla_q_norm[j], mla_kv_norm[j],
                             mla_w_uq[j], mla_w_ukv[j], mla_qn_nope[j], mla_qn_rope[j],
                             mla_kn_nope[j], mla_kn_rope[j], tables)
            pro, w_out, name = (mla_attention(lay, q, k, v),), mla_w_out[j], "mla"
        X, h2, top_e, gates, rank, counts = mixer_out(
            lay, name, pro, w_out, X, mod, gain_f, moe_w_router[layer], moe_b_router[layer])
        dest, pad_rows, blk_e, blk_first, n_used = moe_route(top_e, rank, counts)
        xp = sc_dispatch(h2, dest, pad_rows)
        yp = expert_ffn(layer, xp, blk_e, blk_first, n_used, moe_w1, b1_all, moe_w2, b2_all)
        yg = sc_gather(yp, dest.reshape(-1))
        X = moe_combine(lay, X, yg, gates.T, mod, drop_ctx=(layer == DEPTH - 2))
    return X.reshape(Bn, n_lat, D)
```

```python
import functools

import jax
import jax.numpy as jnp
from jax import lax
from jax.experimental import pallas as pl
from jax.experimental.pallas import tpu as pltpu
from jax.experimental.pallas import tpu_sc as plsc

D_MODEL = 1024
DEPTH = 4
GRID_W = 64
N_MIXERS = 3
N_ADA = 6
RMS_EPS = 1e-6
CONV_WIDTH = 3
ML_HEADS = 8
ML_DQK = 64
ML_DV = 128
ML_QK = ML_HEADS * ML_DQK
ML_V = ML_HEADS * ML_DV
GATE_CAP = 15.0
MLA_HEADS = 8
MLA_NOPE = 128
MLA_ROPE = 64
MLA_V = 128
MLA_Q_LORA = 384
MLA_KV_LORA = 256
MLA_SCALE = (MLA_NOPE + MLA_ROPE) ** -0.5
ROPE_THETA = 10000.0
N_EXPERTS = 32
TOP_K = 4
MOE_FF = D_MODEL
SWIGLU_ALPHA = 1.702
SWIGLU_LIMIT = 7.0
MOE_BLOCK = 512

TL = 256
LANES = 128
BF16_ROWS = 16
VMEM_LIMIT = 48 * 1024 * 1024
HI = lax.Precision.HIGHEST
F32 = jnp.float32
BF16 = jnp.bfloat16


def _params(n_axes):
    return pltpu.CompilerParams(dimension_semantics=("arbitrary",) * n_axes,
                                vmem_limit_bytes=VMEM_LIMIT)


def _rms(x, width=None):
    width = x.shape[-1] if width is None else width
    return x * lax.rsqrt(jnp.sum(x * x, axis=-1, keepdims=True) * (1.0 / width) + RMS_EPS)


def _norm_mod(x, gain, scale, shift):
    return _rms(x) * (gain * (1.0 + scale)) + shift


def _split_weight_t(w):
    hi = w.astype(BF16)
    lo = (w - hi.astype(F32)).astype(BF16)
    return jnp.concatenate([hi.T, lo.T], axis=0)


def _split_dot_t(w2, h):
    n = w2.shape[0] // 2
    dn = (((1,), (1,)), ((), ()))
    h_hi = h.astype(BF16)
    h_lo = (h - h_hi.astype(F32)).astype(BF16)
    both = lax.dot_general(w2, h_hi, dn, preferred_element_type=F32)
    cross = lax.dot_general(w2[:n], h_lo, dn, preferred_element_type=F32)
    return both[:n] + both[n:] + cross


PACK_W = D_MODEL // 2
HIGH_HALF = -65536


def _pack_rows(x):
    xb = x.astype(BF16).astype(F32)
    lo = lax.bitcast_convert_type(xb[:, :PACK_W], jnp.int32)
    hi = lax.bitcast_convert_type(xb[:, PACK_W:], jnp.int32)
    return hi | lax.shift_right_logical(lo, 16)


def _unpack_rows(w, dtype):
    lo = lax.bitcast_convert_type(lax.shift_left(w, 16), F32)
    hi = lax.bitcast_convert_type(w & HIGH_HALF, F32)
    return jnp.concatenate([lo.astype(dtype), hi.astype(dtype)], axis=1)


class Layout:
    def __init__(self, n_batch, nb, ctx_first):
        self.n_batch, self.nb, self.ctx_first = n_batch, nb, ctx_first
        self.n_blocks = n_batch * nb
        self.rows = self.n_blocks * TL

    def mod_row(self, r):
        b = r // self.nb
        return jnp.where(r % self.nb == 0, self.n_batch, b) if self.ctx_first else b

    def row_spec(self, width):
        return pl.BlockSpec((TL, width), lambda r: (r, 0))

    def mod_spec(self, piece):
        return pl.BlockSpec((1, 1, D_MODEL), lambda r: (self.mod_row(r), 0, piece))


def _const_spec(shape):
    return pl.BlockSpec(shape, lambda *_: (0,) * len(shape))


ADA_ROWS = 16
ADA_TN = 1536


def _ada_kernel(c_ref, w_ref, b_ref, o_ref):
    c = c_ref[...]
    s = c * jax.nn.sigmoid(c)
    o_ref[0] = jnp.dot(s, w_ref[0], precision=HI, preferred_element_type=F32) + b_ref[0]


def ada_all(c, c_ctx, w_mod, b_mod):
    Bn, D = c.shape
    cond = jnp.zeros((ADA_ROWS, D), F32).at[:Bn].set(c).at[Bn].set(c_ctx)
    out = pl.pallas_call(
        _ada_kernel,
        grid=(DEPTH, N_ADA * D // ADA_TN),
        in_specs=[pl.BlockSpec((ADA_ROWS, D), lambda l, n: (0, 0)),
                  pl.BlockSpec((1, D, ADA_TN), lambda l, n: (l, 0, n)),
                  pl.BlockSpec((1, 1, ADA_TN), lambda l, n: (l, 0, n))],
        out_specs=pl.BlockSpec((1, ADA_ROWS, ADA_TN), lambda l, n: (l, 0, n)),
        out_shape=jax.ShapeDtypeStruct((DEPTH, ADA_ROWS, N_ADA * D), F32),
        compiler_params=_params(2),
        name="ada_mod",
    )(cond, w_mod, b_mod.reshape(DEPTH, 1, N_ADA * D))
    return out[:, :Bn + 1, None, :]


def _conv_in_kernel(x_ref, gain_ref, sc_ref, sh_ref, w_ref, bg_ref, v_ref):
    D = D_MODEL
    h = _norm_mod(x_ref[...], gain_ref[...], sc_ref[0], sh_ref[0]).astype(BF16)
    p = jnp.dot(h, w_ref[...], preferred_element_type=F32)
    bg_ref[...] = p[:, :D].astype(BF16)
    v_ref[...] = (p[:, D:2 * D] * p[:, 2 * D:]).astype(BF16)


def conv_in(lay, x, gain, mod, w_in):
    D = D_MODEL
    sds = jax.ShapeDtypeStruct((lay.rows, D), BF16)
    return pl.pallas_call(
        _conv_in_kernel,
        grid=(lay.n_blocks,),
        in_specs=[lay.row_spec(D), _const_spec((1, D)), lay.mod_spec(1), lay.mod_spec(0),
                  _const_spec((D, 3 * D))],
        out_specs=[lay.row_spec(D), lay.row_spec(D)],
        out_shape=[sds, sds],
        compiler_params=_params(1),
        name="conv_in",
    )(x, gain, mod, mod, w_in)


ML_T = 256
ML_SW = 2 * ML_DV
ML_NG = 4 * ML_HEADS


LOG2E = 1.4426950408889634


def _gate_act(g, is_forget):
    g = GATE_CAP * jnp.tanh(g * (1.0 / GATE_CAP))
    log_sig = jnp.minimum(g, 0.0) - jnp.log(1.0 + jnp.exp(-jnp.abs(g)))
    return jnp.where(is_forget, log_sig, g) * LOG2E


def _mlstm_in_kernel(x_ref, gain_ref, sc_ref, sh_ref, w_ref, wkt_ref, wg_ref, bgt_ref,
                     q_ref, k_ref, kt_ref, v_ref, og_ref, g_ref, gt_ref):
    h = _norm_mod(x_ref[...], gain_ref[...], sc_ref[0], sh_ref[0])
    hb = h.astype(BF16)
    p = jnp.dot(hb, w_ref[...], preferred_element_type=F32)
    q_ref[...] = p[:, :ML_QK].astype(BF16)
    k_ref[...] = p[:, ML_QK:2 * ML_QK].astype(BF16)
    kt_ref[...] = lax.dot_general(wkt_ref[...], hb, (((1,), (1,)), ((), ())),
                                  preferred_element_type=F32).astype(BF16)
    v_ref[...] = p[:, 2 * ML_QK:2 * ML_QK + ML_V].astype(BF16)
    og_ref[...] = jax.nn.sigmoid(p[:, 2 * ML_QK + ML_V:]).astype(BF16)
    gt = _split_dot_t(wg_ref[...], h) + bgt_ref[...]
    row = lax.broadcasted_iota(jnp.int32, gt.shape, 0)
    gt = _gate_act(gt, (row // ML_HEADS) % 2 == 1)
    gt_ref[...] = gt
    eye = (lax.broadcasted_iota(jnp.int32, (TL, TL), 0)
           == lax.broadcasted_iota(jnp.int32, (TL, TL), 1)).astype(F32)
    g_ref[...] = lax.dot_general(eye, gt, (((1,), (1,)), ((), ())), precision=HI,
                                 preferred_element_type=F32)


def mlstm_in(lay, x, gain, mod, w_main, w_g, b_g):
    D = D_MODEL
    n_main = 2 * ML_QK + 2 * ML_V
    bf = lambda w: jax.ShapeDtypeStruct((lay.rows, w), BF16)
    return pl.pallas_call(
        _mlstm_in_kernel,
        grid=(lay.n_blocks,),
        in_specs=[lay.row_spec(D), _const_spec((1, D)), lay.mod_spec(1), lay.mod_spec(0),
                  _const_spec((D, n_main)), _const_spec((ML_QK, D)), _const_spec((2 * ML_NG, D)),
                  _const_spec((ML_NG, 1))],
        out_specs=[lay.row_spec(ML_QK), lay.row_spec(ML_QK), pl.BlockSpec((ML_QK, TL), lambda r: (0, r)),
                   lay.row_spec(ML_V), lay.row_spec(ML_V),
                   lay.row_spec(ML_NG), pl.BlockSpec((ML_NG, TL), lambda r: (0, r))],
        out_shape=[bf(ML_QK), bf(ML_QK), jax.ShapeDtypeStruct((ML_QK, lay.rows), BF16),
                   bf(ML_V), bf(ML_V),
                   jax.ShapeDtypeStruct((lay.rows, ML_NG), F32),
                   jax.ShapeDtypeStruct((ML_NG, lay.rows), F32)],
        compiler_params=_params(1),
        name="mlstm_in",
    )(x, gain, mod, mod, w_main, w_main[:, ML_QK:2 * ML_QK].T, _split_weight_t(w_g),
      b_g.reshape(ML_NG, 1))


def _split3(x):
    hi = x.astype(BF16)
    r1 = x - hi.astype(F32)
    mid = r1.astype(BF16)
    lo = (r1 - mid.astype(F32)).astype(BF16)
    return hi, mid, lo


def _dot_exact01(x, sel01, x_on_left):
    sel = sel01.astype(BF16)
    parts = [jnp.dot(p, sel, preferred_element_type=F32) if x_on_left
             else jnp.dot(sel, p, preferred_element_type=F32) for p in _split3(x)]
    return parts[0] + parts[1] + parts[2]


def _mlstm_dir(reverse, q_ref, k_ref, kt_ref, v_ref, g_ref, gt_ref, o_ref, s_ref, m_ref):
    T = ML_T
    row = lax.broadcasted_iota(jnp.int32, (T, T), 0)
    col = lax.broadcasted_iota(jnp.int32, (T, T), 1)
    mask = (col >= row) if reverse else (col <= row)
    gt = gt_ref[...]
    bc = _dot_exact01(g_ref[...], mask, x_on_left=False)
    br = _dot_exact01(gt, (row >= col) if reverse else (row <= col), x_on_left=True)
    gi, gf = (2 * ML_HEADS, 3 * ML_HEADS) if reverse else (0, ML_HEADS)
    lane = lax.broadcasted_iota(jnp.int32, (T, 2 * ML_DQK), 1)
    sub = lax.broadcasted_iota(jnp.int32, (2 * ML_DQK, T), 0)
    gate_row = lax.broadcasted_iota(jnp.int32, (ML_NG, ML_DV), 0)
    ones = jnp.ones((T, ML_DV), BF16)
    heads = range(ML_HEADS)
    qm, vx, s_raw = {}, {}, {}
    for h in heads:
        pair = (h // 2) * 2 * ML_DQK
        own = (lane >= ML_DQK) if (h % 2) else (lane < ML_DQK)
        qp = q_ref[:, pair:pair + 2 * ML_DQK]
        qm[h] = jnp.where(own, qp, jnp.zeros_like(qp))
        vx[h] = jnp.concatenate([v_ref[:, h * ML_DV:(h + 1) * ML_DV], ones], axis=1)
        s_raw[h] = lax.dot_general(qm[h], k_ref[:, pair:pair + 2 * ML_DQK], (((1,), (1,)), ((), ())),
                                   preferred_element_type=F32)
    p, a, u_rep, m_prev, s_prev = {}, {}, {}, {}, {}
    for h in heads:
        c_row = gt[gi + h:gi + h + 1, :] - br[gf + h:gf + h + 1, :]
        e = jnp.where(mask, c_row, -jnp.inf)
        m_prev[h] = m_ref[h][0:1, 0:1]
        u = jnp.maximum(m_prev[h], jnp.max(e, axis=1, keepdims=True))
        p[h] = (s_raw[h] * jnp.exp2(e - u)).astype(BF16)
        u_rep[h] = jnp.broadcast_to(u, (T, ML_DV))
        a[h] = jnp.exp2(m_prev[h] - u_rep[h])
        s_prev[h] = s_ref[h]
    for h in heads:
        r = jnp.dot(p[h], vx[h], preferred_element_type=F32)
        qs = jnp.dot(qm[h], s_prev[h].astype(BF16), preferred_element_type=F32)
        num = r[:, :ML_DV] + a[h] * qs[:, :ML_DV]
        den = r[:, ML_DV:] + a[h] * qs[:, ML_DV:]
        b_rep = _dot_exact01(bc, gate_row == gf + h, x_on_left=True)
        floor = jnp.exp2(-(b_rep + u_rep[h]))
        o_ref[:, h * ML_DV:(h + 1) * ML_DV] = (num / jnp.maximum(jnp.abs(den), floor)).astype(o_ref.dtype)
    for h in heads:
        pair = (h // 2) * 2 * ML_DQK
        b_row = br[gf + h:gf + h + 1, :]
        tot = b_row[:, 0:1] if reverse else b_row[:, T - 1:T]
        g_row = tot - b_row + gt[gi + h:gi + h + 1, :]
        m_new = jnp.maximum(tot + m_prev[h], jnp.max(g_row, axis=1, keepdims=True))
        decay = jnp.exp2(tot + m_prev[h] - m_new)
        wk = jnp.exp2(g_row - m_new)
        own_t = (sub >= ML_DQK) if (h % 2) else (sub < ML_DQK)
        kt = kt_ref[pair:pair + 2 * ML_DQK, :].astype(F32)
        kw = jnp.where(own_t, kt * wk, 0.0).astype(BF16)
        s_ref[h] = decay * s_prev[h] + jnp.dot(kw, vx[h], preferred_element_type=F32)
        m_ref[h] = jnp.broadcast_to(m_new, m_ref.shape[1:])


def _mlstm_scan_kernel(qf_ref, kf_ref, ktf_ref, vf_ref, gf_ref, gtf_ref,
                       qb_ref, kb_ref, ktb_ref, vb_ref, gb_ref, gtb_ref,
                       of_ref, ob_ref, sf_ref, mf_ref, sb_ref, mb_ref):
    @pl.when(pl.program_id(1) == 0)
    def _():
        sf_ref[...] = jnp.zeros_like(sf_ref)
        mf_ref[...] = jnp.zeros_like(mf_ref)
        sb_ref[...] = jnp.zeros_like(sb_ref)
        mb_ref[...] = jnp.zeros_like(mb_ref)

    _mlstm_dir(False, qf_ref, kf_ref, ktf_ref, vf_ref, gf_ref, gtf_ref, of_ref, sf_ref, mf_ref)
    _mlstm_dir(True, qb_ref, kb_ref, ktb_ref, vb_ref, gb_ref, gtb_ref, ob_ref, sb_ref, mb_ref)


def mlstm_scan(lay, q, k, kt, v, g, gt):
    assert lay.ctx_first
    per = TL // ML_T
    nb = lay.nb * per
    rev = lambda j: jnp.where(j < per, per - 1 - j, nb + per - 1 - j)
    fwd = lambda b, j: (b * nb + j, 0)
    bwd = lambda b, j: (b * nb + rev(j), 0)
    fwd_t = lambda b, j: (0, b * nb + j)
    bwd_t = lambda b, j: (0, b * nb + rev(j))

    def specs(im, imt):
        return [pl.BlockSpec((ML_T, ML_QK), im), pl.BlockSpec((ML_T, ML_QK), im),
                pl.BlockSpec((ML_QK, ML_T), imt), pl.BlockSpec((ML_T, ML_V), im),
                pl.BlockSpec((ML_T, ML_NG), im), pl.BlockSpec((ML_NG, ML_T), imt)]

    out_sds = jax.ShapeDtypeStruct((lay.rows, ML_V), F32)
    state = [pltpu.VMEM((ML_HEADS, 2 * ML_DQK, ML_SW), F32),
             pltpu.VMEM((ML_HEADS, 8, LANES), F32)]
    return pl.pallas_call(
        _mlstm_scan_kernel,
        grid=(lay.n_batch, nb),
        in_specs=specs(fwd, fwd_t) + specs(bwd, bwd_t),
        out_specs=[pl.BlockSpec((ML_T, ML_V), fwd), pl.BlockSpec((ML_T, ML_V), bwd)],
        out_shape=[out_sds, out_sds],
        scratch_shapes=state + state,
        compiler_params=_params(2),
        name="mlstm_scan",
    )(q, k, kt, v, g, gt, q, k, kt, v, g, gt)


MLA_QK_PAD = 256
MLA_VW = 2 * MLA_V
MLA_IN_PAD = MLA_Q_LORA + MLA_KV_LORA + LANES
ROPE_HALF = MLA_ROPE // 4


def _mla_in_kernel(x_ref, gain_ref, sc_ref, sh_ref, win_ref, wuq_ref, wukv_ref, qn_ref, kvn_ref,
                   qnn_ref, qnr_ref, knn_ref, knr_ref, cos_ref, sa_ref, sb_ref,
                   q_out, k_out, v_out):
    h = _norm_mod(x_ref[...], gain_ref[...], sc_ref[0], sh_ref[0]).astype(BF16)
    p = jnp.dot(h, win_ref[...], preferred_element_type=F32)
    cq = _rms(p[:, :MLA_Q_LORA]) * qn_ref[...]
    ckv = _rms(p[:, MLA_Q_LORA:MLA_Q_LORA + MLA_KV_LORA]) * kvn_ref[...]
    kr = p[:, MLA_Q_LORA + MLA_KV_LORA:]
    q = jnp.dot(cq.astype(BF16), wuq_ref[...], preferred_element_type=F32)
    kv = jnp.dot(ckv.astype(BF16), wukv_ref[...], preferred_element_type=F32)
    nv = MLA_HEADS * MLA_NOPE
    ones_col = (lax.broadcasted_iota(jnp.int32, (TL, MLA_VW - MLA_V), 1) == 0).astype(BF16)
    cos, sa, sb = cos_ref[...], sa_ref[...], sb_ref[...]

    def rope(xp):
        return (xp * cos + pltpu.roll(xp, LANES - ROPE_HALF, 1) * sa + pltpu.roll(xp, ROPE_HALF, 1) * sb)

    kr = rope(_rms(kr, MLA_ROPE) * knr_ref[...]).astype(BF16)
    for hd in range(MLA_HEADS):
        c0 = hd * MLA_QK_PAD
        qn = _rms(q[:, c0:c0 + MLA_NOPE]) * qnn_ref[...]
        qr = rope(_rms(q[:, c0 + MLA_NOPE:c0 + MLA_QK_PAD], MLA_ROPE) * qnr_ref[...])
        q_out[:, c0:c0 + MLA_NOPE] = qn.astype(BF16)
        q_out[:, c0 + MLA_NOPE:c0 + MLA_QK_PAD] = qr.astype(BF16)
        v0 = nv + hd * MLA_V
        v_out[:, hd * MLA_VW:hd * MLA_VW + MLA_V] = kv[:, v0:v0 + MLA_V].astype(BF16)
        v_out[:, hd * MLA_VW + MLA_V:(hd + 1) * MLA_VW] = ones_col
        kn = _rms(kv[:, hd * MLA_NOPE:(hd + 1) * MLA_NOPE]) * knn_ref[...]
        k_out[:, c0:c0 + MLA_NOPE] = kn.astype(BF16)
        k_out[:, c0 + MLA_NOPE:c0 + MLA_QK_PAD] = kr


def _pad_lanes(g):
    return jnp.pad(g, (0, LANES - g.shape[0])).reshape(1, LANES)


def mla_in(lay, x, gain, mod, w_in, q_norm, kv_norm, w_uq, w_ukv, qn_nope, qn_rope, kn_nope, kn_rope,
           tables):
    D = D_MODEL
    Hn = MLA_HEADS
    win = jnp.pad(w_in, ((0, 0), (0, MLA_IN_PAD - w_in.shape[1]))).astype(BF16)
    wuq = jnp.pad(w_uq.reshape(MLA_Q_LORA, Hn, MLA_NOPE + MLA_ROPE),
                  ((0, 0), (0, 0), (0, MLA_QK_PAD - MLA_NOPE - MLA_ROPE)))
    wuq = wuq.reshape(MLA_Q_LORA, Hn * MLA_QK_PAD).astype(BF16)
    wkv = w_ukv.reshape(MLA_KV_LORA, Hn, MLA_NOPE + MLA_V)
    wukv = jnp.concatenate([wkv[:, :, :MLA_NOPE].reshape(MLA_KV_LORA, Hn * MLA_NOPE),
                            wkv[:, :, MLA_NOPE:].reshape(MLA_KV_LORA, Hn * MLA_V)], axis=1).astype(BF16)
    nb = lay.nb
    tab_spec = pl.BlockSpec((TL, LANES), lambda r: (r % nb, 0))
    bf = lambda w: jax.ShapeDtypeStruct((lay.rows, w), BF16)
    return pl.pallas_call(
        _mla_in_kernel,
        grid=(lay.n_blocks,),
        in_specs=[lay.row_spec(D), _const_spec((1, D)), lay.mod_spec(1), lay.mod_spec(0),
                  _const_spec(win.shape), _const_spec(wuq.shape), _const_spec(wukv.shape),
                  _const_spec((1, MLA_Q_LORA)), _const_spec((1, MLA_KV_LORA)),
                  _const_spec((1, LANES)), _const_spec((1, LANES)), _const_spec((1, LANES)),
                  _const_spec((1, LANES)), tab_spec, tab_spec, tab_spec],
        out_specs=[lay.row_spec(Hn * MLA_QK_PAD), lay.row_spec(Hn * MLA_QK_PAD), lay.row_spec(Hn * MLA_VW)],
        out_shape=[bf(Hn * MLA_QK_PAD), bf(Hn * MLA_QK_PAD), bf(Hn * MLA_VW)],
        compiler_params=_params(1),
        name="mla_in",
    )(x, gain, mod, mod, win, wuq, wukv, q_norm.reshape(1, -1), kv_norm.reshape(1, -1),
      (qn_nope * (MLA_SCALE * LOG2E)).reshape(1, -1), _pad_lanes(qn_rope * (MLA_SCALE * LOG2E)),
      kn_nope.reshape(1, -1), _pad_lanes(kn_rope), *tables)


def rope_tables(n_ctx, n_lat):
    n_freq = MLA_ROPE // 4
    inv = ROPE_THETA ** (-jnp.arange(n_freq, dtype=F32) / n_freq)
    t = jnp.arange(n_lat)
    a_r = (t // GRID_W).astype(F32)[:, None] * inv
    a_c = (t % GRID_W).astype(F32)[:, None] * inv
    ang = jnp.concatenate([a_r, a_r, a_c, a_c], axis=-1)
    ang = jnp.concatenate([jnp.zeros((n_ctx, MLA_ROPE), F32), ang], axis=0)
    cos, sin = jnp.cos(ang), jnp.sin(ang)
    low = (jnp.arange(MLA_ROPE) % (2 * ROPE_HALF)) < ROPE_HALF
    pad = lambda a: jnp.pad(a, ((0, 0), (0, LANES - MLA_ROPE)))
    return pad(cos), pad(jnp.where(low, -sin, 0.0)), pad(jnp.where(low, 0.0, sin))


MLA_HPS = 2


def _mla_attn_kernel(q_ref, k_ref, v_ref, o_ref, *, n_ctx):
    def attend(n_keys):
        heads = range(MLA_HPS)
        s, p = {}, {}
        for hd in heads:
            q = q_ref[0, :, hd * MLA_QK_PAD:(hd + 1) * MLA_QK_PAD]
            k = k_ref[0, :n_keys, hd * MLA_QK_PAD:(hd + 1) * MLA_QK_PAD]
            s[hd] = lax.dot_general(q, k, (((1,), (1,)), ((), ())), preferred_element_type=F32)
        for hd in heads:
            p[hd] = jnp.exp2(s[hd] - s[hd].max(axis=1, keepdims=True)).astype(BF16)
        for hd in heads:
            v = v_ref[0, :n_keys, hd * MLA_VW:(hd + 1) * MLA_VW]
            r = jnp.dot(p[hd], v, preferred_element_type=F32)
            o_ref[0, :, hd * MLA_V:(hd + 1) * MLA_V] = (
                r[:, :MLA_V] / r[:, MLA_V:MLA_V + 1]).astype(o_ref.dtype)

    @pl.when(pl.program_id(2) == 0)
    def _():
        attend(n_ctx)

    @pl.when(pl.program_id(2) > 0)
    def _():
        attend(k_ref.shape[1])


def mla_attention(lay, q, k, v):
    assert lay.ctx_first
    Bn, S = lay.n_batch, lay.nb * TL
    q3, k3, v3 = (a.reshape(Bn, S, a.shape[-1]) for a in (q, k, v))
    out = pl.pallas_call(
        functools.partial(_mla_attn_kernel, n_ctx=TL),
        grid=(Bn, MLA_HEADS // MLA_HPS, lay.nb),
        in_specs=[pl.BlockSpec((1, TL, MLA_HPS * MLA_QK_PAD), lambda b, h, i: (b, i, h)),
                  pl.BlockSpec((1, S, MLA_HPS * MLA_QK_PAD), lambda b, h, i: (b, 0, h)),
                  pl.BlockSpec((1, S, MLA_HPS * MLA_VW), lambda b, h, i: (b, 0, h))],
        out_specs=pl.BlockSpec((1, TL, MLA_HPS * MLA_V), lambda b, h, i: (b, i, h)),
        out_shape=jax.ShapeDtypeStruct((Bn, S, MLA_HEADS * MLA_V), BF16),
        compiler_params=_params(3),
        name="mla_attention",
    )(q3, k3, v3)
    return out.reshape(lay.rows, MLA_HEADS * MLA_V)


N_PROLOGUE = {"conv": 5, "mlstm": 4, "mla": 1}


def _mixer_out_kernel(*refs, kind, nb, ctx_first):
    n_pro = N_PROLOGUE[kind]
    pro = refs[:n_pro]
    (wout_ref, x_ref, ga_ref, gain_ref, sc_ref, sh_ref, wr_ref, br_ref,
     xo_ref, h2_ref, te_ref, gate_ref, rank_ref, cnt_ref, carry_ref) = refs[n_pro:]
    r = pl.program_id(0)

    if kind == "conv":
        vprev_ref, v_ref, vnext_ref, bg_ref, cw_ref = pro
        j = r % nb
        first = (j == 0) | (j == 1) if ctx_first else (j == 0)
        last = (j == nb - 1) | (j == 0) if ctx_first else (j == nb - 1)
        v = v_ref[...].astype(F32)
        rows = lax.broadcasted_iota(jnp.int32, (TL, 1), 0)
        prev_row = jnp.where(first, 0.0, vprev_ref[BF16_ROWS - 1:BF16_ROWS, :].astype(F32))
        next_row = jnp.where(last, 0.0, vnext_ref[0:1, :].astype(F32))
        up = jnp.where(rows == 0, prev_row, pltpu.roll(v, 1, 0))
        dn = jnp.where(rows == TL - 1, next_row, pltpu.roll(v, TL - 1, 0))
        cw = cw_ref[...]
        a = bg_ref[...].astype(F32) * (up * cw[0:1] + v * cw[1:2] + dn * cw[2:3])
    elif kind == "mlstm":
        hf_ref, hb_ref, og_ref, ng_ref = pro
        hh = hf_ref[...] + hb_ref[...]
        a = jnp.concatenate([_rms(hh[:, h * ML_DV:(h + 1) * ML_DV]) for h in range(ML_HEADS)], axis=1)
        a = a * ng_ref[...] * og_ref[...].astype(F32)
    else:
        a = pro[0][...]

    y = jnp.dot(a.astype(BF16), wout_ref[...], preferred_element_type=F32)
    xn = x_ref[...] + ga_ref[0] * y
    xo_ref[...] = xn
    h2 = _norm_mod(xn, gain_ref[...], sc_ref[0], sh_ref[0])
    h2_ref[...] = _pack_rows(h2)
    logits = _split_dot_t(wr_ref[...], h2) + br_ref[...]

    sub = lax.broadcasted_iota(jnp.int32, (N_EXPERTS, TL), 0)
    sub_k = lax.broadcasted_iota(jnp.int32, (TOP_K, TL), 0)
    work = logits
    sel = jnp.zeros((N_EXPERTS, TL), F32)
    top_e = jnp.zeros((TOP_K, TL), jnp.int32)
    top_v = jnp.zeros((TOP_K, TL), F32)
    picks = []
    for kk in range(TOP_K):
        m = work.max(axis=0, keepdims=True)
        idx = jnp.min(jnp.where(work == m, sub, N_EXPERTS), axis=0, keepdims=True)
        hit = sub == idx
        picks.append(hit)
        sel = jnp.where(hit, 1.0, sel)
        work = jnp.where(hit, -jnp.inf, work)
        top_e = jnp.where(sub_k == kk, idx, top_e)
        top_v = jnp.where(sub_k == kk, m, top_v)
    ex = jnp.exp(top_v - top_v[0:1])
    gate_ref[...] = ex / ex.sum(axis=0, keepdims=True)
    te_ref[...] = top_e

    @pl.when(r == 0)
    def _():
        carry_ref[...] = jnp.zeros_like(carry_ref)

    tr = lax.broadcasted_iota(jnp.int32, (TL, TL), 0)
    tc = lax.broadcasted_iota(jnp.int32, (TL, TL), 1)
    before = jnp.dot(sel.astype(BF16), (tr < tc).astype(BF16), preferred_element_type=F32)
    pos = before + carry_ref[...]
    rank = jnp.zeros((TOP_K, TL), F32)
    for kk in range(TOP_K):
        rk = jnp.sum(jnp.where(picks[kk], pos, 0.0), axis=0, keepdims=True)
        rank = jnp.where(sub_k == kk, rk, rank)
    rank_ref[...] = rank.astype(jnp.int32)
    total = carry_ref[...] + jnp.sum(sel, axis=1, keepdims=True)
    carry_ref[...] = total
    cnt_ref[...] = total


def mixer_out(lay, kind, pro_args, w_out, x, mod, gain_f, w_r, b_r):
    D = D_MODEL
    nb = lay.nb
    if kind == "conv":
        v, bg, cw = pro_args
        per = TL // BF16_ROWS
        last_tile = lay.rows // BF16_ROWS - 1
        pro_specs = [pl.BlockSpec((BF16_ROWS, D), lambda r: (jnp.maximum(r * per - 1, 0), 0)),
                     lay.row_spec(D),
                     pl.BlockSpec((BF16_ROWS, D), lambda r: (jnp.minimum((r + 1) * per, last_tile), 0)),
                     lay.row_spec(D), _const_spec((CONV_WIDTH, D))]
        pro_in = [v, v, v, bg, cw]
    elif kind == "mlstm":
        h_f, h_b, og, ng = pro_args
        pro_specs = [lay.row_spec(ML_V), lay.row_spec(ML_V), lay.row_spec(ML_V), _const_spec((1, ML_V))]
        pro_in = [h_f, h_b, og, ng.reshape(1, ML_V)]
    else:
        pro_specs = [lay.row_spec(D)]
        pro_in = list(pro_args)
    k_in = w_out.shape[0]
    small = lambda dt: jax.ShapeDtypeStruct((TOP_K, lay.rows), dt)
    small_spec = pl.BlockSpec((TOP_K, TL), lambda r: (0, r))
    return pl.pallas_call(
        functools.partial(_mixer_out_kernel, kind=kind, nb=nb, ctx_first=lay.ctx_first),
        grid=(lay.n_blocks,),
        in_specs=pro_specs + [_const_spec((k_in, D)), lay.row_spec(D), lay.mod_spec(2),
                              _const_spec((1, D)), lay.mod_spec(4), lay.mod_spec(3),
                              _const_spec((2 * N_EXPERTS, D)), _const_spec((N_EXPERTS, 1))],
        out_specs=[lay.row_spec(D), lay.row_spec(PACK_W), small_spec, small_spec, small_spec,
                   _const_spec((N_EXPERTS, 1))],
        out_shape=[jax.ShapeDtypeStruct((lay.rows, D), F32),
                   jax.ShapeDtypeStruct((lay.rows, PACK_W), jnp.int32),
                   small(jnp.int32), small(F32), small(jnp.int32),
                   jax.ShapeDtypeStruct((N_EXPERTS, 1), F32)],
        scratch_shapes=[pltpu.VMEM((N_EXPERTS, 1), F32)],
        compiler_params=_params(1),
        name="mixer_out_" + kind,
    )(*pro_in, w_out.astype(BF16), x, mod, gain_f, mod, mod, _split_weight_t(w_r),
      b_r.reshape(N_EXPERTS, 1))


def _expert_ffn_kernel(blk_e_ref, first_ref, used_ref, x_ref, w1_ref, b1_ref, w2_ref, b2_ref, o_ref,
                       w1b_ref, w2b_ref):
    del blk_e_ref
    i = pl.program_id(0)

    @pl.when(first_ref[i] == 1)
    def _():
        w1b_ref[...] = w1_ref[0, 0].astype(BF16)
        w2b_ref[...] = w2_ref[0, 0].astype(BF16)

    @pl.when(i < used_ref[0])
    def _():
        x = _unpack_rows(x_ref[...], BF16)
        h = jnp.dot(x, w1b_ref[...], preferred_element_type=F32) + b1_ref[0, 0]
        glu = jnp.minimum(h[:, :MOE_FF], SWIGLU_LIMIT)
        lin = jnp.clip(h[:, MOE_FF:], -SWIGLU_LIMIT, SWIGLU_LIMIT)
        act = glu * jax.nn.sigmoid(SWIGLU_ALPHA * glu) * (lin + 1.0)
        y = jnp.dot(act.astype(BF16), w2b_ref[...], preferred_element_type=F32)
        o_ref[...] = _pack_rows(y + b2_ref[0, 0])

    @pl.when(i >= used_ref[0])
    def _():
        o_ref[...] = jnp.zeros_like(o_ref)


def expert_ffn(layer, xp, blk_e, blk_first, n_used, w1, b1, w2, b2):
    n_rows = xp.shape[0]
    D, F2 = D_MODEL, 2 * MOE_FF
    n_blk = n_rows // MOE_BLOCK
    grid_spec = pltpu.PrefetchScalarGridSpec(
        num_scalar_prefetch=3,
        grid=(n_blk,),
        in_specs=[
            pl.BlockSpec((MOE_BLOCK, PACK_W), lambda i, be, fi, nu: (i, 0)),
            pl.BlockSpec((1, 1, D, F2), lambda i, be, fi, nu: (layer, be[i], 0, 0)),
            pl.BlockSpec((1, 1, 1, F2), lambda i, be, fi, nu: (layer, be[i], 0, 0)),
            pl.BlockSpec((1, 1, MOE_FF, D), lambda i, be, fi, nu: (layer, be[i], 0, 0)),
            pl.BlockSpec((1, 1, 1, D), lambda i, be, fi, nu: (layer, be[i], 0, 0)),
        ],
        out_specs=pl.BlockSpec((MOE_BLOCK, PACK_W), lambda i, be, fi, nu: (i, 0)),
        scratch_shapes=[pltpu.VMEM((D, F2), BF16), pltpu.VMEM((MOE_FF, D), BF16)],
    )
    return pl.pallas_call(
        _expert_ffn_kernel,
        grid_spec=grid_spec,
        out_shape=jax.ShapeDtypeStruct((n_rows, PACK_W), jnp.int32),
        compiler_params=_params(1),
        name="expert_ffn",
    )(blk_e, blk_first, n_used, xp, w1, b1, w2, b2)


SC_CORES = 2
SC_SUBCORES = 16
SC_CHUNK = 64


def sc_gather(table, idx):
    n_idx = idx.shape[0]
    width = table.shape[1]
    n_workers = SC_CORES * SC_SUBCORES
    per_worker = n_idx // n_workers
    n_chunks = per_worker // SC_CHUNK
    assert n_chunks * SC_CHUNK * n_workers == n_idx and n_chunks % 2 == 0
    mesh = plsc.VectorSubcoreMesh(core_axis_name="c", subcore_axis_name="s",
                                  num_cores=SC_CORES, num_subcores=SC_SUBCORES)

    def body(table_hbm, idx_hbm, out_hbm, idx_v, rows_v, gsem, wsem):
        wid = lax.axis_index("s") * SC_CORES + lax.axis_index("c")
        pltpu.sync_copy(idx_hbm.at[wid], idx_v)

        def gather(ci, slot):
            return pltpu.make_async_copy(table_hbm.at[idx_v.at[ci]], rows_v.at[slot], gsem.at[slot])

        def write(ci, slot):
            return pltpu.make_async_copy(rows_v.at[slot], out_hbm.at[ci, wid], wsem.at[slot])

        gather(0, 0).start()

        @pl.loop(0, n_chunks, step=2)
        def _(c0):
            for slot in range(2):
                ci = c0 + slot
                other = 1 - slot

                @pl.when(ci + 1 < n_chunks)
                def _():
                    @pl.when(ci >= 1)
                    def _():
                        write(ci - 1, other).wait()
                    gather(ci + 1, other).start()

                gather(ci, slot).wait()
                write(ci, slot).start()

        write(n_chunks - 2, 0).wait()
        write(n_chunks - 1, 1).wait()

    out = pl.kernel(
        body,
        out_type=jax.ShapeDtypeStruct((n_chunks, n_workers, SC_CHUNK, width), table.dtype),
        mesh=mesh,
        scratch_types=[pltpu.VMEM((n_chunks, SC_CHUNK), jnp.int32),
                       pltpu.VMEM((2, SC_CHUNK, width), table.dtype),
                       pltpu.SemaphoreType.DMA((2,)),
                       pltpu.SemaphoreType.DMA((2,))],
        name="sc_gather",
    )(table, idx.reshape(n_chunks, n_workers, SC_CHUNK).transpose(1, 0, 2))
    return out.reshape(n_idx, width)


def sc_dispatch(table, dest, pad_rows):
    n_tok, width = table.shape
    n_picks = dest.shape[0]
    n_pad = pad_rows.shape[0]
    n_workers = SC_CORES * SC_SUBCORES
    per_w = n_tok // SC_CHUNK // n_workers
    pad_w = n_pad // SC_CHUNK // n_workers
    assert per_w * SC_CHUNK * n_workers == n_tok and pad_w * SC_CHUNK * n_workers == n_pad
    mesh = plsc.VectorSubcoreMesh(core_axis_name="c", subcore_axis_name="s",
                                  num_cores=SC_CORES, num_subcores=SC_SUBCORES)

    def body(table_hbm, idx_hbm, pad_hbm, zero_hbm, out_hbm, idx_v, pad_v, rows_v, zero_v,
             rsem, ssem, zsem):
        wid = lax.axis_index("s") * SC_CORES + lax.axis_index("c")
        pltpu.sync_copy(idx_hbm.at[wid], idx_v)
        pltpu.sync_copy(pad_hbm.at[wid], pad_v)
        pltpu.sync_copy(zero_hbm, zero_v)

        def zero_fill(pc):
            return pltpu.make_async_copy(zero_v, out_hbm.at[pad_v.at[pc]], zsem)

        for pc in range(pad_w):
            zero_fill(pc).start()

        def scatter(ci, kk):
            return pltpu.make_async_copy(rows_v, out_hbm.at[idx_v.at[ci * n_picks + kk]], ssem)

        @pl.loop(0, per_w)
        def _(ci):
            pltpu.async_copy(table_hbm.at[ci, wid], rows_v, rsem).wait()
            for kk in range(n_picks):
                scatter(ci, kk).start()
            for kk in range(n_picks):
                scatter(ci, kk).wait()

        for pc in range(pad_w):
            zero_fill(pc).wait()

    idx = dest.reshape(n_picks, per_w, n_workers, SC_CHUNK).transpose(2, 1, 0, 3)
    idx = idx.reshape(n_workers, per_w * n_picks, SC_CHUNK)
    return pl.kernel(
        body,
        out_type=jax.ShapeDtypeStruct((n_tok * n_picks + n_pad, width), table.dtype),
        mesh=mesh,
        scratch_types=[pltpu.VMEM((per_w * n_picks, SC_CHUNK), jnp.int32),
                       pltpu.VMEM((pad_w, SC_CHUNK), jnp.int32),
                       pltpu.VMEM((SC_CHUNK, width), table.dtype),
                       pltpu.VMEM((SC_CHUNK, width), table.dtype),
                       pltpu.SemaphoreType.DMA, pltpu.SemaphoreType.DMA, pltpu.SemaphoreType.DMA],
        name="sc_dispatch",
    )(table.reshape(per_w, n_workers, SC_CHUNK, width), idx,
      pad_rows.reshape(n_workers, pad_w, SC_CHUNK), jnp.zeros((SC_CHUNK, width), table.dtype))


def _combine_kernel(x_ref, *refs):
    y_refs, (gate_ref, gf_ref, o_ref) = refs[:TOP_K], refs[TOP_K:]
    gates = gate_ref[...]
    acc = gates[:, 0:1] * _unpack_rows(y_refs[0][...], F32)
    for kk in range(1, TOP_K):
        acc = acc + gates[:, kk:kk + 1] * _unpack_rows(y_refs[kk][...], F32)
    o_ref[...] = x_ref[...] + gf_ref[0] * acc


def moe_combine(lay, x, yg, gates, mod, drop_ctx):
    D = D_MODEL
    if drop_ctx:
        nbo = lay.nb - 1
        src = lambda r: (r // nbo) * lay.nb + 1 + r % nbo
        n_out = lay.n_batch * nbo
    else:
        src = lambda r: r
        n_out = lay.n_blocks
    y_specs = [pl.BlockSpec((TL, PACK_W), functools.partial(lambda kk, r: (kk * lay.n_blocks + src(r), 0), kk))
               for kk in range(TOP_K)]
    return pl.pallas_call(
        _combine_kernel,
        grid=(n_out,),
        in_specs=[pl.BlockSpec((TL, D), lambda r: (src(r), 0))] + y_specs + [
            pl.BlockSpec((TL, TOP_K), lambda r: (src(r), 0)),
            pl.BlockSpec((1, 1, D), lambda r: (lay.mod_row(src(r)), 0, 5))],
        out_specs=pl.BlockSpec((TL, D), lambda r: (r, 0)),
        out_shape=jax.ShapeDtypeStruct((n_out * TL, D), F32),
        compiler_params=_params(1),
        name="moe_combine",
    )(x, yg, yg, yg, yg, gates, mod)


def moe_route(top_e, rank, counts):
    T = top_e.shape[1]
    assert (T * TOP_K) % MOE_BLOCK == 0
    counts = counts.reshape(N_EXPERTS).astype(jnp.int32)
    padded = (counts + MOE_BLOCK - 1) // MOE_BLOCK * MOE_BLOCK
    padded_end = jnp.cumsum(padded)
    padded_start = padded_end - padded
    experts = jnp.arange(N_EXPERTS)
    start_of = jnp.sum(jnp.where(top_e[..., None] == experts, padded_start, 0), axis=-1)
    dest = (start_of + rank).astype(jnp.int32)
    n_pad = N_EXPERTS * MOE_BLOCK
    n_rows = T * TOP_K + n_pad
    n_blk = n_rows // MOE_BLOCK
    blk_start = jnp.arange(n_blk) * MOE_BLOCK
    blk_e = jnp.minimum(jnp.sum(padded_end[None, :] <= blk_start[:, None], axis=1), N_EXPERTS - 1)
    blk_e = blk_e.astype(jnp.int32)
    blk_first = jnp.concatenate([jnp.ones((1,), jnp.int32), (blk_e[1:] != blk_e[:-1]).astype(jnp.int32)])
    n_used = (padded_end[-1:] // MOE_BLOCK).astype(jnp.int32)
    tail = padded - counts
    tail_end = jnp.cumsum(tail)
    j = jnp.arange(n_pad)
    owner = jnp.sum(tail_end[None, :] <= j[:, None], axis=1)
    base = padded_start + counts - (tail_end - tail)
    in_group = jnp.sum(jnp.where(owner[:, None] == experts, base, 0), axis=-1) + j
    pad_rows = jnp.where(j < tail_end[-1], in_group, padded_end[-1] + j - tail_end[-1])
    return dest, pad_rows.astype(jnp.int32), blk_e, blk_first, n_used


def kernel(x, c, ctx, c_ctx, norm_mix, norm_ffn, w_mod, b_mod, conv_w_in, conv_w, conv_w_out, ml_w_in, ml_b_gate, ml_norm, ml_w_out, mla_w_in, mla_q_norm, mla_kv_norm, mla_w_uq, mla_w_ukv, mla_qn_nope, mla_qn_rope, mla_kn_nope, mla_kn_rope, mla_w_out, moe_w_router, moe_b_router, moe_w1, moe_b1, moe_w2, moe_b2):
    Bn, n_lat, D = x.shape
    n_ctx = ctx.shape[1]
    assert D == D_MODEL and n_ctx == TL and n_lat % TL == 0
    assert (DEPTH - 1) % N_MIXERS == 0
    full = Layout(Bn, (n_ctx + n_lat) // TL, True)
    lat_only = Layout(Bn, n_lat // TL, False)
    mods = ada_all(c, c_ctx, w_mod, b_mod)
    tables = rope_tables(n_ctx, n_lat)
    b1_all = moe_b1.reshape(DEPTH, N_EXPERTS, 1, 2 * MOE_FF)
    b2_all = moe_b2.reshape(DEPTH, N_EXPERTS, 1, D)
    X = jnp.concatenate([ctx, x], axis=1).reshape(full.rows, D)
    for layer in range(DEPTH):
        kind, j = layer % N_MIXERS, layer // N_MIXERS
        last = layer == DEPTH - 1
        lay = lat_only if last else full
        mod = mods[layer]
        gain_a = norm_mix[layer].reshape(1, D)
        gain_f = norm_ffn[layer].reshape(1, D)
        if kind == 0:
            bg, v = conv_in(lay, X, gain_a, mod, conv_w_in[j].astype(BF16))
            pro, w_out, name = (v, bg, conv_w[j]), conv_w_out[j], "conv"
        elif kind == 1:
            w = ml_w_in[j]
            n_main = 2 * ML_QK + 2 * ML_V
            w_main = jnp.concatenate([w[:, :ML_QK] * ML_DQK ** -0.5, w[:, ML_QK:n_main]], axis=1)
            q, k, kt, v, og, g, gt = mlstm_in(lay, X, gain_a, mod, w_main.astype(BF16), w[:, n_main:],
                                              ml_b_gate[j])
            h_f, h_b = mlstm_scan(lay, q, k, kt, v, g, gt)
            pro, w_out, name = (h_f, h_b, og, ml_norm[j]), ml_w_out[j], "mlstm"
        else:
            q, k, v = mla_in(lay, X, gain_a, mod, mla_w_in[j], mla_q_norm[j], mla_kv_norm[j],
                             mla_w_uq[j], mla_w_ukv[j], mla_qn_nope[j], mla_qn_rope[j],
                             mla_kn_nope[j], mla_kn_rope[j], tables)
            pro, w_out, name = (mla_attention(lay, q, k, v),), mla_w_out[j], "mla"
        X, h2, top_e, gates, rank, counts = mixer_out(
            lay, name, pro, w_out, X, mod, gain_f, moe_w_router[layer], moe_b_router[layer])
        dest, pad_rows, blk_e, blk_first, n_used = moe_route(top_e, rank, counts)
        xp = sc_dispatch(h2, dest, pad_rows)
        yp = expert_ffn(layer, xp, blk_e, blk_first, n_used, moe_w1, b1_all, moe_w2, b2_all)
        yg = sc_gather(yp, dest.reshape(-1))
        X = moe_combine(lay, X, yg, gates.T, mod, drop_ctx=(layer == DEPTH - 2))
    return X.reshape(Bn, n_lat, D)
```

```python
import functools

import jax
import jax.numpy as jnp
from jax import lax
from jax.experimental import pallas as pl
from jax.experimental.pallas import tpu as pltpu
from jax.experimental.pallas import tpu_sc as plsc

D_MODEL = 1024
DEPTH = 4
GRID_W = 64
N_MIXERS = 3
N_ADA = 6
RMS_EPS = 1e-6
CONV_WIDTH = 3
ML_HEADS = 8
ML_DQK = 64
ML_DV = 128
ML_QK = ML_HEADS * ML_DQK
ML_V = ML_HEADS * ML_DV
GATE_CAP = 15.0
MLA_HEADS = 8
MLA_NOPE = 128
MLA_ROPE = 64
MLA_V = 128
MLA_Q_LORA = 384
MLA_KV_LORA = 256
MLA_SCALE = (MLA_NOPE + MLA_ROPE) ** -0.5
ROPE_THETA = 10000.0
N_EXPERTS = 32
TOP_K = 4
MOE_FF = D_MODEL
SWIGLU_ALPHA = 1.702
SWIGLU_LIMIT = 7.0
MOE_BLOCK = 512

TL = 256
LANES = 128
BF16_ROWS = 16
VMEM_LIMIT = 48 * 1024 * 1024
HI = lax.Precision.HIGHEST
F32 = jnp.float32
BF16 = jnp.bfloat16


def _params(n_axes):
    return pltpu.CompilerParams(dimension_semantics=("arbitrary",) * n_axes,
                                vmem_limit_bytes=VMEM_LIMIT)


def _rms(x, width=None):
    width = x.shape[-1] if width is None else width
    return x * lax.rsqrt(jnp.sum(x * x, axis=-1, keepdims=True) * (1.0 / width) + RMS_EPS)


def _norm_mod(x, gain, scale, shift):
    return _rms(x) * (gain * (1.0 + scale)) + shift


def _split_weight_t(w):
    hi = w.astype(BF16)
    lo = (w - hi.astype(F32)).astype(BF16)
    return jnp.concatenate([hi.T, lo.T], axis=0)


def _split_dot_t(w2, h):
    n = w2.shape[0] // 2
    dn = (((1,), (1,)), ((), ()))
    h_hi = h.astype(BF16)
    h_lo = (h - h_hi.astype(F32)).astype(BF16)
    both = lax.dot_general(w2, h_hi, dn, preferred_element_type=F32)
    cross = lax.dot_general(w2[:n], h_lo, dn, preferred_element_type=F32)
    return both[:n] + both[n:] + cross


PACK_W = D_MODEL // 2
HIGH_HALF = -65536


def _pack_rows(x):
    xb = x.astype(BF16).astype(F32)
    lo = lax.bitcast_convert_type(xb[:, :PACK_W], jnp.int32)
    hi = lax.bitcast_convert_type(xb[:, PACK_W:], jnp.int32)
    return hi | lax.shift_right_logical(lo, 16)


def _unpack_rows(w, dtype):
    lo = lax.bitcast_convert_type(lax.shift_left(w, 16), F32)
    hi = lax.bitcast_convert_type(w & HIGH_HALF, F32)
    return jnp.concatenate([lo.astype(dtype), hi.astype(dtype)], axis=1)


class Layout:
    def __init__(self, n_batch, nb, ctx_first):
        self.n_batch, self.nb, self.ctx_first = n_batch, nb, ctx_first
        self.n_blocks = n_batch * nb
        self.rows = self.n_blocks * TL

    def mod_row(self, r):
        b = r // self.nb
        return jnp.where(r % self.nb == 0, self.n_batch, b) if self.ctx_first else b

    def row_spec(self, width):
        return pl.BlockSpec((TL, width), lambda r: (r, 0))

    def mod_spec(self, piece):
        return pl.BlockSpec((1, 1, D_MODEL), lambda r: (self.mod_row(r), 0, piece))


def _const_spec(shape):
    return pl.BlockSpec(shape, lambda *_: (0,) * len(shape))


ADA_ROWS = 16
ADA_TN = 1536


def _ada_kernel(c_ref, w_ref, b_ref, o_ref):
    c = c_ref[...]
    s = c * jax.nn.sigmoid(c)
    o_ref[0] = jnp.dot(s, w_ref[0], precision=HI, preferred_element_type=F32) + b_ref[0]


def ada_all(c, c_ctx, w_mod, b_mod):
    Bn, D = c.shape
    cond = jnp.zeros((ADA_ROWS, D), F32).at[:Bn].set(c).at[Bn].set(c_ctx)
    out = pl.pallas_call(
        _ada_kernel,
        grid=(DEPTH, N_ADA * D // ADA_TN),
        in_specs=[pl.BlockSpec((ADA_ROWS, D), lambda l, n: (0, 0)),
                  pl.BlockSpec((1, D, ADA_TN), lambda l, n: (l, 0, n)),
                  pl.BlockSpec((1, 1, ADA_TN), lambda l, n: (l, 0, n))],
        out_specs=pl.BlockSpec((1, ADA_ROWS, ADA_TN), lambda l, n: (l, 0, n)),
        out_shape=jax.ShapeDtypeStruct((DEPTH, ADA_ROWS, N_ADA * D), F32),
        compiler_params=_params(2),
        name="ada_mod",
    )(cond, w_mod, b_mod.reshape(DEPTH, 1, N_ADA * D))
    return out[:, :Bn + 1, None, :]


def _conv_in_kernel(x_ref, gain_ref, sc_ref, sh_ref, w_ref, bg_ref, v_ref):
    D = D_MODEL
    h = _norm_mod(x_ref[...], gain_ref[...], sc_ref[0], sh_ref[0]).astype(BF16)
    p = jnp.dot(h, w_ref[...], preferred_element_type=F32)
    bg_ref[...] = p[:, :D].astype(BF16)
    v_ref[...] = (p[:, D:2 * D] * p[:, 2 * D:]).astype(BF16)


def conv_in(lay, x, gain, mod, w_in):
    D = D_MODEL
    sds = jax.ShapeDtypeStruct((lay.rows, D), BF16)
    return pl.pallas_call(
        _conv_in_kernel,
        grid=(lay.n_blocks,),
        in_specs=[lay.row_spec(D), _const_spec((1, D)), lay.mod_spec(1), lay.mod_spec(0),
                  _const_spec((D, 3 * D))],
        out_specs=[lay.row_spec(D), lay.row_spec(D)],
        out_shape=[sds, sds],
        compiler_params=_params(1),
        name="conv_in",
    )(x, gain, mod, mod, w_in)


ML_T = 256
ML_SW = 2 * ML_DV
ML_NG = 4 * ML_HEADS


LOG2E = 1.4426950408889634


def _gate_act(g, is_forget):
    g = GATE_CAP * jnp.tanh(g * (1.0 / GATE_CAP))
    log_sig = jnp.minimum(g, 0.0) - jnp.log(1.0 + jnp.exp(-jnp.abs(g)))
    return jnp.where(is_forget, log_sig, g) * LOG2E


def _mlstm_in_kernel(x_ref, gain_ref, sc_ref, sh_ref, w_ref, wkt_ref, wg_ref, bgt_ref,
                     q_ref, k_ref, kt_ref, v_ref, og_ref, g_ref, gt_ref):
    h = _norm_mod(x_ref[...], gain_ref[...], sc_ref[0], sh_ref[0])
    hb = h.astype(BF16)
    p = jnp.dot(hb, w_ref[...], preferred_element_type=F32)
    q_ref[...] = p[:, :ML_QK].astype(BF16)
    k_ref[...] = p[:, ML_QK:2 * ML_QK].astype(BF16)
    kt_ref[...] = lax.dot_general(wkt_ref[...], hb, (((1,), (1,)), ((), ())),
                                  preferred_element_type=F32).astype(BF16)
    v_ref[...] = p[:, 2 * ML_QK:2 * ML_QK + ML_V].astype(BF16)
    og_ref[...] = jax.nn.sigmoid(p[:, 2 * ML_QK + ML_V:]).astype(BF16)
    gt = _split_dot_t(wg_ref[...], h) + bgt_ref[...]
    row = lax.broadcasted_iota(jnp.int32, gt.shape, 0)
    gt = _gate_act(gt, (row // ML_HEADS) % 2 == 1)
    gt_ref[...] = gt
    eye = (lax.broadcasted_iota(jnp.int32, (TL, TL), 0)
           == lax.broadcasted_iota(jnp.int32, (TL, TL), 1)).astype(F32)
    g_ref[...] = lax.dot_general(eye, gt, (((1,), (1,)), ((), ())), precision=HI,
                                 preferred_element_type=F32)


def mlstm_in(lay, x, gain, mod, w_main, w_g, b_g):
    D = D_MODEL
    n_main = 2 * ML_QK + 2 * ML_V
    bf = lambda w: jax.ShapeDtypeStruct((lay.rows, w), BF16)
    return pl.pallas_call(
        _mlstm_in_kernel,
        grid=(lay.n_blocks,),
        in_specs=[lay.row_spec(D), _const_spec((1, D)), lay.mod_spec(1), lay.mod_spec(0),
                  _const_spec((D, n_main)), _const_spec((ML_QK, D)), _const_spec((2 * ML_NG, D)),
                  _const_spec((ML_NG, 1))],
        out_specs=[lay.row_spec(ML_QK), lay.row_spec(ML_QK), pl.BlockSpec((ML_QK, TL), lambda r: (0, r)),
                   lay.row_spec(ML_V), lay.row_spec(ML_V),
                   lay.row_spec(ML_NG), pl.BlockSpec((ML_NG, TL), lambda r: (0, r))],
        out_shape=[bf(ML_QK), bf(ML_QK), jax.ShapeDtypeStruct((ML_QK, lay.rows), BF16),
                   bf(ML_V), bf(ML_V),
                   jax.ShapeDtypeStruct((lay.rows, ML_NG), F32),
                   jax.ShapeDtypeStruct((ML_NG, lay.rows), F32)],
        compiler_params=_params(1),
        name="mlstm_in",
    )(x, gain, mod, mod, w_main, w_main[:, ML_QK:2 * ML_QK].T, _split_weight_t(w_g),
      b_g.reshape(ML_NG, 1))


def _split3(x):
    hi = x.astype(BF16)
    r1 = x - hi.astype(F32)
    mid = r1.astype(BF16)
    lo = (r1 - mid.astype(F32)).astype(BF16)
    return hi, mid, lo


def _dot_exact01(x, sel01, x_on_left):
    sel = sel01.astype(BF16)
    parts = [jnp.dot(p, sel, preferred_element_type=F32) if x_on_left
             else jnp.dot(sel, p, preferred_element_type=F32) for p in _split3(x)]
    return parts[0] + parts[1] + parts[2]


def _mlstm_dir(reverse, q_ref, k_ref, kt_ref, v_ref, g_ref, gt_ref, o_ref, s_ref, m_ref):
    T = ML_T
    row = lax.broadcasted_iota(jnp.int32, (T, T), 0)
    col = lax.broadcasted_iota(jnp.int32, (T, T), 1)
    mask = (col >= row) if reverse else (col <= row)
    gt = gt_ref[...]
    bc = _dot_exact01(g_ref[...], mask, x_on_left=False)
    br = _dot_exact01(gt, (row >= col) if reverse else (row <= col), x_on_left=True)
    gi, gf = (2 * ML_HEADS, 3 * ML_HEADS) if reverse else (0, ML_HEADS)
    lane = lax.broadcasted_iota(jnp.int32, (T, 2 * ML_DQK), 1)
    sub = lax.broadcasted_iota(jnp.int32, (2 * ML_DQK, T), 0)
    gate_row = lax.broadcasted_iota(jnp.int32, (ML_NG, ML_DV), 0)
    ones = jnp.ones((T, ML_DV), BF16)
    heads = range(ML_HEADS)
    qm, vx, s_raw = {}, {}, {}
    for h in heads:
        pair = (h // 2) * 2 * ML_DQK
        own = (lane >= ML_DQK) if (h % 2) else (lane < ML_DQK)
        qp = q_ref[:, pair:pair + 2 * ML_DQK]
        qm[h] = jnp.where(own, qp, jnp.zeros_like(qp))
        vx[h] = jnp.concatenate([v_ref[:, h * ML_DV:(h + 1) * ML_DV], ones], axis=1)
        s_raw[h] = lax.dot_general(qm[h], k_ref[:, pair:pair + 2 * ML_DQK], (((1,), (1,)), ((), ())),
                                   preferred_element_type=F32)
    p, a, u_rep, m_prev, s_prev = {}, {}, {}, {}, {}
    for h in heads:
        c_row = gt[gi + h:gi + h + 1, :] - br[gf + h:gf + h + 1, :]
        e = jnp.where(mask, c_row, -jnp.inf)
        m_prev[h] = m_ref[h][0:1, 0:1]
        u = jnp.maximum(m_prev[h], jnp.max(e, axis=1, keepdims=True))
        p[h] = (s_raw[h] * jnp.exp2(e - u)).astype(BF16)
        u_rep[h] = jnp.broadcast_to(u, (T, ML_DV))
        a[h] = jnp.exp2(m_prev[h] - u_rep[h])
        s_prev[h] = s_ref[h]
    for h in heads:
        r = jnp.dot(p[h], vx[h], preferred_element_type=F32)
        qs = jnp.dot(qm[h], s_prev[h].astype(BF16), preferred_element_type=F32)
        num = r[:, :ML_DV] + a[h] * qs[:, :ML_DV]
        den = r[:, ML_DV:] + a[h] * qs[:, ML_DV:]
        b_rep = _dot_exact01(bc, gate_row == gf + h, x_on_left=True)
        floor = jnp.exp2(-(b_rep + u_rep[h]))
        o_ref[:, h * ML_DV:(h + 1) * ML_DV] = (num / jnp.maximum(jnp.abs(den), floor)).astype(o_ref.dtype)
    for h in heads:
        pair = (h // 2) * 2 * ML_DQK
        b_row = br[gf + h:gf + h + 1, :]
        tot = b_row[:, 0:1] if reverse else b_row[:, T - 1:T]
        g_row = tot - b_row + gt[gi + h:gi + h + 1, :]
        m_new = jnp.maximum(tot + m_prev[h], jnp.max(g_row, axis=1, keepdims=True))
        decay = jnp.exp2(tot + m_prev[h] - m_new)
        wk = jnp.exp2(g_row - m_new)
        own_t = (sub >= ML_DQK) if (h % 2) else (sub < ML_DQK)
        kt = kt_ref[pair:pair + 2 * ML_DQK, :].astype(F32)
        kw = jnp.where(own_t, kt * wk, 0.0).astype(BF16)
        s_ref[h] = decay * s_prev[h] + jnp.dot(kw, vx[h], preferred_element_type=F32)
        m_ref[h] = jnp.broadcast_to(m_new, m_ref.shape[1:])


def _mlstm_scan_kernel(qf_ref, kf_ref, ktf_ref, vf_ref, gf_ref, gtf_ref,
                       qb_ref, kb_ref, ktb_ref, vb_ref, gb_ref, gtb_ref,
                       of_ref, ob_ref, sf_ref, mf_ref, sb_ref, mb_ref):
    @pl.when(pl.program_id(1) == 0)
    def _():
        sf_ref[...] = jnp.zeros_like(sf_ref)
        mf_ref[...] = jnp.zeros_like(mf_ref)
        sb_ref[...] = jnp.zeros_like(sb_ref)
        mb_ref[...] = jnp.zeros_like(mb_ref)

    _mlstm_dir(False, qf_ref, kf_ref, ktf_ref, vf_ref, gf_ref, gtf_ref, of_ref, sf_ref, mf_ref)
    _mlstm_dir(True, qb_ref, kb_ref, ktb_ref, vb_ref, gb_ref, gtb_ref, ob_ref, sb_ref, mb_ref)


def mlstm_scan(lay, q, k, kt, v, g, gt):
    assert lay.ctx_first
    per = TL // ML_T
    nb = lay.nb * per
    rev = lambda j: jnp.where(j < per, per - 1 - j, nb + per - 1 - j)
    fwd = lambda b, j: (b * nb + j, 0)
    bwd = lambda b, j: (b * nb + rev(j), 0)
    fwd_t = lambda b, j: (0, b * nb + j)
    bwd_t = lambda b, j: (0, b * nb + rev(j))

    def specs(im, imt):
        return [pl.BlockSpec((ML_T, ML_QK), im), pl.BlockSpec((ML_T, ML_QK), im),
                pl.BlockSpec((ML_QK, ML_T), imt), pl.BlockSpec((ML_T, ML_V), im),
                pl.BlockSpec((ML_T, ML_NG), im), pl.BlockSpec((ML_NG, ML_T), imt)]

    out_sds = jax.ShapeDtypeStruct((lay.rows, ML_V), F32)
    state = [pltpu.VMEM((ML_HEADS, 2 * ML_DQK, ML_SW), F32),
             pltpu.VMEM((ML_HEADS, 8, LANES), F32)]
    return pl.pallas_call(
        _mlstm_scan_kernel,
        grid=(lay.n_batch, nb),
        in_specs=specs(fwd, fwd_t) + specs(bwd, bwd_t),
        out_specs=[pl.BlockSpec((ML_T, ML_V), fwd), pl.BlockSpec((ML_T, ML_V), bwd)],
        out_shape=[out_sds, out_sds],
        scratch_shapes=state + state,
        compiler_params=_params(2),
        name="mlstm_scan",
    )(q, k, kt, v, g, gt, q, k, kt, v, g, gt)


MLA_QK_PAD = 256
MLA_VW = 2 * MLA_V
MLA_IN_PAD = MLA_Q_LORA + MLA_KV_LORA + LANES
ROPE_HALF = MLA_ROPE // 4


def _mla_in_kernel(x_ref, gain_ref, sc_ref, sh_ref, win_ref, wuq_ref, wukv_ref, qn_ref, kvn_ref,
                   qnn_ref, qnr_ref, knn_ref, knr_ref, cos_ref, sa_ref, sb_ref,
                   q_out, k_out, v_out):
    h = _norm_mod(x_ref[...], gain_ref[...], sc_ref[0], sh_ref[0]).astype(BF16)
    p = jnp.dot(h, win_ref[...], preferred_element_type=F32)
    cq = _rms(p[:, :MLA_Q_LORA]) * qn_ref[...]
    ckv = _rms(p[:, MLA_Q_LORA:MLA_Q_LORA + MLA_KV_LORA]) * kvn_ref[...]
    kr = p[:, MLA_Q_LORA + MLA_KV_LORA:]
    q = jnp.dot(cq.astype(BF16), wuq_ref[...], preferred_element_type=F32)
    kv = jnp.dot(ckv.astype(BF16), wukv_ref[...], preferred_element_type=F32)
    nv = MLA_HEADS * MLA_NOPE
    ones_col = (lax.broadcasted_iota(jnp.int32, (TL, MLA_VW - MLA_V), 1) == 0).astype(BF16)
    cos, sa, sb = cos_ref[...], sa_ref[...], sb_ref[...]

    def rope(xp):
        return (xp * cos + pltpu.roll(xp, LANES - ROPE_HALF, 1) * sa + pltpu.roll(xp, ROPE_HALF, 1) * sb)

    kr = rope(_rms(kr, MLA_ROPE) * knr_ref[...]).astype(BF16)
    for hd in range(MLA_HEADS):
        c0 = hd * MLA_QK_PAD
        qn = _rms(q[:, c0:c0 + MLA_NOPE]) * qnn_ref[...]
        qr = rope(_rms(q[:, c0 + MLA_NOPE:c0 + MLA_QK_PAD], MLA_ROPE) * qnr_ref[...])
        q_out[:, c0:c0 + MLA_NOPE] = qn.astype(BF16)
        q_out[:, c0 + MLA_NOPE:c0 + MLA_QK_PAD] = qr.astype(BF16)
        v0 = nv + hd * MLA_V
        v_out[:, hd * MLA_VW:hd * MLA_VW + MLA_V] = kv[:, v0:v0 + MLA_V].astype(BF16)
        v_out[:, hd * MLA_VW + MLA_V:(hd + 1) * MLA_VW] = ones_col
        kn = _rms(kv[:, hd * MLA_NOPE:(hd + 1) * MLA_NOPE]) * knn_ref[...]
        k_out[:, c0:c0 + MLA_NOPE] = kn.astype(BF16)
        k_out[:, c0 + MLA_NOPE:c0 + MLA_QK_PAD] = kr


def _pad_lanes(g):
    return jnp.pad(g, (0, LANES - g.shape[0])).reshape(1, LANES)


def mla_in(lay, x, gain, mod, w_in, q_norm, kv_norm, w_uq, w_ukv, qn_nope, qn_rope, kn_nope, kn_rope,
           tables):
    D = D_MODEL
    Hn = MLA_HEADS
    win = jnp.pad(w_in, ((0, 0), (0, MLA_IN_PAD - w_in.shape[1]))).astype(BF16)
    wuq = jnp.pad(w_uq.reshape(MLA_Q_LORA, Hn, MLA_NOPE + MLA_ROPE),
                  ((0, 0), (0, 0), (0, MLA_QK_PAD - MLA_NOPE - MLA_ROPE)))
    wuq = wuq.reshape(MLA_Q_LORA, Hn * MLA_QK_PAD).astype(BF16)
    wkv = w_ukv.reshape(MLA_KV_LORA, Hn, MLA_NOPE + MLA_V)
    wukv = jnp.concatenate([wkv[:, :, :MLA_NOPE].reshape(MLA_KV_LORA, Hn * MLA_NOPE),
                            wkv[:, :, MLA_NOPE:].reshape(MLA_KV_LORA, Hn * MLA_V)], axis=1).astype(BF16)
    nb = lay.nb
    tab_spec = pl.BlockSpec((TL, LANES), lambda r: (r % nb, 0))
    bf = lambda w: jax.ShapeDtypeStruct((lay.rows, w), BF16)
    return pl.pallas_call(
        _mla_in_kernel,
        grid=(lay.n_blocks,),
        in_specs=[lay.row_spec(D), _const_spec((1, D)), lay.mod_spec(1), lay.mod_spec(0),
                  _const_spec(win.shape), _const_spec(wuq.shape), _const_spec(wukv.shape),
                  _const_spec((1, MLA_Q_LORA)), _const_spec((1, MLA_KV_LORA)),
                  _const_spec((1, LANES)), _const_spec((1, LANES)), _const_spec((1, LANES)),
                  _const_spec((1, LANES)), tab_spec, tab_spec, tab_spec],
        out_specs=[lay.row_spec(Hn * MLA_QK_PAD), lay.row_spec(Hn * MLA_QK_PAD), lay.row_spec(Hn * MLA_VW)],
        out_shape=[bf(Hn * MLA_QK_PAD), bf(Hn * MLA_QK_PAD), bf(Hn * MLA_VW)],
        compiler_params=_params(1),
        name="mla_in",
    )(x, gain, mod, mod, win, wuq, wukv, q_norm.reshape(1, -1), kv_norm.reshape(1, -1),
      (qn_nope * (MLA_SCALE * LOG2E)).reshape(1, -1), _pad_lanes(qn_rope * (MLA_SCALE * LOG2E)),
      kn_nope.reshape(1, -1), _pad_lanes(kn_rope), *tables)


def rope_tables(n_ctx, n_lat):
    n_freq = MLA_ROPE // 4
    inv = ROPE_THETA ** (-jnp.arange(n_freq, dtype=F32) / n_freq)
    t = jnp.arange(n_lat)
    a_r = (t // GRID_W).astype(F32)[:, None] * inv
    a_c = (t % GRID_W).astype(F32)[:, None] * inv
    ang = jnp.concatenate([a_r, a_r, a_c, a_c], axis=-1)
    ang = jnp.concatenate([jnp.zeros((n_ctx, MLA_ROPE), F32), ang], axis=0)
    cos, sin = jnp.cos(ang), jnp.sin(ang)
    low = (jnp.arange(MLA_ROPE) % (2 * ROPE_HALF)) < ROPE_HALF
    pad = lambda a: jnp.pad(a, ((0, 0), (0, LANES - MLA_ROPE)))
    return pad(cos), pad(jnp.where(low, -sin, 0.0)), pad(jnp.where(low, 0.0, sin))


MLA_HPS = 2


def _mla_attn_kernel(q_ref, k_ref, v_ref, o_ref, *, n_ctx):
    def attend(n_keys):
        heads = range(MLA_HPS)
        s, p = {}, {}
        for hd in heads:
            q = q_ref[0, :, hd * MLA_QK_PAD:(hd + 1) * MLA_QK_PAD]
            k = k_ref[0, :n_keys, hd * MLA_QK_PAD:(hd + 1) * MLA_QK_PAD]
            s[hd] = lax.dot_general(q, k, (((1,), (1,)), ((), ())), preferred_element_type=F32)
        for hd in heads:
            p[hd] = jnp.exp2(s[hd] - s[hd].max(axis=1, keepdims=True)).astype(BF16)
        for hd in heads:
            v = v_ref[0, :n_keys, hd * MLA_VW:(hd + 1) * MLA_VW]
            r = jnp.dot(p[hd], v, preferred_element_type=F32)
            o_ref[0, :, hd * MLA_V:(hd + 1) * MLA_V] = (
                r[:, :MLA_V] / r[:, MLA_V:MLA_V + 1]).astype(o_ref.dtype)

    @pl.when(pl.program_id(2) == 0)
    def _():
        attend(n_ctx)

    @pl.when(pl.program_id(2) > 0)
    def _():
        attend(k_ref.shape[1])


def mla_attention(lay, q, k, v):
    assert lay.ctx_first
    Bn, S = lay.n_batch, lay.nb * TL
    q3, k3, v3 = (a.reshape(Bn, S, a.shape[-1]) for a in (q, k, v))
    out = pl.pallas_call(
        functools.partial(_mla_attn_kernel, n_ctx=TL),
        grid=(Bn, MLA_HEADS // MLA_HPS, lay.nb),
        in_specs=[pl.BlockSpec((1, TL, MLA_HPS * MLA_QK_PAD), lambda b, h, i: (b, i, h)),
                  pl.BlockSpec((1, S, MLA_HPS * MLA_QK_PAD), lambda b, h, i: (b, 0, h)),
                  pl.BlockSpec((1, S, MLA_HPS * MLA_VW), lambda b, h, i: (b, 0, h))],
        out_specs=pl.BlockSpec((1, TL, MLA_HPS * MLA_V), lambda b, h, i: (b, i, h)),
        out_shape=jax.ShapeDtypeStruct((Bn, S, MLA_HEADS * MLA_V), BF16),
        compiler_params=_params(3),
        name="mla_attention",
    )(q3, k3, v3)
    return out.reshape(lay.rows, MLA_HEADS * MLA_V)


N_PROLOGUE = {"conv": 5, "mlstm": 4, "mla": 1}


def _mixer_out_kernel(*refs, kind, nb, ctx_first):
    n_pro = N_PROLOGUE[kind]
    pro = refs[:n_pro]
    (wout_ref, x_ref, ga_ref, gain_ref, sc_ref, sh_ref, wr_ref, br_ref,
     xo_ref, h2_ref, te_ref, gate_ref, rank_ref, cnt_ref, carry_ref) = refs[n_pro:]
    r = pl.program_id(0)

    if kind == "conv":
        vprev_ref, v_ref, vnext_ref, bg_ref, cw_ref = pro
        j = r % nb
        first = (j == 0) | (j == 1) if ctx_first else (j == 0)
        last = (j == nb - 1) | (j == 0) if ctx_first else (j == nb - 1)
        v = v_ref[...].astype(F32)
        rows = lax.broadcasted_iota(jnp.int32, (TL, 1), 0)
        prev_row = jnp.where(first, 0.0, vprev_ref[BF16_ROWS - 1:BF16_ROWS, :].astype(F32))
        next_row = jnp.where(last, 0.0, vnext_ref[0:1, :].astype(F32))
        up = jnp.where(rows == 0, prev_row, pltpu.roll(v, 1, 0))
        dn = jnp.where(rows == TL - 1, next_row, pltpu.roll(v, TL - 1, 0))
        cw = cw_ref[...]
        a = bg_ref[...].astype(F32) * (up * cw[0:1] + v * cw[1:2] + dn * cw[2:3])
    elif kind == "mlstm":
        hf_ref, hb_ref, og_ref, ng_ref = pro
        hh = hf_ref[...] + hb_ref[...]
        a = jnp.concatenate([_rms(hh[:, h * ML_DV:(h + 1) * ML_DV]) for h in range(ML_HEADS)], axis=1)
        a = a * ng_ref[...] * og_ref[...].astype(F32)
    else:
        a = pro[0][...]

    y = jnp.dot(a.astype(BF16), wout_ref[...], preferred_element_type=F32)
    xn = x_ref[...] + ga_ref[0] * y
    xo_ref[...] = xn
    h2 = _norm_mod(xn, gain_ref[...], sc_ref[0], sh_ref[0])
    h2_ref[...] = _pack_rows(h2)
    logits = _split_dot_t(wr_ref[...], h2) + br_ref[...]

    sub = lax.broadcasted_iota(jnp.int32, (N_EXPERTS, TL), 0)
    sub_k = lax.broadcasted_iota(jnp.int32, (TOP_K, TL), 0)
    work = logits
    sel = jnp.zeros((N_EXPERTS, TL), F32)
    top_e = jnp.zeros((TOP_K, TL), jnp.int32)
    top_v = jnp.zeros((TOP_K, TL), F32)
    picks = []
    for kk in range(TOP_K):
        m = work.max(axis=0, keepdims=True)
        idx = jnp.min(jnp.where(work == m, sub, N_EXPERTS), axis=0, keepdims=True)
        hit = sub == idx
        picks.append(hit)
        sel = jnp.where(hit, 1.0, sel)
        work = jnp.where(hit, -jnp.inf, work)
        top_e = jnp.where(sub_k == kk, idx, top_e)
        top_v = jnp.where(sub_k == kk, m, top_v)
    ex = jnp.exp(top_v - top_v[0:1])
    gate_ref[...] = ex / ex.sum(axis=0, keepdims=True)
    te_ref[...] = top_e

    @pl.when(r == 0)
    def _():
        carry_ref[...] = jnp.zeros_like(carry_ref)

    tr = lax.broadcasted_iota(jnp.int32, (TL, TL), 0)
    tc = lax.broadcasted_iota(jnp.int32, (TL, TL), 1)
    before = jnp.dot(sel.astype(BF16), (tr < tc).astype(BF16), preferred_element_type=F32)
    pos = before + carry_ref[...]
    rank = jnp.zeros((TOP_K, TL), F32)
    for kk in range(TOP_K):
        rk = jnp.sum(jnp.where(picks[kk], pos, 0.0), axis=0, keepdims=True)
        rank = jnp.where(sub_k == kk, rk, rank)
    rank_ref[...] = rank.astype(jnp.int32)
    total = carry_ref[...] + jnp.sum(sel, axis=1, keepdims=True)
    carry_ref[...] = total
    cnt_ref[...] = total


def mixer_out(lay, kind, pro_args, w_out, x, mod, gain_f, w_r, b_r):
    D = D_MODEL
    nb = lay.nb
    if kind == "conv":
        v, bg, cw = pro_args
        per = TL // BF16_ROWS
        last_tile = lay.rows // BF16_ROWS - 1
        pro_specs = [pl.BlockSpec((BF16_ROWS, D), lambda r: (jnp.maximum(r * per - 1, 0), 0)),
                     lay.row_spec(D),
                     pl.BlockSpec((BF16_ROWS, D), lambda r: (jnp.minimum((r + 1) * per, last_tile), 0)),
                     lay.row_spec(D), _const_spec((CONV_WIDTH, D))]
        pro_in = [v, v, v, bg, cw]
    elif kind == "mlstm":
        h_f, h_b, og, ng = pro_args
        pro_specs = [lay.row_spec(ML_V), lay.row_spec(ML_V), lay.row_spec(ML_V), _const_spec((1, ML_V))]
        pro_in = [h_f, h_b, og, ng.reshape(1, ML_V)]
    else:
        pro_specs = [lay.row_spec(D)]
        pro_in = list(pro_args)
    k_in = w_out.shape[0]
    small = lambda dt: jax.ShapeDtypeStruct((TOP_K, lay.rows), dt)
    small_spec = pl.BlockSpec((TOP_K, TL), lambda r: (0, r))
    return pl.pallas_call(
        functools.partial(_mixer_out_kernel, kind=kind, nb=nb, ctx_first=lay.ctx_first),
        grid=(lay.n_blocks,),
        in_specs=pro_specs + [_const_spec((k_in, D)), lay.row_spec(D), lay.mod_spec(2),
                              _const_spec((1, D)), lay.mod_spec(4), lay.mod_spec(3),
                              _const_spec((2 * N_EXPERTS, D)), _const_spec((N_EXPERTS, 1))],
        out_specs=[lay.row_spec(D), lay.row_spec(PACK_W), small_spec, small_spec, small_spec,
                   _const_spec((N_EXPERTS, 1))],
        out_shape=[jax.ShapeDtypeStruct((lay.rows, D), F32),
                   jax.ShapeDtypeStruct((lay.rows, PACK_W), jnp.int32),
                   small(jnp.int32), small(F32), small(jnp.int32),
                   jax.ShapeDtypeStruct((N_EXPERTS, 1), F32)],
        scratch_shapes=[pltpu.VMEM((N_EXPERTS, 1), F32)],
        compiler_params=_params(1),
        name="mixer_out_" + kind,
    )(*pro_in, w_out.astype(BF16), x, mod, gain_f, mod, mod, _split_weight_t(w_r),
      b_r.reshape(N_EXPERTS, 1))


def _expert_ffn_kernel(blk_e_ref, first_ref, used_ref, x_ref, w1_ref, b1_ref, w2_ref, b2_ref, o_ref,
                       w1b_ref, w2b_ref):
    del blk_e_ref
    i = pl.program_id(0)

    @pl.when(first_ref[i] == 1)
    def _():
        w1b_ref[...] = w1_ref[0, 0].astype(BF16)
        w2b_ref[...] = w2_ref[0, 0].astype(BF16)

    @pl.when(i < used_ref[0])
    def _():
        x = _unpack_rows(x_ref[...], BF16)
        h = jnp.dot(x, w1b_ref[...], preferred_element_type=F32) + b1_ref[0, 0]
        glu = jnp.minimum(h[:, :MOE_FF], SWIGLU_LIMIT)
        lin = jnp.clip(h[:, MOE_FF:], -SWIGLU_LIMIT, SWIGLU_LIMIT)
        act = glu * jax.nn.sigmoid(SWIGLU_ALPHA * glu) * (lin + 1.0)
        y = jnp.dot(act.astype(BF16), w2b_ref[...], preferred_element_type=F32)
        o_ref[...] = _pack_rows(y + b2_ref[0, 0])

    @pl.when(i >= used_ref[0])
    def _():
        o_ref[...] = jnp.zeros_like(o_ref)


def expert_ffn(layer, xp, blk_e, blk_first, n_used, w1, b1, w2, b2):
    n_rows = xp.shape[0]
    D, F2 = D_MODEL, 2 * MOE_FF
    n_blk = n_rows // MOE_BLOCK
    grid_spec = pltpu.PrefetchScalarGridSpec(
        num_scalar_prefetch=3,
        grid=(n_blk,),
        in_specs=[
            pl.BlockSpec((MOE_BLOCK, PACK_W), lambda i, be, fi, nu: (i, 0)),
            pl.BlockSpec((1, 1, D, F2), lambda i, be, fi, nu: (layer, be[i], 0, 0)),
            pl.BlockSpec((1, 1, 1, F2), lambda i, be, fi, nu: (layer, be[i], 0, 0)),
            pl.BlockSpec((1, 1, MOE_FF, D), lambda i, be, fi, nu: (layer, be[i], 0, 0)),
            pl.BlockSpec((1, 1, 1, D), lambda i, be, fi, nu: (layer, be[i], 0, 0)),
        ],
        out_specs=pl.BlockSpec((MOE_BLOCK, PACK_W), lambda i, be, fi, nu: (i, 0)),
        scratch_shapes=[pltpu.VMEM((D, F2), BF16), pltpu.VMEM((MOE_FF, D), BF16)],
    )
    return pl.pallas_call(
        _expert_ffn_kernel,
        grid_spec=grid_spec,
        out_shape=jax.ShapeDtypeStruct((n_rows, PACK_W), jnp.int32),
        compiler_params=_params(1),
        name="expert_ffn",
    )(blk_e, blk_first, n_used, xp, w1, b1, w2, b2)


SC_CORES = 2
SC_SUBCORES = 16
SC_CHUNKS = (64, 32)
BATCH_GROUPS = 2


def _sc_chunk(*counts):
    n_workers = SC_CORES * SC_SUBCORES
    for chunk in SC_CHUNKS:
        if all(n % (chunk * n_workers) == 0 for n in counts):
            return chunk
    raise ValueError(f"row counts {counts} do not split over {n_workers} subcores")


def sc_gather(table, idx):
    n_idx = idx.shape[0]
    width = table.shape[1]
    n_workers = SC_CORES * SC_SUBCORES
    per_worker = n_idx // n_workers
    chunk = _sc_chunk(n_idx)
    n_chunks = per_worker // chunk
    assert n_chunks * chunk * n_workers == n_idx and n_chunks % 2 == 0
    mesh = plsc.VectorSubcoreMesh(core_axis_name="c", subcore_axis_name="s",
                                  num_cores=SC_CORES, num_subcores=SC_SUBCORES)

    def body(table_hbm, idx_hbm, out_hbm, idx_v, rows_v, gsem, wsem):
        wid = lax.axis_index("s") * SC_CORES + lax.axis_index("c")
        pltpu.sync_copy(idx_hbm.at[wid], idx_v)

        def gather(ci, slot):
            return pltpu.make_async_copy(table_hbm.at[idx_v.at[ci]], rows_v.at[slot], gsem.at[slot])

        def write(ci, slot):
            return pltpu.make_async_copy(rows_v.at[slot], out_hbm.at[ci, wid], wsem.at[slot])

        gather(0, 0).start()

        @pl.loop(0, n_chunks, step=2)
        def _(c0):
            for slot in range(2):
                ci = c0 + slot
                other = 1 - slot

                @pl.when(ci + 1 < n_chunks)
                def _():
                    @pl.when(ci >= 1)
                    def _():
                        write(ci - 1, other).wait()
                    gather(ci + 1, other).start()

                gather(ci, slot).wait()
                write(ci, slot).start()

        write(n_chunks - 2, 0).wait()
        write(n_chunks - 1, 1).wait()

    out = pl.kernel(
        body,
        out_type=jax.ShapeDtypeStruct((n_chunks, n_workers, chunk, width), table.dtype),
        mesh=mesh,
        scratch_types=[pltpu.VMEM((n_chunks, chunk), jnp.int32),
                       pltpu.VMEM((2, chunk, width), table.dtype),
                       pltpu.SemaphoreType.DMA((2,)),
                       pltpu.SemaphoreType.DMA((2,))],
        name="sc_gather",
    )(table, idx.reshape(n_chunks, n_workers, chunk).transpose(1, 0, 2))
    return out.reshape(n_idx, width)


def sc_dispatch(table, dest, pad_rows):
    n_tok, width = table.shape
    n_picks = dest.shape[0]
    n_pad = pad_rows.shape[0]
    n_workers = SC_CORES * SC_SUBCORES
    chunk = _sc_chunk(n_tok, n_pad)
    per_w = n_tok // chunk // n_workers
    pad_w = n_pad // chunk // n_workers
    assert per_w * chunk * n_workers == n_tok and pad_w * chunk * n_workers == n_pad
    mesh = plsc.VectorSubcoreMesh(core_axis_name="c", subcore_axis_name="s",
                                  num_cores=SC_CORES, num_subcores=SC_SUBCORES)

    def body(table_hbm, idx_hbm, pad_hbm, zero_hbm, out_hbm, idx_v, pad_v, rows_v, zero_v,
             rsem, ssem, zsem):
        wid = lax.axis_index("s") * SC_CORES + lax.axis_index("c")
        pltpu.sync_copy(idx_hbm.at[wid], idx_v)
        pltpu.sync_copy(pad_hbm.at[wid], pad_v)
        pltpu.sync_copy(zero_hbm, zero_v)

        def zero_fill(pc):
            return pltpu.make_async_copy(zero_v, out_hbm.at[pad_v.at[pc]], zsem)

        for pc in range(pad_w):
            zero_fill(pc).start()

        def scatter(ci, kk):
            return pltpu.make_async_copy(rows_v, out_hbm.at[idx_v.at[ci * n_picks + kk]], ssem)

        @pl.loop(0, per_w)
        def _(ci):
            pltpu.async_copy(table_hbm.at[ci, wid], rows_v, rsem).wait()
            for kk in range(n_picks):
                scatter(ci, kk).start()
            for kk in range(n_picks):
                scatter(ci, kk).wait()

        for pc in range(pad_w):
            zero_fill(pc).wait()

    idx = dest.reshape(n_picks, per_w, n_workers, chunk).transpose(2, 1, 0, 3)
    idx = idx.reshape(n_workers, per_w * n_picks, chunk)
    return pl.kernel(
        body,
        out_type=jax.ShapeDtypeStruct((n_tok * n_picks + n_pad, width), table.dtype),
        mesh=mesh,
        scratch_types=[pltpu.VMEM((per_w * n_picks, chunk), jnp.int32),
                       pltpu.VMEM((pad_w, chunk), jnp.int32),
                       pltpu.VMEM((chunk, width), table.dtype),
                       pltpu.VMEM((chunk, width), table.dtype),
                       pltpu.SemaphoreType.DMA, pltpu.SemaphoreType.DMA, pltpu.SemaphoreType.DMA],
        name="sc_dispatch",
    )(table.reshape(per_w, n_workers, chunk, width), idx,
      pad_rows.reshape(n_workers, pad_w, chunk), jnp.zeros((chunk, width), table.dtype))


def _combine_kernel(x_ref, *refs):
    y_refs, (gate_ref, gf_ref, o_ref) = refs[:TOP_K], refs[TOP_K:]
    gates = gate_ref[...]
    acc = gates[:, 0:1] * _unpack_rows(y_refs[0][...], F32)
    for kk in range(1, TOP_K):
        acc = acc + gates[:, kk:kk + 1] * _unpack_rows(y_refs[kk][...], F32)
    o_ref[...] = x_ref[...] + gf_ref[0] * acc


def moe_combine(lay, x, yg, gates, mod, drop_ctx):
    D = D_MODEL
    if drop_ctx:
        nbo = lay.nb - 1
        src = lambda r: (r // nbo) * lay.nb + 1 + r % nbo
        n_out = lay.n_batch * nbo
    else:
        src = lambda r: r
        n_out = lay.n_blocks
    y_specs = [pl.BlockSpec((TL, PACK_W), functools.partial(lambda kk, r: (kk * lay.n_blocks + src(r), 0), kk))
               for kk in range(TOP_K)]
    return pl.pallas_call(
        _combine_kernel,
        grid=(n_out,),
        in_specs=[pl.BlockSpec((TL, D), lambda r: (src(r), 0))] + y_specs + [
            pl.BlockSpec((TL, TOP_K), lambda r: (src(r), 0)),
            pl.BlockSpec((1, 1, D), lambda r: (lay.mod_row(src(r)), 0, 5))],
        out_specs=pl.BlockSpec((TL, D), lambda r: (r, 0)),
        out_shape=jax.ShapeDtypeStruct((n_out * TL, D), F32),
        compiler_params=_params(1),
        name="moe_combine",
    )(x, yg, yg, yg, yg, gates, mod)


def moe_route(top_e, rank, counts):
    T = top_e.shape[1]
    assert (T * TOP_K) % MOE_BLOCK == 0
    counts = counts.reshape(N_EXPERTS).astype(jnp.int32)
    padded = (counts + MOE_BLOCK - 1) // MOE_BLOCK * MOE_BLOCK
    padded_end = jnp.cumsum(padded)
    padded_start = padded_end - padded
    experts = jnp.arange(N_EXPERTS)
    start_of = jnp.sum(jnp.where(top_e[..., None] == experts, padded_start, 0), axis=-1)
    dest = (start_of + rank).astype(jnp.int32)
    n_pad = N_EXPERTS * MOE_BLOCK
    n_rows = T * TOP_K + n_pad
    n_blk = n_rows // MOE_BLOCK
    blk_start = jnp.arange(n_blk) * MOE_BLOCK
    blk_e = jnp.minimum(jnp.sum(padded_end[None, :] <= blk_start[:, None], axis=1), N_EXPERTS - 1)
    blk_e = blk_e.astype(jnp.int32)
    blk_first = jnp.concatenate([jnp.ones((1,), jnp.int32), (blk_e[1:] != blk_e[:-1]).astype(jnp.int32)])
    n_used = (padded_end[-1:] // MOE_BLOCK).astype(jnp.int32)
    tail = padded - counts
    tail_end = jnp.cumsum(tail)
    j = jnp.arange(n_pad)
    owner = jnp.sum(tail_end[None, :] <= j[:, None], axis=1)
    base = padded_start + counts - (tail_end - tail)
    in_group = jnp.sum(jnp.where(owner[:, None] == experts, base, 0), axis=-1) + j
    pad_rows = jnp.where(j < tail_end[-1], in_group, padded_end[-1] + j - tail_end[-1])
    return dest, pad_rows.astype(jnp.int32), blk_e, blk_first, n_used


def kernel(x, c, ctx, c_ctx, norm_mix, norm_ffn, w_mod, b_mod, conv_w_in, conv_w, conv_w_out, ml_w_in, ml_b_gate, ml_norm, ml_w_out, mla_w_in, mla_q_norm, mla_kv_norm, mla_w_uq, mla_w_ukv, mla_qn_nope, mla_qn_rope, mla_kn_nope, mla_kn_rope, mla_w_out, moe_w_router, moe_b_router, moe_w1, moe_b1, moe_w2, moe_b2):
    Bn, n_lat, D = x.shape
    n_ctx = ctx.shape[1]
    assert D == D_MODEL and n_ctx == TL and n_lat % TL == 0
    assert (DEPTH - 1) % N_MIXERS == 0
    n_groups = BATCH_GROUPS if Bn % BATCH_GROUPS == 0 else 1
    gb = Bn // n_groups
    full = Layout(gb, (n_ctx + n_lat) // TL, True)
    lat_only = Layout(gb, n_lat // TL, False)
    mods = ada_all(c, c_ctx, w_mod, b_mod)
    tables = rope_tables(n_ctx, n_lat)
    b1_all = moe_b1.reshape(DEPTH, N_EXPERTS, 1, 2 * MOE_FF)
    b2_all = moe_b2.reshape(DEPTH, N_EXPERTS, 1, D)
    groups = [slice(gi * gb, (gi + 1) * gb) for gi in range(n_groups)]
    Xs = [jnp.concatenate([ctx[s], x[s]], axis=1).reshape(full.rows, D) for s in groups]

    def mixer(layer, lay, X, mod):
        kind, j = layer % N_MIXERS, layer // N_MIXERS
        gain_a = norm_mix[layer].reshape(1, D)
        gain_f = norm_ffn[layer].reshape(1, D)
        if kind == 0:
            bg, v = conv_in(lay, X, gain_a, mod, conv_w_in[j].astype(BF16))
            pro, w_out, name = (v, bg, conv_w[j]), conv_w_out[j], "conv"
        elif kind == 1:
            w = ml_w_in[j]
            n_main = 2 * ML_QK + 2 * ML_V
            w_main = jnp.concatenate([w[:, :ML_QK] * ML_DQK ** -0.5, w[:, ML_QK:n_main]], axis=1)
            q, k, kt, v, og, g, gt = mlstm_in(lay, X, gain_a, mod, w_main.astype(BF16), w[:, n_main:],
                                              ml_b_gate[j])
            h_f, h_b = mlstm_scan(lay, q, k, kt, v, g, gt)
            pro, w_out, name = (h_f, h_b, og, ml_norm[j]), ml_w_out[j], "mlstm"
        else:
            q, k, v = mla_in(lay, X, gain_a, mod, mla_w_in[j], mla_q_norm[j], mla_kv_norm[j],
                             mla_w_uq[j], mla_w_ukv[j], mla_qn_nope[j], mla_qn_rope[j],
                             mla_kn_nope[j], mla_kn_rope[j], tables)
            pro, w_out, name = (mla_attention(lay, q, k, v),), mla_w_out[j], "mla"
        return mixer_out(lay, name, pro, w_out, X, mod, gain_f, moe_w_router[layer], moe_b_router[layer])

    for layer in range(DEPTH):
        lay = lat_only if layer == DEPTH - 1 else full
        mod_g = [jnp.concatenate([mods[layer, s], mods[layer, Bn:Bn + 1]], axis=0) for s in groups]
        st = []
        for gi in range(n_groups):
            X, h2, top_e, gates, rank, counts = mixer(layer, lay, Xs[gi], mod_g[gi])
            dest, pad_rows, blk_e, blk_first, n_used = moe_route(top_e, rank, counts)
            xp = sc_dispatch(h2, dest, pad_rows)
            st.append((X, gates, dest, xp, blk_e, blk_first, n_used))
        ys = []
        for gi in range(n_groups):
            X, gates, dest, xp, blk_e, blk_first, n_used = st[gi]
            yp = expert_ffn(layer, xp, blk_e, blk_first, n_used, moe_w1, b1_all, moe_w2, b2_all)
            ys.append(sc_gather(yp, dest.reshape(-1)))
        for gi in range(n_groups):
            X, gates = st[gi][0], st[gi][1]
            Xs[gi] = moe_combine(lay, X, ys[gi], gates.T, mod_g[gi], drop_ctx=(layer == DEPTH - 2))
    return jnp.concatenate([X.reshape(gb, n_lat, D) for X in Xs], axis=0)
```

```python
import functools

import jax
import jax.numpy as jnp
from jax import lax
from jax.experimental import pallas as pl
from jax.experimental.pallas import tpu as pltpu
from jax.experimental.pallas import tpu_sc as plsc

D_MODEL = 1024
DEPTH = 4
GRID_W = 64
N_MIXERS = 3
N_ADA = 6
RMS_EPS = 1e-6
CONV_WIDTH = 3
ML_HEADS = 8
ML_DQK = 64
ML_DV = 128
ML_QK = ML_HEADS * ML_DQK
ML_V = ML_HEADS * ML_DV
GATE_CAP = 15.0
MLA_HEADS = 8
MLA_NOPE = 128
MLA_ROPE = 64
MLA_V = 128
MLA_Q_LORA = 384
MLA_KV_LORA = 256
MLA_SCALE = (MLA_NOPE + MLA_ROPE) ** -0.5
ROPE_THETA = 10000.0
N_EXPERTS = 32
TOP_K = 4
MOE_FF = D_MODEL
SWIGLU_ALPHA = 1.702
SWIGLU_LIMIT = 7.0
MOE_BLOCK = 512

TL = 256
LANES = 128
BF16_ROWS = 16
VMEM_LIMIT = 48 * 1024 * 1024
HI = lax.Precision.HIGHEST
F32 = jnp.float32
BF16 = jnp.bfloat16


def _params(n_axes):
    return pltpu.CompilerParams(dimension_semantics=("arbitrary",) * n_axes,
                                vmem_limit_bytes=VMEM_LIMIT)


def _rms(x, width=None):
    width = x.shape[-1] if width is None else width
    return x * lax.rsqrt(jnp.sum(x * x, axis=-1, keepdims=True) * (1.0 / width) + RMS_EPS)


def _norm_mod(x, gain, scale, shift):
    return _rms(x) * (gain * (1.0 + scale)) + shift


def _split_weight_t(w):
    hi = w.astype(BF16)
    lo = (w - hi.astype(F32)).astype(BF16)
    return jnp.concatenate([hi.T, lo.T], axis=0)


def _split_dot_t(w2, h):
    n = w2.shape[0] // 2
    dn = (((1,), (1,)), ((), ()))
    h_hi = h.astype(BF16)
    h_lo = (h - h_hi.astype(F32)).astype(BF16)
    both = lax.dot_general(w2, h_hi, dn, preferred_element_type=F32)
    cross = lax.dot_general(w2[:n], h_lo, dn, preferred_element_type=F32)
    return both[:n] + both[n:] + cross


PACK_W = D_MODEL // 2
HIGH_HALF = -65536


def _pack_rows(x):
    xb = x.astype(BF16).astype(F32)
    lo = lax.bitcast_convert_type(xb[:, :PACK_W], jnp.int32)
    hi = lax.bitcast_convert_type(xb[:, PACK_W:], jnp.int32)
    return hi | lax.shift_right_logical(lo, 16)


def _unpack_rows(w, dtype):
    lo = lax.bitcast_convert_type(lax.shift_left(w, 16), F32)
    hi = lax.bitcast_convert_type(w & HIGH_HALF, F32)
    return jnp.concatenate([lo.astype(dtype), hi.astype(dtype)], axis=1)


class Layout:
    def __init__(self, n_batch, nb, ctx_first):
        self.n_batch, self.nb, self.ctx_first = n_batch, nb, ctx_first
        self.n_blocks = n_batch * nb
        self.rows = self.n_blocks * TL

    def mod_row(self, r):
        b = r // self.nb
        return jnp.where(r % self.nb == 0, self.n_batch, b) if self.ctx_first else b

    def row_spec(self, width):
        return pl.BlockSpec((TL, width), lambda r: (r, 0))

    def mod_spec(self, piece):
        return pl.BlockSpec((1, 1, D_MODEL), lambda r: (self.mod_row(r), 0, piece))


def _const_spec(shape):
    return pl.BlockSpec(shape, lambda *_: (0,) * len(shape))


ADA_ROWS = 16
ADA_TN = 1536


def _ada_kernel(c_ref, w_ref, b_ref, o_ref):
    c = c_ref[...]
    s = c * jax.nn.sigmoid(c)
    o_ref[0] = jnp.dot(s, w_ref[0], precision=HI, preferred_element_type=F32) + b_ref[0]


def ada_all(c, c_ctx, w_mod, b_mod):
    Bn, D = c.shape
    cond = jnp.zeros((ADA_ROWS, D), F32).at[:Bn].set(c).at[Bn].set(c_ctx)
    out = pl.pallas_call(
        _ada_kernel,
        grid=(DEPTH, N_ADA * D // ADA_TN),
        in_specs=[pl.BlockSpec((ADA_ROWS, D), lambda l, n: (0, 0)),
                  pl.BlockSpec((1, D, ADA_TN), lambda l, n: (l, 0, n)),
                  pl.BlockSpec((1, 1, ADA_TN), lambda l, n: (l, 0, n))],
        out_specs=pl.BlockSpec((1, ADA_ROWS, ADA_TN), lambda l, n: (l, 0, n)),
        out_shape=jax.ShapeDtypeStruct((DEPTH, ADA_ROWS, N_ADA * D), F32),
        compiler_params=_params(2),
        name="ada_mod",
    )(cond, w_mod, b_mod.reshape(DEPTH, 1, N_ADA * D))
    return out[:, :Bn + 1, None, :]


def _conv_in_kernel(x_ref, gain_ref, sc_ref, sh_ref, w_ref, bg_ref, v_ref):
    D = D_MODEL
    h = _norm_mod(x_ref[...], gain_ref[...], sc_ref[0], sh_ref[0]).astype(BF16)
    p = jnp.dot(h, w_ref[...], preferred_element_type=F32)
    bg_ref[...] = p[:, :D].astype(BF16)
    v_ref[...] = (p[:, D:2 * D] * p[:, 2 * D:]).astype(BF16)


def conv_in(lay, x, gain, mod, w_in):
    D = D_MODEL
    sds = jax.ShapeDtypeStruct((lay.rows, D), BF16)
    return pl.pallas_call(
        _conv_in_kernel,
        grid=(lay.n_blocks,),
        in_specs=[lay.row_spec(D), _const_spec((1, D)), lay.mod_spec(1), lay.mod_spec(0),
                  _const_spec((D, 3 * D))],
        out_specs=[lay.row_spec(D), lay.row_spec(D)],
        out_shape=[sds, sds],
        compiler_params=_params(1),
        name="conv_in",
    )(x, gain, mod, mod, w_in)


ML_T = 256
ML_SW = 2 * ML_DV
ML_NG = 4 * ML_HEADS


LOG2E = 1.4426950408889634


def _gate_act(g, is_forget):
    g = GATE_CAP * jnp.tanh(g * (1.0 / GATE_CAP))
    log_sig = jnp.minimum(g, 0.0) - jnp.log(1.0 + jnp.exp(-jnp.abs(g)))
    return jnp.where(is_forget, log_sig, g) * LOG2E


def _mlstm_in_kernel(x_ref, gain_ref, sc_ref, sh_ref, w_ref, wkt_ref, wg_ref, bgt_ref,
                     q_ref, k_ref, kt_ref, v_ref, og_ref, g_ref, gt_ref):
    h = _norm_mod(x_ref[...], gain_ref[...], sc_ref[0], sh_ref[0])
    hb = h.astype(BF16)
    p = jnp.dot(hb, w_ref[...], preferred_element_type=F32)
    q_ref[...] = p[:, :ML_QK].astype(BF16)
    k_ref[...] = p[:, ML_QK:2 * ML_QK].astype(BF16)
    kt_ref[...] = lax.dot_general(wkt_ref[...], hb, (((1,), (1,)), ((), ())),
                                  preferred_element_type=F32).astype(BF16)
    v_ref[...] = p[:, 2 * ML_QK:2 * ML_QK + ML_V].astype(BF16)
    og_ref[...] = jax.nn.sigmoid(p[:, 2 * ML_QK + ML_V:]).astype(BF16)
    gt = _split_dot_t(wg_ref[...], h) + bgt_ref[...]
    row = lax.broadcasted_iota(jnp.int32, gt.shape, 0)
    gt = _gate_act(gt, (row // ML_HEADS) % 2 == 1)
    gt_ref[...] = gt
    eye = (lax.broadcasted_iota(jnp.int32, (TL, TL), 0)
           == lax.broadcasted_iota(jnp.int32, (TL, TL), 1)).astype(F32)
    g_ref[...] = lax.dot_general(eye, gt, (((1,), (1,)), ((), ())), precision=HI,
                                 preferred_element_type=F32)


def mlstm_in(lay, x, gain, mod, w_main, w_g, b_g):
    D = D_MODEL
    n_main = 2 * ML_QK + 2 * ML_V
    bf = lambda w: jax.ShapeDtypeStruct((lay.rows, w), BF16)
    return pl.pallas_call(
        _mlstm_in_kernel,
        grid=(lay.n_blocks,),
        in_specs=[lay.row_spec(D), _const_spec((1, D)), lay.mod_spec(1), lay.mod_spec(0),
                  _const_spec((D, n_main)), _const_spec((ML_QK, D)), _const_spec((2 * ML_NG, D)),
                  _const_spec((ML_NG, 1))],
        out_specs=[lay.row_spec(ML_QK), lay.row_spec(ML_QK), pl.BlockSpec((ML_QK, TL), lambda r: (0, r)),
                   lay.row_spec(ML_V), lay.row_spec(ML_V),
                   lay.row_spec(ML_NG), pl.BlockSpec((ML_NG, TL), lambda r: (0, r))],
        out_shape=[bf(ML_QK), bf(ML_QK), jax.ShapeDtypeStruct((ML_QK, lay.rows), BF16),
                   bf(ML_V), bf(ML_V),
                   jax.ShapeDtypeStruct((lay.rows, ML_NG), F32),
                   jax.ShapeDtypeStruct((ML_NG, lay.rows), F32)],
        compiler_params=_params(1),
        name="mlstm_in",
    )(x, gain, mod, mod, w_main, w_main[:, ML_QK:2 * ML_QK].T, _split_weight_t(w_g),
      b_g.reshape(ML_NG, 1))


def _split3(x):
    hi = x.astype(BF16)
    r1 = x - hi.astype(F32)
    mid = r1.astype(BF16)
    lo = (r1 - mid.astype(F32)).astype(BF16)
    return hi, mid, lo


def _dot_exact01(x, sel01, x_on_left):
    sel = sel01.astype(BF16)
    parts = [jnp.dot(p, sel, preferred_element_type=F32) if x_on_left
             else jnp.dot(sel, p, preferred_element_type=F32) for p in _split3(x)]
    return parts[0] + parts[1] + parts[2]


def _mlstm_dir(reverse, q_ref, k_ref, kt_ref, v_ref, g_ref, gt_ref, o_ref, s_ref, m_ref):
    T = ML_T
    row = lax.broadcasted_iota(jnp.int32, (T, T), 0)
    col = lax.broadcasted_iota(jnp.int32, (T, T), 1)
    mask = (col >= row) if reverse else (col <= row)
    gt = gt_ref[...]
    bc = _dot_exact01(g_ref[...], mask, x_on_left=False)
    br = _dot_exact01(gt, (row >= col) if reverse else (row <= col), x_on_left=True)
    gi, gf = (2 * ML_HEADS, 3 * ML_HEADS) if reverse else (0, ML_HEADS)
    lane = lax.broadcasted_iota(jnp.int32, (T, 2 * ML_DQK), 1)
    sub = lax.broadcasted_iota(jnp.int32, (2 * ML_DQK, T), 0)
    gate_row = lax.broadcasted_iota(jnp.int32, (ML_NG, ML_DV), 0)
    ones = jnp.ones((T, ML_DV), BF16)
    heads = range(ML_HEADS)
    qm, vx, s_raw = {}, {}, {}
    for h in heads:
        pair = (h // 2) * 2 * ML_DQK
        own = (lane >= ML_DQK) if (h % 2) else (lane < ML_DQK)
        qp = q_ref[:, pair:pair + 2 * ML_DQK]
        qm[h] = jnp.where(own, qp, jnp.zeros_like(qp))
        vx[h] = jnp.concatenate([v_ref[:, h * ML_DV:(h + 1) * ML_DV], ones], axis=1)
        s_raw[h] = lax.dot_general(qm[h], k_ref[:, pair:pair + 2 * ML_DQK], (((1,), (1,)), ((), ())),
                                   preferred_element_type=F32)
    p, a, u_rep, m_prev, s_prev = {}, {}, {}, {}, {}
    for h in heads:
        c_row = gt[gi + h:gi + h + 1, :] - br[gf + h:gf + h + 1, :]
        e = jnp.where(mask, c_row, -jnp.inf)
        m_prev[h] = m_ref[h][0:1, 0:1]
        u = jnp.maximum(m_prev[h], jnp.max(e, axis=1, keepdims=True))
        p[h] = (s_raw[h] * jnp.exp2(e - u)).astype(BF16)
        u_rep[h] = jnp.broadcast_to(u, (T, ML_DV))
        a[h] = jnp.exp2(m_prev[h] - u_rep[h])
        s_prev[h] = s_ref[h]
    for h in heads:
        r = jnp.dot(p[h], vx[h], preferred_element_type=F32)
        qs = jnp.dot(qm[h], s_prev[h].astype(BF16), preferred_element_type=F32)
        num = r[:, :ML_DV] + a[h] * qs[:, :ML_DV]
        den = r[:, ML_DV:] + a[h] * qs[:, ML_DV:]
        b_rep = _dot_exact01(bc, gate_row == gf + h, x_on_left=True)
        floor = jnp.exp2(-(b_rep + u_rep[h]))
        o_ref[:, h * ML_DV:(h + 1) * ML_DV] = (num / jnp.maximum(jnp.abs(den), floor)).astype(o_ref.dtype)
    for h in heads:
        pair = (h // 2) * 2 * ML_DQK
        b_row = br[gf + h:gf + h + 1, :]
        tot = b_row[:, 0:1] if reverse else b_row[:, T - 1:T]
        g_row = tot - b_row + gt[gi + h:gi + h + 1, :]
        m_new = jnp.maximum(tot + m_prev[h], jnp.max(g_row, axis=1, keepdims=True))
        decay = jnp.exp2(tot + m_prev[h] - m_new)
        wk = jnp.exp2(g_row - m_new)
        own_t = (sub >= ML_DQK) if (h % 2) else (sub < ML_DQK)
        kt = kt_ref[pair:pair + 2 * ML_DQK, :].astype(F32)
        kw = jnp.where(own_t, kt * wk, 0.0).astype(BF16)
        s_ref[h] = decay * s_prev[h] + jnp.dot(kw, vx[h], preferred_element_type=F32)
        m_ref[h] = jnp.broadcast_to(m_new, m_ref.shape[1:])


def _mlstm_scan_kernel(qf_ref, kf_ref, ktf_ref, vf_ref, gf_ref, gtf_ref,
                       qb_ref, kb_ref, ktb_ref, vb_ref, gb_ref, gtb_ref,
                       of_ref, ob_ref, sf_ref, mf_ref, sb_ref, mb_ref):
    @pl.when(pl.program_id(1) == 0)
    def _():
        sf_ref[...] = jnp.zeros_like(sf_ref)
        mf_ref[...] = jnp.zeros_like(mf_ref)
        sb_ref[...] = jnp.zeros_like(sb_ref)
        mb_ref[...] = jnp.zeros_like(mb_ref)

    _mlstm_dir(False, qf_ref, kf_ref, ktf_ref, vf_ref, gf_ref, gtf_ref, of_ref, sf_ref, mf_ref)
    _mlstm_dir(True, qb_ref, kb_ref, ktb_ref, vb_ref, gb_ref, gtb_ref, ob_ref, sb_ref, mb_ref)


def mlstm_scan(lay, q, k, kt, v, g, gt):
    assert lay.ctx_first
    per = TL // ML_T
    nb = lay.nb * per
    rev = lambda j: jnp.where(j < per, per - 1 - j, nb + per - 1 - j)
    fwd = lambda b, j: (b * nb + j, 0)
    bwd = lambda b, j: (b * nb + rev(j), 0)
    fwd_t = lambda b, j: (0, b * nb + j)
    bwd_t = lambda b, j: (0, b * nb + rev(j))

    def specs(im, imt):
        return [pl.BlockSpec((ML_T, ML_QK), im), pl.BlockSpec((ML_T, ML_QK), im),
                pl.BlockSpec((ML_QK, ML_T), imt), pl.BlockSpec((ML_T, ML_V), im),
                pl.BlockSpec((ML_T, ML_NG), im), pl.BlockSpec((ML_NG, ML_T), imt)]

    out_sds = jax.ShapeDtypeStruct((lay.rows, ML_V), BF16)
    state = [pltpu.VMEM((ML_HEADS, 2 * ML_DQK, ML_SW), F32),
             pltpu.VMEM((ML_HEADS, 8, LANES), F32)]
    return pl.pallas_call(
        _mlstm_scan_kernel,
        grid=(lay.n_batch, nb),
        in_specs=specs(fwd, fwd_t) + specs(bwd, bwd_t),
        out_specs=[pl.BlockSpec((ML_T, ML_V), fwd), pl.BlockSpec((ML_T, ML_V), bwd)],
        out_shape=[out_sds, out_sds],
        scratch_shapes=state + state,
        compiler_params=_params(2),
        name="mlstm_scan",
    )(q, k, kt, v, g, gt, q, k, kt, v, g, gt)


MLA_QK_PAD = 256
MLA_VW = 2 * MLA_V
MLA_IN_PAD = MLA_Q_LORA + MLA_KV_LORA + LANES
ROPE_HALF = MLA_ROPE // 4


def _mla_in_kernel(x_ref, gain_ref, sc_ref, sh_ref, win_ref, wuq_ref, wukv_ref, qn_ref, kvn_ref,
                   qnn_ref, qnr_ref, knn_ref, knr_ref, cos_ref, sa_ref, sb_ref,
                   q_out, k_out, v_out):
    h = _norm_mod(x_ref[...], gain_ref[...], sc_ref[0], sh_ref[0]).astype(BF16)
    p = jnp.dot(h, win_ref[...], preferred_element_type=F32)
    cq = _rms(p[:, :MLA_Q_LORA]) * qn_ref[...]
    ckv = _rms(p[:, MLA_Q_LORA:MLA_Q_LORA + MLA_KV_LORA]) * kvn_ref[...]
    kr = p[:, MLA_Q_LORA + MLA_KV_LORA:]
    q = jnp.dot(cq.astype(BF16), wuq_ref[...], preferred_element_type=F32)
    kv = jnp.dot(ckv.astype(BF16), wukv_ref[...], preferred_element_type=F32)
    nv = MLA_HEADS * MLA_NOPE
    ones_col = (lax.broadcasted_iota(jnp.int32, (TL, MLA_VW - MLA_V), 1) == 0).astype(BF16)
    cos, sa, sb = cos_ref[...], sa_ref[...], sb_ref[...]

    def rope(xp):
        return (xp * cos + pltpu.roll(xp, LANES - ROPE_HALF, 1) * sa + pltpu.roll(xp, ROPE_HALF, 1) * sb)

    kr = rope(_rms(kr, MLA_ROPE) * knr_ref[...]).astype(BF16)
    for hd in range(MLA_HEADS):
        c0 = hd * MLA_QK_PAD
        qn = _rms(q[:, c0:c0 + MLA_NOPE]) * qnn_ref[...]
        qr = rope(_rms(q[:, c0 + MLA_NOPE:c0 + MLA_QK_PAD], MLA_ROPE) * qnr_ref[...])
        q_out[:, c0:c0 + MLA_NOPE] = qn.astype(BF16)
        q_out[:, c0 + MLA_NOPE:c0 + MLA_QK_PAD] = qr.astype(BF16)
        v0 = nv + hd * MLA_V
        v_out[:, hd * MLA_VW:hd * MLA_VW + MLA_V] = kv[:, v0:v0 + MLA_V].astype(BF16)
        v_out[:, hd * MLA_VW + MLA_V:(hd + 1) * MLA_VW] = ones_col
        kn = _rms(kv[:, hd * MLA_NOPE:(hd + 1) * MLA_NOPE]) * knn_ref[...]
        k_out[:, c0:c0 + MLA_NOPE] = kn.astype(BF16)
        k_out[:, c0 + MLA_NOPE:c0 + MLA_QK_PAD] = kr


def _pad_lanes(g):
    return jnp.pad(g, (0, LANES - g.shape[0])).reshape(1, LANES)


def mla_in(lay, x, gain, mod, w_in, q_norm, kv_norm, w_uq, w_ukv, qn_nope, qn_rope, kn_nope, kn_rope,
           tables):
    D = D_MODEL
    Hn = MLA_HEADS
    win = jnp.pad(w_in, ((0, 0), (0, MLA_IN_PAD - w_in.shape[1]))).astype(BF16)
    wuq = jnp.pad(w_uq.reshape(MLA_Q_LORA, Hn, MLA_NOPE + MLA_ROPE),
                  ((0, 0), (0, 0), (0, MLA_QK_PAD - MLA_NOPE - MLA_ROPE)))
    wuq = wuq.reshape(MLA_Q_LORA, Hn * MLA_QK_PAD).astype(BF16)
    wkv = w_ukv.reshape(MLA_KV_LORA, Hn, MLA_NOPE + MLA_V)
    wukv = jnp.concatenate([wkv[:, :, :MLA_NOPE].reshape(MLA_KV_LORA, Hn * MLA_NOPE),
                            wkv[:, :, MLA_NOPE:].reshape(MLA_KV_LORA, Hn * MLA_V)], axis=1).astype(BF16)
    nb = lay.nb
    tab_spec = pl.BlockSpec((TL, LANES), lambda r: (r % nb, 0))
    bf = lambda w: jax.ShapeDtypeStruct((lay.rows, w), BF16)
    return pl.pallas_call(
        _mla_in_kernel,
        grid=(lay.n_blocks,),
        in_specs=[lay.row_spec(D), _const_spec((1, D)), lay.mod_spec(1), lay.mod_spec(0),
                  _const_spec(win.shape), _const_spec(wuq.shape), _const_spec(wukv.shape),
                  _const_spec((1, MLA_Q_LORA)), _const_spec((1, MLA_KV_LORA)),
                  _const_spec((1, LANES)), _const_spec((1, LANES)), _const_spec((1, LANES)),
                  _const_spec((1, LANES)), tab_spec, tab_spec, tab_spec],
        out_specs=[lay.row_spec(Hn * MLA_QK_PAD), lay.row_spec(Hn * MLA_QK_PAD), lay.row_spec(Hn * MLA_VW)],
        out_shape=[bf(Hn * MLA_QK_PAD), bf(Hn * MLA_QK_PAD), bf(Hn * MLA_VW)],
        compiler_params=_params(1),
        name="mla_in",
    )(x, gain, mod, mod, win, wuq, wukv, q_norm.reshape(1, -1), kv_norm.reshape(1, -1),
      (qn_nope * (MLA_SCALE * LOG2E)).reshape(1, -1), _pad_lanes(qn_rope * (MLA_SCALE * LOG2E)),
      kn_nope.reshape(1, -1), _pad_lanes(kn_rope), *tables)


def rope_tables(n_ctx, n_lat):
    n_freq = MLA_ROPE // 4
    inv = ROPE_THETA ** (-jnp.arange(n_freq, dtype=F32) / n_freq)
    t = jnp.arange(n_lat)
    a_r = (t // GRID_W).astype(F32)[:, None] * inv
    a_c = (t % GRID_W).astype(F32)[:, None] * inv
    ang = jnp.concatenate([a_r, a_r, a_c, a_c], axis=-1)
    ang = jnp.concatenate([jnp.zeros((n_ctx, MLA_ROPE), F32), ang], axis=0)
    cos, sin = jnp.cos(ang), jnp.sin(ang)
    low = (jnp.arange(MLA_ROPE) % (2 * ROPE_HALF)) < ROPE_HALF
    pad = lambda a: jnp.pad(a, ((0, 0), (0, LANES - MLA_ROPE)))
    return pad(cos), pad(jnp.where(low, -sin, 0.0)), pad(jnp.where(low, 0.0, sin))


MLA_HPS = 2


def _mla_attn_kernel(q_ref, k_ref, v_ref, o_ref, *, n_ctx):
    def attend(n_keys):
        heads = range(MLA_HPS)
        s, p = {}, {}
        for hd in heads:
            q = q_ref[0, :, hd * MLA_QK_PAD:(hd + 1) * MLA_QK_PAD]
            k = k_ref[0, :n_keys, hd * MLA_QK_PAD:(hd + 1) * MLA_QK_PAD]
            s[hd] = lax.dot_general(q, k, (((1,), (1,)), ((), ())), preferred_element_type=F32)
        for hd in heads:
            p[hd] = jnp.exp2(s[hd] - s[hd].max(axis=1, keepdims=True)).astype(BF16)
        for hd in heads:
            v = v_ref[0, :n_keys, hd * MLA_VW:(hd + 1) * MLA_VW]
            r = jnp.dot(p[hd], v, preferred_element_type=F32)
            o_ref[0, :, hd * MLA_V:(hd + 1) * MLA_V] = (
                r[:, :MLA_V] / r[:, MLA_V:MLA_V + 1]).astype(o_ref.dtype)

    @pl.when(pl.program_id(2) == 0)
    def _():
        attend(n_ctx)

    @pl.when(pl.program_id(2) > 0)
    def _():
        attend(k_ref.shape[1])


def mla_attention(lay, q, k, v):
    assert lay.ctx_first
    Bn, S = lay.n_batch, lay.nb * TL
    q3, k3, v3 = (a.reshape(Bn, S, a.shape[-1]) for a in (q, k, v))
    out = pl.pallas_call(
        functools.partial(_mla_attn_kernel, n_ctx=TL),
        grid=(Bn, MLA_HEADS // MLA_HPS, lay.nb),
        in_specs=[pl.BlockSpec((1, TL, MLA_HPS * MLA_QK_PAD), lambda b, h, i: (b, i, h)),
                  pl.BlockSpec((1, S, MLA_HPS * MLA_QK_PAD), lambda b, h, i: (b, 0, h)),
                  pl.BlockSpec((1, S, MLA_HPS * MLA_VW), lambda b, h, i: (b, 0, h))],
        out_specs=pl.BlockSpec((1, TL, MLA_HPS * MLA_V), lambda b, h, i: (b, i, h)),
        out_shape=jax.ShapeDtypeStruct((Bn, S, MLA_HEADS * MLA_V), BF16),
        compiler_params=_params(3),
        name="mla_attention",
    )(q3, k3, v3)
    return out.reshape(lay.rows, MLA_HEADS * MLA_V)


N_PROLOGUE = {"conv": 5, "mlstm": 4, "mla": 1}


def _mixer_out_kernel(*refs, kind, nb, ctx_first):
    n_pro = N_PROLOGUE[kind]
    pro = refs[:n_pro]
    (wout_ref, x_ref, ga_ref, gain_ref, sc_ref, sh_ref, wr_ref, br_ref,
     xo_ref, h2_ref, te_ref, gate_ref, rank_ref, cnt_ref, carry_ref) = refs[n_pro:]
    r = pl.program_id(0)

    if kind == "conv":
        vprev_ref, v_ref, vnext_ref, bg_ref, cw_ref = pro
        j = r % nb
        first = (j == 0) | (j == 1) if ctx_first else (j == 0)
        last = (j == nb - 1) | (j == 0) if ctx_first else (j == nb - 1)
        v = v_ref[...].astype(F32)
        rows = lax.broadcasted_iota(jnp.int32, (TL, 1), 0)
        prev_row = jnp.where(first, 0.0, vprev_ref[BF16_ROWS - 1:BF16_ROWS, :].astype(F32))
        next_row = jnp.where(last, 0.0, vnext_ref[0:1, :].astype(F32))
        up = jnp.where(rows == 0, prev_row, pltpu.roll(v, 1, 0))
        dn = jnp.where(rows == TL - 1, next_row, pltpu.roll(v, TL - 1, 0))
        cw = cw_ref[...]
        a = bg_ref[...].astype(F32) * (up * cw[0:1] + v * cw[1:2] + dn * cw[2:3])
    elif kind == "mlstm":
        hf_ref, hb_ref, og_ref, ng_ref = pro
        hh = hf_ref[...].astype(F32) + hb_ref[...].astype(F32)
        a = jnp.concatenate([_rms(hh[:, h * ML_DV:(h + 1) * ML_DV]) for h in range(ML_HEADS)], axis=1)
        a = a * ng_ref[...] * og_ref[...].astype(F32)
    else:
        a = pro[0][...]

    y = jnp.dot(a.astype(BF16), wout_ref[...], preferred_element_type=F32)
    xn = x_ref[...] + ga_ref[0] * y
    xo_ref[...] = xn
    h2 = _norm_mod(xn, gain_ref[...], sc_ref[0], sh_ref[0])
    h2_ref[...] = _pack_rows(h2)
    logits = _split_dot_t(wr_ref[...], h2) + br_ref[...]

    sub = lax.broadcasted_iota(jnp.int32, (N_EXPERTS, TL), 0)
    sub_k = lax.broadcasted_iota(jnp.int32, (TOP_K, TL), 0)
    work = logits
    sel = jnp.zeros((N_EXPERTS, TL), F32)
    top_e = jnp.zeros((TOP_K, TL), jnp.int32)
    top_v = jnp.zeros((TOP_K, TL), F32)
    picks = []
    for kk in range(TOP_K):
        m = work.max(axis=0, keepdims=True)
        idx = jnp.min(jnp.where(work == m, sub, N_EXPERTS), axis=0, keepdims=True)
        hit = sub == idx
        picks.append(hit)
        sel = jnp.where(hit, 1.0, sel)
        work = jnp.where(hit, -jnp.inf, work)
        top_e = jnp.where(sub_k == kk, idx, top_e)
        top_v = jnp.where(sub_k == kk, m, top_v)
    ex = jnp.exp(top_v - top_v[0:1])
    gate_ref[...] = ex / ex.sum(axis=0, keepdims=True)
    te_ref[...] = top_e

    @pl.when(r == 0)
    def _():
        carry_ref[...] = jnp.zeros_like(carry_ref)

    tr = lax.broadcasted_iota(jnp.int32, (TL, TL), 0)
    tc = lax.broadcasted_iota(jnp.int32, (TL, TL), 1)
    before = jnp.dot(sel.astype(BF16), (tr < tc).astype(BF16), preferred_element_type=F32)
    pos = before + carry_ref[...]
    rank = jnp.zeros((TOP_K, TL), F32)
    for kk in range(TOP_K):
        rk = jnp.sum(jnp.where(picks[kk], pos, 0.0), axis=0, keepdims=True)
        rank = jnp.where(sub_k == kk, rk, rank)
    rank_ref[...] = rank.astype(jnp.int32)
    total = carry_ref[...] + jnp.sum(sel, axis=1, keepdims=True)
    carry_ref[...] = total
    cnt_ref[...] = total


def mixer_out(lay, kind, pro_args, w_out, x, mod, gain_f, w_r, b_r):
    D = D_MODEL
    nb = lay.nb
    if kind == "conv":
        v, bg, cw = pro_args
        per = TL // BF16_ROWS
        last_tile = lay.rows // BF16_ROWS - 1
        pro_specs = [pl.BlockSpec((BF16_ROWS, D), lambda r: (jnp.maximum(r * per - 1, 0), 0)),
                     lay.row_spec(D),
                     pl.BlockSpec((BF16_ROWS, D), lambda r: (jnp.minimum((r + 1) * per, last_tile), 0)),
                     lay.row_spec(D), _const_spec((CONV_WIDTH, D))]
        pro_in = [v, v, v, bg, cw]
    elif kind == "mlstm":
        h_f, h_b, og, ng = pro_args
        pro_specs = [lay.row_spec(ML_V), lay.row_spec(ML_V), lay.row_spec(ML_V), _const_spec((1, ML_V))]
        pro_in = [h_f, h_b, og, ng.reshape(1, ML_V)]
    else:
        pro_specs = [lay.row_spec(D)]
        pro_in = list(pro_args)
    k_in = w_out.shape[0]
    small = lambda dt: jax.ShapeDtypeStruct((TOP_K, lay.rows), dt)
    small_spec = pl.BlockSpec((TOP_K, TL), lambda r: (0, r))
    return pl.pallas_call(
        functools.partial(_mixer_out_kernel, kind=kind, nb=nb, ctx_first=lay.ctx_first),
        grid=(lay.n_blocks,),
        in_specs=pro_specs + [_const_spec((k_in, D)), lay.row_spec(D), lay.mod_spec(2),
                              _const_spec((1, D)), lay.mod_spec(4), lay.mod_spec(3),
                              _const_spec((2 * N_EXPERTS, D)), _const_spec((N_EXPERTS, 1))],
        out_specs=[lay.row_spec(D), lay.row_spec(PACK_W), small_spec, small_spec, small_spec,
                   _const_spec((N_EXPERTS, 1))],
        out_shape=[jax.ShapeDtypeStruct((lay.rows, D), F32),
                   jax.ShapeDtypeStruct((lay.rows, PACK_W), jnp.int32),
                   small(jnp.int32), small(F32), small(jnp.int32),
                   jax.ShapeDtypeStruct((N_EXPERTS, 1), F32)],
        scratch_shapes=[pltpu.VMEM((N_EXPERTS, 1), F32)],
        compiler_params=_params(1),
        name="mixer_out_" + kind,
    )(*pro_in, w_out.astype(BF16), x, mod, gain_f, mod, mod, _split_weight_t(w_r),
      b_r.reshape(N_EXPERTS, 1))


def _expert_ffn_kernel(blk_e_ref, first_ref, valid_ref, x_ref, w1_ref, b1_ref, w2_ref, b2_ref, o_ref,
                       w1b_ref, w2b_ref):
    del blk_e_ref
    i = pl.program_id(0)
    half = MOE_BLOCK // 2

    @pl.when(first_ref[i] == 1)
    def _():
        w1b_ref[...] = w1_ref[0, 0].astype(BF16)
        w2b_ref[...] = w2_ref[0, 0].astype(BF16)

    def ffn(rows):
        x = _unpack_rows(x_ref[rows, :], BF16)
        h = jnp.dot(x, w1b_ref[...], preferred_element_type=F32) + b1_ref[0, 0]
        glu = jnp.minimum(h[:, :MOE_FF], SWIGLU_LIMIT)
        lin = jnp.clip(h[:, MOE_FF:], -SWIGLU_LIMIT, SWIGLU_LIMIT)
        act = glu * jax.nn.sigmoid(SWIGLU_ALPHA * glu) * (lin + 1.0)
        y = jnp.dot(act.astype(BF16), w2b_ref[...], preferred_element_type=F32)
        o_ref[rows, :] = _pack_rows(y + b2_ref[0, 0])

    valid = valid_ref[i]

    @pl.when(valid > half)
    def _():
        ffn(pl.ds(0, MOE_BLOCK))

    @pl.when((valid > 0) & (valid <= half))
    def _():
        ffn(pl.ds(0, half))
        o_ref[pl.ds(half, half), :] = jnp.zeros((half, PACK_W), o_ref.dtype)

    @pl.when(valid == 0)
    def _():
        o_ref[...] = jnp.zeros_like(o_ref)


def expert_ffn(layer, xp, blk_e, blk_first, blk_valid, w1, b1, w2, b2):
    n_rows = xp.shape[0]
    D, F2 = D_MODEL, 2 * MOE_FF
    n_blk = n_rows // MOE_BLOCK
    grid_spec = pltpu.PrefetchScalarGridSpec(
        num_scalar_prefetch=3,
        grid=(n_blk,),
        in_specs=[
            pl.BlockSpec((MOE_BLOCK, PACK_W), lambda i, be, fi, nu: (i, 0)),
            pl.BlockSpec((1, 1, D, F2), lambda i, be, fi, nu: (layer, be[i], 0, 0)),
            pl.BlockSpec((1, 1, 1, F2), lambda i, be, fi, nu: (layer, be[i], 0, 0)),
            pl.BlockSpec((1, 1, MOE_FF, D), lambda i, be, fi, nu: (layer, be[i], 0, 0)),
            pl.BlockSpec((1, 1, 1, D), lambda i, be, fi, nu: (layer, be[i], 0, 0)),
        ],
        out_specs=pl.BlockSpec((MOE_BLOCK, PACK_W), lambda i, be, fi, nu: (i, 0)),
        scratch_shapes=[pltpu.VMEM((D, F2), BF16), pltpu.VMEM((MOE_FF, D), BF16)],
    )
    return pl.pallas_call(
        _expert_ffn_kernel,
        grid_spec=grid_spec,
        out_shape=jax.ShapeDtypeStruct((n_rows, PACK_W), jnp.int32),
        compiler_params=_params(1),
        name="expert_ffn",
    )(blk_e, blk_first, blk_valid, xp, w1, b1, w2, b2)


SC_CORES = 2
SC_SUBCORES = 16
SC_CHUNKS = (64, 32)


def _sc_chunk(*counts):
    n_workers = SC_CORES * SC_SUBCORES
    for chunk in SC_CHUNKS:
        if all(n % (chunk * n_workers) == 0 for n in counts):
            return chunk
    raise ValueError(f"row counts {counts} do not split over {n_workers} subcores")


def sc_gather(table, idx):
    n_idx = idx.shape[0]
    width = table.shape[1]
    n_workers = SC_CORES * SC_SUBCORES
    per_worker = n_idx // n_workers
    chunk = _sc_chunk(n_idx)
    n_chunks = per_worker // chunk
    assert n_chunks * chunk * n_workers == n_idx and n_chunks % 2 == 0
    mesh = plsc.VectorSubcoreMesh(core_axis_name="c", subcore_axis_name="s",
                                  num_cores=SC_CORES, num_subcores=SC_SUBCORES)

    def body(table_hbm, idx_hbm, out_hbm, idx_v, rows_v, gsem, wsem):
        wid = lax.axis_index("s") * SC_CORES + lax.axis_index("c")
        pltpu.sync_copy(idx_hbm.at[wid], idx_v)

        def gather(ci, slot):
            return pltpu.make_async_copy(table_hbm.at[idx_v.at[ci]], rows_v.at[slot], gsem.at[slot])

        def write(ci, slot):
            return pltpu.make_async_copy(rows_v.at[slot], out_hbm.at[ci, wid], wsem.at[slot])

        gather(0, 0).start()

        @pl.loop(0, n_chunks, step=2)
        def _(c0):
            for slot in range(2):
                ci = c0 + slot
                other = 1 - slot

                @pl.when(ci + 1 < n_chunks)
                def _():
                    @pl.when(ci >= 1)
                    def _():
                        write(ci - 1, other).wait()
                    gather(ci + 1, other).start()

                gather(ci, slot).wait()
                write(ci, slot).start()

        write(n_chunks - 2, 0).wait()
        write(n_chunks - 1, 1).wait()

    out = pl.kernel(
        body,
        out_type=jax.ShapeDtypeStruct((n_chunks, n_workers, chunk, width), table.dtype),
        mesh=mesh,
        scratch_types=[pltpu.VMEM((n_chunks, chunk), jnp.int32),
                       pltpu.VMEM((2, chunk, width), table.dtype),
                       pltpu.SemaphoreType.DMA((2,)),
                       pltpu.SemaphoreType.DMA((2,))],
        name="sc_gather",
    )(table, idx.reshape(n_chunks, n_workers, chunk).transpose(1, 0, 2))
    return out.reshape(n_idx, width)


def sc_dispatch(table, dest, pad_rows):
    n_tok, width = table.shape
    n_picks = dest.shape[0]
    n_pad = pad_rows.shape[0]
    n_workers = SC_CORES * SC_SUBCORES
    chunk = _sc_chunk(n_tok, n_pad)
    per_w = n_tok // chunk // n_workers
    pad_w = n_pad // chunk // n_workers
    assert per_w * chunk * n_workers == n_tok and pad_w * chunk * n_workers == n_pad
    mesh = plsc.VectorSubcoreMesh(core_axis_name="c", subcore_axis_name="s",
                                  num_cores=SC_CORES, num_subcores=SC_SUBCORES)

    def body(table_hbm, idx_hbm, pad_hbm, zero_hbm, out_hbm, idx_v, pad_v, rows_v, zero_v,
             rsem, ssem, zsem):
        wid = lax.axis_index("s") * SC_CORES + lax.axis_index("c")
        pltpu.sync_copy(idx_hbm.at[wid], idx_v)
        pltpu.sync_copy(pad_hbm.at[wid], pad_v)
        pltpu.sync_copy(zero_hbm, zero_v)

        def zero_fill(pc):
            return pltpu.make_async_copy(zero_v, out_hbm.at[pad_v.at[pc]], zsem)

        for pc in range(pad_w):
            zero_fill(pc).start()

        def scatter(ci, kk):
            return pltpu.make_async_copy(rows_v, out_hbm.at[idx_v.at[ci * n_picks + kk]], ssem)

        @pl.loop(0, per_w)
        def _(ci):
            pltpu.async_copy(table_hbm.at[ci, wid], rows_v, rsem).wait()
            for kk in range(n_picks):
                scatter(ci, kk).start()
            for kk in range(n_picks):
                scatter(ci, kk).wait()

        for pc in range(pad_w):
            zero_fill(pc).wait()

    idx = dest.reshape(n_picks, per_w, n_workers, chunk).transpose(2, 1, 0, 3)
    idx = idx.reshape(n_workers, per_w * n_picks, chunk)
    return pl.kernel(
        body,
        out_type=jax.ShapeDtypeStruct((n_tok * n_picks + n_pad, width), table.dtype),
        mesh=mesh,
        scratch_types=[pltpu.VMEM((per_w * n_picks, chunk), jnp.int32),
                       pltpu.VMEM((pad_w, chunk), jnp.int32),
                       pltpu.VMEM((chunk, width), table.dtype),
                       pltpu.VMEM((chunk, width), table.dtype),
                       pltpu.SemaphoreType.DMA, pltpu.SemaphoreType.DMA, pltpu.SemaphoreType.DMA],
        name="sc_dispatch",
    )(table.reshape(per_w, n_workers, chunk, width), idx,
      pad_rows.reshape(n_workers, pad_w, chunk), jnp.zeros((chunk, width), table.dtype))


def _combine_kernel(x_ref, *refs):
    y_refs, (gate_ref, gf_ref, o_ref) = refs[:TOP_K], refs[TOP_K:]
    gates = gate_ref[...]
    acc = gates[:, 0:1] * _unpack_rows(y_refs[0][...], F32)
    for kk in range(1, TOP_K):
        acc = acc + gates[:, kk:kk + 1] * _unpack_rows(y_refs[kk][...], F32)
    o_ref[...] = x_ref[...] + gf_ref[0] * acc


def moe_combine(lay, x, yg, gates, mod, drop_ctx):
    D = D_MODEL
    if drop_ctx:
        nbo = lay.nb - 1
        src = lambda r: (r // nbo) * lay.nb + 1 + r % nbo
        n_out = lay.n_batch * nbo
    else:
        src = lambda r: r
        n_out = lay.n_blocks
    y_specs = [pl.BlockSpec((TL, PACK_W), functools.partial(lambda kk, r: (kk * lay.n_blocks + src(r), 0), kk))
               for kk in range(TOP_K)]
    return pl.pallas_call(
        _combine_kernel,
        grid=(n_out,),
        in_specs=[pl.BlockSpec((TL, D), lambda r: (src(r), 0))] + y_specs + [
            pl.BlockSpec((TL, TOP_K), lambda r: (src(r), 0)),
            pl.BlockSpec((1, 1, D), lambda r: (lay.mod_row(src(r)), 0, 5))],
        out_specs=pl.BlockSpec((TL, D), lambda r: (r, 0)),
        out_shape=jax.ShapeDtypeStruct((n_out * TL, D), F32),
        compiler_params=_params(1),
        name="moe_combine",
    )(x, yg, yg, yg, yg, gates, mod)


def moe_route(top_e, rank, counts):
    T = top_e.shape[1]
    assert (T * TOP_K) % MOE_BLOCK == 0
    counts = counts.reshape(N_EXPERTS).astype(jnp.int32)
    padded = (counts + MOE_BLOCK - 1) // MOE_BLOCK * MOE_BLOCK
    padded_end = jnp.cumsum(padded)
    padded_start = padded_end - padded
    experts = jnp.arange(N_EXPERTS)
    start_of = jnp.sum(jnp.where(top_e[..., None] == experts, padded_start, 0), axis=-1)
    dest = (start_of + rank).astype(jnp.int32)
    n_pad = N_EXPERTS * MOE_BLOCK
    n_rows = T * TOP_K + n_pad
    n_blk = n_rows // MOE_BLOCK
    blk_start = jnp.arange(n_blk) * MOE_BLOCK
    blk_e = jnp.minimum(jnp.sum(padded_end[None, :] <= blk_start[:, None], axis=1), N_EXPERTS - 1)
    blk_e = blk_e.astype(jnp.int32)
    blk_first = jnp.concatenate([jnp.ones((1,), jnp.int32), (blk_e[1:] != blk_e[:-1]).astype(jnp.int32)])
    blk_hot = blk_e[:, None] == experts
    in_grp = blk_start - jnp.sum(jnp.where(blk_hot, padded_start, 0), axis=-1)
    blk_valid = jnp.clip(jnp.sum(jnp.where(blk_hot, counts, 0), axis=-1) - in_grp, 0, MOE_BLOCK)
    blk_valid = jnp.where(blk_start < padded_end[-1], blk_valid, 0).astype(jnp.int32)
    tail = padded - counts
    tail_end = jnp.cumsum(tail)
    j = jnp.arange(n_pad)
    owner = jnp.sum(tail_end[None, :] <= j[:, None], axis=1)
    base = padded_start + counts - (tail_end - tail)
    in_group = jnp.sum(jnp.where(owner[:, None] == experts, base, 0), axis=-1) + j
    pad_rows = jnp.where(j < tail_end[-1], in_group, padded_end[-1] + j - tail_end[-1])
    return dest, pad_rows.astype(jnp.int32), blk_e, blk_first, blk_valid


def kernel(x, c, ctx, c_ctx, norm_mix, norm_ffn, w_mod, b_mod, conv_w_in, conv_w, conv_w_out, ml_w_in, ml_b_gate, ml_norm, ml_w_out, mla_w_in, mla_q_norm, mla_kv_norm, mla_w_uq, mla_w_ukv, mla_qn_nope, mla_qn_rope, mla_kn_nope, mla_kn_rope, mla_w_out, moe_w_router, moe_b_router, moe_w1, moe_b1, moe_w2, moe_b2):
    Bn, n_lat, D = x.shape
    n_ctx = ctx.shape[1]
    assert D == D_MODEL and n_ctx == TL and n_lat % TL == 0
    assert (DEPTH - 1) % N_MIXERS == 0
    full = Layout(Bn, (n_ctx + n_lat) // TL, True)
    lat_only = Layout(Bn, n_lat // TL, False)
    mods = ada_all(c, c_ctx, w_mod, b_mod)
    tables = rope_tables(n_ctx, n_lat)
    b1_all = moe_b1.reshape(DEPTH, N_EXPERTS, 1, 2 * MOE_FF)
    b2_all = moe_b2.reshape(DEPTH, N_EXPERTS, 1, D)
    X = jnp.concatenate([ctx, x], axis=1).reshape(full.rows, D)
    for layer in range(DEPTH):
        kind, j = layer % N_MIXERS, layer // N_MIXERS
        lay = lat_only if layer == DEPTH - 1 else full
        mod = mods[layer]
        gain_a = norm_mix[layer].reshape(1, D)
        gain_f = norm_ffn[layer].reshape(1, D)
        if kind == 0:
            bg, v = conv_in(lay, X, gain_a, mod, conv_w_in[j].astype(BF16))
            pro, w_out, name = (v, bg, conv_w[j]), conv_w_out[j], "conv"
        elif kind == 1:
            w = ml_w_in[j]
            n_main = 2 * ML_QK + 2 * ML_V
            w_main = jnp.concatenate([w[:, :ML_QK] * ML_DQK ** -0.5, w[:, ML_QK:n_main]], axis=1)
            q, k, kt, v, og, g, gt = mlstm_in(lay, X, gain_a, mod, w_main.astype(BF16), w[:, n_main:],
                                              ml_b_gate[j])
            h_f, h_b = mlstm_scan(lay, q, k, kt, v, g, gt)
            pro, w_out, name = (h_f, h_b, og, ml_norm[j]), ml_w_out[j], "mlstm"
        else:
            q, k, v = mla_in(lay, X, gain_a, mod, mla_w_in[j], mla_q_norm[j], mla_kv_norm[j],
                             mla_w_uq[j], mla_w_ukv[j], mla_qn_nope[j], mla_qn_rope[j],
                             mla_kn_nope[j], mla_kn_rope[j], tables)
            pro, w_out, name = (mla_attention(lay, q, k, v),), mla_w_out[j], "mla"
        X, h2, top_e, gates, rank, counts = mixer_out(
            lay, name, pro, w_out, X, mod, gain_f, moe_w_router[layer], moe_b_router[layer])
        dest, pad_rows, blk_e, blk_first, blk_valid = moe_route(top_e, rank, counts)
        xp = sc_dispatch(h2, dest, pad_rows)
        yp = expert_ffn(layer, xp, blk_e, blk_first, blk_valid, moe_w1, b1_all, moe_w2, b2_all)
        yg = sc_gather(yp, dest.reshape(-1))
        X = moe_combine(lay, X, yg, gates.T, mod, drop_ctx=(layer == DEPTH - 2))
    return X.reshape(Bn, n_lat, D)
```

```python
import functools

import jax
import jax.numpy as jnp
from jax import lax
from jax.experimental import pallas as pl
from jax.experimental.pallas import tpu as pltpu
from jax.experimental.pallas import tpu_sc as plsc

D_MODEL = 1024
DEPTH = 4
GRID_W = 64
N_MIXERS = 3
N_ADA = 6
RMS_EPS = 1e-6
CONV_WIDTH = 3
ML_HEADS = 8
ML_DQK = 64
ML_DV = 128
ML_QK = ML_HEADS * ML_DQK
ML_V = ML_HEADS * ML_DV
GATE_CAP = 15.0
MLA_HEADS = 8
MLA_NOPE = 128
MLA_ROPE = 64
MLA_V = 128
MLA_Q_LORA = 384
MLA_KV_LORA = 256
MLA_SCALE = (MLA_NOPE + MLA_ROPE) ** -0.5
ROPE_THETA = 10000.0
N_EXPERTS = 32
TOP_K = 4
MOE_FF = D_MODEL
SWIGLU_ALPHA = 1.702
SWIGLU_LIMIT = 7.0
MOE_BLOCK = 512

TL = 256
LANES = 128
BF16_ROWS = 16
VMEM_LIMIT = 48 * 1024 * 1024
HI = lax.Precision.HIGHEST
F32 = jnp.float32
BF16 = jnp.bfloat16


def _params(n_axes):
    return pltpu.CompilerParams(dimension_semantics=("arbitrary",) * n_axes,
                                vmem_limit_bytes=VMEM_LIMIT)


def _rms(x, width=None):
    width = x.shape[-1] if width is None else width
    return x * lax.rsqrt(jnp.sum(x * x, axis=-1, keepdims=True) * (1.0 / width) + RMS_EPS)


def _norm_mod(x, gain, scale, shift):
    return _rms(x) * (gain * (1.0 + scale)) + shift


def _split_weight_t(w):
    hi = w.astype(BF16)
    lo = (w - hi.astype(F32)).astype(BF16)
    return jnp.concatenate([hi.T, lo.T], axis=0)


def _split_dot_t(w2, h):
    n = w2.shape[0] // 2
    dn = (((1,), (1,)), ((), ()))
    h_hi = h.astype(BF16)
    h_lo = (h - h_hi.astype(F32)).astype(BF16)
    both = lax.dot_general(w2, h_hi, dn, preferred_element_type=F32)
    cross = lax.dot_general(w2[:n], h_lo, dn, preferred_element_type=F32)
    return both[:n] + both[n:] + cross


PACK_W = D_MODEL // 2
HIGH_HALF = -65536


def _pack_rows(x):
    xb = x.astype(BF16).astype(F32)
    lo = lax.bitcast_convert_type(xb[:, :PACK_W], jnp.int32)
    hi = lax.bitcast_convert_type(xb[:, PACK_W:], jnp.int32)
    return hi | lax.shift_right_logical(lo, 16)


def _unpack_rows(w, dtype):
    lo = lax.bitcast_convert_type(lax.shift_left(w, 16), F32)
    hi = lax.bitcast_convert_type(w & HIGH_HALF, F32)
    return jnp.concatenate([lo.astype(dtype), hi.astype(dtype)], axis=1)


class Layout:
    def __init__(self, n_batch, nb, ctx_first):
        self.n_batch, self.nb, self.ctx_first = n_batch, nb, ctx_first
        self.n_blocks = n_batch * nb
        self.rows = self.n_blocks * TL

    def mod_row(self, r):
        b = r // self.nb
        return jnp.where(r % self.nb == 0, self.n_batch, b) if self.ctx_first else b

    def row_spec(self, width):
        return pl.BlockSpec((TL, width), lambda r: (r, 0))

    def mod_spec(self, piece):
        return pl.BlockSpec((1, 1, D_MODEL), lambda r: (self.mod_row(r), 0, piece))


def _const_spec(shape):
    return pl.BlockSpec(shape, lambda *_: (0,) * len(shape))


ADA_ROWS = 16
ADA_TN = 1536


def _ada_kernel(c_ref, w_ref, b_ref, o_ref):
    c = c_ref[...]
    s = c * jax.nn.sigmoid(c)
    o_ref[0] = jnp.dot(s, w_ref[0], precision=HI, preferred_element_type=F32) + b_ref[0]


def ada_all(c, c_ctx, w_mod, b_mod):
    Bn, D = c.shape
    assert Bn + 1 <= ADA_ROWS
    cond = jnp.zeros((ADA_ROWS, D), F32).at[:Bn].set(c).at[Bn].set(c_ctx)
    out = pl.pallas_call(
        _ada_kernel,
        grid=(DEPTH, N_ADA * D // ADA_TN),
        in_specs=[pl.BlockSpec((ADA_ROWS, D), lambda l, n: (0, 0)),
                  pl.BlockSpec((1, D, ADA_TN), lambda l, n: (l, 0, n)),
                  pl.BlockSpec((1, 1, ADA_TN), lambda l, n: (l, 0, n))],
        out_specs=pl.BlockSpec((1, ADA_ROWS, ADA_TN), lambda l, n: (l, 0, n)),
        out_shape=jax.ShapeDtypeStruct((DEPTH, ADA_ROWS, N_ADA * D), F32),
        compiler_params=_params(2),
        name="ada_mod",
    )(cond, w_mod, b_mod.reshape(DEPTH, 1, N_ADA * D))
    return out[:, :Bn + 1, None, :]


def _conv_in_kernel(x_ref, gain_ref, sc_ref, sh_ref, w_ref, bg_ref, v_ref):
    D = D_MODEL
    h = _norm_mod(x_ref[...], gain_ref[...], sc_ref[0], sh_ref[0]).astype(BF16)
    p = jnp.dot(h, w_ref[...], preferred_element_type=F32)
    bg_ref[...] = p[:, :D].astype(BF16)
    v_ref[...] = (p[:, D:2 * D] * p[:, 2 * D:]).astype(BF16)


def conv_in(lay, x, gain, mod, w_in):
    D = D_MODEL
    sds = jax.ShapeDtypeStruct((lay.rows, D), BF16)
    return pl.pallas_call(
        _conv_in_kernel,
        grid=(lay.n_blocks,),
        in_specs=[lay.row_spec(D), _const_spec((1, D)), lay.mod_spec(1), lay.mod_spec(0),
                  _const_spec((D, 3 * D))],
        out_specs=[lay.row_spec(D), lay.row_spec(D)],
        out_shape=[sds, sds],
        compiler_params=_params(1),
        name="conv_in",
    )(x, gain, mod, mod, w_in)


ML_T = 256
ML_SW = 2 * ML_DV
ML_NG = 4 * ML_HEADS


LOG2E = 1.4426950408889634


def _gate_act(g, is_forget):
    g = GATE_CAP * jnp.tanh(g * (1.0 / GATE_CAP))
    log_sig = jnp.minimum(g, 0.0) - jnp.log(1.0 + jnp.exp(-jnp.abs(g)))
    return jnp.where(is_forget, log_sig, g) * LOG2E


def _mlstm_in_kernel(x_ref, gain_ref, sc_ref, sh_ref, w_ref, wkt_ref, wg_ref, bgt_ref,
                     q_ref, k_ref, kt_ref, v_ref, og_ref, g_ref, gt_ref):
    h = _norm_mod(x_ref[...], gain_ref[...], sc_ref[0], sh_ref[0])
    hb = h.astype(BF16)
    p = jnp.dot(hb, w_ref[...], preferred_element_type=F32)
    q_ref[...] = p[:, :ML_QK].astype(BF16)
    k_ref[...] = p[:, ML_QK:2 * ML_QK].astype(BF16)
    kt_ref[...] = lax.dot_general(wkt_ref[...], hb, (((1,), (1,)), ((), ())),
                                  preferred_element_type=F32).astype(BF16)
    v_ref[...] = p[:, 2 * ML_QK:2 * ML_QK + ML_V].astype(BF16)
    og_ref[...] = jax.nn.sigmoid(p[:, 2 * ML_QK + ML_V:]).astype(BF16)
    gt = _split_dot_t(wg_ref[...], h) + bgt_ref[...]
    row = lax.broadcasted_iota(jnp.int32, gt.shape, 0)
    gt = _gate_act(gt, (row // ML_HEADS) % 2 == 1)
    gt_ref[...] = gt
    eye = (lax.broadcasted_iota(jnp.int32, (TL, TL), 0)
           == lax.broadcasted_iota(jnp.int32, (TL, TL), 1)).astype(F32)
    g_ref[...] = lax.dot_general(eye, gt, (((1,), (1,)), ((), ())), precision=HI,
                                 preferred_element_type=F32)


def mlstm_in(lay, x, gain, mod, w_main, w_g, b_g):
    D = D_MODEL
    n_main = 2 * ML_QK + 2 * ML_V
    bf = lambda w: jax.ShapeDtypeStruct((lay.rows, w), BF16)
    return pl.pallas_call(
        _mlstm_in_kernel,
        grid=(lay.n_blocks,),
        in_specs=[lay.row_spec(D), _const_spec((1, D)), lay.mod_spec(1), lay.mod_spec(0),
                  _const_spec((D, n_main)), _const_spec((ML_QK, D)), _const_spec((2 * ML_NG, D)),
                  _const_spec((ML_NG, 1))],
        out_specs=[lay.row_spec(ML_QK), lay.row_spec(ML_QK), pl.BlockSpec((ML_QK, TL), lambda r: (0, r)),
                   lay.row_spec(ML_V), lay.row_spec(ML_V),
                   lay.row_spec(ML_NG), pl.BlockSpec((ML_NG, TL), lambda r: (0, r))],
        out_shape=[bf(ML_QK), bf(ML_QK), jax.ShapeDtypeStruct((ML_QK, lay.rows), BF16),
                   bf(ML_V), bf(ML_V),
                   jax.ShapeDtypeStruct((lay.rows, ML_NG), F32),
                   jax.ShapeDtypeStruct((ML_NG, lay.rows), F32)],
        compiler_params=_params(1),
        name="mlstm_in",
    )(x, gain, mod, mod, w_main, w_main[:, ML_QK:2 * ML_QK].T, _split_weight_t(w_g),
      b_g.reshape(ML_NG, 1))


def _split3(x):
    hi = x.astype(BF16)
    r1 = x - hi.astype(F32)
    mid = r1.astype(BF16)
    lo = (r1 - mid.astype(F32)).astype(BF16)
    return hi, mid, lo


def _dot_exact01(x, sel01, x_on_left):
    sel = sel01.astype(BF16)
    parts = [jnp.dot(p, sel, preferred_element_type=F32) if x_on_left
             else jnp.dot(sel, p, preferred_element_type=F32) for p in _split3(x)]
    return parts[0] + parts[1] + parts[2]


def _mlstm_dir(reverse, q_ref, k_ref, kt_ref, v_ref, g_ref, gt_ref, o_ref, s_ref, m_ref):
    T = ML_T
    row = lax.broadcasted_iota(jnp.int32, (T, T), 0)
    col = lax.broadcasted_iota(jnp.int32, (T, T), 1)
    mask = (col >= row) if reverse else (col <= row)
    gt = gt_ref[...]
    bc = _dot_exact01(g_ref[...], mask, x_on_left=False)
    br = _dot_exact01(gt, (row >= col) if reverse else (row <= col), x_on_left=True)
    gi, gf = (2 * ML_HEADS, 3 * ML_HEADS) if reverse else (0, ML_HEADS)
    lane = lax.broadcasted_iota(jnp.int32, (T, 2 * ML_DQK), 1)
    sub = lax.broadcasted_iota(jnp.int32, (2 * ML_DQK, T), 0)
    gate_row = lax.broadcasted_iota(jnp.int32, (ML_NG, ML_DV), 0)
    ones = jnp.ones((T, ML_DV), BF16)
    heads = range(ML_HEADS)
    qm, vx, s_raw = {}, {}, {}
    for h in heads:
        pair = (h // 2) * 2 * ML_DQK
        own = (lane >= ML_DQK) if (h % 2) else (lane < ML_DQK)
        qp = q_ref[:, pair:pair + 2 * ML_DQK]
        qm[h] = jnp.where(own, qp, jnp.zeros_like(qp))
        vx[h] = jnp.concatenate([v_ref[:, h * ML_DV:(h + 1) * ML_DV], ones], axis=1)
        s_raw[h] = lax.dot_general(qm[h], k_ref[:, pair:pair + 2 * ML_DQK], (((1,), (1,)), ((), ())),
                                   preferred_element_type=F32)
    p, a, u_rep, m_prev, s_prev = {}, {}, {}, {}, {}
    for h in heads:
        c_row = gt[gi + h:gi + h + 1, :] - br[gf + h:gf + h + 1, :]
        e = jnp.where(mask, c_row, -jnp.inf)
        m_prev[h] = m_ref[h][0:1, 0:1]
        u = jnp.maximum(m_prev[h], jnp.max(e, axis=1, keepdims=True))
        p[h] = (s_raw[h] * jnp.exp2(e - u)).astype(BF16)
        u_rep[h] = jnp.broadcast_to(u, (T, ML_DV))
        a[h] = jnp.exp2(m_prev[h] - u_rep[h])
        s_prev[h] = s_ref[h]
    for h in heads:
        r = jnp.dot(p[h], vx[h], preferred_element_type=F32)
        qs = jnp.dot(qm[h], s_prev[h].astype(BF16), preferred_element_type=F32)
        num = r[:, :ML_DV] + a[h] * qs[:, :ML_DV]
        den = r[:, ML_DV:] + a[h] * qs[:, ML_DV:]
        b_rep = _dot_exact01(bc, gate_row == gf + h, x_on_left=True)
        floor = jnp.exp2(-(b_rep + u_rep[h]))
        o_ref[:, h * ML_DV:(h + 1) * ML_DV] = (num / jnp.maximum(jnp.abs(den), floor)).astype(o_ref.dtype)
    for h in heads:
        pair = (h // 2) * 2 * ML_DQK
        b_row = br[gf + h:gf + h + 1, :]
        tot = b_row[:, 0:1] if reverse else b_row[:, T - 1:T]
        g_row = tot - b_row + gt[gi + h:gi + h + 1, :]
        m_new = jnp.maximum(tot + m_prev[h], jnp.max(g_row, axis=1, keepdims=True))
        decay = jnp.exp2(tot + m_prev[h] - m_new)
        wk = jnp.exp2(g_row - m_new)
        own_t = (sub >= ML_DQK) if (h % 2) else (sub < ML_DQK)
        kt = kt_ref[pair:pair + 2 * ML_DQK, :].astype(F32)
        kw = jnp.where(own_t, kt * wk, 0.0).astype(BF16)
        s_ref[h] = decay * s_prev[h] + jnp.dot(kw, vx[h], preferred_element_type=F32)
        m_ref[h] = jnp.broadcast_to(m_new, m_ref.shape[1:])


def _mlstm_scan_kernel(qf_ref, kf_ref, ktf_ref, vf_ref, gf_ref, gtf_ref,
                       qb_ref, kb_ref, ktb_ref, vb_ref, gb_ref, gtb_ref,
                       of_ref, ob_ref, sf_ref, mf_ref, sb_ref, mb_ref):
    @pl.when(pl.program_id(1) == 0)
    def _():
        sf_ref[...] = jnp.zeros_like(sf_ref)
        mf_ref[...] = jnp.zeros_like(mf_ref)
        sb_ref[...] = jnp.zeros_like(sb_ref)
        mb_ref[...] = jnp.zeros_like(mb_ref)

    _mlstm_dir(False, qf_ref, kf_ref, ktf_ref, vf_ref, gf_ref, gtf_ref, of_ref, sf_ref, mf_ref)
    _mlstm_dir(True, qb_ref, kb_ref, ktb_ref, vb_ref, gb_ref, gtb_ref, ob_ref, sb_ref, mb_ref)


def mlstm_scan(lay, q, k, kt, v, g, gt):
    assert lay.ctx_first
    per = TL // ML_T
    nb = lay.nb * per
    rev = lambda j: jnp.where(j < per, per - 1 - j, nb + per - 1 - j)
    fwd = lambda b, j: (b * nb + j, 0)
    bwd = lambda b, j: (b * nb + rev(j), 0)
    fwd_t = lambda b, j: (0, b * nb + j)
    bwd_t = lambda b, j: (0, b * nb + rev(j))

    def specs(im, imt):
        return [pl.BlockSpec((ML_T, ML_QK), im), pl.BlockSpec((ML_T, ML_QK), im),
                pl.BlockSpec((ML_QK, ML_T), imt), pl.BlockSpec((ML_T, ML_V), im),
                pl.BlockSpec((ML_T, ML_NG), im), pl.BlockSpec((ML_NG, ML_T), imt)]

    out_sds = jax.ShapeDtypeStruct((lay.rows, ML_V), BF16)
    state = [pltpu.VMEM((ML_HEADS, 2 * ML_DQK, ML_SW), F32),
             pltpu.VMEM((ML_HEADS, 8, LANES), F32)]
    return pl.pallas_call(
        _mlstm_scan_kernel,
        grid=(lay.n_batch, nb),
        in_specs=specs(fwd, fwd_t) + specs(bwd, bwd_t),
        out_specs=[pl.BlockSpec((ML_T, ML_V), fwd), pl.BlockSpec((ML_T, ML_V), bwd)],
        out_shape=[out_sds, out_sds],
        scratch_shapes=state + state,
        compiler_params=_params(2),
        name="mlstm_scan",
    )(q, k, kt, v, g, gt, q, k, kt, v, g, gt)


MLA_QK_PAD = 256
MLA_VW = 2 * MLA_V
MLA_IN_PAD = MLA_Q_LORA + MLA_KV_LORA + LANES
ROPE_HALF = MLA_ROPE // 4


def _mla_in_kernel(x_ref, gain_ref, sc_ref, sh_ref, win_ref, wuq_ref, wukv_ref, qn_ref, kvn_ref,
                   qnn_ref, qnr_ref, knn_ref, knr_ref, cos_ref, sa_ref, sb_ref,
                   q_out, k_out, v_out):
    h = _norm_mod(x_ref[...], gain_ref[...], sc_ref[0], sh_ref[0]).astype(BF16)
    p = jnp.dot(h, win_ref[...], preferred_element_type=F32)
    cq = _rms(p[:, :MLA_Q_LORA]) * qn_ref[...]
    ckv = _rms(p[:, MLA_Q_LORA:MLA_Q_LORA + MLA_KV_LORA]) * kvn_ref[...]
    kr = p[:, MLA_Q_LORA + MLA_KV_LORA:]
    q = jnp.dot(cq.astype(BF16), wuq_ref[...], preferred_element_type=F32)
    kv = jnp.dot(ckv.astype(BF16), wukv_ref[...], preferred_element_type=F32)
    nv = MLA_HEADS * MLA_NOPE
    ones_col = (lax.broadcasted_iota(jnp.int32, (TL, MLA_VW - MLA_V), 1) == 0).astype(BF16)
    cos, sa, sb = cos_ref[...], sa_ref[...], sb_ref[...]

    def rope(xp):
        return (xp * cos + pltpu.roll(xp, LANES - ROPE_HALF, 1) * sa + pltpu.roll(xp, ROPE_HALF, 1) * sb)

    kr = rope(_rms(kr, MLA_ROPE) * knr_ref[...]).astype(BF16)
    for hd in range(MLA_HEADS):
        c0 = hd * MLA_QK_PAD
        qn = _rms(q[:, c0:c0 + MLA_NOPE]) * qnn_ref[...]
        qr = rope(_rms(q[:, c0 + MLA_NOPE:c0 + MLA_QK_PAD], MLA_ROPE) * qnr_ref[...])
        q_out[:, c0:c0 + MLA_NOPE] = qn.astype(BF16)
        q_out[:, c0 + MLA_NOPE:c0 + MLA_QK_PAD] = qr.astype(BF16)
        v0 = nv + hd * MLA_V
        v_out[:, hd * MLA_VW:hd * MLA_VW + MLA_V] = kv[:, v0:v0 + MLA_V].astype(BF16)
        v_out[:, hd * MLA_VW + MLA_V:(hd + 1) * MLA_VW] = ones_col
        kn = _rms(kv[:, hd * MLA_NOPE:(hd + 1) * MLA_NOPE]) * knn_ref[...]
        k_out[:, c0:c0 + MLA_NOPE] = kn.astype(BF16)
        k_out[:, c0 + MLA_NOPE:c0 + MLA_QK_PAD] = kr


def _pad_lanes(g):
    return jnp.pad(g, (0, LANES - g.shape[0])).reshape(1, LANES)


def mla_in(lay, x, gain, mod, w_in, q_norm, kv_norm, w_uq, w_ukv, qn_nope, qn_rope, kn_nope, kn_rope,
           tables):
    D = D_MODEL
    Hn = MLA_HEADS
    win = jnp.pad(w_in, ((0, 0), (0, MLA_IN_PAD - w_in.shape[1]))).astype(BF16)
    wuq = jnp.pad(w_uq.reshape(MLA_Q_LORA, Hn, MLA_NOPE + MLA_ROPE),
                  ((0, 0), (0, 0), (0, MLA_QK_PAD - MLA_NOPE - MLA_ROPE)))
    wuq = wuq.reshape(MLA_Q_LORA, Hn * MLA_QK_PAD).astype(BF16)
    wkv = w_ukv.reshape(MLA_KV_LORA, Hn, MLA_NOPE + MLA_V)
    wukv = jnp.concatenate([wkv[:, :, :MLA_NOPE].reshape(MLA_KV_LORA, Hn * MLA_NOPE),
                            wkv[:, :, MLA_NOPE:].reshape(MLA_KV_LORA, Hn * MLA_V)], axis=1).astype(BF16)
    nb = lay.nb
    tab_spec = pl.BlockSpec((TL, LANES), lambda r: (r % nb, 0))
    bf = lambda w: jax.ShapeDtypeStruct((lay.rows, w), BF16)
    return pl.pallas_call(
        _mla_in_kernel,
        grid=(lay.n_blocks,),
        in_specs=[lay.row_spec(D), _const_spec((1, D)), lay.mod_spec(1), lay.mod_spec(0),
                  _const_spec(win.shape), _const_spec(wuq.shape), _const_spec(wukv.shape),
                  _const_spec((1, MLA_Q_LORA)), _const_spec((1, MLA_KV_LORA)),
                  _const_spec((1, LANES)), _const_spec((1, LANES)), _const_spec((1, LANES)),
                  _const_spec((1, LANES)), tab_spec, tab_spec, tab_spec],
        out_specs=[lay.row_spec(Hn * MLA_QK_PAD), lay.row_spec(Hn * MLA_QK_PAD), lay.row_spec(Hn * MLA_VW)],
        out_shape=[bf(Hn * MLA_QK_PAD), bf(Hn * MLA_QK_PAD), bf(Hn * MLA_VW)],
        compiler_params=_params(1),
        name="mla_in",
    )(x, gain, mod, mod, win, wuq, wukv, q_norm.reshape(1, -1), kv_norm.reshape(1, -1),
      (qn_nope * (MLA_SCALE * LOG2E)).reshape(1, -1), _pad_lanes(qn_rope * (MLA_SCALE * LOG2E)),
      kn_nope.reshape(1, -1), _pad_lanes(kn_rope), *tables)


def rope_tables(n_ctx, n_lat):
    n_freq = MLA_ROPE // 4
    inv = ROPE_THETA ** (-jnp.arange(n_freq, dtype=F32) / n_freq)
    t = jnp.arange(n_lat)
    a_r = (t // GRID_W).astype(F32)[:, None] * inv
    a_c = (t % GRID_W).astype(F32)[:, None] * inv
    ang = jnp.concatenate([a_r, a_r, a_c, a_c], axis=-1)
    ang = jnp.concatenate([jnp.zeros((n_ctx, MLA_ROPE), F32), ang], axis=0)
    cos, sin = jnp.cos(ang), jnp.sin(ang)
    low = (jnp.arange(MLA_ROPE) % (2 * ROPE_HALF)) < ROPE_HALF
    pad = lambda a: jnp.pad(a, ((0, 0), (0, LANES - MLA_ROPE)))
    return pad(cos), pad(jnp.where(low, -sin, 0.0)), pad(jnp.where(low, 0.0, sin))


MLA_HPS = 2


def _mla_attn_kernel(q_ref, k_ref, v_ref, o_ref, *, n_ctx):
    def attend(n_keys):
        heads = range(MLA_HPS)
        s, p = {}, {}
        for hd in heads:
            q = q_ref[0, :, hd * MLA_QK_PAD:(hd + 1) * MLA_QK_PAD]
            k = k_ref[0, :n_keys, hd * MLA_QK_PAD:(hd + 1) * MLA_QK_PAD]
            s[hd] = lax.dot_general(q, k, (((1,), (1,)), ((), ())), preferred_element_type=F32)
        for hd in heads:
            p[hd] = jnp.exp2(s[hd] - s[hd].max(axis=1, keepdims=True)).astype(BF16)
        for hd in heads:
            v = v_ref[0, :n_keys, hd * MLA_VW:(hd + 1) * MLA_VW]
            r = jnp.dot(p[hd], v, preferred_element_type=F32)
            o_ref[0, :, hd * MLA_V:(hd + 1) * MLA_V] = (
                r[:, :MLA_V] / r[:, MLA_V:MLA_V + 1]).astype(o_ref.dtype)

    @pl.when(pl.program_id(2) == 0)
    def _():
        attend(n_ctx)

    @pl.when(pl.program_id(2) > 0)
    def _():
        attend(k_ref.shape[1])


def mla_attention(lay, q, k, v):
    assert lay.ctx_first
    Bn, S = lay.n_batch, lay.nb * TL
    q3, k3, v3 = (a.reshape(Bn, S, a.shape[-1]) for a in (q, k, v))
    out = pl.pallas_call(
        functools.partial(_mla_attn_kernel, n_ctx=TL),
        grid=(Bn, MLA_HEADS // MLA_HPS, lay.nb),
        in_specs=[pl.BlockSpec((1, TL, MLA_HPS * MLA_QK_PAD), lambda b, h, i: (b, i, h)),
                  pl.BlockSpec((1, S, MLA_HPS * MLA_QK_PAD), lambda b, h, i: (b, 0, h)),
                  pl.BlockSpec((1, S, MLA_HPS * MLA_VW), lambda b, h, i: (b, 0, h))],
        out_specs=pl.BlockSpec((1, TL, MLA_HPS * MLA_V), lambda b, h, i: (b, i, h)),
        out_shape=jax.ShapeDtypeStruct((Bn, S, MLA_HEADS * MLA_V), BF16),
        compiler_params=_params(3),
        name="mla_attention",
    )(q3, k3, v3)
    return out.reshape(lay.rows, MLA_HEADS * MLA_V)


N_PROLOGUE = {"conv": 5, "mlstm": 4, "mla": 1}


def _mixer_out_kernel(*refs, kind, nb, ctx_first):
    n_pro = N_PROLOGUE[kind]
    pro = refs[:n_pro]
    (wout_ref, x_ref, ga_ref, gain_ref, sc_ref, sh_ref, wr_ref, br_ref,
     xo_ref, h2_ref, te_ref, gate_ref, rank_ref, cnt_ref, carry_ref) = refs[n_pro:]
    r = pl.program_id(0)

    if kind == "conv":
        vprev_ref, v_ref, vnext_ref, bg_ref, cw_ref = pro
        j = r % nb
        first = (j == 0) | (j == 1) if ctx_first else (j == 0)
        last = (j == nb - 1) | (j == 0) if ctx_first else (j == nb - 1)
        v = v_ref[...].astype(F32)
        rows = lax.broadcasted_iota(jnp.int32, (TL, 1), 0)
        prev_row = jnp.where(first, 0.0, vprev_ref[BF16_ROWS - 1:BF16_ROWS, :].astype(F32))
        next_row = jnp.where(last, 0.0, vnext_ref[0:1, :].astype(F32))
        up = jnp.where(rows == 0, prev_row, pltpu.roll(v, 1, 0))
        dn = jnp.where(rows == TL - 1, next_row, pltpu.roll(v, TL - 1, 0))
        cw = cw_ref[...]
        a = bg_ref[...].astype(F32) * (up * cw[0:1] + v * cw[1:2] + dn * cw[2:3])
    elif kind == "mlstm":
        hf_ref, hb_ref, og_ref, ng_ref = pro
        hh = hf_ref[...].astype(F32) + hb_ref[...].astype(F32)
        a = jnp.concatenate([_rms(hh[:, h * ML_DV:(h + 1) * ML_DV]) for h in range(ML_HEADS)], axis=1)
        a = a * ng_ref[...] * og_ref[...].astype(F32)
    else:
        a = pro[0][...]

    y = jnp.dot(a.astype(BF16), wout_ref[...], preferred_element_type=F32)
    xn = x_ref[...] + ga_ref[0] * y
    xo_ref[...] = xn
    h2 = _norm_mod(xn, gain_ref[...], sc_ref[0], sh_ref[0])
    h2_ref[...] = _pack_rows(h2)
    logits = _split_dot_t(wr_ref[...], h2) + br_ref[...]

    sub = lax.broadcasted_iota(jnp.int32, (N_EXPERTS, TL), 0)
    sub_k = lax.broadcasted_iota(jnp.int32, (TOP_K, TL), 0)
    work = logits
    sel = jnp.zeros((N_EXPERTS, TL), F32)
    top_e = jnp.zeros((TOP_K, TL), jnp.int32)
    top_v = jnp.zeros((TOP_K, TL), F32)
    picks = []
    for kk in range(TOP_K):
        m = work.max(axis=0, keepdims=True)
        idx = jnp.min(jnp.where(work == m, sub, N_EXPERTS), axis=0, keepdims=True)
        hit = sub == idx
        picks.append(hit)
        sel = jnp.where(hit, 1.0, sel)
        work = jnp.where(hit, -jnp.inf, work)
        top_e = jnp.where(sub_k == kk, idx, top_e)
        top_v = jnp.where(sub_k == kk, m, top_v)
    ex = jnp.exp(top_v - top_v[0:1])
    gate_ref[...] = ex / ex.sum(axis=0, keepdims=True)
    te_ref[...] = top_e

    @pl.when(r == 0)
    def _():
        carry_ref[...] = jnp.zeros_like(carry_ref)

    tr = lax.broadcasted_iota(jnp.int32, (TL, TL), 0)
    tc = lax.broadcasted_iota(jnp.int32, (TL, TL), 1)
    before = jnp.dot(sel.astype(BF16), (tr < tc).astype(BF16), preferred_element_type=F32)
    pos = before + carry_ref[...]
    rank = jnp.zeros((TOP_K, TL), F32)
    for kk in range(TOP_K):
        rk = jnp.sum(jnp.where(picks[kk], pos, 0.0), axis=0, keepdims=True)
        rank = jnp.where(sub_k == kk, rk, rank)
    rank_ref[...] = rank.astype(jnp.int32)
    total = carry_ref[...] + jnp.sum(sel, axis=1, keepdims=True)
    carry_ref[...] = total
    cnt_ref[...] = total


def mixer_out(lay, kind, pro_args, w_out, x, mod, gain_f, w_r, b_r):
    D = D_MODEL
    nb = lay.nb
    if kind == "conv":
        v, bg, cw = pro_args
        per = TL // BF16_ROWS
        last_tile = lay.rows // BF16_ROWS - 1
        pro_specs = [pl.BlockSpec((BF16_ROWS, D), lambda r: (jnp.maximum(r * per - 1, 0), 0)),
                     lay.row_spec(D),
                     pl.BlockSpec((BF16_ROWS, D), lambda r: (jnp.minimum((r + 1) * per, last_tile), 0)),
                     lay.row_spec(D), _const_spec((CONV_WIDTH, D))]
        pro_in = [v, v, v, bg, cw]
    elif kind == "mlstm":
        h_f, h_b, og, ng = pro_args
        pro_specs = [lay.row_spec(ML_V), lay.row_spec(ML_V), lay.row_spec(ML_V), _const_spec((1, ML_V))]
        pro_in = [h_f, h_b, og, ng.reshape(1, ML_V)]
    else:
        pro_specs = [lay.row_spec(D)]
        pro_in = list(pro_args)
    k_in = w_out.shape[0]
    small = lambda dt: jax.ShapeDtypeStruct((TOP_K, lay.rows), dt)
    small_spec = pl.BlockSpec((TOP_K, TL), lambda r: (0, r))
    return pl.pallas_call(
        functools.partial(_mixer_out_kernel, kind=kind, nb=nb, ctx_first=lay.ctx_first),
        grid=(lay.n_blocks,),
        in_specs=pro_specs + [_const_spec((k_in, D)), lay.row_spec(D), lay.mod_spec(2),
                              _const_spec((1, D)), lay.mod_spec(4), lay.mod_spec(3),
                              _const_spec((2 * N_EXPERTS, D)), _const_spec((N_EXPERTS, 1))],
        out_specs=[lay.row_spec(D), lay.row_spec(PACK_W), small_spec, small_spec, small_spec,
                   _const_spec((N_EXPERTS, 1))],
        out_shape=[jax.ShapeDtypeStruct((lay.rows, D), F32),
                   jax.ShapeDtypeStruct((lay.rows, PACK_W), jnp.int32),
                   small(jnp.int32), small(F32), small(jnp.int32),
                   jax.ShapeDtypeStruct((N_EXPERTS, 1), F32)],
        scratch_shapes=[pltpu.VMEM((N_EXPERTS, 1), F32)],
        compiler_params=_params(1),
        name="mixer_out_" + kind,
    )(*pro_in, w_out.astype(BF16), x, mod, gain_f, mod, mod, _split_weight_t(w_r),
      b_r.reshape(N_EXPERTS, 1))


def _expert_ffn_kernel(blk_e_ref, first_ref, valid_ref, x_ref, w1_ref, b1_ref, w2_ref, b2_ref, o_ref,
                       w1b_ref, w2b_ref):
    del blk_e_ref
    i = pl.program_id(0)

    @pl.when(first_ref[i] == 1)
    def _():
        w1b_ref[...] = w1_ref[0, 0].astype(BF16)
        w2b_ref[...] = w2_ref[0, 0].astype(BF16)

    @pl.when(valid_ref[i] > 0)
    def _():
        x = _unpack_rows(x_ref[...], BF16)
        h = jnp.dot(x, w1b_ref[...], preferred_element_type=F32) + b1_ref[0, 0]
        glu = jnp.minimum(h[:, :MOE_FF], SWIGLU_LIMIT)
        lin = jnp.clip(h[:, MOE_FF:], -SWIGLU_LIMIT, SWIGLU_LIMIT)
        act = glu * jax.nn.sigmoid(SWIGLU_ALPHA * glu) * (lin + 1.0)
        y = jnp.dot(act.astype(BF16), w2b_ref[...], preferred_element_type=F32)
        o_ref[...] = _pack_rows(y + b2_ref[0, 0])

    @pl.when(valid_ref[i] == 0)
    def _():
        o_ref[...] = jnp.zeros_like(o_ref)


def expert_ffn(layer, xp, blk_e, blk_first, blk_valid, w1, b1, w2, b2):
    n_rows = xp.shape[0]
    D, F2 = D_MODEL, 2 * MOE_FF
    n_blk = n_rows // MOE_BLOCK
    grid_spec = pltpu.PrefetchScalarGridSpec(
        num_scalar_prefetch=3,
        grid=(n_blk,),
        in_specs=[
            pl.BlockSpec((MOE_BLOCK, PACK_W), lambda i, be, fi, nu: (i, 0)),
            pl.BlockSpec((1, 1, D, F2), lambda i, be, fi, nu: (layer, be[i], 0, 0)),
            pl.BlockSpec((1, 1, 1, F2), lambda i, be, fi, nu: (layer, be[i], 0, 0)),
            pl.BlockSpec((1, 1, MOE_FF, D), lambda i, be, fi, nu: (layer, be[i], 0, 0)),
            pl.BlockSpec((1, 1, 1, D), lambda i, be, fi, nu: (layer, be[i], 0, 0)),
        ],
        out_specs=pl.BlockSpec((MOE_BLOCK, PACK_W), lambda i, be, fi, nu: (i, 0)),
        scratch_shapes=[pltpu.VMEM((D, F2), BF16), pltpu.VMEM((MOE_FF, D), BF16)],
    )
    return pl.pallas_call(
        _expert_ffn_kernel,
        grid_spec=grid_spec,
        out_shape=jax.ShapeDtypeStruct((n_rows, PACK_W), jnp.int32),
        compiler_params=_params(1),
        name="expert_ffn",
    )(blk_e, blk_first, blk_valid, xp, w1, b1, w2, b2)


SC_CORES = 2
SC_SUBCORES = 16
SC_CHUNKS = (64, 32)


def _sc_chunk(*counts):
    n_workers = SC_CORES * SC_SUBCORES
    for chunk in SC_CHUNKS:
        if all(n % (chunk * n_workers) == 0 for n in counts):
            return chunk
    raise ValueError(f"row counts {counts} do not split over {n_workers} subcores")


def sc_gather(table, idx):
    n_idx = idx.shape[0]
    width = table.shape[1]
    n_workers = SC_CORES * SC_SUBCORES
    per_worker = n_idx // n_workers
    chunk = _sc_chunk(n_idx)
    n_chunks = per_worker // chunk
    assert n_chunks * chunk * n_workers == n_idx and n_chunks % 2 == 0
    mesh = plsc.VectorSubcoreMesh(core_axis_name="c", subcore_axis_name="s",
                                  num_cores=SC_CORES, num_subcores=SC_SUBCORES)

    def body(table_hbm, idx_hbm, out_hbm, idx_v, rows_v, gsem, wsem):
        wid = lax.axis_index("s") * SC_CORES + lax.axis_index("c")
        pltpu.sync_copy(idx_hbm.at[wid], idx_v)

        def gather(ci, slot):
            return pltpu.make_async_copy(table_hbm.at[idx_v.at[ci]], rows_v.at[slot], gsem.at[slot])

        def write(ci, slot):
            return pltpu.make_async_copy(rows_v.at[slot], out_hbm.at[ci, wid], wsem.at[slot])

        gather(0, 0).start()

        @pl.loop(0, n_chunks, step=2)
        def _(c0):
            for slot in range(2):
                ci = c0 + slot
                other = 1 - slot

                @pl.when(ci + 1 < n_chunks)
                def _():
                    @pl.when(ci >= 1)
                    def _():
                        write(ci - 1, other).wait()
                    gather(ci + 1, other).start()

                gather(ci, slot).wait()
                write(ci, slot).start()

        write(n_chunks - 2, 0).wait()
        write(n_chunks - 1, 1).wait()

    out = pl.kernel(
        body,
        out_type=jax.ShapeDtypeStruct((n_chunks, n_workers, chunk, width), table.dtype),
        mesh=mesh,
        scratch_types=[pltpu.VMEM((n_chunks, chunk), jnp.int32),
                       pltpu.VMEM((2, chunk, width), table.dtype),
                       pltpu.SemaphoreType.DMA((2,)),
                       pltpu.SemaphoreType.DMA((2,))],
        name="sc_gather",
    )(table, idx.reshape(n_chunks, n_workers, chunk).transpose(1, 0, 2))
    return out.reshape(n_idx, width)


def sc_dispatch(table, dest, pad_rows):
    n_tok, width = table.shape
    n_picks = dest.shape[0]
    n_pad = pad_rows.shape[0]
    n_workers = SC_CORES * SC_SUBCORES
    chunk = _sc_chunk(n_tok, n_pad)
    per_w = n_tok // chunk // n_workers
    pad_w = n_pad // chunk // n_workers
    assert per_w * chunk * n_workers == n_tok and pad_w * chunk * n_workers == n_pad
    mesh = plsc.VectorSubcoreMesh(core_axis_name="c", subcore_axis_name="s",
                                  num_cores=SC_CORES, num_subcores=SC_SUBCORES)

    def body(table_hbm, idx_hbm, pad_hbm, zero_hbm, out_hbm, idx_v, pad_v, rows_v, zero_v,
             rsem, ssem, zsem):
        wid = lax.axis_index("s") * SC_CORES + lax.axis_index("c")
        pltpu.sync_copy(idx_hbm.at[wid], idx_v)
        pltpu.sync_copy(pad_hbm.at[wid], pad_v)
        pltpu.sync_copy(zero_hbm, zero_v)

        def zero_fill(pc):
            return pltpu.make_async_copy(zero_v, out_hbm.at[pad_v.at[pc]], zsem)

        for pc in range(pad_w):
            zero_fill(pc).start()

        def scatter(ci, kk):
            return pltpu.make_async_copy(rows_v, out_hbm.at[idx_v.at[ci * n_picks + kk]], ssem)

        @pl.loop(0, per_w)
        def _(ci):
            pltpu.async_copy(table_hbm.at[ci, wid], rows_v, rsem).wait()
            for kk in range(n_picks):
                scatter(ci, kk).start()
            for kk in range(n_picks):
                scatter(ci, kk).wait()

        for pc in range(pad_w):
            zero_fill(pc).wait()

    idx = dest.reshape(n_picks, per_w, n_workers, chunk).transpose(2, 1, 0, 3)
    idx = idx.reshape(n_workers, per_w * n_picks, chunk)
    return pl.kernel(
        body,
        out_type=jax.ShapeDtypeStruct((n_tok * n_picks + n_pad, width), table.dtype),
        mesh=mesh,
        scratch_types=[pltpu.VMEM((per_w * n_picks, chunk), jnp.int32),
                       pltpu.VMEM((pad_w, chunk), jnp.int32),
                       pltpu.VMEM((chunk, width), table.dtype),
                       pltpu.VMEM((chunk, width), table.dtype),
                       pltpu.SemaphoreType.DMA, pltpu.SemaphoreType.DMA, pltpu.SemaphoreType.DMA],
        name="sc_dispatch",
    )(table.reshape(per_w, n_workers, chunk, width), idx,
      pad_rows.reshape(n_workers, pad_w, chunk), jnp.zeros((chunk, width), table.dtype))


def _combine_kernel(x_ref, *refs):
    y_refs, (gate_ref, gf_ref, o_ref) = refs[:TOP_K], refs[TOP_K:]
    gates = gate_ref[...]
    acc = gates[:, 0:1] * _unpack_rows(y_refs[0][...], F32)
    for kk in range(1, TOP_K):
        acc = acc + gates[:, kk:kk + 1] * _unpack_rows(y_refs[kk][...], F32)
    o_ref[...] = x_ref[...] + gf_ref[0] * acc


def moe_combine(lay, x, yg, gates, mod, drop_ctx):
    D = D_MODEL
    if drop_ctx:
        nbo = lay.nb - 1
        src = lambda r: (r // nbo) * lay.nb + 1 + r % nbo
        n_out = lay.n_batch * nbo
    else:
        src = lambda r: r
        n_out = lay.n_blocks
    y_specs = [pl.BlockSpec((TL, PACK_W), functools.partial(lambda kk, r: (kk * lay.n_blocks + src(r), 0), kk))
               for kk in range(TOP_K)]
    return pl.pallas_call(
        _combine_kernel,
        grid=(n_out,),
        in_specs=[pl.BlockSpec((TL, D), lambda r: (src(r), 0))] + y_specs + [
            pl.BlockSpec((TL, TOP_K), lambda r: (src(r), 0)),
            pl.BlockSpec((1, 1, D), lambda r: (lay.mod_row(src(r)), 0, 5))],
        out_specs=pl.BlockSpec((TL, D), lambda r: (r, 0)),
        out_shape=jax.ShapeDtypeStruct((n_out * TL, D), F32),
        compiler_params=_params(1),
        name="moe_combine",
    )(x, yg, yg, yg, yg, gates, mod)


def moe_route(top_e, rank, counts):
    T = top_e.shape[1]
    assert (T * TOP_K) % MOE_BLOCK == 0
    counts = counts.reshape(N_EXPERTS).astype(jnp.int32)
    padded = (counts + MOE_BLOCK - 1) // MOE_BLOCK * MOE_BLOCK
    padded_end = jnp.cumsum(padded)
    padded_start = padded_end - padded
    experts = jnp.arange(N_EXPERTS)
    start_of = jnp.sum(jnp.where(top_e[..., None] == experts, padded_start, 0), axis=-1)
    dest = (start_of + rank).astype(jnp.int32)
    n_pad = N_EXPERTS * MOE_BLOCK
    n_rows = T * TOP_K + n_pad
    n_blk = n_rows // MOE_BLOCK
    blk_start = jnp.arange(n_blk) * MOE_BLOCK
    blk_e = jnp.minimum(jnp.sum(padded_end[None, :] <= blk_start[:, None], axis=1), N_EXPERTS - 1)
    blk_e = blk_e.astype(jnp.int32)
    blk_first = jnp.concatenate([jnp.ones((1,), jnp.int32), (blk_e[1:] != blk_e[:-1]).astype(jnp.int32)])
    blk_hot = blk_e[:, None] == experts
    in_grp = blk_start - jnp.sum(jnp.where(blk_hot, padded_start, 0), axis=-1)
    blk_valid = jnp.clip(jnp.sum(jnp.where(blk_hot, counts, 0), axis=-1) - in_grp, 0, MOE_BLOCK)
    blk_valid = jnp.where(blk_start < padded_end[-1], blk_valid, 0).astype(jnp.int32)
    tail = padded - counts
    tail_end = jnp.cumsum(tail)
    j = jnp.arange(n_pad)
    owner = jnp.sum(tail_end[None, :] <= j[:, None], axis=1)
    base = padded_start + counts - (tail_end - tail)
    in_group = jnp.sum(jnp.where(owner[:, None] == experts, base, 0), axis=-1) + j
    pad_rows = jnp.where(j < tail_end[-1], in_group, padded_end[-1] + j - tail_end[-1])
    return dest, pad_rows.astype(jnp.int32), blk_e, blk_first, blk_valid


def kernel(x, c, ctx, c_ctx, norm_mix, norm_ffn, w_mod, b_mod, conv_w_in, conv_w, conv_w_out, ml_w_in, ml_b_gate, ml_norm, ml_w_out, mla_w_in, mla_q_norm, mla_kv_norm, mla_w_uq, mla_w_ukv, mla_qn_nope, mla_qn_rope, mla_kn_nope, mla_kn_rope, mla_w_out, moe_w_router, moe_b_router, moe_w1, moe_b1, moe_w2, moe_b2):
    Bn, n_lat, D = x.shape
    n_ctx = ctx.shape[1]
    assert D == D_MODEL and n_ctx == TL and n_lat % TL == 0
    assert (DEPTH - 1) % N_MIXERS == 0
    full = Layout(Bn, (n_ctx + n_lat) // TL, True)
    lat_only = Layout(Bn, n_lat // TL, False)
    mods = ada_all(c, c_ctx, w_mod, b_mod)
    tables = rope_tables(n_ctx, n_lat)
    b1_all = moe_b1.reshape(DEPTH, N_EXPERTS, 1, 2 * MOE_FF)
    b2_all = moe_b2.reshape(DEPTH, N_EXPERTS, 1, D)
    X = jnp.concatenate([ctx, x], axis=1).reshape(full.rows, D)
    for layer in range(DEPTH):
        kind, j = layer % N_MIXERS, layer // N_MIXERS
        lay = lat_only if layer == DEPTH - 1 else full
        mod = mods[layer]
        gain_a = norm_mix[layer].reshape(1, D)
        gain_f = norm_ffn[layer].reshape(1, D)
        if kind == 0:
            bg, v = conv_in(lay, X, gain_a, mod, conv_w_in[j].astype(BF16))
            pro, w_out, name = (v, bg, conv_w[j]), conv_w_out[j], "conv"
        elif kind == 1:
            w = ml_w_in[j]
            n_main = 2 * ML_QK + 2 * ML_V
            w_main = jnp.concatenate([w[:, :ML_QK] * ML_DQK ** -0.5, w[:, ML_QK:n_main]], axis=1)
            q, k, kt, v, og, g, gt = mlstm_in(lay, X, gain_a, mod, w_main.astype(BF16), w[:, n_main:],
                                              ml_b_gate[j])
            h_f, h_b = mlstm_scan(lay, q, k, kt, v, g, gt)
            pro, w_out, name = (h_f, h_b, og, ml_norm[j]), ml_w_out[j], "mlstm"
        else:
            q, k, v = mla_in(lay, X, gain_a, mod, mla_w_in[j], mla_q_norm[j], mla_kv_norm[j],
                             mla_w_uq[j], mla_w_ukv[j], mla_qn_nope[j], mla_qn_rope[j],
                             mla_kn_nope[j], mla_kn_rope[j], tables)
            pro, w_out, name = (mla_attention(lay, q, k, v),), mla_w_out[j], "mla"
        X, h2, top_e, gates, rank, counts = mixer_out(
            lay, name, pro, w_out, X, mod, gain_f, moe_w_router[layer], moe_b_router[layer])
        dest, pad_rows, blk_e, blk_first, blk_valid = moe_route(top_e, rank, counts)
        xp = sc_dispatch(h2, dest, pad_rows)
        yp = expert_ffn(layer, xp, blk_e, blk_first, blk_valid, moe_w1, b1_all, moe_w2, b2_all)
        yg = sc_gather(yp, dest.reshape(-1))
        X = moe_combine(lay, X, yg, gates.T, mod, drop_ctx=(layer == DEPTH - 2))
    return X.reshape(Bn, n_lat, D)
```

```python
import functools

import jax
import jax.numpy as jnp
from jax import lax
from jax.experimental import pallas as pl
from jax.experimental.pallas import tpu as pltpu
from jax.experimental.pallas import tpu_sc as plsc

D_MODEL = 1024
DEPTH = 4
GRID_W = 64
N_MIXERS = 3
N_ADA = 6
RMS_EPS = 1e-6
CONV_WIDTH = 3
ML_HEADS = 8
ML_DQK = 64
ML_DV = 128
ML_QK = ML_HEADS * ML_DQK
ML_V = ML_HEADS * ML_DV
GATE_CAP = 15.0
MLA_HEADS = 8
MLA_NOPE = 128
MLA_ROPE = 64
MLA_V = 128
MLA_Q_LORA = 384
MLA_KV_LORA = 256
MLA_SCALE = (MLA_NOPE + MLA_ROPE) ** -0.5
ROPE_THETA = 10000.0
N_EXPERTS = 32
TOP_K = 4
MOE_FF = D_MODEL
SWIGLU_ALPHA = 1.702
SWIGLU_LIMIT = 7.0
MOE_BLOCK = 512

TL = 256
LANES = 128
BF16_ROWS = 16
VMEM_LIMIT = 48 * 1024 * 1024
HI = lax.Precision.HIGHEST
F32 = jnp.float32
BF16 = jnp.bfloat16


def _params(n_axes):
    return pltpu.CompilerParams(dimension_semantics=("arbitrary",) * n_axes,
                                vmem_limit_bytes=VMEM_LIMIT)


def _rms(x, width=None):
    width = x.shape[-1] if width is None else width
    return x * lax.rsqrt(jnp.sum(x * x, axis=-1, keepdims=True) * (1.0 / width) + RMS_EPS)


def _norm_mod(x, gain, scale, shift):
    return _rms(x) * (gain * (1.0 + scale)) + shift


def _split_weight_t(w):
    hi = w.astype(BF16)
    lo = (w - hi.astype(F32)).astype(BF16)
    return jnp.concatenate([hi.T, lo.T], axis=0)


def _split_dot_t(w2, h):
    n = w2.shape[0] // 2
    both = lax.dot_general(w2, h.astype(BF16), (((1,), (1,)), ((), ())), preferred_element_type=F32)
    return both[:n] + both[n:]


PACK_W = D_MODEL // 2
HIGH_HALF = -65536


def _pack_rows(x):
    xb = x.astype(BF16).astype(F32)
    lo = lax.bitcast_convert_type(xb[:, :PACK_W], jnp.int32)
    hi = lax.bitcast_convert_type(xb[:, PACK_W:], jnp.int32)
    return hi | lax.shift_right_logical(lo, 16)


def _unpack_rows(w, dtype):
    lo = lax.bitcast_convert_type(lax.shift_left(w, 16), F32)
    hi = lax.bitcast_convert_type(w & HIGH_HALF, F32)
    return jnp.concatenate([lo.astype(dtype), hi.astype(dtype)], axis=1)


class Layout:
    def __init__(self, n_batch, nb, ctx_first):
        self.n_batch, self.nb, self.ctx_first = n_batch, nb, ctx_first
        self.n_blocks = n_batch * nb
        self.rows = self.n_blocks * TL

    def mod_row(self, r):
        b = r // self.nb
        return jnp.where(r % self.nb == 0, self.n_batch, b) if self.ctx_first else b

    def row_spec(self, width):
        return pl.BlockSpec((TL, width), lambda r: (r, 0))

    def mod_spec(self, piece):
        return pl.BlockSpec((1, 1, D_MODEL), lambda r: (self.mod_row(r), 0, piece))


def _const_spec(shape):
    return pl.BlockSpec(shape, lambda *_: (0,) * len(shape))


ADA_ROWS = 16
ADA_TN = 1536


def _ada_kernel(c_ref, w_ref, b_ref, o_ref):
    c = c_ref[...]
    s = c * jax.nn.sigmoid(c)
    o_ref[0] = jnp.dot(s, w_ref[0], precision=HI, preferred_element_type=F32) + b_ref[0]


def ada_all(c, c_ctx, w_mod, b_mod):
    Bn, D = c.shape
    assert Bn + 1 <= ADA_ROWS
    cond = jnp.zeros((ADA_ROWS, D), F32).at[:Bn].set(c).at[Bn].set(c_ctx)
    out = pl.pallas_call(
        _ada_kernel,
        grid=(DEPTH, N_ADA * D // ADA_TN),
        in_specs=[pl.BlockSpec((ADA_ROWS, D), lambda l, n: (0, 0)),
                  pl.BlockSpec((1, D, ADA_TN), lambda l, n: (l, 0, n)),
                  pl.BlockSpec((1, 1, ADA_TN), lambda l, n: (l, 0, n))],
        out_specs=pl.BlockSpec((1, ADA_ROWS, ADA_TN), lambda l, n: (l, 0, n)),
        out_shape=jax.ShapeDtypeStruct((DEPTH, ADA_ROWS, N_ADA * D), F32),
        compiler_params=_params(2),
        name="ada_mod",
    )(cond, w_mod, b_mod.reshape(DEPTH, 1, N_ADA * D))
    return out[:, :Bn + 1, None, :]


def _conv_in_kernel(x_ref, gain_ref, sc_ref, sh_ref, w_ref, bg_ref, v_ref):
    D = D_MODEL
    h = _norm_mod(x_ref[...], gain_ref[...], sc_ref[0], sh_ref[0]).astype(BF16)
    p = jnp.dot(h, w_ref[...], preferred_element_type=F32)
    bg_ref[...] = p[:, :D].astype(BF16)
    v_ref[...] = (p[:, D:2 * D] * p[:, 2 * D:]).astype(BF16)


def conv_in(lay, x, gain, mod, w_in):
    D = D_MODEL
    sds = jax.ShapeDtypeStruct((lay.rows, D), BF16)
    return pl.pallas_call(
        _conv_in_kernel,
        grid=(lay.n_blocks,),
        in_specs=[lay.row_spec(D), _const_spec((1, D)), lay.mod_spec(1), lay.mod_spec(0),
                  _const_spec((D, 3 * D))],
        out_specs=[lay.row_spec(D), lay.row_spec(D)],
        out_shape=[sds, sds],
        compiler_params=_params(1),
        name="conv_in",
    )(x, gain, mod, mod, w_in)


ML_T = 256
ML_SW = 2 * ML_DV
ML_NG = 4 * ML_HEADS


LOG2E = 1.4426950408889634


def _gate_act(g, is_forget):
    g = GATE_CAP * jnp.tanh(g * (1.0 / GATE_CAP))
    log_sig = jnp.minimum(g, 0.0) - jnp.log(1.0 + jnp.exp(-jnp.abs(g)))
    return jnp.where(is_forget, log_sig, g) * LOG2E


def _mlstm_in_kernel(x_ref, gain_ref, sc_ref, sh_ref, w_ref, wt_ref, bgt_ref,
                     q_ref, k_ref, kt_ref, v_ref, og_ref, g_ref, gt_ref):
    h = _norm_mod(x_ref[...], gain_ref[...], sc_ref[0], sh_ref[0])
    hb = h.astype(BF16)
    p = jnp.dot(hb, w_ref[...], preferred_element_type=F32)
    q_ref[...] = p[:, :ML_QK].astype(BF16)
    k_ref[...] = p[:, ML_QK:2 * ML_QK].astype(BF16)
    v_ref[...] = p[:, 2 * ML_QK:2 * ML_QK + ML_V].astype(BF16)
    og_ref[...] = jax.nn.sigmoid(p[:, 2 * ML_QK + ML_V:]).astype(BF16)
    dn = (((1,), (1,)), ((), ()))
    t = lax.dot_general(wt_ref[...], hb, dn, preferred_element_type=F32)
    kt_ref[...] = t[:ML_QK].astype(BF16)
    h_lo = (h - hb.astype(F32)).astype(BF16)
    gt = (t[ML_QK:ML_QK + ML_NG] + t[ML_QK + ML_NG:]
          + lax.dot_general(wt_ref[ML_QK:ML_QK + ML_NG, :], h_lo, dn, preferred_element_type=F32)
          + bgt_ref[...])
    row = lax.broadcasted_iota(jnp.int32, gt.shape, 0)
    gt = _gate_act(gt, (row // ML_HEADS) % 2 == 1)
    gt_ref[...] = gt
    eye = (lax.broadcasted_iota(jnp.int32, (TL, TL), 0)
           == lax.broadcasted_iota(jnp.int32, (TL, TL), 1)).astype(F32)
    g_ref[...] = lax.dot_general(eye, gt, (((1,), (1,)), ((), ())), precision=HI,
                                 preferred_element_type=F32)


def mlstm_in(lay, x, gain, mod, w_main, w_g, b_g):
    D = D_MODEL
    n_main = 2 * ML_QK + 2 * ML_V
    bf = lambda w: jax.ShapeDtypeStruct((lay.rows, w), BF16)
    return pl.pallas_call(
        _mlstm_in_kernel,
        grid=(lay.n_blocks,),
        in_specs=[lay.row_spec(D), _const_spec((1, D)), lay.mod_spec(1), lay.mod_spec(0),
                  _const_spec((D, n_main)), _const_spec((ML_QK + 2 * ML_NG, D)), _const_spec((ML_NG, 1))],
        out_specs=[lay.row_spec(ML_QK), lay.row_spec(ML_QK), pl.BlockSpec((ML_QK, TL), lambda r: (0, r)),
                   lay.row_spec(ML_V), lay.row_spec(ML_V),
                   lay.row_spec(ML_NG), pl.BlockSpec((ML_NG, TL), lambda r: (0, r))],
        out_shape=[bf(ML_QK), bf(ML_QK), jax.ShapeDtypeStruct((ML_QK, lay.rows), BF16),
                   bf(ML_V), bf(ML_V),
                   jax.ShapeDtypeStruct((lay.rows, ML_NG), F32),
                   jax.ShapeDtypeStruct((ML_NG, lay.rows), F32)],
        compiler_params=_params(1),
        name="mlstm_in",
    )(x, gain, mod, mod, w_main,
      jnp.concatenate([w_main[:, ML_QK:2 * ML_QK].T, _split_weight_t(w_g)], axis=0),
      b_g.reshape(ML_NG, 1))


def _split3(x):
    hi = x.astype(BF16)
    r1 = x - hi.astype(F32)
    mid = r1.astype(BF16)
    lo = (r1 - mid.astype(F32)).astype(BF16)
    return hi, mid, lo


def _dot_exact01(x, sel01, x_on_left):
    sel = sel01.astype(BF16)
    parts = [jnp.dot(p, sel, preferred_element_type=F32) if x_on_left
             else jnp.dot(sel, p, preferred_element_type=F32) for p in _split3(x)]
    return parts[0] + parts[1] + parts[2]


def _mlstm_dir(reverse, q_ref, k_ref, kt_ref, v_ref, g_ref, gt_ref, o_ref, s_ref, m_ref):
    T = ML_T
    row = lax.broadcasted_iota(jnp.int32, (T, T), 0)
    col = lax.broadcasted_iota(jnp.int32, (T, T), 1)
    mask = (col >= row) if reverse else (col <= row)
    gt = gt_ref[...]
    bc = _dot_exact01(g_ref[...], mask, x_on_left=False)
    br = _dot_exact01(gt, (row >= col) if reverse else (row <= col), x_on_left=True)
    gi, gf = (2 * ML_HEADS, 3 * ML_HEADS) if reverse else (0, ML_HEADS)
    lane = lax.broadcasted_iota(jnp.int32, (T, 2 * ML_DQK), 1)
    sub = lax.broadcasted_iota(jnp.int32, (2 * ML_DQK, T), 0)
    gate_row = lax.broadcasted_iota(jnp.int32, (ML_NG, ML_DV), 0)
    ones = jnp.ones((T, ML_DV), BF16)
    heads = range(ML_HEADS)
    qm, vx, s_raw = {}, {}, {}
    for h in heads:
        pair = (h // 2) * 2 * ML_DQK
        own = (lane >= ML_DQK) if (h % 2) else (lane < ML_DQK)
        qp = q_ref[:, pair:pair + 2 * ML_DQK]
        qm[h] = jnp.where(own, qp, jnp.zeros_like(qp))
        vx[h] = jnp.concatenate([v_ref[:, h * ML_DV:(h + 1) * ML_DV], ones], axis=1)
        s_raw[h] = lax.dot_general(qm[h], k_ref[:, pair:pair + 2 * ML_DQK], (((1,), (1,)), ((), ())),
                                   preferred_element_type=F32)
    p, a, u_rep, m_prev, s_prev = {}, {}, {}, {}, {}
    for h in heads:
        c_row = gt[gi + h:gi + h + 1, :] - br[gf + h:gf + h + 1, :]
        e = jnp.where(mask, c_row, -jnp.inf)
        m_prev[h] = m_ref[h][0:1, 0:1]
        u = jnp.maximum(m_prev[h], jnp.max(e, axis=1, keepdims=True))
        p[h] = (s_raw[h] * jnp.exp2(e - u)).astype(BF16)
        u_rep[h] = jnp.broadcast_to(u, (T, ML_DV))
        a[h] = jnp.exp2(m_prev[h] - u_rep[h])
        s_prev[h] = s_ref[h]
    for h in heads:
        r = jnp.dot(p[h], vx[h], preferred_element_type=F32)
        qs = jnp.dot(qm[h], s_prev[h].astype(BF16), preferred_element_type=F32)
        num = r[:, :ML_DV] + a[h] * qs[:, :ML_DV]
        den = r[:, ML_DV:] + a[h] * qs[:, ML_DV:]
        b_rep = _dot_exact01(bc, gate_row == gf + h, x_on_left=True)
        floor = jnp.exp2(-(b_rep + u_rep[h]))
        o_ref[:, h * ML_DV:(h + 1) * ML_DV] = (num / jnp.maximum(jnp.abs(den), floor)).astype(o_ref.dtype)
    for h in heads:
        pair = (h // 2) * 2 * ML_DQK
        b_row = br[gf + h:gf + h + 1, :]
        tot = b_row[:, 0:1] if reverse else b_row[:, T - 1:T]
        g_row = tot - b_row + gt[gi + h:gi + h + 1, :]
        m_new = jnp.maximum(tot + m_prev[h], jnp.max(g_row, axis=1, keepdims=True))
        decay = jnp.exp2(tot + m_prev[h] - m_new)
        wk = jnp.exp2(g_row - m_new)
        own_t = (sub >= ML_DQK) if (h % 2) else (sub < ML_DQK)
        kt = kt_ref[pair:pair + 2 * ML_DQK, :].astype(F32)
        kw = jnp.where(own_t, kt * wk, 0.0).astype(BF16)
        s_ref[h] = decay * s_prev[h] + jnp.dot(kw, vx[h], preferred_element_type=F32)
        m_ref[h] = jnp.broadcast_to(m_new, m_ref.shape[1:])


def _mlstm_scan_kernel(qf_ref, kf_ref, ktf_ref, vf_ref, gf_ref, gtf_ref,
                       qb_ref, kb_ref, ktb_ref, vb_ref, gb_ref, gtb_ref,
                       of_ref, ob_ref, sf_ref, mf_ref, sb_ref, mb_ref):
    @pl.when(pl.program_id(1) == 0)
    def _():
        sf_ref[...] = jnp.zeros_like(sf_ref)
        mf_ref[...] = jnp.zeros_like(mf_ref)
        sb_ref[...] = jnp.zeros_like(sb_ref)
        mb_ref[...] = jnp.zeros_like(mb_ref)

    _mlstm_dir(False, qf_ref, kf_ref, ktf_ref, vf_ref, gf_ref, gtf_ref, of_ref, sf_ref, mf_ref)
    _mlstm_dir(True, qb_ref, kb_ref, ktb_ref, vb_ref, gb_ref, gtb_ref, ob_ref, sb_ref, mb_ref)


def mlstm_scan(lay, q, k, kt, v, g, gt):
    assert lay.ctx_first
    per = TL // ML_T
    nb = lay.nb * per
    rev = lambda j: jnp.where(j < per, per - 1 - j, nb + per - 1 - j)
    fwd = lambda b, j: (b * nb + j, 0)
    bwd = lambda b, j: (b * nb + rev(j), 0)
    fwd_t = lambda b, j: (0, b * nb + j)
    bwd_t = lambda b, j: (0, b * nb + rev(j))

    def specs(im, imt):
        return [pl.BlockSpec((ML_T, ML_QK), im), pl.BlockSpec((ML_T, ML_QK), im),
                pl.BlockSpec((ML_QK, ML_T), imt), pl.BlockSpec((ML_T, ML_V), im),
                pl.BlockSpec((ML_T, ML_NG), im), pl.BlockSpec((ML_NG, ML_T), imt)]

    out_sds = jax.ShapeDtypeStruct((lay.rows, ML_V), BF16)
    state = [pltpu.VMEM((ML_HEADS, 2 * ML_DQK, ML_SW), F32),
             pltpu.VMEM((ML_HEADS, 8, LANES), F32)]
    return pl.pallas_call(
        _mlstm_scan_kernel,
        grid=(lay.n_batch, nb),
        in_specs=specs(fwd, fwd_t) + specs(bwd, bwd_t),
        out_specs=[pl.BlockSpec((ML_T, ML_V), fwd), pl.BlockSpec((ML_T, ML_V), bwd)],
        out_shape=[out_sds, out_sds],
        scratch_shapes=state + state,
        compiler_params=_params(2),
        name="mlstm_scan",
    )(q, k, kt, v, g, gt, q, k, kt, v, g, gt)


MLA_QK_PAD = 256
MLA_VW = 2 * MLA_V
MLA_IN_PAD = MLA_Q_LORA + MLA_KV_LORA + LANES
ROPE_HALF = MLA_ROPE // 4


def _mla_in_kernel(x_ref, gain_ref, sc_ref, sh_ref, win_ref, wuq_ref, wukv_ref, qn_ref, kvn_ref,
                   qnn_ref, qnr_ref, knn_ref, knr_ref, cos_ref, sa_ref, sb_ref,
                   q_out, k_out, v_out):
    h = _norm_mod(x_ref[...], gain_ref[...], sc_ref[0], sh_ref[0]).astype(BF16)
    p = jnp.dot(h, win_ref[...], preferred_element_type=F32)
    cq = _rms(p[:, :MLA_Q_LORA]) * qn_ref[...]
    ckv = _rms(p[:, MLA_Q_LORA:MLA_Q_LORA + MLA_KV_LORA]) * kvn_ref[...]
    kr = p[:, MLA_Q_LORA + MLA_KV_LORA:]
    q = jnp.dot(cq.astype(BF16), wuq_ref[...], preferred_element_type=F32)
    kv = jnp.dot(ckv.astype(BF16), wukv_ref[...], preferred_element_type=F32)
    nv = MLA_HEADS * MLA_NOPE
    ones_col = (lax.broadcasted_iota(jnp.int32, (TL, MLA_VW - MLA_V), 1) == 0).astype(BF16)
    cos, sa, sb = cos_ref[...], sa_ref[...], sb_ref[...]

    def rope(xp):
        return (xp * cos + pltpu.roll(xp, LANES - ROPE_HALF, 1) * sa + pltpu.roll(xp, ROPE_HALF, 1) * sb)

    heads = range(MLA_HEADS)
    kr = _rms(kr, MLA_ROPE) * knr_ref[...]
    qn = {hd: _rms(q[:, hd * MLA_QK_PAD:hd * MLA_QK_PAD + MLA_NOPE]) * qnn_ref[...] for hd in heads}
    qr = {hd: _rms(q[:, hd * MLA_QK_PAD + MLA_NOPE:(hd + 1) * MLA_QK_PAD], MLA_ROPE) * qnr_ref[...]
          for hd in heads}
    kn = {hd: _rms(kv[:, hd * MLA_NOPE:(hd + 1) * MLA_NOPE]) * knn_ref[...] for hd in heads}
    kr = rope(kr).astype(BF16)
    qr = {hd: rope(qr[hd]) for hd in heads}
    for hd in heads:
        c0 = hd * MLA_QK_PAD
        q_out[:, c0:c0 + MLA_NOPE] = qn[hd].astype(BF16)
        q_out[:, c0 + MLA_NOPE:c0 + MLA_QK_PAD] = qr[hd].astype(BF16)
        v0 = nv + hd * MLA_V
        v_out[:, hd * MLA_VW:hd * MLA_VW + MLA_V] = kv[:, v0:v0 + MLA_V].astype(BF16)
        v_out[:, hd * MLA_VW + MLA_V:(hd + 1) * MLA_VW] = ones_col
        k_out[:, c0:c0 + MLA_NOPE] = kn[hd].astype(BF16)
        k_out[:, c0 + MLA_NOPE:c0 + MLA_QK_PAD] = kr


def _pad_lanes(g):
    return jnp.pad(g, (0, LANES - g.shape[0])).reshape(1, LANES)


def mla_in(lay, x, gain, mod, w_in, q_norm, kv_norm, w_uq, w_ukv, qn_nope, qn_rope, kn_nope, kn_rope,
           tables):
    D = D_MODEL
    Hn = MLA_HEADS
    win = jnp.pad(w_in, ((0, 0), (0, MLA_IN_PAD - w_in.shape[1]))).astype(BF16)
    wuq = jnp.pad(w_uq.reshape(MLA_Q_LORA, Hn, MLA_NOPE + MLA_ROPE),
                  ((0, 0), (0, 0), (0, MLA_QK_PAD - MLA_NOPE - MLA_ROPE)))
    wuq = wuq.reshape(MLA_Q_LORA, Hn * MLA_QK_PAD).astype(BF16)
    wkv = w_ukv.reshape(MLA_KV_LORA, Hn, MLA_NOPE + MLA_V)
    wukv = jnp.concatenate([wkv[:, :, :MLA_NOPE].reshape(MLA_KV_LORA, Hn * MLA_NOPE),
                            wkv[:, :, MLA_NOPE:].reshape(MLA_KV_LORA, Hn * MLA_V)], axis=1).astype(BF16)
    nb = lay.nb
    tab_spec = pl.BlockSpec((TL, LANES), lambda r: (r % nb, 0))
    bf = lambda w: jax.ShapeDtypeStruct((lay.rows, w), BF16)
    return pl.pallas_call(
        _mla_in_kernel,
        grid=(lay.n_blocks,),
        in_specs=[lay.row_spec(D), _const_spec((1, D)), lay.mod_spec(1), lay.mod_spec(0),
                  _const_spec(win.shape), _const_spec(wuq.shape), _const_spec(wukv.shape),
                  _const_spec((1, MLA_Q_LORA)), _const_spec((1, MLA_KV_LORA)),
                  _const_spec((1, LANES)), _const_spec((1, LANES)), _const_spec((1, LANES)),
                  _const_spec((1, LANES)), tab_spec, tab_spec, tab_spec],
        out_specs=[lay.row_spec(Hn * MLA_QK_PAD), lay.row_spec(Hn * MLA_QK_PAD), lay.row_spec(Hn * MLA_VW)],
        out_shape=[bf(Hn * MLA_QK_PAD), bf(Hn * MLA_QK_PAD), bf(Hn * MLA_VW)],
        compiler_params=_params(1),
        name="mla_in",
    )(x, gain, mod, mod, win, wuq, wukv, q_norm.reshape(1, -1), kv_norm.reshape(1, -1),
      (qn_nope * (MLA_SCALE * LOG2E)).reshape(1, -1), _pad_lanes(qn_rope * (MLA_SCALE * LOG2E)),
      kn_nope.reshape(1, -1), _pad_lanes(kn_rope), *tables)


def rope_tables(n_ctx, n_lat):
    n_freq = MLA_ROPE // 4
    inv = ROPE_THETA ** (-jnp.arange(n_freq, dtype=F32) / n_freq)
    t = jnp.arange(n_lat)
    a_r = (t // GRID_W).astype(F32)[:, None] * inv
    a_c = (t % GRID_W).astype(F32)[:, None] * inv
    ang = jnp.concatenate([a_r, a_r, a_c, a_c], axis=-1)
    ang = jnp.concatenate([jnp.zeros((n_ctx, MLA_ROPE), F32), ang], axis=0)
    cos, sin = jnp.cos(ang), jnp.sin(ang)
    low = (jnp.arange(MLA_ROPE) % (2 * ROPE_HALF)) < ROPE_HALF
    pad = lambda a: jnp.pad(a, ((0, 0), (0, LANES - MLA_ROPE)))
    return pad(cos), pad(jnp.where(low, -sin, 0.0)), pad(jnp.where(low, 0.0, sin))


MLA_HPS = 2


def _mla_attn_kernel(q_ref, k_ref, v_ref, o_ref, *, n_ctx):
    def attend(n_keys):
        heads = range(MLA_HPS)
        s, p = {}, {}
        for hd in heads:
            q = q_ref[0, :, hd * MLA_QK_PAD:(hd + 1) * MLA_QK_PAD]
            k = k_ref[0, :n_keys, hd * MLA_QK_PAD:(hd + 1) * MLA_QK_PAD]
            s[hd] = lax.dot_general(q, k, (((1,), (1,)), ((), ())), preferred_element_type=F32)
        for hd in heads:
            p[hd] = jnp.exp2(s[hd] - s[hd].max(axis=1, keepdims=True)).astype(BF16)
        for hd in heads:
            v = v_ref[0, :n_keys, hd * MLA_VW:(hd + 1) * MLA_VW]
            r = jnp.dot(p[hd], v, preferred_element_type=F32)
            o_ref[0, :, hd * MLA_V:(hd + 1) * MLA_V] = (
                r[:, :MLA_V] / r[:, MLA_V:MLA_V + 1]).astype(o_ref.dtype)

    @pl.when(pl.program_id(2) == 0)
    def _():
        attend(n_ctx)

    @pl.when(pl.program_id(2) > 0)
    def _():
        attend(k_ref.shape[1])


def mla_attention(lay, q, k, v):
    assert lay.ctx_first
    Bn, S = lay.n_batch, lay.nb * TL
    q3, k3, v3 = (a.reshape(Bn, S, a.shape[-1]) for a in (q, k, v))
    out = pl.pallas_call(
        functools.partial(_mla_attn_kernel, n_ctx=TL),
        grid=(Bn, MLA_HEADS // MLA_HPS, lay.nb),
        in_specs=[pl.BlockSpec((1, TL, MLA_HPS * MLA_QK_PAD), lambda b, h, i: (b, i, h)),
                  pl.BlockSpec((1, S, MLA_HPS * MLA_QK_PAD), lambda b, h, i: (b, 0, h)),
                  pl.BlockSpec((1, S, MLA_HPS * MLA_VW), lambda b, h, i: (b, 0, h))],
        out_specs=pl.BlockSpec((1, TL, MLA_HPS * MLA_V), lambda b, h, i: (b, i, h)),
        out_shape=jax.ShapeDtypeStruct((Bn, S, MLA_HEADS * MLA_V), BF16),
        compiler_params=_params(3),
        name="mla_attention",
    )(q3, k3, v3)
    return out.reshape(lay.rows, MLA_HEADS * MLA_V)


N_PROLOGUE = {"conv": 5, "mlstm": 4, "mla": 1}


def _mixer_out_kernel(*refs, kind, nb, ctx_first):
    n_pro = N_PROLOGUE[kind]
    pro = refs[:n_pro]
    (wout_ref, x_ref, ga_ref, gain_ref, sc_ref, sh_ref, wr_ref, br_ref,
     xo_ref, h2_ref, te_ref, gate_ref, rank_ref, cnt_ref, carry_ref) = refs[n_pro:]
    r = pl.program_id(0)

    if kind == "conv":
        vprev_ref, v_ref, vnext_ref, bg_ref, cw_ref = pro
        j = r % nb
        first = (j == 0) | (j == 1) if ctx_first else (j == 0)
        last = (j == nb - 1) | (j == 0) if ctx_first else (j == nb - 1)
        v = v_ref[...].astype(F32)
        rows = lax.broadcasted_iota(jnp.int32, (TL, 1), 0)
        prev_row = jnp.where(first, 0.0, vprev_ref[BF16_ROWS - 1:BF16_ROWS, :].astype(F32))
        next_row = jnp.where(last, 0.0, vnext_ref[0:1, :].astype(F32))
        up = jnp.where(rows == 0, prev_row, pltpu.roll(v, 1, 0))
        dn = jnp.where(rows == TL - 1, next_row, pltpu.roll(v, TL - 1, 0))
        cw = cw_ref[...]
        a = bg_ref[...].astype(F32) * (up * cw[0:1] + v * cw[1:2] + dn * cw[2:3])
    elif kind == "mlstm":
        hf_ref, hb_ref, og_ref, ng_ref = pro
        hh = hf_ref[...].astype(F32) + hb_ref[...].astype(F32)
        a = jnp.concatenate([_rms(hh[:, h * ML_DV:(h + 1) * ML_DV]) for h in range(ML_HEADS)], axis=1)
        a = a * ng_ref[...] * og_ref[...].astype(F32)
    else:
        a = pro[0][...]

    y = jnp.dot(a.astype(BF16), wout_ref[...], preferred_element_type=F32)
    xn = x_ref[...] + ga_ref[0] * y
    xo_ref[...] = xn
    h2 = _norm_mod(xn, gain_ref[...], sc_ref[0], sh_ref[0])
    h2_ref[...] = _pack_rows(h2)
    logits = _split_dot_t(wr_ref[...], h2) + br_ref[...]

    sub = lax.broadcasted_iota(jnp.int32, (N_EXPERTS, TL), 0)
    sub_k = lax.broadcasted_iota(jnp.int32, (TOP_K, TL), 0)
    work = logits
    sel = jnp.zeros((N_EXPERTS, TL), F32)
    top_e = jnp.zeros((TOP_K, TL), jnp.int32)
    top_v = jnp.zeros((TOP_K, TL), F32)
    picks = []
    for kk in range(TOP_K):
        m = work.max(axis=0, keepdims=True)
        idx = jnp.min(jnp.where(work == m, sub, N_EXPERTS), axis=0, keepdims=True)
        hit = sub == idx
        picks.append(hit)
        sel = jnp.where(hit, 1.0, sel)
        work = jnp.where(hit, -jnp.inf, work)
        top_e = jnp.where(sub_k == kk, idx, top_e)
        top_v = jnp.where(sub_k == kk, m, top_v)
    ex = jnp.exp(top_v - top_v[0:1])
    gate_ref[...] = ex / ex.sum(axis=0, keepdims=True)
    te_ref[...] = top_e

    @pl.when(r == 0)
    def _():
        carry_ref[...] = jnp.zeros_like(carry_ref)

    tr = lax.broadcasted_iota(jnp.int32, (TL, TL), 0)
    tc = lax.broadcasted_iota(jnp.int32, (TL, TL), 1)
    before = jnp.dot(sel.astype(BF16), (tr < tc).astype(BF16), preferred_element_type=F32)
    pos = before + carry_ref[...]
    rank = jnp.zeros((TOP_K, TL), F32)
    for kk in range(TOP_K):
        rk = jnp.sum(jnp.where(picks[kk], pos, 0.0), axis=0, keepdims=True)
        rank = jnp.where(sub_k == kk, rk, rank)
    rank_ref[...] = rank.astype(jnp.int32)
    total = carry_ref[...] + jnp.sum(sel, axis=1, keepdims=True)
    carry_ref[...] = total
    cnt_ref[...] = total


def mixer_out(lay, kind, pro_args, w_out, x, mod, gain_f, w_r, b_r):
    D = D_MODEL
    nb = lay.nb
    if kind == "conv":
        v, bg, cw = pro_args
        per = TL // BF16_ROWS
        last_tile = lay.rows // BF16_ROWS - 1
        pro_specs = [pl.BlockSpec((BF16_ROWS, D), lambda r: (jnp.maximum(r * per - 1, 0), 0)),
                     lay.row_spec(D),
                     pl.BlockSpec((BF16_ROWS, D), lambda r: (jnp.minimum((r + 1) * per, last_tile), 0)),
                     lay.row_spec(D), _const_spec((CONV_WIDTH, D))]
        pro_in = [v, v, v, bg, cw]
    elif kind == "mlstm":
        h_f, h_b, og, ng = pro_args
        pro_specs = [lay.row_spec(ML_V), lay.row_spec(ML_V), lay.row_spec(ML_V), _const_spec((1, ML_V))]
        pro_in = [h_f, h_b, og, ng.reshape(1, ML_V)]
    else:
        pro_specs = [lay.row_spec(D)]
        pro_in = list(pro_args)
    k_in = w_out.shape[0]
    small = lambda dt: jax.ShapeDtypeStruct((TOP_K, lay.rows), dt)
    small_spec = pl.BlockSpec((TOP_K, TL), lambda r: (0, r))
    return pl.pallas_call(
        functools.partial(_mixer_out_kernel, kind=kind, nb=nb, ctx_first=lay.ctx_first),
        grid=(lay.n_blocks,),
        in_specs=pro_specs + [_const_spec((k_in, D)), lay.row_spec(D), lay.mod_spec(2),
                              _const_spec((1, D)), lay.mod_spec(4), lay.mod_spec(3),
                              _const_spec((2 * N_EXPERTS, D)), _const_spec((N_EXPERTS, 1))],
        out_specs=[lay.row_spec(D), lay.row_spec(PACK_W), small_spec, small_spec, small_spec,
                   _const_spec((N_EXPERTS, 1))],
        out_shape=[jax.ShapeDtypeStruct((lay.rows, D), F32),
                   jax.ShapeDtypeStruct((lay.rows, PACK_W), jnp.int32),
                   small(jnp.int32), small(F32), small(jnp.int32),
                   jax.ShapeDtypeStruct((N_EXPERTS, 1), F32)],
        scratch_shapes=[pltpu.VMEM((N_EXPERTS, 1), F32)],
        compiler_params=_params(1),
        name="mixer_out_" + kind,
    )(*pro_in, w_out.astype(BF16), x, mod, gain_f, mod, mod, _split_weight_t(w_r),
      b_r.reshape(N_EXPERTS, 1))


def _expert_ffn_kernel(blk_e_ref, first_ref, valid_ref, x_ref, w1_ref, b1_ref, w2_ref, b2_ref, o_ref,
                       w1b_ref, w2b_ref):
    del blk_e_ref
    i = pl.program_id(0)

    @pl.when(first_ref[i] == 1)
    def _():
        w1b_ref[...] = w1_ref[0, 0].astype(BF16)
        w2b_ref[...] = w2_ref[0, 0].astype(BF16)

    @pl.when(valid_ref[i] > 0)
    def _():
        x = _unpack_rows(x_ref[...], BF16)
        h = jnp.dot(x, w1b_ref[...], preferred_element_type=F32) + b1_ref[0, 0]
        glu = jnp.minimum(h[:, :MOE_FF], SWIGLU_LIMIT)
        lin = jnp.clip(h[:, MOE_FF:], -SWIGLU_LIMIT, SWIGLU_LIMIT)
        act = glu * jax.nn.sigmoid(SWIGLU_ALPHA * glu) * (lin + 1.0)
        y = jnp.dot(act.astype(BF16), w2b_ref[...], preferred_element_type=F32)
        o_ref[...] = _pack_rows(y + b2_ref[0, 0])

    @pl.when(valid_ref[i] == 0)
    def _():
        o_ref[...] = jnp.zeros_like(o_ref)


def expert_ffn(layer, xp, blk_e, blk_first, blk_valid, w1, b1, w2, b2):
    n_rows = xp.shape[0]
    D, F2 = D_MODEL, 2 * MOE_FF
    n_blk = n_rows // MOE_BLOCK
    grid_spec = pltpu.PrefetchScalarGridSpec(
        num_scalar_prefetch=3,
        grid=(n_blk,),
        in_specs=[
            pl.BlockSpec((MOE_BLOCK, PACK_W), lambda i, be, fi, nu: (i, 0)),
            pl.BlockSpec((1, 1, D, F2), lambda i, be, fi, nu: (layer, be[i], 0, 0)),
            pl.BlockSpec((1, 1, 1, F2), lambda i, be, fi, nu: (layer, be[i], 0, 0)),
            pl.BlockSpec((1, 1, MOE_FF, D), lambda i, be, fi, nu: (layer, be[i], 0, 0)),
            pl.BlockSpec((1, 1, 1, D), lambda i, be, fi, nu: (layer, be[i], 0, 0)),
        ],
        out_specs=pl.BlockSpec((MOE_BLOCK, PACK_W), lambda i, be, fi, nu: (i, 0)),
        scratch_shapes=[pltpu.VMEM((D, F2), BF16), pltpu.VMEM((MOE_FF, D), BF16)],
    )
    return pl.pallas_call(
        _expert_ffn_kernel,
        grid_spec=grid_spec,
        out_shape=jax.ShapeDtypeStruct((n_rows, PACK_W), jnp.int32),
        compiler_params=_params(1),
        name="expert_ffn",
    )(blk_e, blk_first, blk_valid, xp, w1, b1, w2, b2)


SC_CORES = 2
SC_SUBCORES = 16
SC_CHUNKS = (64, 32)


def _sc_chunk(*counts):
    n_workers = SC_CORES * SC_SUBCORES
    for chunk in SC_CHUNKS:
        if all(n % (chunk * n_workers) == 0 for n in counts):
            return chunk
    raise ValueError(f"row counts {counts} do not split over {n_workers} subcores")


def sc_gather(table, idx):
    n_idx = idx.shape[0]
    width = table.shape[1]
    n_workers = SC_CORES * SC_SUBCORES
    per_worker = n_idx // n_workers
    chunk = _sc_chunk(n_idx)
    n_chunks = per_worker // chunk
    assert n_chunks * chunk * n_workers == n_idx and n_chunks % 2 == 0
    mesh = plsc.VectorSubcoreMesh(core_axis_name="c", subcore_axis_name="s",
                                  num_cores=SC_CORES, num_subcores=SC_SUBCORES)

    def body(table_hbm, idx_hbm, out_hbm, idx_v, rows_v, gsem, wsem):
        wid = lax.axis_index("s") * SC_CORES + lax.axis_index("c")
        pltpu.sync_copy(idx_hbm.at[wid], idx_v)

        def gather(ci, slot):
            return pltpu.make_async_copy(table_hbm.at[idx_v.at[ci]], rows_v.at[slot], gsem.at[slot])

        def write(ci, slot):
            return pltpu.make_async_copy(rows_v.at[slot], out_hbm.at[ci, wid], wsem.at[slot])

        gather(0, 0).start()

        @pl.loop(0, n_chunks, step=2)
        def _(c0):
            for slot in range(2):
                ci = c0 + slot
                other = 1 - slot

                @pl.when(ci + 1 < n_chunks)
                def _():
                    @pl.when(ci >= 1)
                    def _():
                        write(ci - 1, other).wait()
                    gather(ci + 1, other).start()

                gather(ci, slot).wait()
                write(ci, slot).start()

        write(n_chunks - 2, 0).wait()
        write(n_chunks - 1, 1).wait()

    out = pl.kernel(
        body,
        out_type=jax.ShapeDtypeStruct((n_chunks, n_workers, chunk, width), table.dtype),
        mesh=mesh,
        scratch_types=[pltpu.VMEM((n_chunks, chunk), jnp.int32),
                       pltpu.VMEM((2, chunk, width), table.dtype),
                       pltpu.SemaphoreType.DMA((2,)),
                       pltpu.SemaphoreType.DMA((2,))],
        name="sc_gather",
    )(table, idx.reshape(n_chunks, n_workers, chunk).transpose(1, 0, 2))
    return out.reshape(n_idx, width)


def sc_dispatch(table, dest, pad_rows):
    n_tok, width = table.shape
    n_picks = dest.shape[0]
    n_pad = pad_rows.shape[0]
    n_workers = SC_CORES * SC_SUBCORES
    chunk = _sc_chunk(n_tok, n_pad)
    per_w = n_tok // chunk // n_workers
    pad_w = n_pad // chunk // n_workers
    assert per_w * chunk * n_workers == n_tok and pad_w * chunk * n_workers == n_pad
    mesh = plsc.VectorSubcoreMesh(core_axis_name="c", subcore_axis_name="s",
                                  num_cores=SC_CORES, num_subcores=SC_SUBCORES)

    def body(table_hbm, idx_hbm, pad_hbm, zero_hbm, out_hbm, idx_v, pad_v, rows_v, zero_v,
             rsem, ssem, zsem):
        wid = lax.axis_index("s") * SC_CORES + lax.axis_index("c")
        pltpu.sync_copy(idx_hbm.at[wid], idx_v)
        pltpu.sync_copy(pad_hbm.at[wid], pad_v)
        pltpu.sync_copy(zero_hbm, zero_v)

        def zero_fill(pc):
            return pltpu.make_async_copy(zero_v, out_hbm.at[pad_v.at[pc]], zsem)

        for pc in range(pad_w):
            zero_fill(pc).start()

        def scatter(ci, kk):
            return pltpu.make_async_copy(rows_v, out_hbm.at[idx_v.at[ci * n_picks + kk]], ssem)

        @pl.loop(0, per_w)
        def _(ci):
            pltpu.async_copy(table_hbm.at[ci, wid], rows_v, rsem).wait()
            for kk in range(n_picks):
                scatter(ci, kk).start()
            for kk in range(n_picks):
                scatter(ci, kk).wait()

        for pc in range(pad_w):
            zero_fill(pc).wait()

    idx = dest.reshape(n_picks, per_w, n_workers, chunk).transpose(2, 1, 0, 3)
    idx = idx.reshape(n_workers, per_w * n_picks, chunk)
    return pl.kernel(
        body,
        out_type=jax.ShapeDtypeStruct((n_tok * n_picks + n_pad, width), table.dtype),
        mesh=mesh,
        scratch_types=[pltpu.VMEM((per_w * n_picks, chunk), jnp.int32),
                       pltpu.VMEM((pad_w, chunk), jnp.int32),
                       pltpu.VMEM((chunk, width), table.dtype),
                       pltpu.VMEM((chunk, width), table.dtype),
                       pltpu.SemaphoreType.DMA, pltpu.SemaphoreType.DMA, pltpu.SemaphoreType.DMA],
        name="sc_dispatch",
    )(table.reshape(per_w, n_workers, chunk, width), idx,
      pad_rows.reshape(n_workers, pad_w, chunk), jnp.zeros((chunk, width), table.dtype))


def _combine_kernel(x_ref, *refs):
    y_refs, (gate_ref, gf_ref, o_ref) = refs[:TOP_K], refs[TOP_K:]
    gates = gate_ref[...]
    acc = gates[:, 0:1] * _unpack_rows(y_refs[0][...], F32)
    for kk in range(1, TOP_K):
        acc = acc + gates[:, kk:kk + 1] * _unpack_rows(y_refs[kk][...], F32)
    o_ref[...] = x_ref[...] + gf_ref[0] * acc


def moe_combine(lay, x, yg, gates, mod, drop_ctx):
    D = D_MODEL
    if drop_ctx:
        nbo = lay.nb - 1
        src = lambda r: (r // nbo) * lay.nb + 1 + r % nbo
        n_out = lay.n_batch * nbo
    else:
        src = lambda r: r
        n_out = lay.n_blocks
    y_specs = [pl.BlockSpec((TL, PACK_W), functools.partial(lambda kk, r: (kk * lay.n_blocks + src(r), 0), kk))
               for kk in range(TOP_K)]
    return pl.pallas_call(
        _combine_kernel,
        grid=(n_out,),
        in_specs=[pl.BlockSpec((TL, D), lambda r: (src(r), 0))] + y_specs + [
            pl.BlockSpec((TL, TOP_K), lambda r: (src(r), 0)),
            pl.BlockSpec((1, 1, D), lambda r: (lay.mod_row(src(r)), 0, 5))],
        out_specs=pl.BlockSpec((TL, D), lambda r: (r, 0)),
        out_shape=jax.ShapeDtypeStruct((n_out * TL, D), F32),
        compiler_params=_params(1),
        name="moe_combine",
    )(x, yg, yg, yg, yg, gates, mod)


def moe_route(top_e, rank, counts):
    T = top_e.shape[1]
    assert (T * TOP_K) % MOE_BLOCK == 0
    counts = counts.reshape(N_EXPERTS).astype(jnp.int32)
    padded = (counts + MOE_BLOCK - 1) // MOE_BLOCK * MOE_BLOCK
    padded_end = jnp.cumsum(padded)
    padded_start = padded_end - padded
    experts = jnp.arange(N_EXPERTS)
    start_of = jnp.sum(jnp.where(top_e[..., None] == experts, padded_start, 0), axis=-1)
    dest = (start_of + rank).astype(jnp.int32)
    n_pad = N_EXPERTS * MOE_BLOCK
    n_rows = T * TOP_K + n_pad
    n_blk = n_rows // MOE_BLOCK
    blk_start = jnp.arange(n_blk) * MOE_BLOCK
    blk_e = jnp.minimum(jnp.sum(padded_end[None, :] <= blk_start[:, None], axis=1), N_EXPERTS - 1)
    blk_e = blk_e.astype(jnp.int32)
    blk_first = jnp.concatenate([jnp.ones((1,), jnp.int32), (blk_e[1:] != blk_e[:-1]).astype(jnp.int32)])
    blk_hot = blk_e[:, None] == experts
    in_grp = blk_start - jnp.sum(jnp.where(blk_hot, padded_start, 0), axis=-1)
    blk_valid = jnp.clip(jnp.sum(jnp.where(blk_hot, counts, 0), axis=-1) - in_grp, 0, MOE_BLOCK)
    blk_valid = jnp.where(blk_start < padded_end[-1], blk_valid, 0).astype(jnp.int32)
    tail = padded - counts
    tail_end = jnp.cumsum(tail)
    j = jnp.arange(n_pad)
    owner = jnp.sum(tail_end[None, :] <= j[:, None], axis=1)
    base = padded_start + counts - (tail_end - tail)
    in_group = jnp.sum(jnp.where(owner[:, None] == experts, base, 0), axis=-1) + j
    pad_rows = jnp.where(j < tail_end[-1], in_group, padded_end[-1] + j - tail_end[-1])
    return dest, pad_rows.astype(jnp.int32), blk_e, blk_first, blk_valid


def kernel(x, c, ctx, c_ctx, norm_mix, norm_ffn, w_mod, b_mod, conv_w_in, conv_w, conv_w_out, ml_w_in, ml_b_gate, ml_norm, ml_w_out, mla_w_in, mla_q_norm, mla_kv_norm, mla_w_uq, mla_w_ukv, mla_qn_nope, mla_qn_rope, mla_kn_nope, mla_kn_rope, mla_w_out, moe_w_router, moe_b_router, moe_w1, moe_b1, moe_w2, moe_b2):
    Bn, n_lat, D = x.shape
    n_ctx = ctx.shape[1]
    assert D == D_MODEL and n_ctx == TL and n_lat % TL == 0
    assert (DEPTH - 1) % N_MIXERS == 0
    full = Layout(Bn, (n_ctx + n_lat) // TL, True)
    lat_only = Layout(Bn, n_lat // TL, False)
    mods = ada_all(c, c_ctx, w_mod, b_mod)
    tables = rope_tables(n_ctx, n_lat)
    b1_all = moe_b1.reshape(DEPTH, N_EXPERTS, 1, 2 * MOE_FF)
    b2_all = moe_b2.reshape(DEPTH, N_EXPERTS, 1, D)
    X = jnp.concatenate([ctx, x], axis=1).reshape(full.rows, D)
    for layer in range(DEPTH):
        kind, j = layer % N_MIXERS, layer // N_MIXERS
        lay = lat_only if layer == DEPTH - 1 else full
        mod = mods[layer]
        gain_a = norm_mix[layer].reshape(1, D)
        gain_f = norm_ffn[layer].reshape(1, D)
        if kind == 0:
            bg, v = conv_in(lay, X, gain_a, mod, conv_w_in[j].astype(BF16))
            pro, w_out, name = (v, bg, conv_w[j]), conv_w_out[j], "conv"
        elif kind == 1:
            w = ml_w_in[j]
            n_main = 2 * ML_QK + 2 * ML_V
            w_main = jnp.concatenate([w[:, :ML_QK] * ML_DQK ** -0.5, w[:, ML_QK:n_main]], axis=1)
            q, k, kt, v, og, g, gt = mlstm_in(lay, X, gain_a, mod, w_main.astype(BF16), w[:, n_main:],
                                              ml_b_gate[j])
            h_f, h_b = mlstm_scan(lay, q, k, kt, v, g, gt)
            pro, w_out, name = (h_f, h_b, og, ml_norm[j]), ml_w_out[j], "mlstm"
        else:
            q, k, v = mla_in(lay, X, gain_a, mod, mla_w_in[j], mla_q_norm[j], mla_kv_norm[j],
                             mla_w_uq[j], mla_w_ukv[j], mla_qn_nope[j], mla_qn_rope[j],
                             mla_kn_nope[j], mla_kn_rope[j], tables)
            pro, w_out, name = (mla_attention(lay, q, k, v),), mla_w_out[j], "mla"
        X, h2, top_e, gates, rank, counts = mixer_out(
            lay, name, pro, w_out, X, mod, gain_f, moe_w_router[layer], moe_b_router[layer])
        dest, pad_rows, blk_e, blk_first, blk_valid = moe_route(top_e, rank, counts)
        xp = sc_dispatch(h2, dest, pad_rows)
        yp = expert_ffn(layer, xp, blk_e, blk_first, blk_valid, moe_w1, b1_all, moe_w2, b2_all)
        yg = sc_gather(yp, dest.reshape(-1))
        X = moe_combine(lay, X, yg, gates.T, mod, drop_ctx=(layer == DEPTH - 2))
    return X.reshape(Bn, n_lat, D)
```

```python
import functools

import jax
import jax.numpy as jnp
from jax import lax
from jax.experimental import pallas as pl
from jax.experimental.pallas import tpu as pltpu
from jax.experimental.pallas import tpu_sc as plsc

D_MODEL = 1024
DEPTH = 4
GRID_W = 64
N_MIXERS = 3
N_ADA = 6
RMS_EPS = 1e-6
CONV_WIDTH = 3
ML_HEADS = 8
ML_DQK = 64
ML_DV = 128
ML_QK = ML_HEADS * ML_DQK
ML_V = ML_HEADS * ML_DV
GATE_CAP = 15.0
MLA_HEADS = 8
MLA_NOPE = 128
MLA_ROPE = 64
MLA_V = 128
MLA_Q_LORA = 384
MLA_KV_LORA = 256
MLA_SCALE = (MLA_NOPE + MLA_ROPE) ** -0.5
ROPE_THETA = 10000.0
N_EXPERTS = 32
TOP_K = 4
MOE_FF = D_MODEL
SWIGLU_ALPHA = 1.702
SWIGLU_LIMIT = 7.0
MOE_BLOCK = 512

TL = 256
LANES = 128
BF16_ROWS = 16
VMEM_LIMIT = 48 * 1024 * 1024
HI = lax.Precision.HIGHEST
F32 = jnp.float32
BF16 = jnp.bfloat16


def _params(n_axes):
    return pltpu.CompilerParams(dimension_semantics=("arbitrary",) * n_axes,
                                vmem_limit_bytes=VMEM_LIMIT)


def _rms(x, width=None):
    width = x.shape[-1] if width is None else width
    return x * lax.rsqrt(jnp.sum(x * x, axis=-1, keepdims=True) * (1.0 / width) + RMS_EPS)


def _norm_mod(x, gain, scale, shift):
    return _rms(x) * (gain * (1.0 + scale)) + shift


def _split_weight_lanes(w):
    hi = w.astype(BF16)
    lo = (w - hi.astype(F32)).astype(BF16)
    return jnp.pad(jnp.concatenate([hi, lo], axis=1), ((0, 0), (0, LANES - 2 * w.shape[1])))


PACK_W = D_MODEL // 2
HIGH_HALF = -65536


def _pack_rows(x):
    xb = x.astype(BF16).astype(F32)
    lo = lax.bitcast_convert_type(xb[:, :PACK_W], jnp.int32)
    hi = lax.bitcast_convert_type(xb[:, PACK_W:], jnp.int32)
    return hi | lax.shift_right_logical(lo, 16)


def _unpack_rows(w, dtype):
    lo = lax.bitcast_convert_type(lax.shift_left(w, 16), F32)
    hi = lax.bitcast_convert_type(w & HIGH_HALF, F32)
    return jnp.concatenate([lo.astype(dtype), hi.astype(dtype)], axis=1)


class Layout:
    def __init__(self, n_batch, nb, ctx_first):
        self.n_batch, self.nb, self.ctx_first = n_batch, nb, ctx_first
        self.n_blocks = n_batch * nb
        self.rows = self.n_blocks * TL

    def mod_row(self, r):
        b = r // self.nb
        return jnp.where(r % self.nb == 0, self.n_batch, b) if self.ctx_first else b

    def row_spec(self, width):
        return pl.BlockSpec((TL, width), lambda r: (r, 0))

    def mod_spec(self, piece):
        return pl.BlockSpec((1, 1, D_MODEL), lambda r: (self.mod_row(r), 0, piece))


def _const_spec(shape):
    return pl.BlockSpec(shape, lambda *_: (0,) * len(shape))


ADA_ROWS = 16
ADA_TN = 1536


def _ada_kernel(c_ref, w_ref, b_ref, o_ref):
    c = c_ref[...]
    s = c * jax.nn.sigmoid(c)
    o_ref[0] = jnp.dot(s, w_ref[0], precision=HI, preferred_element_type=F32) + b_ref[0]


def ada_all(c, c_ctx, w_mod, b_mod):
    Bn, D = c.shape
    assert Bn + 1 <= ADA_ROWS
    cond = jnp.zeros((ADA_ROWS, D), F32).at[:Bn].set(c).at[Bn].set(c_ctx)
    out = pl.pallas_call(
        _ada_kernel,
        grid=(DEPTH, N_ADA * D // ADA_TN),
        in_specs=[pl.BlockSpec((ADA_ROWS, D), lambda l, n: (0, 0)),
                  pl.BlockSpec((1, D, ADA_TN), lambda l, n: (l, 0, n)),
                  pl.BlockSpec((1, 1, ADA_TN), lambda l, n: (l, 0, n))],
        out_specs=pl.BlockSpec((1, ADA_ROWS, ADA_TN), lambda l, n: (l, 0, n)),
        out_shape=jax.ShapeDtypeStruct((DEPTH, ADA_ROWS, N_ADA * D), F32),
        compiler_params=_params(2),
        name="ada_mod",
    )(cond, w_mod, b_mod.reshape(DEPTH, 1, N_ADA * D))
    return out[:, :Bn + 1, None, :]


def _conv_in_kernel(x_ref, gain_ref, sc_ref, sh_ref, w_ref, bg_ref, v_ref):
    D = D_MODEL
    h = _norm_mod(x_ref[...], gain_ref[...], sc_ref[0], sh_ref[0]).astype(BF16)
    p = jnp.dot(h, w_ref[...], preferred_element_type=F32)
    bg_ref[...] = p[:, :D].astype(BF16)
    v_ref[...] = (p[:, D:2 * D] * p[:, 2 * D:]).astype(BF16)


def conv_in(lay, x, gain, mod, w_in):
    D = D_MODEL
    sds = jax.ShapeDtypeStruct((lay.rows, D), BF16)
    return pl.pallas_call(
        _conv_in_kernel,
        grid=(lay.n_blocks,),
        in_specs=[lay.row_spec(D), _const_spec((1, D)), lay.mod_spec(1), lay.mod_spec(0),
                  _const_spec((D, 3 * D))],
        out_specs=[lay.row_spec(D), lay.row_spec(D)],
        out_shape=[sds, sds],
        compiler_params=_params(1),
        name="conv_in",
    )(x, gain, mod, mod, w_in)


ML_T = 256
ML_SW = 2 * ML_DV
ML_NG = 4 * ML_HEADS


LOG2E = 1.4426950408889634


def _gate_act(g, is_forget):
    g = GATE_CAP * jnp.tanh(g * (1.0 / GATE_CAP))
    log_sig = jnp.minimum(g, 0.0) - jnp.log(1.0 + jnp.exp(-jnp.abs(g)))
    return jnp.where(is_forget, log_sig, g) * LOG2E


def _mlstm_in_kernel(x_ref, gain_ref, sc_ref, sh_ref, w_ref, wg_ref, bg_ref,
                     q_ref, k_ref, kt_ref, v_ref, og_ref, g_ref, gt_ref):
    h = _norm_mod(x_ref[...], gain_ref[...], sc_ref[0], sh_ref[0])
    hb = h.astype(BF16)
    p = jnp.dot(hb, w_ref[...], preferred_element_type=F32)
    q_ref[...] = p[:, :ML_QK].astype(BF16)
    k_ref[...] = p[:, ML_QK:2 * ML_QK].astype(BF16)
    kt_ref[...] = p[:, ML_QK:2 * ML_QK].T.astype(BF16)
    v_ref[...] = p[:, 2 * ML_QK:2 * ML_QK + ML_V].astype(BF16)
    og_ref[...] = jax.nn.sigmoid(p[:, 2 * ML_QK + ML_V:]).astype(BF16)
    h_lo = (h - hb.astype(F32)).astype(BF16)
    g = jnp.dot(jnp.concatenate([hb, h_lo, hb], axis=1), wg_ref[...],
                preferred_element_type=F32) + bg_ref[...]
    col = lax.broadcasted_iota(jnp.int32, g.shape, 1)
    g = _gate_act(g, (col // ML_HEADS) % 2 == 1)
    g_ref[...] = g[:, :ML_NG]
    gt_ref[...] = g.T[:ML_NG]


def mlstm_in(lay, x, gain, mod, w_main, w_g, b_g):
    D = D_MODEL
    n_main = 2 * ML_QK + 2 * ML_V
    g_hi = w_g.astype(BF16)
    g_lo = (w_g - g_hi.astype(F32)).astype(BF16)
    wg3 = jnp.pad(jnp.concatenate([g_hi, g_hi, g_lo], axis=0), ((0, 0), (0, LANES - ML_NG)))
    bf = lambda w: jax.ShapeDtypeStruct((lay.rows, w), BF16)
    return pl.pallas_call(
        _mlstm_in_kernel,
        grid=(lay.n_blocks,),
        in_specs=[lay.row_spec(D), _const_spec((1, D)), lay.mod_spec(1), lay.mod_spec(0),
                  _const_spec((D, n_main)), _const_spec((3 * D, LANES)), _const_spec((1, LANES))],
        out_specs=[lay.row_spec(ML_QK), lay.row_spec(ML_QK), pl.BlockSpec((ML_QK, TL), lambda r: (0, r)),
                   lay.row_spec(ML_V), lay.row_spec(ML_V),
                   lay.row_spec(ML_NG), pl.BlockSpec((ML_NG, TL), lambda r: (0, r))],
        out_shape=[bf(ML_QK), bf(ML_QK), jax.ShapeDtypeStruct((ML_QK, lay.rows), BF16),
                   bf(ML_V), bf(ML_V),
                   jax.ShapeDtypeStruct((lay.rows, ML_NG), F32),
                   jax.ShapeDtypeStruct((ML_NG, lay.rows), F32)],
        compiler_params=_params(1),
        name="mlstm_in",
    )(x, gain, mod, mod, w_main, wg3, _pad_lanes(b_g))


def _split3(x):
    hi = x.astype(BF16)
    r1 = x - hi.astype(F32)
    mid = r1.astype(BF16)
    lo = (r1 - mid.astype(F32)).astype(BF16)
    return hi, mid, lo


def _dot_exact01(x, sel01, x_on_left):
    sel = sel01.astype(BF16)
    parts = [jnp.dot(p, sel, preferred_element_type=F32) if x_on_left
             else jnp.dot(sel, p, preferred_element_type=F32) for p in _split3(x)]
    return parts[0] + parts[1] + parts[2]


def _mlstm_dir(reverse, q_ref, k_ref, kt_ref, v_ref, g_ref, gt_ref, o_ref, s_ref, m_ref):
    T = ML_T
    row = lax.broadcasted_iota(jnp.int32, (T, T), 0)
    col = lax.broadcasted_iota(jnp.int32, (T, T), 1)
    mask = (col >= row) if reverse else (col <= row)
    gt = gt_ref[...]
    bc = _dot_exact01(g_ref[...], mask, x_on_left=False)
    br = _dot_exact01(gt, (row >= col) if reverse else (row <= col), x_on_left=True)
    gi, gf = (2 * ML_HEADS, 3 * ML_HEADS) if reverse else (0, ML_HEADS)
    lane = lax.broadcasted_iota(jnp.int32, (T, 2 * ML_DQK), 1)
    sub = lax.broadcasted_iota(jnp.int32, (2 * ML_DQK, T), 0)
    gate_row = lax.broadcasted_iota(jnp.int32, (ML_NG, ML_DV), 0)
    ones = jnp.ones((T, ML_DV), BF16)
    heads = range(ML_HEADS)
    qm, vx, s_raw = {}, {}, {}
    for h in heads:
        pair = (h // 2) * 2 * ML_DQK
        own = (lane >= ML_DQK) if (h % 2) else (lane < ML_DQK)
        qp = q_ref[:, pair:pair + 2 * ML_DQK]
        qm[h] = jnp.where(own, qp, jnp.zeros_like(qp))
        vx[h] = jnp.concatenate([v_ref[:, h * ML_DV:(h + 1) * ML_DV], ones], axis=1)
        s_raw[h] = lax.dot_general(qm[h], k_ref[:, pair:pair + 2 * ML_DQK], (((1,), (1,)), ((), ())),
                                   preferred_element_type=F32)
    p, a, u_rep, m_prev, s_prev = {}, {}, {}, {}, {}
    for h in heads:
        c_row = gt[gi + h:gi + h + 1, :] - br[gf + h:gf + h + 1, :]
        e = jnp.where(mask, c_row, -jnp.inf)
        m_prev[h] = m_ref[h][0:1, 0:1]
        u = jnp.maximum(m_prev[h], jnp.max(e, axis=1, keepdims=True))
        p[h] = (s_raw[h] * jnp.exp2(e - u)).astype(BF16)
        u_rep[h] = jnp.broadcast_to(u, (T, ML_DV))
        a[h] = jnp.exp2(m_prev[h] - u_rep[h])
        s_prev[h] = s_ref[h]
    for h in heads:
        r = jnp.dot(p[h], vx[h], preferred_element_type=F32)
        qs = jnp.dot(qm[h], s_prev[h].astype(BF16), preferred_element_type=F32)
        num = r[:, :ML_DV] + a[h] * qs[:, :ML_DV]
        den = r[:, ML_DV:] + a[h] * qs[:, ML_DV:]
        b_rep = _dot_exact01(bc, gate_row == gf + h, x_on_left=True)
        floor = jnp.exp2(-(b_rep + u_rep[h]))
        o_ref[:, h * ML_DV:(h + 1) * ML_DV] = (num / jnp.maximum(jnp.abs(den), floor)).astype(o_ref.dtype)
    for h in heads:
        pair = (h // 2) * 2 * ML_DQK
        b_row = br[gf + h:gf + h + 1, :]
        tot = b_row[:, 0:1] if reverse else b_row[:, T - 1:T]
        g_row = tot - b_row + gt[gi + h:gi + h + 1, :]
        m_new = jnp.maximum(tot + m_prev[h], jnp.max(g_row, axis=1, keepdims=True))
        decay = jnp.exp2(tot + m_prev[h] - m_new)
        wk = jnp.exp2(g_row - m_new)
        own_t = (sub >= ML_DQK) if (h % 2) else (sub < ML_DQK)
        kt = kt_ref[pair:pair + 2 * ML_DQK, :].astype(F32)
        kw = jnp.where(own_t, kt * wk, 0.0).astype(BF16)
        s_ref[h] = decay * s_prev[h] + jnp.dot(kw, vx[h], preferred_element_type=F32)
        m_ref[h] = jnp.broadcast_to(m_new, m_ref.shape[1:])


def _mlstm_scan_kernel(qf_ref, kf_ref, ktf_ref, vf_ref, gf_ref, gtf_ref,
                       qb_ref, kb_ref, ktb_ref, vb_ref, gb_ref, gtb_ref,
                       of_ref, ob_ref, sf_ref, mf_ref, sb_ref, mb_ref):
    @pl.when(pl.program_id(1) == 0)
    def _():
        sf_ref[...] = jnp.zeros_like(sf_ref)
        mf_ref[...] = jnp.zeros_like(mf_ref)
        sb_ref[...] = jnp.zeros_like(sb_ref)
        mb_ref[...] = jnp.zeros_like(mb_ref)

    _mlstm_dir(False, qf_ref, kf_ref, ktf_ref, vf_ref, gf_ref, gtf_ref, of_ref, sf_ref, mf_ref)
    _mlstm_dir(True, qb_ref, kb_ref, ktb_ref, vb_ref, gb_ref, gtb_ref, ob_ref, sb_ref, mb_ref)


def mlstm_scan(lay, q, k, kt, v, g, gt):
    assert lay.ctx_first
    per = TL // ML_T
    nb = lay.nb * per
    rev = lambda j: jnp.where(j < per, per - 1 - j, nb + per - 1 - j)
    fwd = lambda b, j: (b * nb + j, 0)
    bwd = lambda b, j: (b * nb + rev(j), 0)
    fwd_t = lambda b, j: (0, b * nb + j)
    bwd_t = lambda b, j: (0, b * nb + rev(j))

    def specs(im, imt):
        return [pl.BlockSpec((ML_T, ML_QK), im), pl.BlockSpec((ML_T, ML_QK), im),
                pl.BlockSpec((ML_QK, ML_T), imt), pl.BlockSpec((ML_T, ML_V), im),
                pl.BlockSpec((ML_T, ML_NG), im), pl.BlockSpec((ML_NG, ML_T), imt)]

    out_sds = jax.ShapeDtypeStruct((lay.rows, ML_V), BF16)
    state = [pltpu.VMEM((ML_HEADS, 2 * ML_DQK, ML_SW), F32),
             pltpu.VMEM((ML_HEADS, 8, LANES), F32)]
    return pl.pallas_call(
        _mlstm_scan_kernel,
        grid=(lay.n_batch, nb),
        in_specs=specs(fwd, fwd_t) + specs(bwd, bwd_t),
        out_specs=[pl.BlockSpec((ML_T, ML_V), fwd), pl.BlockSpec((ML_T, ML_V), bwd)],
        out_shape=[out_sds, out_sds],
        scratch_shapes=state + state,
        compiler_params=_params(2),
        name="mlstm_scan",
    )(q, k, kt, v, g, gt, q, k, kt, v, g, gt)


MLA_QK_PAD = 256
MLA_VW = 2 * MLA_V
MLA_IN_PAD = MLA_Q_LORA + MLA_KV_LORA + LANES
ROPE_HALF = MLA_ROPE // 4


def _mla_in_kernel(x_ref, gain_ref, sc_ref, sh_ref, win_ref, wuq_ref, wukv_ref, qn_ref, kvn_ref,
                   qnn_ref, qnr_ref, knn_ref, knr_ref, cos_ref, sa_ref, sb_ref,
                   q_out, k_out, v_out):
    h = _norm_mod(x_ref[...], gain_ref[...], sc_ref[0], sh_ref[0]).astype(BF16)
    p = jnp.dot(h, win_ref[...], preferred_element_type=F32)
    cq = _rms(p[:, :MLA_Q_LORA]) * qn_ref[...]
    ckv = _rms(p[:, MLA_Q_LORA:MLA_Q_LORA + MLA_KV_LORA]) * kvn_ref[...]
    kr = p[:, MLA_Q_LORA + MLA_KV_LORA:]
    q = jnp.dot(cq.astype(BF16), wuq_ref[...], preferred_element_type=F32)
    kv = jnp.dot(ckv.astype(BF16), wukv_ref[...], preferred_element_type=F32)
    nv = MLA_HEADS * MLA_NOPE
    ones_col = (lax.broadcasted_iota(jnp.int32, (TL, MLA_VW - MLA_V), 1) == 0).astype(BF16)
    cos, sa, sb = cos_ref[...], sa_ref[...], sb_ref[...]

    def rope(xp):
        return (xp * cos + pltpu.roll(xp, LANES - ROPE_HALF, 1) * sa + pltpu.roll(xp, ROPE_HALF, 1) * sb)

    heads = range(MLA_HEADS)
    kr = _rms(kr, MLA_ROPE) * knr_ref[...]
    qn = {hd: _rms(q[:, hd * MLA_QK_PAD:hd * MLA_QK_PAD + MLA_NOPE]) * qnn_ref[...] for hd in heads}
    qr = {hd: _rms(q[:, hd * MLA_QK_PAD + MLA_NOPE:(hd + 1) * MLA_QK_PAD], MLA_ROPE) * qnr_ref[...]
          for hd in heads}
    kn = {hd: _rms(kv[:, hd * MLA_NOPE:(hd + 1) * MLA_NOPE]) * knn_ref[...] for hd in heads}
    kr = rope(kr).astype(BF16)
    qr = {hd: rope(qr[hd]) for hd in heads}
    for hd in heads:
        c0 = hd * MLA_QK_PAD
        q_out[:, c0:c0 + MLA_NOPE] = qn[hd].astype(BF16)
        q_out[:, c0 + MLA_NOPE:c0 + MLA_QK_PAD] = qr[hd].astype(BF16)
        v0 = nv + hd * MLA_V
        v_out[:, hd * MLA_VW:hd * MLA_VW + MLA_V] = kv[:, v0:v0 + MLA_V].astype(BF16)
        v_out[:, hd * MLA_VW + MLA_V:(hd + 1) * MLA_VW] = ones_col
        k_out[:, c0:c0 + MLA_NOPE] = kn[hd].astype(BF16)
        k_out[:, c0 + MLA_NOPE:c0 + MLA_QK_PAD] = kr


def _pad_lanes(g):
    return jnp.pad(g, (0, LANES - g.shape[0])).reshape(1, LANES)


def mla_in(lay, x, gain, mod, w_in, q_norm, kv_norm, w_uq, w_ukv, qn_nope, qn_rope, kn_nope, kn_rope,
           tables):
    D = D_MODEL
    Hn = MLA_HEADS
    win = jnp.pad(w_in, ((0, 0), (0, MLA_IN_PAD - w_in.shape[1]))).astype(BF16)
    wuq = jnp.pad(w_uq.reshape(MLA_Q_LORA, Hn, MLA_NOPE + MLA_ROPE),
                  ((0, 0), (0, 0), (0, MLA_QK_PAD - MLA_NOPE - MLA_ROPE)))
    wuq = wuq.reshape(MLA_Q_LORA, Hn * MLA_QK_PAD).astype(BF16)
    wkv = w_ukv.reshape(MLA_KV_LORA, Hn, MLA_NOPE + MLA_V)
    wukv = jnp.concatenate([wkv[:, :, :MLA_NOPE].reshape(MLA_KV_LORA, Hn * MLA_NOPE),
                            wkv[:, :, MLA_NOPE:].reshape(MLA_KV_LORA, Hn * MLA_V)], axis=1).astype(BF16)
    nb = lay.nb
    tab_spec = pl.BlockSpec((TL, LANES), lambda r: (r % nb, 0))
    bf = lambda w: jax.ShapeDtypeStruct((lay.rows, w), BF16)
    return pl.pallas_call(
        _mla_in_kernel,
        grid=(lay.n_blocks,),
        in_specs=[lay.row_spec(D), _const_spec((1, D)), lay.mod_spec(1), lay.mod_spec(0),
                  _const_spec(win.shape), _const_spec(wuq.shape), _const_spec(wukv.shape),
                  _const_spec((1, MLA_Q_LORA)), _const_spec((1, MLA_KV_LORA)),
                  _const_spec((1, LANES)), _const_spec((1, LANES)), _const_spec((1, LANES)),
                  _const_spec((1, LANES)), tab_spec, tab_spec, tab_spec],
        out_specs=[lay.row_spec(Hn * MLA_QK_PAD), lay.row_spec(Hn * MLA_QK_PAD), lay.row_spec(Hn * MLA_VW)],
        out_shape=[bf(Hn * MLA_QK_PAD), bf(Hn * MLA_QK_PAD), bf(Hn * MLA_VW)],
        compiler_params=_params(1),
        name="mla_in",
    )(x, gain, mod, mod, win, wuq, wukv, q_norm.reshape(1, -1), kv_norm.reshape(1, -1),
      (qn_nope * (MLA_SCALE * LOG2E)).reshape(1, -1), _pad_lanes(qn_rope * (MLA_SCALE * LOG2E)),
      kn_nope.reshape(1, -1), _pad_lanes(kn_rope), *tables)


def rope_tables(n_ctx, n_lat):
    n_freq = MLA_ROPE // 4
    inv = ROPE_THETA ** (-jnp.arange(n_freq, dtype=F32) / n_freq)
    t = jnp.arange(n_lat)
    a_r = (t // GRID_W).astype(F32)[:, None] * inv
    a_c = (t % GRID_W).astype(F32)[:, None] * inv
    ang = jnp.concatenate([a_r, a_r, a_c, a_c], axis=-1)
    ang = jnp.concatenate([jnp.zeros((n_ctx, MLA_ROPE), F32), ang], axis=0)
    cos, sin = jnp.cos(ang), jnp.sin(ang)
    low = (jnp.arange(MLA_ROPE) % (2 * ROPE_HALF)) < ROPE_HALF
    pad = lambda a: jnp.pad(a, ((0, 0), (0, LANES - MLA_ROPE)))
    return pad(cos), pad(jnp.where(low, -sin, 0.0)), pad(jnp.where(low, 0.0, sin))


MLA_HPS = 2


def _mla_attn_kernel(q_ref, k_ref, v_ref, o_ref, *, n_ctx):
    def attend(n_keys):
        heads = range(MLA_HPS)
        s, p = {}, {}
        for hd in heads:
            q = q_ref[0, :, hd * MLA_QK_PAD:(hd + 1) * MLA_QK_PAD]
            k = k_ref[0, :n_keys, hd * MLA_QK_PAD:(hd + 1) * MLA_QK_PAD]
            s[hd] = lax.dot_general(q, k, (((1,), (1,)), ((), ())), preferred_element_type=F32)
        for hd in heads:
            p[hd] = jnp.exp2(s[hd] - s[hd].max(axis=1, keepdims=True)).astype(BF16)
        for hd in heads:
            v = v_ref[0, :n_keys, hd * MLA_VW:(hd + 1) * MLA_VW]
            r = jnp.dot(p[hd], v, preferred_element_type=F32)
            o_ref[0, :, hd * MLA_V:(hd + 1) * MLA_V] = (
                r[:, :MLA_V] / r[:, MLA_V:MLA_V + 1]).astype(o_ref.dtype)

    @pl.when(pl.program_id(2) == 0)
    def _():
        attend(n_ctx)

    @pl.when(pl.program_id(2) > 0)
    def _():
        attend(k_ref.shape[1])


def mla_attention(lay, q, k, v):
    assert lay.ctx_first
    Bn, S = lay.n_batch, lay.nb * TL
    q3, k3, v3 = (a.reshape(Bn, S, a.shape[-1]) for a in (q, k, v))
    out = pl.pallas_call(
        functools.partial(_mla_attn_kernel, n_ctx=TL),
        grid=(Bn, MLA_HEADS // MLA_HPS, lay.nb),
        in_specs=[pl.BlockSpec((1, TL, MLA_HPS * MLA_QK_PAD), lambda b, h, i: (b, i, h)),
                  pl.BlockSpec((1, S, MLA_HPS * MLA_QK_PAD), lambda b, h, i: (b, 0, h)),
                  pl.BlockSpec((1, S, MLA_HPS * MLA_VW), lambda b, h, i: (b, 0, h))],
        out_specs=pl.BlockSpec((1, TL, MLA_HPS * MLA_V), lambda b, h, i: (b, i, h)),
        out_shape=jax.ShapeDtypeStruct((Bn, S, MLA_HEADS * MLA_V), BF16),
        compiler_params=_params(3),
        name="mla_attention",
    )(q3, k3, v3)
    return out.reshape(lay.rows, MLA_HEADS * MLA_V)


N_PROLOGUE = {"conv": 5, "mlstm": 4, "mla": 1}


def _mixer_out_kernel(*refs, kind, nb, ctx_first):
    n_pro = N_PROLOGUE[kind]
    pro = refs[:n_pro]
    (wout_ref, x_ref, ga_ref, gain_ref, sc_ref, sh_ref, wr_ref, br_ref,
     xo_ref, h2_ref, te_ref, gate_ref, rank_ref, cnt_ref, carry_ref) = refs[n_pro:]
    r = pl.program_id(0)

    if kind == "conv":
        vprev_ref, v_ref, vnext_ref, bg_ref, cw_ref = pro
        j = r % nb
        first = (j == 0) | (j == 1) if ctx_first else (j == 0)
        last = (j == nb - 1) | (j == 0) if ctx_first else (j == nb - 1)
        v = v_ref[...].astype(F32)
        rows = lax.broadcasted_iota(jnp.int32, (TL, 1), 0)
        prev_row = jnp.where(first, 0.0, vprev_ref[BF16_ROWS - 1:BF16_ROWS, :].astype(F32))
        next_row = jnp.where(last, 0.0, vnext_ref[0:1, :].astype(F32))
        up = jnp.where(rows == 0, prev_row, pltpu.roll(v, 1, 0))
        dn = jnp.where(rows == TL - 1, next_row, pltpu.roll(v, TL - 1, 0))
        cw = cw_ref[...]
        a = bg_ref[...].astype(F32) * (up * cw[0:1] + v * cw[1:2] + dn * cw[2:3])
    elif kind == "mlstm":
        hf_ref, hb_ref, og_ref, ng_ref = pro
        hh = hf_ref[...].astype(F32) + hb_ref[...].astype(F32)
        a = jnp.concatenate([_rms(hh[:, h * ML_DV:(h + 1) * ML_DV]) for h in range(ML_HEADS)], axis=1)
        a = a * ng_ref[...] * og_ref[...].astype(F32)
    else:
        a = pro[0][...]

    y = jnp.dot(a.astype(BF16), wout_ref[...], preferred_element_type=F32)
    xn = x_ref[...] + ga_ref[0] * y
    xo_ref[...] = xn
    h2 = _norm_mod(xn, gain_ref[...], sc_ref[0], sh_ref[0])
    h2_ref[...] = _pack_rows(h2)
    lt = jnp.dot(h2.astype(BF16), wr_ref[...], preferred_element_type=F32).T
    logits = lt[:N_EXPERTS] + lt[N_EXPERTS:2 * N_EXPERTS] + br_ref[...]

    sub = lax.broadcasted_iota(jnp.int32, (N_EXPERTS, TL), 0)
    sub_k = lax.broadcasted_iota(jnp.int32, (TOP_K, TL), 0)
    work = logits
    sel = jnp.zeros((N_EXPERTS, TL), F32)
    top_e = jnp.zeros((TOP_K, TL), jnp.int32)
    top_v = jnp.zeros((TOP_K, TL), F32)
    picks = []
    for kk in range(TOP_K):
        m = work.max(axis=0, keepdims=True)
        idx = jnp.min(jnp.where(work == m, sub, N_EXPERTS), axis=0, keepdims=True)
        hit = sub == idx
        picks.append(hit)
        sel = jnp.where(hit, 1.0, sel)
        work = jnp.where(hit, -jnp.inf, work)
        top_e = jnp.where(sub_k == kk, idx, top_e)
        top_v = jnp.where(sub_k == kk, m, top_v)
    ex = jnp.exp(top_v - top_v[0:1])
    gate_ref[...] = ex / ex.sum(axis=0, keepdims=True)
    te_ref[...] = top_e

    @pl.when(r == 0)
    def _():
        carry_ref[...] = jnp.zeros_like(carry_ref)

    tr = lax.broadcasted_iota(jnp.int32, (TL, TL), 0)
    tc = lax.broadcasted_iota(jnp.int32, (TL, TL), 1)
    before = jnp.dot(sel.astype(BF16), (tr < tc).astype(BF16), preferred_element_type=F32)
    pos = before + carry_ref[...]
    rank = jnp.zeros((TOP_K, TL), F32)
    for kk in range(TOP_K):
        rk = jnp.sum(jnp.where(picks[kk], pos, 0.0), axis=0, keepdims=True)
        rank = jnp.where(sub_k == kk, rk, rank)
    rank_ref[...] = rank.astype(jnp.int32)
    total = carry_ref[...] + jnp.sum(sel, axis=1, keepdims=True)
    carry_ref[...] = total
    cnt_ref[...] = total


def mixer_out(lay, kind, pro_args, w_out, x, mod, gain_f, w_r, b_r):
    D = D_MODEL
    nb = lay.nb
    if kind == "conv":
        v, bg, cw = pro_args
        per = TL // BF16_ROWS
        last_tile = lay.rows // BF16_ROWS - 1
        pro_specs = [pl.BlockSpec((BF16_ROWS, D), lambda r: (jnp.maximum(r * per - 1, 0), 0)),
                     lay.row_spec(D),
                     pl.BlockSpec((BF16_ROWS, D), lambda r: (jnp.minimum((r + 1) * per, last_tile), 0)),
                     lay.row_spec(D), _const_spec((CONV_WIDTH, D))]
        pro_in = [v, v, v, bg, cw]
    elif kind == "mlstm":
        h_f, h_b, og, ng = pro_args
        pro_specs = [lay.row_spec(ML_V), lay.row_spec(ML_V), lay.row_spec(ML_V), _const_spec((1, ML_V))]
        pro_in = [h_f, h_b, og, ng.reshape(1, ML_V)]
    else:
        pro_specs = [lay.row_spec(D)]
        pro_in = list(pro_args)
    k_in = w_out.shape[0]
    small = lambda dt: jax.ShapeDtypeStruct((TOP_K, lay.rows), dt)
    small_spec = pl.BlockSpec((TOP_K, TL), lambda r: (0, r))
    return pl.pallas_call(
        functools.partial(_mixer_out_kernel, kind=kind, nb=nb, ctx_first=lay.ctx_first),
        grid=(lay.n_blocks,),
        in_specs=pro_specs + [_const_spec((k_in, D)), lay.row_spec(D), lay.mod_spec(2),
                              _const_spec((1, D)), lay.mod_spec(4), lay.mod_spec(3),
                              _const_spec((D, LANES)), _const_spec((N_EXPERTS, 1))],
        out_specs=[lay.row_spec(D), lay.row_spec(PACK_W), small_spec, small_spec, small_spec,
                   _const_spec((N_EXPERTS, 1))],
        out_shape=[jax.ShapeDtypeStruct((lay.rows, D), F32),
                   jax.ShapeDtypeStruct((lay.rows, PACK_W), jnp.int32),
                   small(jnp.int32), small(F32), small(jnp.int32),
                   jax.ShapeDtypeStruct((N_EXPERTS, 1), F32)],
        scratch_shapes=[pltpu.VMEM((N_EXPERTS, 1), F32)],
        compiler_params=_params(1),
        name="mixer_out_" + kind,
    )(*pro_in, w_out.astype(BF16), x, mod, gain_f, mod, mod, _split_weight_lanes(w_r),
      b_r.reshape(N_EXPERTS, 1))


def _expert_ffn_kernel(blk_e_ref, first_ref, valid_ref, x_ref, w1_ref, b1_ref, w2_ref, b2_ref, o_ref,
                       w1b_ref, w2b_ref):
    del blk_e_ref
    i = pl.program_id(0)

    @pl.when(first_ref[i] == 1)
    def _():
        w1b_ref[...] = w1_ref[0, 0].astype(BF16)
        w2b_ref[...] = w2_ref[0, 0].astype(BF16)

    @pl.when(valid_ref[i] > 0)
    def _():
        x = _unpack_rows(x_ref[...], BF16)
        h = jnp.dot(x, w1b_ref[...], preferred_element_type=F32) + b1_ref[0, 0]
        glu = jnp.minimum(h[:, :MOE_FF], SWIGLU_LIMIT)
        lin = jnp.clip(h[:, MOE_FF:], -SWIGLU_LIMIT, SWIGLU_LIMIT)
        act = glu * jax.nn.sigmoid(SWIGLU_ALPHA * glu) * (lin + 1.0)
        y = jnp.dot(act.astype(BF16), w2b_ref[...], preferred_element_type=F32)
        o_ref[...] = _pack_rows(y + b2_ref[0, 0])

    @pl.when(valid_ref[i] == 0)
    def _():
        o_ref[...] = jnp.zeros_like(o_ref)


def expert_ffn(layer, xp, blk_e, blk_first, blk_valid, w1, b1, w2, b2):
    n_rows = xp.shape[0]
    D, F2 = D_MODEL, 2 * MOE_FF
    n_blk = n_rows // MOE_BLOCK
    grid_spec = pltpu.PrefetchScalarGridSpec(
        num_scalar_prefetch=3,
        grid=(n_blk,),
        in_specs=[
            pl.BlockSpec((MOE_BLOCK, PACK_W), lambda i, be, fi, nu: (i, 0)),
            pl.BlockSpec((1, 1, D, F2), lambda i, be, fi, nu: (layer, be[i], 0, 0)),
            pl.BlockSpec((1, 1, 1, F2), lambda i, be, fi, nu: (layer, be[i], 0, 0)),
            pl.BlockSpec((1, 1, MOE_FF, D), lambda i, be, fi, nu: (layer, be[i], 0, 0)),
            pl.BlockSpec((1, 1, 1, D), lambda i, be, fi, nu: (layer, be[i], 0, 0)),
        ],
        out_specs=pl.BlockSpec((MOE_BLOCK, PACK_W), lambda i, be, fi, nu: (i, 0)),
        scratch_shapes=[pltpu.VMEM((D, F2), BF16), pltpu.VMEM((MOE_FF, D), BF16)],
    )
    return pl.pallas_call(
        _expert_ffn_kernel,
        grid_spec=grid_spec,
        out_shape=jax.ShapeDtypeStruct((n_rows, PACK_W), jnp.int32),
        compiler_params=_params(1),
        name="expert_ffn",
    )(blk_e, blk_first, blk_valid, xp, w1, b1, w2, b2)


SC_CORES = 2
SC_SUBCORES = 16
SC_CHUNKS = (64, 32)


def _sc_chunk(*counts):
    n_workers = SC_CORES * SC_SUBCORES
    for chunk in SC_CHUNKS:
        if all(n % (chunk * n_workers) == 0 for n in counts):
            return chunk
    raise ValueError(f"row counts {counts} do not split over {n_workers} subcores")


def sc_gather(table, idx):
    n_idx = idx.shape[0]
    width = table.shape[1]
    n_workers = SC_CORES * SC_SUBCORES
    per_worker = n_idx // n_workers
    chunk = _sc_chunk(n_idx)
    n_chunks = per_worker // chunk
    assert n_chunks * chunk * n_workers == n_idx and n_chunks % 2 == 0
    mesh = plsc.VectorSubcoreMesh(core_axis_name="c", subcore_axis_name="s",
                                  num_cores=SC_CORES, num_subcores=SC_SUBCORES)

    def body(table_hbm, idx_hbm, out_hbm, idx_v, rows_v, gsem, wsem):
        wid = lax.axis_index("s") * SC_CORES + lax.axis_index("c")
        pltpu.sync_copy(idx_hbm.at[wid], idx_v)

        def gather(ci, slot):
            return pltpu.make_async_copy(table_hbm.at[idx_v.at[ci]], rows_v.at[slot], gsem.at[slot])

        def write(ci, slot):
            return pltpu.make_async_copy(rows_v.at[slot], out_hbm.at[ci, wid], wsem.at[slot])

        gather(0, 0).start()

        @pl.loop(0, n_chunks, step=2)
        def _(c0):
            for slot in range(2):
                ci = c0 + slot
                other = 1 - slot

                @pl.when(ci + 1 < n_chunks)
                def _():
                    @pl.when(ci >= 1)
                    def _():
                        write(ci - 1, other).wait()
                    gather(ci + 1, other).start()

                gather(ci, slot).wait()
                write(ci, slot).start()

        write(n_chunks - 2, 0).wait()
        write(n_chunks - 1, 1).wait()

    out = pl.kernel(
        body,
        out_type=jax.ShapeDtypeStruct((n_chunks, n_workers, chunk, width), table.dtype),
        mesh=mesh,
        scratch_types=[pltpu.VMEM((n_chunks, chunk), jnp.int32),
                       pltpu.VMEM((2, chunk, width), table.dtype),
                       pltpu.SemaphoreType.DMA((2,)),
                       pltpu.SemaphoreType.DMA((2,))],
        name="sc_gather",
    )(table, idx.reshape(n_chunks, n_workers, chunk).transpose(1, 0, 2))
    return out.reshape(n_idx, width)


def sc_dispatch(table, dest, pad_rows):
    n_tok, width = table.shape
    n_picks = dest.shape[0]
    n_pad = pad_rows.shape[0]
    n_workers = SC_CORES * SC_SUBCORES
    chunk = _sc_chunk(n_tok, n_pad)
    per_w = n_tok // chunk // n_workers
    pad_w = n_pad // chunk // n_workers
    assert per_w * chunk * n_workers == n_tok and pad_w * chunk * n_workers == n_pad
    mesh = plsc.VectorSubcoreMesh(core_axis_name="c", subcore_axis_name="s",
                                  num_cores=SC_CORES, num_subcores=SC_SUBCORES)

    def body(table_hbm, idx_hbm, pad_hbm, zero_hbm, out_hbm, idx_v, pad_v, rows_v, zero_v,
             rsem, ssem, zsem):
        wid = lax.axis_index("s") * SC_CORES + lax.axis_index("c")
        pltpu.sync_copy(idx_hbm.at[wid], idx_v)
        pltpu.sync_copy(pad_hbm.at[wid], pad_v)
        pltpu.sync_copy(zero_hbm, zero_v)

        def zero_fill(pc):
            return pltpu.make_async_copy(zero_v, out_hbm.at[pad_v.at[pc]], zsem)

        for pc in range(pad_w):
            zero_fill(pc).start()

        def scatter(ci, kk):
            return pltpu.make_async_copy(rows_v, out_hbm.at[idx_v.at[ci * n_picks + kk]], ssem)

        @pl.loop(0, per_w)
        def _(ci):
            pltpu.async_copy(table_hbm.at[ci, wid], rows_v, rsem).wait()
            for kk in range(n_picks):
                scatter(ci, kk).start()
            for kk in range(n_picks):
                scatter(ci, kk).wait()

        for pc in range(pad_w):
            zero_fill(pc).wait()

    idx = dest.reshape(n_picks, per_w, n_workers, chunk).transpose(2, 1, 0, 3)
    idx = idx.reshape(n_workers, per_w * n_picks, chunk)
    return pl.kernel(
        body,
        out_type=jax.ShapeDtypeStruct((n_tok * n_picks + n_pad, width), table.dtype),
        mesh=mesh,
        scratch_types=[pltpu.VMEM((per_w * n_picks, chunk), jnp.int32),
                       pltpu.VMEM((pad_w, chunk), jnp.int32),
                       pltpu.VMEM((chunk, width), table.dtype),
                       pltpu.VMEM((chunk, width), table.dtype),
                       pltpu.SemaphoreType.DMA, pltpu.SemaphoreType.DMA, pltpu.SemaphoreType.DMA],
        name="sc_dispatch",
    )(table.reshape(per_w, n_workers, chunk, width), idx,
      pad_rows.reshape(n_workers, pad_w, chunk), jnp.zeros((chunk, width), table.dtype))


def _combine_kernel(x_ref, *refs):
    y_refs, (gate_ref, gf_ref, o_ref) = refs[:TOP_K], refs[TOP_K:]
    gates = gate_ref[...]
    acc = gates[:, 0:1] * _unpack_rows(y_refs[0][...], F32)
    for kk in range(1, TOP_K):
        acc = acc + gates[:, kk:kk + 1] * _unpack_rows(y_refs[kk][...], F32)
    o_ref[...] = x_ref[...] + gf_ref[0] * acc


def moe_combine(lay, x, yg, gates, mod, drop_ctx):
    D = D_MODEL
    if drop_ctx:
        nbo = lay.nb - 1
        src = lambda r: (r // nbo) * lay.nb + 1 + r % nbo
        n_out = lay.n_batch * nbo
    else:
        src = lambda r: r
        n_out = lay.n_blocks
    y_specs = [pl.BlockSpec((TL, PACK_W), functools.partial(lambda kk, r: (kk * lay.n_blocks + src(r), 0), kk))
               for kk in range(TOP_K)]
    return pl.pallas_call(
        _combine_kernel,
        grid=(n_out,),
        in_specs=[pl.BlockSpec((TL, D), lambda r: (src(r), 0))] + y_specs + [
            pl.BlockSpec((TL, TOP_K), lambda r: (src(r), 0)),
            pl.BlockSpec((1, 1, D), lambda r: (lay.mod_row(src(r)), 0, 5))],
        out_specs=pl.BlockSpec((TL, D), lambda r: (r, 0)),
        out_shape=jax.ShapeDtypeStruct((n_out * TL, D), F32),
        compiler_params=_params(1),
        name="moe_combine",
    )(x, yg, yg, yg, yg, gates, mod)


def moe_route(top_e, rank, counts):
    T = top_e.shape[1]
    assert (T * TOP_K) % MOE_BLOCK == 0
    counts = counts.reshape(N_EXPERTS).astype(jnp.int32)
    padded = (counts + MOE_BLOCK - 1) // MOE_BLOCK * MOE_BLOCK
    padded_end = jnp.cumsum(padded)
    padded_start = padded_end - padded
    experts = jnp.arange(N_EXPERTS)
    start_of = jnp.sum(jnp.where(top_e[..., None] == experts, padded_start, 0), axis=-1)
    dest = (start_of + rank).astype(jnp.int32)
    n_pad = N_EXPERTS * MOE_BLOCK
    n_rows = T * TOP_K + n_pad
    n_blk = n_rows // MOE_BLOCK
    blk_start = jnp.arange(n_blk) * MOE_BLOCK
    blk_e = jnp.minimum(jnp.sum(padded_end[None, :] <= blk_start[:, None], axis=1), N_EXPERTS - 1)
    blk_e = blk_e.astype(jnp.int32)
    blk_first = jnp.concatenate([jnp.ones((1,), jnp.int32), (blk_e[1:] != blk_e[:-1]).astype(jnp.int32)])
    blk_hot = blk_e[:, None] == experts
    in_grp = blk_start - jnp.sum(jnp.where(blk_hot, padded_start, 0), axis=-1)
    blk_valid = jnp.clip(jnp.sum(jnp.where(blk_hot, counts, 0), axis=-1) - in_grp, 0, MOE_BLOCK)
    blk_valid = jnp.where(blk_start < padded_end[-1], blk_valid, 0).astype(jnp.int32)
    tail = padded - counts
    tail_end = jnp.cumsum(tail)
    j = jnp.arange(n_pad)
    owner = jnp.sum(tail_end[None, :] <= j[:, None], axis=1)
    base = padded_start + counts - (tail_end - tail)
    in_group = jnp.sum(jnp.where(owner[:, None] == experts, base, 0), axis=-1) + j
    pad_rows = jnp.where(j < tail_end[-1], in_group, padded_end[-1] + j - tail_end[-1])
    return dest, pad_rows.astype(jnp.int32), blk_e, blk_first, blk_valid


def kernel(x, c, ctx, c_ctx, norm_mix, norm_ffn, w_mod, b_mod, conv_w_in, conv_w, conv_w_out, ml_w_in, ml_b_gate, ml_norm, ml_w_out, mla_w_in, mla_q_norm, mla_kv_norm, mla_w_uq, mla_w_ukv, mla_qn_nope, mla_qn_rope, mla_kn_nope, mla_kn_rope, mla_w_out, moe_w_router, moe_b_router, moe_w1, moe_b1, moe_w2, moe_b2):
    Bn, n_lat, D = x.shape
    n_ctx = ctx.shape[1]
    assert D == D_MODEL and n_ctx == TL and n_lat % TL == 0
    assert (DEPTH - 1) % N_MIXERS == 0
    full = Layout(Bn, (n_ctx + n_lat) // TL, True)
    lat_only = Layout(Bn, n_lat // TL, False)
    mods = ada_all(c, c_ctx, w_mod, b_mod)
    tables = rope_tables(n_ctx, n_lat)
    b1_all = moe_b1.reshape(DEPTH, N_EXPERTS, 1, 2 * MOE_FF)
    b2_all = moe_b2.reshape(DEPTH, N_EXPERTS, 1, D)
    X = jnp.concatenate([ctx, x], axis=1).reshape(full.rows, D)
    for layer in range(DEPTH):
        kind, j = layer % N_MIXERS, layer // N_MIXERS
        lay = lat_only if layer == DEPTH - 1 else full
        mod = mods[layer]
        gain_a = norm_mix[layer].reshape(1, D)
        gain_f = norm_ffn[layer].reshape(1, D)
        if kind == 0:
            bg, v = conv_in(lay, X, gain_a, mod, conv_w_in[j].astype(BF16))
            pro, w_out, name = (v, bg, conv_w[j]), conv_w_out[j], "conv"
        elif kind == 1:
            w = ml_w_in[j]
            n_main = 2 * ML_QK + 2 * ML_V
            w_main = jnp.concatenate([w[:, :ML_QK] * ML_DQK ** -0.5, w[:, ML_QK:n_main]], axis=1)
            q, k, kt, v, og, g, gt = mlstm_in(lay, X, gain_a, mod, w_main.astype(BF16), w[:, n_main:],
                                              ml_b_gate[j])
            h_f, h_b = mlstm_scan(lay, q, k, kt, v, g, gt)
            pro, w_out, name = (h_f, h_b, og, ml_norm[j]), ml_w_out[j], "mlstm"
        else:
            q, k, v = mla_in(lay, X, gain_a, mod, mla_w_in[j], mla_q_norm[j], mla_kv_norm[j],
                             mla_w_uq[j], mla_w_ukv[j], mla_qn_nope[j], mla_qn_rope[j],
                             mla_kn_nope[j], mla_kn_rope[j], tables)
            pro, w_out, name = (mla_attention(lay, q, k, v),), mla_w_out[j], "mla"
        X, h2, top_e, gates, rank, counts = mixer_out(
            lay, name, pro, w_out, X, mod, gain_f, moe_w_router[layer], moe_b_router[layer])
        dest, pad_rows, blk_e, blk_first, blk_valid = moe_route(top_e, rank, counts)
        xp = sc_dispatch(h2, dest, pad_rows)
        yp = expert_ffn(layer, xp, blk_e, blk_first, blk_valid, moe_w1, b1_all, moe_w2, b2_all)
        yg = sc_gather(yp, dest.reshape(-1))
        X = moe_combine(lay, X, yg, gates.T, mod, drop_ctx=(layer == DEPTH - 2))
    return X.reshape(Bn, n_lat, D)
```

```python
import functools

import jax
import jax.numpy as jnp
from jax import lax
from jax.experimental import pallas as pl
from jax.experimental.pallas import tpu as pltpu
from jax.experimental.pallas import tpu_sc as plsc

D_MODEL = 1024
DEPTH = 4
GRID_W = 64
N_MIXERS = 3
N_ADA = 6
RMS_EPS = 1e-6
CONV_WIDTH = 3
ML_HEADS = 8
ML_DQK = 64
ML_DV = 128
ML_QK = ML_HEADS * ML_DQK
ML_V = ML_HEADS * ML_DV
GATE_CAP = 15.0
MLA_HEADS = 8
MLA_NOPE = 128
MLA_ROPE = 64
MLA_V = 128
MLA_Q_LORA = 384
MLA_KV_LORA = 256
MLA_SCALE = (MLA_NOPE + MLA_ROPE) ** -0.5
ROPE_THETA = 10000.0
N_EXPERTS = 32
TOP_K = 4
MOE_FF = D_MODEL
SWIGLU_ALPHA = 1.702
SWIGLU_LIMIT = 7.0
MOE_BLOCK = 512

TL = 256
LANES = 128
BF16_ROWS = 16
VMEM_LIMIT = 48 * 1024 * 1024
HI = lax.Precision.HIGHEST
F32 = jnp.float32
BF16 = jnp.bfloat16


def _params(n_axes):
    return pltpu.CompilerParams(dimension_semantics=("arbitrary",) * n_axes,
                                vmem_limit_bytes=VMEM_LIMIT)


def _rms(x, width=None):
    width = x.shape[-1] if width is None else width
    return x * lax.rsqrt(jnp.sum(x * x, axis=-1, keepdims=True) * (1.0 / width) + RMS_EPS)


def _norm_mod(x, gain, scale, shift):
    return _rms(x) * (gain * (1.0 + scale)) + shift


def _split_weight_lanes(w):
    hi = w.astype(BF16)
    lo = (w - hi.astype(F32)).astype(BF16)
    return jnp.pad(jnp.concatenate([hi, lo], axis=1), ((0, 0), (0, LANES - 2 * w.shape[1])))


PACK_W = D_MODEL // 2
HIGH_HALF = -65536


def _pack_rows(x):
    xb = x.astype(BF16).astype(F32)
    lo = lax.bitcast_convert_type(xb[:, :PACK_W], jnp.int32)
    hi = lax.bitcast_convert_type(xb[:, PACK_W:], jnp.int32)
    return hi | lax.shift_right_logical(lo, 16)


def _unpack_rows(w, dtype):
    lo = lax.bitcast_convert_type(lax.shift_left(w, 16), F32)
    hi = lax.bitcast_convert_type(w & HIGH_HALF, F32)
    return jnp.concatenate([lo.astype(dtype), hi.astype(dtype)], axis=1)


class Layout:
    def __init__(self, n_batch, nb, ctx_first):
        self.n_batch, self.nb, self.ctx_first = n_batch, nb, ctx_first
        self.n_blocks = n_batch * nb
        self.rows = self.n_blocks * TL

    def mod_row(self, r):
        b = r // self.nb
        return jnp.where(r % self.nb == 0, self.n_batch, b) if self.ctx_first else b

    def row_spec(self, width):
        return pl.BlockSpec((TL, width), lambda r: (r, 0))

    def mod_spec(self, piece):
        return pl.BlockSpec((1, 1, D_MODEL), lambda r: (self.mod_row(r), 0, piece))


def _const_spec(shape):
    return pl.BlockSpec(shape, lambda *_: (0,) * len(shape))


ADA_ROWS = 16
ADA_TN = 1536


def _ada_kernel(c_ref, w_ref, b_ref, o_ref):
    c = c_ref[...]
    s = c * jax.nn.sigmoid(c)
    o_ref[0] = jnp.dot(s, w_ref[0], precision=HI, preferred_element_type=F32) + b_ref[0]


def ada_all(c, c_ctx, w_mod, b_mod):
    Bn, D = c.shape
    assert Bn + 1 <= ADA_ROWS
    cond = jnp.zeros((ADA_ROWS, D), F32).at[:Bn].set(c).at[Bn].set(c_ctx)
    out = pl.pallas_call(
        _ada_kernel,
        grid=(DEPTH, N_ADA * D // ADA_TN),
        in_specs=[pl.BlockSpec((ADA_ROWS, D), lambda l, n: (0, 0)),
                  pl.BlockSpec((1, D, ADA_TN), lambda l, n: (l, 0, n)),
                  pl.BlockSpec((1, 1, ADA_TN), lambda l, n: (l, 0, n))],
        out_specs=pl.BlockSpec((1, ADA_ROWS, ADA_TN), lambda l, n: (l, 0, n)),
        out_shape=jax.ShapeDtypeStruct((DEPTH, ADA_ROWS, N_ADA * D), F32),
        compiler_params=_params(2),
        name="ada_mod",
    )(cond, w_mod, b_mod.reshape(DEPTH, 1, N_ADA * D))
    return out[:, :Bn + 1, None, :]


def _conv_in_kernel(x_ref, gain_ref, sc_ref, sh_ref, w_ref, bg_ref, v_ref):
    D = D_MODEL
    h = _norm_mod(x_ref[...], gain_ref[...], sc_ref[0], sh_ref[0]).astype(BF16)
    p = jnp.dot(h, w_ref[...], preferred_element_type=F32)
    bg_ref[...] = p[:, :D].astype(BF16)
    v_ref[...] = (p[:, D:2 * D] * p[:, 2 * D:]).astype(BF16)


def conv_in(lay, x, gain, mod, w_in):
    D = D_MODEL
    sds = jax.ShapeDtypeStruct((lay.rows, D), BF16)
    return pl.pallas_call(
        _conv_in_kernel,
        grid=(lay.n_blocks,),
        in_specs=[lay.row_spec(D), _const_spec((1, D)), lay.mod_spec(1), lay.mod_spec(0),
                  _const_spec((D, 3 * D))],
        out_specs=[lay.row_spec(D), lay.row_spec(D)],
        out_shape=[sds, sds],
        compiler_params=_params(1),
        name="conv_in",
    )(x, gain, mod, mod, w_in)


ML_T = 256
ML_SW = 2 * ML_DV
ML_NG = 4 * ML_HEADS


LOG2E = 1.4426950408889634


def _gate_act(g, is_forget):
    g = GATE_CAP * jnp.tanh(g * (1.0 / GATE_CAP))
    log_sig = jnp.minimum(g, 0.0) - jnp.log(1.0 + jnp.exp(-jnp.abs(g)))
    return jnp.where(is_forget, log_sig, g) * LOG2E


def _mlstm_in_kernel(x_ref, gain_ref, sc_ref, sh_ref, w_ref, wg_ref, bg_ref,
                     q_ref, k_ref, kt_ref, v_ref, og_ref, g_ref, gt_ref):
    h = _norm_mod(x_ref[...], gain_ref[...], sc_ref[0], sh_ref[0])
    hb = h.astype(BF16)
    p = jnp.dot(hb, w_ref[...], preferred_element_type=F32)
    q_ref[...] = p[:, :ML_QK].astype(BF16)
    k_ref[...] = p[:, ML_QK:2 * ML_QK].astype(BF16)
    kt_ref[...] = p[:, ML_QK:2 * ML_QK].T.astype(BF16)
    v_ref[...] = p[:, 2 * ML_QK:2 * ML_QK + ML_V].astype(BF16)
    og_ref[...] = jax.nn.sigmoid(p[:, 2 * ML_QK + ML_V:]).astype(BF16)
    h_lo = (h - hb.astype(F32)).astype(BF16)
    g = jnp.dot(jnp.concatenate([hb, h_lo, hb], axis=1), wg_ref[...],
                preferred_element_type=F32) + bg_ref[...]
    col = lax.broadcasted_iota(jnp.int32, g.shape, 1)
    g = _gate_act(g, (col // ML_HEADS) % 2 == 1)
    g_ref[...] = g[:, :ML_NG]
    gt_ref[...] = g.T[:ML_NG]


def mlstm_in(lay, x, gain, mod, w_main, w_g, b_g):
    D = D_MODEL
    n_main = 2 * ML_QK + 2 * ML_V
    g_hi = w_g.astype(BF16)
    g_lo = (w_g - g_hi.astype(F32)).astype(BF16)
    wg3 = jnp.pad(jnp.concatenate([g_hi, g_hi, g_lo], axis=0), ((0, 0), (0, LANES - ML_NG)))
    bf = lambda w: jax.ShapeDtypeStruct((lay.rows, w), BF16)
    return pl.pallas_call(
        _mlstm_in_kernel,
        grid=(lay.n_blocks,),
        in_specs=[lay.row_spec(D), _const_spec((1, D)), lay.mod_spec(1), lay.mod_spec(0),
                  _const_spec((D, n_main)), _const_spec((3 * D, LANES)), _const_spec((1, LANES))],
        out_specs=[lay.row_spec(ML_QK), lay.row_spec(ML_QK), pl.BlockSpec((ML_QK, TL), lambda r: (0, r)),
                   lay.row_spec(ML_V), lay.row_spec(ML_V),
                   lay.row_spec(ML_NG), pl.BlockSpec((ML_NG, TL), lambda r: (0, r))],
        out_shape=[bf(ML_QK), bf(ML_QK), jax.ShapeDtypeStruct((ML_QK, lay.rows), BF16),
                   bf(ML_V), bf(ML_V),
                   jax.ShapeDtypeStruct((lay.rows, ML_NG), F32),
                   jax.ShapeDtypeStruct((ML_NG, lay.rows), F32)],
        compiler_params=_params(1),
        name="mlstm_in",
    )(x, gain, mod, mod, w_main, wg3, _pad_lanes(b_g))


def _split3(x):
    hi = x.astype(BF16)
    r1 = x - hi.astype(F32)
    mid = r1.astype(BF16)
    lo = (r1 - mid.astype(F32)).astype(BF16)
    return hi, mid, lo


def _dot_exact01(x, sel01, x_on_left):
    sel = sel01.astype(BF16)
    parts = [jnp.dot(p, sel, preferred_element_type=F32) if x_on_left
             else jnp.dot(sel, p, preferred_element_type=F32) for p in _split3(x)]
    return parts[0] + parts[1] + parts[2]


def _mlstm_dir(reverse, q_ref, k_ref, kt_ref, v_ref, g_ref, gt_ref, o_ref, s_ref, m_ref):
    T = ML_T
    row = lax.broadcasted_iota(jnp.int32, (T, T), 0)
    col = lax.broadcasted_iota(jnp.int32, (T, T), 1)
    mask = (col >= row) if reverse else (col <= row)
    gt = gt_ref[...]
    bc = _dot_exact01(g_ref[...], mask, x_on_left=False)
    br = _dot_exact01(gt, (row >= col) if reverse else (row <= col), x_on_left=True)
    gi, gf = (2 * ML_HEADS, 3 * ML_HEADS) if reverse else (0, ML_HEADS)
    lane = lax.broadcasted_iota(jnp.int32, (T, 2 * ML_DQK), 1)
    sub = lax.broadcasted_iota(jnp.int32, (2 * ML_DQK, T), 0)
    pick = (lax.broadcasted_iota(jnp.int32, (ML_NG, ML_V), 0)
            == gf + lax.broadcasted_iota(jnp.int32, (ML_NG, ML_V), 1) // ML_DV)
    b_all = _dot_exact01(bc, pick, x_on_left=True)
    ones = jnp.ones((T, ML_DV), BF16)
    heads = range(ML_HEADS)
    qm, vx, s_raw = {}, {}, {}
    for h in heads:
        pair = (h // 2) * 2 * ML_DQK
        own = (lane >= ML_DQK) if (h % 2) else (lane < ML_DQK)
        qp = q_ref[:, pair:pair + 2 * ML_DQK]
        qm[h] = jnp.where(own, qp, jnp.zeros_like(qp))
        vx[h] = jnp.concatenate([v_ref[:, h * ML_DV:(h + 1) * ML_DV], ones], axis=1)
        s_raw[h] = lax.dot_general(qm[h], k_ref[:, pair:pair + 2 * ML_DQK], (((1,), (1,)), ((), ())),
                                   preferred_element_type=F32)
    p, a, u_rep, m_prev, s_prev = {}, {}, {}, {}, {}
    for h in heads:
        c_row = gt[gi + h:gi + h + 1, :] - br[gf + h:gf + h + 1, :]
        e = jnp.where(mask, c_row, -jnp.inf)
        m_prev[h] = m_ref[h][0:1, 0:1]
        u = jnp.maximum(m_prev[h], jnp.max(e, axis=1, keepdims=True))
        p[h] = (s_raw[h] * jnp.exp2(e - u)).astype(BF16)
        u_rep[h] = jnp.broadcast_to(u, (T, ML_DV))
        a[h] = jnp.exp2(m_prev[h] - u_rep[h])
        s_prev[h] = s_ref[h]
    for h in heads:
        r = jnp.dot(p[h], vx[h], preferred_element_type=F32)
        qs = jnp.dot(qm[h], s_prev[h].astype(BF16), preferred_element_type=F32)
        num = r[:, :ML_DV] + a[h] * qs[:, :ML_DV]
        den = r[:, ML_DV:] + a[h] * qs[:, ML_DV:]
        floor = jnp.exp2(-(b_all[:, h * ML_DV:(h + 1) * ML_DV] + u_rep[h]))
        o_ref[:, h * ML_DV:(h + 1) * ML_DV] = (num / jnp.maximum(jnp.abs(den), floor)).astype(o_ref.dtype)
    for h in heads:
        pair = (h // 2) * 2 * ML_DQK
        b_row = br[gf + h:gf + h + 1, :]
        tot = b_row[:, 0:1] if reverse else b_row[:, T - 1:T]
        g_row = tot - b_row + gt[gi + h:gi + h + 1, :]
        m_new = jnp.maximum(tot + m_prev[h], jnp.max(g_row, axis=1, keepdims=True))
        decay = jnp.exp2(tot + m_prev[h] - m_new)
        wk = jnp.exp2(g_row - m_new)
        own_t = (sub >= ML_DQK) if (h % 2) else (sub < ML_DQK)
        kt = kt_ref[pair:pair + 2 * ML_DQK, :].astype(F32)
        kw = jnp.where(own_t, kt * wk, 0.0).astype(BF16)
        s_ref[h] = decay * s_prev[h] + jnp.dot(kw, vx[h], preferred_element_type=F32)
        m_ref[h] = jnp.broadcast_to(m_new, m_ref.shape[1:])


def _mlstm_scan_kernel(qf_ref, kf_ref, ktf_ref, vf_ref, gf_ref, gtf_ref,
                       qb_ref, kb_ref, ktb_ref, vb_ref, gb_ref, gtb_ref,
                       of_ref, ob_ref, sf_ref, mf_ref, sb_ref, mb_ref):
    @pl.when(pl.program_id(1) == 0)
    def _():
        sf_ref[...] = jnp.zeros_like(sf_ref)
        mf_ref[...] = jnp.zeros_like(mf_ref)
        sb_ref[...] = jnp.zeros_like(sb_ref)
        mb_ref[...] = jnp.zeros_like(mb_ref)

    _mlstm_dir(False, qf_ref, kf_ref, ktf_ref, vf_ref, gf_ref, gtf_ref, of_ref, sf_ref, mf_ref)
    _mlstm_dir(True, qb_ref, kb_ref, ktb_ref, vb_ref, gb_ref, gtb_ref, ob_ref, sb_ref, mb_ref)


def mlstm_scan(lay, q, k, kt, v, g, gt):
    assert lay.ctx_first
    per = TL // ML_T
    nb = lay.nb * per
    rev = lambda j: jnp.where(j < per, per - 1 - j, nb + per - 1 - j)
    fwd = lambda b, j: (b * nb + j, 0)
    bwd = lambda b, j: (b * nb + rev(j), 0)
    fwd_t = lambda b, j: (0, b * nb + j)
    bwd_t = lambda b, j: (0, b * nb + rev(j))

    def specs(im, imt):
        return [pl.BlockSpec((ML_T, ML_QK), im), pl.BlockSpec((ML_T, ML_QK), im),
                pl.BlockSpec((ML_QK, ML_T), imt), pl.BlockSpec((ML_T, ML_V), im),
                pl.BlockSpec((ML_T, ML_NG), im), pl.BlockSpec((ML_NG, ML_T), imt)]

    out_sds = jax.ShapeDtypeStruct((lay.rows, ML_V), BF16)
    state = [pltpu.VMEM((ML_HEADS, 2 * ML_DQK, ML_SW), F32),
             pltpu.VMEM((ML_HEADS, 8, LANES), F32)]
    return pl.pallas_call(
        _mlstm_scan_kernel,
        grid=(lay.n_batch, nb),
        in_specs=specs(fwd, fwd_t) + specs(bwd, bwd_t),
        out_specs=[pl.BlockSpec((ML_T, ML_V), fwd), pl.BlockSpec((ML_T, ML_V), bwd)],
        out_shape=[out_sds, out_sds],
        scratch_shapes=state + state,
        compiler_params=_params(2),
        name="mlstm_scan",
    )(q, k, kt, v, g, gt, q, k, kt, v, g, gt)


MLA_QK_PAD = 256
MLA_VW = 2 * MLA_V
MLA_IN_PAD = MLA_Q_LORA + MLA_KV_LORA + LANES
ROPE_HALF = MLA_ROPE // 4


def _mla_in_kernel(x_ref, gain_ref, sc_ref, sh_ref, win_ref, wuq_ref, wukv_ref, qn_ref, kvn_ref,
                   qnn_ref, qnr_ref, knn_ref, knr_ref, cos_ref, sa_ref, sb_ref,
                   q_out, k_out, v_out):
    h = _norm_mod(x_ref[...], gain_ref[...], sc_ref[0], sh_ref[0]).astype(BF16)
    p = jnp.dot(h, win_ref[...], preferred_element_type=F32)
    cq = _rms(p[:, :MLA_Q_LORA]) * qn_ref[...]
    ckv = _rms(p[:, MLA_Q_LORA:MLA_Q_LORA + MLA_KV_LORA]) * kvn_ref[...]
    kr = p[:, MLA_Q_LORA + MLA_KV_LORA:]
    q = jnp.dot(cq.astype(BF16), wuq_ref[...], preferred_element_type=F32)
    kv = jnp.dot(ckv.astype(BF16), wukv_ref[...], preferred_element_type=F32)
    nv = MLA_HEADS * MLA_NOPE
    ones_col = (lax.broadcasted_iota(jnp.int32, (TL, MLA_VW - MLA_V), 1) == 0).astype(BF16)
    cos, sa, sb = cos_ref[...], sa_ref[...], sb_ref[...]

    def rope(xp):
        return (xp * cos + pltpu.roll(xp, LANES - ROPE_HALF, 1) * sa + pltpu.roll(xp, ROPE_HALF, 1) * sb)

    heads = range(MLA_HEADS)
    kr = _rms(kr, MLA_ROPE) * knr_ref[...]
    qn = {hd: _rms(q[:, hd * MLA_QK_PAD:hd * MLA_QK_PAD + MLA_NOPE]) * qnn_ref[...] for hd in heads}
    qr = {hd: _rms(q[:, hd * MLA_QK_PAD + MLA_NOPE:(hd + 1) * MLA_QK_PAD], MLA_ROPE) * qnr_ref[...]
          for hd in heads}
    kn = {hd: _rms(kv[:, hd * MLA_NOPE:(hd + 1) * MLA_NOPE]) * knn_ref[...] for hd in heads}
    kr = rope(kr).astype(BF16)
    qr = {hd: rope(qr[hd]) for hd in heads}
    for hd in heads:
        c0 = hd * MLA_QK_PAD
        q_out[:, c0:c0 + MLA_NOPE] = qn[hd].astype(BF16)
        q_out[:, c0 + MLA_NOPE:c0 + MLA_QK_PAD] = qr[hd].astype(BF16)
        v0 = nv + hd * MLA_V
        v_out[:, hd * MLA_VW:hd * MLA_VW + MLA_V] = kv[:, v0:v0 + MLA_V].astype(BF16)
        v_out[:, hd * MLA_VW + MLA_V:(hd + 1) * MLA_VW] = ones_col
        k_out[:, c0:c0 + MLA_NOPE] = kn[hd].astype(BF16)
        k_out[:, c0 + MLA_NOPE:c0 + MLA_QK_PAD] = kr


def _pad_lanes(g):
    return jnp.pad(g, (0, LANES - g.shape[0])).reshape(1, LANES)


def mla_in(lay, x, gain, mod, w_in, q_norm, kv_norm, w_uq, w_ukv, qn_nope, qn_rope, kn_nope, kn_rope,
           tables):
    D = D_MODEL
    Hn = MLA_HEADS
    win = jnp.pad(w_in, ((0, 0), (0, MLA_IN_PAD - w_in.shape[1]))).astype(BF16)
    wuq = jnp.pad(w_uq.reshape(MLA_Q_LORA, Hn, MLA_NOPE + MLA_ROPE),
                  ((0, 0), (0, 0), (0, MLA_QK_PAD - MLA_NOPE - MLA_ROPE)))
    wuq = wuq.reshape(MLA_Q_LORA, Hn * MLA_QK_PAD).astype(BF16)
    wkv = w_ukv.reshape(MLA_KV_LORA, Hn, MLA_NOPE + MLA_V)
    wukv = jnp.concatenate([wkv[:, :, :MLA_NOPE].reshape(MLA_KV_LORA, Hn * MLA_NOPE),
                            wkv[:, :, MLA_NOPE:].reshape(MLA_KV_LORA, Hn * MLA_V)], axis=1).astype(BF16)
    nb = lay.nb
    tab_spec = pl.BlockSpec((TL, LANES), lambda r: (r % nb, 0))
    bf = lambda w: jax.ShapeDtypeStruct((lay.rows, w), BF16)
    return pl.pallas_call(
        _mla_in_kernel,
        grid=(lay.n_blocks,),
        in_specs=[lay.row_spec(D), _const_spec((1, D)), lay.mod_spec(1), lay.mod_spec(0),
                  _const_spec(win.shape), _const_spec(wuq.shape), _const_spec(wukv.shape),
                  _const_spec((1, MLA_Q_LORA)), _const_spec((1, MLA_KV_LORA)),
                  _const_spec((1, LANES)), _const_spec((1, LANES)), _const_spec((1, LANES)),
                  _const_spec((1, LANES)), tab_spec, tab_spec, tab_spec],
        out_specs=[lay.row_spec(Hn * MLA_QK_PAD), lay.row_spec(Hn * MLA_QK_PAD), lay.row_spec(Hn * MLA_VW)],
        out_shape=[bf(Hn * MLA_QK_PAD), bf(Hn * MLA_QK_PAD), bf(Hn * MLA_VW)],
        compiler_params=_params(1),
        name="mla_in",
    )(x, gain, mod, mod, win, wuq, wukv, q_norm.reshape(1, -1), kv_norm.reshape(1, -1),
      (qn_nope * (MLA_SCALE * LOG2E)).reshape(1, -1), _pad_lanes(qn_rope * (MLA_SCALE * LOG2E)),
      kn_nope.reshape(1, -1), _pad_lanes(kn_rope), *tables)


def rope_tables(n_ctx, n_lat):
    n_freq = MLA_ROPE // 4
    inv = ROPE_THETA ** (-jnp.arange(n_freq, dtype=F32) / n_freq)
    t = jnp.arange(n_lat)
    a_r = (t // GRID_W).astype(F32)[:, None] * inv
    a_c = (t % GRID_W).astype(F32)[:, None] * inv
    ang = jnp.concatenate([a_r, a_r, a_c, a_c], axis=-1)
    ang = jnp.concatenate([jnp.zeros((n_ctx, MLA_ROPE), F32), ang], axis=0)
    cos, sin = jnp.cos(ang), jnp.sin(ang)
    low = (jnp.arange(MLA_ROPE) % (2 * ROPE_HALF)) < ROPE_HALF
    pad = lambda a: jnp.pad(a, ((0, 0), (0, LANES - MLA_ROPE)))
    return pad(cos), pad(jnp.where(low, -sin, 0.0)), pad(jnp.where(low, 0.0, sin))


MLA_HPS = 2


def _mla_attn_kernel(q_ref, k_ref, v_ref, o_ref, *, n_ctx):
    def attend(n_keys):
        heads = range(MLA_HPS)
        s, p = {}, {}
        for hd in heads:
            q = q_ref[0, :, hd * MLA_QK_PAD:(hd + 1) * MLA_QK_PAD]
            k = k_ref[0, :n_keys, hd * MLA_QK_PAD:(hd + 1) * MLA_QK_PAD]
            s[hd] = lax.dot_general(q, k, (((1,), (1,)), ((), ())), preferred_element_type=F32)
        for hd in heads:
            p[hd] = jnp.exp2(s[hd] - s[hd].max(axis=1, keepdims=True)).astype(BF16)
        for hd in heads:
            v = v_ref[0, :n_keys, hd * MLA_VW:(hd + 1) * MLA_VW]
            r = jnp.dot(p[hd], v, preferred_element_type=F32)
            o_ref[0, :, hd * MLA_V:(hd + 1) * MLA_V] = (
                r[:, :MLA_V] / r[:, MLA_V:MLA_V + 1]).astype(o_ref.dtype)

    @pl.when(pl.program_id(2) == 0)
    def _():
        attend(n_ctx)

    @pl.when(pl.program_id(2) > 0)
    def _():
        attend(k_ref.shape[1])


def mla_attention(lay, q, k, v):
    assert lay.ctx_first
    Bn, S = lay.n_batch, lay.nb * TL
    q3, k3, v3 = (a.reshape(Bn, S, a.shape[-1]) for a in (q, k, v))
    out = pl.pallas_call(
        functools.partial(_mla_attn_kernel, n_ctx=TL),
        grid=(Bn, MLA_HEADS // MLA_HPS, lay.nb),
        in_specs=[pl.BlockSpec((1, TL, MLA_HPS * MLA_QK_PAD), lambda b, h, i: (b, i, h)),
                  pl.BlockSpec((1, S, MLA_HPS * MLA_QK_PAD), lambda b, h, i: (b, 0, h)),
                  pl.BlockSpec((1, S, MLA_HPS * MLA_VW), lambda b, h, i: (b, 0, h))],
        out_specs=pl.BlockSpec((1, TL, MLA_HPS * MLA_V), lambda b, h, i: (b, i, h)),
        out_shape=jax.ShapeDtypeStruct((Bn, S, MLA_HEADS * MLA_V), BF16),
        compiler_params=_params(3),
        name="mla_attention",
    )(q3, k3, v3)
    return out.reshape(lay.rows, MLA_HEADS * MLA_V)


N_PROLOGUE = {"conv": 5, "mlstm": 4, "mla": 1}


def _mixer_out_kernel(*refs, kind, nb, ctx_first):
    n_pro = N_PROLOGUE[kind]
    pro = refs[:n_pro]
    (wout_ref, x_ref, ga_ref, gain_ref, sc_ref, sh_ref, wr_ref, br_ref,
     xo_ref, h2_ref, te_ref, gate_ref, rank_ref, cnt_ref, carry_ref) = refs[n_pro:]
    r = pl.program_id(0)

    if kind == "conv":
        vprev_ref, v_ref, vnext_ref, bg_ref, cw_ref = pro
        j = r % nb
        first = (j == 0) | (j == 1) if ctx_first else (j == 0)
        last = (j == nb - 1) | (j == 0) if ctx_first else (j == nb - 1)
        v = v_ref[...].astype(F32)
        rows = lax.broadcasted_iota(jnp.int32, (TL, 1), 0)
        prev_row = jnp.where(first, 0.0, vprev_ref[BF16_ROWS - 1:BF16_ROWS, :].astype(F32))
        next_row = jnp.where(last, 0.0, vnext_ref[0:1, :].astype(F32))
        up = jnp.where(rows == 0, prev_row, pltpu.roll(v, 1, 0))
        dn = jnp.where(rows == TL - 1, next_row, pltpu.roll(v, TL - 1, 0))
        cw = cw_ref[...]
        a = bg_ref[...].astype(F32) * (up * cw[0:1] + v * cw[1:2] + dn * cw[2:3])
    elif kind == "mlstm":
        hf_ref, hb_ref, og_ref, ng_ref = pro
        hh = hf_ref[...].astype(F32) + hb_ref[...].astype(F32)
        a = jnp.concatenate([_rms(hh[:, h * ML_DV:(h + 1) * ML_DV]) for h in range(ML_HEADS)], axis=1)
        a = a * ng_ref[...] * og_ref[...].astype(F32)
    else:
        a = pro[0][...]

    y = jnp.dot(a.astype(BF16), wout_ref[...], preferred_element_type=F32)
    xn = x_ref[...] + ga_ref[0] * y
    xo_ref[...] = xn
    h2 = _norm_mod(xn, gain_ref[...], sc_ref[0], sh_ref[0])
    h2_ref[...] = _pack_rows(h2)
    lt = jnp.dot(h2.astype(BF16), wr_ref[...], preferred_element_type=F32).T
    logits = lt[:N_EXPERTS] + lt[N_EXPERTS:2 * N_EXPERTS] + br_ref[...]

    sub = lax.broadcasted_iota(jnp.int32, (N_EXPERTS, TL), 0)
    sub_k = lax.broadcasted_iota(jnp.int32, (TOP_K, TL), 0)
    work = logits
    sel = jnp.zeros((N_EXPERTS, TL), F32)
    top_e = jnp.zeros((TOP_K, TL), jnp.int32)
    top_v = jnp.zeros((TOP_K, TL), F32)
    picks = []
    for kk in range(TOP_K):
        m = work.max(axis=0, keepdims=True)
        idx = jnp.min(jnp.where(work == m, sub, N_EXPERTS), axis=0, keepdims=True)
        hit = sub == idx
        picks.append(hit)
        sel = jnp.where(hit, 1.0, sel)
        work = jnp.where(hit, -jnp.inf, work)
        top_e = jnp.where(sub_k == kk, idx, top_e)
        top_v = jnp.where(sub_k == kk, m, top_v)
    ex = jnp.exp(top_v - top_v[0:1])
    gate_ref[...] = ex / ex.sum(axis=0, keepdims=True)
    te_ref[...] = top_e

    @pl.when(r == 0)
    def _():
        carry_ref[...] = jnp.zeros_like(carry_ref)

    tr = lax.broadcasted_iota(jnp.int32, (TL, TL), 0)
    tc = lax.broadcasted_iota(jnp.int32, (TL, TL), 1)
    before = jnp.dot(sel.astype(BF16), (tr < tc).astype(BF16), preferred_element_type=F32)
    pos = before + carry_ref[...]
    rank = jnp.zeros((TOP_K, TL), F32)
    for kk in range(TOP_K):
        rk = jnp.sum(jnp.where(picks[kk], pos, 0.0), axis=0, keepdims=True)
        rank = jnp.where(sub_k == kk, rk, rank)
    rank_ref[...] = rank.astype(jnp.int32)
    total = carry_ref[...] + jnp.sum(sel, axis=1, keepdims=True)
    carry_ref[...] = total
    cnt_ref[...] = total


def mixer_out(lay, kind, pro_args, w_out, x, mod, gain_f, w_r, b_r):
    D = D_MODEL
    nb = lay.nb
    if kind == "conv":
        v, bg, cw = pro_args
        per = TL // BF16_ROWS
        last_tile = lay.rows // BF16_ROWS - 1
        pro_specs = [pl.BlockSpec((BF16_ROWS, D), lambda r: (jnp.maximum(r * per - 1, 0), 0)),
                     lay.row_spec(D),
                     pl.BlockSpec((BF16_ROWS, D), lambda r: (jnp.minimum((r + 1) * per, last_tile), 0)),
                     lay.row_spec(D), _const_spec((CONV_WIDTH, D))]
        pro_in = [v, v, v, bg, cw]
    elif kind == "mlstm":
        h_f, h_b, og, ng = pro_args
        pro_specs = [lay.row_spec(ML_V), lay.row_spec(ML_V), lay.row_spec(ML_V), _const_spec((1, ML_V))]
        pro_in = [h_f, h_b, og, ng.reshape(1, ML_V)]
    else:
        pro_specs = [lay.row_spec(D)]
        pro_in = list(pro_args)
    k_in = w_out.shape[0]
    small = lambda dt: jax.ShapeDtypeStruct((TOP_K, lay.rows), dt)
    small_spec = pl.BlockSpec((TOP_K, TL), lambda r: (0, r))
    return pl.pallas_call(
        functools.partial(_mixer_out_kernel, kind=kind, nb=nb, ctx_first=lay.ctx_first),
        grid=(lay.n_blocks,),
        in_specs=pro_specs + [_const_spec((k_in, D)), lay.row_spec(D), lay.mod_spec(2),
                              _const_spec((1, D)), lay.mod_spec(4), lay.mod_spec(3),
                              _const_spec((D, LANES)), _const_spec((N_EXPERTS, 1))],
        out_specs=[lay.row_spec(D), lay.row_spec(PACK_W), small_spec, small_spec, small_spec,
                   _const_spec((N_EXPERTS, 1))],
        out_shape=[jax.ShapeDtypeStruct((lay.rows, D), F32),
                   jax.ShapeDtypeStruct((lay.rows, PACK_W), jnp.int32),
                   small(jnp.int32), small(F32), small(jnp.int32),
                   jax.ShapeDtypeStruct((N_EXPERTS, 1), F32)],
        scratch_shapes=[pltpu.VMEM((N_EXPERTS, 1), F32)],
        compiler_params=_params(1),
        name="mixer_out_" + kind,
    )(*pro_in, w_out.astype(BF16), x, mod, gain_f, mod, mod, _split_weight_lanes(w_r),
      b_r.reshape(N_EXPERTS, 1))


def _expert_ffn_kernel(blk_e_ref, first_ref, valid_ref, x_ref, w1_ref, b1_ref, w2_ref, b2_ref, o_ref,
                       w1b_ref, w2b_ref):
    del blk_e_ref
    i = pl.program_id(0)

    @pl.when(first_ref[i] == 1)
    def _():
        w1b_ref[...] = w1_ref[0, 0].astype(BF16)
        w2b_ref[...] = w2_ref[0, 0].astype(BF16)

    @pl.when(valid_ref[i] > 0)
    def _():
        x = _unpack_rows(x_ref[...], BF16)
        h = jnp.dot(x, w1b_ref[...], preferred_element_type=F32) + b1_ref[0, 0]
        glu = jnp.minimum(h[:, :MOE_FF], SWIGLU_LIMIT)
        lin = jnp.clip(h[:, MOE_FF:], -SWIGLU_LIMIT, SWIGLU_LIMIT)
        act = glu * jax.nn.sigmoid(SWIGLU_ALPHA * glu) * (lin + 1.0)
        y = jnp.dot(act.astype(BF16), w2b_ref[...], preferred_element_type=F32)
        o_ref[...] = _pack_rows(y + b2_ref[0, 0])

    @pl.when(valid_ref[i] == 0)
    def _():
        o_ref[...] = jnp.zeros_like(o_ref)


def expert_ffn(layer, xp, blk_e, blk_first, blk_valid, w1, b1, w2, b2):
    n_rows = xp.shape[0]
    D, F2 = D_MODEL, 2 * MOE_FF
    n_blk = n_rows // MOE_BLOCK
    grid_spec = pltpu.PrefetchScalarGridSpec(
        num_scalar_prefetch=3,
        grid=(n_blk,),
        in_specs=[
            pl.BlockSpec((MOE_BLOCK, PACK_W), lambda i, be, fi, nu: (i, 0)),
            pl.BlockSpec((1, 1, D, F2), lambda i, be, fi, nu: (layer, be[i], 0, 0)),
            pl.BlockSpec((1, 1, 1, F2), lambda i, be, fi, nu: (layer, be[i], 0, 0)),
            pl.BlockSpec((1, 1, MOE_FF, D), lambda i, be, fi, nu: (layer, be[i], 0, 0)),
            pl.BlockSpec((1, 1, 1, D), lambda i, be, fi, nu: (layer, be[i], 0, 0)),
        ],
        out_specs=pl.BlockSpec((MOE_BLOCK, PACK_W), lambda i, be, fi, nu: (i, 0)),
        scratch_shapes=[pltpu.VMEM((D, F2), BF16), pltpu.VMEM((MOE_FF, D), BF16)],
    )
    return pl.pallas_call(
        _expert_ffn_kernel,
        grid_spec=grid_spec,
        out_shape=jax.ShapeDtypeStruct((n_rows, PACK_W), jnp.int32),
        compiler_params=_params(1),
        name="expert_ffn",
    )(blk_e, blk_first, blk_valid, xp, w1, b1, w2, b2)


SC_CORES = 2
SC_SUBCORES = 16
SC_CHUNKS = (64, 32)


def _sc_chunk(*counts):
    n_workers = SC_CORES * SC_SUBCORES
    for chunk in SC_CHUNKS:
        if all(n % (chunk * n_workers) == 0 for n in counts):
            return chunk
    raise ValueError(f"row counts {counts} do not split over {n_workers} subcores")


def sc_gather(table, idx):
    n_idx = idx.shape[0]
    width = table.shape[1]
    n_workers = SC_CORES * SC_SUBCORES
    per_worker = n_idx // n_workers
    chunk = _sc_chunk(n_idx)
    n_chunks = per_worker // chunk
    assert n_chunks * chunk * n_workers == n_idx and n_chunks % 2 == 0
    mesh = plsc.VectorSubcoreMesh(core_axis_name="c", subcore_axis_name="s",
                                  num_cores=SC_CORES, num_subcores=SC_SUBCORES)

    def body(table_hbm, idx_hbm, out_hbm, idx_v, rows_v, gsem, wsem):
        wid = lax.axis_index("s") * SC_CORES + lax.axis_index("c")
        pltpu.sync_copy(idx_hbm.at[wid], idx_v)

        def gather(ci, slot):
            return pltpu.make_async_copy(table_hbm.at[idx_v.at[ci]], rows_v.at[slot], gsem.at[slot])

        def write(ci, slot):
            return pltpu.make_async_copy(rows_v.at[slot], out_hbm.at[ci, wid], wsem.at[slot])

        gather(0, 0).start()

        @pl.loop(0, n_chunks, step=2)
        def _(c0):
            for slot in range(2):
                ci = c0 + slot
                other = 1 - slot

                @pl.when(ci + 1 < n_chunks)
                def _():
                    @pl.when(ci >= 1)
                    def _():
                        write(ci - 1, other).wait()
                    gather(ci + 1, other).start()

                gather(ci, slot).wait()
                write(ci, slot).start()

        write(n_chunks - 2, 0).wait()
        write(n_chunks - 1, 1).wait()

    out = pl.kernel(
        body,
        out_type=jax.ShapeDtypeStruct((n_chunks, n_workers, chunk, width), table.dtype),
        mesh=mesh,
        scratch_types=[pltpu.VMEM((n_chunks, chunk), jnp.int32),
                       pltpu.VMEM((2, chunk, width), table.dtype),
                       pltpu.SemaphoreType.DMA((2,)),
                       pltpu.SemaphoreType.DMA((2,))],
        name="sc_gather",
    )(table, idx.reshape(n_chunks, n_workers, chunk).transpose(1, 0, 2))
    return out.reshape(n_idx, width)


def sc_dispatch(table, dest, pad_rows):
    n_tok, width = table.shape
    n_picks = dest.shape[0]
    n_pad = pad_rows.shape[0]
    n_workers = SC_CORES * SC_SUBCORES
    chunk = _sc_chunk(n_tok, n_pad)
    per_w = n_tok // chunk // n_workers
    pad_w = n_pad // chunk // n_workers
    assert per_w * chunk * n_workers == n_tok and pad_w * chunk * n_workers == n_pad
    mesh = plsc.VectorSubcoreMesh(core_axis_name="c", subcore_axis_name="s",
                                  num_cores=SC_CORES, num_subcores=SC_SUBCORES)

    def body(table_hbm, idx_hbm, pad_hbm, zero_hbm, out_hbm, idx_v, pad_v, rows_v, zero_v,
             rsem, ssem, zsem):
        wid = lax.axis_index("s") * SC_CORES + lax.axis_index("c")
        pltpu.sync_copy(idx_hbm.at[wid], idx_v)
        pltpu.sync_copy(pad_hbm.at[wid], pad_v)
        pltpu.sync_copy(zero_hbm, zero_v)

        def zero_fill(pc):
            return pltpu.make_async_copy(zero_v, out_hbm.at[pad_v.at[pc]], zsem)

        for pc in range(pad_w):
            zero_fill(pc).start()

        def scatter(ci, kk):
            return pltpu.make_async_copy(rows_v, out_hbm.at[idx_v.at[ci * n_picks + kk]], ssem)

        @pl.loop(0, per_w)
        def _(ci):
            pltpu.async_copy(table_hbm.at[ci, wid], rows_v, rsem).wait()
            for kk in range(n_picks):
                scatter(ci, kk).start()
            for kk in range(n_picks):
                scatter(ci, kk).wait()

        for pc in range(pad_w):
            zero_fill(pc).wait()

    idx = dest.reshape(n_picks, per_w, n_workers, chunk).transpose(2, 1, 0, 3)
    idx = idx.reshape(n_workers, per_w * n_picks, chunk)
    return pl.kernel(
        body,
        out_type=jax.ShapeDtypeStruct((n_tok * n_picks + n_pad, width), table.dtype),
        mesh=mesh,
        scratch_types=[pltpu.VMEM((per_w * n_picks, chunk), jnp.int32),
                       pltpu.VMEM((pad_w, chunk), jnp.int32),
                       pltpu.VMEM((chunk, width), table.dtype),
                       pltpu.VMEM((chunk, width), table.dtype),
                       pltpu.SemaphoreType.DMA, pltpu.SemaphoreType.DMA, pltpu.SemaphoreType.DMA],
        name="sc_dispatch",
    )(table.reshape(per_w, n_workers, chunk, width), idx,
      pad_rows.reshape(n_workers, pad_w, chunk), jnp.zeros((chunk, width), table.dtype))


def _combine_kernel(x_ref, *refs):
    y_refs, (gate_ref, gf_ref, o_ref) = refs[:TOP_K], refs[TOP_K:]
    gates = gate_ref[...]
    acc = gates[:, 0:1] * _unpack_rows(y_refs[0][...], F32)
    for kk in range(1, TOP_K):
        acc = acc + gates[:, kk:kk + 1] * _unpack_rows(y_refs[kk][...], F32)
    o_ref[...] = x_ref[...] + gf_ref[0] * acc


def moe_combine(lay, x, yg, gates, mod, drop_ctx):
    D = D_MODEL
    if drop_ctx:
        nbo = lay.nb - 1
        src = lambda r: (r // nbo) * lay.nb + 1 + r % nbo
        n_out = lay.n_batch * nbo
    else:
        src = lambda r: r
        n_out = lay.n_blocks
    y_specs = [pl.BlockSpec((TL, PACK_W), functools.partial(lambda kk, r: (kk * lay.n_blocks + src(r), 0), kk))
               for kk in range(TOP_K)]
    return pl.pallas_call(
        _combine_kernel,
        grid=(n_out,),
        in_specs=[pl.BlockSpec((TL, D), lambda r: (src(r), 0))] + y_specs + [
            pl.BlockSpec((TL, TOP_K), lambda r: (src(r), 0)),
            pl.BlockSpec((1, 1, D), lambda r: (lay.mod_row(src(r)), 0, 5))],
        out_specs=pl.BlockSpec((TL, D), lambda r: (r, 0)),
        out_shape=jax.ShapeDtypeStruct((n_out * TL, D), F32),
        compiler_params=_params(1),
        name="moe_combine",
    )(x, yg, yg, yg, yg, gates, mod)


def moe_route(top_e, rank, counts):
    T = top_e.shape[1]
    assert (T * TOP_K) % MOE_BLOCK == 0
    counts = counts.reshape(N_EXPERTS).astype(jnp.int32)
    padded = (counts + MOE_BLOCK - 1) // MOE_BLOCK * MOE_BLOCK
    padded_end = jnp.cumsum(padded)
    padded_start = padded_end - padded
    experts = jnp.arange(N_EXPERTS)
    start_of = jnp.sum(jnp.where(top_e[..., None] == experts, padded_start, 0), axis=-1)
    dest = (start_of + rank).astype(jnp.int32)
    n_pad = N_EXPERTS * MOE_BLOCK
    n_rows = T * TOP_K + n_pad
    n_blk = n_rows // MOE_BLOCK
    blk_start = jnp.arange(n_blk) * MOE_BLOCK
    blk_e = jnp.minimum(jnp.sum(padded_end[None, :] <= blk_start[:, None], axis=1), N_EXPERTS - 1)
    blk_e = blk_e.astype(jnp.int32)
    blk_first = jnp.concatenate([jnp.ones((1,), jnp.int32), (blk_e[1:] != blk_e[:-1]).astype(jnp.int32)])
    blk_hot = blk_e[:, None] == experts
    in_grp = blk_start - jnp.sum(jnp.where(blk_hot, padded_start, 0), axis=-1)
    blk_valid = jnp.clip(jnp.sum(jnp.where(blk_hot, counts, 0), axis=-1) - in_grp, 0, MOE_BLOCK)
    blk_valid = jnp.where(blk_start < padded_end[-1], blk_valid, 0).astype(jnp.int32)
    tail = padded - counts
    tail_end = jnp.cumsum(tail)
    j = jnp.arange(n_pad)
    owner = jnp.sum(tail_end[None, :] <= j[:, None], axis=1)
    base = padded_start + counts - (tail_end - tail)
    in_group = jnp.sum(jnp.where(owner[:, None] == experts, base, 0), axis=-1) + j
    pad_rows = jnp.where(j < tail_end[-1], in_group, padded_end[-1] + j - tail_end[-1])
    return dest, pad_rows.astype(jnp.int32), blk_e, blk_first, blk_valid


def kernel(x, c, ctx, c_ctx, norm_mix, norm_ffn, w_mod, b_mod, conv_w_in, conv_w, conv_w_out, ml_w_in, ml_b_gate, ml_norm, ml_w_out, mla_w_in, mla_q_norm, mla_kv_norm, mla_w_uq, mla_w_ukv, mla_qn_nope, mla_qn_rope, mla_kn_nope, mla_kn_rope, mla_w_out, moe_w_router, moe_b_router, moe_w1, moe_b1, moe_w2, moe_b2):
    Bn, n_lat, D = x.shape
    n_ctx = ctx.shape[1]
    assert D == D_MODEL and n_ctx == TL and n_lat % TL == 0
    assert (DEPTH - 1) % N_MIXERS == 0
    full = Layout(Bn, (n_ctx + n_lat) // TL, True)
    lat_only = Layout(Bn, n_lat // TL, False)
    mods = ada_all(c, c_ctx, w_mod, b_mod)
    tables = rope_tables(n_ctx, n_lat)
    b1_all = moe_b1.reshape(DEPTH, N_EXPERTS, 1, 2 * MOE_FF)
    b2_all = moe_b2.reshape(DEPTH, N_EXPERTS, 1, D)
    X = jnp.concatenate([ctx, x], axis=1).reshape(full.rows, D)
    for layer in range(DEPTH):
        kind, j = layer % N_MIXERS, layer // N_MIXERS
        lay = lat_only if layer == DEPTH - 1 else full
        mod = mods[layer]
        gain_a = norm_mix[layer].reshape(1, D)
        gain_f = norm_ffn[layer].reshape(1, D)
        if kind == 0:
            bg, v = conv_in(lay, X, gain_a, mod, conv_w_in[j].astype(BF16))
            pro, w_out, name = (v, bg, conv_w[j]), conv_w_out[j], "conv"
        elif kind == 1:
            w = ml_w_in[j]
            n_main = 2 * ML_QK + 2 * ML_V
            w_main = jnp.concatenate([w[:, :ML_QK] * ML_DQK ** -0.5, w[:, ML_QK:n_main]], axis=1)
            q, k, kt, v, og, g, gt = mlstm_in(lay, X, gain_a, mod, w_main.astype(BF16), w[:, n_main:],
                                              ml_b_gate[j])
            h_f, h_b = mlstm_scan(lay, q, k, kt, v, g, gt)
            pro, w_out, name = (h_f, h_b, og, ml_norm[j]), ml_w_out[j], "mlstm"
        else:
            q, k, v = mla_in(lay, X, gain_a, mod, mla_w_in[j], mla_q_norm[j], mla_kv_norm[j],
                             mla_w_uq[j], mla_w_ukv[j], mla_qn_nope[j], mla_qn_rope[j],
                             mla_kn_nope[j], mla_kn_rope[j], tables)
            pro, w_out, name = (mla_attention(lay, q, k, v),), mla_w_out[j], "mla"
        X, h2, top_e, gates, rank, counts = mixer_out(
            lay, name, pro, w_out, X, mod, gain_f, moe_w_router[layer], moe_b_router[layer])
        dest, pad_rows, blk_e, blk_first, blk_valid = moe_route(top_e, rank, counts)
        xp = sc_dispatch(h2, dest, pad_rows)
        yp = expert_ffn(layer, xp, blk_e, blk_first, blk_valid, moe_w1, b1_all, moe_w2, b2_all)
        yg = sc_gather(yp, dest.reshape(-1))
        X = moe_combine(lay, X, yg, gates.T, mod, drop_ctx=(layer == DEPTH - 2))
    return X.reshape(Bn, n_lat, D)
```

```python
import functools

import jax
import jax.numpy as jnp
from jax import lax
from jax.experimental import pallas as pl
from jax.experimental.pallas import tpu as pltpu
from jax.experimental.pallas import tpu_sc as plsc

D_MODEL = 1024
DEPTH = 4
GRID_W = 64
N_MIXERS = 3
N_ADA = 6
RMS_EPS = 1e-6
CONV_WIDTH = 3
ML_HEADS = 8
ML_DQK = 64
ML_DV = 128
ML_QK = ML_HEADS * ML_DQK
ML_V = ML_HEADS * ML_DV
GATE_CAP = 15.0
MLA_HEADS = 8
MLA_NOPE = 128
MLA_ROPE = 64
MLA_V = 128
MLA_Q_LORA = 384
MLA_KV_LORA = 256
MLA_SCALE = (MLA_NOPE + MLA_ROPE) ** -0.5
ROPE_THETA = 10000.0
N_EXPERTS = 32
TOP_K = 4
MOE_FF = D_MODEL
SWIGLU_ALPHA = 1.702
SWIGLU_LIMIT = 7.0
MOE_BLOCK = 512

TL = 256
LANES = 128
BF16_ROWS = 16
VMEM_LIMIT = 48 * 1024 * 1024
HI = lax.Precision.HIGHEST
F32 = jnp.float32
BF16 = jnp.bfloat16


def _params(n_axes):
    return pltpu.CompilerParams(dimension_semantics=("arbitrary",) * n_axes,
                                vmem_limit_bytes=VMEM_LIMIT)


def _rms(x, width=None):
    width = x.shape[-1] if width is None else width
    return x * lax.rsqrt(jnp.sum(x * x, axis=-1, keepdims=True) * (1.0 / width) + RMS_EPS)


def _norm_mod(x, gain, scale, shift):
    return _rms(x) * (gain * (1.0 + scale)) + shift


def _split_weight_lanes(w):
    hi = w.astype(BF16)
    lo = (w - hi.astype(F32)).astype(BF16)
    return jnp.pad(jnp.concatenate([hi, lo], axis=1), ((0, 0), (0, LANES - 2 * w.shape[1])))


PACK_W = D_MODEL // 2
HIGH_HALF = -65536


def _pack_rows(x):
    xb = x.astype(BF16).astype(F32)
    lo = lax.bitcast_convert_type(xb[:, :PACK_W], jnp.int32)
    hi = lax.bitcast_convert_type(xb[:, PACK_W:], jnp.int32)
    return hi | lax.shift_right_logical(lo, 16)


def _unpack_rows(w, dtype):
    lo = lax.bitcast_convert_type(lax.shift_left(w, 16), F32)
    hi = lax.bitcast_convert_type(w & HIGH_HALF, F32)
    return jnp.concatenate([lo.astype(dtype), hi.astype(dtype)], axis=1)


class Layout:
    def __init__(self, n_batch, nb, ctx_first):
        self.n_batch, self.nb, self.ctx_first = n_batch, nb, ctx_first
        self.n_blocks = n_batch * nb
        self.rows = self.n_blocks * TL

    def mod_row(self, r):
        b = r // self.nb
        return jnp.where(r % self.nb == 0, self.n_batch, b) if self.ctx_first else b

    def row_spec(self, width):
        return pl.BlockSpec((TL, width), lambda r: (r, 0))

    def mod_spec(self, piece):
        return pl.BlockSpec((1, 1, D_MODEL), lambda r: (self.mod_row(r), 0, piece))

    def token_specs(self, width, n_src):
        if n_src == 1:
            return [self.row_spec(width)]
        nb = self.nb
        return [pl.BlockSpec((TL, width), lambda r: (r // nb, 0)),
                pl.BlockSpec((TL, width), lambda r: ((r // nb) * (nb - 1) + jnp.maximum(r % nb - 1, 0), 0))]


def _token_block(srcs, r, nb):
    if len(srcs) == 1:
        return srcs[0][...]
    return jnp.where(r % nb == 0, srcs[0][...], srcs[1][...])


def _const_spec(shape):
    return pl.BlockSpec(shape, lambda *_: (0,) * len(shape))


ADA_ROWS = 16
ADA_TN = 1536


def _ada_kernel(c_ref, w_ref, b_ref, o_ref):
    c = c_ref[...]
    s = c * jax.nn.sigmoid(c)
    o_ref[0] = jnp.dot(s, w_ref[0], precision=HI, preferred_element_type=F32) + b_ref[0]


def ada_all(c, c_ctx, w_mod, b_mod):
    Bn, D = c.shape
    assert Bn + 1 <= ADA_ROWS
    cond = jnp.zeros((ADA_ROWS, D), F32).at[:Bn].set(c).at[Bn].set(c_ctx)
    out = pl.pallas_call(
        _ada_kernel,
        grid=(DEPTH, N_ADA * D // ADA_TN),
        in_specs=[pl.BlockSpec((ADA_ROWS, D), lambda l, n: (0, 0)),
                  pl.BlockSpec((1, D, ADA_TN), lambda l, n: (l, 0, n)),
                  pl.BlockSpec((1, 1, ADA_TN), lambda l, n: (l, 0, n))],
        out_specs=pl.BlockSpec((1, ADA_ROWS, ADA_TN), lambda l, n: (l, 0, n)),
        out_shape=jax.ShapeDtypeStruct((DEPTH, ADA_ROWS, N_ADA * D), F32),
        compiler_params=_params(2),
        name="ada_mod",
    )(cond, w_mod, b_mod.reshape(DEPTH, 1, N_ADA * D))
    return out[:, :Bn + 1, None, :]


def _conv_in_kernel(*refs, nb, n_src):
    srcs, (gain_ref, sc_ref, sh_ref, w_ref, bg_ref, v_ref) = refs[:n_src], refs[n_src:]
    D = D_MODEL
    x = _token_block(srcs, pl.program_id(0), nb)
    h = _norm_mod(x, gain_ref[...], sc_ref[0], sh_ref[0]).astype(BF16)
    p = jnp.dot(h, w_ref[...], preferred_element_type=F32)
    bg_ref[...] = p[:, :D].astype(BF16)
    v_ref[...] = (p[:, D:2 * D] * p[:, 2 * D:]).astype(BF16)


def conv_in(lay, xs, gain, mod, w_in):
    D = D_MODEL
    sds = jax.ShapeDtypeStruct((lay.rows, D), BF16)
    return pl.pallas_call(
        functools.partial(_conv_in_kernel, nb=lay.nb, n_src=len(xs)),
        grid=(lay.n_blocks,),
        in_specs=lay.token_specs(D, len(xs)) + [_const_spec((1, D)), lay.mod_spec(1), lay.mod_spec(0),
                                                _const_spec((D, 3 * D))],
        out_specs=[lay.row_spec(D), lay.row_spec(D)],
        out_shape=[sds, sds],
        compiler_params=_params(1),
        name="conv_in",
    )(*xs, gain, mod, mod, w_in)


ML_T = 256
ML_SW = 2 * ML_DV
ML_NG = 4 * ML_HEADS


LOG2E = 1.4426950408889634


def _gate_act(g, is_forget):
    g = GATE_CAP * jnp.tanh(g * (1.0 / GATE_CAP))
    log_sig = jnp.minimum(g, 0.0) - jnp.log(1.0 + jnp.exp(-jnp.abs(g)))
    return jnp.where(is_forget, log_sig, g) * LOG2E


def _mlstm_in_kernel(x_ref, gain_ref, sc_ref, sh_ref, w_ref, wg_ref, bg_ref,
                     q_ref, k_ref, kt_ref, v_ref, og_ref, g_ref, gt_ref):
    h = _norm_mod(x_ref[...], gain_ref[...], sc_ref[0], sh_ref[0])
    hb = h.astype(BF16)
    p = jnp.dot(hb, w_ref[...], preferred_element_type=F32)
    q_ref[...] = p[:, :ML_QK].astype(BF16)
    k_ref[...] = p[:, ML_QK:2 * ML_QK].astype(BF16)
    kt_ref[...] = p[:, ML_QK:2 * ML_QK].T.astype(BF16)
    v_ref[...] = p[:, 2 * ML_QK:2 * ML_QK + ML_V].astype(BF16)
    og_ref[...] = jax.nn.sigmoid(p[:, 2 * ML_QK + ML_V:]).astype(BF16)
    h_lo = (h - hb.astype(F32)).astype(BF16)
    g = jnp.dot(jnp.concatenate([hb, h_lo, hb], axis=1), wg_ref[...],
                preferred_element_type=F32) + bg_ref[...]
    col = lax.broadcasted_iota(jnp.int32, g.shape, 1)
    g = _gate_act(g, (col // ML_HEADS) % 2 == 1)
    g_ref[...] = g[:, :ML_NG]
    gt_ref[...] = g.T[:ML_NG]


def mlstm_in(lay, x, gain, mod, w_main, w_g, b_g):
    D = D_MODEL
    n_main = 2 * ML_QK + 2 * ML_V
    g_hi = w_g.astype(BF16)
    g_lo = (w_g - g_hi.astype(F32)).astype(BF16)
    wg3 = jnp.pad(jnp.concatenate([g_hi, g_hi, g_lo], axis=0), ((0, 0), (0, LANES - ML_NG)))
    bf = lambda w: jax.ShapeDtypeStruct((lay.rows, w), BF16)
    return pl.pallas_call(
        _mlstm_in_kernel,
        grid=(lay.n_blocks,),
        in_specs=[lay.row_spec(D), _const_spec((1, D)), lay.mod_spec(1), lay.mod_spec(0),
                  _const_spec((D, n_main)), _const_spec((3 * D, LANES)), _const_spec((1, LANES))],
        out_specs=[lay.row_spec(ML_QK), lay.row_spec(ML_QK), pl.BlockSpec((ML_QK, TL), lambda r: (0, r)),
                   lay.row_spec(ML_V), lay.row_spec(ML_V),
                   lay.row_spec(ML_NG), pl.BlockSpec((ML_NG, TL), lambda r: (0, r))],
        out_shape=[bf(ML_QK), bf(ML_QK), jax.ShapeDtypeStruct((ML_QK, lay.rows), BF16),
                   bf(ML_V), bf(ML_V),
                   jax.ShapeDtypeStruct((lay.rows, ML_NG), F32),
                   jax.ShapeDtypeStruct((ML_NG, lay.rows), F32)],
        compiler_params=_params(1),
        name="mlstm_in",
    )(x, gain, mod, mod, w_main, wg3, _pad_lanes(b_g))


def _split3(x):
    hi = x.astype(BF16)
    r1 = x - hi.astype(F32)
    mid = r1.astype(BF16)
    lo = (r1 - mid.astype(F32)).astype(BF16)
    return hi, mid, lo


def _dot_exact01(x, sel01, x_on_left):
    sel = sel01.astype(BF16)
    parts = [jnp.dot(p, sel, preferred_element_type=F32) if x_on_left
             else jnp.dot(sel, p, preferred_element_type=F32) for p in _split3(x)]
    return parts[0] + parts[1] + parts[2]


def _mlstm_dir(reverse, q_ref, k_ref, kt_ref, v_ref, g_ref, gt_ref, o_ref, s_ref, m_ref):
    T = ML_T
    row = lax.broadcasted_iota(jnp.int32, (T, T), 0)
    col = lax.broadcasted_iota(jnp.int32, (T, T), 1)
    mask = (col >= row) if reverse else (col <= row)
    gt = gt_ref[...]
    bc = _dot_exact01(g_ref[...], mask, x_on_left=False)
    br = _dot_exact01(gt, (row >= col) if reverse else (row <= col), x_on_left=True)
    gi, gf = (2 * ML_HEADS, 3 * ML_HEADS) if reverse else (0, ML_HEADS)
    lane = lax.broadcasted_iota(jnp.int32, (T, 2 * ML_DQK), 1)
    sub = lax.broadcasted_iota(jnp.int32, (2 * ML_DQK, T), 0)
    pick = (lax.broadcasted_iota(jnp.int32, (ML_NG, ML_V), 0)
            == gf + lax.broadcasted_iota(jnp.int32, (ML_NG, ML_V), 1) // ML_DV)
    b_all = _dot_exact01(bc, pick, x_on_left=True)
    ones = jnp.ones((T, ML_DV), BF16)
    heads = range(ML_HEADS)
    qm, vx, s_raw = {}, {}, {}
    for h in heads:
        pair = (h // 2) * 2 * ML_DQK
        own = (lane >= ML_DQK) if (h % 2) else (lane < ML_DQK)
        qp = q_ref[:, pair:pair + 2 * ML_DQK]
        qm[h] = jnp.where(own, qp, jnp.zeros_like(qp))
        vx[h] = jnp.concatenate([v_ref[:, h * ML_DV:(h + 1) * ML_DV], ones], axis=1)
        s_raw[h] = lax.dot_general(qm[h], k_ref[:, pair:pair + 2 * ML_DQK], (((1,), (1,)), ((), ())),
                                   preferred_element_type=F32)
    p, a, u_rep, m_prev, s_prev = {}, {}, {}, {}, {}
    for h in heads:
        c_row = gt[gi + h:gi + h + 1, :] - br[gf + h:gf + h + 1, :]
        e = jnp.where(mask, c_row, -jnp.inf)
        m_prev[h] = m_ref[h][0:1, 0:1]
        u = jnp.maximum(m_prev[h], jnp.max(e, axis=1, keepdims=True))
        p[h] = (s_raw[h] * jnp.exp2(e - u)).astype(BF16)
        u_rep[h] = jnp.broadcast_to(u, (T, ML_DV))
        a[h] = jnp.exp2(m_prev[h] - u_rep[h])
        s_prev[h] = s_ref[h]
    for h in heads:
        r = jnp.dot(p[h], vx[h], preferred_element_type=F32)
        qs = jnp.dot(qm[h], s_prev[h].astype(BF16), preferred_element_type=F32)
        num = r[:, :ML_DV] + a[h] * qs[:, :ML_DV]
        den = r[:, ML_DV:] + a[h] * qs[:, ML_DV:]
        floor = jnp.exp2(-(b_all[:, h * ML_DV:(h + 1) * ML_DV] + u_rep[h]))
        o_ref[:, h * ML_DV:(h + 1) * ML_DV] = (num / jnp.maximum(jnp.abs(den), floor)).astype(o_ref.dtype)
    for h in heads:
        pair = (h // 2) * 2 * ML_DQK
        b_row = br[gf + h:gf + h + 1, :]
        tot = b_row[:, 0:1] if reverse else b_row[:, T - 1:T]
        g_row = tot - b_row + gt[gi + h:gi + h + 1, :]
        m_new = jnp.maximum(tot + m_prev[h], jnp.max(g_row, axis=1, keepdims=True))
        decay = jnp.exp2(tot + m_prev[h] - m_new)
        wk = jnp.exp2(g_row - m_new)
        own_t = (sub >= ML_DQK) if (h % 2) else (sub < ML_DQK)
        kt = kt_ref[pair:pair + 2 * ML_DQK, :].astype(F32)
        kw = jnp.where(own_t, kt * wk, 0.0).astype(BF16)
        s_ref[h] = decay * s_prev[h] + jnp.dot(kw, vx[h], preferred_element_type=F32)
        m_ref[h] = jnp.broadcast_to(m_new, m_ref.shape[1:])


def _mlstm_scan_kernel(qf_ref, kf_ref, ktf_ref, vf_ref, gf_ref, gtf_ref,
                       qb_ref, kb_ref, ktb_ref, vb_ref, gb_ref, gtb_ref,
                       of_ref, ob_ref, sf_ref, mf_ref, sb_ref, mb_ref):
    @pl.when(pl.program_id(1) == 0)
    def _():
        sf_ref[...] = jnp.zeros_like(sf_ref)
        mf_ref[...] = jnp.zeros_like(mf_ref)
        sb_ref[...] = jnp.zeros_like(sb_ref)
        mb_ref[...] = jnp.zeros_like(mb_ref)

    _mlstm_dir(False, qf_ref, kf_ref, ktf_ref, vf_ref, gf_ref, gtf_ref, of_ref, sf_ref, mf_ref)
    _mlstm_dir(True, qb_ref, kb_ref, ktb_ref, vb_ref, gb_ref, gtb_ref, ob_ref, sb_ref, mb_ref)


def mlstm_scan(lay, q, k, kt, v, g, gt):
    assert lay.ctx_first
    per = TL // ML_T
    nb = lay.nb * per
    rev = lambda j: jnp.where(j < per, per - 1 - j, nb + per - 1 - j)
    fwd = lambda b, j: (b * nb + j, 0)
    bwd = lambda b, j: (b * nb + rev(j), 0)
    fwd_t = lambda b, j: (0, b * nb + j)
    bwd_t = lambda b, j: (0, b * nb + rev(j))

    def specs(im, imt):
        return [pl.BlockSpec((ML_T, ML_QK), im), pl.BlockSpec((ML_T, ML_QK), im),
                pl.BlockSpec((ML_QK, ML_T), imt), pl.BlockSpec((ML_T, ML_V), im),
                pl.BlockSpec((ML_T, ML_NG), im), pl.BlockSpec((ML_NG, ML_T), imt)]

    out_sds = jax.ShapeDtypeStruct((lay.rows, ML_V), BF16)
    state = [pltpu.VMEM((ML_HEADS, 2 * ML_DQK, ML_SW), F32),
             pltpu.VMEM((ML_HEADS, 8, LANES), F32)]
    return pl.pallas_call(
        _mlstm_scan_kernel,
        grid=(lay.n_batch, nb),
        in_specs=specs(fwd, fwd_t) + specs(bwd, bwd_t),
        out_specs=[pl.BlockSpec((ML_T, ML_V), fwd), pl.BlockSpec((ML_T, ML_V), bwd)],
        out_shape=[out_sds, out_sds],
        scratch_shapes=state + state,
        compiler_params=_params(2),
        name="mlstm_scan",
    )(q, k, kt, v, g, gt, q, k, kt, v, g, gt)


MLA_QK_PAD = 256
MLA_VW = 2 * MLA_V
MLA_IN_PAD = MLA_Q_LORA + MLA_KV_LORA + LANES
ROPE_HALF = MLA_ROPE // 4


def _mla_in_kernel(x_ref, gain_ref, sc_ref, sh_ref, win_ref, wuq_ref, wukv_ref, qn_ref, kvn_ref,
                   qnn_ref, qnr_ref, knn_ref, knr_ref, cos_ref, sa_ref, sb_ref,
                   q_out, k_out, v_out):
    h = _norm_mod(x_ref[...], gain_ref[...], sc_ref[0], sh_ref[0]).astype(BF16)
    p = jnp.dot(h, win_ref[...], preferred_element_type=F32)
    cq = _rms(p[:, :MLA_Q_LORA]) * qn_ref[...]
    ckv = _rms(p[:, MLA_Q_LORA:MLA_Q_LORA + MLA_KV_LORA]) * kvn_ref[...]
    kr = p[:, MLA_Q_LORA + MLA_KV_LORA:]
    q = jnp.dot(cq.astype(BF16), wuq_ref[...], preferred_element_type=F32)
    kv = jnp.dot(ckv.astype(BF16), wukv_ref[...], preferred_element_type=F32)
    nv = MLA_HEADS * MLA_NOPE
    ones_col = (lax.broadcasted_iota(jnp.int32, (TL, MLA_VW - MLA_V), 1) == 0).astype(BF16)
    cos, sa, sb = cos_ref[...], sa_ref[...], sb_ref[...]

    def rope(xp):
        return (xp * cos + pltpu.roll(xp, LANES - ROPE_HALF, 1) * sa + pltpu.roll(xp, ROPE_HALF, 1) * sb)

    heads = range(MLA_HEADS)
    kr = _rms(kr, MLA_ROPE) * knr_ref[...]
    qn = {hd: _rms(q[:, hd * MLA_QK_PAD:hd * MLA_QK_PAD + MLA_NOPE]) * qnn_ref[...] for hd in heads}
    qr = {hd: _rms(q[:, hd * MLA_QK_PAD + MLA_NOPE:(hd + 1) * MLA_QK_PAD], MLA_ROPE) * qnr_ref[...]
          for hd in heads}
    kn = {hd: _rms(kv[:, hd * MLA_NOPE:(hd + 1) * MLA_NOPE]) * knn_ref[...] for hd in heads}
    kr = rope(kr).astype(BF16)
    qr = {hd: rope(qr[hd]) for hd in heads}
    for hd in heads:
        c0 = hd * MLA_QK_PAD
        q_out[:, c0:c0 + MLA_NOPE] = qn[hd].astype(BF16)
        q_out[:, c0 + MLA_NOPE:c0 + MLA_QK_PAD] = qr[hd].astype(BF16)
        v0 = nv + hd * MLA_V
        v_out[:, hd * MLA_VW:hd * MLA_VW + MLA_V] = kv[:, v0:v0 + MLA_V].astype(BF16)
        v_out[:, hd * MLA_VW + MLA_V:(hd + 1) * MLA_VW] = ones_col
        k_out[:, c0:c0 + MLA_NOPE] = kn[hd].astype(BF16)
        k_out[:, c0 + MLA_NOPE:c0 + MLA_QK_PAD] = kr


def _pad_lanes(g):
    return jnp.pad(g, (0, LANES - g.shape[0])).reshape(1, LANES)


def mla_in(lay, x, gain, mod, w_in, q_norm, kv_norm, w_uq, w_ukv, qn_nope, qn_rope, kn_nope, kn_rope,
           tables):
    D = D_MODEL
    Hn = MLA_HEADS
    win = jnp.pad(w_in, ((0, 0), (0, MLA_IN_PAD - w_in.shape[1]))).astype(BF16)
    wuq = jnp.pad(w_uq.reshape(MLA_Q_LORA, Hn, MLA_NOPE + MLA_ROPE),
                  ((0, 0), (0, 0), (0, MLA_QK_PAD - MLA_NOPE - MLA_ROPE)))
    wuq = wuq.reshape(MLA_Q_LORA, Hn * MLA_QK_PAD).astype(BF16)
    wkv = w_ukv.reshape(MLA_KV_LORA, Hn, MLA_NOPE + MLA_V)
    wukv = jnp.concatenate([wkv[:, :, :MLA_NOPE].reshape(MLA_KV_LORA, Hn * MLA_NOPE),
                            wkv[:, :, MLA_NOPE:].reshape(MLA_KV_LORA, Hn * MLA_V)], axis=1).astype(BF16)
    nb = lay.nb
    tab_spec = pl.BlockSpec((TL, LANES), lambda r: (r % nb, 0))
    bf = lambda w: jax.ShapeDtypeStruct((lay.rows, w), BF16)
    return pl.pallas_call(
        _mla_in_kernel,
        grid=(lay.n_blocks,),
        in_specs=[lay.row_spec(D), _const_spec((1, D)), lay.mod_spec(1), lay.mod_spec(0),
                  _const_spec(win.shape), _const_spec(wuq.shape), _const_spec(wukv.shape),
                  _const_spec((1, MLA_Q_LORA)), _const_spec((1, MLA_KV_LORA)),
                  _const_spec((1, LANES)), _const_spec((1, LANES)), _const_spec((1, LANES)),
                  _const_spec((1, LANES)), tab_spec, tab_spec, tab_spec],
        out_specs=[lay.row_spec(Hn * MLA_QK_PAD), lay.row_spec(Hn * MLA_QK_PAD), lay.row_spec(Hn * MLA_VW)],
        out_shape=[bf(Hn * MLA_QK_PAD), bf(Hn * MLA_QK_PAD), bf(Hn * MLA_VW)],
        compiler_params=_params(1),
        name="mla_in",
    )(x, gain, mod, mod, win, wuq, wukv, q_norm.reshape(1, -1), kv_norm.reshape(1, -1),
      (qn_nope * (MLA_SCALE * LOG2E)).reshape(1, -1), _pad_lanes(qn_rope * (MLA_SCALE * LOG2E)),
      kn_nope.reshape(1, -1), _pad_lanes(kn_rope), *tables)


def rope_tables(n_ctx, n_lat):
    n_freq = MLA_ROPE // 4
    inv = ROPE_THETA ** (-jnp.arange(n_freq, dtype=F32) / n_freq)
    t = jnp.arange(n_lat)
    a_r = (t // GRID_W).astype(F32)[:, None] * inv
    a_c = (t % GRID_W).astype(F32)[:, None] * inv
    ang = jnp.concatenate([a_r, a_r, a_c, a_c], axis=-1)
    ang = jnp.concatenate([jnp.zeros((n_ctx, MLA_ROPE), F32), ang], axis=0)
    cos, sin = jnp.cos(ang), jnp.sin(ang)
    low = (jnp.arange(MLA_ROPE) % (2 * ROPE_HALF)) < ROPE_HALF
    pad = lambda a: jnp.pad(a, ((0, 0), (0, LANES - MLA_ROPE)))
    return pad(cos), pad(jnp.where(low, -sin, 0.0)), pad(jnp.where(low, 0.0, sin))


MLA_HPS = 2


def _mla_attn_kernel(q_ref, k_ref, v_ref, o_ref, *, n_ctx):
    def attend(n_keys):
        heads = range(MLA_HPS)
        s, p = {}, {}
        for hd in heads:
            q = q_ref[0, :, hd * MLA_QK_PAD:(hd + 1) * MLA_QK_PAD]
            k = k_ref[0, :n_keys, hd * MLA_QK_PAD:(hd + 1) * MLA_QK_PAD]
            s[hd] = lax.dot_general(q, k, (((1,), (1,)), ((), ())), preferred_element_type=F32)
        for hd in heads:
            p[hd] = jnp.exp2(s[hd] - s[hd].max(axis=1, keepdims=True)).astype(BF16)
        for hd in heads:
            v = v_ref[0, :n_keys, hd * MLA_VW:(hd + 1) * MLA_VW]
            r = jnp.dot(p[hd], v, preferred_element_type=F32)
            o_ref[0, :, hd * MLA_V:(hd + 1) * MLA_V] = (
                r[:, :MLA_V] / r[:, MLA_V:MLA_V + 1]).astype(o_ref.dtype)

    @pl.when(pl.program_id(2) == 0)
    def _():
        attend(n_ctx)

    @pl.when(pl.program_id(2) > 0)
    def _():
        attend(k_ref.shape[1])


def mla_attention(lay, q, k, v):
    assert lay.ctx_first
    Bn, S = lay.n_batch, lay.nb * TL
    q3, k3, v3 = (a.reshape(Bn, S, a.shape[-1]) for a in (q, k, v))
    out = pl.pallas_call(
        functools.partial(_mla_attn_kernel, n_ctx=TL),
        grid=(Bn, MLA_HEADS // MLA_HPS, lay.nb),
        in_specs=[pl.BlockSpec((1, TL, MLA_HPS * MLA_QK_PAD), lambda b, h, i: (b, i, h)),
                  pl.BlockSpec((1, S, MLA_HPS * MLA_QK_PAD), lambda b, h, i: (b, 0, h)),
                  pl.BlockSpec((1, S, MLA_HPS * MLA_VW), lambda b, h, i: (b, 0, h))],
        out_specs=pl.BlockSpec((1, TL, MLA_HPS * MLA_V), lambda b, h, i: (b, i, h)),
        out_shape=jax.ShapeDtypeStruct((Bn, S, MLA_HEADS * MLA_V), BF16),
        compiler_params=_params(3),
        name="mla_attention",
    )(q3, k3, v3)
    return out.reshape(lay.rows, MLA_HEADS * MLA_V)


N_PROLOGUE = {"conv": 5, "mlstm": 4, "mla": 1}


def _mixer_out_kernel(*refs, kind, nb, ctx_first, n_src):
    n_pro = N_PROLOGUE[kind]
    pro = refs[:n_pro]
    wout_ref = refs[n_pro]
    x_srcs = refs[n_pro + 1:n_pro + 1 + n_src]
    (ga_ref, gain_ref, sc_ref, sh_ref, wr_ref, br_ref,
     xo_ref, h2_ref, te_ref, gate_ref, rank_ref, cnt_ref, carry_ref) = refs[n_pro + 1 + n_src:]
    r = pl.program_id(0)

    if kind == "conv":
        vprev_ref, v_ref, vnext_ref, bg_ref, cw_ref = pro
        j = r % nb
        first = (j == 0) | (j == 1) if ctx_first else (j == 0)
        last = (j == nb - 1) | (j == 0) if ctx_first else (j == nb - 1)
        v = v_ref[...].astype(F32)
        rows = lax.broadcasted_iota(jnp.int32, (TL, 1), 0)
        prev_row = jnp.where(first, 0.0, vprev_ref[BF16_ROWS - 1:BF16_ROWS, :].astype(F32))
        next_row = jnp.where(last, 0.0, vnext_ref[0:1, :].astype(F32))
        up = jnp.where(rows == 0, prev_row, pltpu.roll(v, 1, 0))
        dn = jnp.where(rows == TL - 1, next_row, pltpu.roll(v, TL - 1, 0))
        cw = cw_ref[...]
        a = bg_ref[...].astype(F32) * (up * cw[0:1] + v * cw[1:2] + dn * cw[2:3])
    elif kind == "mlstm":
        hf_ref, hb_ref, og_ref, ng_ref = pro
        hh = hf_ref[...].astype(F32) + hb_ref[...].astype(F32)
        a = jnp.concatenate([_rms(hh[:, h * ML_DV:(h + 1) * ML_DV]) for h in range(ML_HEADS)], axis=1)
        a = a * ng_ref[...] * og_ref[...].astype(F32)
    else:
        a = pro[0][...]

    y = jnp.dot(a.astype(BF16), wout_ref[...], preferred_element_type=F32)
    xn = _token_block(x_srcs, r, nb) + ga_ref[0] * y
    xo_ref[...] = xn
    h2 = _norm_mod(xn, gain_ref[...], sc_ref[0], sh_ref[0])
    h2_ref[...] = _pack_rows(h2)
    lt = jnp.dot(h2.astype(BF16), wr_ref[...], preferred_element_type=F32).T
    logits = lt[:N_EXPERTS] + lt[N_EXPERTS:2 * N_EXPERTS] + br_ref[...]

    sub = lax.broadcasted_iota(jnp.int32, (N_EXPERTS, TL), 0)
    sub_k = lax.broadcasted_iota(jnp.int32, (TOP_K, TL), 0)
    work = logits
    sel = jnp.zeros((N_EXPERTS, TL), F32)
    top_e = jnp.zeros((TOP_K, TL), jnp.int32)
    top_v = jnp.zeros((TOP_K, TL), F32)
    picks = []
    for kk in range(TOP_K):
        m = work.max(axis=0, keepdims=True)
        idx = jnp.min(jnp.where(work == m, sub, N_EXPERTS), axis=0, keepdims=True)
        hit = sub == idx
        picks.append(hit)
        sel = jnp.where(hit, 1.0, sel)
        work = jnp.where(hit, -jnp.inf, work)
        top_e = jnp.where(sub_k == kk, idx, top_e)
        top_v = jnp.where(sub_k == kk, m, top_v)
    ex = jnp.exp(top_v - top_v[0:1])
    gate_ref[...] = ex / ex.sum(axis=0, keepdims=True)
    te_ref[...] = top_e

    @pl.when(r == 0)
    def _():
        carry_ref[...] = jnp.zeros_like(carry_ref)

    tr = lax.broadcasted_iota(jnp.int32, (TL, TL), 0)
    tc = lax.broadcasted_iota(jnp.int32, (TL, TL), 1)
    before = jnp.dot(sel.astype(BF16), (tr < tc).astype(BF16), preferred_element_type=F32)
    pos = before + carry_ref[...]
    rank = jnp.zeros((TOP_K, TL), F32)
    for kk in range(TOP_K):
        rk = jnp.sum(jnp.where(picks[kk], pos, 0.0), axis=0, keepdims=True)
        rank = jnp.where(sub_k == kk, rk, rank)
    rank_ref[...] = rank.astype(jnp.int32)
    total = carry_ref[...] + jnp.sum(sel, axis=1, keepdims=True)
    carry_ref[...] = total
    cnt_ref[...] = total


def mixer_out(lay, kind, pro_args, w_out, xs, mod, gain_f, w_r, b_r):
    D = D_MODEL
    nb = lay.nb
    if kind == "conv":
        v, bg, cw = pro_args
        per = TL // BF16_ROWS
        last_tile = lay.rows // BF16_ROWS - 1
        pro_specs = [pl.BlockSpec((BF16_ROWS, D), lambda r: (jnp.maximum(r * per - 1, 0), 0)),
                     lay.row_spec(D),
                     pl.BlockSpec((BF16_ROWS, D), lambda r: (jnp.minimum((r + 1) * per, last_tile), 0)),
                     lay.row_spec(D), _const_spec((CONV_WIDTH, D))]
        pro_in = [v, v, v, bg, cw]
    elif kind == "mlstm":
        h_f, h_b, og, ng = pro_args
        pro_specs = [lay.row_spec(ML_V), lay.row_spec(ML_V), lay.row_spec(ML_V), _const_spec((1, ML_V))]
        pro_in = [h_f, h_b, og, ng.reshape(1, ML_V)]
    else:
        pro_specs = [lay.row_spec(D)]
        pro_in = list(pro_args)
    k_in = w_out.shape[0]
    small = lambda dt: jax.ShapeDtypeStruct((TOP_K, lay.rows), dt)
    small_spec = pl.BlockSpec((TOP_K, TL), lambda r: (0, r))
    return pl.pallas_call(
        functools.partial(_mixer_out_kernel, kind=kind, nb=nb, ctx_first=lay.ctx_first, n_src=len(xs)),
        grid=(lay.n_blocks,),
        in_specs=pro_specs + [_const_spec((k_in, D))] + lay.token_specs(D, len(xs)) + [
            lay.mod_spec(2), _const_spec((1, D)), lay.mod_spec(4), lay.mod_spec(3),
            _const_spec((D, LANES)), _const_spec((N_EXPERTS, 1))],
        out_specs=[lay.row_spec(D), lay.row_spec(PACK_W), small_spec, small_spec, small_spec,
                   _const_spec((N_EXPERTS, 1))],
        out_shape=[jax.ShapeDtypeStruct((lay.rows, D), F32),
                   jax.ShapeDtypeStruct((lay.rows, PACK_W), jnp.int32),
                   small(jnp.int32), small(F32), small(jnp.int32),
                   jax.ShapeDtypeStruct((N_EXPERTS, 1), F32)],
        scratch_shapes=[pltpu.VMEM((N_EXPERTS, 1), F32)],
        compiler_params=_params(1),
        name="mixer_out_" + kind,
    )(*pro_in, w_out.astype(BF16), *xs, mod, gain_f, mod, mod, _split_weight_lanes(w_r),
      b_r.reshape(N_EXPERTS, 1))


def _expert_ffn_kernel(blk_e_ref, first_ref, valid_ref, x_ref, w1_ref, b1_ref, w2_ref, b2_ref, o_ref,
                       w1b_ref, w2b_ref):
    del blk_e_ref
    i = pl.program_id(0)

    @pl.when(first_ref[i] == 1)
    def _():
        w1b_ref[...] = w1_ref[0, 0].astype(BF16)
        w2b_ref[...] = w2_ref[0, 0].astype(BF16)

    @pl.when(valid_ref[i] > 0)
    def _():
        x = _unpack_rows(x_ref[...], BF16)
        h = jnp.dot(x, w1b_ref[...], preferred_element_type=F32) + b1_ref[0, 0]
        glu = jnp.minimum(h[:, :MOE_FF], SWIGLU_LIMIT)
        lin = jnp.clip(h[:, MOE_FF:], -SWIGLU_LIMIT, SWIGLU_LIMIT)
        act = glu * jax.nn.sigmoid(SWIGLU_ALPHA * glu) * (lin + 1.0)
        y = jnp.dot(act.astype(BF16), w2b_ref[...], preferred_element_type=F32)
        o_ref[...] = _pack_rows(y + b2_ref[0, 0])

    @pl.when(valid_ref[i] == 0)
    def _():
        o_ref[...] = jnp.zeros_like(o_ref)


def expert_ffn(layer, xp, blk_e, blk_first, blk_valid, w1, b1, w2, b2):
    n_rows = xp.shape[0]
    D, F2 = D_MODEL, 2 * MOE_FF
    n_blk = n_rows // MOE_BLOCK
    grid_spec = pltpu.PrefetchScalarGridSpec(
        num_scalar_prefetch=3,
        grid=(n_blk,),
        in_specs=[
            pl.BlockSpec((MOE_BLOCK, PACK_W), lambda i, be, fi, nu: (i, 0)),
            pl.BlockSpec((1, 1, D, F2), lambda i, be, fi, nu: (layer, be[i], 0, 0)),
            pl.BlockSpec((1, 1, 1, F2), lambda i, be, fi, nu: (layer, be[i], 0, 0)),
            pl.BlockSpec((1, 1, MOE_FF, D), lambda i, be, fi, nu: (layer, be[i], 0, 0)),
            pl.BlockSpec((1, 1, 1, D), lambda i, be, fi, nu: (layer, be[i], 0, 0)),
        ],
        out_specs=pl.BlockSpec((MOE_BLOCK, PACK_W), lambda i, be, fi, nu: (i, 0)),
        scratch_shapes=[pltpu.VMEM((D, F2), BF16), pltpu.VMEM((MOE_FF, D), BF16)],
    )
    return pl.pallas_call(
        _expert_ffn_kernel,
        grid_spec=grid_spec,
        out_shape=jax.ShapeDtypeStruct((n_rows, PACK_W), jnp.int32),
        compiler_params=_params(1),
        name="expert_ffn",
    )(blk_e, blk_first, blk_valid, xp, w1, b1, w2, b2)


SC_CORES = 2
SC_SUBCORES = 16
SC_CHUNKS = (64, 32)


def _sc_chunk(*counts):
    n_workers = SC_CORES * SC_SUBCORES
    for chunk in SC_CHUNKS:
        if all(n % (chunk * n_workers) == 0 for n in counts):
            return chunk
    raise ValueError(f"row counts {counts} do not split over {n_workers} subcores")


def sc_gather(table, idx):
    n_idx = idx.shape[0]
    width = table.shape[1]
    n_workers = SC_CORES * SC_SUBCORES
    per_worker = n_idx // n_workers
    chunk = _sc_chunk(n_idx)
    n_chunks = per_worker // chunk
    assert n_chunks * chunk * n_workers == n_idx and n_chunks % 2 == 0
    mesh = plsc.VectorSubcoreMesh(core_axis_name="c", subcore_axis_name="s",
                                  num_cores=SC_CORES, num_subcores=SC_SUBCORES)

    def body(table_hbm, idx_hbm, out_hbm, idx_v, rows_v, gsem, wsem):
        wid = lax.axis_index("s") * SC_CORES + lax.axis_index("c")
        pltpu.sync_copy(idx_hbm.at[wid], idx_v)

        def gather(ci, slot):
            return pltpu.make_async_copy(table_hbm.at[idx_v.at[ci]], rows_v.at[slot], gsem.at[slot])

        def write(ci, slot):
            return pltpu.make_async_copy(rows_v.at[slot], out_hbm.at[ci, wid], wsem.at[slot])

        gather(0, 0).start()

        @pl.loop(0, n_chunks, step=2)
        def _(c0):
            for slot in range(2):
                ci = c0 + slot
                other = 1 - slot

                @pl.when(ci + 1 < n_chunks)
                def _():
                    @pl.when(ci >= 1)
                    def _():
                        write(ci - 1, other).wait()
                    gather(ci + 1, other).start()

                gather(ci, slot).wait()
                write(ci, slot).start()

        write(n_chunks - 2, 0).wait()
        write(n_chunks - 1, 1).wait()

    out = pl.kernel(
        body,
        out_type=jax.ShapeDtypeStruct((n_chunks, n_workers, chunk, width), table.dtype),
        mesh=mesh,
        scratch_types=[pltpu.VMEM((n_chunks, chunk), jnp.int32),
                       pltpu.VMEM((2, chunk, width), table.dtype),
                       pltpu.SemaphoreType.DMA((2,)),
                       pltpu.SemaphoreType.DMA((2,))],
        name="sc_gather",
    )(table, idx.reshape(n_chunks, n_workers, chunk).transpose(1, 0, 2))
    return out.reshape(n_idx, width)


def sc_dispatch(table, dest, pad_rows):
    n_tok, width = table.shape
    n_picks = dest.shape[0]
    n_pad = pad_rows.shape[0]
    n_workers = SC_CORES * SC_SUBCORES
    chunk = _sc_chunk(n_tok, n_pad)
    per_w = n_tok // chunk // n_workers
    pad_w = n_pad // chunk // n_workers
    assert per_w * chunk * n_workers == n_tok and pad_w * chunk * n_workers == n_pad
    mesh = plsc.VectorSubcoreMesh(core_axis_name="c", subcore_axis_name="s",
                                  num_cores=SC_CORES, num_subcores=SC_SUBCORES)

    def body(table_hbm, idx_hbm, pad_hbm, zero_hbm, out_hbm, idx_v, pad_v, rows_v, zero_v,
             rsem, ssem, zsem):
        wid = lax.axis_index("s") * SC_CORES + lax.axis_index("c")
        pltpu.sync_copy(idx_hbm.at[wid], idx_v)
        pltpu.sync_copy(pad_hbm.at[wid], pad_v)
        pltpu.sync_copy(zero_hbm, zero_v)

        def zero_fill(pc):
            return pltpu.make_async_copy(zero_v, out_hbm.at[pad_v.at[pc]], zsem)

        for pc in range(pad_w):
            zero_fill(pc).start()

        def scatter(ci, kk):
            return pltpu.make_async_copy(rows_v, out_hbm.at[idx_v.at[ci * n_picks + kk]], ssem)

        @pl.loop(0, per_w)
        def _(ci):
            pltpu.async_copy(table_hbm.at[ci, wid], rows_v, rsem).wait()
            for kk in range(n_picks):
                scatter(ci, kk).start()
            for kk in range(n_picks):
                scatter(ci, kk).wait()

        for pc in range(pad_w):
            zero_fill(pc).wait()

    idx = dest.reshape(n_picks, per_w, n_workers, chunk).transpose(2, 1, 0, 3)
    idx = idx.reshape(n_workers, per_w * n_picks, chunk)
    return pl.kernel(
        body,
        out_type=jax.ShapeDtypeStruct((n_tok * n_picks + n_pad, width), table.dtype),
        mesh=mesh,
        scratch_types=[pltpu.VMEM((per_w * n_picks, chunk), jnp.int32),
                       pltpu.VMEM((pad_w, chunk), jnp.int32),
                       pltpu.VMEM((chunk, width), table.dtype),
                       pltpu.VMEM((chunk, width), table.dtype),
                       pltpu.SemaphoreType.DMA, pltpu.SemaphoreType.DMA, pltpu.SemaphoreType.DMA],
        name="sc_dispatch",
    )(table.reshape(per_w, n_workers, chunk, width), idx,
      pad_rows.reshape(n_workers, pad_w, chunk), jnp.zeros((chunk, width), table.dtype))


def _combine_kernel(x_ref, *refs):
    y_refs, (gate_ref, gf_ref, o_ref) = refs[:TOP_K], refs[TOP_K:]
    gates = gate_ref[...]
    acc = gates[:, 0:1] * _unpack_rows(y_refs[0][...], F32)
    for kk in range(1, TOP_K):
        acc = acc + gates[:, kk:kk + 1] * _unpack_rows(y_refs[kk][...], F32)
    o_ref[...] = x_ref[...] + gf_ref[0] * acc


def moe_combine(lay, x, yg, gates, mod, drop_ctx):
    D = D_MODEL
    if drop_ctx:
        nbo = lay.nb - 1
        src = lambda r: (r // nbo) * lay.nb + 1 + r % nbo
        n_out = lay.n_batch * nbo
    else:
        src = lambda r: r
        n_out = lay.n_blocks
    y_specs = [pl.BlockSpec((TL, PACK_W), functools.partial(lambda kk, r: (kk * lay.n_blocks + src(r), 0), kk))
               for kk in range(TOP_K)]
    return pl.pallas_call(
        _combine_kernel,
        grid=(n_out,),
        in_specs=[pl.BlockSpec((TL, D), lambda r: (src(r), 0))] + y_specs + [
            pl.BlockSpec((TL, TOP_K), lambda r: (src(r), 0)),
            pl.BlockSpec((1, 1, D), lambda r: (lay.mod_row(src(r)), 0, 5))],
        out_specs=pl.BlockSpec((TL, D), lambda r: (r, 0)),
        out_shape=jax.ShapeDtypeStruct((n_out * TL, D), F32),
        compiler_params=_params(1),
        name="moe_combine",
    )(x, yg, yg, yg, yg, gates, mod)


def moe_route(top_e, rank, counts):
    T = top_e.shape[1]
    assert (T * TOP_K) % MOE_BLOCK == 0
    counts = counts.reshape(N_EXPERTS).astype(jnp.int32)
    padded = (counts + MOE_BLOCK - 1) // MOE_BLOCK * MOE_BLOCK
    padded_end = jnp.cumsum(padded)
    padded_start = padded_end - padded
    experts = jnp.arange(N_EXPERTS)
    start_of = jnp.sum(jnp.where(top_e[..., None] == experts, padded_start, 0), axis=-1)
    dest = (start_of + rank).astype(jnp.int32)
    n_pad = N_EXPERTS * MOE_BLOCK
    n_rows = T * TOP_K + n_pad
    n_blk = n_rows // MOE_BLOCK
    blk_start = jnp.arange(n_blk) * MOE_BLOCK
    blk_e = jnp.minimum(jnp.sum(padded_end[None, :] <= blk_start[:, None], axis=1), N_EXPERTS - 1)
    blk_e = blk_e.astype(jnp.int32)
    blk_first = jnp.concatenate([jnp.ones((1,), jnp.int32), (blk_e[1:] != blk_e[:-1]).astype(jnp.int32)])
    blk_hot = blk_e[:, None] == experts
    in_grp = blk_start - jnp.sum(jnp.where(blk_hot, padded_start, 0), axis=-1)
    blk_valid = jnp.clip(jnp.sum(jnp.where(blk_hot, counts, 0), axis=-1) - in_grp, 0, MOE_BLOCK)
    blk_valid = jnp.where(blk_start < padded_end[-1], blk_valid, 0).astype(jnp.int32)
    tail = padded - counts
    tail_end = jnp.cumsum(tail)
    j = jnp.arange(n_pad)
    owner = jnp.sum(tail_end[None, :] <= j[:, None], axis=1)
    base = padded_start + counts - (tail_end - tail)
    in_group = jnp.sum(jnp.where(owner[:, None] == experts, base, 0), axis=-1) + j
    pad_rows = jnp.where(j < tail_end[-1], in_group, padded_end[-1] + j - tail_end[-1])
    return dest, pad_rows.astype(jnp.int32), blk_e, blk_first, blk_valid


def kernel(x, c, ctx, c_ctx, norm_mix, norm_ffn, w_mod, b_mod, conv_w_in, conv_w, conv_w_out, ml_w_in, ml_b_gate, ml_norm, ml_w_out, mla_w_in, mla_q_norm, mla_kv_norm, mla_w_uq, mla_w_ukv, mla_qn_nope, mla_qn_rope, mla_kn_nope, mla_kn_rope, mla_w_out, moe_w_router, moe_b_router, moe_w1, moe_b1, moe_w2, moe_b2):
    Bn, n_lat, D = x.shape
    n_ctx = ctx.shape[1]
    assert D == D_MODEL and n_ctx == TL and n_lat % TL == 0
    assert (DEPTH - 1) % N_MIXERS == 0
    full = Layout(Bn, (n_ctx + n_lat) // TL, True)
    lat_only = Layout(Bn, n_lat // TL, False)
    mods = ada_all(c, c_ctx, w_mod, b_mod)
    tables = rope_tables(n_ctx, n_lat)
    b1_all = moe_b1.reshape(DEPTH, N_EXPERTS, 1, 2 * MOE_FF)
    b2_all = moe_b2.reshape(DEPTH, N_EXPERTS, 1, D)
    Xs = (ctx.reshape(Bn * n_ctx, D), x.reshape(Bn * n_lat, D))
    for layer in range(DEPTH):
        kind, j = layer % N_MIXERS, layer // N_MIXERS
        lay = lat_only if layer == DEPTH - 1 else full
        mod = mods[layer]
        gain_a = norm_mix[layer].reshape(1, D)
        gain_f = norm_ffn[layer].reshape(1, D)
        if kind == 0:
            bg, v = conv_in(lay, Xs, gain_a, mod, conv_w_in[j].astype(BF16))
            pro, w_out, name = (v, bg, conv_w[j]), conv_w_out[j], "conv"
        elif kind == 1:
            w = ml_w_in[j]
            n_main = 2 * ML_QK + 2 * ML_V
            w_main = jnp.concatenate([w[:, :ML_QK] * ML_DQK ** -0.5, w[:, ML_QK:n_main]], axis=1)
            q, k, kt, v, og, g, gt = mlstm_in(lay, Xs[0], gain_a, mod, w_main.astype(BF16),
                                              w[:, n_main:], ml_b_gate[j])
            h_f, h_b = mlstm_scan(lay, q, k, kt, v, g, gt)
            pro, w_out, name = (h_f, h_b, og, ml_norm[j]), ml_w_out[j], "mlstm"
        else:
            q, k, v = mla_in(lay, Xs[0], gain_a, mod, mla_w_in[j], mla_q_norm[j], mla_kv_norm[j],
                             mla_w_uq[j], mla_w_ukv[j], mla_qn_nope[j], mla_qn_rope[j],
                             mla_kn_nope[j], mla_kn_rope[j], tables)
            pro, w_out, name = (mla_attention(lay, q, k, v),), mla_w_out[j], "mla"
        X, h2, top_e, gates, rank, counts = mixer_out(
            lay, name, pro, w_out, Xs, mod, gain_f, moe_w_router[layer], moe_b_router[layer])
        dest, pad_rows, blk_e, blk_first, blk_valid = moe_route(top_e, rank, counts)
        xp = sc_dispatch(h2, dest, pad_rows)
        yp = expert_ffn(layer, xp, blk_e, blk_first, blk_valid, moe_w1, b1_all, moe_w2, b2_all)
        yg = sc_gather(yp, dest.reshape(-1))
        Xs = (moe_combine(lay, X, yg, gates.T, mod, drop_ctx=(layer == DEPTH - 2)),)
    return Xs[0].reshape(Bn, n_lat, D)
```

```python
import functools

import jax
import jax.numpy as jnp
from jax import lax
from jax.experimental import pallas as pl
from jax.experimental.pallas import tpu as pltpu
from jax.experimental.pallas import tpu_sc as plsc

D_MODEL = 1024
DEPTH = 4
GRID_W = 64
N_MIXERS = 3
N_ADA = 6
RMS_EPS = 1e-6
CONV_WIDTH = 3
ML_HEADS = 8
ML_DQK = 64
ML_DV = 128
ML_QK = ML_HEADS * ML_DQK
ML_V = ML_HEADS * ML_DV
GATE_CAP = 15.0
MLA_HEADS = 8
MLA_NOPE = 128
MLA_ROPE = 64
MLA_V = 128
MLA_Q_LORA = 384
MLA_KV_LORA = 256
MLA_SCALE = (MLA_NOPE + MLA_ROPE) ** -0.5
ROPE_THETA = 10000.0
N_EXPERTS = 32
TOP_K = 4
MOE_FF = D_MODEL
SWIGLU_ALPHA = 1.702
SWIGLU_LIMIT = 7.0
MOE_BLOCK = 512

TL = 256
LANES = 128
BF16_ROWS = 16
VMEM_LIMIT = 48 * 1024 * 1024
HI = lax.Precision.HIGHEST
F32 = jnp.float32
BF16 = jnp.bfloat16


def _params(n_axes):
    return pltpu.CompilerParams(dimension_semantics=("arbitrary",) * n_axes,
                                vmem_limit_bytes=VMEM_LIMIT)


def _rms(x, width=None):
    width = x.shape[-1] if width is None else width
    return x * lax.rsqrt(jnp.sum(x * x, axis=-1, keepdims=True) * (1.0 / width) + RMS_EPS)


def _norm_mod(x, gain, scale, shift):
    return _rms(x) * (gain * (1.0 + scale)) + shift


def _split_weight_lanes(w):
    hi = w.astype(BF16)
    lo = (w - hi.astype(F32)).astype(BF16)
    return jnp.pad(jnp.concatenate([hi, lo], axis=1), ((0, 0), (0, LANES - 2 * w.shape[1])))


PACK_W = D_MODEL // 2
HIGH_HALF = -65536


def _pack_rows(x):
    xb = x.astype(BF16).astype(F32)
    lo = lax.bitcast_convert_type(xb[:, :PACK_W], jnp.int32)
    hi = lax.bitcast_convert_type(xb[:, PACK_W:], jnp.int32)
    return hi | lax.shift_right_logical(lo, 16)


def _unpack_rows(w, dtype):
    lo = lax.bitcast_convert_type(lax.shift_left(w, 16), F32)
    hi = lax.bitcast_convert_type(w & HIGH_HALF, F32)
    return jnp.concatenate([lo.astype(dtype), hi.astype(dtype)], axis=1)


class Layout:
    def __init__(self, n_batch, nb, ctx_first):
        self.n_batch, self.nb, self.ctx_first = n_batch, nb, ctx_first
        self.n_blocks = n_batch * nb
        self.rows = self.n_blocks * TL

    def mod_row(self, r):
        b = r // self.nb
        return jnp.where(r % self.nb == 0, self.n_batch, b) if self.ctx_first else b

    def row_spec(self, width):
        return pl.BlockSpec((TL, width), lambda r: (r, 0))

    def mod_spec(self, piece):
        return pl.BlockSpec((1, 1, D_MODEL), lambda r: (self.mod_row(r), 0, piece))

    def token_specs(self, width, n_src):
        if n_src == 1:
            return [self.row_spec(width)]
        nb = self.nb
        return [pl.BlockSpec((TL, width), lambda r: (r // nb, 0)),
                pl.BlockSpec((TL, width), lambda r: ((r // nb) * (nb - 1) + jnp.maximum(r % nb - 1, 0), 0))]


def _token_block(srcs, r, nb):
    if len(srcs) == 1:
        return srcs[0][...]
    return jnp.where(r % nb == 0, srcs[0][...], srcs[1][...])


def _const_spec(shape):
    return pl.BlockSpec(shape, lambda *_: (0,) * len(shape))


ADA_ROWS = 16
ADA_TN = 1536


def _ada_kernel(c_ref, w_ref, b_ref, o_ref):
    c = c_ref[...]
    s = c * jax.nn.sigmoid(c)
    o_ref[0] = jnp.dot(s, w_ref[0], precision=HI, preferred_element_type=F32) + b_ref[0]


def ada_all(c, c_ctx, w_mod, b_mod):
    Bn, D = c.shape
    assert Bn + 1 <= ADA_ROWS
    cond = jnp.zeros((ADA_ROWS, D), F32).at[:Bn].set(c).at[Bn].set(c_ctx)
    out = pl.pallas_call(
        _ada_kernel,
        grid=(DEPTH, N_ADA * D // ADA_TN),
        in_specs=[pl.BlockSpec((ADA_ROWS, D), lambda l, n: (0, 0)),
                  pl.BlockSpec((1, D, ADA_TN), lambda l, n: (l, 0, n)),
                  pl.BlockSpec((1, 1, ADA_TN), lambda l, n: (l, 0, n))],
        out_specs=pl.BlockSpec((1, ADA_ROWS, ADA_TN), lambda l, n: (l, 0, n)),
        out_shape=jax.ShapeDtypeStruct((DEPTH, ADA_ROWS, N_ADA * D), F32),
        compiler_params=_params(2),
        name="ada_mod",
    )(cond, w_mod, b_mod.reshape(DEPTH, 1, N_ADA * D))
    return out[:, :Bn + 1, None, :]


def _conv_in_kernel(*refs, nb, n_src):
    srcs, (gain_ref, sc_ref, sh_ref, w_ref, bg_ref, v_ref) = refs[:n_src], refs[n_src:]
    D = D_MODEL
    x = _token_block(srcs, pl.program_id(0), nb)
    h = _norm_mod(x, gain_ref[...], sc_ref[0], sh_ref[0]).astype(BF16)
    p = jnp.dot(h, w_ref[...], preferred_element_type=F32)
    bg_ref[...] = p[:, :D].astype(BF16)
    v_ref[...] = (p[:, D:2 * D] * p[:, 2 * D:]).astype(BF16)


def conv_in(lay, xs, gain, mod, w_in):
    D = D_MODEL
    sds = jax.ShapeDtypeStruct((lay.rows, D), BF16)
    return pl.pallas_call(
        functools.partial(_conv_in_kernel, nb=lay.nb, n_src=len(xs)),
        grid=(lay.n_blocks,),
        in_specs=lay.token_specs(D, len(xs)) + [_const_spec((1, D)), lay.mod_spec(1), lay.mod_spec(0),
                                                _const_spec((D, 3 * D))],
        out_specs=[lay.row_spec(D), lay.row_spec(D)],
        out_shape=[sds, sds],
        compiler_params=_params(1),
        name="conv_in",
    )(*xs, gain, mod, mod, w_in)


ML_T = 256
ML_SW = 2 * ML_DV
ML_NG = 4 * ML_HEADS


LOG2E = 1.4426950408889634


def _gate_act(g, is_forget):
    g = GATE_CAP * jnp.tanh(g * (1.0 / GATE_CAP))
    log_sig = jnp.minimum(g, 0.0) - jnp.log(1.0 + jnp.exp(-jnp.abs(g)))
    return jnp.where(is_forget, log_sig, g) * LOG2E


def _mlstm_in_kernel(x_ref, gain_ref, sc_ref, sh_ref, w_ref, wg_ref, bg_ref,
                     q_ref, k_ref, kt_ref, v_ref, og_ref, g_ref, gt_ref):
    h = _norm_mod(x_ref[...], gain_ref[...], sc_ref[0], sh_ref[0])
    hb = h.astype(BF16)
    p = jnp.dot(hb, w_ref[...], preferred_element_type=F32)
    q_ref[...] = p[:, :ML_QK].astype(BF16)
    k_ref[...] = p[:, ML_QK:2 * ML_QK].astype(BF16)
    kt_ref[...] = p[:, ML_QK:2 * ML_QK].T.astype(BF16)
    v_ref[...] = p[:, 2 * ML_QK:2 * ML_QK + ML_V].astype(BF16)
    og_ref[...] = jax.nn.sigmoid(p[:, 2 * ML_QK + ML_V:]).astype(BF16)
    h_lo = (h - hb.astype(F32)).astype(BF16)
    g = jnp.dot(jnp.concatenate([hb, h_lo, hb], axis=1), wg_ref[...],
                preferred_element_type=F32) + bg_ref[...]
    col = lax.broadcasted_iota(jnp.int32, g.shape, 1)
    g = _gate_act(g, (col // ML_HEADS) % 2 == 1)
    g_ref[...] = g[:, :ML_NG]
    gt_ref[...] = g.T[:ML_NG]


def mlstm_in(lay, x, gain, mod, w_main, w_g, b_g):
    D = D_MODEL
    n_main = 2 * ML_QK + 2 * ML_V
    g_hi = w_g.astype(BF16)
    g_lo = (w_g - g_hi.astype(F32)).astype(BF16)
    wg3 = jnp.pad(jnp.concatenate([g_hi, g_hi, g_lo], axis=0), ((0, 0), (0, LANES - ML_NG)))
    bf = lambda w: jax.ShapeDtypeStruct((lay.rows, w), BF16)
    return pl.pallas_call(
        _mlstm_in_kernel,
        grid=(lay.n_blocks,),
        in_specs=[lay.row_spec(D), _const_spec((1, D)), lay.mod_spec(1), lay.mod_spec(0),
                  _const_spec((D, n_main)), _const_spec((3 * D, LANES)), _const_spec((1, LANES))],
        out_specs=[lay.row_spec(ML_QK), lay.row_spec(ML_QK), pl.BlockSpec((ML_QK, TL), lambda r: (0, r)),
                   lay.row_spec(ML_V), lay.row_spec(ML_V),
                   lay.row_spec(ML_NG), pl.BlockSpec((ML_NG, TL), lambda r: (0, r))],
        out_shape=[bf(ML_QK), bf(ML_QK), jax.ShapeDtypeStruct((ML_QK, lay.rows), BF16),
                   bf(ML_V), bf(ML_V),
                   jax.ShapeDtypeStruct((lay.rows, ML_NG), F32),
                   jax.ShapeDtypeStruct((ML_NG, lay.rows), F32)],
        compiler_params=_params(1),
        name="mlstm_in",
    )(x, gain, mod, mod, w_main, wg3, _pad_lanes(b_g))


def _split3(x):
    hi = x.astype(BF16)
    r1 = x - hi.astype(F32)
    mid = r1.astype(BF16)
    lo = (r1 - mid.astype(F32)).astype(BF16)
    return hi, mid, lo


def _dot_exact01(x, sel01, x_on_left):
    sel = sel01.astype(BF16)
    parts = [jnp.dot(p, sel, preferred_element_type=F32) if x_on_left
             else jnp.dot(sel, p, preferred_element_type=F32) for p in _split3(x)]
    return parts[0] + parts[1] + parts[2]


def _mlstm_dir(reverse, q_ref, k_ref, kt_ref, v_ref, g_ref, gt_ref, o_ref, s_ref, m_ref):
    T = ML_T
    row = lax.broadcasted_iota(jnp.int32, (T, T), 0)
    col = lax.broadcasted_iota(jnp.int32, (T, T), 1)
    mask = (col >= row) if reverse else (col <= row)
    gt = gt_ref[...]
    bc = _dot_exact01(g_ref[...], mask, x_on_left=False)
    br = _dot_exact01(gt, (row >= col) if reverse else (row <= col), x_on_left=True)
    gi, gf = (2 * ML_HEADS, 3 * ML_HEADS) if reverse else (0, ML_HEADS)
    lane = lax.broadcasted_iota(jnp.int32, (T, 2 * ML_DQK), 1)
    sub = lax.broadcasted_iota(jnp.int32, (2 * ML_DQK, T), 0)
    pick = (lax.broadcasted_iota(jnp.int32, (ML_NG, ML_V), 0)
            == gf + lax.broadcasted_iota(jnp.int32, (ML_NG, ML_V), 1) // ML_DV)
    b_all = _dot_exact01(bc, pick, x_on_left=True)
    ones = jnp.ones((T, ML_DV), BF16)
    heads = range(ML_HEADS)
    qm, vx, s_raw = {}, {}, {}
    for h in heads:
        pair = (h // 2) * 2 * ML_DQK
        own = (lane >= ML_DQK) if (h % 2) else (lane < ML_DQK)
        qp = q_ref[:, pair:pair + 2 * ML_DQK]
        qm[h] = jnp.where(own, qp, jnp.zeros_like(qp))
        vx[h] = jnp.concatenate([v_ref[:, h * ML_DV:(h + 1) * ML_DV], ones], axis=1)
        s_raw[h] = lax.dot_general(qm[h], k_ref[:, pair:pair + 2 * ML_DQK], (((1,), (1,)), ((), ())),
                                   preferred_element_type=F32)
    p, a, u_rep, m_prev, s_prev = {}, {}, {}, {}, {}
    for h in heads:
        c_row = gt[gi + h:gi + h + 1, :] - br[gf + h:gf + h + 1, :]
        e = jnp.where(mask, c_row, -jnp.inf)
        m_prev[h] = m_ref[h][0:1, 0:1]
        u = jnp.maximum(m_prev[h], jnp.max(e, axis=1, keepdims=True))
        p[h] = (s_raw[h] * jnp.exp2(e - u)).astype(BF16)
        u_rep[h] = jnp.broadcast_to(u, (T, ML_DV))
        a[h] = jnp.exp2(m_prev[h] - u_rep[h])
        s_prev[h] = s_ref[h]
    for h in heads:
        r = jnp.dot(p[h], vx[h], preferred_element_type=F32)
        qs = jnp.dot(qm[h], s_prev[h].astype(BF16), preferred_element_type=F32)
        num = r[:, :ML_DV] + a[h] * qs[:, :ML_DV]
        den = r[:, ML_DV:] + a[h] * qs[:, ML_DV:]
        floor = jnp.exp2(-(b_all[:, h * ML_DV:(h + 1) * ML_DV] + u_rep[h]))
        o_ref[:, h * ML_DV:(h + 1) * ML_DV] = (num / jnp.maximum(jnp.abs(den), floor)).astype(o_ref.dtype)
    for h in heads:
        pair = (h // 2) * 2 * ML_DQK
        b_row = br[gf + h:gf + h + 1, :]
        tot = b_row[:, 0:1] if reverse else b_row[:, T - 1:T]
        g_row = tot - b_row + gt[gi + h:gi + h + 1, :]
        m_new = jnp.maximum(tot + m_prev[h], jnp.max(g_row, axis=1, keepdims=True))
        decay = jnp.exp2(tot + m_prev[h] - m_new)
        wk = jnp.exp2(g_row - m_new)
        own_t = (sub >= ML_DQK) if (h % 2) else (sub < ML_DQK)
        kt = kt_ref[pair:pair + 2 * ML_DQK, :].astype(F32)
        kw = jnp.where(own_t, kt * wk, 0.0).astype(BF16)
        s_ref[h] = decay * s_prev[h] + jnp.dot(kw, vx[h], preferred_element_type=F32)
        m_ref[h] = jnp.broadcast_to(m_new, m_ref.shape[1:])


def _mlstm_scan_kernel(qf_ref, kf_ref, ktf_ref, vf_ref, gf_ref, gtf_ref,
                       qb_ref, kb_ref, ktb_ref, vb_ref, gb_ref, gtb_ref,
                       of_ref, ob_ref, sf_ref, mf_ref, sb_ref, mb_ref):
    @pl.when(pl.program_id(1) == 0)
    def _():
        sf_ref[...] = jnp.zeros_like(sf_ref)
        mf_ref[...] = jnp.zeros_like(mf_ref)
        sb_ref[...] = jnp.zeros_like(sb_ref)
        mb_ref[...] = jnp.zeros_like(mb_ref)

    _mlstm_dir(False, qf_ref, kf_ref, ktf_ref, vf_ref, gf_ref, gtf_ref, of_ref, sf_ref, mf_ref)
    _mlstm_dir(True, qb_ref, kb_ref, ktb_ref, vb_ref, gb_ref, gtb_ref, ob_ref, sb_ref, mb_ref)


def mlstm_scan(lay, q, k, kt, v, g, gt):
    assert lay.ctx_first
    per = TL // ML_T
    nb = lay.nb * per
    rev = lambda j: jnp.where(j < per, per - 1 - j, nb + per - 1 - j)
    fwd = lambda b, j: (b * nb + j, 0)
    bwd = lambda b, j: (b * nb + rev(j), 0)
    fwd_t = lambda b, j: (0, b * nb + j)
    bwd_t = lambda b, j: (0, b * nb + rev(j))

    def specs(im, imt):
        return [pl.BlockSpec((ML_T, ML_QK), im), pl.BlockSpec((ML_T, ML_QK), im),
                pl.BlockSpec((ML_QK, ML_T), imt), pl.BlockSpec((ML_T, ML_V), im),
                pl.BlockSpec((ML_T, ML_NG), im), pl.BlockSpec((ML_NG, ML_T), imt)]

    out_sds = jax.ShapeDtypeStruct((lay.rows, ML_V), BF16)
    state = [pltpu.VMEM((ML_HEADS, 2 * ML_DQK, ML_SW), F32),
             pltpu.VMEM((ML_HEADS, 8, LANES), F32)]
    return pl.pallas_call(
        _mlstm_scan_kernel,
        grid=(lay.n_batch, nb),
        in_specs=specs(fwd, fwd_t) + specs(bwd, bwd_t),
        out_specs=[pl.BlockSpec((ML_T, ML_V), fwd), pl.BlockSpec((ML_T, ML_V), bwd)],
        out_shape=[out_sds, out_sds],
        scratch_shapes=state + state,
        compiler_params=_params(2),
        name="mlstm_scan",
    )(q, k, kt, v, g, gt, q, k, kt, v, g, gt)


MLA_QK_PAD = 256
MLA_VW = 2 * MLA_V
MLA_IN_PAD = MLA_Q_LORA + MLA_KV_LORA + LANES
ROPE_HALF = MLA_ROPE // 4


def _mla_in_kernel(x_ref, gain_ref, sc_ref, sh_ref, win_ref, wuq_ref, wukv_ref, qn_ref, kvn_ref,
                   qnn_ref, qnr_ref, knn_ref, knr_ref, cos_ref, sa_ref, sb_ref,
                   q_out, k_out, v_out):
    h = _norm_mod(x_ref[...], gain_ref[...], sc_ref[0], sh_ref[0]).astype(BF16)
    p = jnp.dot(h, win_ref[...], preferred_element_type=F32)
    cq = _rms(p[:, :MLA_Q_LORA]) * qn_ref[...]
    ckv = _rms(p[:, MLA_Q_LORA:MLA_Q_LORA + MLA_KV_LORA]) * kvn_ref[...]
    kr = p[:, MLA_Q_LORA + MLA_KV_LORA:]
    q = jnp.dot(cq.astype(BF16), wuq_ref[...], preferred_element_type=F32)
    kv = jnp.dot(ckv.astype(BF16), wukv_ref[...], preferred_element_type=F32)
    nv = MLA_HEADS * MLA_NOPE
    ones_col = (lax.broadcasted_iota(jnp.int32, (TL, MLA_VW - MLA_V), 1) == 0).astype(BF16)
    cos, sa, sb = cos_ref[...], sa_ref[...], sb_ref[...]

    def rope(xp):
        return (xp * cos + pltpu.roll(xp, LANES - ROPE_HALF, 1) * sa + pltpu.roll(xp, ROPE_HALF, 1) * sb)

    heads = range(MLA_HEADS)
    kr = _rms(kr, MLA_ROPE) * knr_ref[...]
    qn = {hd: _rms(q[:, hd * MLA_QK_PAD:hd * MLA_QK_PAD + MLA_NOPE]) * qnn_ref[...] for hd in heads}
    qr = {hd: _rms(q[:, hd * MLA_QK_PAD + MLA_NOPE:(hd + 1) * MLA_QK_PAD], MLA_ROPE) * qnr_ref[...]
          for hd in heads}
    kn = {hd: _rms(kv[:, hd * MLA_NOPE:(hd + 1) * MLA_NOPE]) * knn_ref[...] for hd in heads}
    kr = rope(kr).astype(BF16)
    qr = {hd: rope(qr[hd]) for hd in heads}
    for hd in heads:
        c0 = hd * MLA_QK_PAD
        q_out[:, c0:c0 + MLA_NOPE] = qn[hd].astype(BF16)
        q_out[:, c0 + MLA_NOPE:c0 + MLA_QK_PAD] = qr[hd].astype(BF16)
        v0 = nv + hd * MLA_V
        v_out[:, hd * MLA_VW:hd * MLA_VW + MLA_V] = kv[:, v0:v0 + MLA_V].astype(BF16)
        v_out[:, hd * MLA_VW + MLA_V:(hd + 1) * MLA_VW] = ones_col
        k_out[:, c0:c0 + MLA_NOPE] = kn[hd].astype(BF16)
        k_out[:, c0 + MLA_NOPE:c0 + MLA_QK_PAD] = kr


def _pad_lanes(g):
    return jnp.pad(g, (0, LANES - g.shape[0])).reshape(1, LANES)


def mla_in(lay, x, gain, mod, w_in, q_norm, kv_norm, w_uq, w_ukv, qn_nope, qn_rope, kn_nope, kn_rope,
           tables):
    D = D_MODEL
    Hn = MLA_HEADS
    win = jnp.pad(w_in, ((0, 0), (0, MLA_IN_PAD - w_in.shape[1]))).astype(BF16)
    wuq = jnp.pad(w_uq.reshape(MLA_Q_LORA, Hn, MLA_NOPE + MLA_ROPE),
                  ((0, 0), (0, 0), (0, MLA_QK_PAD - MLA_NOPE - MLA_ROPE)))
    wuq = wuq.reshape(MLA_Q_LORA, Hn * MLA_QK_PAD).astype(BF16)
    wkv = w_ukv.reshape(MLA_KV_LORA, Hn, MLA_NOPE + MLA_V)
    wukv = jnp.concatenate([wkv[:, :, :MLA_NOPE].reshape(MLA_KV_LORA, Hn * MLA_NOPE),
                            wkv[:, :, MLA_NOPE:].reshape(MLA_KV_LORA, Hn * MLA_V)], axis=1).astype(BF16)
    nb = lay.nb
    tab_spec = pl.BlockSpec((TL, LANES), lambda r: (r % nb, 0))
    bf = lambda w: jax.ShapeDtypeStruct((lay.rows, w), BF16)
    return pl.pallas_call(
        _mla_in_kernel,
        grid=(lay.n_blocks,),
        in_specs=[lay.row_spec(D), _const_spec((1, D)), lay.mod_spec(1), lay.mod_spec(0),
                  _const_spec(win.shape), _const_spec(wuq.shape), _const_spec(wukv.shape),
                  _const_spec((1, MLA_Q_LORA)), _const_spec((1, MLA_KV_LORA)),
                  _const_spec((1, LANES)), _const_spec((1, LANES)), _const_spec((1, LANES)),
                  _const_spec((1, LANES)), tab_spec, tab_spec, tab_spec],
        out_specs=[lay.row_spec(Hn * MLA_QK_PAD), lay.row_spec(Hn * MLA_QK_PAD), lay.row_spec(Hn * MLA_VW)],
        out_shape=[bf(Hn * MLA_QK_PAD), bf(Hn * MLA_QK_PAD), bf(Hn * MLA_VW)],
        compiler_params=_params(1),
        name="mla_in",
    )(x, gain, mod, mod, win, wuq, wukv, q_norm.reshape(1, -1), kv_norm.reshape(1, -1),
      (qn_nope * (MLA_SCALE * LOG2E)).reshape(1, -1), _pad_lanes(qn_rope * (MLA_SCALE * LOG2E)),
      kn_nope.reshape(1, -1), _pad_lanes(kn_rope), *tables)


def rope_tables(n_ctx, n_lat):
    n_freq = MLA_ROPE // 4
    inv = ROPE_THETA ** (-jnp.arange(n_freq, dtype=F32) / n_freq)
    t = jnp.arange(n_lat)
    a_r = (t // GRID_W).astype(F32)[:, None] * inv
    a_c = (t % GRID_W).astype(F32)[:, None] * inv
    ang = jnp.concatenate([a_r, a_r, a_c, a_c], axis=-1)
    ang = jnp.concatenate([jnp.zeros((n_ctx, MLA_ROPE), F32), ang], axis=0)
    cos, sin = jnp.cos(ang), jnp.sin(ang)
    low = (jnp.arange(MLA_ROPE) % (2 * ROPE_HALF)) < ROPE_HALF
    pad = lambda a: jnp.pad(a, ((0, 0), (0, LANES - MLA_ROPE)))
    return pad(cos), pad(jnp.where(low, -sin, 0.0)), pad(jnp.where(low, 0.0, sin))


MLA_HPS = 4


def _mla_attn_kernel(q_ref, k_ref, v_ref, o_ref, *, n_ctx):
    def attend(n_keys):
        heads = range(MLA_HPS)
        s, p = {}, {}
        for hd in heads:
            q = q_ref[0, :, hd * MLA_QK_PAD:(hd + 1) * MLA_QK_PAD]
            k = k_ref[0, :n_keys, hd * MLA_QK_PAD:(hd + 1) * MLA_QK_PAD]
            s[hd] = lax.dot_general(q, k, (((1,), (1,)), ((), ())), preferred_element_type=F32)
        for hd in heads:
            p[hd] = jnp.exp2(s[hd] - s[hd].max(axis=1, keepdims=True)).astype(BF16)
        for hd in heads:
            v = v_ref[0, :n_keys, hd * MLA_VW:(hd + 1) * MLA_VW]
            r = jnp.dot(p[hd], v, preferred_element_type=F32)
            o_ref[0, :, hd * MLA_V:(hd + 1) * MLA_V] = (
                r[:, :MLA_V] / r[:, MLA_V:MLA_V + 1]).astype(o_ref.dtype)

    @pl.when(pl.program_id(2) == 0)
    def _():
        attend(n_ctx)

    @pl.when(pl.program_id(2) > 0)
    def _():
        attend(k_ref.shape[1])


def mla_attention(lay, q, k, v):
    assert lay.ctx_first
    Bn, S = lay.n_batch, lay.nb * TL
    q3, k3, v3 = (a.reshape(Bn, S, a.shape[-1]) for a in (q, k, v))
    out = pl.pallas_call(
        functools.partial(_mla_attn_kernel, n_ctx=TL),
        grid=(Bn, MLA_HEADS // MLA_HPS, lay.nb),
        in_specs=[pl.BlockSpec((1, TL, MLA_HPS * MLA_QK_PAD), lambda b, h, i: (b, i, h)),
                  pl.BlockSpec((1, S, MLA_HPS * MLA_QK_PAD), lambda b, h, i: (b, 0, h),
                               pipeline_mode=pl.Buffered(1)),
                  pl.BlockSpec((1, S, MLA_HPS * MLA_VW), lambda b, h, i: (b, 0, h),
                               pipeline_mode=pl.Buffered(1))],
        out_specs=pl.BlockSpec((1, TL, MLA_HPS * MLA_V), lambda b, h, i: (b, i, h)),
        out_shape=jax.ShapeDtypeStruct((Bn, S, MLA_HEADS * MLA_V), BF16),
        compiler_params=_params(3),
        name="mla_attention",
    )(q3, k3, v3)
    return out.reshape(lay.rows, MLA_HEADS * MLA_V)


N_PROLOGUE = {"conv": 5, "mlstm": 4, "mla": 1}


def _mixer_out_kernel(*refs, kind, nb, ctx_first, n_src):
    n_pro = N_PROLOGUE[kind]
    pro = refs[:n_pro]
    wout_ref = refs[n_pro]
    x_srcs = refs[n_pro + 1:n_pro + 1 + n_src]
    (ga_ref, gain_ref, sc_ref, sh_ref, wr_ref, br_ref,
     xo_ref, h2_ref, te_ref, gate_ref, rank_ref, cnt_ref, carry_ref) = refs[n_pro + 1 + n_src:]
    r = pl.program_id(0)

    if kind == "conv":
        vprev_ref, v_ref, vnext_ref, bg_ref, cw_ref = pro
        j = r % nb
        first = (j == 0) | (j == 1) if ctx_first else (j == 0)
        last = (j == nb - 1) | (j == 0) if ctx_first else (j == nb - 1)
        v = v_ref[...].astype(F32)
        rows = lax.broadcasted_iota(jnp.int32, (TL, 1), 0)
        prev_row = jnp.where(first, 0.0, vprev_ref[BF16_ROWS - 1:BF16_ROWS, :].astype(F32))
        next_row = jnp.where(last, 0.0, vnext_ref[0:1, :].astype(F32))
        up = jnp.where(rows == 0, prev_row, pltpu.roll(v, 1, 0))
        dn = jnp.where(rows == TL - 1, next_row, pltpu.roll(v, TL - 1, 0))
        cw = cw_ref[...]
        a = bg_ref[...].astype(F32) * (up * cw[0:1] + v * cw[1:2] + dn * cw[2:3])
    elif kind == "mlstm":
        hf_ref, hb_ref, og_ref, ng_ref = pro
        hh = hf_ref[...].astype(F32) + hb_ref[...].astype(F32)
        a = jnp.concatenate([_rms(hh[:, h * ML_DV:(h + 1) * ML_DV]) for h in range(ML_HEADS)], axis=1)
        a = a * ng_ref[...] * og_ref[...].astype(F32)
    else:
        a = pro[0][...]

    y = jnp.dot(a.astype(BF16), wout_ref[...], preferred_element_type=F32)
    xn = _token_block(x_srcs, r, nb) + ga_ref[0] * y
    xo_ref[...] = xn
    h2 = _norm_mod(xn, gain_ref[...], sc_ref[0], sh_ref[0])
    h2_ref[...] = _pack_rows(h2)
    lt = jnp.dot(h2.astype(BF16), wr_ref[...], preferred_element_type=F32).T
    logits = lt[:N_EXPERTS] + lt[N_EXPERTS:2 * N_EXPERTS] + br_ref[...]

    sub = lax.broadcasted_iota(jnp.int32, (N_EXPERTS, TL), 0)
    sub_k = lax.broadcasted_iota(jnp.int32, (TOP_K, TL), 0)
    work = logits
    sel = jnp.zeros((N_EXPERTS, TL), F32)
    top_e = jnp.zeros((TOP_K, TL), jnp.int32)
    top_v = jnp.zeros((TOP_K, TL), F32)
    picks = []
    for kk in range(TOP_K):
        m = work.max(axis=0, keepdims=True)
        idx = jnp.min(jnp.where(work == m, sub, N_EXPERTS), axis=0, keepdims=True)
        hit = sub == idx
        picks.append(hit)
        sel = jnp.where(hit, 1.0, sel)
        work = jnp.where(hit, -jnp.inf, work)
        top_e = jnp.where(sub_k == kk, idx, top_e)
        top_v = jnp.where(sub_k == kk, m, top_v)
    ex = jnp.exp(top_v - top_v[0:1])
    gate_ref[...] = ex / ex.sum(axis=0, keepdims=True)
    te_ref[...] = top_e

    @pl.when(r == 0)
    def _():
        carry_ref[...] = jnp.zeros_like(carry_ref)

    tr = lax.broadcasted_iota(jnp.int32, (TL, TL), 0)
    tc = lax.broadcasted_iota(jnp.int32, (TL, TL), 1)
    before = jnp.dot(sel.astype(BF16), (tr < tc).astype(BF16), preferred_element_type=F32)
    pos = before + carry_ref[...]
    rank = jnp.zeros((TOP_K, TL), F32)
    for kk in range(TOP_K):
        rk = jnp.sum(jnp.where(picks[kk], pos, 0.0), axis=0, keepdims=True)
        rank = jnp.where(sub_k == kk, rk, rank)
    rank_ref[...] = rank.astype(jnp.int32)
    total = carry_ref[...] + jnp.sum(sel, axis=1, keepdims=True)
    carry_ref[...] = total
    cnt_ref[...] = total


def mixer_out(lay, kind, pro_args, w_out, xs, mod, gain_f, w_r, b_r):
    D = D_MODEL
    nb = lay.nb
    if kind == "conv":
        v, bg, cw = pro_args
        per = TL // BF16_ROWS
        last_tile = lay.rows // BF16_ROWS - 1
        pro_specs = [pl.BlockSpec((BF16_ROWS, D), lambda r: (jnp.maximum(r * per - 1, 0), 0)),
                     lay.row_spec(D),
                     pl.BlockSpec((BF16_ROWS, D), lambda r: (jnp.minimum((r + 1) * per, last_tile), 0)),
                     lay.row_spec(D), _const_spec((CONV_WIDTH, D))]
        pro_in = [v, v, v, bg, cw]
    elif kind == "mlstm":
        h_f, h_b, og, ng = pro_args
        pro_specs = [lay.row_spec(ML_V), lay.row_spec(ML_V), lay.row_spec(ML_V), _const_spec((1, ML_V))]
        pro_in = [h_f, h_b, og, ng.reshape(1, ML_V)]
    else:
        pro_specs = [lay.row_spec(D)]
        pro_in = list(pro_args)
    k_in = w_out.shape[0]
    small = lambda dt: jax.ShapeDtypeStruct((TOP_K, lay.rows), dt)
    small_spec = pl.BlockSpec((TOP_K, TL), lambda r: (0, r))
    return pl.pallas_call(
        functools.partial(_mixer_out_kernel, kind=kind, nb=nb, ctx_first=lay.ctx_first, n_src=len(xs)),
        grid=(lay.n_blocks,),
        in_specs=pro_specs + [_const_spec((k_in, D))] + lay.token_specs(D, len(xs)) + [
            lay.mod_spec(2), _const_spec((1, D)), lay.mod_spec(4), lay.mod_spec(3),
            _const_spec((D, LANES)), _const_spec((N_EXPERTS, 1))],
        out_specs=[lay.row_spec(D), lay.row_spec(PACK_W), small_spec, small_spec, small_spec,
                   _const_spec((N_EXPERTS, 1))],
        out_shape=[jax.ShapeDtypeStruct((lay.rows, D), F32),
                   jax.ShapeDtypeStruct((lay.rows, PACK_W), jnp.int32),
                   small(jnp.int32), small(F32), small(jnp.int32),
                   jax.ShapeDtypeStruct((N_EXPERTS, 1), F32)],
        scratch_shapes=[pltpu.VMEM((N_EXPERTS, 1), F32)],
        compiler_params=_params(1),
        name="mixer_out_" + kind,
    )(*pro_in, w_out.astype(BF16), *xs, mod, gain_f, mod, mod, _split_weight_lanes(w_r),
      b_r.reshape(N_EXPERTS, 1))


def _expert_ffn_kernel(blk_e_ref, first_ref, valid_ref, x_ref, w1_ref, b1_ref, w2_ref, b2_ref, o_ref,
                       w1b_ref, w2b_ref):
    del blk_e_ref
    i = pl.program_id(0)

    @pl.when(first_ref[i] == 1)
    def _():
        w1b_ref[...] = w1_ref[0, 0].astype(BF16)
        w2b_ref[...] = w2_ref[0, 0].astype(BF16)

    @pl.when(valid_ref[i] > 0)
    def _():
        x = _unpack_rows(x_ref[...], BF16)
        h = jnp.dot(x, w1b_ref[...], preferred_element_type=F32) + b1_ref[0, 0]
        glu = jnp.minimum(h[:, :MOE_FF], SWIGLU_LIMIT)
        lin = jnp.clip(h[:, MOE_FF:], -SWIGLU_LIMIT, SWIGLU_LIMIT)
        act = glu * jax.nn.sigmoid(SWIGLU_ALPHA * glu) * (lin + 1.0)
        y = jnp.dot(act.astype(BF16), w2b_ref[...], preferred_element_type=F32)
        o_ref[...] = _pack_rows(y + b2_ref[0, 0])

    @pl.when(valid_ref[i] == 0)
    def _():
        o_ref[...] = jnp.zeros_like(o_ref)


def expert_ffn(layer, xp, blk_e, blk_first, blk_valid, w1, b1, w2, b2):
    n_rows = xp.shape[0]
    D, F2 = D_MODEL, 2 * MOE_FF
    n_blk = n_rows // MOE_BLOCK
    grid_spec = pltpu.PrefetchScalarGridSpec(
        num_scalar_prefetch=3,
        grid=(n_blk,),
        in_specs=[
            pl.BlockSpec((MOE_BLOCK, PACK_W), lambda i, be, fi, nu: (i, 0)),
            pl.BlockSpec((1, 1, D, F2), lambda i, be, fi, nu: (layer, be[i], 0, 0)),
            pl.BlockSpec((1, 1, 1, F2), lambda i, be, fi, nu: (layer, be[i], 0, 0)),
            pl.BlockSpec((1, 1, MOE_FF, D), lambda i, be, fi, nu: (layer, be[i], 0, 0)),
            pl.BlockSpec((1, 1, 1, D), lambda i, be, fi, nu: (layer, be[i], 0, 0)),
        ],
        out_specs=pl.BlockSpec((MOE_BLOCK, PACK_W), lambda i, be, fi, nu: (i, 0)),
        scratch_shapes=[pltpu.VMEM((D, F2), BF16), pltpu.VMEM((MOE_FF, D), BF16)],
    )
    return pl.pallas_call(
        _expert_ffn_kernel,
        grid_spec=grid_spec,
        out_shape=jax.ShapeDtypeStruct((n_rows, PACK_W), jnp.int32),
        compiler_params=_params(1),
        name="expert_ffn",
    )(blk_e, blk_first, blk_valid, xp, w1, b1, w2, b2)


SC_CORES = 2
SC_SUBCORES = 16
SC_CHUNKS = (64, 32)


def _sc_chunk(*counts):
    n_workers = SC_CORES * SC_SUBCORES
    for chunk in SC_CHUNKS:
        if all(n % (chunk * n_workers) == 0 for n in counts):
            return chunk
    raise ValueError(f"row counts {counts} do not split over {n_workers} subcores")


def sc_gather(table, idx):
    n_idx = idx.shape[0]
    width = table.shape[1]
    n_workers = SC_CORES * SC_SUBCORES
    per_worker = n_idx // n_workers
    chunk = _sc_chunk(n_idx)
    n_chunks = per_worker // chunk
    assert n_chunks * chunk * n_workers == n_idx and n_chunks % 2 == 0
    mesh = plsc.VectorSubcoreMesh(core_axis_name="c", subcore_axis_name="s",
                                  num_cores=SC_CORES, num_subcores=SC_SUBCORES)

    def body(table_hbm, idx_hbm, out_hbm, idx_v, rows_v, gsem, wsem):
        wid = lax.axis_index("s") * SC_CORES + lax.axis_index("c")
        pltpu.sync_copy(idx_hbm.at[wid], idx_v)

        def gather(ci, slot):
            return pltpu.make_async_copy(table_hbm.at[idx_v.at[ci]], rows_v.at[slot], gsem.at[slot])

        def write(ci, slot):
            return pltpu.make_async_copy(rows_v.at[slot], out_hbm.at[ci, wid], wsem.at[slot])

        gather(0, 0).start()

        @pl.loop(0, n_chunks, step=2)
        def _(c0):
            for slot in range(2):
                ci = c0 + slot
                other = 1 - slot

                @pl.when(ci + 1 < n_chunks)
                def _():
                    @pl.when(ci >= 1)
                    def _():
                        write(ci - 1, other).wait()
                    gather(ci + 1, other).start()

                gather(ci, slot).wait()
                write(ci, slot).start()

        write(n_chunks - 2, 0).wait()
        write(n_chunks - 1, 1).wait()

    out = pl.kernel(
        body,
        out_type=jax.ShapeDtypeStruct((n_chunks, n_workers, chunk, width), table.dtype),
        mesh=mesh,
        scratch_types=[pltpu.VMEM((n_chunks, chunk), jnp.int32),
                       pltpu.VMEM((2, chunk, width), table.dtype),
                       pltpu.SemaphoreType.DMA((2,)),
                       pltpu.SemaphoreType.DMA((2,))],
        name="sc_gather",
    )(table, idx.reshape(n_chunks, n_workers, chunk).transpose(1, 0, 2))
    return out.reshape(n_idx, width)


def sc_dispatch(table, dest, pad_rows):
    n_tok, width = table.shape
    n_picks = dest.shape[0]
    n_pad = pad_rows.shape[0]
    n_workers = SC_CORES * SC_SUBCORES
    chunk = _sc_chunk(n_tok, n_pad)
    per_w = n_tok // chunk // n_workers
    pad_w = n_pad // chunk // n_workers
    assert per_w * chunk * n_workers == n_tok and pad_w * chunk * n_workers == n_pad
    mesh = plsc.VectorSubcoreMesh(core_axis_name="c", subcore_axis_name="s",
                                  num_cores=SC_CORES, num_subcores=SC_SUBCORES)

    def body(table_hbm, idx_hbm, pad_hbm, zero_hbm, out_hbm, idx_v, pad_v, rows_v, zero_v,
             rsem, ssem, zsem):
        wid = lax.axis_index("s") * SC_CORES + lax.axis_index("c")
        pltpu.sync_copy(idx_hbm.at[wid], idx_v)
        pltpu.sync_copy(pad_hbm.at[wid], pad_v)
        pltpu.sync_copy(zero_hbm, zero_v)

        def zero_fill(pc):
            return pltpu.make_async_copy(zero_v, out_hbm.at[pad_v.at[pc]], zsem)

        for pc in range(pad_w):
            zero_fill(pc).start()

        def scatter(ci, kk):
            return pltpu.make_async_copy(rows_v, out_hbm.at[idx_v.at[ci * n_picks + kk]], ssem)

        @pl.loop(0, per_w)
        def _(ci):
            pltpu.async_copy(table_hbm.at[ci, wid], rows_v, rsem).wait()
            for kk in range(n_picks):
                scatter(ci, kk).start()
            for kk in range(n_picks):
                scatter(ci, kk).wait()

        for pc in range(pad_w):
            zero_fill(pc).wait()

    idx = dest.reshape(n_picks, per_w, n_workers, chunk).transpose(2, 1, 0, 3)
    idx = idx.reshape(n_workers, per_w * n_picks, chunk)
    return pl.kernel(
        body,
        out_type=jax.ShapeDtypeStruct((n_tok * n_picks + n_pad, width), table.dtype),
        mesh=mesh,
        scratch_types=[pltpu.VMEM((per_w * n_picks, chunk), jnp.int32),
                       pltpu.VMEM((pad_w, chunk), jnp.int32),
                       pltpu.VMEM((chunk, width), table.dtype),
                       pltpu.VMEM((chunk, width), table.dtype),
                       pltpu.SemaphoreType.DMA, pltpu.SemaphoreType.DMA, pltpu.SemaphoreType.DMA],
        name="sc_dispatch",
    )(table.reshape(per_w, n_workers, chunk, width), idx,
      pad_rows.reshape(n_workers, pad_w, chunk), jnp.zeros((chunk, width), table.dtype))


def _combine_kernel(x_ref, *refs):
    y_refs, (gate_ref, gf_ref, o_ref) = refs[:TOP_K], refs[TOP_K:]
    gates = gate_ref[...]
    acc = gates[:, 0:1] * _unpack_rows(y_refs[0][...], F32)
    for kk in range(1, TOP_K):
        acc = acc + gates[:, kk:kk + 1] * _unpack_rows(y_refs[kk][...], F32)
    o_ref[...] = x_ref[...] + gf_ref[0] * acc


def moe_combine(lay, x, yg, gates, mod, drop_ctx):
    D = D_MODEL
    if drop_ctx:
        nbo = lay.nb - 1
        src = lambda r: (r // nbo) * lay.nb + 1 + r % nbo
        n_out = lay.n_batch * nbo
    else:
        src = lambda r: r
        n_out = lay.n_blocks
    y_specs = [pl.BlockSpec((TL, PACK_W), functools.partial(lambda kk, r: (kk * lay.n_blocks + src(r), 0), kk))
               for kk in range(TOP_K)]
    return pl.pallas_call(
        _combine_kernel,
        grid=(n_out,),
        in_specs=[pl.BlockSpec((TL, D), lambda r: (src(r), 0))] + y_specs + [
            pl.BlockSpec((TL, TOP_K), lambda r: (src(r), 0)),
            pl.BlockSpec((1, 1, D), lambda r: (lay.mod_row(src(r)), 0, 5))],
        out_specs=pl.BlockSpec((TL, D), lambda r: (r, 0)),
        out_shape=jax.ShapeDtypeStruct((n_out * TL, D), F32),
        compiler_params=_params(1),
        name="moe_combine",
    )(x, yg, yg, yg, yg, gates, mod)


def moe_route(top_e, rank, counts):
    T = top_e.shape[1]
    assert (T * TOP_K) % MOE_BLOCK == 0
    counts = counts.reshape(N_EXPERTS).astype(jnp.int32)
    padded = (counts + MOE_BLOCK - 1) // MOE_BLOCK * MOE_BLOCK
    padded_end = jnp.cumsum(padded)
    padded_start = padded_end - padded
    experts = jnp.arange(N_EXPERTS)
    start_of = jnp.sum(jnp.where(top_e[..., None] == experts, padded_start, 0), axis=-1)
    dest = (start_of + rank).astype(jnp.int32)
    n_pad = N_EXPERTS * MOE_BLOCK
    n_rows = T * TOP_K + n_pad
    n_blk = n_rows // MOE_BLOCK
    blk_start = jnp.arange(n_blk) * MOE_BLOCK
    blk_e = jnp.minimum(jnp.sum(padded_end[None, :] <= blk_start[:, None], axis=1), N_EXPERTS - 1)
    blk_e = blk_e.astype(jnp.int32)
    blk_first = jnp.concatenate([jnp.ones((1,), jnp.int32), (blk_e[1:] != blk_e[:-1]).astype(jnp.int32)])
    blk_hot = blk_e[:, None] == experts
    in_grp = blk_start - jnp.sum(jnp.where(blk_hot, padded_start, 0), axis=-1)
    blk_valid = jnp.clip(jnp.sum(jnp.where(blk_hot, counts, 0), axis=-1) - in_grp, 0, MOE_BLOCK)
    blk_valid = jnp.where(blk_start < padded_end[-1], blk_valid, 0).astype(jnp.int32)
    tail = padded - counts
    tail_end = jnp.cumsum(tail)
    j = jnp.arange(n_pad)
    owner = jnp.sum(tail_end[None, :] <= j[:, None], axis=1)
    base = padded_start + counts - (tail_end - tail)
    in_group = jnp.sum(jnp.where(owner[:, None] == experts, base, 0), axis=-1) + j
    pad_rows = jnp.where(j < tail_end[-1], in_group, padded_end[-1] + j - tail_end[-1])
    return dest, pad_rows.astype(jnp.int32), blk_e, blk_first, blk_valid


def kernel(x, c, ctx, c_ctx, norm_mix, norm_ffn, w_mod, b_mod, conv_w_in, conv_w, conv_w_out, ml_w_in, ml_b_gate, ml_norm, ml_w_out, mla_w_in, mla_q_norm, mla_kv_norm, mla_w_uq, mla_w_ukv, mla_qn_nope, mla_qn_rope, mla_kn_nope, mla_kn_rope, mla_w_out, moe_w_router, moe_b_router, moe_w1, moe_b1, moe_w2, moe_b2):
    Bn, n_lat, D = x.shape
    n_ctx = ctx.shape[1]
    assert D == D_MODEL and n_ctx == TL and n_lat % TL == 0
    assert (DEPTH - 1) % N_MIXERS == 0
    full = Layout(Bn, (n_ctx + n_lat) // TL, True)
    lat_only = Layout(Bn, n_lat // TL, False)
    mods = ada_all(c, c_ctx, w_mod, b_mod)
    tables = rope_tables(n_ctx, n_lat)
    b1_all = moe_b1.reshape(DEPTH, N_EXPERTS, 1, 2 * MOE_FF)
    b2_all = moe_b2.reshape(DEPTH, N_EXPERTS, 1, D)
    Xs = (ctx.reshape(Bn * n_ctx, D), x.reshape(Bn * n_lat, D))
    for layer in range(DEPTH):
        kind, j = layer % N_MIXERS, layer // N_MIXERS
        lay = lat_only if layer == DEPTH - 1 else full
        mod = mods[layer]
        gain_a = norm_mix[layer].reshape(1, D)
        gain_f = norm_ffn[layer].reshape(1, D)
        if kind == 0:
            bg, v = conv_in(lay, Xs, gain_a, mod, conv_w_in[j].astype(BF16))
            pro, w_out, name = (v, bg, conv_w[j]), conv_w_out[j], "conv"
        elif kind == 1:
            w = ml_w_in[j]
            n_main = 2 * ML_QK + 2 * ML_V
            w_main = jnp.concatenate([w[:, :ML_QK] * ML_DQK ** -0.5, w[:, ML_QK:n_main]], axis=1)
            q, k, kt, v, og, g, gt = mlstm_in(lay, Xs[0], gain_a, mod, w_main.astype(BF16),
                                              w[:, n_main:], ml_b_gate[j])
            h_f, h_b = mlstm_scan(lay, q, k, kt, v, g, gt)
            pro, w_out, name = (h_f, h_b, og, ml_norm[j]), ml_w_out[j], "mlstm"
        else:
            q, k, v = mla_in(lay, Xs[0], gain_a, mod, mla_w_in[j], mla_q_norm[j], mla_kv_norm[j],
                             mla_w_uq[j], mla_w_ukv[j], mla_qn_nope[j], mla_qn_rope[j],
                             mla_kn_nope[j], mla_kn_rope[j], tables)
            pro, w_out, name = (mla_attention(lay, q, k, v),), mla_w_out[j], "mla"
        X, h2, top_e, gates, rank, counts = mixer_out(
            lay, name, pro, w_out, Xs, mod, gain_f, moe_w_router[layer], moe_b_router[layer])
        dest, pad_rows, blk_e, blk_first, blk_valid = moe_route(top_e, rank, counts)
        xp = sc_dispatch(h2, dest, pad_rows)
        yp = expert_ffn(layer, xp, blk_e, blk_first, blk_valid, moe_w1, b1_all, moe_w2, b2_all)
        yg = sc_gather(yp, dest.reshape(-1))
        Xs = (moe_combine(lay, X, yg, gates.T, mod, drop_ctx=(layer == DEPTH - 2)),)
    return Xs[0].reshape(Bn, n_lat, D)
```

```python
import functools

import jax
import jax.numpy as jnp
from jax import lax
from jax.experimental import pallas as pl
from jax.experimental.pallas import tpu as pltpu
from jax.experimental.pallas import tpu_sc as plsc

D_MODEL = 1024
DEPTH = 4
GRID_W = 64
N_MIXERS = 3
N_ADA = 6
RMS_EPS = 1e-6
CONV_WIDTH = 3
ML_HEADS = 8
ML_DQK = 64
ML_DV = 128
ML_QK = ML_HEADS * ML_DQK
ML_V = ML_HEADS * ML_DV
GATE_CAP = 15.0
MLA_HEADS = 8
MLA_NOPE = 128
MLA_ROPE = 64
MLA_V = 128
MLA_Q_LORA = 384
MLA_KV_LORA = 256
MLA_SCALE = (MLA_NOPE + MLA_ROPE) ** -0.5
ROPE_THETA = 10000.0
N_EXPERTS = 32
TOP_K = 4
MOE_FF = D_MODEL
SWIGLU_ALPHA = 1.702
SWIGLU_LIMIT = 7.0
MOE_BLOCK = 512

TL = 256
LANES = 128
BF16_ROWS = 16
VMEM_LIMIT = 48 * 1024 * 1024
HI = lax.Precision.HIGHEST
F32 = jnp.float32
BF16 = jnp.bfloat16


def _params(n_axes):
    return pltpu.CompilerParams(dimension_semantics=("arbitrary",) * n_axes,
                                vmem_limit_bytes=VMEM_LIMIT)


def _rms(x, width=None):
    width = x.shape[-1] if width is None else width
    return x * lax.rsqrt(jnp.sum(x * x, axis=-1, keepdims=True) * (1.0 / width) + RMS_EPS)


def _norm_mod(x, gain, scale, shift):
    return _rms(x) * (gain * (1.0 + scale)) + shift


def _split_weight_lanes(w):
    hi = w.astype(BF16)
    lo = (w - hi.astype(F32)).astype(BF16)
    return jnp.pad(jnp.concatenate([hi, lo], axis=1), ((0, 0), (0, LANES - 2 * w.shape[1])))


PACK_W = D_MODEL // 2
HIGH_HALF = -65536


def _pack_rows(x):
    xb = x.astype(BF16).astype(F32)
    lo = lax.bitcast_convert_type(xb[:, :PACK_W], jnp.int32)
    hi = lax.bitcast_convert_type(xb[:, PACK_W:], jnp.int32)
    return hi | lax.shift_right_logical(lo, 16)


def _unpack_rows(w, dtype):
    lo = lax.bitcast_convert_type(lax.shift_left(w, 16), F32)
    hi = lax.bitcast_convert_type(w & HIGH_HALF, F32)
    return jnp.concatenate([lo.astype(dtype), hi.astype(dtype)], axis=1)


class Layout:
    def __init__(self, n_batch, nb, ctx_first):
        self.n_batch, self.nb, self.ctx_first = n_batch, nb, ctx_first
        self.n_blocks = n_batch * nb
        self.rows = self.n_blocks * TL

    def mod_row(self, r):
        b = r // self.nb
        return jnp.where(r % self.nb == 0, self.n_batch, b) if self.ctx_first else b

    def row_spec(self, width):
        return pl.BlockSpec((TL, width), lambda r: (r, 0))

    def mod_spec(self, piece):
        return pl.BlockSpec((1, 1, D_MODEL), lambda r: (self.mod_row(r), 0, piece))

    def token_specs(self, width, n_src):
        if n_src == 1:
            return [self.row_spec(width)]
        if n_src == N_COMBINE_SRC:
            picks = [pl.BlockSpec((TL, PACK_W), functools.partial(lambda kk, r: (kk * self.n_blocks + r, 0), kk))
                     for kk in range(TOP_K)]
            return [self.row_spec(width)] + picks + [self.row_spec(TOP_K), self.mod_spec(5)]
        nb = self.nb
        return [pl.BlockSpec((TL, width), lambda r: (r // nb, 0)),
                pl.BlockSpec((TL, width), lambda r: ((r // nb) * (nb - 1) + jnp.maximum(r % nb - 1, 0), 0))]


N_COMBINE_SRC = TOP_K + 3


def _token_block(srcs, r, nb):
    if len(srcs) == 1:
        return srcs[0][...]
    if len(srcs) == N_COMBINE_SRC:
        gates = srcs[TOP_K + 1][...]
        acc = gates[:, 0:1] * _unpack_rows(srcs[1][...], F32)
        for kk in range(1, TOP_K):
            acc = acc + gates[:, kk:kk + 1] * _unpack_rows(srcs[1 + kk][...], F32)
        return srcs[0][...] + srcs[TOP_K + 2][0] * acc
    return jnp.where(r % nb == 0, srcs[0][...], srcs[1][...])


def _const_spec(shape):
    return pl.BlockSpec(shape, lambda *_: (0,) * len(shape))


ADA_ROWS = 16
ADA_TN = 1536


def _ada_kernel(c_ref, w_ref, b_ref, o_ref):
    c = c_ref[...]
    s = c * jax.nn.sigmoid(c)
    o_ref[0] = jnp.dot(s, w_ref[0], precision=HI, preferred_element_type=F32) + b_ref[0]


def ada_all(c, c_ctx, w_mod, b_mod):
    Bn, D = c.shape
    assert Bn + 1 <= ADA_ROWS
    cond = jnp.zeros((ADA_ROWS, D), F32).at[:Bn].set(c).at[Bn].set(c_ctx)
    out = pl.pallas_call(
        _ada_kernel,
        grid=(DEPTH, N_ADA * D // ADA_TN),
        in_specs=[pl.BlockSpec((ADA_ROWS, D), lambda l, n: (0, 0)),
                  pl.BlockSpec((1, D, ADA_TN), lambda l, n: (l, 0, n)),
                  pl.BlockSpec((1, 1, ADA_TN), lambda l, n: (l, 0, n))],
        out_specs=pl.BlockSpec((1, ADA_ROWS, ADA_TN), lambda l, n: (l, 0, n)),
        out_shape=jax.ShapeDtypeStruct((DEPTH, ADA_ROWS, N_ADA * D), F32),
        compiler_params=_params(2),
        name="ada_mod",
    )(cond, w_mod, b_mod.reshape(DEPTH, 1, N_ADA * D))
    return out[:, :Bn + 1, None, :]


def _conv_in_kernel(*refs, nb, n_src):
    srcs, (gain_ref, sc_ref, sh_ref, w_ref, bg_ref, v_ref) = refs[:n_src], refs[n_src:]
    D = D_MODEL
    x = _token_block(srcs, pl.program_id(0), nb)
    h = _norm_mod(x, gain_ref[...], sc_ref[0], sh_ref[0]).astype(BF16)
    p = jnp.dot(h, w_ref[...], preferred_element_type=F32)
    bg_ref[...] = p[:, :D].astype(BF16)
    v_ref[...] = (p[:, D:2 * D] * p[:, 2 * D:]).astype(BF16)


def conv_in(lay, xs, gain, mod, w_in):
    D = D_MODEL
    sds = jax.ShapeDtypeStruct((lay.rows, D), BF16)
    return pl.pallas_call(
        functools.partial(_conv_in_kernel, nb=lay.nb, n_src=len(xs)),
        grid=(lay.n_blocks,),
        in_specs=lay.token_specs(D, len(xs)) + [_const_spec((1, D)), lay.mod_spec(1), lay.mod_spec(0),
                                                _const_spec((D, 3 * D))],
        out_specs=[lay.row_spec(D), lay.row_spec(D)],
        out_shape=[sds, sds],
        compiler_params=_params(1),
        name="conv_in",
    )(*xs, gain, mod, mod, w_in)


ML_T = 256
ML_SW = 2 * ML_DV
ML_NG = 4 * ML_HEADS


LOG2E = 1.4426950408889634


def _gate_act(g, is_forget):
    g = GATE_CAP * jnp.tanh(g * (1.0 / GATE_CAP))
    log_sig = jnp.minimum(g, 0.0) - jnp.log(1.0 + jnp.exp(-jnp.abs(g)))
    return jnp.where(is_forget, log_sig, g) * LOG2E


def _mlstm_in_kernel(*refs, nb, n_src):
    srcs, (gain_ref, sc_ref, sh_ref, w_ref, wg_ref, bg_ref), outs = refs[:n_src], refs[n_src:n_src + 6], refs[n_src + 6:]
    q_ref, k_ref, kt_ref, v_ref, og_ref, g_ref, gt_ref = outs[:7]
    x = _token_block(srcs, pl.program_id(0), nb)
    if n_src == N_COMBINE_SRC:
        outs[7][...] = x
    h = _norm_mod(x, gain_ref[...], sc_ref[0], sh_ref[0])
    hb = h.astype(BF16)
    p = jnp.dot(hb, w_ref[...], preferred_element_type=F32)
    q_ref[...] = p[:, :ML_QK].astype(BF16)
    k_ref[...] = p[:, ML_QK:2 * ML_QK].astype(BF16)
    kt_ref[...] = p[:, ML_QK:2 * ML_QK].T.astype(BF16)
    v_ref[...] = p[:, 2 * ML_QK:2 * ML_QK + ML_V].astype(BF16)
    og_ref[...] = jax.nn.sigmoid(p[:, 2 * ML_QK + ML_V:]).astype(BF16)
    h_lo = (h - hb.astype(F32)).astype(BF16)
    g = jnp.dot(jnp.concatenate([hb, h_lo, hb], axis=1), wg_ref[...],
                preferred_element_type=F32) + bg_ref[...]
    col = lax.broadcasted_iota(jnp.int32, g.shape, 1)
    g = _gate_act(g, (col // ML_HEADS) % 2 == 1)
    g_ref[...] = g[:, :ML_NG]
    gt_ref[...] = g.T[:ML_NG]


def _row_outputs(lay, n_src):
    if n_src != N_COMBINE_SRC:
        return [], []
    return [lay.row_spec(D_MODEL)], [jax.ShapeDtypeStruct((lay.rows, D_MODEL), F32)]


def mlstm_in(lay, xs, gain, mod, w_main, w_g, b_g):
    D = D_MODEL
    x_specs, x_shapes = _row_outputs(lay, len(xs))
    n_main = 2 * ML_QK + 2 * ML_V
    g_hi = w_g.astype(BF16)
    g_lo = (w_g - g_hi.astype(F32)).astype(BF16)
    wg3 = jnp.pad(jnp.concatenate([g_hi, g_hi, g_lo], axis=0), ((0, 0), (0, LANES - ML_NG)))
    bf = lambda w: jax.ShapeDtypeStruct((lay.rows, w), BF16)
    return pl.pallas_call(
        functools.partial(_mlstm_in_kernel, nb=lay.nb, n_src=len(xs)),
        grid=(lay.n_blocks,),
        in_specs=lay.token_specs(D, len(xs)) + [
            _const_spec((1, D)), lay.mod_spec(1), lay.mod_spec(0),
            _const_spec((D, n_main)), _const_spec((3 * D, LANES)), _const_spec((1, LANES))],
        out_specs=[lay.row_spec(ML_QK), lay.row_spec(ML_QK), pl.BlockSpec((ML_QK, TL), lambda r: (0, r)),
                   lay.row_spec(ML_V), lay.row_spec(ML_V),
                   lay.row_spec(ML_NG), pl.BlockSpec((ML_NG, TL), lambda r: (0, r))] + x_specs,
        out_shape=[bf(ML_QK), bf(ML_QK), jax.ShapeDtypeStruct((ML_QK, lay.rows), BF16),
                   bf(ML_V), bf(ML_V),
                   jax.ShapeDtypeStruct((lay.rows, ML_NG), F32),
                   jax.ShapeDtypeStruct((ML_NG, lay.rows), F32)] + x_shapes,
        compiler_params=_params(1),
        name="mlstm_in",
    )(*xs, gain, mod, mod, w_main, wg3, _pad_lanes(b_g))


def _split3(x):
    hi = x.astype(BF16)
    r1 = x - hi.astype(F32)
    mid = r1.astype(BF16)
    lo = (r1 - mid.astype(F32)).astype(BF16)
    return hi, mid, lo


def _dot_exact01(x, sel01, x_on_left):
    sel = sel01.astype(BF16)
    parts = [jnp.dot(p, sel, preferred_element_type=F32) if x_on_left
             else jnp.dot(sel, p, preferred_element_type=F32) for p in _split3(x)]
    return parts[0] + parts[1] + parts[2]


def _mlstm_dir(reverse, q_ref, k_ref, kt_ref, v_ref, g_ref, gt_ref, o_ref, s_ref, m_ref):
    T = ML_T
    row = lax.broadcasted_iota(jnp.int32, (T, T), 0)
    col = lax.broadcasted_iota(jnp.int32, (T, T), 1)
    mask = (col >= row) if reverse else (col <= row)
    gt = gt_ref[...]
    bc = _dot_exact01(g_ref[...], mask, x_on_left=False)
    br = _dot_exact01(gt, (row >= col) if reverse else (row <= col), x_on_left=True)
    gi, gf = (2 * ML_HEADS, 3 * ML_HEADS) if reverse else (0, ML_HEADS)
    lane = lax.broadcasted_iota(jnp.int32, (T, 2 * ML_DQK), 1)
    sub = lax.broadcasted_iota(jnp.int32, (2 * ML_DQK, T), 0)
    pick = (lax.broadcasted_iota(jnp.int32, (ML_NG, ML_V), 0)
            == gf + lax.broadcasted_iota(jnp.int32, (ML_NG, ML_V), 1) // ML_DV)
    b_all = _dot_exact01(bc, pick, x_on_left=True)
    ones = jnp.ones((T, ML_DV), BF16)
    heads = range(ML_HEADS)
    qm, vx, s_raw = {}, {}, {}
    for h in heads:
        pair = (h // 2) * 2 * ML_DQK
        own = (lane >= ML_DQK) if (h % 2) else (lane < ML_DQK)
        qp = q_ref[:, pair:pair + 2 * ML_DQK]
        qm[h] = jnp.where(own, qp, jnp.zeros_like(qp))
        vx[h] = jnp.concatenate([v_ref[:, h * ML_DV:(h + 1) * ML_DV], ones], axis=1)
        s_raw[h] = lax.dot_general(qm[h], k_ref[:, pair:pair + 2 * ML_DQK], (((1,), (1,)), ((), ())),
                                   preferred_element_type=F32)
    p, a, u_rep, m_prev, s_prev = {}, {}, {}, {}, {}
    for h in heads:
        c_row = gt[gi + h:gi + h + 1, :] - br[gf + h:gf + h + 1, :]
        e = jnp.where(mask, c_row, -jnp.inf)
        m_prev[h] = m_ref[h][0:1, 0:1]
        u = jnp.maximum(m_prev[h], jnp.max(e, axis=1, keepdims=True))
        p[h] = (s_raw[h] * jnp.exp2(e - u)).astype(BF16)
        u_rep[h] = jnp.broadcast_to(u, (T, ML_DV))
        a[h] = jnp.exp2(m_prev[h] - u_rep[h])
        s_prev[h] = s_ref[h]
    for h in heads:
        r = jnp.dot(p[h], vx[h], preferred_element_type=F32)
        qs = jnp.dot(qm[h], s_prev[h].astype(BF16), preferred_element_type=F32)
        num = r[:, :ML_DV] + a[h] * qs[:, :ML_DV]
        den = r[:, ML_DV:] + a[h] * qs[:, ML_DV:]
        floor = jnp.exp2(-(b_all[:, h * ML_DV:(h + 1) * ML_DV] + u_rep[h]))
        o_ref[:, h * ML_DV:(h + 1) * ML_DV] = (num / jnp.maximum(jnp.abs(den), floor)).astype(o_ref.dtype)
    for h in heads:
        pair = (h // 2) * 2 * ML_DQK
        b_row = br[gf + h:gf + h + 1, :]
        tot = b_row[:, 0:1] if reverse else b_row[:, T - 1:T]
        g_row = tot - b_row + gt[gi + h:gi + h + 1, :]
        m_new = jnp.maximum(tot + m_prev[h], jnp.max(g_row, axis=1, keepdims=True))
        decay = jnp.exp2(tot + m_prev[h] - m_new)
        wk = jnp.exp2(g_row - m_new)
        own_t = (sub >= ML_DQK) if (h % 2) else (sub < ML_DQK)
        kt = kt_ref[pair:pair + 2 * ML_DQK, :].astype(F32)
        kw = jnp.where(own_t, kt * wk, 0.0).astype(BF16)
        s_ref[h] = decay * s_prev[h] + jnp.dot(kw, vx[h], preferred_element_type=F32)
        m_ref[h] = jnp.broadcast_to(m_new, m_ref.shape[1:])


def _mlstm_scan_kernel(qf_ref, kf_ref, ktf_ref, vf_ref, gf_ref, gtf_ref,
                       qb_ref, kb_ref, ktb_ref, vb_ref, gb_ref, gtb_ref,
                       of_ref, ob_ref, sf_ref, mf_ref, sb_ref, mb_ref):
    @pl.when(pl.program_id(1) == 0)
    def _():
        sf_ref[...] = jnp.zeros_like(sf_ref)
        mf_ref[...] = jnp.zeros_like(mf_ref)
        sb_ref[...] = jnp.zeros_like(sb_ref)
        mb_ref[...] = jnp.zeros_like(mb_ref)

    _mlstm_dir(False, qf_ref, kf_ref, ktf_ref, vf_ref, gf_ref, gtf_ref, of_ref, sf_ref, mf_ref)
    _mlstm_dir(True, qb_ref, kb_ref, ktb_ref, vb_ref, gb_ref, gtb_ref, ob_ref, sb_ref, mb_ref)


def mlstm_scan(lay, q, k, kt, v, g, gt):
    assert lay.ctx_first
    per = TL // ML_T
    nb = lay.nb * per
    rev = lambda j: jnp.where(j < per, per - 1 - j, nb + per - 1 - j)
    fwd = lambda b, j: (b * nb + j, 0)
    bwd = lambda b, j: (b * nb + rev(j), 0)
    fwd_t = lambda b, j: (0, b * nb + j)
    bwd_t = lambda b, j: (0, b * nb + rev(j))

    def specs(im, imt):
        return [pl.BlockSpec((ML_T, ML_QK), im), pl.BlockSpec((ML_T, ML_QK), im),
                pl.BlockSpec((ML_QK, ML_T), imt), pl.BlockSpec((ML_T, ML_V), im),
                pl.BlockSpec((ML_T, ML_NG), im), pl.BlockSpec((ML_NG, ML_T), imt)]

    out_sds = jax.ShapeDtypeStruct((lay.rows, ML_V), BF16)
    state = [pltpu.VMEM((ML_HEADS, 2 * ML_DQK, ML_SW), F32),
             pltpu.VMEM((ML_HEADS, 8, LANES), F32)]
    return pl.pallas_call(
        _mlstm_scan_kernel,
        grid=(lay.n_batch, nb),
        in_specs=specs(fwd, fwd_t) + specs(bwd, bwd_t),
        out_specs=[pl.BlockSpec((ML_T, ML_V), fwd), pl.BlockSpec((ML_T, ML_V), bwd)],
        out_shape=[out_sds, out_sds],
        scratch_shapes=state + state,
        compiler_params=_params(2),
        name="mlstm_scan",
    )(q, k, kt, v, g, gt, q, k, kt, v, g, gt)


MLA_QK_PAD = 256
MLA_VW = 2 * MLA_V
MLA_IN_PAD = MLA_Q_LORA + MLA_KV_LORA + LANES
ROPE_HALF = MLA_ROPE // 4


def _mla_in_kernel(*refs, nb, n_src):
    srcs, rest = refs[:n_src], refs[n_src:]
    (gain_ref, sc_ref, sh_ref, win_ref, wuq_ref, wukv_ref, qn_ref, kvn_ref,
     qnn_ref, qnr_ref, knn_ref, knr_ref, cos_ref, sa_ref, sb_ref) = rest[:15]
    q_out, k_out, v_out = rest[15:18]
    x = _token_block(srcs, pl.program_id(0), nb)
    if n_src == N_COMBINE_SRC:
        rest[18][...] = x
    h = _norm_mod(x, gain_ref[...], sc_ref[0], sh_ref[0]).astype(BF16)
    p = jnp.dot(h, win_ref[...], preferred_element_type=F32)
    cq = _rms(p[:, :MLA_Q_LORA]) * qn_ref[...]
    ckv = _rms(p[:, MLA_Q_LORA:MLA_Q_LORA + MLA_KV_LORA]) * kvn_ref[...]
    kr = p[:, MLA_Q_LORA + MLA_KV_LORA:]
    q = jnp.dot(cq.astype(BF16), wuq_ref[...], preferred_element_type=F32)
    kv = jnp.dot(ckv.astype(BF16), wukv_ref[...], preferred_element_type=F32)
    nv = MLA_HEADS * MLA_NOPE
    ones_col = (lax.broadcasted_iota(jnp.int32, (TL, MLA_VW - MLA_V), 1) == 0).astype(BF16)
    cos, sa, sb = cos_ref[...], sa_ref[...], sb_ref[...]

    def rope(xp):
        return (xp * cos + pltpu.roll(xp, LANES - ROPE_HALF, 1) * sa + pltpu.roll(xp, ROPE_HALF, 1) * sb)

    heads = range(MLA_HEADS)
    kr = _rms(kr, MLA_ROPE) * knr_ref[...]
    qn = {hd: _rms(q[:, hd * MLA_QK_PAD:hd * MLA_QK_PAD + MLA_NOPE]) * qnn_ref[...] for hd in heads}
    qr = {hd: _rms(q[:, hd * MLA_QK_PAD + MLA_NOPE:(hd + 1) * MLA_QK_PAD], MLA_ROPE) * qnr_ref[...]
          for hd in heads}
    kn = {hd: _rms(kv[:, hd * MLA_NOPE:(hd + 1) * MLA_NOPE]) * knn_ref[...] for hd in heads}
    kr = rope(kr).astype(BF16)
    qr = {hd: rope(qr[hd]) for hd in heads}
    for hd in heads:
        c0 = hd * MLA_QK_PAD
        q_out[:, c0:c0 + MLA_NOPE] = qn[hd].astype(BF16)
        q_out[:, c0 + MLA_NOPE:c0 + MLA_QK_PAD] = qr[hd].astype(BF16)
        v0 = nv + hd * MLA_V
        v_out[:, hd * MLA_VW:hd * MLA_VW + MLA_V] = kv[:, v0:v0 + MLA_V].astype(BF16)
        v_out[:, hd * MLA_VW + MLA_V:(hd + 1) * MLA_VW] = ones_col
        k_out[:, c0:c0 + MLA_NOPE] = kn[hd].astype(BF16)
        k_out[:, c0 + MLA_NOPE:c0 + MLA_QK_PAD] = kr


def _pad_lanes(g):
    return jnp.pad(g, (0, LANES - g.shape[0])).reshape(1, LANES)


def mla_in(lay, xs, gain, mod, w_in, q_norm, kv_norm, w_uq, w_ukv, qn_nope, qn_rope, kn_nope, kn_rope,
           tables):
    D = D_MODEL
    Hn = MLA_HEADS
    win = jnp.pad(w_in, ((0, 0), (0, MLA_IN_PAD - w_in.shape[1]))).astype(BF16)
    wuq = jnp.pad(w_uq.reshape(MLA_Q_LORA, Hn, MLA_NOPE + MLA_ROPE),
                  ((0, 0), (0, 0), (0, MLA_QK_PAD - MLA_NOPE - MLA_ROPE)))
    wuq = wuq.reshape(MLA_Q_LORA, Hn * MLA_QK_PAD).astype(BF16)
    wkv = w_ukv.reshape(MLA_KV_LORA, Hn, MLA_NOPE + MLA_V)
    wukv = jnp.concatenate([wkv[:, :, :MLA_NOPE].reshape(MLA_KV_LORA, Hn * MLA_NOPE),
                            wkv[:, :, MLA_NOPE:].reshape(MLA_KV_LORA, Hn * MLA_V)], axis=1).astype(BF16)
    nb = lay.nb
    x_specs, x_shapes = _row_outputs(lay, len(xs))
    tab_spec = pl.BlockSpec((TL, LANES), lambda r: (r % nb, 0))
    bf = lambda w: jax.ShapeDtypeStruct((lay.rows, w), BF16)
    return pl.pallas_call(
        functools.partial(_mla_in_kernel, nb=nb, n_src=len(xs)),
        grid=(lay.n_blocks,),
        in_specs=lay.token_specs(D, len(xs)) + [
                  _const_spec((1, D)), lay.mod_spec(1), lay.mod_spec(0),
                  _const_spec(win.shape), _const_spec(wuq.shape), _const_spec(wukv.shape),
                  _const_spec((1, MLA_Q_LORA)), _const_spec((1, MLA_KV_LORA)),
                  _const_spec((1, LANES)), _const_spec((1, LANES)), _const_spec((1, LANES)),
                  _const_spec((1, LANES)), tab_spec, tab_spec, tab_spec],
        out_specs=[lay.row_spec(Hn * MLA_QK_PAD), lay.row_spec(Hn * MLA_QK_PAD),
                   lay.row_spec(Hn * MLA_VW)] + x_specs,
        out_shape=[bf(Hn * MLA_QK_PAD), bf(Hn * MLA_QK_PAD), bf(Hn * MLA_VW)] + x_shapes,
        compiler_params=_params(1),
        name="mla_in",
    )(*xs, gain, mod, mod, win, wuq, wukv, q_norm.reshape(1, -1), kv_norm.reshape(1, -1),
      (qn_nope * (MLA_SCALE * LOG2E)).reshape(1, -1), _pad_lanes(qn_rope * (MLA_SCALE * LOG2E)),
      kn_nope.reshape(1, -1), _pad_lanes(kn_rope), *tables)


def rope_tables(n_ctx, n_lat):
    n_freq = MLA_ROPE // 4
    inv = ROPE_THETA ** (-jnp.arange(n_freq, dtype=F32) / n_freq)
    t = jnp.arange(n_lat)
    a_r = (t // GRID_W).astype(F32)[:, None] * inv
    a_c = (t % GRID_W).astype(F32)[:, None] * inv
    ang = jnp.concatenate([a_r, a_r, a_c, a_c], axis=-1)
    ang = jnp.concatenate([jnp.zeros((n_ctx, MLA_ROPE), F32), ang], axis=0)
    cos, sin = jnp.cos(ang), jnp.sin(ang)
    low = (jnp.arange(MLA_ROPE) % (2 * ROPE_HALF)) < ROPE_HALF
    pad = lambda a: jnp.pad(a, ((0, 0), (0, LANES - MLA_ROPE)))
    return pad(cos), pad(jnp.where(low, -sin, 0.0)), pad(jnp.where(low, 0.0, sin))


MLA_HPS = 4


def _mla_attn_kernel(q_ref, k_ref, v_ref, o_ref, *, n_ctx):
    def attend(n_keys):
        heads = range(MLA_HPS)
        s, p = {}, {}
        for hd in heads:
            q = q_ref[0, :, hd * MLA_QK_PAD:(hd + 1) * MLA_QK_PAD]
            k = k_ref[0, :n_keys, hd * MLA_QK_PAD:(hd + 1) * MLA_QK_PAD]
            s[hd] = lax.dot_general(q, k, (((1,), (1,)), ((), ())), preferred_element_type=F32)
        for hd in heads:
            p[hd] = jnp.exp2(s[hd] - s[hd].max(axis=1, keepdims=True)).astype(BF16)
        for hd in heads:
            v = v_ref[0, :n_keys, hd * MLA_VW:(hd + 1) * MLA_VW]
            r = jnp.dot(p[hd], v, preferred_element_type=F32)
            o_ref[0, :, hd * MLA_V:(hd + 1) * MLA_V] = (
                r[:, :MLA_V] / r[:, MLA_V:MLA_V + 1]).astype(o_ref.dtype)

    @pl.when(pl.program_id(2) == 0)
    def _():
        attend(n_ctx)

    @pl.when(pl.program_id(2) > 0)
    def _():
        attend(k_ref.shape[1])


def mla_attention(lay, q, k, v):
    assert lay.ctx_first
    Bn, S = lay.n_batch, lay.nb * TL
    q3, k3, v3 = (a.reshape(Bn, S, a.shape[-1]) for a in (q, k, v))
    out = pl.pallas_call(
        functools.partial(_mla_attn_kernel, n_ctx=TL),
        grid=(Bn, MLA_HEADS // MLA_HPS, lay.nb),
        in_specs=[pl.BlockSpec((1, TL, MLA_HPS * MLA_QK_PAD), lambda b, h, i: (b, i, h)),
                  pl.BlockSpec((1, S, MLA_HPS * MLA_QK_PAD), lambda b, h, i: (b, 0, h),
                               pipeline_mode=pl.Buffered(1)),
                  pl.BlockSpec((1, S, MLA_HPS * MLA_VW), lambda b, h, i: (b, 0, h),
                               pipeline_mode=pl.Buffered(1))],
        out_specs=pl.BlockSpec((1, TL, MLA_HPS * MLA_V), lambda b, h, i: (b, i, h)),
        out_shape=jax.ShapeDtypeStruct((Bn, S, MLA_HEADS * MLA_V), BF16),
        compiler_params=_params(3),
        name="mla_attention",
    )(q3, k3, v3)
    return out.reshape(lay.rows, MLA_HEADS * MLA_V)


N_PROLOGUE = {"conv": 5, "mlstm": 4, "mla": 1}


def _mixer_out_kernel(*refs, kind, nb, ctx_first, n_src):
    n_pro = N_PROLOGUE[kind]
    pro = refs[:n_pro]
    wout_ref = refs[n_pro]
    x_srcs = refs[n_pro + 1:n_pro + 1 + n_src]
    (ga_ref, gain_ref, sc_ref, sh_ref, wr_ref, br_ref,
     xo_ref, h2_ref, te_ref, gate_ref, rank_ref, cnt_ref, carry_ref) = refs[n_pro + 1 + n_src:]
    r = pl.program_id(0)

    if kind == "conv":
        vprev_ref, v_ref, vnext_ref, bg_ref, cw_ref = pro
        j = r % nb
        first = (j == 0) | (j == 1) if ctx_first else (j == 0)
        last = (j == nb - 1) | (j == 0) if ctx_first else (j == nb - 1)
        v = v_ref[...].astype(F32)
        rows = lax.broadcasted_iota(jnp.int32, (TL, 1), 0)
        prev_row = jnp.where(first, 0.0, vprev_ref[BF16_ROWS - 1:BF16_ROWS, :].astype(F32))
        next_row = jnp.where(last, 0.0, vnext_ref[0:1, :].astype(F32))
        up = jnp.where(rows == 0, prev_row, pltpu.roll(v, 1, 0))
        dn = jnp.where(rows == TL - 1, next_row, pltpu.roll(v, TL - 1, 0))
        cw = cw_ref[...]
        a = bg_ref[...].astype(F32) * (up * cw[0:1] + v * cw[1:2] + dn * cw[2:3])
    elif kind == "mlstm":
        hf_ref, hb_ref, og_ref, ng_ref = pro
        hh = hf_ref[...].astype(F32) + hb_ref[...].astype(F32)
        a = jnp.concatenate([_rms(hh[:, h * ML_DV:(h + 1) * ML_DV]) for h in range(ML_HEADS)], axis=1)
        a = a * ng_ref[...] * og_ref[...].astype(F32)
    else:
        a = pro[0][...]

    y = jnp.dot(a.astype(BF16), wout_ref[...], preferred_element_type=F32)
    xn = _token_block(x_srcs, r, nb) + ga_ref[0] * y
    xo_ref[...] = xn
    h2 = _norm_mod(xn, gain_ref[...], sc_ref[0], sh_ref[0])
    h2_ref[...] = _pack_rows(h2)
    lt = jnp.dot(h2.astype(BF16), wr_ref[...], preferred_element_type=F32).T
    logits = lt[:N_EXPERTS] + lt[N_EXPERTS:2 * N_EXPERTS] + br_ref[...]

    sub = lax.broadcasted_iota(jnp.int32, (N_EXPERTS, TL), 0)
    sub_k = lax.broadcasted_iota(jnp.int32, (TOP_K, TL), 0)
    work = logits
    sel = jnp.zeros((N_EXPERTS, TL), F32)
    top_e = jnp.zeros((TOP_K, TL), jnp.int32)
    top_v = jnp.zeros((TOP_K, TL), F32)
    picks = []
    for kk in range(TOP_K):
        m = work.max(axis=0, keepdims=True)
        idx = jnp.min(jnp.where(work == m, sub, N_EXPERTS), axis=0, keepdims=True)
        hit = sub == idx
        picks.append(hit)
        sel = jnp.where(hit, 1.0, sel)
        work = jnp.where(hit, -jnp.inf, work)
        top_e = jnp.where(sub_k == kk, idx, top_e)
        top_v = jnp.where(sub_k == kk, m, top_v)
    ex = jnp.exp(top_v - top_v[0:1])
    gate_ref[...] = ex / ex.sum(axis=0, keepdims=True)
    te_ref[...] = top_e

    @pl.when(r == 0)
    def _():
        carry_ref[...] = jnp.zeros_like(carry_ref)

    tr = lax.broadcasted_iota(jnp.int32, (TL, TL), 0)
    tc = lax.broadcasted_iota(jnp.int32, (TL, TL), 1)
    before = jnp.dot(sel.astype(BF16), (tr < tc).astype(BF16), preferred_element_type=F32)
    pos = before + carry_ref[...]
    rank = jnp.zeros((TOP_K, TL), F32)
    for kk in range(TOP_K):
        rk = jnp.sum(jnp.where(picks[kk], pos, 0.0), axis=0, keepdims=True)
        rank = jnp.where(sub_k == kk, rk, rank)
    rank_ref[...] = rank.astype(jnp.int32)
    total = carry_ref[...] + jnp.sum(sel, axis=1, keepdims=True)
    carry_ref[...] = total
    cnt_ref[...] = total


def mixer_out(lay, kind, pro_args, w_out, xs, mod, gain_f, w_r, b_r):
    D = D_MODEL
    nb = lay.nb
    if kind == "conv":
        v, bg, cw = pro_args
        per = TL // BF16_ROWS
        last_tile = lay.rows // BF16_ROWS - 1
        pro_specs = [pl.BlockSpec((BF16_ROWS, D), lambda r: (jnp.maximum(r * per - 1, 0), 0)),
                     lay.row_spec(D),
                     pl.BlockSpec((BF16_ROWS, D), lambda r: (jnp.minimum((r + 1) * per, last_tile), 0)),
                     lay.row_spec(D), _const_spec((CONV_WIDTH, D))]
        pro_in = [v, v, v, bg, cw]
    elif kind == "mlstm":
        h_f, h_b, og, ng = pro_args
        pro_specs = [lay.row_spec(ML_V), lay.row_spec(ML_V), lay.row_spec(ML_V), _const_spec((1, ML_V))]
        pro_in = [h_f, h_b, og, ng.reshape(1, ML_V)]
    else:
        pro_specs = [lay.row_spec(D)]
        pro_in = list(pro_args)
    k_in = w_out.shape[0]
    small = lambda dt: jax.ShapeDtypeStruct((TOP_K, lay.rows), dt)
    small_spec = pl.BlockSpec((TOP_K, TL), lambda r: (0, r))
    return pl.pallas_call(
        functools.partial(_mixer_out_kernel, kind=kind, nb=nb, ctx_first=lay.ctx_first, n_src=len(xs)),
        grid=(lay.n_blocks,),
        in_specs=pro_specs + [_const_spec((k_in, D))] + lay.token_specs(D, len(xs)) + [
            lay.mod_spec(2), _const_spec((1, D)), lay.mod_spec(4), lay.mod_spec(3),
            _const_spec((D, LANES)), _const_spec((N_EXPERTS, 1))],
        out_specs=[lay.row_spec(D), lay.row_spec(PACK_W), small_spec, small_spec, small_spec,
                   _const_spec((N_EXPERTS, 1))],
        out_shape=[jax.ShapeDtypeStruct((lay.rows, D), F32),
                   jax.ShapeDtypeStruct((lay.rows, PACK_W), jnp.int32),
                   small(jnp.int32), small(F32), small(jnp.int32),
                   jax.ShapeDtypeStruct((N_EXPERTS, 1), F32)],
        scratch_shapes=[pltpu.VMEM((N_EXPERTS, 1), F32)],
        compiler_params=_params(1),
        name="mixer_out_" + kind,
    )(*pro_in, w_out.astype(BF16), *xs, mod, gain_f, mod, mod, _split_weight_lanes(w_r),
      b_r.reshape(N_EXPERTS, 1))


def _expert_ffn_kernel(blk_e_ref, first_ref, valid_ref, x_ref, w1_ref, b1_ref, w2_ref, b2_ref, o_ref,
                       w1b_ref, w2b_ref):
    del blk_e_ref
    i = pl.program_id(0)

    @pl.when(first_ref[i] == 1)
    def _():
        w1b_ref[...] = w1_ref[0, 0].astype(BF16)
        w2b_ref[...] = w2_ref[0, 0].astype(BF16)

    @pl.when(valid_ref[i] > 0)
    def _():
        x = _unpack_rows(x_ref[...], BF16)
        h = jnp.dot(x, w1b_ref[...], preferred_element_type=F32) + b1_ref[0, 0]
        glu = jnp.minimum(h[:, :MOE_FF], SWIGLU_LIMIT)
        lin = jnp.clip(h[:, MOE_FF:], -SWIGLU_LIMIT, SWIGLU_LIMIT)
        act = glu * jax.nn.sigmoid(SWIGLU_ALPHA * glu) * (lin + 1.0)
        y = jnp.dot(act.astype(BF16), w2b_ref[...], preferred_element_type=F32)
        o_ref[...] = _pack_rows(y + b2_ref[0, 0])

    @pl.when(valid_ref[i] == 0)
    def _():
        o_ref[...] = jnp.zeros_like(o_ref)


def expert_ffn(layer, xp, blk_e, blk_first, blk_valid, w1, b1, w2, b2):
    n_rows = xp.shape[0]
    D, F2 = D_MODEL, 2 * MOE_FF
    n_blk = n_rows // MOE_BLOCK
    grid_spec = pltpu.PrefetchScalarGridSpec(
        num_scalar_prefetch=3,
        grid=(n_blk,),
        in_specs=[
            pl.BlockSpec((MOE_BLOCK, PACK_W), lambda i, be, fi, nu: (i, 0)),
            pl.BlockSpec((1, 1, D, F2), lambda i, be, fi, nu: (layer, be[i], 0, 0)),
            pl.BlockSpec((1, 1, 1, F2), lambda i, be, fi, nu: (layer, be[i], 0, 0)),
            pl.BlockSpec((1, 1, MOE_FF, D), lambda i, be, fi, nu: (layer, be[i], 0, 0)),
            pl.BlockSpec((1, 1, 1, D), lambda i, be, fi, nu: (layer, be[i], 0, 0)),
        ],
        out_specs=pl.BlockSpec((MOE_BLOCK, PACK_W), lambda i, be, fi, nu: (i, 0)),
        scratch_shapes=[pltpu.VMEM((D, F2), BF16), pltpu.VMEM((MOE_FF, D), BF16)],
    )
    return pl.pallas_call(
        _expert_ffn_kernel,
        grid_spec=grid_spec,
        out_shape=jax.ShapeDtypeStruct((n_rows, PACK_W), jnp.int32),
        compiler_params=_params(1),
        name="expert_ffn",
    )(blk_e, blk_first, blk_valid, xp, w1, b1, w2, b2)


SC_CORES = 2
SC_SUBCORES = 16
SC_CHUNKS = (64, 32)


def _sc_chunk(*counts):
    n_workers = SC_CORES * SC_SUBCORES
    for chunk in SC_CHUNKS:
        if all(n % (chunk * n_workers) == 0 for n in counts):
            return chunk
    raise ValueError(f"row counts {counts} do not split over {n_workers} subcores")


def sc_gather(table, idx):
    n_idx = idx.shape[0]
    width = table.shape[1]
    n_workers = SC_CORES * SC_SUBCORES
    per_worker = n_idx // n_workers
    chunk = _sc_chunk(n_idx)
    n_chunks = per_worker // chunk
    assert n_chunks * chunk * n_workers == n_idx and n_chunks % 2 == 0
    mesh = plsc.VectorSubcoreMesh(core_axis_name="c", subcore_axis_name="s",
                                  num_cores=SC_CORES, num_subcores=SC_SUBCORES)

    def body(table_hbm, idx_hbm, out_hbm, idx_v, rows_v, gsem, wsem):
        wid = lax.axis_index("s") * SC_CORES + lax.axis_index("c")
        pltpu.sync_copy(idx_hbm.at[wid], idx_v)

        def gather(ci, slot):
            return pltpu.make_async_copy(table_hbm.at[idx_v.at[ci]], rows_v.at[slot], gsem.at[slot])

        def write(ci, slot):
            return pltpu.make_async_copy(rows_v.at[slot], out_hbm.at[ci, wid], wsem.at[slot])

        gather(0, 0).start()

        @pl.loop(0, n_chunks, step=2)
        def _(c0):
            for slot in range(2):
                ci = c0 + slot
                other = 1 - slot

                @pl.when(ci + 1 < n_chunks)
                def _():
                    @pl.when(ci >= 1)
                    def _():
                        write(ci - 1, other).wait()
                    gather(ci + 1, other).start()

                gather(ci, slot).wait()
                write(ci, slot).start()

        write(n_chunks - 2, 0).wait()
        write(n_chunks - 1, 1).wait()

    out = pl.kernel(
        body,
        out_type=jax.ShapeDtypeStruct((n_chunks, n_workers, chunk, width), table.dtype),
        mesh=mesh,
        scratch_types=[pltpu.VMEM((n_chunks, chunk), jnp.int32),
                       pltpu.VMEM((2, chunk, width), table.dtype),
                       pltpu.SemaphoreType.DMA((2,)),
                       pltpu.SemaphoreType.DMA((2,))],
        name="sc_gather",
    )(table, idx.reshape(n_chunks, n_workers, chunk).transpose(1, 0, 2))
    return out.reshape(n_idx, width)


def sc_dispatch(table, dest, pad_rows):
    n_tok, width = table.shape
    n_picks = dest.shape[0]
    n_pad = pad_rows.shape[0]
    n_workers = SC_CORES * SC_SUBCORES
    chunk = _sc_chunk(n_tok, n_pad)
    per_w = n_tok // chunk // n_workers
    pad_w = n_pad // chunk // n_workers
    assert per_w * chunk * n_workers == n_tok and pad_w * chunk * n_workers == n_pad
    mesh = plsc.VectorSubcoreMesh(core_axis_name="c", subcore_axis_name="s",
                                  num_cores=SC_CORES, num_subcores=SC_SUBCORES)

    def body(table_hbm, idx_hbm, pad_hbm, zero_hbm, out_hbm, idx_v, pad_v, rows_v, zero_v,
             rsem, ssem, zsem):
        wid = lax.axis_index("s") * SC_CORES + lax.axis_index("c")
        pltpu.sync_copy(idx_hbm.at[wid], idx_v)
        pltpu.sync_copy(pad_hbm.at[wid], pad_v)
        pltpu.sync_copy(zero_hbm, zero_v)

        def zero_fill(pc):
            return pltpu.make_async_copy(zero_v, out_hbm.at[pad_v.at[pc]], zsem)

        for pc in range(pad_w):
            zero_fill(pc).start()

        def scatter(ci, kk):
            return pltpu.make_async_copy(rows_v, out_hbm.at[idx_v.at[ci * n_picks + kk]], ssem)

        @pl.loop(0, per_w)
        def _(ci):
            pltpu.async_copy(table_hbm.at[ci, wid], rows_v, rsem).wait()
            for kk in range(n_picks):
                scatter(ci, kk).start()
            for kk in range(n_picks):
                scatter(ci, kk).wait()

        for pc in range(pad_w):
            zero_fill(pc).wait()

    idx = dest.reshape(n_picks, per_w, n_workers, chunk).transpose(2, 1, 0, 3)
    idx = idx.reshape(n_workers, per_w * n_picks, chunk)
    return pl.kernel(
        body,
        out_type=jax.ShapeDtypeStruct((n_tok * n_picks + n_pad, width), table.dtype),
        mesh=mesh,
        scratch_types=[pltpu.VMEM((per_w * n_picks, chunk), jnp.int32),
                       pltpu.VMEM((pad_w, chunk), jnp.int32),
                       pltpu.VMEM((chunk, width), table.dtype),
                       pltpu.VMEM((chunk, width), table.dtype),
                       pltpu.SemaphoreType.DMA, pltpu.SemaphoreType.DMA, pltpu.SemaphoreType.DMA],
        name="sc_dispatch",
    )(table.reshape(per_w, n_workers, chunk, width), idx,
      pad_rows.reshape(n_workers, pad_w, chunk), jnp.zeros((chunk, width), table.dtype))


def _combine_kernel(x_ref, *refs):
    y_refs, (gate_ref, gf_ref, o_ref) = refs[:TOP_K], refs[TOP_K:]
    gates = gate_ref[...]
    acc = gates[:, 0:1] * _unpack_rows(y_refs[0][...], F32)
    for kk in range(1, TOP_K):
        acc = acc + gates[:, kk:kk + 1] * _unpack_rows(y_refs[kk][...], F32)
    o_ref[...] = x_ref[...] + gf_ref[0] * acc


def moe_combine(lay, x, yg, gates, mod, drop_ctx):
    D = D_MODEL
    if drop_ctx:
        nbo = lay.nb - 1
        src = lambda r: (r // nbo) * lay.nb + 1 + r % nbo
        n_out = lay.n_batch * nbo
    else:
        src = lambda r: r
        n_out = lay.n_blocks
    y_specs = [pl.BlockSpec((TL, PACK_W), functools.partial(lambda kk, r: (kk * lay.n_blocks + src(r), 0), kk))
               for kk in range(TOP_K)]
    return pl.pallas_call(
        _combine_kernel,
        grid=(n_out,),
        in_specs=[pl.BlockSpec((TL, D), lambda r: (src(r), 0))] + y_specs + [
            pl.BlockSpec((TL, TOP_K), lambda r: (src(r), 0)),
            pl.BlockSpec((1, 1, D), lambda r: (lay.mod_row(src(r)), 0, 5))],
        out_specs=pl.BlockSpec((TL, D), lambda r: (r, 0)),
        out_shape=jax.ShapeDtypeStruct((n_out * TL, D), F32),
        compiler_params=_params(1),
        name="moe_combine",
    )(x, yg, yg, yg, yg, gates, mod)


def moe_route(top_e, rank, counts):
    T = top_e.shape[1]
    assert (T * TOP_K) % MOE_BLOCK == 0
    counts = counts.reshape(N_EXPERTS).astype(jnp.int32)
    padded = (counts + MOE_BLOCK - 1) // MOE_BLOCK * MOE_BLOCK
    padded_end = jnp.cumsum(padded)
    padded_start = padded_end - padded
    experts = jnp.arange(N_EXPERTS)
    start_of = jnp.sum(jnp.where(top_e[..., None] == experts, padded_start, 0), axis=-1)
    dest = (start_of + rank).astype(jnp.int32)
    n_pad = N_EXPERTS * MOE_BLOCK
    n_rows = T * TOP_K + n_pad
    n_blk = n_rows // MOE_BLOCK
    blk_start = jnp.arange(n_blk) * MOE_BLOCK
    blk_e = jnp.minimum(jnp.sum(padded_end[None, :] <= blk_start[:, None], axis=1), N_EXPERTS - 1)
    blk_e = blk_e.astype(jnp.int32)
    blk_first = jnp.concatenate([jnp.ones((1,), jnp.int32), (blk_e[1:] != blk_e[:-1]).astype(jnp.int32)])
    blk_hot = blk_e[:, None] == experts
    in_grp = blk_start - jnp.sum(jnp.where(blk_hot, padded_start, 0), axis=-1)
    blk_valid = jnp.clip(jnp.sum(jnp.where(blk_hot, counts, 0), axis=-1) - in_grp, 0, MOE_BLOCK)
    blk_valid = jnp.where(blk_start < padded_end[-1], blk_valid, 0).astype(jnp.int32)
    tail = padded - counts
    tail_end = jnp.cumsum(tail)
    j = jnp.arange(n_pad)
    owner = jnp.sum(tail_end[None, :] <= j[:, None], axis=1)
    base = padded_start + counts - (tail_end - tail)
    in_group = jnp.sum(jnp.where(owner[:, None] == experts, base, 0), axis=-1) + j
    pad_rows = jnp.where(j < tail_end[-1], in_group, padded_end[-1] + j - tail_end[-1])
    return dest, pad_rows.astype(jnp.int32), blk_e, blk_first, blk_valid


def kernel(x, c, ctx, c_ctx, norm_mix, norm_ffn, w_mod, b_mod, conv_w_in, conv_w, conv_w_out, ml_w_in, ml_b_gate, ml_norm, ml_w_out, mla_w_in, mla_q_norm, mla_kv_norm, mla_w_uq, mla_w_ukv, mla_qn_nope, mla_qn_rope, mla_kn_nope, mla_kn_rope, mla_w_out, moe_w_router, moe_b_router, moe_w1, moe_b1, moe_w2, moe_b2):
    Bn, n_lat, D = x.shape
    n_ctx = ctx.shape[1]
    assert D == D_MODEL and n_ctx == TL and n_lat % TL == 0
    assert (DEPTH - 1) % N_MIXERS == 0
    full = Layout(Bn, (n_ctx + n_lat) // TL, True)
    lat_only = Layout(Bn, n_lat // TL, False)
    mods = ada_all(c, c_ctx, w_mod, b_mod)
    tables = rope_tables(n_ctx, n_lat)
    b1_all = moe_b1.reshape(DEPTH, N_EXPERTS, 1, 2 * MOE_FF)
    b2_all = moe_b2.reshape(DEPTH, N_EXPERTS, 1, D)
    Xs = (ctx.reshape(Bn * n_ctx, D), x.reshape(Bn * n_lat, D))
    for layer in range(DEPTH):
        kind, j = layer % N_MIXERS, layer // N_MIXERS
        lay = lat_only if layer == DEPTH - 1 else full
        mod = mods[layer]
        gain_a = norm_mix[layer].reshape(1, D)
        gain_f = norm_ffn[layer].reshape(1, D)
        if kind == 0:
            bg, v = conv_in(lay, Xs, gain_a, mod, conv_w_in[j].astype(BF16))
            pro, w_out, name = (v, bg, conv_w[j]), conv_w_out[j], "conv"
        elif kind == 1:
            w = ml_w_in[j]
            n_main = 2 * ML_QK + 2 * ML_V
            w_main = jnp.concatenate([w[:, :ML_QK] * ML_DQK ** -0.5, w[:, ML_QK:n_main]], axis=1)
            outs = mlstm_in(lay, Xs, gain_a, mod, w_main.astype(BF16), w[:, n_main:], ml_b_gate[j])
            q, k, kt, v, og, g, gt = outs[:7]
            Xs = tuple(outs[7:]) or Xs
            h_f, h_b = mlstm_scan(lay, q, k, kt, v, g, gt)
            pro, w_out, name = (h_f, h_b, og, ml_norm[j]), ml_w_out[j], "mlstm"
        else:
            outs = mla_in(lay, Xs, gain_a, mod, mla_w_in[j], mla_q_norm[j], mla_kv_norm[j],
                          mla_w_uq[j], mla_w_ukv[j], mla_qn_nope[j], mla_qn_rope[j],
                          mla_kn_nope[j], mla_kn_rope[j], tables)
            q, k, v = outs[:3]
            Xs = tuple(outs[3:]) or Xs
            pro, w_out, name = (mla_attention(lay, q, k, v),), mla_w_out[j], "mla"
        X, h2, top_e, gates, rank, counts = mixer_out(
            lay, name, pro, w_out, Xs, mod, gain_f, moe_w_router[layer], moe_b_router[layer])
        dest, pad_rows, blk_e, blk_first, blk_valid = moe_route(top_e, rank, counts)
        xp = sc_dispatch(h2, dest, pad_rows)
        yp = expert_ffn(layer, xp, blk_e, blk_first, blk_valid, moe_w1, b1_all, moe_w2, b2_all)
        yg = sc_gather(yp, dest.reshape(-1))
        if layer + 1 < DEPTH and (layer + 1) % N_MIXERS != 0:
            Xs = (X,) + (yg,) * TOP_K + (gates.T, mod)
        else:
            Xs = (moe_combine(lay, X, yg, gates.T, mod, drop_ctx=(layer == DEPTH - 2)),)
    return Xs[0].reshape(Bn, n_lat, D)
```

```python
import functools

import jax
import jax.numpy as jnp
from jax import lax
from jax.experimental import pallas as pl
from jax.experimental.pallas import tpu as pltpu
from jax.experimental.pallas import tpu_sc as plsc

D_MODEL = 1024
DEPTH = 4
GRID_W = 64
N_MIXERS = 3
N_ADA = 6
RMS_EPS = 1e-6
CONV_WIDTH = 3
ML_HEADS = 8
ML_DQK = 64
ML_DV = 128
ML_QK = ML_HEADS * ML_DQK
ML_V = ML_HEADS * ML_DV
GATE_CAP = 15.0
MLA_HEADS = 8
MLA_NOPE = 128
MLA_ROPE = 64
MLA_V = 128
MLA_Q_LORA = 384
MLA_KV_LORA = 256
MLA_SCALE = (MLA_NOPE + MLA_ROPE) ** -0.5
ROPE_THETA = 10000.0
N_EXPERTS = 32
TOP_K = 4
MOE_FF = D_MODEL
SWIGLU_ALPHA = 1.702
SWIGLU_LIMIT = 7.0
MOE_BLOCK = 512

TL = 256
LANES = 128
BF16_ROWS = 16
VMEM_LIMIT = 48 * 1024 * 1024
HI = lax.Precision.HIGHEST
F32 = jnp.float32
BF16 = jnp.bfloat16


def _params(n_axes):
    return pltpu.CompilerParams(dimension_semantics=("arbitrary",) * n_axes,
                                vmem_limit_bytes=VMEM_LIMIT)


def _rms(x, width=None):
    width = x.shape[-1] if width is None else width
    return x * lax.rsqrt(jnp.sum(x * x, axis=-1, keepdims=True) * (1.0 / width) + RMS_EPS)


def _norm_mod(x, gain, scale, shift):
    return _rms(x) * (gain * (1.0 + scale)) + shift


def _split_weight_lanes(w):
    hi = w.astype(BF16)
    lo = (w - hi.astype(F32)).astype(BF16)
    return jnp.pad(jnp.concatenate([hi, lo], axis=1), ((0, 0), (0, LANES - 2 * w.shape[1])))


PACK_W = D_MODEL // 2
HIGH_HALF = -65536


def _pack_rows(x):
    xb = x.astype(BF16).astype(F32)
    lo = lax.bitcast_convert_type(xb[:, :PACK_W], jnp.int32)
    hi = lax.bitcast_convert_type(xb[:, PACK_W:], jnp.int32)
    return hi | lax.shift_right_logical(lo, 16)


def _unpack_rows(w, dtype):
    lo = lax.bitcast_convert_type(lax.shift_left(w, 16), F32)
    hi = lax.bitcast_convert_type(w & HIGH_HALF, F32)
    return jnp.concatenate([lo.astype(dtype), hi.astype(dtype)], axis=1)


class Layout:
    def __init__(self, n_batch, nb, ctx_first):
        self.n_batch, self.nb, self.ctx_first = n_batch, nb, ctx_first
        self.n_blocks = n_batch * nb
        self.rows = self.n_blocks * TL

    def mod_row(self, r):
        b = r // self.nb
        return jnp.where(r % self.nb == 0, self.n_batch, b) if self.ctx_first else b

    def row_spec(self, width):
        return pl.BlockSpec((TL, width), lambda r: (r, 0))

    def mod_spec(self, piece):
        return pl.BlockSpec((1, 1, D_MODEL), lambda r: (self.mod_row(r), 0, piece))

    def token_specs(self, width, n_src, src=None):
        if n_src == 1:
            return [self.row_spec(width)]
        if n_src == N_COMBINE_SRC:
            src = src or self
            at = (lambda r: r) if src.nb == self.nb else (lambda r: (r // self.nb) * src.nb + 1 + r % self.nb)
            picks = [pl.BlockSpec((TL, PACK_W), functools.partial(lambda kk, r: (kk * src.n_blocks + at(r), 0), kk))
                     for kk in range(TOP_K)]
            return ([pl.BlockSpec((TL, width), lambda r: (at(r), 0))] + picks
                    + [pl.BlockSpec((TL, TOP_K), lambda r: (at(r), 0)),
                       pl.BlockSpec((1, 1, D_MODEL), lambda r: (src.mod_row(at(r)), 0, 5))])
        nb = self.nb
        return [pl.BlockSpec((TL, width), lambda r: (r // nb, 0)),
                pl.BlockSpec((TL, width), lambda r: ((r // nb) * (nb - 1) + jnp.maximum(r % nb - 1, 0), 0))]


N_COMBINE_SRC = TOP_K + 3


def _token_block(srcs, r, nb):
    if len(srcs) == 1:
        return srcs[0][...]
    if len(srcs) == N_COMBINE_SRC:
        gates = srcs[TOP_K + 1][...]
        acc = gates[:, 0:1] * _unpack_rows(srcs[1][...], F32)
        for kk in range(1, TOP_K):
            acc = acc + gates[:, kk:kk + 1] * _unpack_rows(srcs[1 + kk][...], F32)
        return srcs[0][...] + srcs[TOP_K + 2][0] * acc
    return jnp.where(r % nb == 0, srcs[0][...], srcs[1][...])


def _const_spec(shape):
    return pl.BlockSpec(shape, lambda *_: (0,) * len(shape))


ADA_ROWS = 16
ADA_TN = 1536


def _ada_kernel(c_ref, w_ref, b_ref, o_ref):
    c = c_ref[...]
    s = c * jax.nn.sigmoid(c)
    o_ref[0] = jnp.dot(s, w_ref[0], precision=HI, preferred_element_type=F32) + b_ref[0]


def ada_all(c, c_ctx, w_mod, b_mod):
    Bn, D = c.shape
    assert Bn + 1 <= ADA_ROWS
    cond = jnp.zeros((ADA_ROWS, D), F32).at[:Bn].set(c).at[Bn].set(c_ctx)
    out = pl.pallas_call(
        _ada_kernel,
        grid=(DEPTH, N_ADA * D // ADA_TN),
        in_specs=[pl.BlockSpec((ADA_ROWS, D), lambda l, n: (0, 0)),
                  pl.BlockSpec((1, D, ADA_TN), lambda l, n: (l, 0, n)),
                  pl.BlockSpec((1, 1, ADA_TN), lambda l, n: (l, 0, n))],
        out_specs=pl.BlockSpec((1, ADA_ROWS, ADA_TN), lambda l, n: (l, 0, n)),
        out_shape=jax.ShapeDtypeStruct((DEPTH, ADA_ROWS, N_ADA * D), F32),
        compiler_params=_params(2),
        name="ada_mod",
    )(cond, w_mod, b_mod.reshape(DEPTH, 1, N_ADA * D))
    return out[:, :Bn + 1, None, :]


def _conv_in_kernel(*refs, nb, n_src):
    srcs, (gain_ref, sc_ref, sh_ref, w_ref, bg_ref, v_ref) = refs[:n_src], refs[n_src:n_src + 6]
    D = D_MODEL
    x = _token_block(srcs, pl.program_id(0), nb)
    if n_src == N_COMBINE_SRC:
        refs[n_src + 6][...] = x
    h = _norm_mod(x, gain_ref[...], sc_ref[0], sh_ref[0]).astype(BF16)
    p = jnp.dot(h, w_ref[...], preferred_element_type=F32)
    bg_ref[...] = p[:, :D].astype(BF16)
    v_ref[...] = (p[:, D:2 * D] * p[:, 2 * D:]).astype(BF16)


def conv_in(lay, xs, gain, mod, w_in, src=None):
    D = D_MODEL
    sds = jax.ShapeDtypeStruct((lay.rows, D), BF16)
    x_specs, x_shapes = _row_outputs(lay, len(xs))
    return pl.pallas_call(
        functools.partial(_conv_in_kernel, nb=lay.nb, n_src=len(xs)),
        grid=(lay.n_blocks,),
        in_specs=lay.token_specs(D, len(xs), src) + [_const_spec((1, D)), lay.mod_spec(1), lay.mod_spec(0),
                                                     _const_spec((D, 3 * D))],
        out_specs=[lay.row_spec(D), lay.row_spec(D)] + x_specs,
        out_shape=[sds, sds] + x_shapes,
        compiler_params=_params(1),
        name="conv_in",
    )(*xs, gain, mod, mod, w_in)


ML_T = 256
ML_SW = 2 * ML_DV
ML_NG = 4 * ML_HEADS


LOG2E = 1.4426950408889634


def _gate_act(g, is_forget):
    g = GATE_CAP * jnp.tanh(g * (1.0 / GATE_CAP))
    log_sig = jnp.minimum(g, 0.0) - jnp.log(1.0 + jnp.exp(-jnp.abs(g)))
    return jnp.where(is_forget, log_sig, g) * LOG2E


def _mlstm_in_kernel(*refs, nb, n_src):
    srcs, (gain_ref, sc_ref, sh_ref, w_ref, wg_ref, bg_ref), outs = refs[:n_src], refs[n_src:n_src + 6], refs[n_src + 6:]
    q_ref, k_ref, kt_ref, v_ref, og_ref, g_ref, gt_ref = outs[:7]
    x = _token_block(srcs, pl.program_id(0), nb)
    if n_src == N_COMBINE_SRC:
        outs[7][...] = x
    h = _norm_mod(x, gain_ref[...], sc_ref[0], sh_ref[0])
    hb = h.astype(BF16)
    p = jnp.dot(hb, w_ref[...], preferred_element_type=F32)
    q_ref[...] = p[:, :ML_QK].astype(BF16)
    k_ref[...] = p[:, ML_QK:2 * ML_QK].astype(BF16)
    kt_ref[...] = p[:, ML_QK:2 * ML_QK].T.astype(BF16)
    v_ref[...] = p[:, 2 * ML_QK:2 * ML_QK + ML_V].astype(BF16)
    og_ref[...] = jax.nn.sigmoid(p[:, 2 * ML_QK + ML_V:]).astype(BF16)
    h_lo = (h - hb.astype(F32)).astype(BF16)
    g = jnp.dot(jnp.concatenate([hb, h_lo, hb], axis=1), wg_ref[...],
                preferred_element_type=F32) + bg_ref[...]
    col = lax.broadcasted_iota(jnp.int32, g.shape, 1)
    g = _gate_act(g, (col // ML_HEADS) % 2 == 1)
    g_ref[...] = g[:, :ML_NG]
    gt_ref[...] = g.T[:ML_NG]


def _row_outputs(lay, n_src):
    if n_src != N_COMBINE_SRC:
        return [], []
    return [lay.row_spec(D_MODEL)], [jax.ShapeDtypeStruct((lay.rows, D_MODEL), F32)]


def mlstm_in(lay, xs, gain, mod, w_main, w_g, b_g):
    D = D_MODEL
    x_specs, x_shapes = _row_outputs(lay, len(xs))
    n_main = 2 * ML_QK + 2 * ML_V
    g_hi = w_g.astype(BF16)
    g_lo = (w_g - g_hi.astype(F32)).astype(BF16)
    wg3 = jnp.pad(jnp.concatenate([g_hi, g_hi, g_lo], axis=0), ((0, 0), (0, LANES - ML_NG)))
    bf = lambda w: jax.ShapeDtypeStruct((lay.rows, w), BF16)
    return pl.pallas_call(
        functools.partial(_mlstm_in_kernel, nb=lay.nb, n_src=len(xs)),
        grid=(lay.n_blocks,),
        in_specs=lay.token_specs(D, len(xs)) + [
            _const_spec((1, D)), lay.mod_spec(1), lay.mod_spec(0),
            _const_spec((D, n_main)), _const_spec((3 * D, LANES)), _const_spec((1, LANES))],
        out_specs=[lay.row_spec(ML_QK), lay.row_spec(ML_QK), pl.BlockSpec((ML_QK, TL), lambda r: (0, r)),
                   lay.row_spec(ML_V), lay.row_spec(ML_V),
                   lay.row_spec(ML_NG), pl.BlockSpec((ML_NG, TL), lambda r: (0, r))] + x_specs,
        out_shape=[bf(ML_QK), bf(ML_QK), jax.ShapeDtypeStruct((ML_QK, lay.rows), BF16),
                   bf(ML_V), bf(ML_V),
                   jax.ShapeDtypeStruct((lay.rows, ML_NG), F32),
                   jax.ShapeDtypeStruct((ML_NG, lay.rows), F32)] + x_shapes,
        compiler_params=_params(1),
        name="mlstm_in",
    )(*xs, gain, mod, mod, w_main, wg3, _pad_lanes(b_g))


def _split3(x):
    hi = x.astype(BF16)
    r1 = x - hi.astype(F32)
    mid = r1.astype(BF16)
    lo = (r1 - mid.astype(F32)).astype(BF16)
    return hi, mid, lo


def _dot_exact01(x, sel01, x_on_left):
    sel = sel01.astype(BF16)
    parts = [jnp.dot(p, sel, preferred_element_type=F32) if x_on_left
             else jnp.dot(sel, p, preferred_element_type=F32) for p in _split3(x)]
    return parts[0] + parts[1] + parts[2]


def _mlstm_dir(reverse, q_ref, k_ref, kt_ref, v_ref, g_ref, gt_ref, o_ref, s_ref, m_ref):
    T = ML_T
    row = lax.broadcasted_iota(jnp.int32, (T, T), 0)
    col = lax.broadcasted_iota(jnp.int32, (T, T), 1)
    mask = (col >= row) if reverse else (col <= row)
    gt = gt_ref[...]
    bc = _dot_exact01(g_ref[...], mask, x_on_left=False)
    br = _dot_exact01(gt, (row >= col) if reverse else (row <= col), x_on_left=True)
    gi, gf = (2 * ML_HEADS, 3 * ML_HEADS) if reverse else (0, ML_HEADS)
    lane = lax.broadcasted_iota(jnp.int32, (T, 2 * ML_DQK), 1)
    sub = lax.broadcasted_iota(jnp.int32, (2 * ML_DQK, T), 0)
    pick = (lax.broadcasted_iota(jnp.int32, (ML_NG, ML_V), 0)
            == gf + lax.broadcasted_iota(jnp.int32, (ML_NG, ML_V), 1) // ML_DV)
    b_all = _dot_exact01(bc, pick, x_on_left=True)
    ones = jnp.ones((T, ML_DV), BF16)
    heads = range(ML_HEADS)
    qm, vx, s_raw = {}, {}, {}
    for h in heads:
        pair = (h // 2) * 2 * ML_DQK
        own = (lane >= ML_DQK) if (h % 2) else (lane < ML_DQK)
        qp = q_ref[:, pair:pair + 2 * ML_DQK]
        qm[h] = jnp.where(own, qp, jnp.zeros_like(qp))
        vx[h] = jnp.concatenate([v_ref[:, h * ML_DV:(h + 1) * ML_DV], ones], axis=1)
        s_raw[h] = lax.dot_general(qm[h], k_ref[:, pair:pair + 2 * ML_DQK], (((1,), (1,)), ((), ())),
                                   preferred_element_type=F32)
    p, a, u_rep, m_prev, s_prev = {}, {}, {}, {}, {}
    for h in heads:
        c_row = gt[gi + h:gi + h + 1, :] - br[gf + h:gf + h + 1, :]
        e = jnp.where(mask, c_row, -jnp.inf)
        m_prev[h] = m_ref[h][0:1, 0:1]
        u = jnp.maximum(m_prev[h], jnp.max(e, axis=1, keepdims=True))
        p[h] = (s_raw[h] * jnp.exp2(e - u)).astype(BF16)
        u_rep[h] = jnp.broadcast_to(u, (T, ML_DV))
        a[h] = jnp.exp2(m_prev[h] - u_rep[h])
        s_prev[h] = s_ref[h]
    for h in heads:
        r = jnp.dot(p[h], vx[h], preferred_element_type=F32)
        qs = jnp.dot(qm[h], s_prev[h].astype(BF16), preferred_element_type=F32)
        num = r[:, :ML_DV] + a[h] * qs[:, :ML_DV]
        den = r[:, ML_DV:] + a[h] * qs[:, ML_DV:]
        floor = jnp.exp2(-(b_all[:, h * ML_DV:(h + 1) * ML_DV] + u_rep[h]))
        o_ref[:, h * ML_DV:(h + 1) * ML_DV] = (num / jnp.maximum(jnp.abs(den), floor)).astype(o_ref.dtype)
    for h in heads:
        pair = (h // 2) * 2 * ML_DQK
        b_row = br[gf + h:gf + h + 1, :]
        tot = b_row[:, 0:1] if reverse else b_row[:, T - 1:T]
        g_row = tot - b_row + gt[gi + h:gi + h + 1, :]
        m_new = jnp.maximum(tot + m_prev[h], jnp.max(g_row, axis=1, keepdims=True))
        decay = jnp.exp2(tot + m_prev[h] - m_new)
        wk = jnp.exp2(g_row - m_new)
        own_t = (sub >= ML_DQK) if (h % 2) else (sub < ML_DQK)
        kt = kt_ref[pair:pair + 2 * ML_DQK, :].astype(F32)
        kw = jnp.where(own_t, kt * wk, 0.0).astype(BF16)
        s_ref[h] = decay * s_prev[h] + jnp.dot(kw, vx[h], preferred_element_type=F32)
        m_ref[h] = jnp.broadcast_to(m_new, m_ref.shape[1:])


def _mlstm_scan_kernel(qf_ref, kf_ref, ktf_ref, vf_ref, gf_ref, gtf_ref,
                       qb_ref, kb_ref, ktb_ref, vb_ref, gb_ref, gtb_ref,
                       of_ref, ob_ref, sf_ref, mf_ref, sb_ref, mb_ref):
    @pl.when(pl.program_id(1) == 0)
    def _():
        sf_ref[...] = jnp.zeros_like(sf_ref)
        mf_ref[...] = jnp.zeros_like(mf_ref)
        sb_ref[...] = jnp.zeros_like(sb_ref)
        mb_ref[...] = jnp.zeros_like(mb_ref)

    _mlstm_dir(False, qf_ref, kf_ref, ktf_ref, vf_ref, gf_ref, gtf_ref, of_ref, sf_ref, mf_ref)
    _mlstm_dir(True, qb_ref, kb_ref, ktb_ref, vb_ref, gb_ref, gtb_ref, ob_ref, sb_ref, mb_ref)


def mlstm_scan(lay, q, k, kt, v, g, gt):
    assert lay.ctx_first
    per = TL // ML_T
    nb = lay.nb * per
    rev = lambda j: jnp.where(j < per, per - 1 - j, nb + per - 1 - j)
    fwd = lambda b, j: (b * nb + j, 0)
    bwd = lambda b, j: (b * nb + rev(j), 0)
    fwd_t = lambda b, j: (0, b * nb + j)
    bwd_t = lambda b, j: (0, b * nb + rev(j))

    def specs(im, imt):
        return [pl.BlockSpec((ML_T, ML_QK), im), pl.BlockSpec((ML_T, ML_QK), im),
                pl.BlockSpec((ML_QK, ML_T), imt), pl.BlockSpec((ML_T, ML_V), im),
                pl.BlockSpec((ML_T, ML_NG), im), pl.BlockSpec((ML_NG, ML_T), imt)]

    out_sds = jax.ShapeDtypeStruct((lay.rows, ML_V), BF16)
    state = [pltpu.VMEM((ML_HEADS, 2 * ML_DQK, ML_SW), F32),
             pltpu.VMEM((ML_HEADS, 8, LANES), F32)]
    return pl.pallas_call(
        _mlstm_scan_kernel,
        grid=(lay.n_batch, nb),
        in_specs=specs(fwd, fwd_t) + specs(bwd, bwd_t),
        out_specs=[pl.BlockSpec((ML_T, ML_V), fwd), pl.BlockSpec((ML_T, ML_V), bwd)],
        out_shape=[out_sds, out_sds],
        scratch_shapes=state + state,
        compiler_params=_params(2),
        name="mlstm_scan",
    )(q, k, kt, v, g, gt, q, k, kt, v, g, gt)


MLA_QK_PAD = 256
MLA_VW = 2 * MLA_V
MLA_IN_PAD = MLA_Q_LORA + MLA_KV_LORA + LANES
ROPE_HALF = MLA_ROPE // 4


def _mla_in_kernel(*refs, nb, n_src):
    srcs, rest = refs[:n_src], refs[n_src:]
    (gain_ref, sc_ref, sh_ref, win_ref, wuq_ref, wukv_ref, qn_ref, kvn_ref,
     qnn_ref, qnr_ref, knn_ref, knr_ref, cos_ref, sa_ref, sb_ref) = rest[:15]
    q_out, k_out, v_out = rest[15:18]
    x = _token_block(srcs, pl.program_id(0), nb)
    if n_src == N_COMBINE_SRC:
        rest[18][...] = x
    h = _norm_mod(x, gain_ref[...], sc_ref[0], sh_ref[0]).astype(BF16)
    p = jnp.dot(h, win_ref[...], preferred_element_type=F32)
    cq = _rms(p[:, :MLA_Q_LORA]) * qn_ref[...]
    ckv = _rms(p[:, MLA_Q_LORA:MLA_Q_LORA + MLA_KV_LORA]) * kvn_ref[...]
    kr = p[:, MLA_Q_LORA + MLA_KV_LORA:]
    q = jnp.dot(cq.astype(BF16), wuq_ref[...], preferred_element_type=F32)
    kv = jnp.dot(ckv.astype(BF16), wukv_ref[...], preferred_element_type=F32)
    nv = MLA_HEADS * MLA_NOPE
    ones_col = (lax.broadcasted_iota(jnp.int32, (TL, MLA_VW - MLA_V), 1) == 0).astype(BF16)
    cos, sa, sb = cos_ref[...], sa_ref[...], sb_ref[...]

    def rope(xp):
        return (xp * cos + pltpu.roll(xp, LANES - ROPE_HALF, 1) * sa + pltpu.roll(xp, ROPE_HALF, 1) * sb)

    heads = range(MLA_HEADS)
    kr = _rms(kr, MLA_ROPE) * knr_ref[...]
    qn = {hd: _rms(q[:, hd * MLA_QK_PAD:hd * MLA_QK_PAD + MLA_NOPE]) * qnn_ref[...] for hd in heads}
    qr = {hd: _rms(q[:, hd * MLA_QK_PAD + MLA_NOPE:(hd + 1) * MLA_QK_PAD], MLA_ROPE) * qnr_ref[...]
          for hd in heads}
    kn = {hd: _rms(kv[:, hd * MLA_NOPE:(hd + 1) * MLA_NOPE]) * knn_ref[...] for hd in heads}
    kr = rope(kr).astype(BF16)
    qr = {hd: rope(qr[hd]) for hd in heads}
    for hd in heads:
        c0 = hd * MLA_QK_PAD
        q_out[:, c0:c0 + MLA_NOPE] = qn[hd].astype(BF16)
        q_out[:, c0 + MLA_NOPE:c0 + MLA_QK_PAD] = qr[hd].astype(BF16)
        v0 = nv + hd * MLA_V
        v_out[:, hd * MLA_VW:hd * MLA_VW + MLA_V] = kv[:, v0:v0 + MLA_V].astype(BF16)
        v_out[:, hd * MLA_VW + MLA_V:(hd + 1) * MLA_VW] = ones_col
        k_out[:, c0:c0 + MLA_NOPE] = kn[hd].astype(BF16)
        k_out[:, c0 + MLA_NOPE:c0 + MLA_QK_PAD] = kr


def _pad_lanes(g):
    return jnp.pad(g, (0, LANES - g.shape[0])).reshape(1, LANES)


def mla_in(lay, xs, gain, mod, w_in, q_norm, kv_norm, w_uq, w_ukv, qn_nope, qn_rope, kn_nope, kn_rope,
           tables):
    D = D_MODEL
    Hn = MLA_HEADS
    win = jnp.pad(w_in, ((0, 0), (0, MLA_IN_PAD - w_in.shape[1]))).astype(BF16)
    wuq = jnp.pad(w_uq.reshape(MLA_Q_LORA, Hn, MLA_NOPE + MLA_ROPE),
                  ((0, 0), (0, 0), (0, MLA_QK_PAD - MLA_NOPE - MLA_ROPE)))
    wuq = wuq.reshape(MLA_Q_LORA, Hn * MLA_QK_PAD).astype(BF16)
    wkv = w_ukv.reshape(MLA_KV_LORA, Hn, MLA_NOPE + MLA_V)
    wukv = jnp.concatenate([wkv[:, :, :MLA_NOPE].reshape(MLA_KV_LORA, Hn * MLA_NOPE),
                            wkv[:, :, MLA_NOPE:].reshape(MLA_KV_LORA, Hn * MLA_V)], axis=1).astype(BF16)
    nb = lay.nb
    x_specs, x_shapes = _row_outputs(lay, len(xs))
    tab_spec = pl.BlockSpec((TL, LANES), lambda r: (r % nb, 0))
    bf = lambda w: jax.ShapeDtypeStruct((lay.rows, w), BF16)
    return pl.pallas_call(
        functools.partial(_mla_in_kernel, nb=nb, n_src=len(xs)),
        grid=(lay.n_blocks,),
        in_specs=lay.token_specs(D, len(xs)) + [
                  _const_spec((1, D)), lay.mod_spec(1), lay.mod_spec(0),
                  _const_spec(win.shape), _const_spec(wuq.shape), _const_spec(wukv.shape),
                  _const_spec((1, MLA_Q_LORA)), _const_spec((1, MLA_KV_LORA)),
                  _const_spec((1, LANES)), _const_spec((1, LANES)), _const_spec((1, LANES)),
                  _const_spec((1, LANES)), tab_spec, tab_spec, tab_spec],
        out_specs=[lay.row_spec(Hn * MLA_QK_PAD), lay.row_spec(Hn * MLA_QK_PAD),
                   lay.row_spec(Hn * MLA_VW)] + x_specs,
        out_shape=[bf(Hn * MLA_QK_PAD), bf(Hn * MLA_QK_PAD), bf(Hn * MLA_VW)] + x_shapes,
        compiler_params=_params(1),
        name="mla_in",
    )(*xs, gain, mod, mod, win, wuq, wukv, q_norm.reshape(1, -1), kv_norm.reshape(1, -1),
      (qn_nope * (MLA_SCALE * LOG2E)).reshape(1, -1), _pad_lanes(qn_rope * (MLA_SCALE * LOG2E)),
      kn_nope.reshape(1, -1), _pad_lanes(kn_rope), *tables)


def rope_tables(n_ctx, n_lat):
    n_freq = MLA_ROPE // 4
    inv = ROPE_THETA ** (-jnp.arange(n_freq, dtype=F32) / n_freq)
    t = jnp.arange(n_lat)
    a_r = (t // GRID_W).astype(F32)[:, None] * inv
    a_c = (t % GRID_W).astype(F32)[:, None] * inv
    ang = jnp.concatenate([a_r, a_r, a_c, a_c], axis=-1)
    ang = jnp.concatenate([jnp.zeros((n_ctx, MLA_ROPE), F32), ang], axis=0)
    cos, sin = jnp.cos(ang), jnp.sin(ang)
    low = (jnp.arange(MLA_ROPE) % (2 * ROPE_HALF)) < ROPE_HALF
    pad = lambda a: jnp.pad(a, ((0, 0), (0, LANES - MLA_ROPE)))
    return pad(cos), pad(jnp.where(low, -sin, 0.0)), pad(jnp.where(low, 0.0, sin))


MLA_HPS = 4


def _mla_attn_kernel(q_ref, k_ref, v_ref, o_ref, *, n_ctx):
    def attend(n_keys):
        heads = range(MLA_HPS)
        s, p = {}, {}
        for hd in heads:
            q = q_ref[0, :, hd * MLA_QK_PAD:(hd + 1) * MLA_QK_PAD]
            k = k_ref[0, :n_keys, hd * MLA_QK_PAD:(hd + 1) * MLA_QK_PAD]
            s[hd] = lax.dot_general(q, k, (((1,), (1,)), ((), ())), preferred_element_type=F32)
        for hd in heads:
            p[hd] = jnp.exp2(s[hd] - s[hd].max(axis=1, keepdims=True)).astype(BF16)
        for hd in heads:
            v = v_ref[0, :n_keys, hd * MLA_VW:(hd + 1) * MLA_VW]
            r = jnp.dot(p[hd], v, preferred_element_type=F32)
            o_ref[0, :, hd * MLA_V:(hd + 1) * MLA_V] = (
                r[:, :MLA_V] / r[:, MLA_V:MLA_V + 1]).astype(o_ref.dtype)

    @pl.when(pl.program_id(2) == 0)
    def _():
        attend(n_ctx)

    @pl.when(pl.program_id(2) > 0)
    def _():
        attend(k_ref.shape[1])


def mla_attention(lay, q, k, v):
    assert lay.ctx_first
    Bn, S = lay.n_batch, lay.nb * TL
    q3, k3, v3 = (a.reshape(Bn, S, a.shape[-1]) for a in (q, k, v))
    out = pl.pallas_call(
        functools.partial(_mla_attn_kernel, n_ctx=TL),
        grid=(Bn, MLA_HEADS // MLA_HPS, lay.nb),
        in_specs=[pl.BlockSpec((1, TL, MLA_HPS * MLA_QK_PAD), lambda b, h, i: (b, i, h)),
                  pl.BlockSpec((1, S, MLA_HPS * MLA_QK_PAD), lambda b, h, i: (b, 0, h),
                               pipeline_mode=pl.Buffered(1)),
                  pl.BlockSpec((1, S, MLA_HPS * MLA_VW), lambda b, h, i: (b, 0, h),
                               pipeline_mode=pl.Buffered(1))],
        out_specs=pl.BlockSpec((1, TL, MLA_HPS * MLA_V), lambda b, h, i: (b, i, h)),
        out_shape=jax.ShapeDtypeStruct((Bn, S, MLA_HEADS * MLA_V), BF16),
        compiler_params=_params(3),
        name="mla_attention",
    )(q3, k3, v3)
    return out.reshape(lay.rows, MLA_HEADS * MLA_V)


N_PROLOGUE = {"conv": 5, "mlstm": 4, "mla": 1}


def _mixer_out_kernel(*refs, kind, nb, ctx_first, n_src):
    n_pro = N_PROLOGUE[kind]
    pro = refs[:n_pro]
    wout_ref = refs[n_pro]
    x_srcs = refs[n_pro + 1:n_pro + 1 + n_src]
    (ga_ref, gain_ref, sc_ref, sh_ref, wr_ref, br_ref,
     xo_ref, h2_ref, te_ref, gate_ref, rank_ref, cnt_ref, carry_ref) = refs[n_pro + 1 + n_src:]
    r = pl.program_id(0)

    if kind == "conv":
        vprev_ref, v_ref, vnext_ref, bg_ref, cw_ref = pro
        j = r % nb
        first = (j == 0) | (j == 1) if ctx_first else (j == 0)
        last = (j == nb - 1) | (j == 0) if ctx_first else (j == nb - 1)
        v = v_ref[...].astype(F32)
        rows = lax.broadcasted_iota(jnp.int32, (TL, 1), 0)
        prev_row = jnp.where(first, 0.0, vprev_ref[BF16_ROWS - 1:BF16_ROWS, :].astype(F32))
        next_row = jnp.where(last, 0.0, vnext_ref[0:1, :].astype(F32))
        up = jnp.where(rows == 0, prev_row, pltpu.roll(v, 1, 0))
        dn = jnp.where(rows == TL - 1, next_row, pltpu.roll(v, TL - 1, 0))
        cw = cw_ref[...]
        a = bg_ref[...].astype(F32) * (up * cw[0:1] + v * cw[1:2] + dn * cw[2:3])
    elif kind == "mlstm":
        hf_ref, hb_ref, og_ref, ng_ref = pro
        hh = hf_ref[...].astype(F32) + hb_ref[...].astype(F32)
        a = jnp.concatenate([_rms(hh[:, h * ML_DV:(h + 1) * ML_DV]) for h in range(ML_HEADS)], axis=1)
        a = a * ng_ref[...] * og_ref[...].astype(F32)
    else:
        a = pro[0][...]

    y = jnp.dot(a.astype(BF16), wout_ref[...], preferred_element_type=F32)
    xn = _token_block(x_srcs, r, nb) + ga_ref[0] * y
    xo_ref[...] = xn
    h2 = _norm_mod(xn, gain_ref[...], sc_ref[0], sh_ref[0])
    h2_ref[...] = _pack_rows(h2)
    lt = jnp.dot(h2.astype(BF16), wr_ref[...], preferred_element_type=F32).T
    logits = lt[:N_EXPERTS] + lt[N_EXPERTS:2 * N_EXPERTS] + br_ref[...]

    sub = lax.broadcasted_iota(jnp.int32, (N_EXPERTS, TL), 0)
    sub_k = lax.broadcasted_iota(jnp.int32, (TOP_K, TL), 0)
    work = logits
    sel = jnp.zeros((N_EXPERTS, TL), F32)
    top_e = jnp.zeros((TOP_K, TL), jnp.int32)
    top_v = jnp.zeros((TOP_K, TL), F32)
    picks = []
    for kk in range(TOP_K):
        m = work.max(axis=0, keepdims=True)
        idx = jnp.min(jnp.where(work == m, sub, N_EXPERTS), axis=0, keepdims=True)
        hit = sub == idx
        picks.append(hit)
        sel = jnp.where(hit, 1.0, sel)
        work = jnp.where(hit, -jnp.inf, work)
        top_e = jnp.where(sub_k == kk, idx, top_e)
        top_v = jnp.where(sub_k == kk, m, top_v)
    ex = jnp.exp(top_v - top_v[0:1])
    gate_ref[...] = ex / ex.sum(axis=0, keepdims=True)
    te_ref[...] = top_e

    @pl.when(r == 0)
    def _():
        carry_ref[...] = jnp.zeros_like(carry_ref)

    tr = lax.broadcasted_iota(jnp.int32, (TL, TL), 0)
    tc = lax.broadcasted_iota(jnp.int32, (TL, TL), 1)
    before = jnp.dot(sel.astype(BF16), (tr < tc).astype(BF16), preferred_element_type=F32)
    pos = before + carry_ref[...]
    rank = jnp.zeros((TOP_K, TL), F32)
    for kk in range(TOP_K):
        rk = jnp.sum(jnp.where(picks[kk], pos, 0.0), axis=0, keepdims=True)
        rank = jnp.where(sub_k == kk, rk, rank)
    rank_ref[...] = rank.astype(jnp.int32)
    total = carry_ref[...] + jnp.sum(sel, axis=1, keepdims=True)
    carry_ref[...] = total
    cnt_ref[...] = total


def mixer_out(lay, kind, pro_args, w_out, xs, mod, gain_f, w_r, b_r):
    D = D_MODEL
    nb = lay.nb
    if kind == "conv":
        v, bg, cw = pro_args
        per = TL // BF16_ROWS
        last_tile = lay.rows // BF16_ROWS - 1
        pro_specs = [pl.BlockSpec((BF16_ROWS, D), lambda r: (jnp.maximum(r * per - 1, 0), 0)),
                     lay.row_spec(D),
                     pl.BlockSpec((BF16_ROWS, D), lambda r: (jnp.minimum((r + 1) * per, last_tile), 0)),
                     lay.row_spec(D), _const_spec((CONV_WIDTH, D))]
        pro_in = [v, v, v, bg, cw]
    elif kind == "mlstm":
        h_f, h_b, og, ng = pro_args
        pro_specs = [lay.row_spec(ML_V), lay.row_spec(ML_V), lay.row_spec(ML_V), _const_spec((1, ML_V))]
        pro_in = [h_f, h_b, og, ng.reshape(1, ML_V)]
    else:
        pro_specs = [lay.row_spec(D)]
        pro_in = list(pro_args)
    k_in = w_out.shape[0]
    small = lambda dt: jax.ShapeDtypeStruct((TOP_K, lay.rows), dt)
    small_spec = pl.BlockSpec((TOP_K, TL), lambda r: (0, r))
    return pl.pallas_call(
        functools.partial(_mixer_out_kernel, kind=kind, nb=nb, ctx_first=lay.ctx_first, n_src=len(xs)),
        grid=(lay.n_blocks,),
        in_specs=pro_specs + [_const_spec((k_in, D))] + lay.token_specs(D, len(xs)) + [
            lay.mod_spec(2), _const_spec((1, D)), lay.mod_spec(4), lay.mod_spec(3),
            _const_spec((D, LANES)), _const_spec((N_EXPERTS, 1))],
        out_specs=[lay.row_spec(D), lay.row_spec(PACK_W), small_spec, small_spec, small_spec,
                   _const_spec((N_EXPERTS, 1))],
        out_shape=[jax.ShapeDtypeStruct((lay.rows, D), F32),
                   jax.ShapeDtypeStruct((lay.rows, PACK_W), jnp.int32),
                   small(jnp.int32), small(F32), small(jnp.int32),
                   jax.ShapeDtypeStruct((N_EXPERTS, 1), F32)],
        scratch_shapes=[pltpu.VMEM((N_EXPERTS, 1), F32)],
        compiler_params=_params(1),
        name="mixer_out_" + kind,
    )(*pro_in, w_out.astype(BF16), *xs, mod, gain_f, mod, mod, _split_weight_lanes(w_r),
      b_r.reshape(N_EXPERTS, 1))


def _expert_ffn_kernel(blk_e_ref, first_ref, valid_ref, x_ref, w1_ref, b1_ref, w2_ref, b2_ref, o_ref,
                       w1b_ref, w2b_ref):
    del blk_e_ref
    i = pl.program_id(0)

    @pl.when(first_ref[i] == 1)
    def _():
        w1b_ref[...] = w1_ref[0, 0].astype(BF16)
        w2b_ref[...] = w2_ref[0, 0].astype(BF16)

    @pl.when(valid_ref[i] > 0)
    def _():
        x = _unpack_rows(x_ref[...], BF16)
        h = jnp.dot(x, w1b_ref[...], preferred_element_type=F32) + b1_ref[0, 0]
        glu = jnp.minimum(h[:, :MOE_FF], SWIGLU_LIMIT)
        lin = jnp.clip(h[:, MOE_FF:], -SWIGLU_LIMIT, SWIGLU_LIMIT)
        act = glu * jax.nn.sigmoid(SWIGLU_ALPHA * glu) * (lin + 1.0)
        y = jnp.dot(act.astype(BF16), w2b_ref[...], preferred_element_type=F32)
        o_ref[...] = _pack_rows(y + b2_ref[0, 0])

    @pl.when(valid_ref[i] == 0)
    def _():
        o_ref[...] = jnp.zeros_like(o_ref)


def expert_ffn(layer, xp, blk_e, blk_first, blk_valid, w1, b1, w2, b2):
    n_rows = xp.shape[0]
    D, F2 = D_MODEL, 2 * MOE_FF
    n_blk = n_rows // MOE_BLOCK
    grid_spec = pltpu.PrefetchScalarGridSpec(
        num_scalar_prefetch=3,
        grid=(n_blk,),
        in_specs=[
            pl.BlockSpec((MOE_BLOCK, PACK_W), lambda i, be, fi, nu: (i, 0)),
            pl.BlockSpec((1, 1, D, F2), lambda i, be, fi, nu: (layer, be[i], 0, 0)),
            pl.BlockSpec((1, 1, 1, F2), lambda i, be, fi, nu: (layer, be[i], 0, 0)),
            pl.BlockSpec((1, 1, MOE_FF, D), lambda i, be, fi, nu: (layer, be[i], 0, 0)),
            pl.BlockSpec((1, 1, 1, D), lambda i, be, fi, nu: (layer, be[i], 0, 0)),
        ],
        out_specs=pl.BlockSpec((MOE_BLOCK, PACK_W), lambda i, be, fi, nu: (i, 0)),
        scratch_shapes=[pltpu.VMEM((D, F2), BF16), pltpu.VMEM((MOE_FF, D), BF16)],
    )
    return pl.pallas_call(
        _expert_ffn_kernel,
        grid_spec=grid_spec,
        out_shape=jax.ShapeDtypeStruct((n_rows, PACK_W), jnp.int32),
        compiler_params=_params(1),
        name="expert_ffn",
    )(blk_e, blk_first, blk_valid, xp, w1, b1, w2, b2)


SC_CORES = 2
SC_SUBCORES = 16
SC_CHUNKS = (64, 32)


def _sc_chunk(*counts):
    n_workers = SC_CORES * SC_SUBCORES
    for chunk in SC_CHUNKS:
        if all(n % (chunk * n_workers) == 0 for n in counts):
            return chunk
    raise ValueError(f"row counts {counts} do not split over {n_workers} subcores")


def sc_gather(table, idx):
    n_idx = idx.shape[0]
    width = table.shape[1]
    n_workers = SC_CORES * SC_SUBCORES
    per_worker = n_idx // n_workers
    chunk = _sc_chunk(n_idx)
    n_chunks = per_worker // chunk
    assert n_chunks * chunk * n_workers == n_idx and n_chunks % 2 == 0
    mesh = plsc.VectorSubcoreMesh(core_axis_name="c", subcore_axis_name="s",
                                  num_cores=SC_CORES, num_subcores=SC_SUBCORES)

    def body(table_hbm, idx_hbm, out_hbm, idx_v, rows_v, gsem, wsem):
        wid = lax.axis_index("s") * SC_CORES + lax.axis_index("c")
        pltpu.sync_copy(idx_hbm.at[wid], idx_v)

        def gather(ci, slot):
            return pltpu.make_async_copy(table_hbm.at[idx_v.at[ci]], rows_v.at[slot], gsem.at[slot])

        def write(ci, slot):
            return pltpu.make_async_copy(rows_v.at[slot], out_hbm.at[ci, wid], wsem.at[slot])

        gather(0, 0).start()

        @pl.loop(0, n_chunks, step=2)
        def _(c0):
            for slot in range(2):
                ci = c0 + slot
                other = 1 - slot

                @pl.when(ci + 1 < n_chunks)
                def _():
                    @pl.when(ci >= 1)
                    def _():
                        write(ci - 1, other).wait()
                    gather(ci + 1, other).start()

                gather(ci, slot).wait()
                write(ci, slot).start()

        write(n_chunks - 2, 0).wait()
        write(n_chunks - 1, 1).wait()

    out = pl.kernel(
        body,
        out_type=jax.ShapeDtypeStruct((n_chunks, n_workers, chunk, width), table.dtype),
        mesh=mesh,
        scratch_types=[pltpu.VMEM((n_chunks, chunk), jnp.int32),
                       pltpu.VMEM((2, chunk, width), table.dtype),
                       pltpu.SemaphoreType.DMA((2,)),
                       pltpu.SemaphoreType.DMA((2,))],
        name="sc_gather",
    )(table, idx.reshape(n_chunks, n_workers, chunk).transpose(1, 0, 2))
    return out.reshape(n_idx, width)


def sc_dispatch(table, dest, pad_rows):
    n_tok, width = table.shape
    n_picks = dest.shape[0]
    n_pad = pad_rows.shape[0]
    n_workers = SC_CORES * SC_SUBCORES
    chunk = _sc_chunk(n_tok, n_pad)
    per_w = n_tok // chunk // n_workers
    pad_w = n_pad // chunk // n_workers
    assert per_w * chunk * n_workers == n_tok and pad_w * chunk * n_workers == n_pad
    mesh = plsc.VectorSubcoreMesh(core_axis_name="c", subcore_axis_name="s",
                                  num_cores=SC_CORES, num_subcores=SC_SUBCORES)

    def body(table_hbm, idx_hbm, pad_hbm, zero_hbm, out_hbm, idx_v, pad_v, rows_v, zero_v,
             rsem, ssem, zsem):
        wid = lax.axis_index("s") * SC_CORES + lax.axis_index("c")
        pltpu.sync_copy(idx_hbm.at[wid], idx_v)
        pltpu.sync_copy(pad_hbm.at[wid], pad_v)
        pltpu.sync_copy(zero_hbm, zero_v)

        def zero_fill(pc):
            return pltpu.make_async_copy(zero_v, out_hbm.at[pad_v.at[pc]], zsem)

        for pc in range(pad_w):
            zero_fill(pc).start()

        def scatter(ci, kk):
            return pltpu.make_async_copy(rows_v, out_hbm.at[idx_v.at[ci * n_picks + kk]], ssem)

        @pl.loop(0, per_w)
        def _(ci):
            pltpu.async_copy(table_hbm.at[ci, wid], rows_v, rsem).wait()
            for kk in range(n_picks):
                scatter(ci, kk).start()
            for kk in range(n_picks):
                scatter(ci, kk).wait()

        for pc in range(pad_w):
            zero_fill(pc).wait()

    idx = dest.reshape(n_picks, per_w, n_workers, chunk).transpose(2, 1, 0, 3)
    idx = idx.reshape(n_workers, per_w * n_picks, chunk)
    return pl.kernel(
        body,
        out_type=jax.ShapeDtypeStruct((n_tok * n_picks + n_pad, width), table.dtype),
        mesh=mesh,
        scratch_types=[pltpu.VMEM((per_w * n_picks, chunk), jnp.int32),
                       pltpu.VMEM((pad_w, chunk), jnp.int32),
                       pltpu.VMEM((chunk, width), table.dtype),
                       pltpu.VMEM((chunk, width), table.dtype),
                       pltpu.SemaphoreType.DMA, pltpu.SemaphoreType.DMA, pltpu.SemaphoreType.DMA],
        name="sc_dispatch",
    )(table.reshape(per_w, n_workers, chunk, width), idx,
      pad_rows.reshape(n_workers, pad_w, chunk), jnp.zeros((chunk, width), table.dtype))


def _combine_kernel(x_ref, *refs):
    y_refs, (gate_ref, gf_ref, o_ref) = refs[:TOP_K], refs[TOP_K:]
    gates = gate_ref[...]
    acc = gates[:, 0:1] * _unpack_rows(y_refs[0][...], F32)
    for kk in range(1, TOP_K):
        acc = acc + gates[:, kk:kk + 1] * _unpack_rows(y_refs[kk][...], F32)
    o_ref[...] = x_ref[...] + gf_ref[0] * acc


def moe_combine(lay, x, yg, gates, mod, drop_ctx):
    D = D_MODEL
    if drop_ctx:
        nbo = lay.nb - 1
        src = lambda r: (r // nbo) * lay.nb + 1 + r % nbo
        n_out = lay.n_batch * nbo
    else:
        src = lambda r: r
        n_out = lay.n_blocks
    y_specs = [pl.BlockSpec((TL, PACK_W), functools.partial(lambda kk, r: (kk * lay.n_blocks + src(r), 0), kk))
               for kk in range(TOP_K)]
    return pl.pallas_call(
        _combine_kernel,
        grid=(n_out,),
        in_specs=[pl.BlockSpec((TL, D), lambda r: (src(r), 0))] + y_specs + [
            pl.BlockSpec((TL, TOP_K), lambda r: (src(r), 0)),
            pl.BlockSpec((1, 1, D), lambda r: (lay.mod_row(src(r)), 0, 5))],
        out_specs=pl.BlockSpec((TL, D), lambda r: (r, 0)),
        out_shape=jax.ShapeDtypeStruct((n_out * TL, D), F32),
        compiler_params=_params(1),
        name="moe_combine",
    )(x, yg, yg, yg, yg, gates, mod)


def moe_route(top_e, rank, counts):
    T = top_e.shape[1]
    assert (T * TOP_K) % MOE_BLOCK == 0
    counts = counts.reshape(N_EXPERTS).astype(jnp.int32)
    padded = (counts + MOE_BLOCK - 1) // MOE_BLOCK * MOE_BLOCK
    padded_end = jnp.cumsum(padded)
    padded_start = padded_end - padded
    experts = jnp.arange(N_EXPERTS)
    start_of = jnp.sum(jnp.where(top_e[..., None] == experts, padded_start, 0), axis=-1)
    dest = (start_of + rank).astype(jnp.int32)
    n_pad = N_EXPERTS * MOE_BLOCK
    n_rows = T * TOP_K + n_pad
    n_blk = n_rows // MOE_BLOCK
    blk_start = jnp.arange(n_blk) * MOE_BLOCK
    blk_e = jnp.minimum(jnp.sum(padded_end[None, :] <= blk_start[:, None], axis=1), N_EXPERTS - 1)
    blk_e = blk_e.astype(jnp.int32)
    blk_first = jnp.concatenate([jnp.ones((1,), jnp.int32), (blk_e[1:] != blk_e[:-1]).astype(jnp.int32)])
    blk_hot = blk_e[:, None] == experts
    in_grp = blk_start - jnp.sum(jnp.where(blk_hot, padded_start, 0), axis=-1)
    blk_valid = jnp.clip(jnp.sum(jnp.where(blk_hot, counts, 0), axis=-1) - in_grp, 0, MOE_BLOCK)
    blk_valid = jnp.where(blk_start < padded_end[-1], blk_valid, 0).astype(jnp.int32)
    tail = padded - counts
    tail_end = jnp.cumsum(tail)
    j = jnp.arange(n_pad)
    owner = jnp.sum(tail_end[None, :] <= j[:, None], axis=1)
    base = padded_start + counts - (tail_end - tail)
    in_group = jnp.sum(jnp.where(owner[:, None] == experts, base, 0), axis=-1) + j
    pad_rows = jnp.where(j < tail_end[-1], in_group, padded_end[-1] + j - tail_end[-1])
    return dest, pad_rows.astype(jnp.int32), blk_e, blk_first, blk_valid


def kernel(x, c, ctx, c_ctx, norm_mix, norm_ffn, w_mod, b_mod, conv_w_in, conv_w, conv_w_out, ml_w_in, ml_b_gate, ml_norm, ml_w_out, mla_w_in, mla_q_norm, mla_kv_norm, mla_w_uq, mla_w_ukv, mla_qn_nope, mla_qn_rope, mla_kn_nope, mla_kn_rope, mla_w_out, moe_w_router, moe_b_router, moe_w1, moe_b1, moe_w2, moe_b2):
    Bn, n_lat, D = x.shape
    n_ctx = ctx.shape[1]
    assert D == D_MODEL and n_ctx == TL and n_lat % TL == 0
    assert (DEPTH - 1) % N_MIXERS == 0
    full = Layout(Bn, (n_ctx + n_lat) // TL, True)
    lat_only = Layout(Bn, n_lat // TL, False)
    mods = ada_all(c, c_ctx, w_mod, b_mod)
    tables = rope_tables(n_ctx, n_lat)
    b1_all = moe_b1.reshape(DEPTH, N_EXPERTS, 1, 2 * MOE_FF)
    b2_all = moe_b2.reshape(DEPTH, N_EXPERTS, 1, D)
    Xs = (ctx.reshape(Bn * n_ctx, D), x.reshape(Bn * n_lat, D))
    for layer in range(DEPTH):
        kind, j = layer % N_MIXERS, layer // N_MIXERS
        lay = lat_only if layer == DEPTH - 1 else full
        mod = mods[layer]
        gain_a = norm_mix[layer].reshape(1, D)
        gain_f = norm_ffn[layer].reshape(1, D)
        if kind == 0:
            outs = conv_in(lay, Xs, gain_a, mod, conv_w_in[j].astype(BF16), src=full)
            bg, v = outs[:2]
            Xs = tuple(outs[2:]) or Xs
            pro, w_out, name = (v, bg, conv_w[j]), conv_w_out[j], "conv"
        elif kind == 1:
            w = ml_w_in[j]
            n_main = 2 * ML_QK + 2 * ML_V
            w_main = jnp.concatenate([w[:, :ML_QK] * ML_DQK ** -0.5, w[:, ML_QK:n_main]], axis=1)
            outs = mlstm_in(lay, Xs, gain_a, mod, w_main.astype(BF16), w[:, n_main:], ml_b_gate[j])
            q, k, kt, v, og, g, gt = outs[:7]
            Xs = tuple(outs[7:]) or Xs
            h_f, h_b = mlstm_scan(lay, q, k, kt, v, g, gt)
            pro, w_out, name = (h_f, h_b, og, ml_norm[j]), ml_w_out[j], "mlstm"
        else:
            outs = mla_in(lay, Xs, gain_a, mod, mla_w_in[j], mla_q_norm[j], mla_kv_norm[j],
                          mla_w_uq[j], mla_w_ukv[j], mla_qn_nope[j], mla_qn_rope[j],
                          mla_kn_nope[j], mla_kn_rope[j], tables)
            q, k, v = outs[:3]
            Xs = tuple(outs[3:]) or Xs
            pro, w_out, name = (mla_attention(lay, q, k, v),), mla_w_out[j], "mla"
        X, h2, top_e, gates, rank, counts = mixer_out(
            lay, name, pro, w_out, Xs, mod, gain_f, moe_w_router[layer], moe_b_router[layer])
        dest, pad_rows, blk_e, blk_first, blk_valid = moe_route(top_e, rank, counts)
        xp = sc_dispatch(h2, dest, pad_rows)
        yp = expert_ffn(layer, xp, blk_e, blk_first, blk_valid, moe_w1, b1_all, moe_w2, b2_all)
        yg = sc_gather(yp, dest.reshape(-1))
        if layer + 1 < DEPTH:
            Xs = (X,) + (yg,) * TOP_K + (gates.T, mod)
        else:
            Xs = (moe_combine(lay, X, yg, gates.T, mod, drop_ctx=False),)
    return Xs[0].reshape(Bn, n_lat, D)
```

```python
import functools

import jax
import jax.numpy as jnp
from jax import lax
from jax.experimental import pallas as pl
from jax.experimental.pallas import tpu as pltpu
from jax.experimental.pallas import tpu_sc as plsc

D_MODEL = 1024
DEPTH = 4
GRID_W = 64
N_MIXERS = 3
N_ADA = 6
RMS_EPS = 1e-6
CONV_WIDTH = 3
ML_HEADS = 8
ML_DQK = 64
ML_DV = 128
ML_QK = ML_HEADS * ML_DQK
ML_V = ML_HEADS * ML_DV
GATE_CAP = 15.0
MLA_HEADS = 8
MLA_NOPE = 128
MLA_ROPE = 64
MLA_V = 128
MLA_Q_LORA = 384
MLA_KV_LORA = 256
MLA_SCALE = (MLA_NOPE + MLA_ROPE) ** -0.5
ROPE_THETA = 10000.0
N_EXPERTS = 32
TOP_K = 4
MOE_FF = D_MODEL
SWIGLU_ALPHA = 1.702
SWIGLU_LIMIT = 7.0
MOE_BLOCK = 512

TL = 256
LANES = 128
BF16_ROWS = 16
VMEM_LIMIT = 48 * 1024 * 1024
HI = lax.Precision.HIGHEST
F32 = jnp.float32
BF16 = jnp.bfloat16


def _params(n_axes):
    return pltpu.CompilerParams(dimension_semantics=("arbitrary",) * n_axes,
                                vmem_limit_bytes=VMEM_LIMIT)


def _rms(x, width=None):
    width = x.shape[-1] if width is None else width
    return x * lax.rsqrt(jnp.sum(x * x, axis=-1, keepdims=True) * (1.0 / width) + RMS_EPS)


def _norm_mod(x, gain, scale, shift):
    return _rms(x) * (gain * (1.0 + scale)) + shift


def _split_weight_lanes(w):
    hi = w.astype(BF16)
    lo = (w - hi.astype(F32)).astype(BF16)
    return jnp.pad(jnp.concatenate([hi, lo], axis=1), ((0, 0), (0, LANES - 2 * w.shape[1])))


PACK_W = D_MODEL // 2
HIGH_HALF = -65536


def _pack_rows(x):
    xb = x.astype(BF16).astype(F32)
    lo = lax.bitcast_convert_type(xb[:, :PACK_W], jnp.int32)
    hi = lax.bitcast_convert_type(xb[:, PACK_W:], jnp.int32)
    return hi | lax.shift_right_logical(lo, 16)


def _unpack_rows(w, dtype):
    lo = lax.bitcast_convert_type(lax.shift_left(w, 16), F32)
    hi = lax.bitcast_convert_type(w & HIGH_HALF, F32)
    return jnp.concatenate([lo.astype(dtype), hi.astype(dtype)], axis=1)


class Layout:
    def __init__(self, n_batch, nb, ctx_first):
        self.n_batch, self.nb, self.ctx_first = n_batch, nb, ctx_first
        self.n_blocks = n_batch * nb
        self.rows = self.n_blocks * TL

    def mod_row(self, r):
        b = r // self.nb
        return jnp.where(r % self.nb == 0, self.n_batch, b) if self.ctx_first else b

    def row_spec(self, width):
        return pl.BlockSpec((TL, width), lambda r: (r, 0))

    def mod_spec(self, piece):
        return pl.BlockSpec((1, 1, D_MODEL), lambda r: (self.mod_row(r), 0, piece))

    def token_specs(self, width, n_src, src=None):
        if n_src == 1:
            return [self.row_spec(width)]
        if n_src == N_COMBINE_SRC:
            src = src or self
            at = (lambda r: r) if src.nb == self.nb else (lambda r: (r // self.nb) * src.nb + 1 + r % self.nb)
            picks = [pl.BlockSpec((TL, PACK_W), functools.partial(lambda kk, r: (kk * src.n_blocks + at(r), 0), kk))
                     for kk in range(TOP_K)]
            return ([pl.BlockSpec((TL, width), lambda r: (at(r), 0))] + picks
                    + [pl.BlockSpec((TL, TOP_K), lambda r: (at(r), 0)),
                       pl.BlockSpec((1, 1, D_MODEL), lambda r: (src.mod_row(at(r)), 0, 5))])
        nb = self.nb
        return [pl.BlockSpec((TL, width), lambda r: (r // nb, 0)),
                pl.BlockSpec((TL, width), lambda r: ((r // nb) * (nb - 1) + jnp.maximum(r % nb - 1, 0), 0))]


N_COMBINE_SRC = TOP_K + 3


def _token_block(srcs, r, nb):
    if len(srcs) == 1:
        return srcs[0][...]
    if len(srcs) == N_COMBINE_SRC:
        gates = srcs[TOP_K + 1][...]
        acc = gates[:, 0:1] * _unpack_rows(srcs[1][...], F32)
        for kk in range(1, TOP_K):
            acc = acc + gates[:, kk:kk + 1] * _unpack_rows(srcs[1 + kk][...], F32)
        return srcs[0][...] + srcs[TOP_K + 2][0] * acc
    return jnp.where(r % nb == 0, srcs[0][...], srcs[1][...])


def _const_spec(shape):
    return pl.BlockSpec(shape, lambda *_: (0,) * len(shape))


ADA_ROWS = 16
ADA_TN = 1536


def _ada_kernel(c_ref, w_ref, b_ref, o_ref):
    c = c_ref[...]
    s = c * jax.nn.sigmoid(c)
    o_ref[0] = jnp.dot(s, w_ref[0], precision=HI, preferred_element_type=F32) + b_ref[0]


def ada_all(c, c_ctx, w_mod, b_mod):
    Bn, D = c.shape
    assert Bn + 1 <= ADA_ROWS
    cond = jnp.zeros((ADA_ROWS, D), F32).at[:Bn].set(c).at[Bn].set(c_ctx)
    out = pl.pallas_call(
        _ada_kernel,
        grid=(DEPTH, N_ADA * D // ADA_TN),
        in_specs=[pl.BlockSpec((ADA_ROWS, D), lambda l, n: (0, 0)),
                  pl.BlockSpec((1, D, ADA_TN), lambda l, n: (l, 0, n)),
                  pl.BlockSpec((1, 1, ADA_TN), lambda l, n: (l, 0, n))],
        out_specs=pl.BlockSpec((1, ADA_ROWS, ADA_TN), lambda l, n: (l, 0, n)),
        out_shape=jax.ShapeDtypeStruct((DEPTH, ADA_ROWS, N_ADA * D), F32),
        compiler_params=_params(2),
        name="ada_mod",
    )(cond, w_mod, b_mod.reshape(DEPTH, 1, N_ADA * D))
    return out[:, :Bn + 1, None, :]


def _conv_in_kernel(*refs, nb, n_src):
    srcs, (gain_ref, sc_ref, sh_ref, w_ref, bg_ref, v_ref) = refs[:n_src], refs[n_src:n_src + 6]
    D = D_MODEL
    x = _token_block(srcs, pl.program_id(0), nb)
    if n_src == N_COMBINE_SRC:
        refs[n_src + 6][...] = x
    h = _norm_mod(x, gain_ref[...], sc_ref[0], sh_ref[0]).astype(BF16)
    p = jnp.dot(h, w_ref[...], preferred_element_type=F32)
    bg_ref[...] = p[:, :D].astype(BF16)
    v_ref[...] = (p[:, D:2 * D] * p[:, 2 * D:]).astype(BF16)


def conv_in(lay, xs, gain, mod, w_in, src=None):
    D = D_MODEL
    sds = jax.ShapeDtypeStruct((lay.rows, D), BF16)
    x_specs, x_shapes = _row_outputs(lay, len(xs))
    return pl.pallas_call(
        functools.partial(_conv_in_kernel, nb=lay.nb, n_src=len(xs)),
        grid=(lay.n_blocks,),
        in_specs=lay.token_specs(D, len(xs), src) + [_const_spec((1, D)), lay.mod_spec(1), lay.mod_spec(0),
                                                     _const_spec((D, 3 * D))],
        out_specs=[lay.row_spec(D), lay.row_spec(D)] + x_specs,
        out_shape=[sds, sds] + x_shapes,
        compiler_params=_params(1),
        name="conv_in",
    )(*xs, gain, mod, mod, w_in)


ML_T = 256
ML_SW = 2 * ML_DV
ML_NG = 4 * ML_HEADS


LOG2E = 1.4426950408889634


def _gate_act(g, is_forget):
    g = GATE_CAP * jnp.tanh(g * (1.0 / GATE_CAP))
    log_sig = jnp.minimum(g, 0.0) - jnp.log(1.0 + jnp.exp(-jnp.abs(g)))
    return jnp.where(is_forget, log_sig, g) * LOG2E


def _mlstm_in_kernel(*refs, nb, n_src):
    srcs, (gain_ref, sc_ref, sh_ref, w_ref, wg_ref, bg_ref), outs = refs[:n_src], refs[n_src:n_src + 6], refs[n_src + 6:]
    q_ref, k_ref, kt_ref, v_ref, og_ref, g_ref, gt_ref = outs[:7]
    x = _token_block(srcs, pl.program_id(0), nb)
    if n_src == N_COMBINE_SRC:
        outs[7][...] = x
    h = _norm_mod(x, gain_ref[...], sc_ref[0], sh_ref[0])
    hb = h.astype(BF16)
    p = jnp.dot(hb, w_ref[...], preferred_element_type=F32)
    q_ref[...] = p[:, :ML_QK].astype(BF16)
    k_ref[...] = p[:, ML_QK:2 * ML_QK].astype(BF16)
    kt_ref[...] = p[:, ML_QK:2 * ML_QK].T.astype(BF16)
    v_ref[...] = p[:, 2 * ML_QK:2 * ML_QK + ML_V].astype(BF16)
    og_ref[...] = jax.nn.sigmoid(p[:, 2 * ML_QK + ML_V:]).astype(BF16)
    h_lo = (h - hb.astype(F32)).astype(BF16)
    g = jnp.dot(jnp.concatenate([hb, h_lo, hb], axis=1), wg_ref[...],
                preferred_element_type=F32) + bg_ref[...]
    col = lax.broadcasted_iota(jnp.int32, g.shape, 1)
    g = _gate_act(g, (col // ML_HEADS) % 2 == 1)
    g_ref[...] = g[:, :ML_NG]
    gt_ref[...] = g.T[:ML_NG]


def _row_outputs(lay, n_src):
    if n_src != N_COMBINE_SRC:
        return [], []
    return [lay.row_spec(D_MODEL)], [jax.ShapeDtypeStruct((lay.rows, D_MODEL), F32)]


def mlstm_in(lay, xs, gain, mod, w_main, w_g, b_g):
    D = D_MODEL
    x_specs, x_shapes = _row_outputs(lay, len(xs))
    n_main = 2 * ML_QK + 2 * ML_V
    g_hi = w_g.astype(BF16)
    g_lo = (w_g - g_hi.astype(F32)).astype(BF16)
    wg3 = jnp.pad(jnp.concatenate([g_hi, g_hi, g_lo], axis=0), ((0, 0), (0, LANES - ML_NG)))
    bf = lambda w: jax.ShapeDtypeStruct((lay.rows, w), BF16)
    return pl.pallas_call(
        functools.partial(_mlstm_in_kernel, nb=lay.nb, n_src=len(xs)),
        grid=(lay.n_blocks,),
        in_specs=lay.token_specs(D, len(xs)) + [
            _const_spec((1, D)), lay.mod_spec(1), lay.mod_spec(0),
            _const_spec((D, n_main)), _const_spec((3 * D, LANES)), _const_spec((1, LANES))],
        out_specs=[lay.row_spec(ML_QK), lay.row_spec(ML_QK), pl.BlockSpec((ML_QK, TL), lambda r: (0, r)),
                   lay.row_spec(ML_V), lay.row_spec(ML_V),
                   lay.row_spec(ML_NG), pl.BlockSpec((ML_NG, TL), lambda r: (0, r))] + x_specs,
        out_shape=[bf(ML_QK), bf(ML_QK), jax.ShapeDtypeStruct((ML_QK, lay.rows), BF16),
                   bf(ML_V), bf(ML_V),
                   jax.ShapeDtypeStruct((lay.rows, ML_NG), F32),
                   jax.ShapeDtypeStruct((ML_NG, lay.rows), F32)] + x_shapes,
        compiler_params=_params(1),
        name="mlstm_in",
    )(*xs, gain, mod, mod, w_main, wg3, _pad_lanes(b_g))


def _split3(x):
    hi = x.astype(BF16)
    r1 = x - hi.astype(F32)
    mid = r1.astype(BF16)
    lo = (r1 - mid.astype(F32)).astype(BF16)
    return hi, mid, lo


def _dot_exact01(x, sel01, x_on_left):
    sel = sel01.astype(BF16)
    parts = [jnp.dot(p, sel, preferred_element_type=F32) if x_on_left
             else jnp.dot(sel, p, preferred_element_type=F32) for p in _split3(x)]
    return parts[0] + parts[1] + parts[2]


def _mlstm_dir(reverse, q_ref, k_ref, kt_ref, v_ref, g_ref, gt_ref, o_ref, s_ref, m_ref):
    T = ML_T
    row = lax.broadcasted_iota(jnp.int32, (T, T), 0)
    col = lax.broadcasted_iota(jnp.int32, (T, T), 1)
    mask = (col >= row) if reverse else (col <= row)
    gt = gt_ref[...]
    bc = _dot_exact01(g_ref[...], mask, x_on_left=False)
    br = _dot_exact01(gt, (row >= col) if reverse else (row <= col), x_on_left=True)
    gi, gf = (2 * ML_HEADS, 3 * ML_HEADS) if reverse else (0, ML_HEADS)
    lane = lax.broadcasted_iota(jnp.int32, (T, 2 * ML_DQK), 1)
    sub = lax.broadcasted_iota(jnp.int32, (2 * ML_DQK, T), 0)
    pick = (lax.broadcasted_iota(jnp.int32, (ML_NG, ML_V), 0)
            == gf + lax.broadcasted_iota(jnp.int32, (ML_NG, ML_V), 1) // ML_DV)
    b_all = _dot_exact01(bc, pick, x_on_left=True)
    ones = jnp.ones((T, ML_DV), BF16)
    heads = range(ML_HEADS)
    qm, vx, s_raw = {}, {}, {}
    for h in heads:
        pair = (h // 2) * 2 * ML_DQK
        own = (lane >= ML_DQK) if (h % 2) else (lane < ML_DQK)
        qp = q_ref[:, pair:pair + 2 * ML_DQK]
        qm[h] = jnp.where(own, qp, jnp.zeros_like(qp))
        vx[h] = jnp.concatenate([v_ref[:, h * ML_DV:(h + 1) * ML_DV], ones], axis=1)
        s_raw[h] = lax.dot_general(qm[h], k_ref[:, pair:pair + 2 * ML_DQK], (((1,), (1,)), ((), ())),
                                   preferred_element_type=F32)
    p, a, u_rep, m_prev, s_prev = {}, {}, {}, {}, {}
    for h in heads:
        c_row = gt[gi + h:gi + h + 1, :] - br[gf + h:gf + h + 1, :]
        e = jnp.where(mask, c_row, -jnp.inf)
        m_prev[h] = m_ref[h][0:1, 0:1]
        u = jnp.maximum(m_prev[h], jnp.max(e, axis=1, keepdims=True))
        p[h] = (s_raw[h] * jnp.exp2(e - u)).astype(BF16)
        u_rep[h] = jnp.broadcast_to(u, (T, ML_DV))
        a[h] = jnp.exp2(m_prev[h] - u_rep[h])
        s_prev[h] = s_ref[h]
    qs_all = {}
    for h in range(0, ML_HEADS, 2):
        pair = (h // 2) * 2 * ML_DQK
        s_pair = jnp.concatenate([s_prev[h], s_prev[h + 1]], axis=1).astype(BF16)
        both = jnp.dot(q_ref[:, pair:pair + 2 * ML_DQK], s_pair, preferred_element_type=F32)
        qs_all[h], qs_all[h + 1] = both[:, :ML_SW], both[:, ML_SW:]
    for h in heads:
        r = jnp.dot(p[h], vx[h], preferred_element_type=F32)
        qs = qs_all[h]
        num = r[:, :ML_DV] + a[h] * qs[:, :ML_DV]
        den = r[:, ML_DV:] + a[h] * qs[:, ML_DV:]
        floor = jnp.exp2(-(b_all[:, h * ML_DV:(h + 1) * ML_DV] + u_rep[h]))
        o_ref[:, h * ML_DV:(h + 1) * ML_DV] = (num / jnp.maximum(jnp.abs(den), floor)).astype(o_ref.dtype)
    for h in heads:
        pair = (h // 2) * 2 * ML_DQK
        b_row = br[gf + h:gf + h + 1, :]
        tot = b_row[:, 0:1] if reverse else b_row[:, T - 1:T]
        g_row = tot - b_row + gt[gi + h:gi + h + 1, :]
        m_new = jnp.maximum(tot + m_prev[h], jnp.max(g_row, axis=1, keepdims=True))
        decay = jnp.exp2(tot + m_prev[h] - m_new)
        wk = jnp.exp2(g_row - m_new)
        own_t = (sub >= ML_DQK) if (h % 2) else (sub < ML_DQK)
        kt = kt_ref[pair:pair + 2 * ML_DQK, :].astype(F32)
        kw = jnp.where(own_t, kt * wk, 0.0).astype(BF16)
        s_ref[h] = decay * s_prev[h] + jnp.dot(kw, vx[h], preferred_element_type=F32)
        m_ref[h] = jnp.broadcast_to(m_new, m_ref.shape[1:])


def _mlstm_scan_kernel(qf_ref, kf_ref, ktf_ref, vf_ref, gf_ref, gtf_ref,
                       qb_ref, kb_ref, ktb_ref, vb_ref, gb_ref, gtb_ref,
                       of_ref, ob_ref, sf_ref, mf_ref, sb_ref, mb_ref):
    @pl.when(pl.program_id(1) == 0)
    def _():
        sf_ref[...] = jnp.zeros_like(sf_ref)
        mf_ref[...] = jnp.zeros_like(mf_ref)
        sb_ref[...] = jnp.zeros_like(sb_ref)
        mb_ref[...] = jnp.zeros_like(mb_ref)

    _mlstm_dir(False, qf_ref, kf_ref, ktf_ref, vf_ref, gf_ref, gtf_ref, of_ref, sf_ref, mf_ref)
    _mlstm_dir(True, qb_ref, kb_ref, ktb_ref, vb_ref, gb_ref, gtb_ref, ob_ref, sb_ref, mb_ref)


def mlstm_scan(lay, q, k, kt, v, g, gt):
    assert lay.ctx_first
    per = TL // ML_T
    nb = lay.nb * per
    rev = lambda j: jnp.where(j < per, per - 1 - j, nb + per - 1 - j)
    fwd = lambda b, j: (b * nb + j, 0)
    bwd = lambda b, j: (b * nb + rev(j), 0)
    fwd_t = lambda b, j: (0, b * nb + j)
    bwd_t = lambda b, j: (0, b * nb + rev(j))

    def specs(im, imt):
        return [pl.BlockSpec((ML_T, ML_QK), im), pl.BlockSpec((ML_T, ML_QK), im),
                pl.BlockSpec((ML_QK, ML_T), imt), pl.BlockSpec((ML_T, ML_V), im),
                pl.BlockSpec((ML_T, ML_NG), im), pl.BlockSpec((ML_NG, ML_T), imt)]

    out_sds = jax.ShapeDtypeStruct((lay.rows, ML_V), BF16)
    state = [pltpu.VMEM((ML_HEADS, 2 * ML_DQK, ML_SW), F32),
             pltpu.VMEM((ML_HEADS, 8, LANES), F32)]
    return pl.pallas_call(
        _mlstm_scan_kernel,
        grid=(lay.n_batch, nb),
        in_specs=specs(fwd, fwd_t) + specs(bwd, bwd_t),
        out_specs=[pl.BlockSpec((ML_T, ML_V), fwd), pl.BlockSpec((ML_T, ML_V), bwd)],
        out_shape=[out_sds, out_sds],
        scratch_shapes=state + state,
        compiler_params=_params(2),
        name="mlstm_scan",
    )(q, k, kt, v, g, gt, q, k, kt, v, g, gt)


MLA_QK_PAD = 256
MLA_VW = 2 * MLA_V
MLA_IN_PAD = MLA_Q_LORA + MLA_KV_LORA + LANES
ROPE_HALF = MLA_ROPE // 4


def _mla_in_kernel(*refs, nb, n_src):
    srcs, rest = refs[:n_src], refs[n_src:]
    (gain_ref, sc_ref, sh_ref, win_ref, wuq_ref, wukv_ref, qn_ref, kvn_ref,
     qnn_ref, qnr_ref, knn_ref, knr_ref, cos_ref, sa_ref, sb_ref) = rest[:15]
    q_out, k_out, v_out = rest[15:18]
    x = _token_block(srcs, pl.program_id(0), nb)
    if n_src == N_COMBINE_SRC:
        rest[18][...] = x
    h = _norm_mod(x, gain_ref[...], sc_ref[0], sh_ref[0]).astype(BF16)
    p = jnp.dot(h, win_ref[...], preferred_element_type=F32)
    cq = _rms(p[:, :MLA_Q_LORA]) * qn_ref[...]
    ckv = _rms(p[:, MLA_Q_LORA:MLA_Q_LORA + MLA_KV_LORA]) * kvn_ref[...]
    kr = p[:, MLA_Q_LORA + MLA_KV_LORA:]
    q = jnp.dot(cq.astype(BF16), wuq_ref[...], preferred_element_type=F32)
    kv = jnp.dot(ckv.astype(BF16), wukv_ref[...], preferred_element_type=F32)
    nv = MLA_HEADS * MLA_NOPE
    ones_col = (lax.broadcasted_iota(jnp.int32, (TL, MLA_VW - MLA_V), 1) == 0).astype(BF16)
    cos, sa, sb = cos_ref[...], sa_ref[...], sb_ref[...]

    def rope(xp):
        return (xp * cos + pltpu.roll(xp, LANES - ROPE_HALF, 1) * sa + pltpu.roll(xp, ROPE_HALF, 1) * sb)

    heads = range(MLA_HEADS)
    kr = _rms(kr, MLA_ROPE) * knr_ref[...]
    qn = {hd: _rms(q[:, hd * MLA_QK_PAD:hd * MLA_QK_PAD + MLA_NOPE]) * qnn_ref[...] for hd in heads}
    qr = {hd: _rms(q[:, hd * MLA_QK_PAD + MLA_NOPE:(hd + 1) * MLA_QK_PAD], MLA_ROPE) * qnr_ref[...]
          for hd in heads}
    kn = {hd: _rms(kv[:, hd * MLA_NOPE:(hd + 1) * MLA_NOPE]) * knn_ref[...] for hd in heads}
    kr = rope(kr).astype(BF16)
    qr = {hd: rope(qr[hd]) for hd in heads}
    for hd in heads:
        c0 = hd * MLA_QK_PAD
        q_out[:, c0:c0 + MLA_NOPE] = qn[hd].astype(BF16)
        q_out[:, c0 + MLA_NOPE:c0 + MLA_QK_PAD] = qr[hd].astype(BF16)
        v0 = nv + hd * MLA_V
        v_out[:, hd * MLA_VW:hd * MLA_VW + MLA_V] = kv[:, v0:v0 + MLA_V].astype(BF16)
        v_out[:, hd * MLA_VW + MLA_V:(hd + 1) * MLA_VW] = ones_col
        k_out[:, c0:c0 + MLA_NOPE] = kn[hd].astype(BF16)
        k_out[:, c0 + MLA_NOPE:c0 + MLA_QK_PAD] = kr


def _pad_lanes(g):
    return jnp.pad(g, (0, LANES - g.shape[0])).reshape(1, LANES)


def mla_in(lay, xs, gain, mod, w_in, q_norm, kv_norm, w_uq, w_ukv, qn_nope, qn_rope, kn_nope, kn_rope,
           tables):
    D = D_MODEL
    Hn = MLA_HEADS
    win = jnp.pad(w_in, ((0, 0), (0, MLA_IN_PAD - w_in.shape[1]))).astype(BF16)
    wuq = jnp.pad(w_uq.reshape(MLA_Q_LORA, Hn, MLA_NOPE + MLA_ROPE),
                  ((0, 0), (0, 0), (0, MLA_QK_PAD - MLA_NOPE - MLA_ROPE)))
    wuq = wuq.reshape(MLA_Q_LORA, Hn * MLA_QK_PAD).astype(BF16)
    wkv = w_ukv.reshape(MLA_KV_LORA, Hn, MLA_NOPE + MLA_V)
    wukv = jnp.concatenate([wkv[:, :, :MLA_NOPE].reshape(MLA_KV_LORA, Hn * MLA_NOPE),
                            wkv[:, :, MLA_NOPE:].reshape(MLA_KV_LORA, Hn * MLA_V)], axis=1).astype(BF16)
    nb = lay.nb
    x_specs, x_shapes = _row_outputs(lay, len(xs))
    tab_spec = pl.BlockSpec((TL, LANES), lambda r: (r % nb, 0))
    bf = lambda w: jax.ShapeDtypeStruct((lay.rows, w), BF16)
    return pl.pallas_call(
        functools.partial(_mla_in_kernel, nb=nb, n_src=len(xs)),
        grid=(lay.n_blocks,),
        in_specs=lay.token_specs(D, len(xs)) + [
                  _const_spec((1, D)), lay.mod_spec(1), lay.mod_spec(0),
                  _const_spec(win.shape), _const_spec(wuq.shape), _const_spec(wukv.shape),
                  _const_spec((1, MLA_Q_LORA)), _const_spec((1, MLA_KV_LORA)),
                  _const_spec((1, LANES)), _const_spec((1, LANES)), _const_spec((1, LANES)),
                  _const_spec((1, LANES)), tab_spec, tab_spec, tab_spec],
        out_specs=[lay.row_spec(Hn * MLA_QK_PAD), lay.row_spec(Hn * MLA_QK_PAD),
                   lay.row_spec(Hn * MLA_VW)] + x_specs,
        out_shape=[bf(Hn * MLA_QK_PAD), bf(Hn * MLA_QK_PAD), bf(Hn * MLA_VW)] + x_shapes,
        compiler_params=_params(1),
        name="mla_in",
    )(*xs, gain, mod, mod, win, wuq, wukv, q_norm.reshape(1, -1), kv_norm.reshape(1, -1),
      (qn_nope * (MLA_SCALE * LOG2E)).reshape(1, -1), _pad_lanes(qn_rope * (MLA_SCALE * LOG2E)),
      kn_nope.reshape(1, -1), _pad_lanes(kn_rope), *tables)


def rope_tables(n_ctx, n_lat):
    n_freq = MLA_ROPE // 4
    inv = ROPE_THETA ** (-jnp.arange(n_freq, dtype=F32) / n_freq)
    t = jnp.arange(n_lat)
    a_r = (t // GRID_W).astype(F32)[:, None] * inv
    a_c = (t % GRID_W).astype(F32)[:, None] * inv
    ang = jnp.concatenate([a_r, a_r, a_c, a_c], axis=-1)
    ang = jnp.concatenate([jnp.zeros((n_ctx, MLA_ROPE), F32), ang], axis=0)
    cos, sin = jnp.cos(ang), jnp.sin(ang)
    low = (jnp.arange(MLA_ROPE) % (2 * ROPE_HALF)) < ROPE_HALF
    pad = lambda a: jnp.pad(a, ((0, 0), (0, LANES - MLA_ROPE)))
    return pad(cos), pad(jnp.where(low, -sin, 0.0)), pad(jnp.where(low, 0.0, sin))


MLA_HPS = 4


def _mla_attn_kernel(q_ref, k_ref, v_ref, o_ref, *, n_ctx):
    def attend(n_keys):
        heads = range(MLA_HPS)
        s, p = {}, {}
        for hd in heads:
            q = q_ref[0, :, hd * MLA_QK_PAD:(hd + 1) * MLA_QK_PAD]
            k = k_ref[0, :n_keys, hd * MLA_QK_PAD:(hd + 1) * MLA_QK_PAD]
            s[hd] = lax.dot_general(q, k, (((1,), (1,)), ((), ())), preferred_element_type=F32)
        for hd in heads:
            p[hd] = jnp.exp2(s[hd] - s[hd].max(axis=1, keepdims=True)).astype(BF16)
        for hd in heads:
            v = v_ref[0, :n_keys, hd * MLA_VW:(hd + 1) * MLA_VW]
            r = jnp.dot(p[hd], v, preferred_element_type=F32)
            o_ref[0, :, hd * MLA_V:(hd + 1) * MLA_V] = (
                r[:, :MLA_V] / r[:, MLA_V:MLA_V + 1]).astype(o_ref.dtype)

    @pl.when(pl.program_id(2) == 0)
    def _():
        attend(n_ctx)

    @pl.when(pl.program_id(2) > 0)
    def _():
        attend(k_ref.shape[1])


def mla_attention(lay, q, k, v):
    assert lay.ctx_first
    Bn, S = lay.n_batch, lay.nb * TL
    q3, k3, v3 = (a.reshape(Bn, S, a.shape[-1]) for a in (q, k, v))
    out = pl.pallas_call(
        functools.partial(_mla_attn_kernel, n_ctx=TL),
        grid=(Bn, MLA_HEADS // MLA_HPS, lay.nb),
        in_specs=[pl.BlockSpec((1, TL, MLA_HPS * MLA_QK_PAD), lambda b, h, i: (b, i, h)),
                  pl.BlockSpec((1, S, MLA_HPS * MLA_QK_PAD), lambda b, h, i: (b, 0, h),
                               pipeline_mode=pl.Buffered(1)),
                  pl.BlockSpec((1, S, MLA_HPS * MLA_VW), lambda b, h, i: (b, 0, h),
                               pipeline_mode=pl.Buffered(1))],
        out_specs=pl.BlockSpec((1, TL, MLA_HPS * MLA_V), lambda b, h, i: (b, i, h)),
        out_shape=jax.ShapeDtypeStruct((Bn, S, MLA_HEADS * MLA_V), BF16),
        compiler_params=_params(3),
        name="mla_attention",
    )(q3, k3, v3)
    return out.reshape(lay.rows, MLA_HEADS * MLA_V)


N_PROLOGUE = {"conv": 5, "mlstm": 4, "mla": 1}


def _mixer_out_kernel(*refs, kind, nb, ctx_first, n_src):
    n_pro = N_PROLOGUE[kind]
    pro = refs[:n_pro]
    wout_ref = refs[n_pro]
    x_srcs = refs[n_pro + 1:n_pro + 1 + n_src]
    (ga_ref, gain_ref, sc_ref, sh_ref, wr_ref, br_ref,
     xo_ref, h2_ref, te_ref, gate_ref, rank_ref, cnt_ref, carry_ref) = refs[n_pro + 1 + n_src:]
    r = pl.program_id(0)

    if kind == "conv":
        vprev_ref, v_ref, vnext_ref, bg_ref, cw_ref = pro
        j = r % nb
        first = (j == 0) | (j == 1) if ctx_first else (j == 0)
        last = (j == nb - 1) | (j == 0) if ctx_first else (j == nb - 1)
        v = v_ref[...].astype(F32)
        rows = lax.broadcasted_iota(jnp.int32, (TL, 1), 0)
        prev_row = jnp.where(first, 0.0, vprev_ref[BF16_ROWS - 1:BF16_ROWS, :].astype(F32))
        next_row = jnp.where(last, 0.0, vnext_ref[0:1, :].astype(F32))
        up = jnp.where(rows == 0, prev_row, pltpu.roll(v, 1, 0))
        dn = jnp.where(rows == TL - 1, next_row, pltpu.roll(v, TL - 1, 0))
        cw = cw_ref[...]
        a = bg_ref[...].astype(F32) * (up * cw[0:1] + v * cw[1:2] + dn * cw[2:3])
    elif kind == "mlstm":
        hf_ref, hb_ref, og_ref, ng_ref = pro
        hh = hf_ref[...].astype(F32) + hb_ref[...].astype(F32)
        a = jnp.concatenate([_rms(hh[:, h * ML_DV:(h + 1) * ML_DV]) for h in range(ML_HEADS)], axis=1)
        a = a * ng_ref[...] * og_ref[...].astype(F32)
    else:
        a = pro[0][...]

    y = jnp.dot(a.astype(BF16), wout_ref[...], preferred_element_type=F32)
    xn = _token_block(x_srcs, r, nb) + ga_ref[0] * y
    xo_ref[...] = xn
    h2 = _norm_mod(xn, gain_ref[...], sc_ref[0], sh_ref[0])
    h2_ref[...] = _pack_rows(h2)
    lt = jnp.dot(h2.astype(BF16), wr_ref[...], preferred_element_type=F32).T
    logits = lt[:N_EXPERTS] + lt[N_EXPERTS:2 * N_EXPERTS] + br_ref[...]

    sub = lax.broadcasted_iota(jnp.int32, (N_EXPERTS, TL), 0)
    sub_k = lax.broadcasted_iota(jnp.int32, (TOP_K, TL), 0)
    work = logits
    sel = jnp.zeros((N_EXPERTS, TL), F32)
    top_e = jnp.zeros((TOP_K, TL), jnp.int32)
    top_v = jnp.zeros((TOP_K, TL), F32)
    picks = []
    for kk in range(TOP_K):
        m = work.max(axis=0, keepdims=True)
        idx = jnp.min(jnp.where(work == m, sub, N_EXPERTS), axis=0, keepdims=True)
        hit = sub == idx
        picks.append(hit)
        sel = jnp.where(hit, 1.0, sel)
        work = jnp.where(hit, -jnp.inf, work)
        top_e = jnp.where(sub_k == kk, idx, top_e)
        top_v = jnp.where(sub_k == kk, m, top_v)
    ex = jnp.exp(top_v - top_v[0:1])
    gate_ref[...] = ex / ex.sum(axis=0, keepdims=True)
    te_ref[...] = top_e

    @pl.when(r == 0)
    def _():
        carry_ref[...] = jnp.zeros_like(carry_ref)

    tr = lax.broadcasted_iota(jnp.int32, (TL, TL), 0)
    tc = lax.broadcasted_iota(jnp.int32, (TL, TL), 1)
    before = jnp.dot(sel.astype(BF16), (tr < tc).astype(BF16), preferred_element_type=F32)
    pos = before + carry_ref[...]
    rank = jnp.zeros((TOP_K, TL), F32)
    for kk in range(TOP_K):
        rk = jnp.sum(jnp.where(picks[kk], pos, 0.0), axis=0, keepdims=True)
        rank = jnp.where(sub_k == kk, rk, rank)
    rank_ref[...] = rank.astype(jnp.int32)
    total = carry_ref[...] + jnp.sum(sel, axis=1, keepdims=True)
    carry_ref[...] = total
    cnt_ref[...] = total


def mixer_out(lay, kind, pro_args, w_out, xs, mod, gain_f, w_r, b_r):
    D = D_MODEL
    nb = lay.nb
    if kind == "conv":
        v, bg, cw = pro_args
        per = TL // BF16_ROWS
        last_tile = lay.rows // BF16_ROWS - 1
        pro_specs = [pl.BlockSpec((BF16_ROWS, D), lambda r: (jnp.maximum(r * per - 1, 0), 0)),
                     lay.row_spec(D),
                     pl.BlockSpec((BF16_ROWS, D), lambda r: (jnp.minimum((r + 1) * per, last_tile), 0)),
                     lay.row_spec(D), _const_spec((CONV_WIDTH, D))]
        pro_in = [v, v, v, bg, cw]
    elif kind == "mlstm":
        h_f, h_b, og, ng = pro_args
        pro_specs = [lay.row_spec(ML_V), lay.row_spec(ML_V), lay.row_spec(ML_V), _const_spec((1, ML_V))]
        pro_in = [h_f, h_b, og, ng.reshape(1, ML_V)]
    else:
        pro_specs = [lay.row_spec(D)]
        pro_in = list(pro_args)
    k_in = w_out.shape[0]
    small = lambda dt: jax.ShapeDtypeStruct((TOP_K, lay.rows), dt)
    small_spec = pl.BlockSpec((TOP_K, TL), lambda r: (0, r))
    return pl.pallas_call(
        functools.partial(_mixer_out_kernel, kind=kind, nb=nb, ctx_first=lay.ctx_first, n_src=len(xs)),
        grid=(lay.n_blocks,),
        in_specs=pro_specs + [_const_spec((k_in, D))] + lay.token_specs(D, len(xs)) + [
            lay.mod_spec(2), _const_spec((1, D)), lay.mod_spec(4), lay.mod_spec(3),
            _const_spec((D, LANES)), _const_spec((N_EXPERTS, 1))],
        out_specs=[lay.row_spec(D), lay.row_spec(PACK_W), small_spec, small_spec, small_spec,
                   _const_spec((N_EXPERTS, 1))],
        out_shape=[jax.ShapeDtypeStruct((lay.rows, D), F32),
                   jax.ShapeDtypeStruct((lay.rows, PACK_W), jnp.int32),
                   small(jnp.int32), small(F32), small(jnp.int32),
                   jax.ShapeDtypeStruct((N_EXPERTS, 1), F32)],
        scratch_shapes=[pltpu.VMEM((N_EXPERTS, 1), F32)],
        compiler_params=_params(1),
        name="mixer_out_" + kind,
    )(*pro_in, w_out.astype(BF16), *xs, mod, gain_f, mod, mod, _split_weight_lanes(w_r),
      b_r.reshape(N_EXPERTS, 1))


def _expert_ffn_kernel(blk_e_ref, first_ref, valid_ref, x_ref, w1_ref, b1_ref, w2_ref, b2_ref, o_ref,
                       w1b_ref, w2b_ref):
    del blk_e_ref
    i = pl.program_id(0)

    @pl.when(first_ref[i] == 1)
    def _():
        w1b_ref[...] = w1_ref[0, 0].astype(BF16)
        w2b_ref[...] = w2_ref[0, 0].astype(BF16)

    @pl.when(valid_ref[i] > 0)
    def _():
        x = _unpack_rows(x_ref[...], BF16)
        h = jnp.dot(x, w1b_ref[...], preferred_element_type=F32) + b1_ref[0, 0]
        glu = jnp.minimum(h[:, :MOE_FF], SWIGLU_LIMIT)
        lin = jnp.clip(h[:, MOE_FF:], -SWIGLU_LIMIT, SWIGLU_LIMIT)
        act = glu * jax.nn.sigmoid(SWIGLU_ALPHA * glu) * (lin + 1.0)
        y = jnp.dot(act.astype(BF16), w2b_ref[...], preferred_element_type=F32)
        o_ref[...] = _pack_rows(y + b2_ref[0, 0])

    @pl.when(valid_ref[i] == 0)
    def _():
        o_ref[...] = jnp.zeros_like(o_ref)


def expert_ffn(layer, xp, blk_e, blk_first, blk_valid, w1, b1, w2, b2):
    n_rows = xp.shape[0]
    D, F2 = D_MODEL, 2 * MOE_FF
    n_blk = n_rows // MOE_BLOCK
    grid_spec = pltpu.PrefetchScalarGridSpec(
        num_scalar_prefetch=3,
        grid=(n_blk,),
        in_specs=[
            pl.BlockSpec((MOE_BLOCK, PACK_W), lambda i, be, fi, nu: (i, 0)),
            pl.BlockSpec((1, 1, D, F2), lambda i, be, fi, nu: (layer, be[i], 0, 0)),
            pl.BlockSpec((1, 1, 1, F2), lambda i, be, fi, nu: (layer, be[i], 0, 0)),
            pl.BlockSpec((1, 1, MOE_FF, D), lambda i, be, fi, nu: (layer, be[i], 0, 0)),
            pl.BlockSpec((1, 1, 1, D), lambda i, be, fi, nu: (layer, be[i], 0, 0)),
        ],
        out_specs=pl.BlockSpec((MOE_BLOCK, PACK_W), lambda i, be, fi, nu: (i, 0)),
        scratch_shapes=[pltpu.VMEM((D, F2), BF16), pltpu.VMEM((MOE_FF, D), BF16)],
    )
    return pl.pallas_call(
        _expert_ffn_kernel,
        grid_spec=grid_spec,
        out_shape=jax.ShapeDtypeStruct((n_rows, PACK_W), jnp.int32),
        compiler_params=_params(1),
        name="expert_ffn",
    )(blk_e, blk_first, blk_valid, xp, w1, b1, w2, b2)


SC_CORES = 2
SC_SUBCORES = 16
SC_CHUNKS = (64, 32)


def _sc_chunk(*counts):
    n_workers = SC_CORES * SC_SUBCORES
    for chunk in SC_CHUNKS:
        if all(n % (chunk * n_workers) == 0 for n in counts):
            return chunk
    raise ValueError(f"row counts {counts} do not split over {n_workers} subcores")


def sc_gather(table, idx):
    n_idx = idx.shape[0]
    width = table.shape[1]
    n_workers = SC_CORES * SC_SUBCORES
    per_worker = n_idx // n_workers
    chunk = _sc_chunk(n_idx)
    n_chunks = per_worker // chunk
    assert n_chunks * chunk * n_workers == n_idx and n_chunks % 2 == 0
    mesh = plsc.VectorSubcoreMesh(core_axis_name="c", subcore_axis_name="s",
                                  num_cores=SC_CORES, num_subcores=SC_SUBCORES)

    def body(table_hbm, idx_hbm, out_hbm, idx_v, rows_v, gsem, wsem):
        wid = lax.axis_index("s") * SC_CORES + lax.axis_index("c")
        pltpu.sync_copy(idx_hbm.at[wid], idx_v)

        def gather(ci, slot):
            return pltpu.make_async_copy(table_hbm.at[idx_v.at[ci]], rows_v.at[slot], gsem.at[slot])

        def write(ci, slot):
            return pltpu.make_async_copy(rows_v.at[slot], out_hbm.at[ci, wid], wsem.at[slot])

        gather(0, 0).start()

        @pl.loop(0, n_chunks, step=2)
        def _(c0):
            for slot in range(2):
                ci = c0 + slot
                other = 1 - slot

                @pl.when(ci + 1 < n_chunks)
                def _():
                    @pl.when(ci >= 1)
                    def _():
                        write(ci - 1, other).wait()
                    gather(ci + 1, other).start()

                gather(ci, slot).wait()
                write(ci, slot).start()

        write(n_chunks - 2, 0).wait()
        write(n_chunks - 1, 1).wait()

    out = pl.kernel(
        body,
        out_type=jax.ShapeDtypeStruct((n_chunks, n_workers, chunk, width), table.dtype),
        mesh=mesh,
        scratch_types=[pltpu.VMEM((n_chunks, chunk), jnp.int32),
                       pltpu.VMEM((2, chunk, width), table.dtype),
                       pltpu.SemaphoreType.DMA((2,)),
                       pltpu.SemaphoreType.DMA((2,))],
        name="sc_gather",
    )(table, idx.reshape(n_chunks, n_workers, chunk).transpose(1, 0, 2))
    return out.reshape(n_idx, width)


def sc_dispatch(table, dest, pad_rows):
    n_tok, width = table.shape
    n_picks = dest.shape[0]
    n_pad = pad_rows.shape[0]
    n_workers = SC_CORES * SC_SUBCORES
    chunk = _sc_chunk(n_tok, n_pad)
    per_w = n_tok // chunk // n_workers
    pad_w = n_pad // chunk // n_workers
    assert per_w * chunk * n_workers == n_tok and pad_w * chunk * n_workers == n_pad
    mesh = plsc.VectorSubcoreMesh(core_axis_name="c", subcore_axis_name="s",
                                  num_cores=SC_CORES, num_subcores=SC_SUBCORES)

    def body(table_hbm, idx_hbm, pad_hbm, zero_hbm, out_hbm, idx_v, pad_v, rows_v, zero_v,
             rsem, ssem, zsem):
        wid = lax.axis_index("s") * SC_CORES + lax.axis_index("c")
        pltpu.sync_copy(idx_hbm.at[wid], idx_v)
        pltpu.sync_copy(pad_hbm.at[wid], pad_v)
        pltpu.sync_copy(zero_hbm, zero_v)

        def zero_fill(pc):
            return pltpu.make_async_copy(zero_v, out_hbm.at[pad_v.at[pc]], zsem)

        for pc in range(pad_w):
            zero_fill(pc).start()

        def scatter(ci, kk):
            return pltpu.make_async_copy(rows_v, out_hbm.at[idx_v.at[ci * n_picks + kk]], ssem)

        @pl.loop(0, per_w)
        def _(ci):
            pltpu.async_copy(table_hbm.at[ci, wid], rows_v, rsem).wait()
            for kk in range(n_picks):
                scatter(ci, kk).start()
            for kk in range(n_picks):
                scatter(ci, kk).wait()

        for pc in range(pad_w):
            zero_fill(pc).wait()

    idx = dest.reshape(n_picks, per_w, n_workers, chunk).transpose(2, 1, 0, 3)
    idx = idx.reshape(n_workers, per_w * n_picks, chunk)
    return pl.kernel(
        body,
        out_type=jax.ShapeDtypeStruct((n_tok * n_picks + n_pad, width), table.dtype),
        mesh=mesh,
        scratch_types=[pltpu.VMEM((per_w * n_picks, chunk), jnp.int32),
                       pltpu.VMEM((pad_w, chunk), jnp.int32),
                       pltpu.VMEM((chunk, width), table.dtype),
                       pltpu.VMEM((chunk, width), table.dtype),
                       pltpu.SemaphoreType.DMA, pltpu.SemaphoreType.DMA, pltpu.SemaphoreType.DMA],
        name="sc_dispatch",
    )(table.reshape(per_w, n_workers, chunk, width), idx,
      pad_rows.reshape(n_workers, pad_w, chunk), jnp.zeros((chunk, width), table.dtype))


def _combine_kernel(x_ref, *refs):
    y_refs, (gate_ref, gf_ref, o_ref) = refs[:TOP_K], refs[TOP_K:]
    gates = gate_ref[...]
    acc = gates[:, 0:1] * _unpack_rows(y_refs[0][...], F32)
    for kk in range(1, TOP_K):
        acc = acc + gates[:, kk:kk + 1] * _unpack_rows(y_refs[kk][...], F32)
    o_ref[...] = x_ref[...] + gf_ref[0] * acc


def moe_combine(lay, x, yg, gates, mod, drop_ctx):
    D = D_MODEL
    if drop_ctx:
        nbo = lay.nb - 1
        src = lambda r: (r // nbo) * lay.nb + 1 + r % nbo
        n_out = lay.n_batch * nbo
    else:
        src = lambda r: r
        n_out = lay.n_blocks
    y_specs = [pl.BlockSpec((TL, PACK_W), functools.partial(lambda kk, r: (kk * lay.n_blocks + src(r), 0), kk))
               for kk in range(TOP_K)]
    return pl.pallas_call(
        _combine_kernel,
        grid=(n_out,),
        in_specs=[pl.BlockSpec((TL, D), lambda r: (src(r), 0))] + y_specs + [
            pl.BlockSpec((TL, TOP_K), lambda r: (src(r), 0)),
            pl.BlockSpec((1, 1, D), lambda r: (lay.mod_row(src(r)), 0, 5))],
        out_specs=pl.BlockSpec((TL, D), lambda r: (r, 0)),
        out_shape=jax.ShapeDtypeStruct((n_out * TL, D), F32),
        compiler_params=_params(1),
        name="moe_combine",
    )(x, yg, yg, yg, yg, gates, mod)


def moe_route(top_e, rank, counts):
    T = top_e.shape[1]
    assert (T * TOP_K) % MOE_BLOCK == 0
    counts = counts.reshape(N_EXPERTS).astype(jnp.int32)
    padded = (counts + MOE_BLOCK - 1) // MOE_BLOCK * MOE_BLOCK
    padded_end = jnp.cumsum(padded)
    padded_start = padded_end - padded
    experts = jnp.arange(N_EXPERTS)
    start_of = jnp.sum(jnp.where(top_e[..., None] == experts, padded_start, 0), axis=-1)
    dest = (start_of + rank).astype(jnp.int32)
    n_pad = N_EXPERTS * MOE_BLOCK
    n_rows = T * TOP_K + n_pad
    n_blk = n_rows // MOE_BLOCK
    blk_start = jnp.arange(n_blk) * MOE_BLOCK
    blk_e = jnp.minimum(jnp.sum(padded_end[None, :] <= blk_start[:, None], axis=1), N_EXPERTS - 1)
    blk_e = blk_e.astype(jnp.int32)
    blk_first = jnp.concatenate([jnp.ones((1,), jnp.int32), (blk_e[1:] != blk_e[:-1]).astype(jnp.int32)])
    blk_hot = blk_e[:, None] == experts
    in_grp = blk_start - jnp.sum(jnp.where(blk_hot, padded_start, 0), axis=-1)
    blk_valid = jnp.clip(jnp.sum(jnp.where(blk_hot, counts, 0), axis=-1) - in_grp, 0, MOE_BLOCK)
    blk_valid = jnp.where(blk_start < padded_end[-1], blk_valid, 0).astype(jnp.int32)
    tail = padded - counts
    tail_end = jnp.cumsum(tail)
    j = jnp.arange(n_pad)
    owner = jnp.sum(tail_end[None, :] <= j[:, None], axis=1)
    base = padded_start + counts - (tail_end - tail)
    in_group = jnp.sum(jnp.where(owner[:, None] == experts, base, 0), axis=-1) + j
    pad_rows = jnp.where(j < tail_end[-1], in_group, padded_end[-1] + j - tail_end[-1])
    return dest, pad_rows.astype(jnp.int32), blk_e, blk_first, blk_valid


def kernel(x, c, ctx, c_ctx, norm_mix, norm_ffn, w_mod, b_mod, conv_w_in, conv_w, conv_w_out, ml_w_in, ml_b_gate, ml_norm, ml_w_out, mla_w_in, mla_q_norm, mla_kv_norm, mla_w_uq, mla_w_ukv, mla_qn_nope, mla_qn_rope, mla_kn_nope, mla_kn_rope, mla_w_out, moe_w_router, moe_b_router, moe_w1, moe_b1, moe_w2, moe_b2):
    Bn, n_lat, D = x.shape
    n_ctx = ctx.shape[1]
    assert D == D_MODEL and n_ctx == TL and n_lat % TL == 0
    assert (DEPTH - 1) % N_MIXERS == 0
    full = Layout(Bn, (n_ctx + n_lat) // TL, True)
    lat_only = Layout(Bn, n_lat // TL, False)
    mods = ada_all(c, c_ctx, w_mod, b_mod)
    tables = rope_tables(n_ctx, n_lat)
    b1_all = moe_b1.reshape(DEPTH, N_EXPERTS, 1, 2 * MOE_FF)
    b2_all = moe_b2.reshape(DEPTH, N_EXPERTS, 1, D)
    Xs = (ctx.reshape(Bn * n_ctx, D), x.reshape(Bn * n_lat, D))
    for layer in range(DEPTH):
        kind, j = layer % N_MIXERS, layer // N_MIXERS
        lay = lat_only if layer == DEPTH - 1 else full
        mod = mods[layer]
        gain_a = norm_mix[layer].reshape(1, D)
        gain_f = norm_ffn[layer].reshape(1, D)
        if kind == 0:
            outs = conv_in(lay, Xs, gain_a, mod, conv_w_in[j].astype(BF16), src=full)
            bg, v = outs[:2]
            Xs = tuple(outs[2:]) or Xs
            pro, w_out, name = (v, bg, conv_w[j]), conv_w_out[j], "conv"
        elif kind == 1:
            w = ml_w_in[j]
            n_main = 2 * ML_QK + 2 * ML_V
            w_main = jnp.concatenate([w[:, :ML_QK] * ML_DQK ** -0.5, w[:, ML_QK:n_main]], axis=1)
            outs = mlstm_in(lay, Xs, gain_a, mod, w_main.astype(BF16), w[:, n_main:], ml_b_gate[j])
            q, k, kt, v, og, g, gt = outs[:7]
            Xs = tuple(outs[7:]) or Xs
            h_f, h_b = mlstm_scan(lay, q, k, kt, v, g, gt)
            pro, w_out, name = (h_f, h_b, og, ml_norm[j]), ml_w_out[j], "mlstm"
        else:
            outs = mla_in(lay, Xs, gain_a, mod, mla_w_in[j], mla_q_norm[j], mla_kv_norm[j],
                          mla_w_uq[j], mla_w_ukv[j], mla_qn_nope[j], mla_qn_rope[j],
                          mla_kn_nope[j], mla_kn_rope[j], tables)
            q, k, v = outs[:3]
            Xs = tuple(outs[3:]) or Xs
            pro, w_out, name = (mla_attention(lay, q, k, v),), mla_w_out[j], "mla"
        X, h2, top_e, gates, rank, counts = mixer_out(
            lay, name, pro, w_out, Xs, mod, gain_f, moe_w_router[layer], moe_b_router[layer])
        dest, pad_rows, blk_e, blk_first, blk_valid = moe_route(top_e, rank, counts)
        xp = sc_dispatch(h2, dest, pad_rows)
        yp = expert_ffn(layer, xp, blk_e, blk_first, blk_valid, moe_w1, b1_all, moe_w2, b2_all)
        yg = sc_gather(yp, dest.reshape(-1))
        if layer + 1 < DEPTH:
            Xs = (X,) + (yg,) * TOP_K + (gates.T, mod)
        else:
            Xs = (moe_combine(lay, X, yg, gates.T, mod, drop_ctx=False),)
    return Xs[0].reshape(Bn, n_lat, D)
```
